```python
import jax, jax.numpy as jnp
from jax import lax
import numpy as np

D_MODEL = 1024
BATCH = 8
SEQ = 8192
DEPTH = 4

CHUNK = 64
N_MIXERS = 2
GLA_HEADS = 4
GLA_DK = D_MODEL // 2
GLA_DV = D_MODEL
GLA_DK_HEAD = GLA_DK // GLA_HEADS
GLA_DV_HEAD = GLA_DV // GLA_HEADS
GLA_GATE_RANK = 16
GLA_GATE_TAU = 16.0
GLA_IN_WIDTH = 2 * GLA_DK + 2 * GLA_DV + GLA_GATE_RANK
ATT_HEADS = 16
ATT_HEAD_DIM = D_MODEL // ATT_HEADS
LEFT_CHUNKS = 8
BAND = (LEFT_CHUNKS + 1) * CHUNK
MAX_REL = 128
N_REL = 2 * MAX_REL + 1
D_FF = 4 * D_MODEL
DEEPNORM_ALPHA = (2.0 * DEPTH) ** 0.25
DEEPNORM_BETA = (8.0 * DEPTH) ** -0.25
LN_EPS = 1e-5
RMS_EPS = 1e-6
NEG_INF = -1e30
N_GLA_LAYERS = (DEPTH + 1) // 2
N_ATT_LAYERS = DEPTH // 2

kernel_name = "hybrid_gla_chunkattn_deepnorm_adaln"


def layer_norm(x, g, b):
    xf = x.astype(jnp.float32)
    mu = jnp.mean(xf, -1, keepdims=True)
    var = jnp.mean(jnp.square(xf - mu), -1, keepdims=True)
    return ((xf - mu) * lax.rsqrt(var + LN_EPS)).astype(x.dtype) * g + b


def gla_mixer(u, w_in, w_gk2, b_gk, g_norm, w_out):
    B_, S_, _ = u.shape
    nC = S_ // CHUNK
    proj = u @ w_in
    q, k, v, g, gk_lr = jnp.split(
        proj, [GLA_DK, 2 * GLA_DK, 2 * GLA_DK + GLA_DV, 2 * GLA_DK + 2 * GLA_DV], axis=-1)
    log_a = jax.nn.log_sigmoid((gk_lr @ w_gk2 + b_gk).astype(jnp.float32)) / GLA_GATE_TAU

    def to_chunks(t, hd):
        return t.reshape(B_, nC, CHUNK, GLA_HEADS, hd).transpose(0, 3, 1, 2, 4)

    qc = to_chunks(q.astype(jnp.float32), GLA_DK_HEAD) * (GLA_DK_HEAD ** -0.5)
    kc = to_chunks(k.astype(jnp.float32), GLA_DK_HEAD)
    vc = to_chunks(v.astype(jnp.float32), GLA_DV_HEAD)
    cum = jnp.cumsum(to_chunks(log_a, GLA_DK_HEAD), axis=3)
    e_pos = jnp.exp(cum)
    e_neg = jnp.exp(-cum)
    q_fwd = qc * e_pos
    a_fwd = jnp.einsum('bhntd,bhnsd->bhnts', q_fwd, kc * e_neg)
    a_bwd = jnp.einsum('bhntd,bhnsd->bhnts', qc * e_neg, kc * e_pos)
    lower = jnp.tril(jnp.ones((CHUNK, CHUNK), dtype=bool))
    o_intra = jnp.einsum('bhnts,bhnsv->bhntv', jnp.where(lower, a_fwd, a_bwd), vc)
    k_to_end = kc * jnp.exp(cum[:, :, :, -1:, :] - cum)
    chunk_decay = jnp.exp(cum[:, :, :, -1, :])

    def step(state, inp):
        qs, ke, vv, dec = inp
        o = jnp.einsum('bhtk,bhkv->bhtv', qs, state)
        state = state * dec[..., None] + jnp.einsum('bhtk,bhtv->bhkv', ke, vv)
        return state, o

    s0 = jnp.zeros((B_, GLA_HEADS, GLA_DK_HEAD, GLA_DV_HEAD), jnp.float32)
    _, o_inter = lax.scan(step, s0, (jnp.moveaxis(q_fwd, 2, 0), jnp.moveaxis(k_to_end, 2, 0),
                                     jnp.moveaxis(vc, 2, 0), jnp.moveaxis(chunk_decay, 2, 0)))
    o = o_intra + jnp.moveaxis(o_inter, 0, 2)
    o = o * lax.rsqrt(jnp.mean(jnp.square(o), -1, keepdims=True) + RMS_EPS)
    o = o * g_norm.astype(jnp.float32)[None, :, None, None, :]
    o = o.transpose(0, 2, 3, 1, 4).reshape(B_, S_, GLA_DV).astype(u.dtype)
    return (o * jax.nn.silu(g)) @ w_out


def chunk_attention(u, w_in, b_in, rel_bias, w_out):
    B_, S_, _ = u.shape
    nC = S_ // CHUNK
    q, k, v = jnp.split(u @ w_in + b_in, 3, axis=-1)

    def heads(t):
        return t.reshape(B_, S_, ATT_HEADS, ATT_HEAD_DIM).transpose(0, 2, 1, 3)

    pad = LEFT_CHUNKS * CHUNK
    qc = (heads(q) * (ATT_HEAD_DIM ** -0.5)).reshape(B_, ATT_HEADS, nC, CHUNK, ATT_HEAD_DIM)
    kp = jnp.pad(heads(k), ((0, 0), (0, 0), (pad, 0), (0, 0)))
    vp = jnp.pad(heads(v), ((0, 0), (0, 0), (pad, 0), (0, 0)))
    key_valid = jnp.arange(S_ + pad) >= pad
    rel = jnp.clip(pad + jnp.arange(CHUNK)[:, None] - jnp.arange(BAND)[None, :], -MAX_REL, MAX_REL) + MAX_REL
    bias = rel_bias.astype(jnp.float32)[:, rel]

    def one_chunk(inp):
        qi, start = inp
        kb = lax.dynamic_slice_in_dim(kp, start, BAND, axis=2)
        vb = lax.dynamic_slice_in_dim(vp, start, BAND, axis=2)
        mb = lax.dynamic_slice_in_dim(key_valid, start, BAND)
        s = jnp.einsum('bhtd,bhjd->bhtj', qi, kb).astype(jnp.float32) + bias
        p = jax.nn.softmax(jnp.where(mb, s, NEG_INF), axis=-1).astype(vb.dtype)
        return jnp.einsum('bhtj,bhjd->bhtd', p, vb)

    o = lax.map(one_chunk, (jnp.moveaxis(qc, 2, 0), jnp.arange(nC) * CHUNK))
    o = o.transpose(1, 0, 3, 2, 4).reshape(B_, S_, D_MODEL)
    return o @ w_out


def squared_relu_mlp(u, w1, w2):
    return jnp.square(jax.nn.relu(u @ w1)) @ w2


def _fwd_setup_inputs(seed: int = 0) -> dict:
    key = jax.random.key(seed)
    ks = jax.random.split(key, 16)
    nrm = lambda k, shape, s: jax.random.normal(k, shape, jnp.float32) * s
    D = D_MODEL
    return {
        "x": nrm(ks[0], (BATCH, SEQ, D), 1.0),
        "c": nrm(ks[1], (BATCH, D), 1.0),
        "w_ada": nrm(ks[2], (DEPTH, D, 6 * D), 0.1 * D ** -0.5),
        "b_ada": nrm(ks[3], (DEPTH, 6 * D), 0.02),
        "ln_g": 1.0 + nrm(ks[4], (DEPTH, 2, D), 0.02),
        "ln_b": nrm(ks[5], (DEPTH, 2, D), 0.02),
        "gla_w_in": nrm(ks[6], (N_GLA_LAYERS, D, GLA_IN_WIDTH), D ** -0.5),
        "gla_w_gk2": nrm(ks[7], (N_GLA_LAYERS, GLA_GATE_RANK, GLA_DK), GLA_GATE_RANK ** -0.5),
        "gla_b_gk": nrm(ks[8], (N_GLA_LAYERS, GLA_DK), 0.1),
        "gla_g_norm": 1.0 + nrm(ks[9], (N_GLA_LAYERS, GLA_HEADS, GLA_DV_HEAD), 0.02),
        "gla_w_out": nrm(ks[10], (N_GLA_LAYERS, GLA_DV, D), DEEPNORM_BETA * GLA_DV ** -0.5),
        "att_w_in": nrm(ks[11], (N_ATT_LAYERS, D, 3 * D), D ** -0.5),
        "att_b_in": nrm(ks[12], (N_ATT_LAYERS, 3 * D), 0.02),
        "att_rel_bias": nrm(ks[13], (N_ATT_LAYERS, ATT_HEADS, N_REL), 0.2),
        "att_w_out": nrm(ks[14], (N_ATT_LAYERS, D, D), DEEPNORM_BETA * D ** -0.5),
        "ff_w1": nrm(jax.random.fold_in(ks[15], 0), (DEPTH, D, D_FF), D ** -0.5),
        "ff_w2": nrm(jax.random.fold_in(ks[15], 1), (DEPTH, D_FF, D), DEEPNORM_BETA * D_FF ** -0.5),
    }


def _fwd_reference(x, c, w_ada, b_ada, ln_g, ln_b, gla_w_in, gla_w_gk2, gla_b_gk, gla_g_norm, gla_w_out,
              att_w_in, att_b_in, att_rel_bias, att_w_out, ff_w1, ff_w2):
    c_act = jax.nn.silu(c)
    for i in range(DEPTH):
        mods = jnp.split(c_act @ w_ada[i] + b_ada[i], 6, axis=-1)
        sh1, sc1, g1, sh2, sc2, g2 = [m[:, None, :] for m in mods]
        u = x * (1.0 + sc1) + sh1
        j = i // N_MIXERS
        if i % N_MIXERS == 0:
            y = gla_mixer(u, gla_w_in[j], gla_w_gk2[j], gla_b_gk[j], gla_g_norm[j], gla_w_out[j])
        else:
            y = chunk_attention(u, att_w_in[j], att_b_in[j], att_rel_bias[j], att_w_out[j])
        x = layer_norm(DEEPNORM_ALPHA * x + (1.0 + g1) * y, ln_g[i, 0], ln_b[i, 0])
        u = x * (1.0 + sc2) + sh2
        y = squared_relu_mlp(u, ff_w1[i], ff_w2[i])
        x = layer_norm(DEEPNORM_ALPHA * x + (1.0 + g2) * y, ln_g[i, 1], ln_b[i, 1])
    return x


import jax as _jax
import jax.numpy as _jnp

TWIN_FORMAT = 'train_step'
FWD_PARAMS = ['x', 'c', 'w_ada', 'b_ada', 'ln_g', 'ln_b', 'gla_w_in', 'gla_w_gk2', 'gla_b_gk', 'gla_g_norm', 'gla_w_out', 'att_w_in', 'att_b_in', 'att_rel_bias', 'att_w_out', 'ff_w1', 'ff_w2']
TWIN_WEIGHTS = ['w_ada', 'b_ada', 'ln_g', 'ln_b', 'gla_w_in', 'gla_w_gk2', 'gla_b_gk', 'gla_g_norm', 'gla_w_out', 'att_w_in', 'att_b_in', 'att_rel_bias', 'att_w_out', 'ff_w1', 'ff_w2']
TWIN_DIFF_INPUT = 'x'
TWIN_INPUTS = ['x', 'c', 'w_ada', 'b_ada', 'ln_g', 'ln_b', 'gla_w_in', 'gla_w_gk2', 'gla_b_gk', 'gla_g_norm', 'gla_w_out', 'att_w_in', 'att_b_in', 'att_rel_bias', 'att_w_out', 'ff_w1', 'ff_w2', 'loss_target', 'm_w_ada', 'm_b_ada', 'm_ln_g', 'm_ln_b', 'm_gla_w_in', 'm_gla_w_gk2', 'm_gla_b_gk', 'm_gla_g_norm', 'm_gla_w_out', 'm_att_w_in', 'm_att_b_in', 'm_att_rel_bias', 'm_att_w_out', 'm_ff_w1', 'm_ff_w2', 'v_w_ada', 'v_b_ada', 'v_ln_g', 'v_ln_b', 'v_gla_w_in', 'v_gla_w_gk2', 'v_gla_b_gk', 'v_gla_g_norm', 'v_gla_w_out', 'v_att_w_in', 'v_att_b_in', 'v_att_rel_bias', 'v_att_w_out', 'v_ff_w1', 'v_ff_w2']
TWIN_OUTPUTS = ['loss', 'grad_x', 'grad_w_ada', 'grad_b_ada', 'grad_ln_g', 'grad_ln_b', 'grad_gla_w_in', 'grad_gla_w_gk2', 'grad_gla_b_gk', 'grad_gla_g_norm', 'grad_gla_w_out', 'grad_att_w_in', 'grad_att_b_in', 'grad_att_rel_bias', 'grad_att_w_out', 'grad_ff_w1', 'grad_ff_w2', 'delta_w_ada', 'delta_b_ada', 'delta_ln_g', 'delta_ln_b', 'delta_gla_w_in', 'delta_gla_w_gk2', 'delta_gla_b_gk', 'delta_gla_g_norm', 'delta_gla_w_out', 'delta_att_w_in', 'delta_att_b_in', 'delta_att_rel_bias', 'delta_att_w_out', 'delta_ff_w1', 'delta_ff_w2', 'new_m_w_ada', 'new_m_b_ada', 'new_m_ln_g', 'new_m_ln_b', 'new_m_gla_w_in', 'new_m_gla_w_gk2', 'new_m_gla_b_gk', 'new_m_gla_g_norm', 'new_m_gla_w_out', 'new_m_att_w_in', 'new_m_att_b_in', 'new_m_att_rel_bias', 'new_m_att_w_out', 'new_m_ff_w1', 'new_m_ff_w2', 'new_v_w_ada', 'new_v_b_ada', 'new_v_ln_g', 'new_v_ln_b', 'new_v_gla_w_in', 'new_v_gla_w_gk2', 'new_v_gla_b_gk', 'new_v_gla_g_norm', 'new_v_gla_w_out', 'new_v_att_w_in', 'new_v_att_b_in', 'new_v_att_rel_bias', 'new_v_att_w_out', 'new_v_ff_w1', 'new_v_ff_w2']
TWIN_LEAF_KINDS = {'loss': 'loss', 'grad_x': 'grad_x', 'grad_w_ada': 'grad_w', 'grad_b_ada': 'grad_w', 'grad_ln_g': 'grad_w', 'grad_ln_b': 'grad_w', 'grad_gla_w_in': 'grad_w', 'grad_gla_w_gk2': 'grad_w', 'grad_gla_b_gk': 'grad_w', 'grad_gla_g_norm': 'grad_w', 'grad_gla_w_out': 'grad_w', 'grad_att_w_in': 'grad_w', 'grad_att_b_in': 'grad_w', 'grad_att_rel_bias': 'grad_w', 'grad_att_w_out': 'grad_w', 'grad_ff_w1': 'grad_w', 'grad_ff_w2': 'grad_w', 'delta_w_ada': 'delta_w', 'delta_b_ada': 'delta_w', 'delta_ln_g': 'delta_w', 'delta_ln_b': 'delta_w', 'delta_gla_w_in': 'delta_w', 'delta_gla_w_gk2': 'delta_w', 'delta_gla_b_gk': 'delta_w', 'delta_gla_g_norm': 'delta_w', 'delta_gla_w_out': 'delta_w', 'delta_att_w_in': 'delta_w', 'delta_att_b_in': 'delta_w', 'delta_att_rel_bias': 'delta_w', 'delta_att_w_out': 'delta_w', 'delta_ff_w1': 'delta_w', 'delta_ff_w2': 'delta_w', 'new_m_w_ada': 'new_m', 'new_m_b_ada': 'new_m', 'new_m_ln_g': 'new_m', 'new_m_ln_b': 'new_m', 'new_m_gla_w_in': 'new_m', 'new_m_gla_w_gk2': 'new_m', 'new_m_gla_b_gk': 'new_m', 'new_m_gla_g_norm': 'new_m', 'new_m_gla_w_out': 'new_m', 'new_m_att_w_in': 'new_m', 'new_m_att_b_in': 'new_m', 'new_m_att_rel_bias': 'new_m', 'new_m_att_w_out': 'new_m', 'new_m_ff_w1': 'new_m', 'new_m_ff_w2': 'new_m', 'new_v_w_ada': 'new_v', 'new_v_b_ada': 'new_v', 'new_v_ln_g': 'new_v', 'new_v_ln_b': 'new_v', 'new_v_gla_w_in': 'new_v', 'new_v_gla_w_gk2': 'new_v', 'new_v_gla_b_gk': 'new_v', 'new_v_gla_g_norm': 'new_v', 'new_v_gla_w_out': 'new_v', 'new_v_att_w_in': 'new_v', 'new_v_att_b_in': 'new_v', 'new_v_att_rel_bias': 'new_v', 'new_v_att_w_out': 'new_v', 'new_v_ff_w1': 'new_v', 'new_v_ff_w2': 'new_v'}


def _forward(args):
    return _fwd_reference(*[args[k] for k in FWD_PARAMS])


def _output_shape():
    def fwd():
        inp = _fwd_setup_inputs(0)
        return _fwd_reference(*[inp[k] for k in FWD_PARAMS])
    out = _jax.eval_shape(fwd)
    return out.shape, out.dtype

N_MICROBATCH = 1
ADAM_LR = 0.001
ADAM_B1 = 0.9
ADAM_B2 = 0.999
ADAM_EPS = 1e-08
ADAM_WD = 0.01
ADAM_STEP = 10
PER_EXAMPLE_BATCH_AXIS = {'x': 0, 'c': 0, 'loss_target': 0}
SHARED_INPUTS = []
_WEIGHT_DTYPES = {'w_ada': _jnp.float32, 'b_ada': _jnp.float32, 'ln_g': _jnp.float32, 'ln_b': _jnp.float32, 'gla_w_in': _jnp.float32, 'gla_w_gk2': _jnp.float32, 'gla_b_gk': _jnp.float32, 'gla_g_norm': _jnp.float32, 'gla_w_out': _jnp.float32, 'att_w_in': _jnp.float32, 'att_b_in': _jnp.float32, 'att_rel_bias': _jnp.float32, 'att_w_out': _jnp.float32, 'ff_w1': _jnp.float32, 'ff_w2': _jnp.float32}
MOMENT_SCALE = {'w_ada': 5.328714e-02, 'b_ada': 1.352178e-01, 'ln_g': 2.285562e+01, 'ln_b': 5.424982e+00, 'gla_w_in': 4.636264e-02, 'gla_w_gk2': 7.063163e-03, 'gla_b_gk': 2.398975e-02, 'gla_g_norm': 3.987625e-02, 'gla_w_out': 9.364166e-02, 'att_w_in': 2.066739e-02, 'att_b_in': 1.594686e-01, 'att_rel_bias': 4.404597e-03, 'att_w_out': 9.333087e-02, 'ff_w1': 4.594835e-02, 'ff_w2': 3.027847e-01}


def _to_microbatches(a, axis):
    t = _jnp.moveaxis(a, axis, 0)
    t = t.reshape((N_MICROBATCH, t.shape[0] // N_MICROBATCH) + t.shape[1:])
    return _jnp.moveaxis(t, 1, axis + 1)


def setup_inputs(seed: int = 0) -> dict:
    inp = _fwd_setup_inputs(seed)
    key = _jax.random.fold_in(_jax.random.key(seed), 7919)
    shape, _ = _output_shape()
    out = dict(inp)
    out["loss_target"] = _jax.random.normal(_jax.random.fold_in(key, 0), shape, _jnp.float32)
    for i, name in enumerate(TWIN_WEIGHTS):
        w = inp[name].astype(_jnp.float32)
        if MOMENT_SCALE is None:
            s = _jnp.sqrt(_jnp.mean(_jnp.square(w)) + 1e-30)
        else:
            s = MOMENT_SCALE[name]
        km, kv = _jax.random.split(_jax.random.fold_in(key, i + 1))
        out[name] = w
        out["m_" + name] = s * _jax.random.normal(km, w.shape, _jnp.float32)
        out["v_" + name] = (s * s) * _jax.random.uniform(kv, w.shape, _jnp.float32, 0.5, 1.5)
    if N_MICROBATCH > 1:
        for name, axis in PER_EXAMPLE_BATCH_AXIS.items():
            out[name] = _to_microbatches(out[name], axis)
    return {'x': out['x'], 'c': out['c'], 'w_ada': out['w_ada'], 'b_ada': out['b_ada'], 'ln_g': out['ln_g'], 'ln_b': out['ln_b'], 'gla_w_in': out['gla_w_in'], 'gla_w_gk2': out['gla_w_gk2'], 'gla_b_gk': out['gla_b_gk'], 'gla_g_norm': out['gla_g_norm'], 'gla_w_out': out['gla_w_out'], 'att_w_in': out['att_w_in'], 'att_b_in': out['att_b_in'], 'att_rel_bias': out['att_rel_bias'], 'att_w_out': out['att_w_out'], 'ff_w1': out['ff_w1'], 'ff_w2': out['ff_w2'], 'loss_target': out['loss_target'], 'm_w_ada': out['m_w_ada'], 'm_b_ada': out['m_b_ada'], 'm_ln_g': out['m_ln_g'], 'm_ln_b': out['m_ln_b'], 'm_gla_w_in': out['m_gla_w_in'], 'm_gla_w_gk2': out['m_gla_w_gk2'], 'm_gla_b_gk': out['m_gla_b_gk'], 'm_gla_g_norm': out['m_gla_g_norm'], 'm_gla_w_out': out['m_gla_w_out'], 'm_att_w_in': out['m_att_w_in'], 'm_att_b_in': out['m_att_b_in'], 'm_att_rel_bias': out['m_att_rel_bias'], 'm_att_w_out': out['m_att_w_out'], 'm_ff_w1': out['m_ff_w1'], 'm_ff_w2': out['m_ff_w2'], 'v_w_ada': out['v_w_ada'], 'v_b_ada': out['v_b_ada'], 'v_ln_g': out['v_ln_g'], 'v_ln_b': out['v_ln_b'], 'v_gla_w_in': out['v_gla_w_in'], 'v_gla_w_gk2': out['v_gla_w_gk2'], 'v_gla_b_gk': out['v_gla_b_gk'], 'v_gla_g_norm': out['v_gla_g_norm'], 'v_gla_w_out': out['v_gla_w_out'], 'v_att_w_in': out['v_att_w_in'], 'v_att_b_in': out['v_att_b_in'], 'v_att_rel_bias': out['v_att_rel_bias'], 'v_att_w_out': out['v_att_w_out'], 'v_ff_w1': out['v_ff_w1'], 'v_ff_w2': out['v_ff_w2']}


def _loss(weights, diff, rest, loss_target):
    with _jax.named_scope("forward"):
        args = {**rest, TWIN_DIFF_INPUT: diff, **{k: w.astype(_WEIGHT_DTYPES[k]) for k, w in weights.items()}}
        y = _forward(args)
    with _jax.named_scope("loss_head"):
        err = _jnp.square(y.astype(_jnp.float32) - loss_target)
        return 0.5 * _jnp.sum(_jnp.mean(err, axis=-1)) if err.ndim else 0.5 * err


def _adamw(w, g, m, v):
    m = ADAM_B1 * m + (1.0 - ADAM_B1) * g
    v = ADAM_B2 * v + (1.0 - ADAM_B2) * _jnp.square(g)
    m_hat = m / (1.0 - ADAM_B1 ** ADAM_STEP)
    v_hat = v / (1.0 - ADAM_B2 ** ADAM_STEP)
    delta = -ADAM_LR * (m_hat / (_jnp.sqrt(v_hat) + ADAM_EPS) + ADAM_WD * w)
    return delta, m, v


def reference(x, c, w_ada, b_ada, ln_g, ln_b, gla_w_in, gla_w_gk2, gla_b_gk, gla_g_norm, gla_w_out, att_w_in, att_b_in, att_rel_bias, att_w_out, ff_w1, ff_w2, loss_target, m_w_ada, m_b_ada, m_ln_g, m_ln_b, m_gla_w_in, m_gla_w_gk2, m_gla_b_gk, m_gla_g_norm, m_gla_w_out, m_att_w_in, m_att_b_in, m_att_rel_bias, m_att_w_out, m_ff_w1, m_ff_w2, v_w_ada, v_b_ada, v_ln_g, v_ln_b, v_gla_w_in, v_gla_w_gk2, v_gla_b_gk, v_gla_g_norm, v_gla_w_out, v_att_w_in, v_att_b_in, v_att_rel_bias, v_att_w_out, v_ff_w1, v_ff_w2):
    given = dict(x=x, c=c, w_ada=w_ada, b_ada=b_ada, ln_g=ln_g, ln_b=ln_b, gla_w_in=gla_w_in, gla_w_gk2=gla_w_gk2, gla_b_gk=gla_b_gk, gla_g_norm=gla_g_norm, gla_w_out=gla_w_out, att_w_in=att_w_in, att_b_in=att_b_in, att_rel_bias=att_rel_bias, att_w_out=att_w_out, ff_w1=ff_w1, ff_w2=ff_w2, loss_target=loss_target, m_w_ada=m_w_ada, m_b_ada=m_b_ada, m_ln_g=m_ln_g, m_ln_b=m_ln_b, m_gla_w_in=m_gla_w_in, m_gla_w_gk2=m_gla_w_gk2, m_gla_b_gk=m_gla_b_gk, m_gla_g_norm=m_gla_g_norm, m_gla_w_out=m_gla_w_out, m_att_w_in=m_att_w_in, m_att_b_in=m_att_b_in, m_att_rel_bias=m_att_rel_bias, m_att_w_out=m_att_w_out, m_ff_w1=m_ff_w1, m_ff_w2=m_ff_w2, v_w_ada=v_w_ada, v_b_ada=v_b_ada, v_ln_g=v_ln_g, v_ln_b=v_ln_b, v_gla_w_in=v_gla_w_in, v_gla_w_gk2=v_gla_w_gk2, v_gla_b_gk=v_gla_b_gk, v_gla_g_norm=v_gla_g_norm, v_gla_w_out=v_gla_w_out, v_att_w_in=v_att_w_in, v_att_b_in=v_att_b_in, v_att_rel_bias=v_att_rel_bias, v_att_w_out=v_att_w_out, v_ff_w1=v_ff_w1, v_ff_w2=v_ff_w2)
    weights = {n: given[n] for n in TWIN_WEIGHTS}
    shared = {n: given[n] for n in SHARED_INPUTS}
    per_example = {n: given[n] for n in ['x', 'c']}
    grad_fn = _jax.value_and_grad(_loss, argnums=(0, 1))

    def one_microbatch(ex, loss_target):
        ex = dict(ex)
        diff = ex.pop(TWIN_DIFF_INPUT)
        return grad_fn(weights, diff, {**shared, **ex}, loss_target)

    if N_MICROBATCH == 1:
        loss, (grad_w, grad_x) = one_microbatch(per_example, given["loss_target"])
    else:
        def body(carry, xs):
            loss_sum, grad_sum = carry
            l_k, (gw_k, gx_k) = one_microbatch(xs[0], xs[1])
            with _jax.named_scope("update"):
                return (loss_sum + l_k, _jax.tree.map(_jnp.add, grad_sum, gw_k)), gx_k

        init = (_jnp.zeros((), _jnp.float32), _jax.tree.map(_jnp.zeros_like, weights))
        (loss, grad_w), grad_x = _jax.lax.scan(body, init, (per_example, given["loss_target"]))
    with _jax.named_scope("update"):
        delta_w, new_m, new_v = {}, {}, {}
        for n in TWIN_WEIGHTS:
            delta_w[n], new_m[n], new_v[n] = _adamw(weights[n], grad_w[n], given["m_" + n], given["v_" + n])
    return (loss, grad_x, *[grad_w[n] for n in TWIN_WEIGHTS], *[delta_w[n] for n in TWIN_WEIGHTS],
            *[new_m[n] for n in TWIN_WEIGHTS], *[new_v[n] for n in TWIN_WEIGHTS])
```

```python
import functools

import jax
import jax.numpy as jnp
from jax import lax
from jax.experimental import pallas as pl
from jax.experimental.pallas import tpu as pltpu

F32 = jnp.float32
BF16 = jnp.bfloat16
HIGHEST = lax.Precision.HIGHEST
MESH = pl.DeviceIdType.MESH

D_MODEL = 1024
DEPTH = 4
CHUNK = 64
GLA_HEADS = 4
GLA_DK = 512
GLA_DV = 1024
GLA_DK_HEAD = 128
GLA_DV_HEAD = 256
GLA_RANK = 16
GLA_IN = 3088
GLA_IN_PAD = 3200
GLA_LR_OFF = 3072
ATT_HEADS = 16
ATT_HD = 64
LEFT_CHUNKS = 8
MAX_REL = 128
N_REL = 257
D_FF = 4096
ALPHA = (2.0 * DEPTH) ** 0.25
LN_EPS = 1e-5
RMS_EPS = 1e-6
NEG_INF = -1e30
GLA_SCALE = GLA_DK_HEAD ** -0.5
ATT_SCALE = ATT_HD ** -0.5
ADAM_LR = 0.001
ADAM_B1 = 0.9
ADAM_B2 = 0.999
ADAM_EPS = 1e-08
ADAM_WD = 0.01
ADAM_STEP = 10

ATT_TQ = 256
ATT_KW = 768
GLA_TB = 256
VMEM_LIMIT = 56 * 1024 * 1024
N_CHIPS = 4
N_DEV = 8
PACK_W = 1024


def _dot(a, b):
    return jnp.dot(a, b, preferred_element_type=F32)


def _dot_nt(a, b):
    return lax.dot_general(a, b, (((1,), (1,)), ((), ())), preferred_element_type=F32)


def _dot_tn(a, b):
    return lax.dot_general(a, b, (((0,), (0,)), ((), ())), preferred_element_type=F32)


def _cp(sem, vmem=VMEM_LIMIT):
    return pltpu.CompilerParams(dimension_semantics=sem, vmem_limit_bytes=vmem)


def _row_spec(n):
    return pl.BlockSpec((1, n), lambda *_: (0, 0))


def _sigmoid(x):
    return 1.0 / (1.0 + jnp.exp(-x))


def _log_sigmoid(x):
    return jnp.minimum(x, 0.0) - jnp.log1p(jnp.exp(-jnp.abs(x)))


def modulate(x, sc, sh, name):
    S, D = x.shape
    tm = min(512, S)

    def kern(x_ref, sc_ref, sh_ref, u_ref):
        u_ref[...] = (x_ref[...] * (1.0 + sc_ref[...]) + sh_ref[...]).astype(BF16)

    return pl.pallas_call(
        kern, name=name, grid=(S // tm,),
        in_specs=[pl.BlockSpec((tm, D), lambda i: (i, 0)), _row_spec(D), _row_spec(D)],
        out_specs=pl.BlockSpec((tm, D), lambda i: (i, 0)),
        out_shape=jax.ShapeDtypeStruct((S, D), BF16),
        compiler_params=_cp(("parallel",)),
    )(x, sc, sh)


def loss_head(x, t, name):
    S, D = x.shape
    tm = min(512, S)

    def kern(x_ref, t_ref, dx_ref, l_ref):
        @pl.when(pl.program_id(0) == 0)
        def _():
            l_ref[...] = jnp.zeros_like(l_ref)
        e = x_ref[...] - t_ref[...]
        dx_ref[...] = e * (1.0 / D)
        l_ref[...] += jnp.sum(e * e)

    return pl.pallas_call(
        kern, name=name, grid=(S // tm,),
        in_specs=[pl.BlockSpec((tm, D), lambda i: (i, 0)), pl.BlockSpec((tm, D), lambda i: (i, 0))],
        out_specs=[pl.BlockSpec((tm, D), lambda i: (i, 0)), pl.BlockSpec((8, 128), lambda i: (0, 0))],
        out_shape=[jax.ShapeDtypeStruct((S, D), F32), jax.ShapeDtypeStruct((8, 128), F32)],
        compiler_params=_cp(("arbitrary",)),
    )(x, t)


def colsum(a, name):
    S, N = a.shape
    tm = min(512, S)

    def kern(a_ref, o_ref):
        @pl.when(pl.program_id(0) == 0)
        def _():
            o_ref[...] = jnp.zeros_like(o_ref)
        o_ref[...] += jnp.sum(a_ref[...].astype(F32), axis=0, keepdims=True)

    return pl.pallas_call(
        kern, name=name, grid=(S // tm,),
        in_specs=[pl.BlockSpec((tm, N), lambda i: (i, 0))],
        out_specs=pl.BlockSpec((1, N), lambda i: (0, 0)),
        out_shape=jax.ShapeDtypeStruct((1, N), F32),
        compiler_params=_cp(("arbitrary",)),
    )(a)


def silu_rows(c_all, name):
    def kern(c_ref, o_ref):
        c = c_ref[...]
        o_ref[...] = (c * _sigmoid(c)).astype(BF16)

    return pl.pallas_call(kern, name=name, out_shape=jax.ShapeDtypeStruct(c_all.shape, BF16))(c_all)


def sum_over_devices(g, name):
    n, R, C = g.shape

    def kern(g_ref, o_ref):
        acc = g_ref[0]
        for d in range(1, n):
            acc = acc + g_ref[d]
        o_ref[...] = acc

    return pl.pallas_call(kern, name=name, out_shape=jax.ShapeDtypeStruct((R, C), F32))(g)


def _rows_block(R, C, budget=1 << 20):
    if R * C * 4 <= budget or R % 8:
        return R
    tr = max(8, (budget // (C * 4)) // 8 * 8)
    while R % tr:
        tr -= 8
    return tr


def adamw(w, g, m, v, name):
    shape = w.shape
    C = shape[-1]
    R = w.size // C
    w2, g2, m2, v2 = (t.reshape(R, C) for t in (w, g, m, v))
    tr = _rows_block(R, C)
    c1 = 1.0 - ADAM_B1 ** ADAM_STEP
    c2 = 1.0 - ADAM_B2 ** ADAM_STEP

    def kern(w_ref, g_ref, m_ref, v_ref, d_ref, nm_ref, nv_ref):
        gg = g_ref[...]
        nm = ADAM_B1 * m_ref[...] + (1.0 - ADAM_B1) * gg
        nv = ADAM_B2 * v_ref[...] + (1.0 - ADAM_B2) * (gg * gg)
        m_hat = nm / c1
        v_hat = nv / c2
        d_ref[...] = -ADAM_LR * (m_hat / (jnp.sqrt(v_hat) + ADAM_EPS) + ADAM_WD * w_ref[...])
        nm_ref[...] = nm
        nv_ref[...] = nv

    spec = pl.BlockSpec((tr, C), lambda i: (i, 0))
    outs = pl.pallas_call(
        kern, name=name, grid=(R // tr,),
        in_specs=[spec] * 4, out_specs=[spec] * 3,
        out_shape=[jax.ShapeDtypeStruct((R, C), F32)] * 3,
        compiler_params=_cp(("parallel",)),
    )(w2, g2, m2, v2)
    return tuple(o.reshape(shape) for o in outs)


def _tn_for(N):
    for tn in (1024, 768, 640, 512, 384, 256, 128):
        if N % tn == 0:
            return tn
    return N


def mm_plain(a, b, name, *, mode="f32", bias=None, h=None, tm=512):
    M, K = a.shape
    N = b.shape[1]
    tm = min(tm, M)
    tn = _tn_for(N)
    a_spec = pl.BlockSpec((tm, K), lambda i, j: (i, 0))
    b_spec = pl.BlockSpec((K, tn), lambda i, j: (0, j))
    o_spec = pl.BlockSpec((tm, tn), lambda i, j: (i, j))
    ins, in_specs = [a, b], [a_spec, b_spec]

    if mode == "f32":
        if bias is not None:
            ins.append(bias)
            in_specs.append(pl.BlockSpec((1, tn), lambda i, j: (0, j)))

            def kern(a_ref, b_ref, bias_ref, o_ref):
                o_ref[...] = _dot(a_ref[...], b_ref[...]) + bias_ref[...]
        else:
            def kern(a_ref, b_ref, o_ref):
                o_ref[...] = _dot(a_ref[...], b_ref[...])
        out_specs, out_shape = o_spec, jax.ShapeDtypeStruct((M, N), F32)
    elif mode == "mlp_up":
        def kern(a_ref, b_ref, h_ref, act_ref):
            acc = _dot(a_ref[...], b_ref[...])
            h_ref[...] = acc
            r = jnp.maximum(acc, 0.0)
            act_ref[...] = (r * r).astype(BF16)
        out_specs = [o_spec, o_spec]
        out_shape = [jax.ShapeDtypeStruct((M, N), F32), jax.ShapeDtypeStruct((M, N), BF16)]
    elif mode == "mlp_dn":
        ins.append(h)
        in_specs.append(o_spec)

        def kern(a_ref, b_ref, h_ref, o_ref):
            acc = _dot(a_ref[...], b_ref[...])
            o_ref[...] = (acc * (2.0 * jnp.maximum(h_ref[...], 0.0))).astype(BF16)
        out_specs, out_shape = o_spec, jax.ShapeDtypeStruct((M, N), BF16)
    else:
        raise ValueError(mode)

    return pl.pallas_call(
        kern, name=name, grid=(M // tm, N // tn), in_specs=in_specs, out_specs=out_specs,
        out_shape=out_shape, compiler_params=_cp(("parallel", "parallel")),
    )(*ins)


def mm_down_ln(a, b, x_in, gate1p, ln_g, ln_b, name, *, tm=256):
    M, K = a.shape
    D = b.shape[1]
    tm = min(tm, M)

    def kern(a_ref, b_ref, x_ref, gp_ref, lg_ref, lb_ref, y_ref, xo_ref):
        y = _dot(a_ref[...], b_ref[...])
        y_ref[...] = y
        z = ALPHA * x_ref[...] + gp_ref[...] * y
        mu = jnp.mean(z, axis=-1, keepdims=True)
        zc = z - mu
        var = jnp.mean(zc * zc, axis=-1, keepdims=True)
        xo_ref[...] = (zc * lax.rsqrt(var + LN_EPS)) * lg_ref[...] + lb_ref[...]

    tile = pl.BlockSpec((tm, D), lambda i: (i, 0))
    return pl.pallas_call(
        kern, name=name, grid=(M // tm,),
        in_specs=[pl.BlockSpec((tm, K), lambda i: (i, 0)), pl.BlockSpec((K, D), lambda i: (0, 0)), tile,
                  _row_spec(D), _row_spec(D), _row_spec(D)],
        out_specs=[tile, tile],
        out_shape=[jax.ShapeDtypeStruct((M, D), F32)] * 2,
        compiler_params=_cp(("parallel",)),
    )(a, b, x_in, gate1p, ln_g, ln_b)


def mm_down_comb(a, b, dz, x_in, sc1p, name, *, tm=256):
    M, K = a.shape
    D = b.shape[1]
    tm = min(tm, M)

    def kern(a_ref, b_ref, dz_ref, x_ref, sp_ref, dx_ref, s_ref):
        @pl.when(pl.program_id(0) == 0)
        def _():
            s_ref[...] = jnp.zeros_like(s_ref)
        du = _dot(a_ref[...], b_ref[...])
        dx_ref[...] = ALPHA * dz_ref[...] + du * sp_ref[...]
        s_ref[0:1, :] += jnp.sum(du * x_ref[...], axis=0, keepdims=True)
        s_ref[1:2, :] += jnp.sum(du, axis=0, keepdims=True)

    tile = pl.BlockSpec((tm, D), lambda i: (i, 0))
    return pl.pallas_call(
        kern, name=name, grid=(M // tm,),
        in_specs=[pl.BlockSpec((tm, K), lambda i: (i, 0)), pl.BlockSpec((K, D), lambda i: (0, 0)), tile, tile,
                  _row_spec(D)],
        out_specs=[tile, pl.BlockSpec((8, D), lambda i: (0, 0))],
        out_shape=[jax.ShapeDtypeStruct((M, D), F32), jax.ShapeDtypeStruct((8, D), F32)],
        compiler_params=_cp(("arbitrary",)),
    )(a, b, dz, x_in, sc1p)


def mm_w(a, b, name, *, ts=1024, tk=512):
    S, K = a.shape
    N = b.shape[1]
    ts = min(ts, S)
    tk = min(tk, K)
    tn = _tn_for(N)

    def kern(a_ref, b_ref, o_ref):
        @pl.when(pl.program_id(2) == 0)
        def _():
            o_ref[...] = jnp.zeros_like(o_ref)
        o_ref[...] += _dot_tn(a_ref[...], b_ref[...])

    return pl.pallas_call(
        kern, name=name, grid=(K // tk, N // tn, S // ts),
        in_specs=[pl.BlockSpec((ts, tk), lambda k, n, s: (s, k)), pl.BlockSpec((ts, tn), lambda k, n, s: (s, n))],
        out_specs=pl.BlockSpec((tk, tn), lambda k, n, s: (k, n)),
        out_shape=jax.ShapeDtypeStruct((K, N), F32),
        compiler_params=_cp(("parallel", "parallel", "arbitrary")),
    )(a, b)


def mm_f32(a, b, name):
    def kern(a_ref, b_ref, o_ref):
        o_ref[...] = jnp.dot(a_ref[...], b_ref[...], precision=HIGHEST, preferred_element_type=F32)

    return pl.pallas_call(kern, name=name, out_shape=jax.ShapeDtypeStruct((a.shape[0], b.shape[1]), F32),
                          compiler_params=pltpu.CompilerParams(vmem_limit_bytes=VMEM_LIMIT))(a, b)


def ln_bwd(dxo, x_in, y, gate1p, ln_g, name, *, tm=256):
    S, D = dxo.shape
    tm = min(tm, S)

    def kern(dxo_ref, x_ref, y_ref, gp_ref, lg_ref, dz_ref, dy_ref, s_ref):
        @pl.when(pl.program_id(0) == 0)
        def _():
            s_ref[...] = jnp.zeros_like(s_ref)
        dxo_t = dxo_ref[...]
        yv = y_ref[...]
        z = ALPHA * x_ref[...] + gp_ref[...] * yv
        mu = jnp.mean(z, axis=-1, keepdims=True)
        zc = z - mu
        var = jnp.mean(zc * zc, axis=-1, keepdims=True)
        rstd = lax.rsqrt(var + LN_EPS)
        xhat = zc * rstd
        dxh = dxo_t * lg_ref[...]
        dz = rstd * (dxh - jnp.mean(dxh, axis=-1, keepdims=True)
                     - xhat * jnp.mean(dxh * xhat, axis=-1, keepdims=True))
        dz_ref[...] = dz
        dy_ref[...] = (gp_ref[...] * dz).astype(BF16)
        s_ref[0:1, :] += jnp.sum(dxo_t * xhat, axis=0, keepdims=True)
        s_ref[1:2, :] += jnp.sum(dxo_t, axis=0, keepdims=True)
        s_ref[2:3, :] += jnp.sum(dz * yv, axis=0, keepdims=True)

    tile = pl.BlockSpec((tm, D), lambda i: (i, 0))
    return pl.pallas_call(
        kern, name=name, grid=(S // tm,),
        in_specs=[tile, tile, tile, _row_spec(D), _row_spec(D)],
        out_specs=[tile, tile, pl.BlockSpec((8, D), lambda i: (0, 0))],
        out_shape=[jax.ShapeDtypeStruct((S, D), F32), jax.ShapeDtypeStruct((S, D), BF16),
                   jax.ShapeDtypeStruct((8, D), F32)],
        compiler_params=_cp(("arbitrary",)),
    )(dxo, x_in, y, gate1p, ln_g)


def _tri64():
    r = lax.broadcasted_iota(jnp.int32, (CHUNK, CHUNK), 0)
    c = lax.broadcasted_iota(jnp.int32, (CHUNK, CHUNK), 1)
    return r >= c


def _gla_chunk_common(proj_ref, rows, b, h):
    kc = slice(h * GLA_DK_HEAD, (h + 1) * GLA_DK_HEAD)
    bh = b[:, kc]
    ep = jnp.exp(bh)
    en = jnp.exp(-bh)
    bl = bh[CHUNK - 1:CHUNK, :]
    ee = jnp.exp(bl - bh)
    dec = jnp.exp(bl)
    q = proj_ref[rows, h * GLA_DK_HEAD:(h + 1) * GLA_DK_HEAD] * GLA_SCALE
    k = proj_ref[rows, GLA_DK + h * GLA_DK_HEAD:GLA_DK + (h + 1) * GLA_DK_HEAD]
    v = proj_ref[rows, 2 * GLA_DK + h * GLA_DV_HEAD:2 * GLA_DK + (h + 1) * GLA_DV_HEAD]
    g = proj_ref[rows, 2 * GLA_DK + GLA_DV + h * GLA_DV_HEAD:2 * GLA_DK + GLA_DV + (h + 1) * GLA_DV_HEAD]
    return ep, en, ee, dec, q, k, v, g


def gla_fwd(proj, wgk_p, bgk, gnorm, name):
    S = proj.shape[0]
    TB = min(GLA_TB, S)
    ncb = TB // CHUNK

    def kern(proj_ref, wgk_ref, bgk_ref, gn_ref, zg_ref, st_ref, state_scr, la_scr):
        @pl.when(pl.program_id(0) == 0)
        def _():
            state_scr[...] = jnp.zeros_like(state_scr)
        lr = proj_ref[:, GLA_LR_OFF:GLA_IN_PAD].astype(BF16)
        gk = _dot(lr, wgk_ref[...]) + bgk_ref[...]
        la_scr[...] = _log_sigmoid(gk) * (1.0 / 16.0)
        lower = _tri64()
        tri = lower.astype(F32)

        def chunk(c, carry):
            rows = pl.ds(pl.multiple_of(c * CHUNK, CHUNK), CHUNK)
            b = jnp.dot(tri, la_scr[rows, :], precision=HIGHEST, preferred_element_type=F32)
            for h in range(GLA_HEADS):
                ep, en, ee, dec, q, k, v, g = _gla_chunk_common(proj_ref, rows, b, h)
                qf = (q * ep).astype(BF16)
                a_f = _dot_nt(qf, (k * en).astype(BF16))
                a_b = _dot_nt((q * en).astype(BF16), (k * ep).astype(BF16))
                amat = jnp.where(lower, a_f, a_b).astype(BF16)
                st = state_scr[h]
                st_ref[c, h] = st
                vb = v.astype(BF16)
                o = _dot(amat, vb) + _dot_nt(qf, st.astype(BF16))
                r = lax.rsqrt(jnp.mean(o * o, axis=-1, keepdims=True) + RMS_EPS)
                on = (o * r) * gn_ref[:, h * GLA_DV_HEAD:(h + 1) * GLA_DV_HEAD]
                zg_ref[rows, h * GLA_DV_HEAD:(h + 1) * GLA_DV_HEAD] = (on * (g * _sigmoid(g))).astype(BF16)
                state_scr[h] = st * dec + _dot_tn(vb, (k * ee).astype(BF16))
            return carry

        lax.fori_loop(0, ncb, chunk, 0)

    return pl.pallas_call(
        kern, name=name, grid=(S // TB,),
        in_specs=[pl.BlockSpec((TB, GLA_IN_PAD), lambda i: (i, 0)),
                  pl.BlockSpec((128, GLA_DK), lambda i: (0, 0)), _row_spec(GLA_DK), _row_spec(GLA_DV)],
        out_specs=[pl.BlockSpec((TB, GLA_DV), lambda i: (i, 0)),
                   pl.BlockSpec((ncb, GLA_HEADS, GLA_DV_HEAD, GLA_DK_HEAD), lambda i: (i, 0, 0, 0))],
        out_shape=[jax.ShapeDtypeStruct((S, GLA_DV), BF16),
                   jax.ShapeDtypeStruct((S // CHUNK, GLA_HEADS, GLA_DV_HEAD, GLA_DK_HEAD), F32)],
        scratch_shapes=[pltpu.VMEM((GLA_HEADS, GLA_DV_HEAD, GLA_DK_HEAD), F32), pltpu.VMEM((TB, GLA_DK), F32)],
        compiler_params=_cp(("arbitrary",)),
    )(proj, wgk_p, bgk, gnorm)


def gla_bwd(proj, states, dzg, wgk_p, bgk, gnorm, name):
    S = proj.shape[0]
    TB = min(GLA_TB, S)
    ncb = TB // CHUNK
    nb = S // TB

    def kern(proj_ref, st_ref, dzg_ref, wgk_ref, bgk_ref, gn_ref,
             dproj_ref, dwgk_ref, dbgk_ref, dgn_ref, dstate_scr, la_scr, gk_scr, dgk_scr):
        @pl.when(pl.program_id(0) == 0)
        def _():
            dstate_scr[...] = jnp.zeros_like(dstate_scr)
            dwgk_ref[...] = jnp.zeros_like(dwgk_ref)
            dbgk_ref[...] = jnp.zeros_like(dbgk_ref)
            dgn_ref[...] = jnp.zeros_like(dgn_ref)
        lr = proj_ref[:, GLA_LR_OFF:GLA_IN_PAD].astype(BF16)
        gk = _dot(lr, wgk_ref[...]) + bgk_ref[...]
        gk_scr[...] = gk
        la_scr[...] = _log_sigmoid(gk) * (1.0 / 16.0)
        lower = _tri64()
        tri = lower.astype(F32)
        r_i = lax.broadcasted_iota(jnp.int32, (CHUNK, CHUNK), 0)
        c_i = lax.broadcasted_iota(jnp.int32, (CHUNK, CHUNK), 1)
        triu = (c_i >= r_i).astype(F32)
        last_row = lax.broadcasted_iota(jnp.int32, (CHUNK, GLA_DK_HEAD), 0) == CHUNK - 1

        def chunk(cc, carry):
            c = ncb - 1 - cc
            rows = pl.ds(pl.multiple_of(c * CHUNK, CHUNK), CHUNK)
            b = jnp.dot(tri, la_scr[rows, :], precision=HIGHEST, preferred_element_type=F32)
            for h in range(GLA_HEADS):
                kc = slice(h * GLA_DK_HEAD, (h + 1) * GLA_DK_HEAD)
                vc = slice(h * GLA_DV_HEAD, (h + 1) * GLA_DV_HEAD)
                ep, en, ee, dec, q, k, v, g = _gla_chunk_common(proj_ref, rows, b, h)
                qf = q * ep
                kn = k * en
                qn = q * en
                kp = k * ep
                ke = k * ee
                qf_b, kn_b, qn_b, kp_b, ke_b = (t.astype(BF16) for t in (qf, kn, qn, kp, ke))
                vb = v.astype(BF16)
                amat = jnp.where(lower, _dot_nt(qf_b, kn_b), _dot_nt(qn_b, kp_b)).astype(BF16)
                st = st_ref[c, h]
                st_b = st.astype(BF16)
                o = _dot(amat, vb) + _dot_nt(qf_b, st_b)
                r = lax.rsqrt(jnp.mean(o * o, axis=-1, keepdims=True) + RMS_EPS)
                oh = o * r
                gn = gn_ref[:, vc]
                sg = _sigmoid(g)
                dz = dzg_ref[rows, vc]
                don = dz * (g * sg)
                dg = dz * (oh * gn) * (sg * (1.0 + g * (1.0 - sg)))
                dgn_ref[:, vc] += jnp.sum(don * oh, axis=0, keepdims=True)
                doh = don * gn
                do = r * (doh - oh * jnp.mean(doh * oh, axis=-1, keepdims=True))
                do_b = do.astype(BF16)
                dst = dstate_scr[h]
                dst_b = dst.astype(BF16)
                dv = _dot_tn(amat, do_b) + _dot_nt(ke_b, dst_b)
                da = _dot_nt(do_b, vb)
                da_f = jnp.where(lower, da, 0.0).astype(BF16)
                da_b = jnp.where(lower, 0.0, da).astype(BF16)
                dqf = _dot(da_f, kn_b) + _dot(do_b, st_b)
                dkn = _dot_tn(da_f, qf_b)
                dqn = _dot(da_b, kp_b)
                dkp = _dot_tn(da_b, qn_b)
                dke = _dot(vb, dst_b)
                ddec = jnp.sum(dst * st, axis=0, keepdims=True)
                dstate_scr[h] = dst * dec + _dot_tn(do_b, qf_b)
                dq = (dqf * ep + dqn * en) * GLA_SCALE
                dk = dkn * en + dkp * ep + dke * ee
                db = dqf * qf - dkn * kn - dqn * qn + dkp * kp - dke * ke
                dbl = jnp.sum(dke * ke, axis=0, keepdims=True) + ddec * dec
                db = db + jnp.where(last_row, dbl, 0.0)
                dla = jnp.dot(triu, db, precision=HIGHEST, preferred_element_type=F32)
                dgk_scr[rows, kc] = dla * (1.0 / 16.0) * _sigmoid(-gk_scr[rows, kc])
                dproj_ref[rows, kc] = dq.astype(BF16)
                dproj_ref[rows, GLA_DK + h * GLA_DK_HEAD:GLA_DK + (h + 1) * GLA_DK_HEAD] = dk.astype(BF16)
                dproj_ref[rows, 2 * GLA_DK + h * GLA_DV_HEAD:2 * GLA_DK + (h + 1) * GLA_DV_HEAD] = dv.astype(BF16)
                dproj_ref[rows, 2 * GLA_DK + GLA_DV + h * GLA_DV_HEAD:
                          2 * GLA_DK + GLA_DV + (h + 1) * GLA_DV_HEAD] = dg.astype(BF16)
            return carry

        lax.fori_loop(0, ncb, chunk, 0)
        dgk = dgk_scr[...]
        dgk_b = dgk.astype(BF16)
        dproj_ref[:, GLA_LR_OFF:GLA_IN_PAD] = _dot_nt(dgk_b, wgk_ref[...]).astype(BF16)
        dwgk_ref[...] += _dot_tn(lr, dgk_b)
        dbgk_ref[...] += jnp.sum(dgk, axis=0, keepdims=True)

    rev = lambda i: (nb - 1 - i, 0)
    return pl.pallas_call(
        kern, name=name, grid=(nb,),
        in_specs=[pl.BlockSpec((TB, GLA_IN_PAD), rev),
                  pl.BlockSpec((ncb, GLA_HEADS, GLA_DV_HEAD, GLA_DK_HEAD), lambda i: (nb - 1 - i, 0, 0, 0)),
                  pl.BlockSpec((TB, GLA_DV), rev),
                  pl.BlockSpec((128, GLA_DK), lambda i: (0, 0)), _row_spec(GLA_DK), _row_spec(GLA_DV)],
        out_specs=[pl.BlockSpec((TB, GLA_IN_PAD), rev),
                   pl.BlockSpec((128, GLA_DK), lambda i: (0, 0)), _row_spec(GLA_DK), _row_spec(GLA_DV)],
        out_shape=[jax.ShapeDtypeStruct((S, GLA_IN_PAD), BF16), jax.ShapeDtypeStruct((128, GLA_DK), F32),
                   jax.ShapeDtypeStruct((1, GLA_DK), F32), jax.ShapeDtypeStruct((1, GLA_DV), F32)],
        scratch_shapes=[pltpu.VMEM((GLA_HEADS, GLA_DV_HEAD, GLA_DK_HEAD), F32), pltpu.VMEM((TB, GLA_DK), F32),
                        pltpu.VMEM((TB, GLA_DK), F32), pltpu.VMEM((TB, GLA_DK), F32)],
        compiler_params=_cp(("arbitrary",)),
    )(proj, states, dzg, wgk_p, bgk, gnorm)


def _att_window(i):
    return pl.multiple_of(jnp.maximum(i * ATT_TQ - LEFT_CHUNKS * CHUNK, 0), ATT_TQ)


def _att_valid(i, ws):
    t = i * ATT_TQ + lax.broadcasted_iota(jnp.int32, (ATT_TQ, ATT_KW), 0)
    j = ws + lax.broadcasted_iota(jnp.int32, (ATT_TQ, ATT_KW), 1)
    qc = lax.shift_right_arithmetic(t, 6)
    kc = lax.shift_right_arithmetic(j, 6)
    return (kc <= qc) & (kc >= qc - LEFT_CHUNKS)


def _att_probs(q_ref, kw, bias_ref, valid, hh):
    hs = slice(hh * ATT_HD, (hh + 1) * ATT_HD)
    q = (q_ref[:, hs] * ATT_SCALE).astype(BF16)
    k = kw[:, hs].astype(BF16)
    s = jnp.where(valid, _dot_nt(q, k) + bias_ref[hh, 0], NEG_INF)
    p = jnp.exp(s - jnp.max(s, axis=-1, keepdims=True))
    p = p / jnp.sum(p, axis=-1, keepdims=True)
    return q, k, p


def _att_specs(S):
    nq = D_MODEL // 128
    q_spec = pl.BlockSpec((ATT_TQ, 128), lambda p, i: (i, p))
    k_spec = pl.BlockSpec((S, 128), lambda p, i: (0, nq + p))
    v_spec = pl.BlockSpec((S, 128), lambda p, i: (0, 2 * nq + p))
    b_spec = pl.BlockSpec((2, 1, ATT_TQ, ATT_KW), lambda p, i: (p, jnp.minimum(i, 2), 0, 0))
    return q_spec, k_spec, v_spec, b_spec


def attn_fwd(qkv, bias, name):
    S = qkv.shape[0]
    q_spec, k_spec, v_spec, b_spec = _att_specs(S)

    def kern(q_ref, k_ref, v_ref, bias_ref, o_ref):
        i = pl.program_id(1)
        ws = _att_window(i)
        valid = _att_valid(i, ws)
        kw = k_ref[pl.ds(ws, ATT_KW), :]
        vw = v_ref[pl.ds(ws, ATT_KW), :]
        outs = []
        for hh in range(2):
            _, _, p = _att_probs(q_ref, kw, bias_ref, valid, hh)
            outs.append(_dot(p.astype(BF16), vw[:, hh * ATT_HD:(hh + 1) * ATT_HD].astype(BF16)))
        o_ref[...] = jnp.concatenate(outs, axis=1).astype(BF16)

    return pl.pallas_call(
        kern, name=name, grid=(ATT_HEADS // 2, S // ATT_TQ),
        in_specs=[q_spec, k_spec, v_spec, b_spec],
        out_specs=pl.BlockSpec((ATT_TQ, 128), lambda p, i: (i, p)),
        out_shape=jax.ShapeDtypeStruct((S, D_MODEL), BF16),
        compiler_params=_cp(("parallel", "arbitrary")),
    )(qkv, qkv, qkv, bias)


def attn_bwd(qkv, bias, do, name):
    S = qkv.shape[0]
    q_spec, k_spec, v_spec, b_spec = _att_specs(S)
    slab = pl.BlockSpec((S, 128), lambda p, i: (0, p))

    def kern(q_ref, k_ref, v_ref, bias_ref, do_ref, dq_ref, dk_ref, dv_ref, db_ref):
        i = pl.program_id(1)

        @pl.when(i == 0)
        def _():
            dk_ref[...] = jnp.zeros_like(dk_ref)
            dv_ref[...] = jnp.zeros_like(dv_ref)
            db_ref[...] = jnp.zeros_like(db_ref)
        ws = _att_window(i)
        valid = _att_valid(i, ws)
        win = pl.ds(ws, ATT_KW)
        kw = k_ref[win, :]
        vw = v_ref[win, :]
        o_cls = jnp.minimum(i, 2)
        dqs, dks, dvs = [], [], []
        for hh in range(2):
            hs = slice(hh * ATT_HD, (hh + 1) * ATT_HD)
            q, k, p = _att_probs(q_ref, kw, bias_ref, valid, hh)
            do_h = do_ref[:, hs]
            dp = _dot_nt(do_h, vw[:, hs].astype(BF16))
            ds = p * (dp - jnp.sum(p * dp, axis=-1, keepdims=True))
            db_ref[hh, o_cls] += ds
            ds_b = ds.astype(BF16)
            dqs.append(_dot(ds_b, k) * ATT_SCALE)
            dks.append(_dot_tn(ds_b, q))
            dvs.append(_dot_tn(p.astype(BF16), do_h))
        dq_ref[...] = jnp.concatenate(dqs, axis=1)
        dk_ref[win, :] += jnp.concatenate(dks, axis=1)
        dv_ref[win, :] += jnp.concatenate(dvs, axis=1)

    return pl.pallas_call(
        kern, name=name, grid=(ATT_HEADS // 2, S // ATT_TQ),
        in_specs=[q_spec, k_spec, v_spec, b_spec, pl.BlockSpec((ATT_TQ, 128), lambda p, i: (i, p))],
        out_specs=[pl.BlockSpec((ATT_TQ, 128), lambda p, i: (i, p)), slab, slab,
                   pl.BlockSpec((2, 3, ATT_TQ, ATT_KW), lambda p, i: (p, 0, 0, 0))],
        out_shape=[jax.ShapeDtypeStruct((S, D_MODEL), F32)] * 3
        + [jax.ShapeDtypeStruct((ATT_HEADS, 3, ATT_TQ, ATT_KW), F32)],
        compiler_params=_cp(("parallel", "arbitrary")),
    )(qkv, qkv, qkv, bias, do)


_ATT_DIAGS = ATT_TQ + ATT_KW - 1


def _att_rel_index():
    m = jnp.arange(_ATT_DIAGS)[None, :]
    off = jnp.array([0, ATT_TQ, 2 * ATT_TQ])[:, None]
    return jnp.clip(off + (ATT_TQ - 1) - m, -MAX_REL, MAX_REL) + MAX_REL


def att_bias_tiles(rel_bias):
    tab = jnp.take(rel_bias, _att_rel_index(), axis=1)
    wide = jnp.broadcast_to(tab[:, :, None, :], (ATT_HEADS, 3, ATT_TQ + 1, _ATT_DIAGS))
    flat = wide.reshape(ATT_HEADS, 3, -1)[:, :, :ATT_TQ * (_ATT_DIAGS + 1)]
    skew = flat.reshape(ATT_HEADS, 3, ATT_TQ, _ATT_DIAGS + 1)[:, :, :, :ATT_KW]
    return skew[:, :, ::-1, :]


def att_bias_grad(dbt, name):
    x = dbt[:, :, ::-1, :]
    x = jnp.pad(x, ((0, 0), (0, 0), (0, 0), (0, _ATT_DIAGS + 1 - ATT_KW)))
    flat = x.reshape(ATT_HEADS, 3, -1)
    flat = jnp.pad(flat, ((0, 0), (0, 0), (0, (ATT_TQ + 1) * _ATT_DIAGS - flat.shape[-1])))
    skew = flat.reshape(ATT_HEADS * 3, ATT_TQ + 1, _ATT_DIAGS)
    width = _ATT_DIAGS + 1
    skew = jnp.pad(skew, ((0, 0), (0, 7), (0, 1)))
    rows = skew.shape[1]

    def kern(s_ref, o_ref):
        o_ref[...] = jnp.sum(s_ref[...], axis=1)

    diag = pl.pallas_call(
        kern, name=name + "_diag", grid=(ATT_HEADS * 3 // 8,),
        in_specs=[pl.BlockSpec((8, rows, width), lambda i: (i, 0, 0))],
        out_specs=pl.BlockSpec((8, width), lambda i: (i, 0)),
        out_shape=jax.ShapeDtypeStruct((ATT_HEADS * 3, width), F32),
        compiler_params=_cp(("parallel",)),
    )(skew)
    diag = diag.reshape(ATT_HEADS, 3 * width)
    idx = jnp.pad(_att_rel_index(), ((0, 0), (0, 1)), constant_values=-1).reshape(-1)
    onehot = (idx[:, None] == jnp.arange(384)[None, :]).astype(F32)
    return mm_f32(diag, onehot, name + "_bins")[:, :N_REL]


def _me():
    return lax.axis_index("x"), lax.axis_index("y"), lax.axis_index("c")


def _other_chips(x, y):
    return [(1 - x, y), (x, 1 - y), (1 - x, 1 - y)]


def all_gather8(x_shard, name):
    m_per, n = x_shard.shape

    def body(x_ref, out_ref, send_sems, recv_sems, local_sem):
        x, y, c = _me()
        me, sibling = (x, y, c), (x, y, 1 - c)
        chips = _other_chips(x, y)

        def rows(px, py, pc):
            return out_ref.at[pl.ds((4 * px + 2 * py + pc) * m_per, m_per), :]

        def copy(k, block, to, src=None):
            return pltpu.make_async_remote_copy(
                src_ref=rows(*block) if src is None else src, dst_ref=rows(*block),
                send_sem=send_sems.at[k], recv_sem=recv_sems.at[k], device_id=to, device_id_type=MESH)

        mine = pltpu.make_async_copy(x_ref, rows(*me), local_sem)
        mine.start()
        first = [copy(0, me, sibling, src=x_ref)]
        first += [copy(1 + j, me, (*chip, c), src=x_ref) for j, chip in enumerate(chips)]
        for cp in first:
            cp.start()
        passed = [copy(4 + j, (*chip, c), sibling) for j, chip in enumerate(chips)]
        for j, chip in enumerate(chips):
            copy(1 + j, (*chip, c), me).wait_recv()
            passed[j].start()
        copy(0, sibling, me).wait_recv()
        for j, chip in enumerate(chips):
            copy(4 + j, (*chip, 1 - c), me).wait_recv()
        for cp in first + passed:
            cp.wait_send()
        mine.wait()

    return pl.pallas_call(
        body, name=name,
        out_shape=jax.ShapeDtypeStruct((N_DEV * m_per, n), x_shard.dtype),
        in_specs=[pl.BlockSpec(memory_space=pltpu.VMEM)],
        out_specs=pl.BlockSpec(memory_space=pltpu.VMEM),
        scratch_shapes=[pltpu.SemaphoreType.DMA((7,)), pltpu.SemaphoreType.DMA((7,)), pltpu.SemaphoreType.DMA],
        compiler_params=pltpu.CompilerParams(vmem_limit_bytes=VMEM_LIMIT),
    )(x_shard)


def all_gather_chips(w, name):
    R, C = w.shape
    half = R // 2

    def body(w_ref, out_ref, send_sems, recv_sems, local_sem):
        x, y, c = _me()
        sibling = (x, y, 1 - c)
        chips = _other_chips(x, y)
        my_half = pl.ds(c * half, half)
        sib_half = pl.ds((1 - c) * half, half)

        def part(chip, rows):
            return out_ref.at[2 * chip[0] + chip[1], rows, :]

        def copy(k, src, dst, to):
            return pltpu.make_async_remote_copy(src_ref=src, dst_ref=dst, send_sem=send_sems.at[k],
                                                recv_sem=recv_sems.at[k], device_id=to, device_id_type=MESH)

        mine = pltpu.make_async_copy(w_ref, out_ref.at[2 * x + y], local_sem)
        mine.start()
        first = [copy(j, w_ref.at[my_half, :], part((x, y), my_half), (*chip, c)) for j, chip in enumerate(chips)]
        for cp in first:
            cp.start()
        passed = [copy(3 + j, part(chip, my_half), part(chip, my_half), sibling) for j, chip in enumerate(chips)]
        for j, chip in enumerate(chips):
            copy(j, part(chip, my_half), part(chip, my_half), (*chip, c)).wait_recv()
            passed[j].start()
        for j, chip in enumerate(chips):
            copy(3 + j, part(chip, sib_half), part(chip, sib_half), sibling).wait_recv()
        for cp in first + passed:
            cp.wait_send()
        mine.wait()

    return pl.pallas_call(
        body, name=name,
        out_shape=jax.ShapeDtypeStruct((N_CHIPS, R, C), w.dtype),
        in_specs=[pl.BlockSpec(memory_space=pl.ANY)],
        out_specs=pl.BlockSpec(memory_space=pl.ANY),
        scratch_shapes=[pltpu.SemaphoreType.DMA((6,)), pltpu.SemaphoreType.DMA((6,)), pltpu.SemaphoreType.DMA],
    )(w)


def swap_halves(g, name):
    n, R, C = g.shape
    half = R // 2

    def body(g_ref, out_ref, send_sem, recv_sem):
        x, y, c = _me()
        cp = pltpu.make_async_remote_copy(
            src_ref=g_ref.at[:, pl.ds((1 - c) * half, half), :], dst_ref=out_ref,
            send_sem=send_sem, recv_sem=recv_sem, device_id=(x, y, 1 - c), device_id_type=MESH)
        cp.start()
        cp.wait()

    return pl.pallas_call(
        body, name=name, out_shape=jax.ShapeDtypeStruct((n, half, C), g.dtype),
        in_specs=[pl.BlockSpec(memory_space=pl.ANY)], out_specs=pl.BlockSpec(memory_space=pl.ANY),
        scratch_shapes=[pltpu.SemaphoreType.DMA, pltpu.SemaphoreType.DMA],
    )(g)


def add_half(g, r1, c_idx, name):
    n, R, C = g.shape
    half = R // 2
    tr = _rows_block(half, C)
    nbh = half // tr

    def kern(c_ref, g_ref, r_ref, o_ref):
        o_ref[...] = g_ref[...] + r_ref[...]

    return pl.pallas_call(
        kern, name=name,
        grid_spec=pltpu.PrefetchScalarGridSpec(
            num_scalar_prefetch=1, grid=(n, nbh),
            in_specs=[pl.BlockSpec((1, tr, C), lambda d, r, c_ref: (d, c_ref[0] * nbh + r, 0)),
                      pl.BlockSpec((1, tr, C), lambda d, r, c_ref: (d, r, 0))],
            out_specs=pl.BlockSpec((1, tr, C), lambda d, r, c_ref: (d, r, 0))),
        out_shape=jax.ShapeDtypeStruct((n, half, C), F32),
        compiler_params=_cp(("parallel", "parallel")),
    )(c_idx, g, r1)


def exchange_chips(p, name):
    n, H, C = p.shape

    def body(p_ref, out_ref, send_sems, recv_sems):
        x, y, c = _me()
        chips = _other_chips(x, y)
        cps = [pltpu.make_async_remote_copy(
            src_ref=p_ref.at[2 * chip[0] + chip[1]], dst_ref=out_ref.at[j],
            send_sem=send_sems.at[j], recv_sem=recv_sems.at[j], device_id=(*chip, c), device_id_type=MESH)
            for j, chip in enumerate(chips)]
        for cp in cps:
            cp.start()
        for cp in cps:
            cp.wait()

    return pl.pallas_call(
        body, name=name, out_shape=jax.ShapeDtypeStruct((3, H, C), p.dtype),
        in_specs=[pl.BlockSpec(memory_space=pl.ANY)], out_specs=pl.BlockSpec(memory_space=pl.ANY),
        scratch_shapes=[pltpu.SemaphoreType.DMA((3,)), pltpu.SemaphoreType.DMA((3,))],
    )(p)


def add_chips(p, r2, chip_idx, name):
    n, H, C = p.shape
    tr = _rows_block(H, C)

    def kern(s_ref, p_ref, r_ref, o_ref):
        o_ref[...] = ((p_ref[0] + r_ref[0]) + r_ref[1]) + r_ref[2]

    return pl.pallas_call(
        kern, name=name,
        grid_spec=pltpu.PrefetchScalarGridSpec(
            num_scalar_prefetch=1, grid=(H // tr,),
            in_specs=[pl.BlockSpec((1, tr, C), lambda r, s_ref: (s_ref[0], r, 0)),
                      pl.BlockSpec((3, tr, C), lambda r, s_ref: (0, r, 0))],
            out_specs=pl.BlockSpec((tr, C), lambda r, s_ref: (r, 0))),
        out_shape=jax.ShapeDtypeStruct((H, C), F32),
        compiler_params=_cp(("parallel",)),
    )(chip_idx, p, r2)


def join_halves(s, name):
    H, C = s.shape

    def body(s_ref, out_ref, send_sem, recv_sem, local_sem):
        x, y, c = _me()
        mine = pl.ds(c * H, H)
        local = pltpu.make_async_copy(s_ref, out_ref.at[mine, :], local_sem)
        local.start()
        cp = pltpu.make_async_remote_copy(
            src_ref=s_ref, dst_ref=out_ref.at[mine, :], send_sem=send_sem, recv_sem=recv_sem,
            device_id=(x, y, 1 - c), device_id_type=MESH)
        cp.start()
        cp.wait_send()
        pltpu.make_async_remote_copy(
            src_ref=s_ref, dst_ref=out_ref.at[pl.ds((1 - c) * H, H), :], send_sem=send_sem, recv_sem=recv_sem,
            device_id=(x, y, 1 - c), device_id_type=MESH).wait_recv()
        local.wait()

    return pl.pallas_call(
        body, name=name, out_shape=jax.ShapeDtypeStruct((2 * H, C), s.dtype),
        in_specs=[pl.BlockSpec(memory_space=pl.ANY)], out_specs=pl.BlockSpec(memory_space=pl.ANY),
        scratch_shapes=[pltpu.SemaphoreType.DMA, pltpu.SemaphoreType.DMA, pltpu.SemaphoreType.DMA],
    )(s)


def reduce_scatter_chips(g, c_idx, chip_idx, name):
    r1 = swap_halves(g, name + "_swap")
    p = add_half(g, r1, c_idx, name + "_add2")
    r2 = exchange_chips(p, name + "_xchg")
    s = add_chips(p, r2, chip_idx, name + "_add4")
    return join_halves(s, name + "_join")


BIG = (("gla_w_in", 2, 1024, GLA_IN, 1), ("gla_w_out", 2, 1024, 1024, 0),
       ("att_w_in", 2, 1024, 3072, 1), ("att_w_out", 2, 1024, 1024, 0),
       ("ff_w1", 4, 1024, D_FF, 1), ("ff_w2", 4, D_FF, 1024, 0))
PACK_ALIGN = 32


def _pack_rows():
    return sum(L * K * N // N_CHIPS // PACK_W for _, L, K, N, _ in BIG)


def _pack_rows_padded():
    r = _pack_rows()
    return (r + PACK_ALIGN - 1) // PACK_ALIGN * PACK_ALIGN


def pack_shards(shards, dtype):
    parts = [shards[n].astype(dtype).reshape(-1, PACK_W) for n, *_ in BIG]
    pad = _pack_rows_padded() - _pack_rows()
    parts.append(jnp.zeros((pad, PACK_W), dtype))
    return jnp.concatenate(parts, axis=0)


def unpack_shards(buf):
    out, r = {}, 0
    for n, L, K, N, ax in BIG:
        nr = L * K * N // N_CHIPS // PACK_W
        shape = (L, K, N // N_CHIPS) if ax == 1 else (L, K // N_CHIPS, N)
        out[n] = buf[r:r + nr].reshape(shape)
        r += nr
    return out


def unpack_full(buf4):
    out, r = {}, 0
    for n, L, K, N, ax in BIG:
        nr = L * K * N // N_CHIPS // PACK_W
        blk = buf4[:, r:r + nr]
        if ax == 1:
            out[n] = blk.reshape(N_CHIPS, L, K, N // N_CHIPS).transpose(1, 2, 0, 3).reshape(L, K, N)
        else:
            out[n] = blk.reshape(N_CHIPS, L, K // N_CHIPS, N).transpose(1, 0, 2, 3).reshape(L, K, N)
        r += nr
    return out


def pack_full(grads):
    parts = []
    for n, L, K, N, ax in BIG:
        g = grads[n]
        if ax == 1:
            g = g.reshape(L, K, N_CHIPS, N // N_CHIPS).transpose(2, 0, 1, 3)
        else:
            g = g.reshape(L, N_CHIPS, K // N_CHIPS, N).transpose(1, 0, 2, 3)
        parts.append(g.reshape(N_CHIPS, -1, PACK_W))
    pad = _pack_rows_padded() - _pack_rows()
    parts.append(jnp.zeros((N_CHIPS, pad, PACK_W), F32))
    return jnp.concatenate(parts, axis=1)


def local_step(x, target, mods, W, small):
    S, D = x.shape
    row = lambda v: v.reshape(1, -1)
    saved = []
    bias_tiles = [att_bias_tiles(small["att_rel_bias"][j]) for j in range(2)]
    wgk_p = [jnp.pad(small["gla_w_gk2"][j], ((0, 128 - GLA_RANK), (0, 0))).astype(BF16) for j in range(2)]

    for i in range(DEPTH):
        j = i // 2
        sh1, sc1, g1, sh2, sc2, g2 = (row(mods[i, k]) for k in range(6))
        u1 = modulate(x, sc1, sh1, f"mod1_{i}")
        if i % 2 == 0:
            proj = mm_plain(u1, W["gla_w_in"][j], f"gla_in_{i}")
            zmix, states = gla_fwd(proj, wgk_p[j], row(small["gla_b_gk"][j]), row(small["gla_g_norm"][j]),
                                   f"gla_fwd_{i}")
            w_out = W["gla_w_out"][j]
        else:
            proj = mm_plain(u1, W["att_w_in"][j], f"att_in_{i}", bias=row(small["att_b_in"][j]))
            zmix = attn_fwd(proj, bias_tiles[j], f"att_fwd_{i}")
            states = None
            w_out = W["att_w_out"][j]
        y1, x_mid = mm_down_ln(zmix, w_out, x, 1.0 + g1, row(small["ln_g"][i, 0]), row(small["ln_b"][i, 0]),
                               f"mix_out_{i}")
        u2 = modulate(x_mid, sc2, sh2, f"mod2_{i}")
        h, act = mm_plain(u2, W["ff_w1"][i], f"ff_up_{i}", mode="mlp_up")
        y2, x_out = mm_down_ln(act, W["ff_w2"][i], x_mid, 1.0 + g2, row(small["ln_g"][i, 1]),
                               row(small["ln_b"][i, 1]), f"ff_out_{i}")
        saved.append(dict(x_in=x, u1=u1, proj=proj, zmix=zmix, states=states, y1=y1, x_mid=x_mid, u2=u2, h=h,
                          act=act, y2=y2))
        x = x_out

    dx, sq = loss_head(x, target, "loss_head")

    gW = {n: [None] * L for n, L, *_ in BIG}
    g_small = dict(ln_g=[None] * DEPTH, ln_b=[None] * DEPTH, gla_w_gk2=[None] * 2, gla_b_gk=[None] * 2,
                   gla_g_norm=[None] * 2, att_b_in=[None] * 2, att_rel_bias=[None] * 2)
    dmods = [None] * DEPTH

    for i in reversed(range(DEPTH)):
        j = i // 2
        sv = saved[i]
        sh1, sc1, g1, sh2, sc2, g2 = (row(mods[i, k]) for k in range(6))
        dz2, dy2, s_ln2 = ln_bwd(dx, sv["x_mid"], sv["y2"], 1.0 + g2, row(small["ln_g"][i, 1]), f"ln2_bwd_{i}")
        dh = mm_plain(dy2, W["ff_w2"][i].T, f"ff_dn_{i}", mode="mlp_dn", h=sv["h"])
        gW["ff_w2"][i] = mm_w(sv["act"], dy2, f"ff_w2g_{i}")
        gW["ff_w1"][i] = mm_w(sv["u2"], dh, f"ff_w1g_{i}")
        dx_mid, s_m2 = mm_down_comb(dh, W["ff_w1"][i].T, dz2, sv["x_mid"], 1.0 + sc2, f"ff_dx_{i}")
        dz1, dy1, s_ln1 = ln_bwd(dx_mid, sv["x_in"], sv["y1"], 1.0 + g1, row(small["ln_g"][i, 0]), f"ln1_bwd_{i}")
        if i % 2 == 0:
            gW["gla_w_out"][j] = mm_w(sv["zmix"], dy1, f"gla_wog_{i}")
            dzg = mm_plain(dy1, W["gla_w_out"][j].T, f"gla_dz_{i}")
            dproj, dwgk, dbgk, dgn = gla_bwd(sv["proj"], sv["states"], dzg, wgk_p[j], row(small["gla_b_gk"][j]),
                                             row(small["gla_g_norm"][j]), f"gla_bwd_{i}")
            g_small["gla_w_gk2"][j] = dwgk[:GLA_RANK]
            g_small["gla_b_gk"][j] = dbgk[0]
            g_small["gla_g_norm"][j] = dgn[0].reshape(GLA_HEADS, GLA_DV_HEAD)
            gW["gla_w_in"][j] = mm_w(sv["u1"], dproj, f"gla_wig_{i}")[:, :GLA_IN]
            w_in_t = W["gla_w_in"][j].T
        else:
            gW["att_w_out"][j] = mm_w(sv["zmix"], dy1, f"att_wog_{i}")
            do = mm_plain(dy1, W["att_w_out"][j].T, f"att_do_{i}").astype(BF16)
            dq, dk, dv, dbt = attn_bwd(sv["proj"], bias_tiles[j], do, f"att_bwd_{i}")
            dproj = jnp.concatenate([dq, dk, dv], axis=1).astype(BF16)
            g_small["att_rel_bias"][j] = att_bias_grad(dbt, f"att_bias_{i}")
            g_small["att_b_in"][j] = colsum(dproj, f"att_bin_{i}")[0]
            gW["att_w_in"][j] = mm_w(sv["u1"], dproj, f"att_wig_{i}")
            w_in_t = W["att_w_in"][j].T
        dx, s_m1 = mm_down_comb(dproj, w_in_t, dz1, sv["x_in"], 1.0 + sc1, f"mix_dx_{i}")
        g_small["ln_g"][i] = jnp.stack([s_ln1[0], s_ln2[0]])
        g_small["ln_b"][i] = jnp.stack([s_ln1[1], s_ln2[1]])
        dmods[i] = jnp.stack([s_m1[1], s_m1[0], s_ln1[2], s_m2[1], s_m2[0], s_ln2[2]])

    gW = {n: jnp.stack(v) for n, v in gW.items()}
    g_small = {n: jnp.stack(v) for n, v in g_small.items()}
    return sq, dx, gW, jnp.stack(dmods), g_small


SMALL_SHARDED = (("ln_g", (4, 2, 256)), ("ln_b", (4, 2, 256)), ("gla_g_norm", (2, 4, 64)),
                 ("gla_w_gk2", (2, 16, 128)), ("att_b_in", (2, 768)))
SMALL_FULL = dict(ln_g=(4, 2, 1024), ln_b=(4, 2, 1024), gla_g_norm=(2, 4, 256), gla_w_gk2=(2, 16, 512),
                  att_b_in=(2, 3072), gla_b_gk=(2, 512), att_rel_bias=(2, 16, 257))
SMALL_GRAD_ORDER = ("ln_g", "ln_b", "gla_g_norm", "gla_w_gk2", "att_b_in", "gla_b_gk", "att_rel_bias")


def _pack_small(arrs, rows_total):
    parts = []
    for a in arrs:
        flat = a.reshape(-1)
        pad = (-flat.shape[0]) % PACK_W
        parts.append(jnp.pad(flat, (0, pad)).reshape(-1, PACK_W))
    buf = jnp.concatenate(parts, axis=0)
    return jnp.pad(buf, ((0, rows_total - buf.shape[0]), (0, 0)))


def _unpack_small(buf, shapes):
    out, r = [], 0
    for shp in shapes:
        n = 1
        for s in shp:
            n *= s
        nr = (n + PACK_W - 1) // PACK_W
        out.append(buf[..., r:r + nr, :].reshape(buf.shape[:-2] + (nr * PACK_W,))[..., :n].reshape(buf.shape[:-2] + shp))
        r += nr
    return out


def _unshard_last(g4):
    nd = g4.ndim
    perm = tuple(range(1, nd - 1)) + (0, nd - 1)
    t = g4.transpose(perm)
    return t.reshape(t.shape[:-2] + (-1,))


def _shard_last(full, s):
    n = full.shape[-1] // N_CHIPS
    return lax.dynamic_slice_in_dim(full, s * n, n, axis=full.ndim - 1)


WEIGHT_NAMES = ("w_ada", "b_ada", "ln_g", "ln_b", "gla_w_in", "gla_w_gk2", "gla_b_gk", "gla_g_norm", "gla_w_out",
                "att_w_in", "att_b_in", "att_rel_bias", "att_w_out", "ff_w1", "ff_w2")


def kernel(x, c, w_ada, b_ada, ln_g, ln_b, gla_w_in, gla_w_gk2, gla_b_gk, gla_g_norm, gla_w_out, att_w_in, att_b_in, att_rel_bias, att_w_out, ff_w1, ff_w2, loss_target, m_w_ada, m_b_ada, m_ln_g, m_ln_b, m_gla_w_in, m_gla_w_gk2, m_gla_b_gk, m_gla_g_norm, m_gla_w_out, m_att_w_in, m_att_b_in, m_att_rel_bias, m_att_w_out, m_ff_w1, m_ff_w2, v_w_ada, v_b_ada, v_ln_g, v_ln_b, v_gla_w_in, v_gla_w_gk2, v_gla_b_gk, v_gla_g_norm, v_gla_w_out, v_att_w_in, v_att_b_in, v_att_rel_bias, v_att_w_out, v_ff_w1, v_ff_w2):
    weights = dict(w_ada=w_ada, b_ada=b_ada, ln_g=ln_g, ln_b=ln_b, gla_w_in=gla_w_in, gla_w_gk2=gla_w_gk2,
                   gla_b_gk=gla_b_gk, gla_g_norm=gla_g_norm, gla_w_out=gla_w_out, att_w_in=att_w_in,
                   att_b_in=att_b_in, att_rel_bias=att_rel_bias, att_w_out=att_w_out, ff_w1=ff_w1, ff_w2=ff_w2)
    mom1 = dict(w_ada=m_w_ada, b_ada=m_b_ada, ln_g=m_ln_g, ln_b=m_ln_b, gla_w_in=m_gla_w_in, gla_w_gk2=m_gla_w_gk2,
                gla_b_gk=m_gla_b_gk, gla_g_norm=m_gla_g_norm, gla_w_out=m_gla_w_out, att_w_in=m_att_w_in,
                att_b_in=m_att_b_in, att_rel_bias=m_att_rel_bias, att_w_out=m_att_w_out, ff_w1=m_ff_w1, ff_w2=m_ff_w2)
    mom2 = dict(w_ada=v_w_ada, b_ada=v_b_ada, ln_g=v_ln_g, ln_b=v_ln_b, gla_w_in=v_gla_w_in, gla_w_gk2=v_gla_w_gk2,
                gla_b_gk=v_gla_b_gk, gla_g_norm=v_gla_g_norm, gla_w_out=v_gla_w_out, att_w_in=v_att_w_in,
                att_b_in=v_att_b_in, att_rel_bias=v_att_rel_bias, att_w_out=v_att_w_out, ff_w1=v_ff_w1, ff_w2=v_ff_w2)

    ax, ay, ac = lax.axis_index("x"), lax.axis_index("y"), lax.axis_index("c")
    chip = 2 * ax + ay
    dev = 2 * chip + ac
    c_idx = jnp.reshape(ac, (1,)).astype(jnp.int32)
    chip_idx = jnp.reshape(chip, (1,)).astype(jnp.int32)
    S = x.shape[1]
    x2 = x.reshape(S, D_MODEL)
    t2 = loss_target.reshape(S, D_MODEL)

    wpack = pack_shards({n: weights[n] for n, *_ in BIG}, BF16)
    W = unpack_full(all_gather_chips(wpack, "gather_weights"))
    W["gla_w_in"] = jnp.pad(W["gla_w_in"], ((0, 0), (0, 0), (0, GLA_IN_PAD - GLA_IN)))

    small_rows = 16
    spack = _pack_small([c] + [weights[n] for n, _ in SMALL_SHARDED], small_rows)
    sg = all_gather8(spack, "gather_small").reshape(N_DEV, small_rows, PACK_W)
    parts = _unpack_small(sg, [(1, D_MODEL)] + [shp for _, shp in SMALL_SHARDED])
    c_all = parts[0].reshape(N_DEV, D_MODEL)
    small = {n: _unshard_last(p[0::2]) for (n, _), p in zip(SMALL_SHARDED, parts[1:])}
    small["gla_b_gk"] = gla_b_gk
    small["att_rel_bias"] = att_rel_bias

    c_act = silu_rows(jnp.pad(c_all, ((0, 128 - N_DEV), (0, 0))), "silu_c")
    wa = w_ada.astype(BF16).transpose(1, 0, 2).reshape(D_MODEL, DEPTH * 6 * D_MODEL // N_CHIPS)
    mods_part = mm_plain(c_act, wa, "ada_fwd", tm=128)[:N_DEV]
    mg = all_gather8(mods_part, "gather_mods").reshape(N_CHIPS, 2, N_DEV, DEPTH, 6 * D_MODEL // N_CHIPS)
    mods_mine = lax.dynamic_index_in_dim(mg[:, 0], dev, axis=1, keepdims=False)
    mods = mods_mine.transpose(1, 0, 2).reshape(DEPTH, 6 * D_MODEL) + b_ada
    mods = mods.reshape(DEPTH, 6, D_MODEL)

    sq, grad_x, gW, dmods, g_small = local_step(x2, t2, mods, W, small)
    loss = lax.psum(0.5 * sq[0, 0] / D_MODEL, ("x", "y", "c"))

    g_shard = unpack_shards(reduce_scatter_chips(pack_full(gW), c_idx, chip_idx, "rs"))

    dm_flat = dmods.reshape(DEPTH, 6 * D_MODEL)
    g_rows = 80
    gpack = _pack_small([dm_flat] + [g_small[n] for n in SMALL_GRAD_ORDER], g_rows)
    gg = all_gather8(gpack, "gather_small_grads").reshape(N_DEV, g_rows, PACK_W)
    gsum = sum_over_devices(gg, "sum_small_grads")
    sums = _unpack_small(gsum, [(DEPTH, 6 * D_MODEL)] + [SMALL_FULL[n] for n in SMALL_GRAD_ORDER])
    grads = dict(b_ada=sums[0])
    for n, full in zip(SMALL_GRAD_ORDER, sums[1:]):
        grads[n] = full if n in ("gla_b_gk", "att_rel_bias") else _shard_last(full, chip)
    dm_all = _unpack_small(gg, [(DEPTH, 6 * D_MODEL)])[0]
    dm_cols = _shard_last(dm_all, chip).reshape(N_DEV, DEPTH * 6 * D_MODEL // N_CHIPS)
    dm_cols = jnp.pad(dm_cols, ((0, 128 - N_DEV), (0, 0))).astype(BF16)
    gwa = mm_w(c_act, dm_cols, "ada_bwd", ts=128)
    grads["w_ada"] = gwa.reshape(D_MODEL, DEPTH, 6 * D_MODEL // N_CHIPS).transpose(1, 0, 2)
    grads.update(g_shard)

    deltas, new_m, new_v = {}, {}, {}
    for n in WEIGHT_NAMES:
        deltas[n], new_m[n], new_v[n] = adamw(weights[n], grads[n], mom1[n], mom2[n], "adamw_" + n)

    return (loss, grad_x.reshape(1, S, D_MODEL), *[grads[n] for n in WEIGHT_NAMES], *[deltas[n] for n in WEIGHT_NAMES],
            *[new_m[n] for n in WEIGHT_NAMES], *[new_v[n] for n in WEIGHT_NAMES])
```

```python
import functools

import jax
import jax.numpy as jnp
from jax import lax
from jax.experimental import pallas as pl
from jax.experimental.pallas import tpu as pltpu

F32 = jnp.float32
BF16 = jnp.bfloat16
HIGHEST = lax.Precision.HIGHEST
MESH = pl.DeviceIdType.MESH

D_MODEL = 1024
DEPTH = 4
CHUNK = 64
GLA_HEADS = 4
GLA_DK = 512
GLA_DV = 1024
GLA_DK_HEAD = 128
GLA_DV_HEAD = 256
GLA_RANK = 16
GLA_IN = 3088
GLA_IN_PAD = 3200
GLA_LR_OFF = 3072
ATT_HEADS = 16
ATT_HD = 64
LEFT_CHUNKS = 8
MAX_REL = 128
N_REL = 257
D_FF = 4096
ALPHA = (2.0 * DEPTH) ** 0.25
LN_EPS = 1e-5
RMS_EPS = 1e-6
NEG_INF = -1e30
GLA_SCALE = GLA_DK_HEAD ** -0.5
ATT_SCALE = ATT_HD ** -0.5
ADAM_LR = 0.001
ADAM_B1 = 0.9
ADAM_B2 = 0.999
ADAM_EPS = 1e-08
ADAM_WD = 0.01
ADAM_STEP = 10

ATT_TQ = 256
ATT_KW = 768
GLA_TB = 256
VMEM_LIMIT = 56 * 1024 * 1024
N_CHIPS = 4
N_DEV = 8
PACK_W = 1024


def _dot(a, b):
    return jnp.dot(a, b, preferred_element_type=F32)


def _dot_nt(a, b):
    return lax.dot_general(a, b, (((1,), (1,)), ((), ())), preferred_element_type=F32)


def _dot_tn(a, b):
    return lax.dot_general(a, b, (((0,), (0,)), ((), ())), preferred_element_type=F32)


def _cp(sem, vmem=VMEM_LIMIT):
    return pltpu.CompilerParams(dimension_semantics=sem, vmem_limit_bytes=vmem)


def _row_spec(n):
    return pl.BlockSpec((1, n), lambda *_: (0, 0))


def _sigmoid(x):
    return 1.0 / (1.0 + jnp.exp(-x))


def _log_sigmoid(x):
    return jnp.minimum(x, 0.0) - jnp.log1p(jnp.exp(-jnp.abs(x)))


def modulate(x, sc, sh, name):
    S, D = x.shape
    tm = min(512, S)

    def kern(x_ref, sc_ref, sh_ref, u_ref):
        u_ref[...] = (x_ref[...] * (1.0 + sc_ref[...]) + sh_ref[...]).astype(BF16)

    return pl.pallas_call(
        kern, name=name, grid=(S // tm,),
        in_specs=[pl.BlockSpec((tm, D), lambda i: (i, 0)), _row_spec(D), _row_spec(D)],
        out_specs=pl.BlockSpec((tm, D), lambda i: (i, 0)),
        out_shape=jax.ShapeDtypeStruct((S, D), BF16),
        compiler_params=_cp(("parallel",)),
    )(x, sc, sh)


def loss_head(x, t, name):
    S, D = x.shape
    tm = min(512, S)

    def kern(x_ref, t_ref, dx_ref, l_ref):
        @pl.when(pl.program_id(0) == 0)
        def _():
            l_ref[...] = jnp.zeros_like(l_ref)
        e = x_ref[...] - t_ref[...]
        dx_ref[...] = e * (1.0 / D)
        l_ref[...] += jnp.sum(e * e)

    return pl.pallas_call(
        kern, name=name, grid=(S // tm,),
        in_specs=[pl.BlockSpec((tm, D), lambda i: (i, 0)), pl.BlockSpec((tm, D), lambda i: (i, 0))],
        out_specs=[pl.BlockSpec((tm, D), lambda i: (i, 0)), pl.BlockSpec((8, 128), lambda i: (0, 0))],
        out_shape=[jax.ShapeDtypeStruct((S, D), F32), jax.ShapeDtypeStruct((8, 128), F32)],
        compiler_params=_cp(("arbitrary",)),
    )(x, t)


def silu_rows(c_all, name):
    def kern(c_ref, o_ref):
        c = c_ref[...]
        o_ref[...] = (c * _sigmoid(c)).astype(BF16)

    return pl.pallas_call(kern, name=name, out_shape=jax.ShapeDtypeStruct(c_all.shape, BF16))(c_all)


def sum_over_devices(g, name):
    n, R, C = g.shape

    def kern(g_ref, o_ref):
        acc = g_ref[0]
        for d in range(1, n):
            acc = acc + g_ref[d]
        o_ref[...] = acc

    return pl.pallas_call(kern, name=name, out_shape=jax.ShapeDtypeStruct((R, C), F32))(g)


def _rows_block(R, C, budget=1 << 20):
    if R * C * 4 <= budget or R % 8:
        return R
    tr = max(8, (budget // (C * 4)) // 8 * 8)
    while R % tr:
        tr -= 8
    return tr


def adamw(w, g, m, v, name):
    shape = w.shape
    C = shape[-1]
    R = w.size // C
    w2, g2, m2, v2 = (t.reshape(R, C) for t in (w, g, m, v))
    tr = _rows_block(R, C)
    c1 = 1.0 - ADAM_B1 ** ADAM_STEP
    c2 = 1.0 - ADAM_B2 ** ADAM_STEP

    def kern(w_ref, g_ref, m_ref, v_ref, d_ref, nm_ref, nv_ref):
        gg = g_ref[...]
        nm = ADAM_B1 * m_ref[...] + (1.0 - ADAM_B1) * gg
        nv = ADAM_B2 * v_ref[...] + (1.0 - ADAM_B2) * (gg * gg)
        m_hat = nm / c1
        v_hat = nv / c2
        d_ref[...] = -ADAM_LR * (m_hat / (jnp.sqrt(v_hat) + ADAM_EPS) + ADAM_WD * w_ref[...])
        nm_ref[...] = nm
        nv_ref[...] = nv

    spec = pl.BlockSpec((tr, C), lambda i: (i, 0))
    outs = pl.pallas_call(
        kern, name=name, grid=(R // tr,),
        in_specs=[spec] * 4, out_specs=[spec] * 3,
        out_shape=[jax.ShapeDtypeStruct((R, C), F32)] * 3,
        compiler_params=_cp(("parallel",)),
    )(w2, g2, m2, v2)
    return tuple(o.reshape(shape) for o in outs)


def _tn_for(N):
    for tn in (1024, 768, 640, 512, 384, 256, 128):
        if N % tn == 0:
            return tn
    return N


def mm_plain(a, b3, layer, name, *, mode="f32", nt=False, bias=None, h=None, tm=512):
    M, K = a.shape
    N = b3.shape[1] if nt else b3.shape[2]
    tm = min(tm, M)
    tn = _tn_for(N)
    a_spec = pl.BlockSpec((tm, K), lambda i, j: (i, 0))
    if nt:
        b_spec = pl.BlockSpec((None, tn, K), lambda i, j: (layer, j, 0))
        dot = _dot_nt
    else:
        b_spec = pl.BlockSpec((None, K, tn), lambda i, j: (layer, 0, j))
        dot = _dot
    o_spec = pl.BlockSpec((tm, tn), lambda i, j: (i, j))
    ins, in_specs = [a, b3], [a_spec, b_spec]

    if mode in ("f32", "bf16"):
        odt = F32 if mode == "f32" else BF16
        if bias is not None:
            ins.append(bias)
            in_specs.append(pl.BlockSpec((1, tn), lambda i, j: (0, j)))

            def kern(a_ref, b_ref, bias_ref, o_ref):
                o_ref[...] = (dot(a_ref[...], b_ref[...]) + bias_ref[...]).astype(odt)
        else:
            def kern(a_ref, b_ref, o_ref):
                o_ref[...] = dot(a_ref[...], b_ref[...]).astype(odt)
        out_specs, out_shape = o_spec, jax.ShapeDtypeStruct((M, N), odt)
    elif mode == "mlp_up":
        def kern(a_ref, b_ref, h_ref, act_ref):
            acc = dot(a_ref[...], b_ref[...])
            h_ref[...] = acc
            r = jnp.maximum(acc, 0.0)
            act_ref[...] = (r * r).astype(BF16)
        out_specs = [o_spec, o_spec]
        out_shape = [jax.ShapeDtypeStruct((M, N), F32), jax.ShapeDtypeStruct((M, N), BF16)]
    elif mode == "mlp_dn":
        ins.append(h)
        in_specs.append(o_spec)

        def kern(a_ref, b_ref, h_ref, o_ref):
            acc = dot(a_ref[...], b_ref[...])
            o_ref[...] = (acc * (2.0 * jnp.maximum(h_ref[...], 0.0))).astype(BF16)
        out_specs, out_shape = o_spec, jax.ShapeDtypeStruct((M, N), BF16)
    else:
        raise ValueError(mode)

    return pl.pallas_call(
        kern, name=name, grid=(M // tm, N // tn), in_specs=in_specs, out_specs=out_specs,
        out_shape=out_shape, compiler_params=_cp(("parallel", "parallel")),
    )(*ins)


def mm_down_ln(a, b3, layer, x_in, gate1p, ln_g, ln_b, name, *, tm=256):
    M, K = a.shape
    D = b3.shape[2]
    tm = min(tm, M)

    def kern(a_ref, b_ref, x_ref, gp_ref, lg_ref, lb_ref, y_ref, xo_ref):
        y = _dot(a_ref[...], b_ref[...])
        y_ref[...] = y
        z = ALPHA * x_ref[...] + gp_ref[...] * y
        mu = jnp.mean(z, axis=-1, keepdims=True)
        zc = z - mu
        var = jnp.mean(zc * zc, axis=-1, keepdims=True)
        xo_ref[...] = (zc * lax.rsqrt(var + LN_EPS)) * lg_ref[...] + lb_ref[...]

    tile = pl.BlockSpec((tm, D), lambda i: (i, 0))
    return pl.pallas_call(
        kern, name=name, grid=(M // tm,),
        in_specs=[pl.BlockSpec((tm, K), lambda i: (i, 0)), pl.BlockSpec((None, K, D), lambda i: (layer, 0, 0)), tile,
                  _row_spec(D), _row_spec(D), _row_spec(D)],
        out_specs=[tile, tile],
        out_shape=[jax.ShapeDtypeStruct((M, D), F32)] * 2,
        compiler_params=_cp(("parallel",)),
    )(a, b3, x_in, gate1p, ln_g, ln_b)


def mm_down_comb(a, b3, layer, dz, x_in, sc1p, name, *, parts=1, tm=256):
    D, K = b3.shape[1], b3.shape[2]
    M = a.shape[-2]
    kp = K // parts
    tm = min(tm, M)

    def kern(*refs):
        a_refs = refs[:parts]
        b_ref, dz_ref, x_ref, sp_ref, dx_ref, s_ref = refs[parts:]

        @pl.when(pl.program_id(0) == 0)
        def _():
            s_ref[...] = jnp.zeros_like(s_ref)
        if parts == 1:
            du = _dot_nt(a_refs[0][...], b_ref[...])
        else:
            du = _dot_nt(a_refs[0][...], b_ref[:, 0:kp])
            for p in range(1, parts):
                du = du + _dot_nt(a_refs[p][...], b_ref[:, p * kp:(p + 1) * kp])
        dx_ref[...] = ALPHA * dz_ref[...] + du * sp_ref[...]
        s_ref[0:1, :] += jnp.sum(du * x_ref[...], axis=0, keepdims=True)
        s_ref[1:2, :] += jnp.sum(du, axis=0, keepdims=True)

    tile = pl.BlockSpec((tm, D), lambda i: (i, 0))
    if parts == 1:
        a_ins, a_specs = [a], [pl.BlockSpec((tm, K), lambda i: (i, 0))]
    else:
        a_ins = [a] * parts
        a_specs = [pl.BlockSpec((None, tm, kp), functools.partial(lambda i, p: (p, i, 0), p=p)) for p in range(parts)]
    return pl.pallas_call(
        kern, name=name, grid=(M // tm,),
        in_specs=a_specs + [pl.BlockSpec((None, D, K), lambda i: (layer, 0, 0)), tile, tile, _row_spec(D)],
        out_specs=[tile, pl.BlockSpec((8, D), lambda i: (0, 0))],
        out_shape=[jax.ShapeDtypeStruct((M, D), F32), jax.ShapeDtypeStruct((8, D), F32)],
        compiler_params=_cp(("arbitrary",)),
    )(*a_ins, b3, dz, x_in, sc1p)


def mm_w(a, b, name, *, ts=1024, tk=512, chips_out=False, b_parts=1, tn=None):
    S, K = a.shape
    npart = b.shape[-1]
    N = npart * b_parts
    ts = min(ts, S)
    tk = min(tk, K)
    n_chip = N // N_CHIPS
    if tn is None:
        tn = _tn_for(n_chip if chips_out else npart)
    assert npart % tn == 0 and (not chips_out or n_chip % tn == 0)

    def kern(a_ref, b_ref, o_ref):
        @pl.when(pl.program_id(2) == 0)
        def _():
            o_ref[...] = jnp.zeros_like(o_ref)
        o_ref[...] += _dot_tn(a_ref[...], b_ref[...])

    if b_parts == 1:
        b_spec = pl.BlockSpec((ts, tn), lambda k, n, s: (s, n))
    else:
        per = npart // tn
        b_spec = pl.BlockSpec((None, ts, tn), lambda k, n, s: (n // per, s, n % per))
    if chips_out:
        per_chip = n_chip // tn
        o_spec = pl.BlockSpec((None, tk, tn), lambda k, n, s: (n // per_chip, k, n % per_chip))
        out_shape = jax.ShapeDtypeStruct((N_CHIPS, K, n_chip), F32)
    else:
        o_spec = pl.BlockSpec((tk, tn), lambda k, n, s: (k, n))
        out_shape = jax.ShapeDtypeStruct((K, N), F32)
    return pl.pallas_call(
        kern, name=name, grid=(K // tk, N // tn, S // ts),
        in_specs=[pl.BlockSpec((ts, tk), lambda k, n, s: (s, k)), b_spec],
        out_specs=o_spec, out_shape=out_shape,
        compiler_params=_cp(("parallel", "parallel", "arbitrary")),
    )(a, b)


def mm_f32(a, b, name):
    def kern(a_ref, b_ref, o_ref):
        o_ref[...] = jnp.dot(a_ref[...], b_ref[...], precision=HIGHEST, preferred_element_type=F32)

    return pl.pallas_call(kern, name=name, out_shape=jax.ShapeDtypeStruct((a.shape[0], b.shape[1]), F32),
                          compiler_params=pltpu.CompilerParams(vmem_limit_bytes=VMEM_LIMIT))(a, b)


def ln_bwd(dxo, x_in, y, gate1p, ln_g, name, *, tm=256):
    S, D = dxo.shape
    tm = min(tm, S)

    def kern(dxo_ref, x_ref, y_ref, gp_ref, lg_ref, dz_ref, dy_ref, s_ref):
        @pl.when(pl.program_id(0) == 0)
        def _():
            s_ref[...] = jnp.zeros_like(s_ref)
        dxo_t = dxo_ref[...]
        yv = y_ref[...]
        z = ALPHA * x_ref[...] + gp_ref[...] * yv
        mu = jnp.mean(z, axis=-1, keepdims=True)
        zc = z - mu
        var = jnp.mean(zc * zc, axis=-1, keepdims=True)
        rstd = lax.rsqrt(var + LN_EPS)
        xhat = zc * rstd
        dxh = dxo_t * lg_ref[...]
        dz = rstd * (dxh - jnp.mean(dxh, axis=-1, keepdims=True)
                     - xhat * jnp.mean(dxh * xhat, axis=-1, keepdims=True))
        dz_ref[...] = dz
        dy_ref[...] = (gp_ref[...] * dz).astype(BF16)
        s_ref[0:1, :] += jnp.sum(dxo_t * xhat, axis=0, keepdims=True)
        s_ref[1:2, :] += jnp.sum(dxo_t, axis=0, keepdims=True)
        s_ref[2:3, :] += jnp.sum(dz * yv, axis=0, keepdims=True)

    tile = pl.BlockSpec((tm, D), lambda i: (i, 0))
    return pl.pallas_call(
        kern, name=name, grid=(S // tm,),
        in_specs=[tile, tile, tile, _row_spec(D), _row_spec(D)],
        out_specs=[tile, tile, pl.BlockSpec((8, D), lambda i: (0, 0))],
        out_shape=[jax.ShapeDtypeStruct((S, D), F32), jax.ShapeDtypeStruct((S, D), BF16),
                   jax.ShapeDtypeStruct((8, D), F32)],
        compiler_params=_cp(("arbitrary",)),
    )(dxo, x_in, y, gate1p, ln_g)


def _tri64():
    r = lax.broadcasted_iota(jnp.int32, (CHUNK, CHUNK), 0)
    c = lax.broadcasted_iota(jnp.int32, (CHUNK, CHUNK), 1)
    return r >= c


def _gla_chunk_common(proj_ref, rows, b, h):
    kc = slice(h * GLA_DK_HEAD, (h + 1) * GLA_DK_HEAD)
    bh = b[:, kc]
    ep = jnp.exp(bh)
    en = jnp.exp(-bh)
    bl = bh[CHUNK - 1:CHUNK, :]
    ee = jnp.exp(bl - bh)
    dec = jnp.exp(bl)
    q = proj_ref[rows, h * GLA_DK_HEAD:(h + 1) * GLA_DK_HEAD] * GLA_SCALE
    k = proj_ref[rows, GLA_DK + h * GLA_DK_HEAD:GLA_DK + (h + 1) * GLA_DK_HEAD]
    v = proj_ref[rows, 2 * GLA_DK + h * GLA_DV_HEAD:2 * GLA_DK + (h + 1) * GLA_DV_HEAD]
    g = proj_ref[rows, 2 * GLA_DK + GLA_DV + h * GLA_DV_HEAD:2 * GLA_DK + GLA_DV + (h + 1) * GLA_DV_HEAD]
    return ep, en, ee, dec, q, k, v, g


def gla_fwd(proj, wgk_p, bgk, gnorm, name):
    S = proj.shape[0]
    TB = min(GLA_TB, S)
    ncb = TB // CHUNK

    def kern(proj_ref, wgk_ref, bgk_ref, gn_ref, zg_ref, st_ref, state_scr, la_scr):
        @pl.when(pl.program_id(0) == 0)
        def _():
            state_scr[...] = jnp.zeros_like(state_scr)
        lr = proj_ref[:, GLA_LR_OFF:GLA_IN_PAD].astype(BF16)
        gk = _dot(lr, wgk_ref[...]) + bgk_ref[...]
        la_scr[...] = _log_sigmoid(gk) * (1.0 / 16.0)
        lower = _tri64()
        tri = lower.astype(F32)

        def chunk(c, carry):
            rows = pl.ds(pl.multiple_of(c * CHUNK, CHUNK), CHUNK)
            b = jnp.dot(tri, la_scr[rows, :], precision=HIGHEST, preferred_element_type=F32)
            for h in range(GLA_HEADS):
                ep, en, ee, dec, q, k, v, g = _gla_chunk_common(proj_ref, rows, b, h)
                qf = (q * ep).astype(BF16)
                a_f = _dot_nt(qf, (k * en).astype(BF16))
                a_b = _dot_nt((q * en).astype(BF16), (k * ep).astype(BF16))
                amat = jnp.where(lower, a_f, a_b).astype(BF16)
                st = state_scr[h]
                st_ref[c, h] = st
                vb = v.astype(BF16)
                o = _dot(amat, vb) + _dot_nt(qf, st.astype(BF16))
                r = lax.rsqrt(jnp.mean(o * o, axis=-1, keepdims=True) + RMS_EPS)
                on = (o * r) * gn_ref[:, h * GLA_DV_HEAD:(h + 1) * GLA_DV_HEAD]
                zg_ref[rows, h * GLA_DV_HEAD:(h + 1) * GLA_DV_HEAD] = (on * (g * _sigmoid(g))).astype(BF16)
                state_scr[h] = st * dec + _dot_tn(vb, (k * ee).astype(BF16))
            return carry

        lax.fori_loop(0, ncb, chunk, 0)

    return pl.pallas_call(
        kern, name=name, grid=(S // TB,),
        in_specs=[pl.BlockSpec((TB, GLA_IN_PAD), lambda i: (i, 0)),
                  pl.BlockSpec((128, GLA_DK), lambda i: (0, 0)), _row_spec(GLA_DK), _row_spec(GLA_DV)],
        out_specs=[pl.BlockSpec((TB, GLA_DV), lambda i: (i, 0)),
                   pl.BlockSpec((ncb, GLA_HEADS, GLA_DV_HEAD, GLA_DK_HEAD), lambda i: (i, 0, 0, 0))],
        out_shape=[jax.ShapeDtypeStruct((S, GLA_DV), BF16),
                   jax.ShapeDtypeStruct((S // CHUNK, GLA_HEADS, GLA_DV_HEAD, GLA_DK_HEAD), F32)],
        scratch_shapes=[pltpu.VMEM((GLA_HEADS, GLA_DV_HEAD, GLA_DK_HEAD), F32), pltpu.VMEM((TB, GLA_DK), F32)],
        compiler_params=_cp(("arbitrary",)),
    )(proj, wgk_p, bgk, gnorm)


def gla_bwd(proj, states, dzg, wgk_p, bgk, gnorm, name):
    S = proj.shape[0]
    TB = min(GLA_TB, S)
    ncb = TB // CHUNK
    nb = S // TB

    def kern(proj_ref, st_ref, dzg_ref, wgk_ref, bgk_ref, gn_ref,
             dproj_ref, dwgk_ref, dbgk_ref, dgn_ref, dstate_scr, la_scr, gk_scr, dgk_scr):
        @pl.when(pl.program_id(0) == 0)
        def _():
            dstate_scr[...] = jnp.zeros_like(dstate_scr)
            dwgk_ref[...] = jnp.zeros_like(dwgk_ref)
            dbgk_ref[...] = jnp.zeros_like(dbgk_ref)
            dgn_ref[...] = jnp.zeros_like(dgn_ref)
        lr = proj_ref[:, GLA_LR_OFF:GLA_IN_PAD].astype(BF16)
        gk = _dot(lr, wgk_ref[...]) + bgk_ref[...]
        gk_scr[...] = gk
        la_scr[...] = _log_sigmoid(gk) * (1.0 / 16.0)
        lower = _tri64()
        tri = lower.astype(F32)
        r_i = lax.broadcasted_iota(jnp.int32, (CHUNK, CHUNK), 0)
        c_i = lax.broadcasted_iota(jnp.int32, (CHUNK, CHUNK), 1)
        triu = (c_i >= r_i).astype(F32)
        last_row = lax.broadcasted_iota(jnp.int32, (CHUNK, GLA_DK_HEAD), 0) == CHUNK - 1

        def chunk(cc, carry):
            c = ncb - 1 - cc
            rows = pl.ds(pl.multiple_of(c * CHUNK, CHUNK), CHUNK)
            b = jnp.dot(tri, la_scr[rows, :], precision=HIGHEST, preferred_element_type=F32)
            for h in range(GLA_HEADS):
                kc = slice(h * GLA_DK_HEAD, (h + 1) * GLA_DK_HEAD)
                vc = slice(h * GLA_DV_HEAD, (h + 1) * GLA_DV_HEAD)
                ep, en, ee, dec, q, k, v, g = _gla_chunk_common(proj_ref, rows, b, h)
                qf = q * ep
                kn = k * en
                qn = q * en
                kp = k * ep
                ke = k * ee
                qf_b, kn_b, qn_b, kp_b, ke_b = (t.astype(BF16) for t in (qf, kn, qn, kp, ke))
                vb = v.astype(BF16)
                amat = jnp.where(lower, _dot_nt(qf_b, kn_b), _dot_nt(qn_b, kp_b)).astype(BF16)
                st = st_ref[c, h]
                st_b = st.astype(BF16)
                o = _dot(amat, vb) + _dot_nt(qf_b, st_b)
                r = lax.rsqrt(jnp.mean(o * o, axis=-1, keepdims=True) + RMS_EPS)
                oh = o * r
                gn = gn_ref[:, vc]
                sg = _sigmoid(g)
                dz = dzg_ref[rows, vc]
                don = dz * (g * sg)
                dg = dz * (oh * gn) * (sg * (1.0 + g * (1.0 - sg)))
                dgn_ref[:, vc] += jnp.sum(don * oh, axis=0, keepdims=True)
                doh = don * gn
                do = r * (doh - oh * jnp.mean(doh * oh, axis=-1, keepdims=True))
                do_b = do.astype(BF16)
                dst = dstate_scr[h]
                dst_b = dst.astype(BF16)
                dv = _dot_tn(amat, do_b) + _dot_nt(ke_b, dst_b)
                da = _dot_nt(do_b, vb)
                da_f = jnp.where(lower, da, 0.0).astype(BF16)
                da_b = jnp.where(lower, 0.0, da).astype(BF16)
                dqf = _dot(da_f, kn_b) + _dot(do_b, st_b)
                dkn = _dot_tn(da_f, qf_b)
                dqn = _dot(da_b, kp_b)
                dkp = _dot_tn(da_b, qn_b)
                dke = _dot(vb, dst_b)
                ddec = jnp.sum(dst * st, axis=0, keepdims=True)
                dstate_scr[h] = dst * dec + _dot_tn(do_b, qf_b)
                dq = (dqf * ep + dqn * en) * GLA_SCALE
                dk = dkn * en + dkp * ep + dke * ee
                db = dqf * qf - dkn * kn - dqn * qn + dkp * kp - dke * ke
                dbl = jnp.sum(dke * ke, axis=0, keepdims=True) + ddec * dec
                db = db + jnp.where(last_row, dbl, 0.0)
                dla = jnp.dot(triu, db, precision=HIGHEST, preferred_element_type=F32)
                dgk_scr[rows, kc] = dla * (1.0 / 16.0) * _sigmoid(-gk_scr[rows, kc])
                dproj_ref[rows, kc] = dq.astype(BF16)
                dproj_ref[rows, GLA_DK + h * GLA_DK_HEAD:GLA_DK + (h + 1) * GLA_DK_HEAD] = dk.astype(BF16)
                dproj_ref[rows, 2 * GLA_DK + h * GLA_DV_HEAD:2 * GLA_DK + (h + 1) * GLA_DV_HEAD] = dv.astype(BF16)
                dproj_ref[rows, 2 * GLA_DK + GLA_DV + h * GLA_DV_HEAD:
                          2 * GLA_DK + GLA_DV + (h + 1) * GLA_DV_HEAD] = dg.astype(BF16)
            return carry

        lax.fori_loop(0, ncb, chunk, 0)
        dgk = dgk_scr[...]
        dgk_b = dgk.astype(BF16)
        dproj_ref[:, GLA_LR_OFF:GLA_IN_PAD] = _dot_nt(dgk_b, wgk_ref[...]).astype(BF16)
        dwgk_ref[...] += _dot_tn(lr, dgk_b)
        dbgk_ref[...] += jnp.sum(dgk, axis=0, keepdims=True)

    rev = lambda i: (nb - 1 - i, 0)
    return pl.pallas_call(
        kern, name=name, grid=(nb,),
        in_specs=[pl.BlockSpec((TB, GLA_IN_PAD), rev),
                  pl.BlockSpec((ncb, GLA_HEADS, GLA_DV_HEAD, GLA_DK_HEAD), lambda i: (nb - 1 - i, 0, 0, 0)),
                  pl.BlockSpec((TB, GLA_DV), rev),
                  pl.BlockSpec((128, GLA_DK), lambda i: (0, 0)), _row_spec(GLA_DK), _row_spec(GLA_DV)],
        out_specs=[pl.BlockSpec((TB, GLA_IN_PAD), rev),
                   pl.BlockSpec((128, GLA_DK), lambda i: (0, 0)), _row_spec(GLA_DK), _row_spec(GLA_DV)],
        out_shape=[jax.ShapeDtypeStruct((S, GLA_IN_PAD), BF16), jax.ShapeDtypeStruct((128, GLA_DK), F32),
                   jax.ShapeDtypeStruct((1, GLA_DK), F32), jax.ShapeDtypeStruct((1, GLA_DV), F32)],
        scratch_shapes=[pltpu.VMEM((GLA_HEADS, GLA_DV_HEAD, GLA_DK_HEAD), F32), pltpu.VMEM((TB, GLA_DK), F32),
                        pltpu.VMEM((TB, GLA_DK), F32), pltpu.VMEM((TB, GLA_DK), F32)],
        compiler_params=_cp(("arbitrary",)),
    )(proj, states, dzg, wgk_p, bgk, gnorm)


ATT_TW = 1024
ATT_CLASSES = 3


def _att_window(i):
    return pl.multiple_of(jnp.maximum(i * ATT_TQ - LEFT_CHUNKS * CHUNK, 0), ATT_TQ)


def _att_rel_index():
    e = jnp.arange(ATT_TW)[None, :]
    d = jnp.where(e < ATT_KW, e, e - ATT_TW)
    off = (jnp.arange(ATT_CLASSES) * ATT_TQ)[:, None]
    return jnp.clip(off - d, -MAX_REL, MAX_REL) + MAX_REL


def _row_bits():
    return lax.broadcasted_iota(jnp.int32, (ATT_TQ, ATT_TW), 0)


def att_bias_tiles(rel_bias, name):
    tab = jnp.take(rel_bias, _att_rel_index(), axis=1).reshape(ATT_HEADS * ATT_CLASSES, 1, ATT_TW)

    def kern(t_ref, o_ref):
        cls = pl.program_id(0) % ATT_CLASSES
        x = jnp.broadcast_to(t_ref[...], (ATT_TQ, ATT_TW))
        row = _row_bits()
        for b in range(8):
            x = jnp.where((row & (1 << b)) != 0, pltpu.roll(x, 1 << b, axis=1), x)
        x = x[:, :ATT_KW]
        qc = cls * (ATT_TQ // CHUNK) + lax.shift_right_arithmetic(
            lax.broadcasted_iota(jnp.int32, (ATT_TQ, ATT_KW), 0), 6)
        kc = lax.shift_right_arithmetic(lax.broadcasted_iota(jnp.int32, (ATT_TQ, ATT_KW), 1), 6)
        o_ref[...] = jnp.where((kc <= qc) & (kc >= qc - LEFT_CHUNKS), x, NEG_INF)

    return pl.pallas_call(
        kern, name=name, grid=(ATT_HEADS * ATT_CLASSES,),
        in_specs=[pl.BlockSpec((None, 1, ATT_TW), lambda i: (i, 0, 0))],
        out_specs=pl.BlockSpec((None, ATT_TQ, ATT_KW), lambda i: (i, 0, 0)),
        out_shape=jax.ShapeDtypeStruct((ATT_HEADS * ATT_CLASSES, ATT_TQ, ATT_KW), F32),
        compiler_params=_cp(("parallel",)),
    )(tab)


def att_bias_grad(dbt, name):
    def kern(d_ref, o_ref):
        x = jnp.concatenate([d_ref[...], jnp.zeros((ATT_TQ, ATT_TW - ATT_KW), F32)], axis=1)
        row = _row_bits()
        for b in range(8):
            x = jnp.where((row & (1 << b)) != 0, pltpu.roll(x, ATT_TW - (1 << b), axis=1), x)
        o_ref[...] = jnp.sum(x, axis=0, keepdims=True)

    diag = pl.pallas_call(
        kern, name=name + "_diag", grid=(ATT_HEADS * ATT_CLASSES,),
        in_specs=[pl.BlockSpec((None, ATT_TQ, ATT_KW), lambda i: (i, 0, 0))],
        out_specs=pl.BlockSpec((None, 1, ATT_TW), lambda i: (i, 0, 0)),
        out_shape=jax.ShapeDtypeStruct((ATT_HEADS * ATT_CLASSES, 1, ATT_TW), F32),
        compiler_params=_cp(("parallel",)),
    )(dbt)
    diag = diag.reshape(ATT_HEADS, ATT_CLASSES * ATT_TW)
    onehot = (_att_rel_index().reshape(-1)[:, None] == jnp.arange(384)[None, :]).astype(F32)
    return mm_f32(diag, onehot, name + "_bins")[:, :N_REL]


def _att_scores(q_ref, kw, bias_ref, hh):
    hs = slice(hh * ATT_HD, (hh + 1) * ATT_HD)
    q = q_ref[:, hs] * ATT_SCALE
    k = kw[:, hs]
    s = _dot_nt(q, k) + bias_ref[hh]
    e = jnp.exp(s - jnp.max(s, axis=-1, keepdims=True))
    inv = 1.0 / jnp.sum(e, axis=-1, keepdims=True)
    return q, k, e, inv


def _att_specs(S):
    nq = D_MODEL // 128
    q_spec = pl.BlockSpec((ATT_TQ, 128), lambda p, i: (i, p))
    k_spec = pl.BlockSpec((S, 128), lambda p, i: (0, nq + p))
    v_spec = pl.BlockSpec((S, 128), lambda p, i: (0, 2 * nq + p))
    b_spec = pl.BlockSpec((2, None, ATT_TQ, ATT_KW), lambda p, i: (p, jnp.minimum(i, ATT_CLASSES - 1), 0, 0))
    return q_spec, k_spec, v_spec, b_spec


def attn_fwd(qkv, bias, name):
    S = qkv.shape[0]
    q_spec, k_spec, v_spec, b_spec = _att_specs(S)

    def kern(q_ref, k_ref, v_ref, bias_ref, o_ref):
        ws = _att_window(pl.program_id(1))
        kw = k_ref[pl.ds(ws, ATT_KW), :]
        vw = v_ref[pl.ds(ws, ATT_KW), :]
        outs = []
        for hh in range(2):
            _, _, e, inv = _att_scores(q_ref, kw, bias_ref, hh)
            outs.append(_dot(e.astype(BF16), vw[:, hh * ATT_HD:(hh + 1) * ATT_HD]) * inv)
        o_ref[...] = jnp.concatenate(outs, axis=1).astype(BF16)

    return pl.pallas_call(
        kern, name=name, grid=(ATT_HEADS // 2, S // ATT_TQ),
        in_specs=[q_spec, k_spec, v_spec, b_spec],
        out_specs=pl.BlockSpec((ATT_TQ, 128), lambda p, i: (i, p)),
        out_shape=jax.ShapeDtypeStruct((S, D_MODEL), BF16),
        compiler_params=_cp(("parallel", "arbitrary")),
    )(qkv, qkv, qkv, bias)


def attn_bwd(qkv, bias, do, name):
    S = qkv.shape[0]
    nblk = S // ATT_TQ
    q_spec, k_spec, v_spec, b_spec = _att_specs(S)

    def kern(q_ref, k_ref, v_ref, bias_ref, do_ref, dqkv_ref, db_ref, dk_scr, dv_scr):
        i = pl.program_id(1)

        @pl.when(i == 0)
        def _():
            dk_scr[...] = jnp.zeros_like(dk_scr)
            dv_scr[...] = jnp.zeros_like(dv_scr)
            db_ref[...] = jnp.zeros_like(db_ref)
        ws = _att_window(i)
        win = pl.ds(ws, ATT_KW)
        kw = k_ref[win, :]
        vw = v_ref[win, :]
        o_cls = jnp.minimum(i, ATT_CLASSES - 1)
        dqs, dks, dvs = [], [], []
        for hh in range(2):
            hs = slice(hh * ATT_HD, (hh + 1) * ATT_HD)
            q, k, e, inv = _att_scores(q_ref, kw, bias_ref, hh)
            p = e * inv
            do_h = do_ref[:, hs]
            dp = _dot_nt(do_h, vw[:, hs])
            ds = p * (dp - jnp.sum(p * dp, axis=-1, keepdims=True))
            db_ref[hh, o_cls] += ds
            ds_b = ds.astype(BF16)
            dqs.append(_dot(ds_b, k) * ATT_SCALE)
            dks.append(_dot_tn(ds_b, q))
            dvs.append(_dot_tn(p.astype(BF16), do_h))
        dqkv_ref[0, pl.ds(pl.multiple_of(i * ATT_TQ, ATT_TQ), ATT_TQ), :] = jnp.concatenate(dqs, axis=1).astype(BF16)
        dk_scr[win, :] += jnp.concatenate(dks, axis=1)
        dv_scr[win, :] += jnp.concatenate(dvs, axis=1)

        @pl.when(i == nblk - 1)
        def _():
            dqkv_ref[1] = dk_scr[...].astype(BF16)
            dqkv_ref[2] = dv_scr[...].astype(BF16)

    return pl.pallas_call(
        kern, name=name, grid=(ATT_HEADS // 2, nblk),
        in_specs=[q_spec, k_spec, v_spec, b_spec, pl.BlockSpec((ATT_TQ, 128), lambda p, i: (i, p))],
        out_specs=[pl.BlockSpec((3, S, 128), lambda p, i: (0, 0, p)),
                   pl.BlockSpec((2, ATT_CLASSES, ATT_TQ, ATT_KW), lambda p, i: (p, 0, 0, 0))],
        out_shape=[jax.ShapeDtypeStruct((3, S, D_MODEL), BF16),
                   jax.ShapeDtypeStruct((ATT_HEADS, ATT_CLASSES, ATT_TQ, ATT_KW), F32)],
        scratch_shapes=[pltpu.VMEM((S, 128), F32), pltpu.VMEM((S, 128), F32)],
        compiler_params=_cp(("parallel", "arbitrary")),
    )(qkv, qkv, qkv, bias, do)


def colsum3(a3, name):
    P, S, N = a3.shape
    tm = min(512, S)

    def kern(a_ref, o_ref):
        @pl.when(pl.program_id(1) == 0)
        def _():
            o_ref[...] = jnp.zeros_like(o_ref)
        o_ref[...] += jnp.sum(a_ref[...].astype(F32), axis=0, keepdims=True)

    return pl.pallas_call(
        kern, name=name, grid=(P, S // tm),
        in_specs=[pl.BlockSpec((None, tm, N), lambda p, i: (p, i, 0))],
        out_specs=pl.BlockSpec((None, 1, N), lambda p, i: (p, 0, 0)),
        out_shape=jax.ShapeDtypeStruct((P, 1, N), F32),
        compiler_params=_cp(("parallel", "arbitrary")),
    )(a3)


def _me():
    return lax.axis_index("x"), lax.axis_index("y"), lax.axis_index("c")


def _other_chips(x, y):
    return [(1 - x, y), (x, 1 - y), (1 - x, 1 - y)]


def all_gather8(x_shard, name):
    m_per, n = x_shard.shape

    def body(x_ref, out_ref, send_sems, recv_sems, local_sem):
        x, y, c = _me()
        me, sibling = (x, y, c), (x, y, 1 - c)
        chips = _other_chips(x, y)

        def rows(px, py, pc):
            return out_ref.at[pl.ds((4 * px + 2 * py + pc) * m_per, m_per), :]

        def copy(k, block, to, src=None):
            return pltpu.make_async_remote_copy(
                src_ref=rows(*block) if src is None else src, dst_ref=rows(*block),
                send_sem=send_sems.at[k], recv_sem=recv_sems.at[k], device_id=to, device_id_type=MESH)

        mine = pltpu.make_async_copy(x_ref, rows(*me), local_sem)
        mine.start()
        first = [copy(0, me, sibling, src=x_ref)]
        first += [copy(1 + j, me, (*chip, c), src=x_ref) for j, chip in enumerate(chips)]
        for cp in first:
            cp.start()
        passed = [copy(4 + j, (*chip, c), sibling) for j, chip in enumerate(chips)]
        for j, chip in enumerate(chips):
            copy(1 + j, (*chip, c), me).wait_recv()
            passed[j].start()
        copy(0, sibling, me).wait_recv()
        for j, chip in enumerate(chips):
            copy(4 + j, (*chip, 1 - c), me).wait_recv()
        for cp in first + passed:
            cp.wait_send()
        mine.wait()

    return pl.pallas_call(
        body, name=name,
        out_shape=jax.ShapeDtypeStruct((N_DEV * m_per, n), x_shard.dtype),
        in_specs=[pl.BlockSpec(memory_space=pltpu.VMEM)],
        out_specs=pl.BlockSpec(memory_space=pltpu.VMEM),
        scratch_shapes=[pltpu.SemaphoreType.DMA((7,)), pltpu.SemaphoreType.DMA((7,)), pltpu.SemaphoreType.DMA],
        compiler_params=pltpu.CompilerParams(vmem_limit_bytes=VMEM_LIMIT),
    )(x_shard)


def _half_rows(n_rows, c):
    h = n_rows // 2
    return pl.ds(c * h, h)


def _gathered_shape(shape, flavour):
    L, a, b = shape
    return {"col": (L, a, N_CHIPS * b), "row": (L, N_CHIPS * a, b), "lead": (N_CHIPS, L, a, b)}[flavour]


def _gathered_part(out_ref, shape, flavour, s, rows):
    L, a, b = shape
    if flavour == "col":
        return out_ref.at[:, rows, pl.ds(s * b, b)]
    if flavour == "row":
        return out_ref.at[:, pl.ds(s * a + rows.start, rows.size), :]
    return out_ref.at[s, :, rows, :]


def all_gather_weights(shards, flavours, name):
    n = len(shards)
    shapes = [w.shape for w in shards]

    def body(*refs):
        w_refs, out_refs = refs[:n], refs[n:2 * n]
        send_sems, recv_sems, local_sems = refs[2 * n:]
        x, y, c = _me()
        sibling = (x, y, 1 - c)
        chips = _other_chips(x, y)
        me_s = 2 * x + y

        def copy(k, src, dst, to):
            return pltpu.make_async_remote_copy(src_ref=src, dst_ref=dst, send_sem=send_sems.at[k],
                                                recv_sem=recv_sems.at[k], device_id=to, device_id_type=MESH)

        local, first, passed = [], [], []
        for w in range(n):
            shp, fl = shapes[w], flavours[w]
            rows_all = pl.ds(0, shp[1])
            my_half = _half_rows(shp[1], c)
            cp = pltpu.make_async_copy(w_refs[w], _gathered_part(out_refs[w], shp, fl, me_s, rows_all), local_sems.at[w])
            cp.start()
            local.append(cp)
            for j, chip in enumerate(chips):
                cp = copy(6 * w + j, w_refs[w].at[:, my_half, :], _gathered_part(out_refs[w], shp, fl, me_s, my_half),
                          (*chip, c))
                cp.start()
                first.append(cp)
        for w in range(n):
            shp, fl = shapes[w], flavours[w]
            my_half = _half_rows(shp[1], c)
            for j, chip in enumerate(chips):
                part = _gathered_part(out_refs[w], shp, fl, 2 * chip[0] + chip[1], my_half)
                copy(6 * w + j, part, part, (*chip, c)).wait_recv()
                cp = copy(6 * w + 3 + j, part, part, sibling)
                cp.start()
                passed.append(cp)
        for w in range(n):
            shp, fl = shapes[w], flavours[w]
            sib_half = _half_rows(shp[1], 1 - c)
            for j, chip in enumerate(chips):
                part = _gathered_part(out_refs[w], shp, fl, 2 * chip[0] + chip[1], sib_half)
                copy(6 * w + 3 + j, part, part, sibling).wait_recv()
        for cp in first + passed:
            cp.wait_send()
        for cp in local:
            cp.wait()

    any_spec = pl.BlockSpec(memory_space=pl.ANY)
    return pl.pallas_call(
        body, name=name,
        out_shape=[jax.ShapeDtypeStruct(_gathered_shape(s, f), w.dtype) for w, s, f in zip(shards, shapes, flavours)],
        in_specs=[any_spec] * n, out_specs=[any_spec] * n,
        scratch_shapes=[pltpu.SemaphoreType.DMA((6 * n,)), pltpu.SemaphoreType.DMA((6 * n,)),
                        pltpu.SemaphoreType.DMA((n,))],
    )(*shards)


def swap_halves(gs, name):
    n = len(gs)

    def body(*refs):
        g_refs, out_refs = refs[:n], refs[n:2 * n]
        send_sems, recv_sems = refs[2 * n:]
        x, y, c = _me()
        cps = [pltpu.make_async_remote_copy(
            src_ref=g_refs[w].at[:, _half_rows(gs[w].shape[1], 1 - c), :], dst_ref=out_refs[w],
            send_sem=send_sems.at[w], recv_sem=recv_sems.at[w], device_id=(x, y, 1 - c), device_id_type=MESH)
            for w in range(n)]
        for cp in cps:
            cp.start()
        for cp in cps:
            cp.wait()

    any_spec = pl.BlockSpec(memory_space=pl.ANY)
    return pl.pallas_call(
        body, name=name,
        out_shape=[jax.ShapeDtypeStruct((g.shape[0], g.shape[1] // 2, g.shape[2]), g.dtype) for g in gs],
        in_specs=[any_spec] * n, out_specs=[any_spec] * n,
        scratch_shapes=[pltpu.SemaphoreType.DMA((n,)), pltpu.SemaphoreType.DMA((n,))],
    )(*gs)


def add_half(g, r1, c_idx, name):
    n, R, C = g.shape
    half = R // 2
    tr = _rows_block(half, C)
    nbh = half // tr

    def kern(c_ref, g_ref, r_ref, o_ref):
        o_ref[...] = g_ref[...] + r_ref[...]

    return pl.pallas_call(
        kern, name=name,
        grid_spec=pltpu.PrefetchScalarGridSpec(
            num_scalar_prefetch=1, grid=(n, nbh),
            in_specs=[pl.BlockSpec((1, tr, C), lambda d, r, c_ref: (d, c_ref[0] * nbh + r, 0)),
                      pl.BlockSpec((1, tr, C), lambda d, r, c_ref: (d, r, 0))],
            out_specs=pl.BlockSpec((1, tr, C), lambda d, r, c_ref: (d, r, 0))),
        out_shape=jax.ShapeDtypeStruct((n, half, C), F32),
        compiler_params=_cp(("parallel", "parallel")),
    )(c_idx, g, r1)


def exchange_chips(ps, name):
    n = len(ps)

    def body(*refs):
        p_refs, out_refs = refs[:n], refs[n:2 * n]
        send_sems, recv_sems = refs[2 * n:]
        x, y, c = _me()
        chips = _other_chips(x, y)
        cps = [pltpu.make_async_remote_copy(
            src_ref=p_refs[w].at[2 * chip[0] + chip[1]], dst_ref=out_refs[w].at[j],
            send_sem=send_sems.at[3 * w + j], recv_sem=recv_sems.at[3 * w + j],
            device_id=(*chip, c), device_id_type=MESH)
            for w in range(n) for j, chip in enumerate(chips)]
        for cp in cps:
            cp.start()
        for cp in cps:
            cp.wait()

    any_spec = pl.BlockSpec(memory_space=pl.ANY)
    return pl.pallas_call(
        body, name=name,
        out_shape=[jax.ShapeDtypeStruct((3,) + p.shape[1:], p.dtype) for p in ps],
        in_specs=[any_spec] * n, out_specs=[any_spec] * n,
        scratch_shapes=[pltpu.SemaphoreType.DMA((3 * n,)), pltpu.SemaphoreType.DMA((3 * n,))],
    )(*ps)


def add_chips(p, r2, chip_idx, name):
    n, H, C = p.shape
    tr = _rows_block(H, C)

    def kern(s_ref, p_ref, r_ref, o_ref):
        o_ref[...] = ((p_ref[0] + r_ref[0]) + r_ref[1]) + r_ref[2]

    return pl.pallas_call(
        kern, name=name,
        grid_spec=pltpu.PrefetchScalarGridSpec(
            num_scalar_prefetch=1, grid=(H // tr,),
            in_specs=[pl.BlockSpec((1, tr, C), lambda r, s_ref: (s_ref[0], r, 0)),
                      pl.BlockSpec((3, tr, C), lambda r, s_ref: (0, r, 0))],
            out_specs=pl.BlockSpec((tr, C), lambda r, s_ref: (r, 0))),
        out_shape=jax.ShapeDtypeStruct((H, C), F32),
        compiler_params=_cp(("parallel",)),
    )(chip_idx, p, r2)


def join_halves(ss, groups, name):
    n = len(ss)
    where = [(gi, l) for gi, (L, _, _) in enumerate(groups) for l in range(L)]
    assert len(where) == n

    def body(*refs):
        s_refs, out_refs = refs[:n], refs[n:n + len(groups)]
        send_sems, recv_sems, local_sems = refs[n + len(groups):]
        x, y, c = _me()
        local, remote = [], []
        for w in range(n):
            gi, l = where[w]
            mine = out_refs[gi].at[l, _half_rows(groups[gi][1], c), :]
            cp = pltpu.make_async_copy(s_refs[w], mine, local_sems.at[w])
            cp.start()
            local.append(cp)
            cp = pltpu.make_async_remote_copy(src_ref=s_refs[w], dst_ref=mine, send_sem=send_sems.at[w],
                                              recv_sem=recv_sems.at[w], device_id=(x, y, 1 - c), device_id_type=MESH)
            cp.start()
            remote.append(cp)
        for w in range(n):
            gi, l = where[w]
            theirs = out_refs[gi].at[l, _half_rows(groups[gi][1], 1 - c), :]
            remote[w].wait_send()
            pltpu.make_async_remote_copy(src_ref=s_refs[w], dst_ref=theirs, send_sem=send_sems.at[w],
                                         recv_sem=recv_sems.at[w], device_id=(x, y, 1 - c),
                                         device_id_type=MESH).wait_recv()
            local[w].wait()

    any_spec = pl.BlockSpec(memory_space=pl.ANY)
    return pl.pallas_call(
        body, name=name,
        out_shape=[jax.ShapeDtypeStruct(g, F32) for g in groups],
        in_specs=[any_spec] * n, out_specs=[any_spec] * len(groups),
        scratch_shapes=[pltpu.SemaphoreType.DMA((n,)), pltpu.SemaphoreType.DMA((n,)), pltpu.SemaphoreType.DMA((n,))],
    )(*ss)


def reduce_scatter_chips(gs, groups, c_idx, chip_idx, name):
    r1 = swap_halves(gs, name + "_swap")
    ps = [add_half(g, r, c_idx, f"{name}_add2_{w}") for w, (g, r) in enumerate(zip(gs, r1))]
    r2 = exchange_chips(ps, name + "_xchg")
    ss = [add_chips(p, r, chip_idx, f"{name}_add4_{w}") for w, (p, r) in enumerate(zip(ps, r2))]
    return join_halves(ss, groups, name + "_join")


BIG = (("gla_w_in", 2, (1024, GLA_IN // N_CHIPS), "lead"), ("gla_w_out", 2, (256, 1024), "row"),
       ("att_w_in", 2, (1024, 768), "col"), ("att_w_out", 2, (256, 1024), "row"),
       ("ff_w1", 4, (1024, 1024), "col"), ("ff_w2", 4, (1024, 1024), "row"))


def local_step(x, target, mods, W, small):
    S, D = x.shape
    row = lambda v: v.reshape(1, -1)
    saved = []
    tiles = [att_bias_tiles(small["att_rel_bias"][j], f"att_tiles_{j}").reshape(ATT_HEADS, ATT_CLASSES, ATT_TQ, ATT_KW)
             for j in range(2)]
    wgk_p = [jnp.pad(small["gla_w_gk2"][j], ((0, 128 - GLA_RANK), (0, 0))).astype(BF16) for j in range(2)]

    for i in range(DEPTH):
        j = i // 2
        sh1, sc1, g1, sh2, sc2, g2 = (row(mods[i, k]) for k in range(6))
        u1 = modulate(x, sc1, sh1, f"mod1_{i}")
        if i % 2 == 0:
            proj = mm_plain(u1, W["gla_w_in"], j, f"gla_in_{i}")
            zmix, states = gla_fwd(proj, wgk_p[j], row(small["gla_b_gk"][j]), row(small["gla_g_norm"][j]),
                                   f"gla_fwd_{i}")
            w_out = W["gla_w_out"]
        else:
            proj = mm_plain(u1, W["att_w_in"], j, f"att_in_{i}", mode="bf16", bias=row(small["att_b_in"][j]))
            zmix = attn_fwd(proj, tiles[j], f"att_fwd_{i}")
            states = None
            w_out = W["att_w_out"]
        y1, x_mid = mm_down_ln(zmix, w_out, j, x, 1.0 + g1, row(small["ln_g"][i, 0]), row(small["ln_b"][i, 0]),
                               f"mix_out_{i}")
        u2 = modulate(x_mid, sc2, sh2, f"mod2_{i}")
        h, act = mm_plain(u2, W["ff_w1"], i, f"ff_up_{i}", mode="mlp_up")
        y2, x_out = mm_down_ln(act, W["ff_w2"], i, x_mid, 1.0 + g2, row(small["ln_g"][i, 1]),
                               row(small["ln_b"][i, 1]), f"ff_out_{i}")
        saved.append(dict(x_in=x, u1=u1, proj=proj, zmix=zmix, states=states, y1=y1, x_mid=x_mid, u2=u2, h=h,
                          act=act, y2=y2))
        x = x_out

    dx, sq = loss_head(x, target, "loss_head")

    gW = {n: [None] * L for n, L, *_ in BIG}
    g_small = dict(ln_g=[None] * DEPTH, ln_b=[None] * DEPTH, gla_w_gk2=[None] * 2, gla_b_gk=[None] * 2,
                   gla_g_norm=[None] * 2, att_b_in=[None] * 2, att_rel_bias=[None] * 2)
    dmods = [None] * DEPTH

    for i in reversed(range(DEPTH)):
        j = i // 2
        sv = saved[i]
        sh1, sc1, g1, sh2, sc2, g2 = (row(mods[i, k]) for k in range(6))
        dz2, dy2, s_ln2 = ln_bwd(dx, sv["x_mid"], sv["y2"], 1.0 + g2, row(small["ln_g"][i, 1]), f"ln2_bwd_{i}")
        dh = mm_plain(dy2, W["ff_w2"], i, f"ff_dn_{i}", mode="mlp_dn", nt=True, h=sv["h"])
        gW["ff_w2"][i] = mm_w(sv["act"], dy2, f"ff_w2g_{i}").reshape(N_CHIPS, D_FF // N_CHIPS, D)
        gW["ff_w1"][i] = mm_w(sv["u2"], dh, f"ff_w1g_{i}", chips_out=True)
        dx_mid, s_m2 = mm_down_comb(dh, W["ff_w1"], i, dz2, sv["x_mid"], 1.0 + sc2, f"ff_dx_{i}")
        dz1, dy1, s_ln1 = ln_bwd(dx_mid, sv["x_in"], sv["y1"], 1.0 + g1, row(small["ln_g"][i, 0]), f"ln1_bwd_{i}")
        if i % 2 == 0:
            gW["gla_w_out"][j] = mm_w(sv["zmix"], dy1, f"gla_wog_{i}").reshape(N_CHIPS, D // N_CHIPS, D)
            dzg = mm_plain(dy1, W["gla_w_out"], j, f"gla_dz_{i}", nt=True)
            dproj, dwgk, dbgk, dgn = gla_bwd(sv["proj"], sv["states"], dzg, wgk_p[j], row(small["gla_b_gk"][j]),
                                             row(small["gla_g_norm"][j]), f"gla_bwd_{i}")
            g_small["gla_w_gk2"][j] = dwgk[:GLA_RANK]
            g_small["gla_b_gk"][j] = dbgk[0]
            g_small["gla_g_norm"][j] = dgn[0].reshape(GLA_HEADS, GLA_DV_HEAD)
            gwi = mm_w(sv["u1"], dproj, f"gla_wig_{i}")[:, :GLA_IN]
            gW["gla_w_in"][j] = gwi.reshape(D, N_CHIPS, GLA_IN // N_CHIPS).transpose(1, 0, 2)
            dx, s_m1 = mm_down_comb(dproj, W["gla_w_in"], j, dz1, sv["x_in"], 1.0 + sc1, f"mix_dx_{i}")
        else:
            gW["att_w_out"][j] = mm_w(sv["zmix"], dy1, f"att_wog_{i}").reshape(N_CHIPS, D // N_CHIPS, D)
            do = mm_plain(dy1, W["att_w_out"], j, f"att_do_{i}", mode="bf16", nt=True)
            dqkv, dbt = attn_bwd(sv["proj"], tiles[j], do, f"att_bwd_{i}")
            g_small["att_rel_bias"][j] = att_bias_grad(dbt.reshape(ATT_HEADS * ATT_CLASSES, ATT_TQ, ATT_KW),
                                                       f"att_bias_{i}")
            g_small["att_b_in"][j] = colsum3(dqkv, f"att_bin_{i}").reshape(3 * D)
            gW["att_w_in"][j] = mm_w(sv["u1"], dqkv, f"att_wig_{i}", chips_out=True, b_parts=3, tn=256)
            dx, s_m1 = mm_down_comb(dqkv, W["att_w_in"], j, dz1, sv["x_in"], 1.0 + sc1, f"mix_dx_{i}", parts=3)
        g_small["ln_g"][i] = jnp.stack([s_ln1[0], s_ln2[0]])
        g_small["ln_b"][i] = jnp.stack([s_ln1[1], s_ln2[1]])
        dmods[i] = jnp.stack([s_m1[1], s_m1[0], s_ln1[2], s_m2[1], s_m2[0], s_ln2[2]])

    g_small = {n: jnp.stack(v) for n, v in g_small.items()}
    return sq, dx, gW, jnp.stack(dmods), g_small


SMALL_SHARDED = (("ln_g", (4, 2, 256)), ("ln_b", (4, 2, 256)), ("gla_g_norm", (2, 4, 64)),
                 ("gla_w_gk2", (2, 16, 128)), ("att_b_in", (2, 768)))
SMALL_FULL = dict(ln_g=(4, 2, 1024), ln_b=(4, 2, 1024), gla_g_norm=(2, 4, 256), gla_w_gk2=(2, 16, 512),
                  att_b_in=(2, 3072), gla_b_gk=(2, 512), att_rel_bias=(2, 16, 257))
SMALL_GRAD_ORDER = ("ln_g", "ln_b", "gla_g_norm", "gla_w_gk2", "att_b_in", "gla_b_gk", "att_rel_bias")


def _pack_small(arrs, rows_total):
    parts = []
    for a in arrs:
        flat = a.reshape(-1)
        pad = (-flat.shape[0]) % PACK_W
        parts.append(jnp.pad(flat, (0, pad)).reshape(-1, PACK_W))
    buf = jnp.concatenate(parts, axis=0)
    return jnp.pad(buf, ((0, rows_total - buf.shape[0]), (0, 0)))


def _unpack_small(buf, shapes):
    out, r = [], 0
    for shp in shapes:
        n = 1
        for s in shp:
            n *= s
        nr = (n + PACK_W - 1) // PACK_W
        out.append(buf[..., r:r + nr, :].reshape(buf.shape[:-2] + (nr * PACK_W,))[..., :n].reshape(buf.shape[:-2] + shp))
        r += nr
    return out


def _unshard_last(g4):
    nd = g4.ndim
    perm = tuple(range(1, nd - 1)) + (0, nd - 1)
    t = g4.transpose(perm)
    return t.reshape(t.shape[:-2] + (-1,))


def _shard_last(full, s):
    n = full.shape[-1] // N_CHIPS
    return lax.dynamic_slice_in_dim(full, s * n, n, axis=full.ndim - 1)


WEIGHT_NAMES = ("w_ada", "b_ada", "ln_g", "ln_b", "gla_w_in", "gla_w_gk2", "gla_b_gk", "gla_g_norm", "gla_w_out",
                "att_w_in", "att_b_in", "att_rel_bias", "att_w_out", "ff_w1", "ff_w2")


def kernel(x, c, w_ada, b_ada, ln_g, ln_b, gla_w_in, gla_w_gk2, gla_b_gk, gla_g_norm, gla_w_out, att_w_in, att_b_in, att_rel_bias, att_w_out, ff_w1, ff_w2, loss_target, m_w_ada, m_b_ada, m_ln_g, m_ln_b, m_gla_w_in, m_gla_w_gk2, m_gla_b_gk, m_gla_g_norm, m_gla_w_out, m_att_w_in, m_att_b_in, m_att_rel_bias, m_att_w_out, m_ff_w1, m_ff_w2, v_w_ada, v_b_ada, v_ln_g, v_ln_b, v_gla_w_in, v_gla_w_gk2, v_gla_b_gk, v_gla_g_norm, v_gla_w_out, v_att_w_in, v_att_b_in, v_att_rel_bias, v_att_w_out, v_ff_w1, v_ff_w2):
    weights = dict(w_ada=w_ada, b_ada=b_ada, ln_g=ln_g, ln_b=ln_b, gla_w_in=gla_w_in, gla_w_gk2=gla_w_gk2,
                   gla_b_gk=gla_b_gk, gla_g_norm=gla_g_norm, gla_w_out=gla_w_out, att_w_in=att_w_in,
                   att_b_in=att_b_in, att_rel_bias=att_rel_bias, att_w_out=att_w_out, ff_w1=ff_w1, ff_w2=ff_w2)
    mom1 = dict(w_ada=m_w_ada, b_ada=m_b_ada, ln_g=m_ln_g, ln_b=m_ln_b, gla_w_in=m_gla_w_in, gla_w_gk2=m_gla_w_gk2,
                gla_b_gk=m_gla_b_gk, gla_g_norm=m_gla_g_norm, gla_w_out=m_gla_w_out, att_w_in=m_att_w_in,
                att_b_in=m_att_b_in, att_rel_bias=m_att_rel_bias, att_w_out=m_att_w_out, ff_w1=m_ff_w1, ff_w2=m_ff_w2)
    mom2 = dict(w_ada=v_w_ada, b_ada=v_b_ada, ln_g=v_ln_g, ln_b=v_ln_b, gla_w_in=v_gla_w_in, gla_w_gk2=v_gla_w_gk2,
                gla_b_gk=v_gla_b_gk, gla_g_norm=v_gla_g_norm, gla_w_out=v_gla_w_out, att_w_in=v_att_w_in,
                att_b_in=v_att_b_in, att_rel_bias=v_att_rel_bias, att_w_out=v_att_w_out, ff_w1=v_ff_w1, ff_w2=v_ff_w2)

    ax, ay, ac = lax.axis_index("x"), lax.axis_index("y"), lax.axis_index("c")
    chip = 2 * ax + ay
    dev = 2 * chip + ac
    c_idx = jnp.reshape(ac, (1,)).astype(jnp.int32)
    chip_idx = jnp.reshape(chip, (1,)).astype(jnp.int32)
    S = x.shape[1]
    x2 = x.reshape(S, D_MODEL)
    t2 = loss_target.reshape(S, D_MODEL)

    full = all_gather_weights([weights[n].astype(BF16) for n, *_ in BIG], [f for *_, f in BIG], "gather_weights")
    W = {n: w for (n, *_), w in zip(BIG, full)}
    W["gla_w_in"] = jnp.pad(W["gla_w_in"].transpose(1, 2, 0, 3).reshape(2, D_MODEL, GLA_IN),
                            ((0, 0), (0, 0), (0, GLA_IN_PAD - GLA_IN)))

    small_rows = 16
    spack = _pack_small([c] + [weights[n] for n, _ in SMALL_SHARDED], small_rows)
    sg = all_gather8(spack, "gather_small").reshape(N_DEV, small_rows, PACK_W)
    parts = _unpack_small(sg, [(1, D_MODEL)] + [shp for _, shp in SMALL_SHARDED])
    c_all = parts[0].reshape(N_DEV, D_MODEL)
    small = {n: _unshard_last(p[0::2]) for (n, _), p in zip(SMALL_SHARDED, parts[1:])}
    small["gla_b_gk"] = gla_b_gk
    small["att_rel_bias"] = att_rel_bias

    c_act = silu_rows(jnp.pad(c_all, ((0, 128 - N_DEV), (0, 0))), "silu_c")
    wa = w_ada.astype(BF16).transpose(1, 0, 2).reshape(1, D_MODEL, DEPTH * 6 * D_MODEL // N_CHIPS)
    mods_part = mm_plain(c_act, wa, 0, "ada_fwd", tm=128)[:N_DEV]
    mg = all_gather8(mods_part, "gather_mods").reshape(N_CHIPS, 2, N_DEV, DEPTH, 6 * D_MODEL // N_CHIPS)
    mods_mine = lax.dynamic_index_in_dim(mg[:, 0], dev, axis=1, keepdims=False)
    mods = mods_mine.transpose(1, 0, 2).reshape(DEPTH, 6 * D_MODEL) + b_ada
    mods = mods.reshape(DEPTH, 6, D_MODEL)

    sq, grad_x, gW, dmods, g_small = local_step(x2, t2, mods, W, small)
    loss = lax.psum(0.5 * sq[0, 0] / D_MODEL, ("x", "y", "c"))

    gs = [g for n, *_ in BIG for g in gW[n]]
    groups = [(L,) + shp for _, L, shp, _ in BIG]
    reduced = reduce_scatter_chips(gs, groups, c_idx, chip_idx, "rs")
    g_shard = {n: r for (n, *_), r in zip(BIG, reduced)}

    dm_flat = dmods.reshape(DEPTH, 6 * D_MODEL)
    g_rows = 80
    gpack = _pack_small([dm_flat] + [g_small[n] for n in SMALL_GRAD_ORDER], g_rows)
    gg = all_gather8(gpack, "gather_small_grads").reshape(N_DEV, g_rows, PACK_W)
    gsum = sum_over_devices(gg, "sum_small_grads")
    sums = _unpack_small(gsum, [(DEPTH, 6 * D_MODEL)] + [SMALL_FULL[n] for n in SMALL_GRAD_ORDER])
    grads = dict(b_ada=sums[0])
    for n, full_g in zip(SMALL_GRAD_ORDER, sums[1:]):
        grads[n] = full_g if n in ("gla_b_gk", "att_rel_bias") else _shard_last(full_g, chip)
    dm_all = _unpack_small(gg, [(DEPTH, 6 * D_MODEL)])[0]
    dm_cols = _shard_last(dm_all, chip).reshape(N_DEV, DEPTH * 6 * D_MODEL // N_CHIPS)
    dm_cols = jnp.pad(dm_cols, ((0, 128 - N_DEV), (0, 0))).astype(BF16)
    gwa = mm_w(c_act, dm_cols, "ada_bwd", ts=128)
    grads["w_ada"] = gwa.reshape(D_MODEL, DEPTH, 6 * D_MODEL // N_CHIPS).transpose(1, 0, 2)
    grads.update(g_shard)

    deltas, new_m, new_v = {}, {}, {}
    for n in WEIGHT_NAMES:
        deltas[n], new_m[n], new_v[n] = adamw(weights[n], grads[n], mom1[n], mom2[n], "adamw_" + n)

    return (loss, grad_x.reshape(1, S, D_MODEL), *[grads[n] for n in WEIGHT_NAMES], *[deltas[n] for n in WEIGHT_NAMES],
            *[new_m[n] for n in WEIGHT_NAMES], *[new_v[n] for n in WEIGHT_NAMES])
```

```python
import functools

import jax
import jax.numpy as jnp
from jax import lax
from jax.experimental import pallas as pl
from jax.experimental.pallas import tpu as pltpu

F32 = jnp.float32
BF16 = jnp.bfloat16
HIGHEST = lax.Precision.HIGHEST
MESH = pl.DeviceIdType.MESH

D_MODEL = 1024
DEPTH = 4
CHUNK = 64
GLA_HEADS = 4
GLA_DK = 512
GLA_DV = 1024
GLA_DK_HEAD = 128
GLA_DV_HEAD = 256
GLA_RANK = 16
GLA_IN = 3088
GLA_IN_PAD = 3200
GLA_LR_OFF = 3072
ATT_HEADS = 16
ATT_HD = 64
LEFT_CHUNKS = 8
MAX_REL = 128
N_REL = 257
D_FF = 4096
ALPHA = (2.0 * DEPTH) ** 0.25
LN_EPS = 1e-5
RMS_EPS = 1e-6
NEG_INF = -1e30
GLA_SCALE = GLA_DK_HEAD ** -0.5
ATT_SCALE = ATT_HD ** -0.5
ADAM_LR = 0.001
ADAM_B1 = 0.9
ADAM_B2 = 0.999
ADAM_EPS = 1e-08
ADAM_WD = 0.01
ADAM_STEP = 10

ATT_TQ = 256
ATT_KW = 768
GLA_TB = 256
VMEM_LIMIT = 56 * 1024 * 1024
N_CHIPS = 4
N_DEV = 8
PACK_W = 1024


def _dot(a, b):
    return jnp.dot(a, b, preferred_element_type=F32)


def _dot_nt(a, b):
    return lax.dot_general(a, b, (((1,), (1,)), ((), ())), preferred_element_type=F32)


def _dot_tn(a, b):
    return lax.dot_general(a, b, (((0,), (0,)), ((), ())), preferred_element_type=F32)


def _cp(sem, vmem=VMEM_LIMIT):
    return pltpu.CompilerParams(dimension_semantics=sem, vmem_limit_bytes=vmem)


def _row_spec(n):
    return pl.BlockSpec((1, n), lambda *_: (0, 0))


def _sigmoid(x):
    return 1.0 / (1.0 + jnp.exp(-x))


def _log_sigmoid(x):
    return jnp.minimum(x, 0.0) - jnp.log1p(jnp.exp(-jnp.abs(x)))


def modulate(x, sc, sh, name):
    S, D = x.shape
    tm = min(512, S)

    def kern(x_ref, sc_ref, sh_ref, u_ref):
        u_ref[...] = (x_ref[...] * (1.0 + sc_ref[...]) + sh_ref[...]).astype(BF16)

    return pl.pallas_call(
        kern, name=name, grid=(S // tm,),
        in_specs=[pl.BlockSpec((tm, D), lambda i: (i, 0)), _row_spec(D), _row_spec(D)],
        out_specs=pl.BlockSpec((tm, D), lambda i: (i, 0)),
        out_shape=jax.ShapeDtypeStruct((S, D), BF16),
        compiler_params=_cp(("parallel",)),
    )(x, sc, sh)


def loss_head(x, t, name):
    S, D = x.shape
    tm = min(512, S)

    def kern(x_ref, t_ref, dx_ref, l_ref):
        @pl.when(pl.program_id(0) == 0)
        def _():
            l_ref[...] = jnp.zeros_like(l_ref)
        e = x_ref[...] - t_ref[...]
        dx_ref[...] = e * (1.0 / D)
        l_ref[...] += jnp.sum(e * e)

    return pl.pallas_call(
        kern, name=name, grid=(S // tm,),
        in_specs=[pl.BlockSpec((tm, D), lambda i: (i, 0)), pl.BlockSpec((tm, D), lambda i: (i, 0))],
        out_specs=[pl.BlockSpec((tm, D), lambda i: (i, 0)), pl.BlockSpec((8, 128), lambda i: (0, 0))],
        out_shape=[jax.ShapeDtypeStruct((S, D), F32), jax.ShapeDtypeStruct((8, 128), F32)],
        compiler_params=_cp(("arbitrary",)),
    )(x, t)


def silu_rows(c_all, name):
    def kern(c_ref, o_ref):
        c = c_ref[...]
        o_ref[...] = (c * _sigmoid(c)).astype(BF16)

    return pl.pallas_call(kern, name=name, out_shape=jax.ShapeDtypeStruct(c_all.shape, BF16))(c_all)


def sum_over_devices(g, name):
    n, R, C = g.shape

    def kern(g_ref, o_ref):
        acc = g_ref[0]
        for d in range(1, n):
            acc = acc + g_ref[d]
        o_ref[...] = acc

    return pl.pallas_call(kern, name=name, out_shape=jax.ShapeDtypeStruct((R, C), F32))(g)


def _rows_block(R, C, budget=1 << 20):
    if R * C * 4 <= budget or R % 8:
        return R
    tr = max(8, (budget // (C * 4)) // 8 * 8)
    while R % tr:
        tr -= 8
    return tr


def adamw(w, g, m, v, name):
    shape = w.shape
    C = shape[-1]
    R = w.size // C
    w2, g2, m2, v2 = (t.reshape(R, C) for t in (w, g, m, v))
    tr = _rows_block(R, C)
    c1 = 1.0 - ADAM_B1 ** ADAM_STEP
    c2 = 1.0 - ADAM_B2 ** ADAM_STEP

    def kern(w_ref, g_ref, m_ref, v_ref, d_ref, nm_ref, nv_ref):
        gg = g_ref[...]
        nm = ADAM_B1 * m_ref[...] + (1.0 - ADAM_B1) * gg
        nv = ADAM_B2 * v_ref[...] + (1.0 - ADAM_B2) * (gg * gg)
        m_hat = nm / c1
        v_hat = nv / c2
        d_ref[...] = -ADAM_LR * (m_hat / (jnp.sqrt(v_hat) + ADAM_EPS) + ADAM_WD * w_ref[...])
        nm_ref[...] = nm
        nv_ref[...] = nv

    spec = pl.BlockSpec((tr, C), lambda i: (i, 0))
    outs = pl.pallas_call(
        kern, name=name, grid=(R // tr,),
        in_specs=[spec] * 4, out_specs=[spec] * 3,
        out_shape=[jax.ShapeDtypeStruct((R, C), F32)] * 3,
        compiler_params=_cp(("parallel",)),
    )(w2, g2, m2, v2)
    return tuple(o.reshape(shape) for o in outs)


def _tn_for(N):
    for tn in (1024, 768, 640, 512, 384, 256, 128):
        if N % tn == 0:
            return tn
    return N


def mm_plain(a, b3, layer, name, *, mode="f32", nt=False, bias=None, h=None, tm=512):
    M, K = a.shape
    N = b3.shape[1] if nt else b3.shape[2]
    tm = min(tm, M)
    tn = _tn_for(N)
    a_spec = pl.BlockSpec((tm, K), lambda j, i: (i, 0))
    if nt:
        b_spec = pl.BlockSpec((None, tn, K), lambda j, i: (layer, j, 0))
        dot = _dot_nt
    else:
        b_spec = pl.BlockSpec((None, K, tn), lambda j, i: (layer, 0, j))
        dot = _dot
    o_spec = pl.BlockSpec((tm, tn), lambda j, i: (i, j))
    ins, in_specs = [a, b3], [a_spec, b_spec]

    if mode in ("f32", "bf16"):
        odt = F32 if mode == "f32" else BF16
        if bias is not None:
            ins.append(bias)
            in_specs.append(pl.BlockSpec((1, tn), lambda j, i: (0, j)))

            def kern(a_ref, b_ref, bias_ref, o_ref):
                o_ref[...] = (dot(a_ref[...], b_ref[...]) + bias_ref[...]).astype(odt)
        else:
            def kern(a_ref, b_ref, o_ref):
                o_ref[...] = dot(a_ref[...], b_ref[...]).astype(odt)
        out_specs, out_shape = o_spec, jax.ShapeDtypeStruct((M, N), odt)
    elif mode == "mlp_up":
        def kern(a_ref, b_ref, act_ref):
            r = jnp.maximum(dot(a_ref[...], b_ref[...]), 0.0)
            act_ref[...] = (r * r).astype(BF16)
        out_specs, out_shape = o_spec, jax.ShapeDtypeStruct((M, N), BF16)
    elif mode == "mlp_dn":
        ins.append(h)
        in_specs.append(o_spec)

        def kern(a_ref, b_ref, h_ref, o_ref):
            acc = dot(a_ref[...], b_ref[...])
            o_ref[...] = (acc * (2.0 * jnp.sqrt(h_ref[...].astype(F32)))).astype(BF16)
        out_specs, out_shape = o_spec, jax.ShapeDtypeStruct((M, N), BF16)
    else:
        raise ValueError(mode)

    return pl.pallas_call(
        kern, name=name, grid=(N // tn, M // tm), in_specs=in_specs, out_specs=out_specs,
        out_shape=out_shape, compiler_params=_cp(("parallel", "parallel")),
    )(*ins)


def mm_down_ln(a, b3, layer, x_in, gate1p, ln_g, ln_b, sc_next, sh_next, name, *, tm=256):
    M, K = a.shape
    D = b3.shape[2]
    tm = min(tm, M)

    def kern(a_ref, b_ref, x_ref, gp_ref, lg_ref, lb_ref, sc_ref, sh_ref, y_ref, xo_ref, u_ref):
        y = _dot(a_ref[...], b_ref[...])
        y_ref[...] = y
        z = ALPHA * x_ref[...] + gp_ref[...] * y
        mu = jnp.mean(z, axis=-1, keepdims=True)
        zc = z - mu
        var = jnp.mean(zc * zc, axis=-1, keepdims=True)
        xo = (zc * lax.rsqrt(var + LN_EPS)) * lg_ref[...] + lb_ref[...]
        xo_ref[...] = xo
        u_ref[...] = (xo * (1.0 + sc_ref[...]) + sh_ref[...]).astype(BF16)

    tile = pl.BlockSpec((tm, D), lambda i: (i, 0))
    return pl.pallas_call(
        kern, name=name, grid=(M // tm,),
        in_specs=[pl.BlockSpec((tm, K), lambda i: (i, 0)), pl.BlockSpec((None, K, D), lambda i: (layer, 0, 0)), tile]
        + [_row_spec(D)] * 5,
        out_specs=[tile, tile, tile],
        out_shape=[jax.ShapeDtypeStruct((M, D), F32)] * 2 + [jax.ShapeDtypeStruct((M, D), BF16)],
        compiler_params=_cp(("parallel",)),
    )(a, b3, x_in, gate1p, ln_g, ln_b, sc_next, sh_next)


def mm_down_comb(a, b3, layer, dz, x_in, sc1p, name, *, parts=1, tm=256):
    D, K = b3.shape[1], b3.shape[2]
    M = a.shape[-2]
    kp = K // parts
    tm = min(tm, M)

    def kern(*refs):
        a_refs = refs[:parts]
        b_ref, dz_ref, x_ref, sp_ref, dx_ref, s_ref = refs[parts:]

        @pl.when(pl.program_id(0) == 0)
        def _():
            s_ref[...] = jnp.zeros_like(s_ref)
        if parts == 1:
            du = _dot_nt(a_refs[0][...], b_ref[...])
        else:
            du = _dot_nt(a_refs[0][...], b_ref[:, 0:kp])
            for p in range(1, parts):
                du = du + _dot_nt(a_refs[p][...], b_ref[:, p * kp:(p + 1) * kp])
        dx_ref[...] = ALPHA * dz_ref[...] + du * sp_ref[...]
        s_ref[0:1, :] += jnp.sum(du * x_ref[...], axis=0, keepdims=True)
        s_ref[1:2, :] += jnp.sum(du, axis=0, keepdims=True)

    tile = pl.BlockSpec((tm, D), lambda i: (i, 0))
    if parts == 1:
        a_ins, a_specs = [a], [pl.BlockSpec((tm, K), lambda i: (i, 0))]
    else:
        a_ins = [a] * parts
        a_specs = [pl.BlockSpec((None, tm, kp), functools.partial(lambda i, p: (p, i, 0), p=p)) for p in range(parts)]
    return pl.pallas_call(
        kern, name=name, grid=(M // tm,),
        in_specs=a_specs + [pl.BlockSpec((None, D, K), lambda i: (layer, 0, 0)), tile, tile, _row_spec(D)],
        out_specs=[tile, pl.BlockSpec((8, D), lambda i: (0, 0))],
        out_shape=[jax.ShapeDtypeStruct((M, D), F32), jax.ShapeDtypeStruct((8, D), F32)],
        compiler_params=_cp(("arbitrary",)),
    )(*a_ins, b3, dz, x_in, sc1p)


def mm_w(a, b, name, *, ts=1024, tk=512, chips_out=False, b_parts=1, tn=None):
    S, K = a.shape
    npart = b.shape[-1]
    N = npart * b_parts
    ts = min(ts, S)
    tk = min(tk, K)
    n_chip = N // N_CHIPS
    if tn is None:
        tn = _tn_for(n_chip if chips_out else npart)
    assert npart % tn == 0 and (not chips_out or n_chip % tn == 0)

    def kern(a_ref, b_ref, o_ref):
        @pl.when(pl.program_id(2) == 0)
        def _():
            o_ref[...] = jnp.zeros_like(o_ref)
        o_ref[...] += _dot_tn(a_ref[...], b_ref[...])

    if b_parts == 1:
        b_spec = pl.BlockSpec((ts, tn), lambda k, n, s: (s, n))
    else:
        per = npart // tn
        b_spec = pl.BlockSpec((None, ts, tn), lambda k, n, s: (n // per, s, n % per))
    if chips_out:
        per_chip = n_chip // tn
        o_spec = pl.BlockSpec((None, tk, tn), lambda k, n, s: (n // per_chip, k, n % per_chip))
        out_shape = jax.ShapeDtypeStruct((N_CHIPS, K, n_chip), F32)
    else:
        o_spec = pl.BlockSpec((tk, tn), lambda k, n, s: (k, n))
        out_shape = jax.ShapeDtypeStruct((K, N), F32)
    return pl.pallas_call(
        kern, name=name, grid=(K // tk, N // tn, S // ts),
        in_specs=[pl.BlockSpec((ts, tk), lambda k, n, s: (s, k)), b_spec],
        out_specs=o_spec, out_shape=out_shape,
        compiler_params=_cp(("parallel", "parallel", "arbitrary")),
    )(a, b)


def mm_w_chips3(a, b3, name, *, ts=512):
    S, K = a.shape
    P = b3.shape[2]
    n_chip = 3 * P // N_CHIPS
    ts = min(ts, S)
    pieces = []
    for chip in range(N_CHIPS):
        lo, hi = chip * n_chip, (chip + 1) * n_chip
        while lo < hi:
            part = lo // P
            w = min(hi, (part + 1) * P) - lo
            pieces.append((chip, lo - chip * n_chip, part, lo - part * P, w))
            lo += w

    def kern(a_ref, b_ref, o_ref):
        @pl.when(pl.program_id(0) == 0)
        def _():
            o_ref[...] = jnp.zeros_like(o_ref)
        at = a_ref[...].T
        for chip, oc, part, pc, w in pieces:
            o_ref[chip, :, oc:oc + w] += _dot(at, b_ref[part, :, pc:pc + w])

    return pl.pallas_call(
        kern, name=name, grid=(S // ts,),
        in_specs=[pl.BlockSpec((ts, K), lambda s: (s, 0)), pl.BlockSpec((3, ts, P), lambda s: (0, s, 0))],
        out_specs=pl.BlockSpec((N_CHIPS, K, n_chip), lambda s: (0, 0, 0)),
        out_shape=jax.ShapeDtypeStruct((N_CHIPS, K, n_chip), F32),
        compiler_params=_cp(("arbitrary",)),
    )(a, b3)


def mm_f32(a, b, name):
    def kern(a_ref, b_ref, o_ref):
        o_ref[...] = jnp.dot(a_ref[...], b_ref[...], precision=HIGHEST, preferred_element_type=F32)

    return pl.pallas_call(kern, name=name, out_shape=jax.ShapeDtypeStruct((a.shape[0], b.shape[1]), F32),
                          compiler_params=pltpu.CompilerParams(vmem_limit_bytes=VMEM_LIMIT))(a, b)


def ln_bwd(dxo, x_in, y, gate1p, ln_g, name, *, tm=256):
    S, D = dxo.shape
    tm = min(tm, S)

    def kern(dxo_ref, x_ref, y_ref, gp_ref, lg_ref, dz_ref, dy_ref, s_ref):
        @pl.when(pl.program_id(0) == 0)
        def _():
            s_ref[...] = jnp.zeros_like(s_ref)
        dxo_t = dxo_ref[...]
        yv = y_ref[...]
        z = ALPHA * x_ref[...] + gp_ref[...] * yv
        mu = jnp.mean(z, axis=-1, keepdims=True)
        zc = z - mu
        var = jnp.mean(zc * zc, axis=-1, keepdims=True)
        rstd = lax.rsqrt(var + LN_EPS)
        xhat = zc * rstd
        dxh = dxo_t * lg_ref[...]
        dz = rstd * (dxh - jnp.mean(dxh, axis=-1, keepdims=True)
                     - xhat * jnp.mean(dxh * xhat, axis=-1, keepdims=True))
        dz_ref[...] = dz
        dy_ref[...] = (gp_ref[...] * dz).astype(BF16)
        s_ref[0:1, :] += jnp.sum(dxo_t * xhat, axis=0, keepdims=True)
        s_ref[1:2, :] += jnp.sum(dxo_t, axis=0, keepdims=True)
        s_ref[2:3, :] += jnp.sum(dz * yv, axis=0, keepdims=True)

    tile = pl.BlockSpec((tm, D), lambda i: (i, 0))
    return pl.pallas_call(
        kern, name=name, grid=(S // tm,),
        in_specs=[tile, tile, tile, _row_spec(D), _row_spec(D)],
        out_specs=[tile, tile, pl.BlockSpec((8, D), lambda i: (0, 0))],
        out_shape=[jax.ShapeDtypeStruct((S, D), F32), jax.ShapeDtypeStruct((S, D), BF16),
                   jax.ShapeDtypeStruct((8, D), F32)],
        compiler_params=_cp(("arbitrary",)),
    )(dxo, x_in, y, gate1p, ln_g)


def _tri64():
    r = lax.broadcasted_iota(jnp.int32, (CHUNK, CHUNK), 0)
    c = lax.broadcasted_iota(jnp.int32, (CHUNK, CHUNK), 1)
    return r >= c


def _gla_chunk_common(proj_ref, rows, b, h):
    kc = slice(h * GLA_DK_HEAD, (h + 1) * GLA_DK_HEAD)
    bh = b[:, kc]
    ep = jnp.exp(bh)
    en = jnp.exp(-bh)
    bl = bh[CHUNK - 1:CHUNK, :]
    ee = jnp.exp(bl - bh)
    dec = jnp.exp(bl)
    q = proj_ref[rows, h * GLA_DK_HEAD:(h + 1) * GLA_DK_HEAD] * GLA_SCALE
    k = proj_ref[rows, GLA_DK + h * GLA_DK_HEAD:GLA_DK + (h + 1) * GLA_DK_HEAD]
    v = proj_ref[rows, 2 * GLA_DK + h * GLA_DV_HEAD:2 * GLA_DK + (h + 1) * GLA_DV_HEAD]
    g = proj_ref[rows, 2 * GLA_DK + GLA_DV + h * GLA_DV_HEAD:2 * GLA_DK + GLA_DV + (h + 1) * GLA_DV_HEAD]
    return ep, en, ee, dec, q, k, v, g


def gla_fwd(proj, wgk_p, bgk, gnorm, name):
    S = proj.shape[0]
    TB = min(GLA_TB, S)
    ncb = TB // CHUNK

    def kern(proj_ref, wgk_ref, bgk_ref, gn_ref, zg_ref, st_ref, state_scr, la_scr):
        @pl.when(pl.program_id(0) == 0)
        def _():
            state_scr[...] = jnp.zeros_like(state_scr)
        lr = proj_ref[:, GLA_LR_OFF:GLA_IN_PAD].astype(BF16)
        gk = _dot(lr, wgk_ref[...]) + bgk_ref[...]
        la_scr[...] = _log_sigmoid(gk) * (1.0 / 16.0)
        lower = _tri64()
        tri = lower.astype(F32)

        def chunk(c, carry):
            rows = pl.ds(pl.multiple_of(c * CHUNK, CHUNK), CHUNK)
            b = jnp.dot(tri, la_scr[rows, :], precision=HIGHEST, preferred_element_type=F32)
            for h in range(GLA_HEADS):
                ep, en, ee, dec, q, k, v, g = _gla_chunk_common(proj_ref, rows, b, h)
                qf = (q * ep).astype(BF16)
                a_f = _dot_nt(qf, (k * en).astype(BF16))
                a_b = _dot_nt((q * en).astype(BF16), (k * ep).astype(BF16))
                amat = jnp.where(lower, a_f, a_b).astype(BF16)
                st = state_scr[h]
                st_ref[c, h] = st
                vb = v.astype(BF16)
                o = _dot(amat, vb) + _dot_nt(qf, st.astype(BF16))
                r = lax.rsqrt(jnp.mean(o * o, axis=-1, keepdims=True) + RMS_EPS)
                on = (o * r) * gn_ref[:, h * GLA_DV_HEAD:(h + 1) * GLA_DV_HEAD]
                zg_ref[rows, h * GLA_DV_HEAD:(h + 1) * GLA_DV_HEAD] = (on * (g * _sigmoid(g))).astype(BF16)
                state_scr[h] = st * dec + _dot_tn(vb, (k * ee).astype(BF16))
            return carry

        lax.fori_loop(0, ncb, chunk, 0)

    return pl.pallas_call(
        kern, name=name, grid=(S // TB,),
        in_specs=[pl.BlockSpec((TB, GLA_IN_PAD), lambda i: (i, 0)),
                  pl.BlockSpec((128, GLA_DK), lambda i: (0, 0)), _row_spec(GLA_DK), _row_spec(GLA_DV)],
        out_specs=[pl.BlockSpec((TB, GLA_DV), lambda i: (i, 0)),
                   pl.BlockSpec((ncb, GLA_HEADS, GLA_DV_HEAD, GLA_DK_HEAD), lambda i: (i, 0, 0, 0))],
        out_shape=[jax.ShapeDtypeStruct((S, GLA_DV), BF16),
                   jax.ShapeDtypeStruct((S // CHUNK, GLA_HEADS, GLA_DV_HEAD, GLA_DK_HEAD), F32)],
        scratch_shapes=[pltpu.VMEM((GLA_HEADS, GLA_DV_HEAD, GLA_DK_HEAD), F32), pltpu.VMEM((TB, GLA_DK), F32)],
        compiler_params=_cp(("arbitrary",)),
    )(proj, wgk_p, bgk, gnorm)


def gla_bwd(proj, states, dzg, wgk_p, bgk, gnorm, name):
    S = proj.shape[0]
    TB = min(GLA_TB, S)
    ncb = TB // CHUNK
    nb = S // TB

    def kern(proj_ref, st_ref, dzg_ref, wgk_ref, bgk_ref, gn_ref,
             dproj_ref, dwgk_ref, dbgk_ref, dgn_ref, dstate_scr, la_scr, gk_scr, dgk_scr):
        @pl.when(pl.program_id(0) == 0)
        def _():
            dstate_scr[...] = jnp.zeros_like(dstate_scr)
            dwgk_ref[...] = jnp.zeros_like(dwgk_ref)
            dbgk_ref[...] = jnp.zeros_like(dbgk_ref)
            dgn_ref[...] = jnp.zeros_like(dgn_ref)
        lr = proj_ref[:, GLA_LR_OFF:GLA_IN_PAD].astype(BF16)
        gk = _dot(lr, wgk_ref[...]) + bgk_ref[...]
        gk_scr[...] = gk
        la_scr[...] = _log_sigmoid(gk) * (1.0 / 16.0)
        lower = _tri64()
        tri = lower.astype(F32)
        r_i = lax.broadcasted_iota(jnp.int32, (CHUNK, CHUNK), 0)
        c_i = lax.broadcasted_iota(jnp.int32, (CHUNK, CHUNK), 1)
        triu = (c_i >= r_i).astype(F32)
        last_row = lax.broadcasted_iota(jnp.int32, (CHUNK, GLA_DK_HEAD), 0) == CHUNK - 1

        def chunk(cc, carry):
            c = ncb - 1 - cc
            rows = pl.ds(pl.multiple_of(c * CHUNK, CHUNK), CHUNK)
            b = jnp.dot(tri, la_scr[rows, :], precision=HIGHEST, preferred_element_type=F32)
            for h in range(GLA_HEADS):
                kc = slice(h * GLA_DK_HEAD, (h + 1) * GLA_DK_HEAD)
                vc = slice(h * GLA_DV_HEAD, (h + 1) * GLA_DV_HEAD)
                ep, en, ee, dec, q, k, v, g = _gla_chunk_common(proj_ref, rows, b, h)
                qf = q * ep
                kn = k * en
                qn = q * en
                kp = k * ep
                ke = k * ee
                qf_b, kn_b, qn_b, kp_b, ke_b = (t.astype(BF16) for t in (qf, kn, qn, kp, ke))
                vb = v.astype(BF16)
                amat = jnp.where(lower, _dot_nt(qf_b, kn_b), _dot_nt(qn_b, kp_b)).astype(BF16)
                st = st_ref[c, h]
                st_b = st.astype(BF16)
                o = _dot(amat, vb) + _dot_nt(qf_b, st_b)
                r = lax.rsqrt(jnp.mean(o * o, axis=-1, keepdims=True) + RMS_EPS)
                oh = o * r
                gn = gn_ref[:, vc]
                sg = _sigmoid(g)
                dz = dzg_ref[rows, vc]
                don = dz * (g * sg)
                dg = dz * (oh * gn) * (sg * (1.0 + g * (1.0 - sg)))
                dgn_ref[:, vc] += jnp.sum(don * oh, axis=0, keepdims=True)
                doh = don * gn
                do = r * (doh - oh * jnp.mean(doh * oh, axis=-1, keepdims=True))
                do_b = do.astype(BF16)
                dst = dstate_scr[h]
                dst_b = dst.astype(BF16)
                dv = _dot_tn(amat, do_b) + _dot_nt(ke_b, dst_b)
                da = _dot_nt(do_b, vb)
                da_f = jnp.where(lower, da, 0.0).astype(BF16)
                da_b = jnp.where(lower, 0.0, da).astype(BF16)
                dqf = _dot(da_f, kn_b) + _dot(do_b, st_b)
                dkn = _dot_tn(da_f, qf_b)
                dqn = _dot(da_b, kp_b)
                dkp = _dot_tn(da_b, qn_b)
                dke = _dot(vb, dst_b)
                ddec = jnp.sum(dst * st, axis=0, keepdims=True)
                dstate_scr[h] = dst * dec + _dot_tn(do_b, qf_b)
                dq = (dqf * ep + dqn * en) * GLA_SCALE
                dk = dkn * en + dkp * ep + dke * ee
                db = dqf * qf - dkn * kn - dqn * qn + dkp * kp - dke * ke
                dbl = jnp.sum(dke * ke, axis=0, keepdims=True) + ddec * dec
                db = db + jnp.where(last_row, dbl, 0.0)
                dla = jnp.dot(triu, db, precision=HIGHEST, preferred_element_type=F32)
                dgk_scr[rows, kc] = dla * (1.0 / 16.0) * _sigmoid(-gk_scr[rows, kc])
                dproj_ref[rows, kc] = dq.astype(BF16)
                dproj_ref[rows, GLA_DK + h * GLA_DK_HEAD:GLA_DK + (h + 1) * GLA_DK_HEAD] = dk.astype(BF16)
                dproj_ref[rows, 2 * GLA_DK + h * GLA_DV_HEAD:2 * GLA_DK + (h + 1) * GLA_DV_HEAD] = dv.astype(BF16)
                dproj_ref[rows, 2 * GLA_DK + GLA_DV + h * GLA_DV_HEAD:
                          2 * GLA_DK + GLA_DV + (h + 1) * GLA_DV_HEAD] = dg.astype(BF16)
            return carry

        lax.fori_loop(0, ncb, chunk, 0)
        dgk = dgk_scr[...]
        dgk_b = dgk.astype(BF16)
        dproj_ref[:, GLA_LR_OFF:GLA_IN_PAD] = _dot_nt(dgk_b, wgk_ref[...]).astype(BF16)
        dwgk_ref[...] += _dot_tn(lr, dgk_b)
        dbgk_ref[...] += jnp.sum(dgk, axis=0, keepdims=True)

    rev = lambda i: (nb - 1 - i, 0)
    return pl.pallas_call(
        kern, name=name, grid=(nb,),
        in_specs=[pl.BlockSpec((TB, GLA_IN_PAD), rev),
                  pl.BlockSpec((ncb, GLA_HEADS, GLA_DV_HEAD, GLA_DK_HEAD), lambda i: (nb - 1 - i, 0, 0, 0)),
                  pl.BlockSpec((TB, GLA_DV), rev),
                  pl.BlockSpec((128, GLA_DK), lambda i: (0, 0)), _row_spec(GLA_DK), _row_spec(GLA_DV)],
        out_specs=[pl.BlockSpec((TB, GLA_IN_PAD), rev),
                   pl.BlockSpec((128, GLA_DK), lambda i: (0, 0)), _row_spec(GLA_DK), _row_spec(GLA_DV)],
        out_shape=[jax.ShapeDtypeStruct((S, GLA_IN_PAD), BF16), jax.ShapeDtypeStruct((128, GLA_DK), F32),
                   jax.ShapeDtypeStruct((1, GLA_DK), F32), jax.ShapeDtypeStruct((1, GLA_DV), F32)],
        scratch_shapes=[pltpu.VMEM((GLA_HEADS, GLA_DV_HEAD, GLA_DK_HEAD), F32), pltpu.VMEM((TB, GLA_DK), F32),
                        pltpu.VMEM((TB, GLA_DK), F32), pltpu.VMEM((TB, GLA_DK), F32)],
        compiler_params=_cp(("arbitrary",)),
    )(proj, states, dzg, wgk_p, bgk, gnorm)


ATT_TW = 1024
ATT_CLASSES = 3


def _att_window(i):
    return pl.multiple_of(jnp.maximum(i * ATT_TQ - LEFT_CHUNKS * CHUNK, 0), ATT_TQ)


def _att_rel_index():
    e = jnp.arange(ATT_TW)[None, :]
    d = jnp.where(e < ATT_KW, e, e - ATT_TW)
    off = (jnp.arange(ATT_CLASSES) * ATT_TQ)[:, None]
    return jnp.clip(off - d, -MAX_REL, MAX_REL) + MAX_REL


def _row_bits():
    return lax.broadcasted_iota(jnp.int32, (ATT_TQ, ATT_TW), 0)


def att_bias_tiles(rel_bias, name):
    tab = jnp.take(rel_bias, _att_rel_index(), axis=1).reshape(ATT_HEADS * ATT_CLASSES, 1, ATT_TW)

    def kern(t_ref, o_ref):
        cls = pl.program_id(0) % ATT_CLASSES
        x = jnp.broadcast_to(t_ref[...], (ATT_TQ, ATT_TW))
        row = _row_bits()
        for b in range(8):
            x = jnp.where((row & (1 << b)) != 0, pltpu.roll(x, 1 << b, axis=1), x)
        x = x[:, :ATT_KW]
        qc = cls * (ATT_TQ // CHUNK) + lax.shift_right_arithmetic(
            lax.broadcasted_iota(jnp.int32, (ATT_TQ, ATT_KW), 0), 6)
        kc = lax.shift_right_arithmetic(lax.broadcasted_iota(jnp.int32, (ATT_TQ, ATT_KW), 1), 6)
        o_ref[...] = jnp.where((kc <= qc) & (kc >= qc - LEFT_CHUNKS), x, NEG_INF)

    return pl.pallas_call(
        kern, name=name, grid=(ATT_HEADS * ATT_CLASSES,),
        in_specs=[pl.BlockSpec((None, 1, ATT_TW), lambda i: (i, 0, 0))],
        out_specs=pl.BlockSpec((None, ATT_TQ, ATT_KW), lambda i: (i, 0, 0)),
        out_shape=jax.ShapeDtypeStruct((ATT_HEADS * ATT_CLASSES, ATT_TQ, ATT_KW), F32),
        compiler_params=_cp(("parallel",)),
    )(tab)


def att_bias_grad(dbt, name):
    def kern(d_ref, o_ref):
        x = jnp.concatenate([d_ref[...], jnp.zeros((ATT_TQ, ATT_TW - ATT_KW), F32)], axis=1)
        row = _row_bits()
        for b in range(8):
            x = jnp.where((row & (1 << b)) != 0, pltpu.roll(x, ATT_TW - (1 << b), axis=1), x)
        o_ref[...] = jnp.sum(x, axis=0, keepdims=True)

    diag = pl.pallas_call(
        kern, name=name + "_diag", grid=(ATT_HEADS * ATT_CLASSES,),
        in_specs=[pl.BlockSpec((None, ATT_TQ, ATT_KW), lambda i: (i, 0, 0))],
        out_specs=pl.BlockSpec((None, 1, ATT_TW), lambda i: (i, 0, 0)),
        out_shape=jax.ShapeDtypeStruct((ATT_HEADS * ATT_CLASSES, 1, ATT_TW), F32),
        compiler_params=_cp(("parallel",)),
    )(dbt)
    diag = diag.reshape(ATT_HEADS, ATT_CLASSES * ATT_TW)
    onehot = (_att_rel_index().reshape(-1)[:, None] == jnp.arange(384)[None, :]).astype(F32)
    return mm_f32(diag, onehot, name + "_bins")[:, :N_REL]


def _att_scores(q_ref, kw, bias_ref, hh):
    hs = slice(hh * ATT_HD, (hh + 1) * ATT_HD)
    q = q_ref[:, hs] * ATT_SCALE
    k = kw[:, hs]
    s = _dot_nt(q, k) + bias_ref[hh]
    e = jnp.exp(s - jnp.max(s, axis=-1, keepdims=True))
    inv = 1.0 / jnp.sum(e, axis=-1, keepdims=True)
    return q, k, e, inv


def _att_specs(S):
    nq = D_MODEL // 128
    q_spec = pl.BlockSpec((ATT_TQ, 128), lambda p, i: (i, p))
    k_spec = pl.BlockSpec((S, 128), lambda p, i: (0, nq + p))
    v_spec = pl.BlockSpec((S, 128), lambda p, i: (0, 2 * nq + p))
    b_spec = pl.BlockSpec((2, None, ATT_TQ, ATT_KW), lambda p, i: (p, jnp.minimum(i, ATT_CLASSES - 1), 0, 0))
    return q_spec, k_spec, v_spec, b_spec


def attn_fwd(qkv, bias, name):
    S = qkv.shape[0]
    q_spec, k_spec, v_spec, b_spec = _att_specs(S)

    def kern(q_ref, k_ref, v_ref, bias_ref, o_ref):
        ws = _att_window(pl.program_id(1))
        kw = k_ref[pl.ds(ws, ATT_KW), :]
        vw = v_ref[pl.ds(ws, ATT_KW), :]
        outs = []
        for hh in range(2):
            _, _, e, inv = _att_scores(q_ref, kw, bias_ref, hh)
            outs.append(_dot(e.astype(BF16), vw[:, hh * ATT_HD:(hh + 1) * ATT_HD]) * inv)
        o_ref[...] = jnp.concatenate(outs, axis=1).astype(BF16)

    return pl.pallas_call(
        kern, name=name, grid=(ATT_HEADS // 2, S // ATT_TQ),
        in_specs=[q_spec, k_spec, v_spec, b_spec],
        out_specs=pl.BlockSpec((ATT_TQ, 128), lambda p, i: (i, p)),
        out_shape=jax.ShapeDtypeStruct((S, D_MODEL), BF16),
        compiler_params=_cp(("parallel", "arbitrary")),
    )(qkv, qkv, qkv, bias)


def attn_bwd(qkv, bias, do, name):
    S = qkv.shape[0]
    nblk = S // ATT_TQ
    q_spec, k_spec, v_spec, b_spec = _att_specs(S)

    def kern(q_ref, k_ref, v_ref, bias_ref, do_ref, dqkv_ref, db_ref, dk_scr, dv_scr):
        i = pl.program_id(1)

        @pl.when(i == 0)
        def _():
            dk_scr[...] = jnp.zeros_like(dk_scr)
            dv_scr[...] = jnp.zeros_like(dv_scr)
            db_ref[...] = jnp.zeros_like(db_ref)
        ws = _att_window(i)
        win = pl.ds(ws, ATT_KW)
        kw = k_ref[win, :]
        vw = v_ref[win, :]
        o_cls = jnp.minimum(i, ATT_CLASSES - 1)
        dqs, dks, dvs = [], [], []
        for hh in range(2):
            hs = slice(hh * ATT_HD, (hh + 1) * ATT_HD)
            q, k, e, inv = _att_scores(q_ref, kw, bias_ref, hh)
            p = e * inv
            do_h = do_ref[:, hs]
            dp = _dot_nt(do_h, vw[:, hs])
            ds = p * (dp - jnp.sum(p * dp, axis=-1, keepdims=True))
            db_ref[hh, o_cls] += ds
            ds_b = ds.astype(BF16)
            dqs.append(_dot(ds_b, k) * ATT_SCALE)
            dks.append(_dot_tn(ds_b, q))
            dvs.append(_dot_tn(p.astype(BF16), do_h))
        dqkv_ref[0, pl.ds(pl.multiple_of(i * ATT_TQ, ATT_TQ), ATT_TQ), :] = jnp.concatenate(dqs, axis=1).astype(BF16)
        dk_scr[win, :] += jnp.concatenate(dks, axis=1)
        dv_scr[win, :] += jnp.concatenate(dvs, axis=1)

        @pl.when(i == nblk - 1)
        def _():
            dqkv_ref[1] = dk_scr[...].astype(BF16)
            dqkv_ref[2] = dv_scr[...].astype(BF16)

    return pl.pallas_call(
        kern, name=name, grid=(ATT_HEADS // 2, nblk),
        in_specs=[q_spec, k_spec, v_spec, b_spec, pl.BlockSpec((ATT_TQ, 128), lambda p, i: (i, p))],
        out_specs=[pl.BlockSpec((3, S, 128), lambda p, i: (0, 0, p)),
                   pl.BlockSpec((2, ATT_CLASSES, ATT_TQ, ATT_KW), lambda p, i: (p, 0, 0, 0))],
        out_shape=[jax.ShapeDtypeStruct((3, S, D_MODEL), BF16),
                   jax.ShapeDtypeStruct((ATT_HEADS, ATT_CLASSES, ATT_TQ, ATT_KW), F32)],
        scratch_shapes=[pltpu.VMEM((S, 128), F32), pltpu.VMEM((S, 128), F32)],
        compiler_params=_cp(("parallel", "arbitrary")),
    )(qkv, qkv, qkv, bias, do)


def colsum3(a3, name):
    P, S, N = a3.shape
    tm = min(512, S)

    def kern(a_ref, o_ref):
        @pl.when(pl.program_id(1) == 0)
        def _():
            o_ref[...] = jnp.zeros_like(o_ref)
        o_ref[...] += jnp.sum(a_ref[...].astype(F32), axis=0, keepdims=True)

    return pl.pallas_call(
        kern, name=name, grid=(P, S // tm),
        in_specs=[pl.BlockSpec((None, tm, N), lambda p, i: (p, i, 0))],
        out_specs=pl.BlockSpec((None, 1, N), lambda p, i: (p, 0, 0)),
        out_shape=jax.ShapeDtypeStruct((P, 1, N), F32),
        compiler_params=_cp(("parallel", "arbitrary")),
    )(a3)


def _me():
    return lax.axis_index("x"), lax.axis_index("y"), lax.axis_index("c")


def _other_chips(x, y):
    return [(1 - x, y), (x, 1 - y), (1 - x, 1 - y)]


def all_gather8(x_shard, name):
    m_per, n = x_shard.shape

    def body(x_ref, out_ref, send_sems, recv_sems, local_sem):
        x, y, c = _me()
        me, sibling = (x, y, c), (x, y, 1 - c)
        chips = _other_chips(x, y)

        def rows(px, py, pc):
            return out_ref.at[pl.ds((4 * px + 2 * py + pc) * m_per, m_per), :]

        def copy(k, block, to, src=None):
            return pltpu.make_async_remote_copy(
                src_ref=rows(*block) if src is None else src, dst_ref=rows(*block),
                send_sem=send_sems.at[k], recv_sem=recv_sems.at[k], device_id=to, device_id_type=MESH)

        mine = pltpu.make_async_copy(x_ref, rows(*me), local_sem)
        mine.start()
        first = [copy(0, me, sibling, src=x_ref)]
        first += [copy(1 + j, me, (*chip, c), src=x_ref) for j, chip in enumerate(chips)]
        for cp in first:
            cp.start()
        passed = [copy(4 + j, (*chip, c), sibling) for j, chip in enumerate(chips)]
        for j, chip in enumerate(chips):
            copy(1 + j, (*chip, c), me).wait_recv()
            passed[j].start()
        copy(0, sibling, me).wait_recv()
        for j, chip in enumerate(chips):
            copy(4 + j, (*chip, 1 - c), me).wait_recv()
        for cp in first + passed:
            cp.wait_send()
        mine.wait()

    return pl.pallas_call(
        body, name=name,
        out_shape=jax.ShapeDtypeStruct((N_DEV * m_per, n), x_shard.dtype),
        in_specs=[pl.BlockSpec(memory_space=pltpu.VMEM)],
        out_specs=pl.BlockSpec(memory_space=pltpu.VMEM),
        scratch_shapes=[pltpu.SemaphoreType.DMA((7,)), pltpu.SemaphoreType.DMA((7,)), pltpu.SemaphoreType.DMA],
        compiler_params=pltpu.CompilerParams(vmem_limit_bytes=VMEM_LIMIT),
    )(x_shard)


def _half_rows(n_rows, c):
    h = n_rows // 2
    return pl.ds(c * h, h)


def _gathered_shape(shape, flavour):
    L, a, b = shape
    return {"col": (L, a, N_CHIPS * b), "row": (L, N_CHIPS * a, b), "lead": (N_CHIPS, L, a, b)}[flavour]


def _gathered_part(out_ref, shape, flavour, s, rows):
    L, a, b = shape
    if flavour == "col":
        return out_ref.at[:, rows, pl.ds(s * b, b)]
    if flavour == "row":
        return out_ref.at[:, pl.ds(s * a + rows.start, rows.size), :]
    return out_ref.at[s, :, rows, :]


def all_gather_weights(shards, flavours, name):
    n = len(shards)
    shapes = [w.shape for w in shards]

    def body(*refs):
        w_refs, out_refs = refs[:n], refs[n:2 * n]
        send_sems, recv_sems, local_sems = refs[2 * n:]
        x, y, c = _me()
        sibling = (x, y, 1 - c)
        chips = _other_chips(x, y)
        me_s = 2 * x + y

        def copy(k, src, dst, to):
            return pltpu.make_async_remote_copy(src_ref=src, dst_ref=dst, send_sem=send_sems.at[k],
                                                recv_sem=recv_sems.at[k], device_id=to, device_id_type=MESH)

        local, first, passed = [], [], []
        for w in range(n):
            shp, fl = shapes[w], flavours[w]
            rows_all = pl.ds(0, shp[1])
            my_half = _half_rows(shp[1], c)
            cp = pltpu.make_async_copy(w_refs[w], _gathered_part(out_refs[w], shp, fl, me_s, rows_all), local_sems.at[w])
            cp.start()
            local.append(cp)
            for j, chip in enumerate(chips):
                cp = copy(6 * w + j, w_refs[w].at[:, my_half, :], _gathered_part(out_refs[w], shp, fl, me_s, my_half),
                          (*chip, c))
                cp.start()
                first.append(cp)
        for w in range(n):
            shp, fl = shapes[w], flavours[w]
            my_half = _half_rows(shp[1], c)
            for j, chip in enumerate(chips):
                part = _gathered_part(out_refs[w], shp, fl, 2 * chip[0] + chip[1], my_half)
                copy(6 * w + j, part, part, (*chip, c)).wait_recv()
                cp = copy(6 * w + 3 + j, part, part, sibling)
                cp.start()
                passed.append(cp)
        for w in range(n):
            shp, fl = shapes[w], flavours[w]
            sib_half = _half_rows(shp[1], 1 - c)
            for j, chip in enumerate(chips):
                part = _gathered_part(out_refs[w], shp, fl, 2 * chip[0] + chip[1], sib_half)
                copy(6 * w + 3 + j, part, part, sibling).wait_recv()
        for cp in first + passed:
            cp.wait_send()
        for cp in local:
            cp.wait()

    any_spec = pl.BlockSpec(memory_space=pl.ANY)
    return pl.pallas_call(
        body, name=name,
        out_shape=[jax.ShapeDtypeStruct(_gathered_shape(s, f), w.dtype) for w, s, f in zip(shards, shapes, flavours)],
        in_specs=[any_spec] * n, out_specs=[any_spec] * n,
        scratch_shapes=[pltpu.SemaphoreType.DMA((6 * n,)), pltpu.SemaphoreType.DMA((6 * n,)),
                        pltpu.SemaphoreType.DMA((n,))],
    )(*shards)


def swap_halves(gs, name):
    n = len(gs)

    def body(*refs):
        g_refs, out_refs = refs[:n], refs[n:2 * n]
        send_sems, recv_sems = refs[2 * n:]
        x, y, c = _me()
        cps = [pltpu.make_async_remote_copy(
            src_ref=g_refs[w].at[:, _half_rows(gs[w].shape[1], 1 - c), :], dst_ref=out_refs[w],
            send_sem=send_sems.at[w], recv_sem=recv_sems.at[w], device_id=(x, y, 1 - c), device_id_type=MESH)
            for w in range(n)]
        for cp in cps:
            cp.start()
        for cp in cps:
            cp.wait()

    any_spec = pl.BlockSpec(memory_space=pl.ANY)
    return pl.pallas_call(
        body, name=name,
        out_shape=[jax.ShapeDtypeStruct((g.shape[0], g.shape[1] // 2, g.shape[2]), g.dtype) for g in gs],
        in_specs=[any_spec] * n, out_specs=[any_spec] * n,
        scratch_shapes=[pltpu.SemaphoreType.DMA((n,)), pltpu.SemaphoreType.DMA((n,))],
    )(*gs)


def add_half(g, r1, c_idx, name):
    n, R, C = g.shape
    half = R // 2
    tr = _rows_block(half, C)
    nbh = half // tr

    def kern(c_ref, g_ref, r_ref, o_ref):
        o_ref[...] = g_ref[...] + r_ref[...]

    return pl.pallas_call(
        kern, name=name,
        grid_spec=pltpu.PrefetchScalarGridSpec(
            num_scalar_prefetch=1, grid=(n, nbh),
            in_specs=[pl.BlockSpec((1, tr, C), lambda d, r, c_ref: (d, c_ref[0] * nbh + r, 0)),
                      pl.BlockSpec((1, tr, C), lambda d, r, c_ref: (d, r, 0))],
            out_specs=pl.BlockSpec((1, tr, C), lambda d, r, c_ref: (d, r, 0))),
        out_shape=jax.ShapeDtypeStruct((n, half, C), F32),
        compiler_params=_cp(("parallel", "parallel")),
    )(c_idx, g, r1)


def exchange_chips(ps, name):
    n = len(ps)

    def body(*refs):
        p_refs, out_refs = refs[:n], refs[n:2 * n]
        send_sems, recv_sems = refs[2 * n:]
        x, y, c = _me()
        chips = _other_chips(x, y)
        cps = [pltpu.make_async_remote_copy(
            src_ref=p_refs[w].at[2 * chip[0] + chip[1]], dst_ref=out_refs[w].at[j],
            send_sem=send_sems.at[3 * w + j], recv_sem=recv_sems.at[3 * w + j],
            device_id=(*chip, c), device_id_type=MESH)
            for w in range(n) for j, chip in enumerate(chips)]
        for cp in cps:
            cp.start()
        for cp in cps:
            cp.wait()

    any_spec = pl.BlockSpec(memory_space=pl.ANY)
    return pl.pallas_call(
        body, name=name,
        out_shape=[jax.ShapeDtypeStruct((3,) + p.shape[1:], p.dtype) for p in ps],
        in_specs=[any_spec] * n, out_specs=[any_spec] * n,
        scratch_shapes=[pltpu.SemaphoreType.DMA((3 * n,)), pltpu.SemaphoreType.DMA((3 * n,))],
    )(*ps)


def add_chips(p, r2, chip_idx, name):
    n, H, C = p.shape
    tr = _rows_block(H, C)

    def kern(s_ref, p_ref, r_ref, o_ref):
        o_ref[...] = ((p_ref[0] + r_ref[0]) + r_ref[1]) + r_ref[2]

    return pl.pallas_call(
        kern, name=name,
        grid_spec=pltpu.PrefetchScalarGridSpec(
            num_scalar_prefetch=1, grid=(H // tr,),
            in_specs=[pl.BlockSpec((1, tr, C), lambda r, s_ref: (s_ref[0], r, 0)),
                      pl.BlockSpec((3, tr, C), lambda r, s_ref: (0, r, 0))],
            out_specs=pl.BlockSpec((tr, C), lambda r, s_ref: (r, 0))),
        out_shape=jax.ShapeDtypeStruct((H, C), F32),
        compiler_params=_cp(("parallel",)),
    )(chip_idx, p, r2)


def join_halves(ss, groups, name):
    n = len(ss)
    where = [(gi, l) for gi, (L, _, _) in enumerate(groups) for l in range(L)]
    assert len(where) == n

    def body(*refs):
        s_refs, out_refs = refs[:n], refs[n:n + len(groups)]
        send_sems, recv_sems, local_sems = refs[n + len(groups):]
        x, y, c = _me()
        local, remote = [], []
        for w in range(n):
            gi, l = where[w]
            mine = out_refs[gi].at[l, _half_rows(groups[gi][1], c), :]
            cp = pltpu.make_async_copy(s_refs[w], mine, local_sems.at[w])
            cp.start()
            local.append(cp)
            cp = pltpu.make_async_remote_copy(src_ref=s_refs[w], dst_ref=mine, send_sem=send_sems.at[w],
                                              recv_sem=recv_sems.at[w], device_id=(x, y, 1 - c), device_id_type=MESH)
            cp.start()
            remote.append(cp)
        for w in range(n):
            gi, l = where[w]
            theirs = out_refs[gi].at[l, _half_rows(groups[gi][1], 1 - c), :]
            remote[w].wait_send()
            pltpu.make_async_remote_copy(src_ref=s_refs[w], dst_ref=theirs, send_sem=send_sems.at[w],
                                         recv_sem=recv_sems.at[w], device_id=(x, y, 1 - c),
                                         device_id_type=MESH).wait_recv()
            local[w].wait()

    any_spec = pl.BlockSpec(memory_space=pl.ANY)
    return pl.pallas_call(
        body, name=name,
        out_shape=[jax.ShapeDtypeStruct(g, F32) for g in groups],
        in_specs=[any_spec] * n, out_specs=[any_spec] * len(groups),
        scratch_shapes=[pltpu.SemaphoreType.DMA((n,)), pltpu.SemaphoreType.DMA((n,)), pltpu.SemaphoreType.DMA((n,))],
    )(*ss)


def reduce_scatter_chips(gs, groups, c_idx, chip_idx, name):
    r1 = swap_halves(gs, name + "_swap")
    ps = [add_half(g, r, c_idx, f"{name}_add2_{w}") for w, (g, r) in enumerate(zip(gs, r1))]
    r2 = exchange_chips(ps, name + "_xchg")
    ss = [add_chips(p, r, chip_idx, f"{name}_add4_{w}") for w, (p, r) in enumerate(zip(ps, r2))]
    return join_halves(ss, groups, name + "_join")


BIG = (("gla_w_in", 2, (1024, GLA_IN // N_CHIPS), "lead"), ("gla_w_out", 2, (256, 1024), "row"),
       ("att_w_in", 2, (1024, 768), "col"), ("att_w_out", 2, (256, 1024), "row"),
       ("ff_w1", 4, (1024, 1024), "col"), ("ff_w2", 4, (1024, 1024), "row"))


def local_step(x, target, mods, W, small):
    S, D = x.shape
    row = lambda v: v.reshape(1, -1)
    saved = []
    tiles = [att_bias_tiles(small["att_rel_bias"][j], f"att_tiles_{j}").reshape(ATT_HEADS, ATT_CLASSES, ATT_TQ, ATT_KW)
             for j in range(2)]
    wgk_p = [jnp.pad(small["gla_w_gk2"][j], ((0, 128 - GLA_RANK), (0, 0))).astype(BF16) for j in range(2)]

    u1 = modulate(x, row(mods[0, 1]), row(mods[0, 0]), "mod_first")
    for i in range(DEPTH):
        j = i // 2
        sh1, sc1, g1, sh2, sc2, g2 = (row(mods[i, k]) for k in range(6))
        nxt = min(i + 1, DEPTH - 1)
        if i % 2 == 0:
            proj = mm_plain(u1, W["gla_w_in"], j, f"gla_in_{i}")
            zmix, states = gla_fwd(proj, wgk_p[j], row(small["gla_b_gk"][j]), row(small["gla_g_norm"][j]),
                                   f"gla_fwd_{i}")
            w_out = W["gla_w_out"]
        else:
            proj = mm_plain(u1, W["att_w_in"], j, f"att_in_{i}", mode="bf16", bias=row(small["att_b_in"][j]))
            zmix = attn_fwd(proj, tiles[j], f"att_fwd_{i}")
            states = None
            w_out = W["att_w_out"]
        y1, x_mid, u2 = mm_down_ln(zmix, w_out, j, x, 1.0 + g1, row(small["ln_g"][i, 0]), row(small["ln_b"][i, 0]),
                                   sc2, sh2, f"mix_out_{i}")
        act = mm_plain(u2, W["ff_w1"], i, f"ff_up_{i}", mode="mlp_up")
        y2, x_out, u_next = mm_down_ln(act, W["ff_w2"], i, x_mid, 1.0 + g2, row(small["ln_g"][i, 1]),
                                       row(small["ln_b"][i, 1]), row(mods[nxt, 1]), row(mods[nxt, 0]), f"ff_out_{i}")
        saved.append(dict(x_in=x, u1=u1, proj=proj, zmix=zmix, states=states, y1=y1, x_mid=x_mid, u2=u2,
                          act=act, y2=y2))
        x, u1 = x_out, u_next

    dx, sq = loss_head(x, target, "loss_head")

    gW = {n: [None] * L for n, L, *_ in BIG}
    g_small = dict(ln_g=[None] * DEPTH, ln_b=[None] * DEPTH, gla_w_gk2=[None] * 2, gla_b_gk=[None] * 2,
                   gla_g_norm=[None] * 2, att_b_in=[None] * 2, att_rel_bias=[None] * 2)
    dmods = [None] * DEPTH

    for i in reversed(range(DEPTH)):
        j = i // 2
        sv = saved[i]
        sh1, sc1, g1, sh2, sc2, g2 = (row(mods[i, k]) for k in range(6))
        dz2, dy2, s_ln2 = ln_bwd(dx, sv["x_mid"], sv["y2"], 1.0 + g2, row(small["ln_g"][i, 1]), f"ln2_bwd_{i}")
        dh = mm_plain(dy2, W["ff_w2"], i, f"ff_dn_{i}", mode="mlp_dn", nt=True, h=sv["act"])
        gW["ff_w2"][i] = mm_w(sv["act"], dy2, f"ff_w2g_{i}").reshape(N_CHIPS, D_FF // N_CHIPS, D)
        gW["ff_w1"][i] = mm_w(sv["u2"], dh, f"ff_w1g_{i}", chips_out=True)
        dx_mid, s_m2 = mm_down_comb(dh, W["ff_w1"], i, dz2, sv["x_mid"], 1.0 + sc2, f"ff_dx_{i}")
        dz1, dy1, s_ln1 = ln_bwd(dx_mid, sv["x_in"], sv["y1"], 1.0 + g1, row(small["ln_g"][i, 0]), f"ln1_bwd_{i}")
        if i % 2 == 0:
            gW["gla_w_out"][j] = mm_w(sv["zmix"], dy1, f"gla_wog_{i}").reshape(N_CHIPS, D // N_CHIPS, D)
            dzg = mm_plain(dy1, W["gla_w_out"], j, f"gla_dz_{i}", nt=True)
            dproj, dwgk, dbgk, dgn = gla_bwd(sv["proj"], sv["states"], dzg, wgk_p[j], row(small["gla_b_gk"][j]),
                                             row(small["gla_g_norm"][j]), f"gla_bwd_{i}")
            g_small["gla_w_gk2"][j] = dwgk[:GLA_RANK]
            g_small["gla_b_gk"][j] = dbgk[0]
            g_small["gla_g_norm"][j] = dgn[0].reshape(GLA_HEADS, GLA_DV_HEAD)
            gwi = mm_w(sv["u1"], dproj, f"gla_wig_{i}")[:, :GLA_IN]
            gW["gla_w_in"][j] = gwi.reshape(D, N_CHIPS, GLA_IN // N_CHIPS).transpose(1, 0, 2)
            dx, s_m1 = mm_down_comb(dproj, W["gla_w_in"], j, dz1, sv["x_in"], 1.0 + sc1, f"mix_dx_{i}")
        else:
            gW["att_w_out"][j] = mm_w(sv["zmix"], dy1, f"att_wog_{i}").reshape(N_CHIPS, D // N_CHIPS, D)
            do = mm_plain(dy1, W["att_w_out"], j, f"att_do_{i}", mode="bf16", nt=True)
            dqkv, dbt = attn_bwd(sv["proj"], tiles[j], do, f"att_bwd_{i}")
            g_small["att_rel_bias"][j] = att_bias_grad(dbt.reshape(ATT_HEADS * ATT_CLASSES, ATT_TQ, ATT_KW),
                                                       f"att_bias_{i}")
            g_small["att_b_in"][j] = colsum3(dqkv, f"att_bin_{i}").reshape(3 * D)
            gW["att_w_in"][j] = mm_w_chips3(sv["u1"], dqkv, f"att_wig_{i}")
            dx, s_m1 = mm_down_comb(dqkv, W["att_w_in"], j, dz1, sv["x_in"], 1.0 + sc1, f"mix_dx_{i}", parts=3)
        g_small["ln_g"][i] = jnp.stack([s_ln1[0], s_ln2[0]])
        g_small["ln_b"][i] = jnp.stack([s_ln1[1], s_ln2[1]])
        dmods[i] = jnp.stack([s_m1[1], s_m1[0], s_ln1[2], s_m2[1], s_m2[0], s_ln2[2]])

    g_small = {n: jnp.stack(v) for n, v in g_small.items()}
    return sq, dx, gW, jnp.stack(dmods), g_small


SMALL_SHARDED = (("ln_g", (4, 2, 256)), ("ln_b", (4, 2, 256)), ("gla_g_norm", (2, 4, 64)),
                 ("gla_w_gk2", (2, 16, 128)), ("att_b_in", (2, 768)))
SMALL_FULL = dict(ln_g=(4, 2, 1024), ln_b=(4, 2, 1024), gla_g_norm=(2, 4, 256), gla_w_gk2=(2, 16, 512),
                  att_b_in=(2, 3072), gla_b_gk=(2, 512), att_rel_bias=(2, 16, 257))
SMALL_GRAD_ORDER = ("ln_g", "ln_b", "gla_g_norm", "gla_w_gk2", "att_b_in", "gla_b_gk", "att_rel_bias")


def _pack_small(arrs, rows_total):
    parts = []
    for a in arrs:
        flat = a.reshape(-1)
        pad = (-flat.shape[0]) % PACK_W
        parts.append(jnp.pad(flat, (0, pad)).reshape(-1, PACK_W))
    buf = jnp.concatenate(parts, axis=0)
    return jnp.pad(buf, ((0, rows_total - buf.shape[0]), (0, 0)))


def _unpack_small(buf, shapes):
    out, r = [], 0
    for shp in shapes:
        n = 1
        for s in shp:
            n *= s
        nr = (n + PACK_W - 1) // PACK_W
        out.append(buf[..., r:r + nr, :].reshape(buf.shape[:-2] + (nr * PACK_W,))[..., :n].reshape(buf.shape[:-2] + shp))
        r += nr
    return out


def _unshard_last(g4):
    nd = g4.ndim
    perm = tuple(range(1, nd - 1)) + (0, nd - 1)
    t = g4.transpose(perm)
    return t.reshape(t.shape[:-2] + (-1,))


def _shard_last(full, s):
    n = full.shape[-1] // N_CHIPS
    return lax.dynamic_slice_in_dim(full, s * n, n, axis=full.ndim - 1)


WEIGHT_NAMES = ("w_ada", "b_ada", "ln_g", "ln_b", "gla_w_in", "gla_w_gk2", "gla_b_gk", "gla_g_norm", "gla_w_out",
                "att_w_in", "att_b_in", "att_rel_bias", "att_w_out", "ff_w1", "ff_w2")


def kernel(x, c, w_ada, b_ada, ln_g, ln_b, gla_w_in, gla_w_gk2, gla_b_gk, gla_g_norm, gla_w_out, att_w_in, att_b_in, att_rel_bias, att_w_out, ff_w1, ff_w2, loss_target, m_w_ada, m_b_ada, m_ln_g, m_ln_b, m_gla_w_in, m_gla_w_gk2, m_gla_b_gk, m_gla_g_norm, m_gla_w_out, m_att_w_in, m_att_b_in, m_att_rel_bias, m_att_w_out, m_ff_w1, m_ff_w2, v_w_ada, v_b_ada, v_ln_g, v_ln_b, v_gla_w_in, v_gla_w_gk2, v_gla_b_gk, v_gla_g_norm, v_gla_w_out, v_att_w_in, v_att_b_in, v_att_rel_bias, v_att_w_out, v_ff_w1, v_ff_w2):
    weights = dict(w_ada=w_ada, b_ada=b_ada, ln_g=ln_g, ln_b=ln_b, gla_w_in=gla_w_in, gla_w_gk2=gla_w_gk2,
                   gla_b_gk=gla_b_gk, gla_g_norm=gla_g_norm, gla_w_out=gla_w_out, att_w_in=att_w_in,
                   att_b_in=att_b_in, att_rel_bias=att_rel_bias, att_w_out=att_w_out, ff_w1=ff_w1, ff_w2=ff_w2)
    mom1 = dict(w_ada=m_w_ada, b_ada=m_b_ada, ln_g=m_ln_g, ln_b=m_ln_b, gla_w_in=m_gla_w_in, gla_w_gk2=m_gla_w_gk2,
                gla_b_gk=m_gla_b_gk, gla_g_norm=m_gla_g_norm, gla_w_out=m_gla_w_out, att_w_in=m_att_w_in,
                att_b_in=m_att_b_in, att_rel_bias=m_att_rel_bias, att_w_out=m_att_w_out, ff_w1=m_ff_w1, ff_w2=m_ff_w2)
    mom2 = dict(w_ada=v_w_ada, b_ada=v_b_ada, ln_g=v_ln_g, ln_b=v_ln_b, gla_w_in=v_gla_w_in, gla_w_gk2=v_gla_w_gk2,
                gla_b_gk=v_gla_b_gk, gla_g_norm=v_gla_g_norm, gla_w_out=v_gla_w_out, att_w_in=v_att_w_in,
                att_b_in=v_att_b_in, att_rel_bias=v_att_rel_bias, att_w_out=v_att_w_out, ff_w1=v_ff_w1, ff_w2=v_ff_w2)

    ax, ay, ac = lax.axis_index("x"), lax.axis_index("y"), lax.axis_index("c")
    chip = 2 * ax + ay
    dev = 2 * chip + ac
    c_idx = jnp.reshape(ac, (1,)).astype(jnp.int32)
    chip_idx = jnp.reshape(chip, (1,)).astype(jnp.int32)
    S = x.shape[1]
    x2 = x.reshape(S, D_MODEL)
    t2 = loss_target.reshape(S, D_MODEL)

    full = all_gather_weights([weights[n].astype(BF16) for n, *_ in BIG], [f for *_, f in BIG], "gather_weights")
    W = {n: w for (n, *_), w in zip(BIG, full)}
    W["gla_w_in"] = jnp.pad(W["gla_w_in"].transpose(1, 2, 0, 3).reshape(2, D_MODEL, GLA_IN),
                            ((0, 0), (0, 0), (0, GLA_IN_PAD - GLA_IN)))

    small_rows = 16
    spack = _pack_small([c] + [weights[n] for n, _ in SMALL_SHARDED], small_rows)
    sg = all_gather8(spack, "gather_small").reshape(N_DEV, small_rows, PACK_W)
    parts = _unpack_small(sg, [(1, D_MODEL)] + [shp for _, shp in SMALL_SHARDED])
    c_all = parts[0].reshape(N_DEV, D_MODEL)
    small = {n: _unshard_last(p[0::2]) for (n, _), p in zip(SMALL_SHARDED, parts[1:])}
    small["gla_b_gk"] = gla_b_gk
    small["att_rel_bias"] = att_rel_bias

    c_act = silu_rows(jnp.pad(c_all, ((0, 128 - N_DEV), (0, 0))), "silu_c")
    wa = w_ada.astype(BF16).transpose(1, 0, 2).reshape(1, D_MODEL, DEPTH * 6 * D_MODEL // N_CHIPS)
    mods_part = mm_plain(c_act, wa, 0, "ada_fwd", tm=128)[:N_DEV]
    mg = all_gather8(mods_part, "gather_mods").reshape(N_CHIPS, 2, N_DEV, DEPTH, 6 * D_MODEL // N_CHIPS)
    mods_mine = lax.dynamic_index_in_dim(mg[:, 0], dev, axis=1, keepdims=False)
    mods = mods_mine.transpose(1, 0, 2).reshape(DEPTH, 6 * D_MODEL) + b_ada
    mods = mods.reshape(DEPTH, 6, D_MODEL)

    sq, grad_x, gW, dmods, g_small = local_step(x2, t2, mods, W, small)
    loss = lax.psum(0.5 * sq[0, 0] / D_MODEL, ("x", "y", "c"))

    gs = [g for n, *_ in BIG for g in gW[n]]
    groups = [(L,) + shp for _, L, shp, _ in BIG]
    reduced = reduce_scatter_chips(gs, groups, c_idx, chip_idx, "rs")
    g_shard = {n: r for (n, *_), r in zip(BIG, reduced)}

    dm_flat = dmods.reshape(DEPTH, 6 * D_MODEL)
    g_rows = 80
    gpack = _pack_small([dm_flat] + [g_small[n] for n in SMALL_GRAD_ORDER], g_rows)
    gg = all_gather8(gpack, "gather_small_grads").reshape(N_DEV, g_rows, PACK_W)
    gsum = sum_over_devices(gg, "sum_small_grads")
    sums = _unpack_small(gsum, [(DEPTH, 6 * D_MODEL)] + [SMALL_FULL[n] for n in SMALL_GRAD_ORDER])
    grads = dict(b_ada=sums[0])
    for n, full_g in zip(SMALL_GRAD_ORDER, sums[1:]):
        grads[n] = full_g if n in ("gla_b_gk", "att_rel_bias") else _shard_last(full_g, chip)
    dm_all = _unpack_small(gg, [(DEPTH, 6 * D_MODEL)])[0]
    dm_cols = _shard_last(dm_all, chip).reshape(N_DEV, DEPTH * 6 * D_MODEL // N_CHIPS)
    dm_cols = jnp.pad(dm_cols, ((0, 128 - N_DEV), (0, 0))).astype(BF16)
    gwa = mm_w(c_act, dm_cols, "ada_bwd", ts=128)
    grads["w_ada"] = gwa.reshape(D_MODEL, DEPTH, 6 * D_MODEL // N_CHIPS).transpose(1, 0, 2)
    grads.update(g_shard)

    deltas, new_m, new_v = {}, {}, {}
    for n in WEIGHT_NAMES:
        deltas[n], new_m[n], new_v[n] = adamw(weights[n], grads[n], mom1[n], mom2[n], "adamw_" + n)

    return (loss, grad_x.reshape(1, S, D_MODEL), *[grads[n] for n in WEIGHT_NAMES], *[deltas[n] for n in WEIGHT_NAMES],
            *[new_m[n] for n in WEIGHT_NAMES], *[new_v[n] for n in WEIGHT_NAMES])
```

```python
import functools

import jax
import jax.numpy as jnp
from jax import lax
from jax.experimental import pallas as pl
from jax.experimental.pallas import tpu as pltpu

F32 = jnp.float32
BF16 = jnp.bfloat16
HIGHEST = lax.Precision.HIGHEST
MESH = pl.DeviceIdType.MESH

D_MODEL = 1024
DEPTH = 4
CHUNK = 64
GLA_HEADS = 4
GLA_DK = 512
GLA_DV = 1024
GLA_DK_HEAD = 128
GLA_DV_HEAD = 256
GLA_RANK = 16
GLA_IN = 3088
GLA_IN_PAD = 3200
GLA_LR_OFF = 3072
ATT_HEADS = 16
ATT_HD = 64
LEFT_CHUNKS = 8
MAX_REL = 128
N_REL = 257
D_FF = 4096
ALPHA = (2.0 * DEPTH) ** 0.25
LN_EPS = 1e-5
RMS_EPS = 1e-6
NEG_INF = -1e30
GLA_SCALE = GLA_DK_HEAD ** -0.5
ATT_SCALE = ATT_HD ** -0.5
ADAM_LR = 0.001
ADAM_B1 = 0.9
ADAM_B2 = 0.999
ADAM_EPS = 1e-08
ADAM_WD = 0.01
ADAM_STEP = 10

ATT_TQ = 256
ATT_KW = 768
GLA_TB = 256
VMEM_LIMIT = 56 * 1024 * 1024
N_CHIPS = 4
N_DEV = 8
PACK_W = 1024


def _dot(a, b):
    return jnp.dot(a, b, preferred_element_type=F32)


def _dot_nt(a, b):
    return lax.dot_general(a, b, (((1,), (1,)), ((), ())), preferred_element_type=F32)


def _dot_tn(a, b):
    return lax.dot_general(a, b, (((0,), (0,)), ((), ())), preferred_element_type=F32)


def _cp(sem, vmem=VMEM_LIMIT):
    return pltpu.CompilerParams(dimension_semantics=sem, vmem_limit_bytes=vmem)


def _row_spec(n):
    return pl.BlockSpec((1, n), lambda *_: (0, 0))


def _sigmoid(x):
    return 1.0 / (1.0 + jnp.exp(-x))


def _log_sigmoid(x):
    return jnp.minimum(x, 0.0) - jnp.log1p(jnp.exp(-jnp.abs(x)))


class Side:
    def __init__(self, ins, out_shapes, n_sems, n_local, start, wait):
        self.ins, self.out_shapes, self.n_sems, self.n_local = list(ins), list(out_shapes), n_sems, n_local
        self.start, self.wait = start, wait

    def sem_shapes(self):
        return [pltpu.SemaphoreType.DMA((self.n_sems,)), pltpu.SemaphoreType.DMA((self.n_sems,)),
                pltpu.SemaphoreType.DMA((max(self.n_local, 1),))]


def run_side(side, name):
    n_in = len(side.ins)
    n_out = len(side.out_shapes)

    def body(*refs):
        ins, outs, sems = refs[:n_in], refs[n_in:n_in + n_out], refs[n_in + n_out:]
        side.start(ins, outs, *sems)
        side.wait(ins, outs, *sems)

    any_spec = pl.BlockSpec(memory_space=pl.ANY)
    return pl.pallas_call(body, name=name, out_shape=side.out_shapes, in_specs=[any_spec] * n_in,
                          out_specs=[any_spec] * n_out, scratch_shapes=side.sem_shapes())(*side.ins)


def hosted_call(main, side, *, name, grid, in_specs, out_specs, out_shape, scratch_shapes, dims, args):
    if side is None:
        outs = pl.pallas_call(main, name=name, grid=grid, in_specs=in_specs, out_specs=out_specs,
                              out_shape=out_shape, scratch_shapes=scratch_shapes, compiler_params=_cp(dims))(*args)
        return list(outs), []
    n_mi, n_mo, n_ms = len(in_specs), len(out_specs), len(scratch_shapes)
    n_si, n_so = len(side.ins), len(side.out_shapes)

    def kern(*refs):
        mi, si = refs[:n_mi], refs[n_mi:n_mi + n_si]
        o0 = n_mi + n_si
        mo, so = refs[o0:o0 + n_mo], refs[o0 + n_mo:o0 + n_mo + n_so]
        s0 = o0 + n_mo + n_so
        ms, sems = refs[s0:s0 + n_ms], refs[s0 + n_ms:]
        ids = [pl.program_id(d) for d in range(len(grid))]
        first = functools.reduce(jnp.logical_and, [i == 0 for i in ids])
        last = functools.reduce(jnp.logical_and, [i == g - 1 for i, g in zip(ids, grid)])

        @pl.when(first)
        def _():
            side.start(si, so, *sems)
        main(*mi, *mo, *ms)

        @pl.when(last)
        def _():
            side.wait(si, so, *sems)

    any_spec = pl.BlockSpec(memory_space=pl.ANY)
    outs = pl.pallas_call(
        kern, name=name, grid=grid, in_specs=list(in_specs) + [any_spec] * n_si,
        out_specs=list(out_specs) + [any_spec] * n_so, out_shape=list(out_shape) + side.out_shapes,
        scratch_shapes=list(scratch_shapes) + side.sem_shapes(),
        compiler_params=_cp(("arbitrary",) * len(grid)))(*args, *side.ins)
    return list(outs[:n_mo]), list(outs[n_mo:])


def modulate(x, sc, sh, name):
    S, D = x.shape
    tm = min(512, S)

    def kern(x_ref, sc_ref, sh_ref, u_ref):
        u_ref[...] = (x_ref[...] * (1.0 + sc_ref[...]) + sh_ref[...]).astype(BF16)

    return pl.pallas_call(
        kern, name=name, grid=(S // tm,),
        in_specs=[pl.BlockSpec((tm, D), lambda i: (i, 0)), _row_spec(D), _row_spec(D)],
        out_specs=pl.BlockSpec((tm, D), lambda i: (i, 0)),
        out_shape=jax.ShapeDtypeStruct((S, D), BF16),
        compiler_params=_cp(("parallel",)),
    )(x, sc, sh)


def loss_head(x, t, name):
    S, D = x.shape
    tm = min(512, S)

    def kern(x_ref, t_ref, dx_ref, l_ref):
        @pl.when(pl.program_id(0) == 0)
        def _():
            l_ref[...] = jnp.zeros_like(l_ref)
        e = x_ref[...] - t_ref[...]
        dx_ref[...] = e * (1.0 / D)
        l_ref[...] += jnp.sum(e * e)

    return pl.pallas_call(
        kern, name=name, grid=(S // tm,),
        in_specs=[pl.BlockSpec((tm, D), lambda i: (i, 0)), pl.BlockSpec((tm, D), lambda i: (i, 0))],
        out_specs=[pl.BlockSpec((tm, D), lambda i: (i, 0)), pl.BlockSpec((8, 128), lambda i: (0, 0))],
        out_shape=[jax.ShapeDtypeStruct((S, D), F32), jax.ShapeDtypeStruct((8, 128), F32)],
        compiler_params=_cp(("arbitrary",)),
    )(x, t)


def silu_rows(c_all, name):
    def kern(c_ref, o_ref):
        c = c_ref[...]
        o_ref[...] = (c * _sigmoid(c)).astype(BF16)

    return pl.pallas_call(kern, name=name, out_shape=jax.ShapeDtypeStruct(c_all.shape, BF16))(c_all)


def sum_over_devices(g, name):
    n, R, C = g.shape

    def kern(g_ref, o_ref):
        acc = g_ref[0]
        for d in range(1, n):
            acc = acc + g_ref[d]
        o_ref[...] = acc

    return pl.pallas_call(kern, name=name, out_shape=jax.ShapeDtypeStruct((R, C), F32))(g)


def _rows_block(R, C, budget=1 << 20):
    if R * C * 4 <= budget or R % 8:
        return R
    tr = max(8, (budget // (C * 4)) // 8 * 8)
    while R % tr:
        tr -= 8
    return tr


def adamw(w, g, m, v, name):
    shape = w.shape
    C = shape[-1]
    R = w.size // C
    w2, g2, m2, v2 = (t.reshape(R, C) for t in (w, g, m, v))
    tr = _rows_block(R, C)
    c1 = 1.0 - ADAM_B1 ** ADAM_STEP
    c2 = 1.0 - ADAM_B2 ** ADAM_STEP

    def kern(w_ref, g_ref, m_ref, v_ref, d_ref, nm_ref, nv_ref):
        gg = g_ref[...]
        nm = ADAM_B1 * m_ref[...] + (1.0 - ADAM_B1) * gg
        nv = ADAM_B2 * v_ref[...] + (1.0 - ADAM_B2) * (gg * gg)
        m_hat = nm / c1
        v_hat = nv / c2
        d_ref[...] = -ADAM_LR * (m_hat / (jnp.sqrt(v_hat) + ADAM_EPS) + ADAM_WD * w_ref[...])
        nm_ref[...] = nm
        nv_ref[...] = nv

    spec = pl.BlockSpec((tr, C), lambda i: (i, 0))
    outs = pl.pallas_call(
        kern, name=name, grid=(R // tr,),
        in_specs=[spec] * 4, out_specs=[spec] * 3,
        out_shape=[jax.ShapeDtypeStruct((R, C), F32)] * 3,
        compiler_params=_cp(("parallel",)),
    )(w2, g2, m2, v2)
    return tuple(o.reshape(shape) for o in outs)


def _tn_for(N):
    for tn in (1024, 768, 640, 512, 384, 256, 128):
        if N % tn == 0:
            return tn
    return N


def mm_plain(a, b3, layer, name, *, mode="f32", nt=False, bias=None, h=None, tm=512):
    M, K = a.shape
    N = b3.shape[1] if nt else b3.shape[2]
    tm = min(tm, M)
    tn = _tn_for(N)
    a_spec = pl.BlockSpec((tm, K), lambda j, i: (i, 0))
    if nt:
        b_spec = pl.BlockSpec((None, tn, K), lambda j, i: (layer, j, 0))
        dot = _dot_nt
    else:
        b_spec = pl.BlockSpec((None, K, tn), lambda j, i: (layer, 0, j))
        dot = _dot
    o_spec = pl.BlockSpec((tm, tn), lambda j, i: (i, j))
    ins, in_specs = [a, b3], [a_spec, b_spec]

    if mode in ("f32", "bf16"):
        odt = F32 if mode == "f32" else BF16
        if bias is not None:
            ins.append(bias)
            in_specs.append(pl.BlockSpec((1, tn), lambda j, i: (0, j)))

            def kern(a_ref, b_ref, bias_ref, o_ref):
                o_ref[...] = (dot(a_ref[...], b_ref[...]) + bias_ref[...]).astype(odt)
        else:
            def kern(a_ref, b_ref, o_ref):
                o_ref[...] = dot(a_ref[...], b_ref[...]).astype(odt)
        out_specs, out_shape = o_spec, jax.ShapeDtypeStruct((M, N), odt)
    elif mode == "mlp_up":
        def kern(a_ref, b_ref, act_ref):
            r = jnp.maximum(dot(a_ref[...], b_ref[...]), 0.0)
            act_ref[...] = (r * r).astype(BF16)
        out_specs, out_shape = o_spec, jax.ShapeDtypeStruct((M, N), BF16)
    elif mode == "mlp_dn":
        ins.append(h)
        in_specs.append(o_spec)

        def kern(a_ref, b_ref, h_ref, o_ref):
            acc = dot(a_ref[...], b_ref[...])
            o_ref[...] = (acc * (2.0 * jnp.sqrt(h_ref[...].astype(F32)))).astype(BF16)
        out_specs, out_shape = o_spec, jax.ShapeDtypeStruct((M, N), BF16)
    else:
        raise ValueError(mode)

    return pl.pallas_call(
        kern, name=name, grid=(N // tn, M // tm), in_specs=in_specs, out_specs=out_specs,
        out_shape=out_shape, compiler_params=_cp(("parallel", "parallel")),
    )(*ins)


def mm_down_ln(a, b3, layer, x_in, gate1p, ln_g, ln_b, sc_next, sh_next, name, *, tm=256):
    M, K = a.shape
    D = b3.shape[2]
    tm = min(tm, M)

    def kern(a_ref, b_ref, x_ref, gp_ref, lg_ref, lb_ref, sc_ref, sh_ref, y_ref, xo_ref, u_ref):
        y = _dot(a_ref[...], b_ref[...])
        y_ref[...] = y
        z = ALPHA * x_ref[...] + gp_ref[...] * y
        mu = jnp.mean(z, axis=-1, keepdims=True)
        zc = z - mu
        var = jnp.mean(zc * zc, axis=-1, keepdims=True)
        xo = (zc * lax.rsqrt(var + LN_EPS)) * lg_ref[...] + lb_ref[...]
        xo_ref[...] = xo
        u_ref[...] = (xo * (1.0 + sc_ref[...]) + sh_ref[...]).astype(BF16)

    tile = pl.BlockSpec((tm, D), lambda i: (i, 0))
    return pl.pallas_call(
        kern, name=name, grid=(M // tm,),
        in_specs=[pl.BlockSpec((tm, K), lambda i: (i, 0)), pl.BlockSpec((None, K, D), lambda i: (layer, 0, 0)), tile]
        + [_row_spec(D)] * 5,
        out_specs=[tile, tile, tile],
        out_shape=[jax.ShapeDtypeStruct((M, D), F32)] * 2 + [jax.ShapeDtypeStruct((M, D), BF16)],
        compiler_params=_cp(("parallel",)),
    )(a, b3, x_in, gate1p, ln_g, ln_b, sc_next, sh_next)


def mm_down_comb(a, b3, layer, dz, x_in, sc1p, name, *, parts=1, tm=256):
    D, K = b3.shape[1], b3.shape[2]
    M = a.shape[-2]
    kp = K // parts
    tm = min(tm, M)

    def kern(*refs):
        a_refs = refs[:parts]
        b_ref, dz_ref, x_ref, sp_ref, dx_ref, s_ref = refs[parts:]

        @pl.when(pl.program_id(0) == 0)
        def _():
            s_ref[...] = jnp.zeros_like(s_ref)
        if parts == 1:
            du = _dot_nt(a_refs[0][...], b_ref[...])
        else:
            du = _dot_nt(a_refs[0][...], b_ref[:, 0:kp])
            for p in range(1, parts):
                du = du + _dot_nt(a_refs[p][...], b_ref[:, p * kp:(p + 1) * kp])
        dx_ref[...] = ALPHA * dz_ref[...] + du * sp_ref[...]
        s_ref[0:1, :] += jnp.sum(du * x_ref[...], axis=0, keepdims=True)
        s_ref[1:2, :] += jnp.sum(du, axis=0, keepdims=True)

    tile = pl.BlockSpec((tm, D), lambda i: (i, 0))
    if parts == 1:
        a_ins, a_specs = [a], [pl.BlockSpec((tm, K), lambda i: (i, 0))]
    else:
        a_ins = [a] * parts
        a_specs = [pl.BlockSpec((None, tm, kp), functools.partial(lambda i, p: (p, i, 0), p=p)) for p in range(parts)]
    return pl.pallas_call(
        kern, name=name, grid=(M // tm,),
        in_specs=a_specs + [pl.BlockSpec((None, D, K), lambda i: (layer, 0, 0)), tile, tile, _row_spec(D)],
        out_specs=[tile, pl.BlockSpec((8, D), lambda i: (0, 0))],
        out_shape=[jax.ShapeDtypeStruct((M, D), F32), jax.ShapeDtypeStruct((8, D), F32)],
        compiler_params=_cp(("arbitrary",)),
    )(*a_ins, b3, dz, x_in, sc1p)


def mm_w(a, b, name, *, ts=1024, tk=512, chips_out=False, b_parts=1, tn=None):
    S, K = a.shape
    npart = b.shape[-1]
    N = npart * b_parts
    ts = min(ts, S)
    tk = min(tk, K)
    n_chip = N // N_CHIPS
    if tn is None:
        tn = _tn_for(n_chip if chips_out else npart)
    assert npart % tn == 0 and (not chips_out or n_chip % tn == 0)

    def kern(a_ref, b_ref, o_ref):
        @pl.when(pl.program_id(2) == 0)
        def _():
            o_ref[...] = jnp.zeros_like(o_ref)
        o_ref[...] += _dot_tn(a_ref[...], b_ref[...])

    if b_parts == 1:
        b_spec = pl.BlockSpec((ts, tn), lambda k, n, s: (s, n))
    else:
        per = npart // tn
        b_spec = pl.BlockSpec((None, ts, tn), lambda k, n, s: (n // per, s, n % per))
    if chips_out:
        per_chip = n_chip // tn
        o_spec = pl.BlockSpec((None, tk, tn), lambda k, n, s: (n // per_chip, k, n % per_chip))
        out_shape = jax.ShapeDtypeStruct((N_CHIPS, K, n_chip), F32)
    else:
        o_spec = pl.BlockSpec((tk, tn), lambda k, n, s: (k, n))
        out_shape = jax.ShapeDtypeStruct((K, N), F32)
    return pl.pallas_call(
        kern, name=name, grid=(K // tk, N // tn, S // ts),
        in_specs=[pl.BlockSpec((ts, tk), lambda k, n, s: (s, k)), b_spec],
        out_specs=o_spec, out_shape=out_shape,
        compiler_params=_cp(("parallel", "parallel", "arbitrary")),
    )(a, b)


def mm_w_chips3(a, b3, name, *, ts=512):
    S, K = a.shape
    P = b3.shape[2]
    n_chip = 3 * P // N_CHIPS
    ts = min(ts, S)
    pieces = []
    for chip in range(N_CHIPS):
        lo, hi = chip * n_chip, (chip + 1) * n_chip
        while lo < hi:
            part = lo // P
            w = min(hi, (part + 1) * P) - lo
            pieces.append((chip, lo - chip * n_chip, part, lo - part * P, w))
            lo += w

    def kern(a_ref, b_ref, o_ref):
        @pl.when(pl.program_id(0) == 0)
        def _():
            o_ref[...] = jnp.zeros_like(o_ref)
        at = a_ref[...].T
        for chip, oc, part, pc, w in pieces:
            o_ref[chip, :, oc:oc + w] += _dot(at, b_ref[part, :, pc:pc + w])

    return pl.pallas_call(
        kern, name=name, grid=(S // ts,),
        in_specs=[pl.BlockSpec((ts, K), lambda s: (s, 0)), pl.BlockSpec((3, ts, P), lambda s: (0, s, 0))],
        out_specs=pl.BlockSpec((N_CHIPS, K, n_chip), lambda s: (0, 0, 0)),
        out_shape=jax.ShapeDtypeStruct((N_CHIPS, K, n_chip), F32),
        compiler_params=_cp(("arbitrary",)),
    )(a, b3)


def mm_f32(a, b, name):
    def kern(a_ref, b_ref, o_ref):
        o_ref[...] = jnp.dot(a_ref[...], b_ref[...], precision=HIGHEST, preferred_element_type=F32)

    return pl.pallas_call(kern, name=name, out_shape=jax.ShapeDtypeStruct((a.shape[0], b.shape[1]), F32),
                          compiler_params=pltpu.CompilerParams(vmem_limit_bytes=VMEM_LIMIT))(a, b)


def ln_bwd(dxo, x_in, y, gate1p, ln_g, name, *, tm=256):
    S, D = dxo.shape
    tm = min(tm, S)

    def kern(dxo_ref, x_ref, y_ref, gp_ref, lg_ref, dz_ref, dy_ref, s_ref):
        @pl.when(pl.program_id(0) == 0)
        def _():
            s_ref[...] = jnp.zeros_like(s_ref)
        dxo_t = dxo_ref[...]
        yv = y_ref[...]
        z = ALPHA * x_ref[...] + gp_ref[...] * yv
        mu = jnp.mean(z, axis=-1, keepdims=True)
        zc = z - mu
        var = jnp.mean(zc * zc, axis=-1, keepdims=True)
        rstd = lax.rsqrt(var + LN_EPS)
        xhat = zc * rstd
        dxh = dxo_t * lg_ref[...]
        dz = rstd * (dxh - jnp.mean(dxh, axis=-1, keepdims=True)
                     - xhat * jnp.mean(dxh * xhat, axis=-1, keepdims=True))
        dz_ref[...] = dz
        dy_ref[...] = (gp_ref[...] * dz).astype(BF16)
        s_ref[0:1, :] += jnp.sum(dxo_t * xhat, axis=0, keepdims=True)
        s_ref[1:2, :] += jnp.sum(dxo_t, axis=0, keepdims=True)
        s_ref[2:3, :] += jnp.sum(dz * yv, axis=0, keepdims=True)

    tile = pl.BlockSpec((tm, D), lambda i: (i, 0))
    return pl.pallas_call(
        kern, name=name, grid=(S // tm,),
        in_specs=[tile, tile, tile, _row_spec(D), _row_spec(D)],
        out_specs=[tile, tile, pl.BlockSpec((8, D), lambda i: (0, 0))],
        out_shape=[jax.ShapeDtypeStruct((S, D), F32), jax.ShapeDtypeStruct((S, D), BF16),
                   jax.ShapeDtypeStruct((8, D), F32)],
        compiler_params=_cp(("arbitrary",)),
    )(dxo, x_in, y, gate1p, ln_g)


def _tri64():
    r = lax.broadcasted_iota(jnp.int32, (CHUNK, CHUNK), 0)
    c = lax.broadcasted_iota(jnp.int32, (CHUNK, CHUNK), 1)
    return r >= c


def _gla_chunk_common(proj_ref, rows, b, h):
    kc = slice(h * GLA_DK_HEAD, (h + 1) * GLA_DK_HEAD)
    bh = b[:, kc]
    ep = jnp.exp(bh)
    en = jnp.exp(-bh)
    bl = bh[CHUNK - 1:CHUNK, :]
    ee = jnp.exp(bl - bh)
    dec = jnp.exp(bl)
    q = proj_ref[rows, h * GLA_DK_HEAD:(h + 1) * GLA_DK_HEAD] * GLA_SCALE
    k = proj_ref[rows, GLA_DK + h * GLA_DK_HEAD:GLA_DK + (h + 1) * GLA_DK_HEAD]
    v = proj_ref[rows, 2 * GLA_DK + h * GLA_DV_HEAD:2 * GLA_DK + (h + 1) * GLA_DV_HEAD]
    g = proj_ref[rows, 2 * GLA_DK + GLA_DV + h * GLA_DV_HEAD:2 * GLA_DK + GLA_DV + (h + 1) * GLA_DV_HEAD]
    return ep, en, ee, dec, q, k, v, g


def gla_fwd(proj, wgk_p, bgk, gnorm, name, side=None):
    S = proj.shape[0]
    TB = min(GLA_TB, S)
    ncb = TB // CHUNK

    def kern(proj_ref, wgk_ref, bgk_ref, gn_ref, zg_ref, st_ref, state_scr, la_scr):
        @pl.when(pl.program_id(0) == 0)
        def _():
            state_scr[...] = jnp.zeros_like(state_scr)
        lr = proj_ref[:, GLA_LR_OFF:GLA_IN_PAD].astype(BF16)
        gk = _dot(lr, wgk_ref[...]) + bgk_ref[...]
        la_scr[...] = _log_sigmoid(gk) * (1.0 / 16.0)
        lower = _tri64()
        tri = lower.astype(F32)

        def chunk(c, carry):
            rows = pl.ds(pl.multiple_of(c * CHUNK, CHUNK), CHUNK)
            b = jnp.dot(tri, la_scr[rows, :], precision=HIGHEST, preferred_element_type=F32)
            for h in range(GLA_HEADS):
                ep, en, ee, dec, q, k, v, g = _gla_chunk_common(proj_ref, rows, b, h)
                qf = (q * ep).astype(BF16)
                a_f = _dot_nt(qf, (k * en).astype(BF16))
                a_b = _dot_nt((q * en).astype(BF16), (k * ep).astype(BF16))
                amat = jnp.where(lower, a_f, a_b).astype(BF16)
                st = state_scr[h]
                st_ref[c, h] = st
                vb = v.astype(BF16)
                o = _dot(amat, vb) + _dot_nt(qf, st.astype(BF16))
                r = lax.rsqrt(jnp.mean(o * o, axis=-1, keepdims=True) + RMS_EPS)
                on = (o * r) * gn_ref[:, h * GLA_DV_HEAD:(h + 1) * GLA_DV_HEAD]
                zg_ref[rows, h * GLA_DV_HEAD:(h + 1) * GLA_DV_HEAD] = (on * (g * _sigmoid(g))).astype(BF16)
                state_scr[h] = st * dec + _dot_tn(vb, (k * ee).astype(BF16))
            return carry

        lax.fori_loop(0, ncb, chunk, 0)

    return hosted_call(
        kern, side, name=name, grid=(S // TB,),
        in_specs=[pl.BlockSpec((TB, GLA_IN_PAD), lambda i: (i, 0)),
                  pl.BlockSpec((128, GLA_DK), lambda i: (0, 0)), _row_spec(GLA_DK), _row_spec(GLA_DV)],
        out_specs=[pl.BlockSpec((TB, GLA_DV), lambda i: (i, 0)),
                   pl.BlockSpec((ncb, GLA_HEADS, GLA_DV_HEAD, GLA_DK_HEAD), lambda i: (i, 0, 0, 0))],
        out_shape=[jax.ShapeDtypeStruct((S, GLA_DV), BF16),
                   jax.ShapeDtypeStruct((S // CHUNK, GLA_HEADS, GLA_DV_HEAD, GLA_DK_HEAD), F32)],
        scratch_shapes=[pltpu.VMEM((GLA_HEADS, GLA_DV_HEAD, GLA_DK_HEAD), F32), pltpu.VMEM((TB, GLA_DK), F32)],
        dims=("arbitrary",), args=(proj, wgk_p, bgk, gnorm))


def gla_bwd(proj, states, dzg, wgk_p, bgk, gnorm, name, side=None):
    S = proj.shape[0]
    TB = min(GLA_TB, S)
    ncb = TB // CHUNK
    nb = S // TB

    def kern(proj_ref, st_ref, dzg_ref, wgk_ref, bgk_ref, gn_ref,
             dproj_ref, dwgk_ref, dbgk_ref, dgn_ref, dstate_scr, la_scr, gk_scr, dgk_scr):
        @pl.when(pl.program_id(0) == 0)
        def _():
            dstate_scr[...] = jnp.zeros_like(dstate_scr)
            dwgk_ref[...] = jnp.zeros_like(dwgk_ref)
            dbgk_ref[...] = jnp.zeros_like(dbgk_ref)
            dgn_ref[...] = jnp.zeros_like(dgn_ref)
        lr = proj_ref[:, GLA_LR_OFF:GLA_IN_PAD].astype(BF16)
        gk = _dot(lr, wgk_ref[...]) + bgk_ref[...]
        gk_scr[...] = gk
        la_scr[...] = _log_sigmoid(gk) * (1.0 / 16.0)
        lower = _tri64()
        tri = lower.astype(F32)
        r_i = lax.broadcasted_iota(jnp.int32, (CHUNK, CHUNK), 0)
        c_i = lax.broadcasted_iota(jnp.int32, (CHUNK, CHUNK), 1)
        triu = (c_i >= r_i).astype(F32)
        last_row = lax.broadcasted_iota(jnp.int32, (CHUNK, GLA_DK_HEAD), 0) == CHUNK - 1

        def chunk(cc, carry):
            c = ncb - 1 - cc
            rows = pl.ds(pl.multiple_of(c * CHUNK, CHUNK), CHUNK)
            b = jnp.dot(tri, la_scr[rows, :], precision=HIGHEST, preferred_element_type=F32)
            for h in range(GLA_HEADS):
                kc = slice(h * GLA_DK_HEAD, (h + 1) * GLA_DK_HEAD)
                vc = slice(h * GLA_DV_HEAD, (h + 1) * GLA_DV_HEAD)
                ep, en, ee, dec, q, k, v, g = _gla_chunk_common(proj_ref, rows, b, h)
                qf = q * ep
                kn = k * en
                qn = q * en
                kp = k * ep
                ke = k * ee
                qf_b, kn_b, qn_b, kp_b, ke_b = (t.astype(BF16) for t in (qf, kn, qn, kp, ke))
                vb = v.astype(BF16)
                amat = jnp.where(lower, _dot_nt(qf_b, kn_b), _dot_nt(qn_b, kp_b)).astype(BF16)
                st = st_ref[c, h]
                st_b = st.astype(BF16)
                o = _dot(amat, vb) + _dot_nt(qf_b, st_b)
                r = lax.rsqrt(jnp.mean(o * o, axis=-1, keepdims=True) + RMS_EPS)
                oh = o * r
                gn = gn_ref[:, vc]
                sg = _sigmoid(g)
                dz = dzg_ref[rows, vc]
                don = dz * (g * sg)
                dg = dz * (oh * gn) * (sg * (1.0 + g * (1.0 - sg)))
                dgn_ref[:, vc] += jnp.sum(don * oh, axis=0, keepdims=True)
                doh = don * gn
                do = r * (doh - oh * jnp.mean(doh * oh, axis=-1, keepdims=True))
                do_b = do.astype(BF16)
                dst = dstate_scr[h]
                dst_b = dst.astype(BF16)
                dv = _dot_tn(amat, do_b) + _dot_nt(ke_b, dst_b)
                da = _dot_nt(do_b, vb)
                da_f = jnp.where(lower, da, 0.0).astype(BF16)
                da_b = jnp.where(lower, 0.0, da).astype(BF16)
                dqf = _dot(da_f, kn_b) + _dot(do_b, st_b)
                dkn = _dot_tn(da_f, qf_b)
                dqn = _dot(da_b, kp_b)
                dkp = _dot_tn(da_b, qn_b)
                dke = _dot(vb, dst_b)
                ddec = jnp.sum(dst * st, axis=0, keepdims=True)
                dstate_scr[h] = dst * dec + _dot_tn(do_b, qf_b)
                dq = (dqf * ep + dqn * en) * GLA_SCALE
                dk = dkn * en + dkp * ep + dke * ee
                db = dqf * qf - dkn * kn - dqn * qn + dkp * kp - dke * ke
                dbl = jnp.sum(dke * ke, axis=0, keepdims=True) + ddec * dec
                db = db + jnp.where(last_row, dbl, 0.0)
                dla = jnp.dot(triu, db, precision=HIGHEST, preferred_element_type=F32)
                dgk_scr[rows, kc] = dla * (1.0 / 16.0) * _sigmoid(-gk_scr[rows, kc])
                dproj_ref[rows, kc] = dq.astype(BF16)
                dproj_ref[rows, GLA_DK + h * GLA_DK_HEAD:GLA_DK + (h + 1) * GLA_DK_HEAD] = dk.astype(BF16)
                dproj_ref[rows, 2 * GLA_DK + h * GLA_DV_HEAD:2 * GLA_DK + (h + 1) * GLA_DV_HEAD] = dv.astype(BF16)
                dproj_ref[rows, 2 * GLA_DK + GLA_DV + h * GLA_DV_HEAD:
                          2 * GLA_DK + GLA_DV + (h + 1) * GLA_DV_HEAD] = dg.astype(BF16)
            return carry

        lax.fori_loop(0, ncb, chunk, 0)
        dgk = dgk_scr[...]
        dgk_b = dgk.astype(BF16)
        dproj_ref[:, GLA_LR_OFF:GLA_IN_PAD] = _dot_nt(dgk_b, wgk_ref[...]).astype(BF16)
        dwgk_ref[...] += _dot_tn(lr, dgk_b)
        dbgk_ref[...] += jnp.sum(dgk, axis=0, keepdims=True)

    rev = lambda i: (nb - 1 - i, 0)
    return hosted_call(
        kern, side, name=name, grid=(nb,),
        in_specs=[pl.BlockSpec((TB, GLA_IN_PAD), rev),
                  pl.BlockSpec((ncb, GLA_HEADS, GLA_DV_HEAD, GLA_DK_HEAD), lambda i: (nb - 1 - i, 0, 0, 0)),
                  pl.BlockSpec((TB, GLA_DV), rev),
                  pl.BlockSpec((128, GLA_DK), lambda i: (0, 0)), _row_spec(GLA_DK), _row_spec(GLA_DV)],
        out_specs=[pl.BlockSpec((TB, GLA_IN_PAD), rev),
                   pl.BlockSpec((128, GLA_DK), lambda i: (0, 0)), _row_spec(GLA_DK), _row_spec(GLA_DV)],
        out_shape=[jax.ShapeDtypeStruct((S, GLA_IN_PAD), BF16), jax.ShapeDtypeStruct((128, GLA_DK), F32),
                   jax.ShapeDtypeStruct((1, GLA_DK), F32), jax.ShapeDtypeStruct((1, GLA_DV), F32)],
        scratch_shapes=[pltpu.VMEM((GLA_HEADS, GLA_DV_HEAD, GLA_DK_HEAD), F32), pltpu.VMEM((TB, GLA_DK), F32),
                        pltpu.VMEM((TB, GLA_DK), F32), pltpu.VMEM((TB, GLA_DK), F32)],
        dims=("arbitrary",), args=(proj, states, dzg, wgk_p, bgk, gnorm))


ATT_TW = 1024
ATT_CLASSES = 3


def _att_window(i):
    return pl.multiple_of(jnp.maximum(i * ATT_TQ - LEFT_CHUNKS * CHUNK, 0), ATT_TQ)


def _att_rel_index():
    e = jnp.arange(ATT_TW)[None, :]
    d = jnp.where(e < ATT_KW, e, e - ATT_TW)
    off = (jnp.arange(ATT_CLASSES) * ATT_TQ)[:, None]
    return jnp.clip(off - d, -MAX_REL, MAX_REL) + MAX_REL


def _row_bits():
    return lax.broadcasted_iota(jnp.int32, (ATT_TQ, ATT_TW), 0)


def att_bias_tiles(rel_bias, name):
    tab = jnp.take(rel_bias, _att_rel_index(), axis=1).reshape(ATT_HEADS * ATT_CLASSES, 1, ATT_TW)

    def kern(t_ref, o_ref):
        cls = pl.program_id(0) % ATT_CLASSES
        x = jnp.broadcast_to(t_ref[...], (ATT_TQ, ATT_TW))
        row = _row_bits()
        for b in range(8):
            x = jnp.where((row & (1 << b)) != 0, pltpu.roll(x, 1 << b, axis=1), x)
        x = x[:, :ATT_KW]
        qc = cls * (ATT_TQ // CHUNK) + lax.shift_right_arithmetic(
            lax.broadcasted_iota(jnp.int32, (ATT_TQ, ATT_KW), 0), 6)
        kc = lax.shift_right_arithmetic(lax.broadcasted_iota(jnp.int32, (ATT_TQ, ATT_KW), 1), 6)
        o_ref[...] = jnp.where((kc <= qc) & (kc >= qc - LEFT_CHUNKS), x, NEG_INF)

    return pl.pallas_call(
        kern, name=name, grid=(ATT_HEADS * ATT_CLASSES,),
        in_specs=[pl.BlockSpec((None, 1, ATT_TW), lambda i: (i, 0, 0))],
        out_specs=pl.BlockSpec((None, ATT_TQ, ATT_KW), lambda i: (i, 0, 0)),
        out_shape=jax.ShapeDtypeStruct((ATT_HEADS * ATT_CLASSES, ATT_TQ, ATT_KW), F32),
        compiler_params=_cp(("parallel",)),
    )(tab)


def att_bias_grad(dbt, name):
    def kern(d_ref, o_ref):
        x = jnp.concatenate([d_ref[...], jnp.zeros((ATT_TQ, ATT_TW - ATT_KW), F32)], axis=1)
        row = _row_bits()
        for b in range(8):
            x = jnp.where((row & (1 << b)) != 0, pltpu.roll(x, ATT_TW - (1 << b), axis=1), x)
        o_ref[...] = jnp.sum(x, axis=0, keepdims=True)

    diag = pl.pallas_call(
        kern, name=name + "_diag", grid=(ATT_HEADS * ATT_CLASSES,),
        in_specs=[pl.BlockSpec((None, ATT_TQ, ATT_KW), lambda i: (i, 0, 0))],
        out_specs=pl.BlockSpec((None, 1, ATT_TW), lambda i: (i, 0, 0)),
        out_shape=jax.ShapeDtypeStruct((ATT_HEADS * ATT_CLASSES, 1, ATT_TW), F32),
        compiler_params=_cp(("parallel",)),
    )(dbt)
    diag = diag.reshape(ATT_HEADS, ATT_CLASSES * ATT_TW)
    onehot = (_att_rel_index().reshape(-1)[:, None] == jnp.arange(384)[None, :]).astype(F32)
    return mm_f32(diag, onehot, name + "_bins")[:, :N_REL]


def _att_scores(q_ref, kw, bias_ref, hh):
    hs = slice(hh * ATT_HD, (hh + 1) * ATT_HD)
    q = q_ref[:, hs] * ATT_SCALE
    k = kw[:, hs]
    s = _dot_nt(q, k) + bias_ref[hh]
    e = jnp.exp(s - jnp.max(s, axis=-1, keepdims=True))
    inv = 1.0 / jnp.sum(e, axis=-1, keepdims=True)
    return q, k, e, inv


def _att_specs(S):
    nq = D_MODEL // 128
    q_spec = pl.BlockSpec((ATT_TQ, 128), lambda p, i: (i, p))
    k_spec = pl.BlockSpec((S, 128), lambda p, i: (0, nq + p))
    v_spec = pl.BlockSpec((S, 128), lambda p, i: (0, 2 * nq + p))
    b_spec = pl.BlockSpec((2, None, ATT_TQ, ATT_KW), lambda p, i: (p, jnp.minimum(i, ATT_CLASSES - 1), 0, 0))
    return q_spec, k_spec, v_spec, b_spec


def attn_fwd(qkv, bias, name, side=None):
    S = qkv.shape[0]
    q_spec, k_spec, v_spec, b_spec = _att_specs(S)

    def kern(q_ref, k_ref, v_ref, bias_ref, o_ref):
        ws = _att_window(pl.program_id(1))
        kw = k_ref[pl.ds(ws, ATT_KW), :]
        vw = v_ref[pl.ds(ws, ATT_KW), :]
        outs = []
        for hh in range(2):
            _, _, e, inv = _att_scores(q_ref, kw, bias_ref, hh)
            outs.append(_dot(e.astype(BF16), vw[:, hh * ATT_HD:(hh + 1) * ATT_HD]) * inv)
        o_ref[...] = jnp.concatenate(outs, axis=1).astype(BF16)

    return hosted_call(
        kern, side, name=name, grid=(ATT_HEADS // 2, S // ATT_TQ),
        in_specs=[q_spec, k_spec, v_spec, b_spec],
        out_specs=[pl.BlockSpec((ATT_TQ, 128), lambda p, i: (i, p))],
        out_shape=[jax.ShapeDtypeStruct((S, D_MODEL), BF16)],
        scratch_shapes=[], dims=("parallel", "arbitrary"), args=(qkv, qkv, qkv, bias))


def attn_bwd(qkv, bias, do, name, side=None):
    S = qkv.shape[0]
    nblk = S // ATT_TQ
    q_spec, k_spec, v_spec, b_spec = _att_specs(S)

    def kern(q_ref, k_ref, v_ref, bias_ref, do_ref, dqkv_ref, db_ref, dk_scr, dv_scr):
        i = pl.program_id(1)

        @pl.when(i == 0)
        def _():
            dk_scr[...] = jnp.zeros_like(dk_scr)
            dv_scr[...] = jnp.zeros_like(dv_scr)
            db_ref[...] = jnp.zeros_like(db_ref)
        ws = _att_window(i)
        win = pl.ds(ws, ATT_KW)
        kw = k_ref[win, :]
        vw = v_ref[win, :]
        o_cls = jnp.minimum(i, ATT_CLASSES - 1)
        dqs, dks, dvs = [], [], []
        for hh in range(2):
            hs = slice(hh * ATT_HD, (hh + 1) * ATT_HD)
            q, k, e, inv = _att_scores(q_ref, kw, bias_ref, hh)
            p = e * inv
            do_h = do_ref[:, hs]
            dp = _dot_nt(do_h, vw[:, hs])
            ds = p * (dp - jnp.sum(p * dp, axis=-1, keepdims=True))
            db_ref[hh, o_cls] += ds
            ds_b = ds.astype(BF16)
            dqs.append(_dot(ds_b, k) * ATT_SCALE)
            dks.append(_dot_tn(ds_b, q))
            dvs.append(_dot_tn(p.astype(BF16), do_h))
        dqkv_ref[0, pl.ds(pl.multiple_of(i * ATT_TQ, ATT_TQ), ATT_TQ), :] = jnp.concatenate(dqs, axis=1).astype(BF16)
        dk_scr[win, :] += jnp.concatenate(dks, axis=1)
        dv_scr[win, :] += jnp.concatenate(dvs, axis=1)

        @pl.when(i == nblk - 1)
        def _():
            dqkv_ref[1] = dk_scr[...].astype(BF16)
            dqkv_ref[2] = dv_scr[...].astype(BF16)

    return hosted_call(
        kern, side, name=name, grid=(ATT_HEADS // 2, nblk),
        in_specs=[q_spec, k_spec, v_spec, b_spec, pl.BlockSpec((ATT_TQ, 128), lambda p, i: (i, p))],
        out_specs=[pl.BlockSpec((3, S, 128), lambda p, i: (0, 0, p)),
                   pl.BlockSpec((2, ATT_CLASSES, ATT_TQ, ATT_KW), lambda p, i: (p, 0, 0, 0))],
        out_shape=[jax.ShapeDtypeStruct((3, S, D_MODEL), BF16),
                   jax.ShapeDtypeStruct((ATT_HEADS, ATT_CLASSES, ATT_TQ, ATT_KW), F32)],
        scratch_shapes=[pltpu.VMEM((S, 128), F32), pltpu.VMEM((S, 128), F32)],
        dims=("parallel", "arbitrary"), args=(qkv, qkv, qkv, bias, do))


def colsum3(a3, name):
    P, S, N = a3.shape
    tm = min(512, S)

    def kern(a_ref, o_ref):
        @pl.when(pl.program_id(1) == 0)
        def _():
            o_ref[...] = jnp.zeros_like(o_ref)
        o_ref[...] += jnp.sum(a_ref[...].astype(F32), axis=0, keepdims=True)

    return pl.pallas_call(
        kern, name=name, grid=(P, S // tm),
        in_specs=[pl.BlockSpec((None, tm, N), lambda p, i: (p, i, 0))],
        out_specs=pl.BlockSpec((None, 1, N), lambda p, i: (p, 0, 0)),
        out_shape=jax.ShapeDtypeStruct((P, 1, N), F32),
        compiler_params=_cp(("parallel", "arbitrary")),
    )(a3)


def _me():
    return lax.axis_index("x"), lax.axis_index("y"), lax.axis_index("c")


def _other_chips(x, y):
    return [(1 - x, y), (x, 1 - y), (1 - x, 1 - y)]


def all_gather8(x_shard, name):
    m_per, n = x_shard.shape

    def body(x_ref, out_ref, send_sems, recv_sems, local_sem):
        x, y, c = _me()
        me, sibling = (x, y, c), (x, y, 1 - c)
        chips = _other_chips(x, y)

        def rows(px, py, pc):
            return out_ref.at[pl.ds((4 * px + 2 * py + pc) * m_per, m_per), :]

        def copy(k, block, to, src=None):
            return pltpu.make_async_remote_copy(
                src_ref=rows(*block) if src is None else src, dst_ref=rows(*block),
                send_sem=send_sems.at[k], recv_sem=recv_sems.at[k], device_id=to, device_id_type=MESH)

        mine = pltpu.make_async_copy(x_ref, rows(*me), local_sem)
        mine.start()
        first = [copy(0, me, sibling, src=x_ref)]
        first += [copy(1 + j, me, (*chip, c), src=x_ref) for j, chip in enumerate(chips)]
        for cp in first:
            cp.start()
        passed = [copy(4 + j, (*chip, c), sibling) for j, chip in enumerate(chips)]
        for j, chip in enumerate(chips):
            copy(1 + j, (*chip, c), me).wait_recv()
            passed[j].start()
        copy(0, sibling, me).wait_recv()
        for j, chip in enumerate(chips):
            copy(4 + j, (*chip, 1 - c), me).wait_recv()
        for cp in first + passed:
            cp.wait_send()
        mine.wait()

    return pl.pallas_call(
        body, name=name,
        out_shape=jax.ShapeDtypeStruct((N_DEV * m_per, n), x_shard.dtype),
        in_specs=[pl.BlockSpec(memory_space=pltpu.VMEM)],
        out_specs=pl.BlockSpec(memory_space=pltpu.VMEM),
        scratch_shapes=[pltpu.SemaphoreType.DMA((7,)), pltpu.SemaphoreType.DMA((7,)), pltpu.SemaphoreType.DMA],
        compiler_params=pltpu.CompilerParams(vmem_limit_bytes=VMEM_LIMIT),
    )(x_shard)


def _half_rows(n_rows, c):
    h = n_rows // 2
    return pl.ds(c * h, h)


def _gathered_shape(shape, flavour):
    L, a, b = shape
    return {"col": (L, a, N_CHIPS * b), "row": (L, N_CHIPS * a, b), "lead": (N_CHIPS, L, a, b)}[flavour]


def _gathered_part(out_ref, shape, flavour, s, rows):
    L, a, b = shape
    if flavour == "col":
        return out_ref.at[:, rows, pl.ds(s * b, b)]
    if flavour == "row":
        return out_ref.at[:, pl.ds(s * a + rows.start, rows.size), :]
    return out_ref.at[s, :, rows, :]


def gather_side(shards, flavours):
    n = len(shards)
    shapes = [w.shape for w in shards]

    def copies(w_refs, out_refs, send_sems, recv_sems, local_sems):
        x, y, c = _me()
        sibling = (x, y, 1 - c)
        chips = _other_chips(x, y)
        me_s = 2 * x + y

        def copy(k, src, dst, to):
            return pltpu.make_async_remote_copy(src_ref=src, dst_ref=dst, send_sem=send_sems.at[k],
                                                recv_sem=recv_sems.at[k], device_id=to, device_id_type=MESH)

        local, first, landed, passed, passed_in = [], [], [], [], []
        for w in range(n):
            shp, fl = shapes[w], flavours[w]
            my_half = _half_rows(shp[1], c)
            sib_half = _half_rows(shp[1], 1 - c)
            local.append(pltpu.make_async_copy(
                w_refs[w], _gathered_part(out_refs[w], shp, fl, me_s, pl.ds(0, shp[1])), local_sems.at[w]))
            for j, chip in enumerate(chips):
                s = 2 * chip[0] + chip[1]
                first.append(copy(6 * w + j, w_refs[w].at[:, my_half, :],
                                  _gathered_part(out_refs[w], shp, fl, me_s, my_half), (*chip, c)))
                part = _gathered_part(out_refs[w], shp, fl, s, my_half)
                landed.append(copy(6 * w + j, part, part, (*chip, c)))
                passed.append(copy(6 * w + 3 + j, part, part, sibling))
                theirs = _gathered_part(out_refs[w], shp, fl, s, sib_half)
                passed_in.append(copy(6 * w + 3 + j, theirs, theirs, sibling))
        return local, first, landed, passed, passed_in

    def start(*refs):
        local, first, _, _, _ = copies(*refs)
        for cp in local + first:
            cp.start()

    def wait(*refs):
        local, first, landed, passed, passed_in = copies(*refs)
        for arrived, onward in zip(landed, passed):
            arrived.wait_recv()
            onward.start()
        for cp in passed_in:
            cp.wait_recv()
        for cp in first + passed:
            cp.wait_send()
        for cp in local:
            cp.wait()

    out_shapes = [jax.ShapeDtypeStruct(_gathered_shape(s, f), w.dtype) for w, s, f in zip(shards, shapes, flavours)]
    return Side(shards, out_shapes, 6 * n, n, start, wait)


def swap_halves(gs, name):
    n = len(gs)

    def body(*refs):
        g_refs, out_refs = refs[:n], refs[n:2 * n]
        send_sems, recv_sems = refs[2 * n:]
        x, y, c = _me()
        cps = [pltpu.make_async_remote_copy(
            src_ref=g_refs[w].at[:, _half_rows(gs[w].shape[1], 1 - c), :], dst_ref=out_refs[w],
            send_sem=send_sems.at[w], recv_sem=recv_sems.at[w], device_id=(x, y, 1 - c), device_id_type=MESH)
            for w in range(n)]
        for cp in cps:
            cp.start()
        for cp in cps:
            cp.wait()

    any_spec = pl.BlockSpec(memory_space=pl.ANY)
    return pl.pallas_call(
        body, name=name,
        out_shape=[jax.ShapeDtypeStruct((g.shape[0], g.shape[1] // 2, g.shape[2]), g.dtype) for g in gs],
        in_specs=[any_spec] * n, out_specs=[any_spec] * n,
        scratch_shapes=[pltpu.SemaphoreType.DMA((n,)), pltpu.SemaphoreType.DMA((n,))],
    )(*gs)


def add_half(g, r1, c_idx, name):
    n, R, C = g.shape
    half = R // 2
    tr = _rows_block(half, C)
    nbh = half // tr

    def kern(c_ref, g_ref, r_ref, o_ref):
        o_ref[...] = g_ref[...] + r_ref[...]

    return pl.pallas_call(
        kern, name=name,
        grid_spec=pltpu.PrefetchScalarGridSpec(
            num_scalar_prefetch=1, grid=(n, nbh),
            in_specs=[pl.BlockSpec((1, tr, C), lambda d, r, c_ref: (d, c_ref[0] * nbh + r, 0)),
                      pl.BlockSpec((1, tr, C), lambda d, r, c_ref: (d, r, 0))],
            out_specs=pl.BlockSpec((1, tr, C), lambda d, r, c_ref: (d, r, 0))),
        out_shape=jax.ShapeDtypeStruct((n, half, C), F32),
        compiler_params=_cp(("parallel", "parallel")),
    )(c_idx, g, r1)


def exchange_side(ps):
    n = len(ps)

    def copies(p_refs, out_refs, send_sems, recv_sems, local_sems):
        x, y, c = _me()
        return [pltpu.make_async_remote_copy(
            src_ref=p_refs[w].at[2 * chip[0] + chip[1]], dst_ref=out_refs[w].at[j],
            send_sem=send_sems.at[3 * w + j], recv_sem=recv_sems.at[3 * w + j],
            device_id=(*chip, c), device_id_type=MESH)
            for w in range(n) for j, chip in enumerate(_other_chips(x, y))]

    def start(*refs):
        for cp in copies(*refs):
            cp.start()

    def wait(*refs):
        for cp in copies(*refs):
            cp.wait()

    return Side(ps, [jax.ShapeDtypeStruct((3,) + p.shape[1:], p.dtype) for p in ps], 3 * n, 0, start, wait)


def add_chips(p, r2, chip_idx, name):
    n, H, C = p.shape
    tr = _rows_block(H, C)

    def kern(s_ref, p_ref, r_ref, o_ref):
        o_ref[...] = ((p_ref[0] + r_ref[0]) + r_ref[1]) + r_ref[2]

    return pl.pallas_call(
        kern, name=name,
        grid_spec=pltpu.PrefetchScalarGridSpec(
            num_scalar_prefetch=1, grid=(H // tr,),
            in_specs=[pl.BlockSpec((1, tr, C), lambda r, s_ref: (s_ref[0], r, 0)),
                      pl.BlockSpec((3, tr, C), lambda r, s_ref: (0, r, 0))],
            out_specs=pl.BlockSpec((tr, C), lambda r, s_ref: (r, 0))),
        out_shape=jax.ShapeDtypeStruct((H, C), F32),
        compiler_params=_cp(("parallel",)),
    )(chip_idx, p, r2)


def join_halves(ss, groups, name):
    n = len(ss)
    where = [(gi, l) for gi, (L, _, _) in enumerate(groups) for l in range(L)]
    assert len(where) == n

    def body(*refs):
        s_refs, out_refs = refs[:n], refs[n:n + len(groups)]
        send_sems, recv_sems, local_sems = refs[n + len(groups):]
        x, y, c = _me()
        local, remote = [], []
        for w in range(n):
            gi, l = where[w]
            mine = out_refs[gi].at[l, _half_rows(groups[gi][1], c), :]
            cp = pltpu.make_async_copy(s_refs[w], mine, local_sems.at[w])
            cp.start()
            local.append(cp)
            cp = pltpu.make_async_remote_copy(src_ref=s_refs[w], dst_ref=mine, send_sem=send_sems.at[w],
                                              recv_sem=recv_sems.at[w], device_id=(x, y, 1 - c), device_id_type=MESH)
            cp.start()
            remote.append(cp)
        for w in range(n):
            gi, l = where[w]
            theirs = out_refs[gi].at[l, _half_rows(groups[gi][1], 1 - c), :]
            remote[w].wait_send()
            pltpu.make_async_remote_copy(src_ref=s_refs[w], dst_ref=theirs, send_sem=send_sems.at[w],
                                         recv_sem=recv_sems.at[w], device_id=(x, y, 1 - c),
                                         device_id_type=MESH).wait_recv()
            local[w].wait()

    any_spec = pl.BlockSpec(memory_space=pl.ANY)
    return pl.pallas_call(
        body, name=name,
        out_shape=[jax.ShapeDtypeStruct(g, F32) for g in groups],
        in_specs=[any_spec] * n, out_specs=[any_spec] * len(groups),
        scratch_shapes=[pltpu.SemaphoreType.DMA((n,)), pltpu.SemaphoreType.DMA((n,)), pltpu.SemaphoreType.DMA((n,))],
    )(*ss)


BIG = (("gla_w_in", 2, (1024, GLA_IN // N_CHIPS), "lead"), ("gla_w_out", 2, (256, 1024), "row"),
       ("att_w_in", 2, (1024, 768), "col"), ("att_w_out", 2, (256, 1024), "row"),
       ("ff_w1", 4, (1024, 1024), "col"), ("ff_w2", 4, (1024, 1024), "row"))
FLAVOUR = {n: f for n, _, _, f in BIG}


def layer_weights(i):
    mixer = "gla" if i % 2 == 0 else "att"
    return (("in", mixer + "_w_in", i // 2), ("out", mixer + "_w_out", i // 2), ("w1", "ff_w1", i), ("w2", "ff_w2", i))


class Comm:
    def __init__(self, weights, c_idx, chip_idx):
        self.weights, self.c_idx, self.chip_idx = weights, c_idx, chip_idx
        self.reduced = {}

    def _shards(self, i):
        return [self.weights[n][l:l + 1].astype(BF16) for _, n, l in layer_weights(i)]

    def gather(self, i):
        return gather_side(self._shards(i), [FLAVOUR[n] for _, n, _ in layer_weights(i)])

    def full_weights(self, i, gathered):
        W = {}
        for (role, n, _), w in zip(layer_weights(i), gathered):
            if n == "gla_w_in":
                w = jnp.pad(w.transpose(1, 2, 0, 3).reshape(1, D_MODEL, GLA_IN), ((0, 0), (0, 0), (0, GLA_IN_PAD - GLA_IN)))
            W[role] = (w, 0)
        return W

    def first_weights(self):
        return self.full_weights(0, run_side(self.gather(0), "gather_w0"))

    def reduce_begin(self, i, grads):
        r1 = swap_halves(grads, f"rs_swap_{i}")
        ps = [add_half(g, r, self.c_idx, f"rs_add2_{i}_{w}") for w, (g, r) in enumerate(zip(grads, r1))]
        return i, ps

    def exchange(self, pending):
        return exchange_side(pending[1])

    def reduce_mid(self, pending, landed):
        i, ps = pending
        for w, ((_, n, l), p, r) in enumerate(zip(layer_weights(i), ps, landed)):
            self.reduced[(n, l)] = add_chips(p, r, self.chip_idx, f"rs_add4_{i}_{w}")

    def reduce_last(self, pending):
        self.reduce_mid(pending, run_side(self.exchange(pending), f"rs_xchg_{pending[0]}"))

    def reduce_end(self):
        ss = [self.reduced[(n, l)] for n, L, _, _ in BIG for l in range(L)]
        joined = join_halves(ss, [(L,) + shp for _, L, shp, _ in BIG], "rs_join")
        return {n: g for (n, *_), g in zip(BIG, joined)}


def local_step(x, target, mods, comm, small):
    S, D = x.shape
    row = lambda v: v.reshape(1, -1)
    saved = []
    tiles = [att_bias_tiles(small["att_rel_bias"][j], f"att_tiles_{j}").reshape(ATT_HEADS, ATT_CLASSES, ATT_TQ, ATT_KW)
             for j in range(2)]
    wgk_p = [jnp.pad(small["gla_w_gk2"][j], ((0, 128 - GLA_RANK), (0, 0))).astype(BF16) for j in range(2)]

    u1 = modulate(x, row(mods[0, 1]), row(mods[0, 0]), "mod_first")
    Ws = [comm.first_weights()]
    for i in range(DEPTH):
        j = i // 2
        W = Ws[i]
        sh1, sc1, g1, sh2, sc2, g2 = (row(mods[i, k]) for k in range(6))
        nxt = min(i + 1, DEPTH - 1)
        side = comm.gather(i + 1) if i + 1 < DEPTH else None
        if i % 2 == 0:
            proj = mm_plain(u1, *W["in"], f"gla_in_{i}")
            (zmix, states), landed = gla_fwd(proj, wgk_p[j], row(small["gla_b_gk"][j]), row(small["gla_g_norm"][j]),
                                             f"gla_fwd_{i}", side)
        else:
            proj = mm_plain(u1, *W["in"], f"att_in_{i}", mode="bf16", bias=row(small["att_b_in"][j]))
            (zmix,), landed = attn_fwd(proj, tiles[j], f"att_fwd_{i}", side)
            states = None
        if i + 1 < DEPTH:
            Ws.append(comm.full_weights(i + 1, landed))
        y1, x_mid, u2 = mm_down_ln(zmix, *W["out"], x, 1.0 + g1, row(small["ln_g"][i, 0]), row(small["ln_b"][i, 0]),
                                   sc2, sh2, f"mix_out_{i}")
        act = mm_plain(u2, *W["w1"], f"ff_up_{i}", mode="mlp_up")
        y2, x_out, u_next = mm_down_ln(act, *W["w2"], x_mid, 1.0 + g2, row(small["ln_g"][i, 1]),
                                       row(small["ln_b"][i, 1]), row(mods[nxt, 1]), row(mods[nxt, 0]), f"ff_out_{i}")
        saved.append(dict(x_in=x, u1=u1, proj=proj, zmix=zmix, states=states, y1=y1, x_mid=x_mid, u2=u2,
                          act=act, y2=y2))
        x, u1 = x_out, u_next

    dx, sq = loss_head(x, target, "loss_head")

    g_small = dict(ln_g=[None] * DEPTH, ln_b=[None] * DEPTH, gla_w_gk2=[None] * 2, gla_b_gk=[None] * 2,
                   gla_g_norm=[None] * 2, att_b_in=[None] * 2, att_rel_bias=[None] * 2)
    dmods = [None] * DEPTH
    pending = None

    for i in reversed(range(DEPTH)):
        j = i // 2
        sv = saved[i]
        W = Ws[i]
        sh1, sc1, g1, sh2, sc2, g2 = (row(mods[i, k]) for k in range(6))
        dz2, dy2, s_ln2 = ln_bwd(dx, sv["x_mid"], sv["y2"], 1.0 + g2, row(small["ln_g"][i, 1]), f"ln2_bwd_{i}")
        dh = mm_plain(dy2, *W["w2"], f"ff_dn_{i}", mode="mlp_dn", nt=True, h=sv["act"])
        g_w2 = mm_w(sv["act"], dy2, f"ff_w2g_{i}").reshape(N_CHIPS, D_FF // N_CHIPS, D)
        g_w1 = mm_w(sv["u2"], dh, f"ff_w1g_{i}", chips_out=True)
        dx_mid, s_m2 = mm_down_comb(dh, *W["w1"], dz2, sv["x_mid"], 1.0 + sc2, f"ff_dx_{i}")
        dz1, dy1, s_ln1 = ln_bwd(dx_mid, sv["x_in"], sv["y1"], 1.0 + g1, row(small["ln_g"][i, 0]), f"ln1_bwd_{i}")
        side = comm.exchange(pending) if pending is not None else None
        if i % 2 == 0:
            g_out = mm_w(sv["zmix"], dy1, f"gla_wog_{i}").reshape(N_CHIPS, D // N_CHIPS, D)
            dzg = mm_plain(dy1, *W["out"], f"gla_dz_{i}", nt=True)
            (dproj, dwgk, dbgk, dgn), landed = gla_bwd(sv["proj"], sv["states"], dzg, wgk_p[j],
                                                       row(small["gla_b_gk"][j]), row(small["gla_g_norm"][j]),
                                                       f"gla_bwd_{i}", side)
            g_small["gla_w_gk2"][j] = dwgk[:GLA_RANK]
            g_small["gla_b_gk"][j] = dbgk[0]
            g_small["gla_g_norm"][j] = dgn[0].reshape(GLA_HEADS, GLA_DV_HEAD)
            gwi = mm_w(sv["u1"], dproj, f"gla_wig_{i}")[:, :GLA_IN]
            g_in = gwi.reshape(D, N_CHIPS, GLA_IN // N_CHIPS).transpose(1, 0, 2)
            dx, s_m1 = mm_down_comb(dproj, *W["in"], dz1, sv["x_in"], 1.0 + sc1, f"mix_dx_{i}")
        else:
            g_out = mm_w(sv["zmix"], dy1, f"att_wog_{i}").reshape(N_CHIPS, D // N_CHIPS, D)
            do = mm_plain(dy1, *W["out"], f"att_do_{i}", mode="bf16", nt=True)
            (dqkv, dbt), landed = attn_bwd(sv["proj"], tiles[j], do, f"att_bwd_{i}", side)
            g_small["att_rel_bias"][j] = att_bias_grad(dbt.reshape(ATT_HEADS * ATT_CLASSES, ATT_TQ, ATT_KW),
                                                       f"att_bias_{i}")
            g_small["att_b_in"][j] = colsum3(dqkv, f"att_bin_{i}").reshape(3 * D)
            g_in = mm_w_chips3(sv["u1"], dqkv, f"att_wig_{i}")
            dx, s_m1 = mm_down_comb(dqkv, *W["in"], dz1, sv["x_in"], 1.0 + sc1, f"mix_dx_{i}", parts=3)
        if pending is not None:
            comm.reduce_mid(pending, landed)
        pending = comm.reduce_begin(i, [g_in, g_out, g_w1, g_w2])
        g_small["ln_g"][i] = jnp.stack([s_ln1[0], s_ln2[0]])
        g_small["ln_b"][i] = jnp.stack([s_ln1[1], s_ln2[1]])
        dmods[i] = jnp.stack([s_m1[1], s_m1[0], s_ln1[2], s_m2[1], s_m2[0], s_ln2[2]])
    comm.reduce_last(pending)

    g_small = {n: jnp.stack(v) for n, v in g_small.items()}
    return sq, dx, jnp.stack(dmods), g_small


SMALL_SHARDED = (("ln_g", (4, 2, 256)), ("ln_b", (4, 2, 256)), ("gla_g_norm", (2, 4, 64)),
                 ("gla_w_gk2", (2, 16, 128)), ("att_b_in", (2, 768)))
SMALL_FULL = dict(ln_g=(4, 2, 1024), ln_b=(4, 2, 1024), gla_g_norm=(2, 4, 256), gla_w_gk2=(2, 16, 512),
                  att_b_in=(2, 3072), gla_b_gk=(2, 512), att_rel_bias=(2, 16, 257))
SMALL_GRAD_ORDER = ("ln_g", "ln_b", "gla_g_norm", "gla_w_gk2", "att_b_in", "gla_b_gk", "att_rel_bias")


def _pack_small(arrs, rows_total):
    parts = []
    for a in arrs:
        flat = a.reshape(-1)
        pad = (-flat.shape[0]) % PACK_W
        parts.append(jnp.pad(flat, (0, pad)).reshape(-1, PACK_W))
    buf = jnp.concatenate(parts, axis=0)
    return jnp.pad(buf, ((0, rows_total - buf.shape[0]), (0, 0)))


def _unpack_small(buf, shapes):
    out, r = [], 0
    for shp in shapes:
        n = 1
        for s in shp:
            n *= s
        nr = (n + PACK_W - 1) // PACK_W
        out.append(buf[..., r:r + nr, :].reshape(buf.shape[:-2] + (nr * PACK_W,))[..., :n].reshape(buf.shape[:-2] + shp))
        r += nr
    return out


def _unshard_last(g4):
    nd = g4.ndim
    perm = tuple(range(1, nd - 1)) + (0, nd - 1)
    t = g4.transpose(perm)
    return t.reshape(t.shape[:-2] + (-1,))


def _shard_last(full, s):
    n = full.shape[-1] // N_CHIPS
    return lax.dynamic_slice_in_dim(full, s * n, n, axis=full.ndim - 1)


WEIGHT_NAMES = ("w_ada", "b_ada", "ln_g", "ln_b", "gla_w_in", "gla_w_gk2", "gla_b_gk", "gla_g_norm", "gla_w_out",
                "att_w_in", "att_b_in", "att_rel_bias", "att_w_out", "ff_w1", "ff_w2")


def kernel(x, c, w_ada, b_ada, ln_g, ln_b, gla_w_in, gla_w_gk2, gla_b_gk, gla_g_norm, gla_w_out, att_w_in, att_b_in, att_rel_bias, att_w_out, ff_w1, ff_w2, loss_target, m_w_ada, m_b_ada, m_ln_g, m_ln_b, m_gla_w_in, m_gla_w_gk2, m_gla_b_gk, m_gla_g_norm, m_gla_w_out, m_att_w_in, m_att_b_in, m_att_rel_bias, m_att_w_out, m_ff_w1, m_ff_w2, v_w_ada, v_b_ada, v_ln_g, v_ln_b, v_gla_w_in, v_gla_w_gk2, v_gla_b_gk, v_gla_g_norm, v_gla_w_out, v_att_w_in, v_att_b_in, v_att_rel_bias, v_att_w_out, v_ff_w1, v_ff_w2):
    weights = dict(w_ada=w_ada, b_ada=b_ada, ln_g=ln_g, ln_b=ln_b, gla_w_in=gla_w_in, gla_w_gk2=gla_w_gk2,
                   gla_b_gk=gla_b_gk, gla_g_norm=gla_g_norm, gla_w_out=gla_w_out, att_w_in=att_w_in,
                   att_b_in=att_b_in, att_rel_bias=att_rel_bias, att_w_out=att_w_out, ff_w1=ff_w1, ff_w2=ff_w2)
    mom1 = dict(w_ada=m_w_ada, b_ada=m_b_ada, ln_g=m_ln_g, ln_b=m_ln_b, gla_w_in=m_gla_w_in, gla_w_gk2=m_gla_w_gk2,
                gla_b_gk=m_gla_b_gk, gla_g_norm=m_gla_g_norm, gla_w_out=m_gla_w_out, att_w_in=m_att_w_in,
                att_b_in=m_att_b_in, att_rel_bias=m_att_rel_bias, att_w_out=m_att_w_out, ff_w1=m_ff_w1, ff_w2=m_ff_w2)
    mom2 = dict(w_ada=v_w_ada, b_ada=v_b_ada, ln_g=v_ln_g, ln_b=v_ln_b, gla_w_in=v_gla_w_in, gla_w_gk2=v_gla_w_gk2,
                gla_b_gk=v_gla_b_gk, gla_g_norm=v_gla_g_norm, gla_w_out=v_gla_w_out, att_w_in=v_att_w_in,
                att_b_in=v_att_b_in, att_rel_bias=v_att_rel_bias, att_w_out=v_att_w_out, ff_w1=v_ff_w1, ff_w2=v_ff_w2)

    ax, ay, ac = lax.axis_index("x"), lax.axis_index("y"), lax.axis_index("c")
    chip = 2 * ax + ay
    dev = 2 * chip + ac
    c_idx = jnp.reshape(ac, (1,)).astype(jnp.int32)
    chip_idx = jnp.reshape(chip, (1,)).astype(jnp.int32)
    S = x.shape[1]
    x2 = x.reshape(S, D_MODEL)
    t2 = loss_target.reshape(S, D_MODEL)

    comm = Comm(weights, c_idx, chip_idx)

    small_rows = 16
    spack = _pack_small([c] + [weights[n] for n, _ in SMALL_SHARDED], small_rows)
    sg = all_gather8(spack, "gather_small").reshape(N_DEV, small_rows, PACK_W)
    parts = _unpack_small(sg, [(1, D_MODEL)] + [shp for _, shp in SMALL_SHARDED])
    c_all = parts[0].reshape(N_DEV, D_MODEL)
    small = {n: _unshard_last(p[0::2]) for (n, _), p in zip(SMALL_SHARDED, parts[1:])}
    small["gla_b_gk"] = gla_b_gk
    small["att_rel_bias"] = att_rel_bias

    c_act = silu_rows(jnp.pad(c_all, ((0, 128 - N_DEV), (0, 0))), "silu_c")
    wa = w_ada.astype(BF16).transpose(1, 0, 2).reshape(1, D_MODEL, DEPTH * 6 * D_MODEL // N_CHIPS)
    mods_part = mm_plain(c_act, wa, 0, "ada_fwd", tm=128)[:N_DEV]
    mg = all_gather8(mods_part, "gather_mods").reshape(N_CHIPS, 2, N_DEV, DEPTH, 6 * D_MODEL // N_CHIPS)
    mods_mine = lax.dynamic_index_in_dim(mg[:, 0], dev, axis=1, keepdims=False)
    mods = mods_mine.transpose(1, 0, 2).reshape(DEPTH, 6 * D_MODEL) + b_ada
    mods = mods.reshape(DEPTH, 6, D_MODEL)

    sq, grad_x, dmods, g_small = local_step(x2, t2, mods, comm, small)
    loss = lax.psum(0.5 * sq[0, 0] / D_MODEL, ("x", "y", "c"))

    g_shard = comm.reduce_end()

    dm_flat = dmods.reshape(DEPTH, 6 * D_MODEL)
    g_rows = 80
    gpack = _pack_small([dm_flat] + [g_small[n] for n in SMALL_GRAD_ORDER], g_rows)
    gg = all_gather8(gpack, "gather_small_grads").reshape(N_DEV, g_rows, PACK_W)
    gsum = sum_over_devices(gg, "sum_small_grads")
    sums = _unpack_small(gsum, [(DEPTH, 6 * D_MODEL)] + [SMALL_FULL[n] for n in SMALL_GRAD_ORDER])
    grads = dict(b_ada=sums[0])
    for n, full_g in zip(SMALL_GRAD_ORDER, sums[1:]):
        grads[n] = full_g if n in ("gla_b_gk", "att_rel_bias") else _shard_last(full_g, chip)
    dm_all = _unpack_small(gg, [(DEPTH, 6 * D_MODEL)])[0]
    dm_cols = _shard_last(dm_all, chip).reshape(N_DEV, DEPTH * 6 * D_MODEL // N_CHIPS)
    dm_cols = jnp.pad(dm_cols, ((0, 128 - N_DEV), (0, 0))).astype(BF16)
    gwa = mm_w(c_act, dm_cols, "ada_bwd", ts=128)
    grads["w_ada"] = gwa.reshape(D_MODEL, DEPTH, 6 * D_MODEL // N_CHIPS).transpose(1, 0, 2)
    grads.update(g_shard)

    deltas, new_m, new_v = {}, {}, {}
    for n in WEIGHT_NAMES:
        deltas[n], new_m[n], new_v[n] = adamw(weights[n], grads[n], mom1[n], mom2[n], "adamw_" + n)

    return (loss, grad_x.reshape(1, S, D_MODEL), *[grads[n] for n in WEIGHT_NAMES], *[deltas[n] for n in WEIGHT_NAMES],
            *[new_m[n] for n in WEIGHT_NAMES], *[new_v[n] for n in WEIGHT_NAMES])
```

```python
import functools

import jax
import jax.numpy as jnp
from jax import lax
from jax.experimental import pallas as pl
from jax.experimental.pallas import tpu as pltpu

F32 = jnp.float32
BF16 = jnp.bfloat16
HIGHEST = lax.Precision.HIGHEST
MESH = pl.DeviceIdType.MESH

D_MODEL = 1024
DEPTH = 4
CHUNK = 64
GLA_HEADS = 4
GLA_DK = 512
GLA_DV = 1024
GLA_DK_HEAD = 128
GLA_DV_HEAD = 256
GLA_RANK = 16
GLA_IN = 3088
GLA_IN_PAD = 3200
GLA_LR_OFF = 3072
ATT_HEADS = 16
ATT_HD = 64
LEFT_CHUNKS = 8
MAX_REL = 128
N_REL = 257
D_FF = 4096
ALPHA = (2.0 * DEPTH) ** 0.25
LN_EPS = 1e-5
RMS_EPS = 1e-6
NEG_INF = -1e30
GLA_SCALE = GLA_DK_HEAD ** -0.5
ATT_SCALE = ATT_HD ** -0.5
ADAM_LR = 0.001
ADAM_B1 = 0.9
ADAM_B2 = 0.999
ADAM_EPS = 1e-08
ADAM_WD = 0.01
ADAM_STEP = 10

ATT_TQ = 256
ATT_KW = 768
GLA_TB = 256
VMEM_LIMIT = 56 * 1024 * 1024
N_CHIPS = 4
N_DEV = 8
PACK_W = 1024


def _dot(a, b):
    return jnp.dot(a, b, preferred_element_type=F32)


def _dot_nt(a, b):
    return lax.dot_general(a, b, (((1,), (1,)), ((), ())), preferred_element_type=F32)


def _dot_tn(a, b):
    return lax.dot_general(a, b, (((0,), (0,)), ((), ())), preferred_element_type=F32)


def _cp(sem, vmem=VMEM_LIMIT):
    return pltpu.CompilerParams(dimension_semantics=sem, vmem_limit_bytes=vmem)


def _row_spec(n):
    return pl.BlockSpec((1, n), lambda *_: (0, 0))


def _sigmoid(x):
    return 1.0 / (1.0 + jnp.exp(-x))


def _log_sigmoid(x):
    return jnp.minimum(x, 0.0) - jnp.log1p(jnp.exp(-jnp.abs(x)))


class Side:
    def __init__(self, ins, out_shapes, n_sems, n_local, start, wait):
        self.ins, self.out_shapes, self.n_sems, self.n_local = list(ins), list(out_shapes), n_sems, n_local
        self.start, self.wait = start, wait

    def sem_shapes(self):
        return [pltpu.SemaphoreType.DMA((self.n_sems,)), pltpu.SemaphoreType.DMA((self.n_sems,)),
                pltpu.SemaphoreType.DMA((max(self.n_local, 1),))]


def run_side(side, name):
    n_in = len(side.ins)
    n_out = len(side.out_shapes)

    def body(*refs):
        ins, outs, sems = refs[:n_in], refs[n_in:n_in + n_out], refs[n_in + n_out:]
        side.start(ins, outs, *sems)
        side.wait(ins, outs, *sems)

    any_spec = pl.BlockSpec(memory_space=pl.ANY)
    return pl.pallas_call(body, name=name, out_shape=side.out_shapes, in_specs=[any_spec] * n_in,
                          out_specs=[any_spec] * n_out, scratch_shapes=side.sem_shapes())(*side.ins)


def hosted_call(main, side, *, name, grid, in_specs, out_specs, out_shape, scratch_shapes, dims, args):
    if side is None:
        outs = pl.pallas_call(main, name=name, grid=grid, in_specs=in_specs, out_specs=out_specs,
                              out_shape=out_shape, scratch_shapes=scratch_shapes, compiler_params=_cp(dims))(*args)
        return list(outs), []
    n_mi, n_mo, n_ms = len(in_specs), len(out_specs), len(scratch_shapes)
    n_si, n_so = len(side.ins), len(side.out_shapes)

    def kern(*refs):
        mi, si = refs[:n_mi], refs[n_mi:n_mi + n_si]
        o0 = n_mi + n_si
        mo, so = refs[o0:o0 + n_mo], refs[o0 + n_mo:o0 + n_mo + n_so]
        s0 = o0 + n_mo + n_so
        ms, sems = refs[s0:s0 + n_ms], refs[s0 + n_ms:]
        ids = [pl.program_id(d) for d in range(len(grid))]
        first = functools.reduce(jnp.logical_and, [i == 0 for i in ids])
        last = functools.reduce(jnp.logical_and, [i == g - 1 for i, g in zip(ids, grid)])

        @pl.when(first)
        def _():
            side.start(si, so, *sems)
        main(*mi, *mo, *ms)

        @pl.when(last)
        def _():
            side.wait(si, so, *sems)

    any_spec = pl.BlockSpec(memory_space=pl.ANY)
    outs = pl.pallas_call(
        kern, name=name, grid=grid, in_specs=list(in_specs) + [any_spec] * n_si,
        out_specs=list(out_specs) + [any_spec] * n_so, out_shape=list(out_shape) + side.out_shapes,
        scratch_shapes=list(scratch_shapes) + side.sem_shapes(),
        compiler_params=_cp(("arbitrary",) * len(grid)))(*args, *side.ins)
    return list(outs[:n_mo]), list(outs[n_mo:])


def modulate(x, sc, sh, name):
    S, D = x.shape
    tm = min(512, S)

    def kern(x_ref, sc_ref, sh_ref, u_ref):
        u_ref[...] = (x_ref[...] * (1.0 + sc_ref[...]) + sh_ref[...]).astype(BF16)

    return pl.pallas_call(
        kern, name=name, grid=(S // tm,),
        in_specs=[pl.BlockSpec((tm, D), lambda i: (i, 0)), _row_spec(D), _row_spec(D)],
        out_specs=pl.BlockSpec((tm, D), lambda i: (i, 0)),
        out_shape=jax.ShapeDtypeStruct((S, D), BF16),
        compiler_params=_cp(("parallel",)),
    )(x, sc, sh)


def loss_head(x, t, name):
    S, D = x.shape
    tm = min(512, S)

    def kern(x_ref, t_ref, dx_ref, l_ref):
        @pl.when(pl.program_id(0) == 0)
        def _():
            l_ref[...] = jnp.zeros_like(l_ref)
        e = x_ref[...] - t_ref[...]
        dx_ref[...] = e * (1.0 / D)
        l_ref[...] += jnp.sum(e * e)

    return pl.pallas_call(
        kern, name=name, grid=(S // tm,),
        in_specs=[pl.BlockSpec((tm, D), lambda i: (i, 0)), pl.BlockSpec((tm, D), lambda i: (i, 0))],
        out_specs=[pl.BlockSpec((tm, D), lambda i: (i, 0)), pl.BlockSpec((8, 128), lambda i: (0, 0))],
        out_shape=[jax.ShapeDtypeStruct((S, D), F32), jax.ShapeDtypeStruct((8, 128), F32)],
        compiler_params=_cp(("arbitrary",)),
    )(x, t)


def silu_rows(c_all, name):
    def kern(c_ref, o_ref):
        c = c_ref[...]
        o_ref[...] = (c * _sigmoid(c)).astype(BF16)

    return pl.pallas_call(kern, name=name, out_shape=jax.ShapeDtypeStruct(c_all.shape, BF16))(c_all)


def sum_over_devices(g, name):
    n, R, C = g.shape

    def kern(g_ref, o_ref):
        acc = g_ref[0]
        for d in range(1, n):
            acc = acc + g_ref[d]
        o_ref[...] = acc

    return pl.pallas_call(kern, name=name, out_shape=jax.ShapeDtypeStruct((R, C), F32))(g)


def _rows_block(R, C, budget=1 << 20):
    if R * C * 4 <= budget or R % 8:
        return R
    tr = max(8, (budget // (C * 4)) // 8 * 8)
    while R % tr:
        tr -= 8
    return tr


def adamw(w, g, m, v, name):
    shape = w.shape
    C = shape[-1]
    R = w.size // C
    w2, g2, m2, v2 = (t.reshape(R, C) for t in (w, g, m, v))
    tr = _rows_block(R, C)
    c1 = 1.0 - ADAM_B1 ** ADAM_STEP
    c2 = 1.0 - ADAM_B2 ** ADAM_STEP

    def kern(w_ref, g_ref, m_ref, v_ref, d_ref, nm_ref, nv_ref):
        gg = g_ref[...]
        nm = ADAM_B1 * m_ref[...] + (1.0 - ADAM_B1) * gg
        nv = ADAM_B2 * v_ref[...] + (1.0 - ADAM_B2) * (gg * gg)
        m_hat = nm / c1
        v_hat = nv / c2
        d_ref[...] = -ADAM_LR * (m_hat / (jnp.sqrt(v_hat) + ADAM_EPS) + ADAM_WD * w_ref[...])
        nm_ref[...] = nm
        nv_ref[...] = nv

    spec = pl.BlockSpec((tr, C), lambda i: (i, 0))
    outs = pl.pallas_call(
        kern, name=name, grid=(R // tr,),
        in_specs=[spec] * 4, out_specs=[spec] * 3,
        out_shape=[jax.ShapeDtypeStruct((R, C), F32)] * 3,
        compiler_params=_cp(("parallel",)),
    )(w2, g2, m2, v2)
    return tuple(o.reshape(shape) for o in outs)


def _tn_for(N):
    for tn in (1024, 768, 640, 512, 384, 256, 128):
        if N % tn == 0:
            return tn
    return N


def mm_plain(a, b3, layer, name, *, mode="f32", nt=False, bias=None, h=None, tm=512):
    M, K = a.shape
    N = b3.shape[1] if nt else b3.shape[2]
    tm = min(tm, M)
    tn = _tn_for(N)
    a_spec = pl.BlockSpec((tm, K), lambda j, i: (i, 0))
    if nt:
        b_spec = pl.BlockSpec((None, tn, K), lambda j, i: (layer, j, 0))
        dot = _dot_nt
    else:
        b_spec = pl.BlockSpec((None, K, tn), lambda j, i: (layer, 0, j))
        dot = _dot
    o_spec = pl.BlockSpec((tm, tn), lambda j, i: (i, j))
    ins, in_specs = [a, b3], [a_spec, b_spec]

    if mode in ("f32", "bf16"):
        odt = F32 if mode == "f32" else BF16
        if bias is not None:
            ins.append(bias)
            in_specs.append(pl.BlockSpec((1, tn), lambda j, i: (0, j)))

            def kern(a_ref, b_ref, bias_ref, o_ref):
                o_ref[...] = (dot(a_ref[...], b_ref[...]) + bias_ref[...]).astype(odt)
        else:
            def kern(a_ref, b_ref, o_ref):
                o_ref[...] = dot(a_ref[...], b_ref[...]).astype(odt)
        out_specs, out_shape = o_spec, jax.ShapeDtypeStruct((M, N), odt)
    elif mode == "mlp_up":
        def kern(a_ref, b_ref, act_ref):
            r = jnp.maximum(dot(a_ref[...], b_ref[...]), 0.0)
            act_ref[...] = (r * r).astype(BF16)
        out_specs, out_shape = o_spec, jax.ShapeDtypeStruct((M, N), BF16)
    elif mode == "mlp_dn":
        ins.append(h)
        in_specs.append(o_spec)

        def kern(a_ref, b_ref, h_ref, o_ref):
            acc = dot(a_ref[...], b_ref[...])
            o_ref[...] = (acc * (2.0 * jnp.sqrt(h_ref[...].astype(F32)))).astype(BF16)
        out_specs, out_shape = o_spec, jax.ShapeDtypeStruct((M, N), BF16)
    else:
        raise ValueError(mode)

    return pl.pallas_call(
        kern, name=name, grid=(N // tn, M // tm), in_specs=in_specs, out_specs=out_specs,
        out_shape=out_shape, compiler_params=_cp(("parallel", "parallel")),
    )(*ins)


def mm_down_ln(a, b3, layer, x_in, gate1p, ln_g, ln_b, sc_next, sh_next, name, *, tm=256):
    M, K = a.shape
    D = b3.shape[2]
    tm = min(tm, M)

    def kern(a_ref, b_ref, x_ref, gp_ref, lg_ref, lb_ref, sc_ref, sh_ref, y_ref, xo_ref, u_ref):
        y = _dot(a_ref[...], b_ref[...])
        y_ref[...] = y
        z = ALPHA * x_ref[...] + gp_ref[...] * y
        mu = jnp.mean(z, axis=-1, keepdims=True)
        zc = z - mu
        var = jnp.mean(zc * zc, axis=-1, keepdims=True)
        xo = (zc * lax.rsqrt(var + LN_EPS)) * lg_ref[...] + lb_ref[...]
        xo_ref[...] = xo
        u_ref[...] = (xo * (1.0 + sc_ref[...]) + sh_ref[...]).astype(BF16)

    tile = pl.BlockSpec((tm, D), lambda i: (i, 0))
    return pl.pallas_call(
        kern, name=name, grid=(M // tm,),
        in_specs=[pl.BlockSpec((tm, K), lambda i: (i, 0)), pl.BlockSpec((None, K, D), lambda i: (layer, 0, 0)), tile]
        + [_row_spec(D)] * 5,
        out_specs=[tile, tile, tile],
        out_shape=[jax.ShapeDtypeStruct((M, D), F32)] * 2 + [jax.ShapeDtypeStruct((M, D), BF16)],
        compiler_params=_cp(("parallel",)),
    )(a, b3, x_in, gate1p, ln_g, ln_b, sc_next, sh_next)


def mm_down_comb(a, b3, layer, dz, x_in, sc1p, name, *, parts=1, tm=256):
    D, K = b3.shape[1], b3.shape[2]
    M = a.shape[-2]
    kp = K // parts
    tm = min(tm, M)

    def kern(*refs):
        a_refs = refs[:parts]
        b_ref, dz_ref, x_ref, sp_ref, dx_ref, s_ref = refs[parts:]

        @pl.when(pl.program_id(0) == 0)
        def _():
            s_ref[...] = jnp.zeros_like(s_ref)
        if parts == 1:
            du = _dot_nt(a_refs[0][...], b_ref[...])
        else:
            du = _dot_nt(a_refs[0][...], b_ref[:, 0:kp])
            for p in range(1, parts):
                du = du + _dot_nt(a_refs[p][...], b_ref[:, p * kp:(p + 1) * kp])
        dx_ref[...] = ALPHA * dz_ref[...] + du * sp_ref[...]
        s_ref[0:1, :] += jnp.sum(du * x_ref[...], axis=0, keepdims=True)
        s_ref[1:2, :] += jnp.sum(du, axis=0, keepdims=True)

    tile = pl.BlockSpec((tm, D), lambda i: (i, 0))
    if parts == 1:
        a_ins, a_specs = [a], [pl.BlockSpec((tm, K), lambda i: (i, 0))]
    else:
        a_ins = [a] * parts
        a_specs = [pl.BlockSpec((None, tm, kp), functools.partial(lambda i, p: (p, i, 0), p=p)) for p in range(parts)]
    return pl.pallas_call(
        kern, name=name, grid=(M // tm,),
        in_specs=a_specs + [pl.BlockSpec((None, D, K), lambda i: (layer, 0, 0)), tile, tile, _row_spec(D)],
        out_specs=[tile, pl.BlockSpec((8, D), lambda i: (0, 0))],
        out_shape=[jax.ShapeDtypeStruct((M, D), F32), jax.ShapeDtypeStruct((8, D), F32)],
        compiler_params=_cp(("arbitrary",)),
    )(*a_ins, b3, dz, x_in, sc1p)


def mm_w(a, b, name, *, ts=1024, tk=512, chips_out=False, b_parts=1, tn=None):
    S, K = a.shape
    npart = b.shape[-1]
    N = npart * b_parts
    ts = min(ts, S)
    tk = min(tk, K)
    n_chip = N // N_CHIPS
    if tn is None:
        tn = _tn_for(n_chip if chips_out else npart)
    assert npart % tn == 0 and (not chips_out or n_chip % tn == 0)

    def kern(a_ref, b_ref, o_ref):
        @pl.when(pl.program_id(2) == 0)
        def _():
            o_ref[...] = jnp.zeros_like(o_ref)
        o_ref[...] += _dot_tn(a_ref[...], b_ref[...])

    if b_parts == 1:
        b_spec = pl.BlockSpec((ts, tn), lambda k, n, s: (s, n))
    else:
        per = npart // tn
        b_spec = pl.BlockSpec((None, ts, tn), lambda k, n, s: (n // per, s, n % per))
    if chips_out:
        per_chip = n_chip // tn
        o_spec = pl.BlockSpec((None, tk, tn), lambda k, n, s: (n // per_chip, k, n % per_chip))
        out_shape = jax.ShapeDtypeStruct((N_CHIPS, K, n_chip), F32)
    else:
        o_spec = pl.BlockSpec((tk, tn), lambda k, n, s: (k, n))
        out_shape = jax.ShapeDtypeStruct((K, N), F32)
    return pl.pallas_call(
        kern, name=name, grid=(K // tk, N // tn, S // ts),
        in_specs=[pl.BlockSpec((ts, tk), lambda k, n, s: (s, k)), b_spec],
        out_specs=o_spec, out_shape=out_shape,
        compiler_params=_cp(("parallel", "parallel", "arbitrary")),
    )(a, b)


def mm_w_chips3(a, b3, name, *, ts=512):
    S, K = a.shape
    P = b3.shape[2]
    n_chip = 3 * P // N_CHIPS
    ts = min(ts, S)
    pieces = []
    for chip in range(N_CHIPS):
        lo, hi = chip * n_chip, (chip + 1) * n_chip
        while lo < hi:
            part = lo // P
            w = min(hi, (part + 1) * P) - lo
            pieces.append((chip, lo - chip * n_chip, part, lo - part * P, w))
            lo += w

    def kern(a_ref, b_ref, o_ref):
        @pl.when(pl.program_id(0) == 0)
        def _():
            o_ref[...] = jnp.zeros_like(o_ref)
        at = a_ref[...].T
        for chip, oc, part, pc, w in pieces:
            o_ref[chip, :, oc:oc + w] += _dot(at, b_ref[part, :, pc:pc + w])

    return pl.pallas_call(
        kern, name=name, grid=(S // ts,),
        in_specs=[pl.BlockSpec((ts, K), lambda s: (s, 0)), pl.BlockSpec((3, ts, P), lambda s: (0, s, 0))],
        out_specs=pl.BlockSpec((N_CHIPS, K, n_chip), lambda s: (0, 0, 0)),
        out_shape=jax.ShapeDtypeStruct((N_CHIPS, K, n_chip), F32),
        compiler_params=_cp(("arbitrary",)),
    )(a, b3)


def mm_f32(a, b, name):
    def kern(a_ref, b_ref, o_ref):
        o_ref[...] = jnp.dot(a_ref[...], b_ref[...], precision=HIGHEST, preferred_element_type=F32)

    return pl.pallas_call(kern, name=name, out_shape=jax.ShapeDtypeStruct((a.shape[0], b.shape[1]), F32),
                          compiler_params=pltpu.CompilerParams(vmem_limit_bytes=VMEM_LIMIT))(a, b)


def ln_bwd(dxo, x_in, y, gate1p, ln_g, name, *, tm=256):
    S, D = dxo.shape
    tm = min(tm, S)

    def kern(dxo_ref, x_ref, y_ref, gp_ref, lg_ref, dz_ref, dy_ref, s_ref):
        @pl.when(pl.program_id(0) == 0)
        def _():
            s_ref[...] = jnp.zeros_like(s_ref)
        dxo_t = dxo_ref[...]
        yv = y_ref[...]
        z = ALPHA * x_ref[...] + gp_ref[...] * yv
        mu = jnp.mean(z, axis=-1, keepdims=True)
        zc = z - mu
        var = jnp.mean(zc * zc, axis=-1, keepdims=True)
        rstd = lax.rsqrt(var + LN_EPS)
        xhat = zc * rstd
        dxh = dxo_t * lg_ref[...]
        dz = rstd * (dxh - jnp.mean(dxh, axis=-1, keepdims=True)
                     - xhat * jnp.mean(dxh * xhat, axis=-1, keepdims=True))
        dz_ref[...] = dz
        dy_ref[...] = (gp_ref[...] * dz).astype(BF16)
        s_ref[0:1, :] += jnp.sum(dxo_t * xhat, axis=0, keepdims=True)
        s_ref[1:2, :] += jnp.sum(dxo_t, axis=0, keepdims=True)
        s_ref[2:3, :] += jnp.sum(dz * yv, axis=0, keepdims=True)

    tile = pl.BlockSpec((tm, D), lambda i: (i, 0))
    return pl.pallas_call(
        kern, name=name, grid=(S // tm,),
        in_specs=[tile, tile, tile, _row_spec(D), _row_spec(D)],
        out_specs=[tile, tile, pl.BlockSpec((8, D), lambda i: (0, 0))],
        out_shape=[jax.ShapeDtypeStruct((S, D), F32), jax.ShapeDtypeStruct((S, D), BF16),
                   jax.ShapeDtypeStruct((8, D), F32)],
        compiler_params=_cp(("arbitrary",)),
    )(dxo, x_in, y, gate1p, ln_g)


def _tri64():
    r = lax.broadcasted_iota(jnp.int32, (CHUNK, CHUNK), 0)
    c = lax.broadcasted_iota(jnp.int32, (CHUNK, CHUNK), 1)
    return r >= c


def _gla_chunk_common(proj_ref, rows, b, h):
    kc = slice(h * GLA_DK_HEAD, (h + 1) * GLA_DK_HEAD)
    bh = b[:, kc]
    ep = jnp.exp(bh)
    en = jnp.exp(-bh)
    bl = bh[CHUNK - 1:CHUNK, :]
    ee = jnp.exp(bl - bh)
    dec = jnp.exp(bl)
    q = proj_ref[rows, h * GLA_DK_HEAD:(h + 1) * GLA_DK_HEAD] * GLA_SCALE
    k = proj_ref[rows, GLA_DK + h * GLA_DK_HEAD:GLA_DK + (h + 1) * GLA_DK_HEAD]
    v = proj_ref[rows, 2 * GLA_DK + h * GLA_DV_HEAD:2 * GLA_DK + (h + 1) * GLA_DV_HEAD]
    g = proj_ref[rows, 2 * GLA_DK + GLA_DV + h * GLA_DV_HEAD:2 * GLA_DK + GLA_DV + (h + 1) * GLA_DV_HEAD]
    return ep, en, ee, dec, q, k, v, g


def gla_fwd(proj, wgk_p, bgk, gnorm, name, side=None):
    S = proj.shape[0]
    TB = min(GLA_TB, S)
    ncb = TB // CHUNK

    def kern(proj_ref, wgk_ref, bgk_ref, gn_ref, zg_ref, st_ref, state_scr, la_scr):
        @pl.when(pl.program_id(0) == 0)
        def _():
            state_scr[...] = jnp.zeros_like(state_scr)
        lr = proj_ref[:, GLA_LR_OFF:GLA_IN_PAD].astype(BF16)
        gk = _dot(lr, wgk_ref[...]) + bgk_ref[...]
        la_scr[...] = _log_sigmoid(gk) * (1.0 / 16.0)
        lower = _tri64()
        tri = lower.astype(F32)

        def chunk(c, carry):
            rows = pl.ds(pl.multiple_of(c * CHUNK, CHUNK), CHUNK)
            b = jnp.dot(tri, la_scr[rows, :], precision=HIGHEST, preferred_element_type=F32)
            for h in range(GLA_HEADS):
                ep, en, ee, dec, q, k, v, g = _gla_chunk_common(proj_ref, rows, b, h)
                qf = (q * ep).astype(BF16)
                a_f = _dot_nt(qf, (k * en).astype(BF16))
                a_b = _dot_nt((q * en).astype(BF16), (k * ep).astype(BF16))
                amat = jnp.where(lower, a_f, a_b).astype(BF16)
                st = state_scr[h]
                st_ref[c, h] = st
                vb = v.astype(BF16)
                o = _dot(amat, vb) + _dot_nt(qf, st.astype(BF16))
                r = lax.rsqrt(jnp.mean(o * o, axis=-1, keepdims=True) + RMS_EPS)
                on = (o * r) * gn_ref[:, h * GLA_DV_HEAD:(h + 1) * GLA_DV_HEAD]
                zg_ref[rows, h * GLA_DV_HEAD:(h + 1) * GLA_DV_HEAD] = (on * (g * _sigmoid(g))).astype(BF16)
                state_scr[h] = st * dec + _dot_tn(vb, (k * ee).astype(BF16))
            return carry

        lax.fori_loop(0, ncb, chunk, 0)

    return hosted_call(
        kern, side, name=name, grid=(S // TB,),
        in_specs=[pl.BlockSpec((TB, GLA_IN_PAD), lambda i: (i, 0)),
                  pl.BlockSpec((128, GLA_DK), lambda i: (0, 0)), _row_spec(GLA_DK), _row_spec(GLA_DV)],
        out_specs=[pl.BlockSpec((TB, GLA_DV), lambda i: (i, 0)),
                   pl.BlockSpec((ncb, GLA_HEADS, GLA_DV_HEAD, GLA_DK_HEAD), lambda i: (i, 0, 0, 0))],
        out_shape=[jax.ShapeDtypeStruct((S, GLA_DV), BF16),
                   jax.ShapeDtypeStruct((S // CHUNK, GLA_HEADS, GLA_DV_HEAD, GLA_DK_HEAD), F32)],
        scratch_shapes=[pltpu.VMEM((GLA_HEADS, GLA_DV_HEAD, GLA_DK_HEAD), F32), pltpu.VMEM((TB, GLA_DK), F32)],
        dims=("arbitrary",), args=(proj, wgk_p, bgk, gnorm))


def gla_bwd(proj, states, dzg, wgk_p, bgk, gnorm, name, side=None):
    S = proj.shape[0]
    TB = min(GLA_TB, S)
    ncb = TB // CHUNK
    nb = S // TB

    def kern(proj_ref, st_ref, dzg_ref, wgk_ref, bgk_ref, gn_ref,
             dproj_ref, dwgk_ref, dbgk_ref, dgn_ref, dstate_scr, la_scr, gk_scr, dgk_scr):
        @pl.when(pl.program_id(0) == 0)
        def _():
            dstate_scr[...] = jnp.zeros_like(dstate_scr)
            dwgk_ref[...] = jnp.zeros_like(dwgk_ref)
            dbgk_ref[...] = jnp.zeros_like(dbgk_ref)
            dgn_ref[...] = jnp.zeros_like(dgn_ref)
        lr = proj_ref[:, GLA_LR_OFF:GLA_IN_PAD].astype(BF16)
        gk = _dot(lr, wgk_ref[...]) + bgk_ref[...]
        gk_scr[...] = gk
        la_scr[...] = _log_sigmoid(gk) * (1.0 / 16.0)
        lower = _tri64()
        tri = lower.astype(F32)
        r_i = lax.broadcasted_iota(jnp.int32, (CHUNK, CHUNK), 0)
        c_i = lax.broadcasted_iota(jnp.int32, (CHUNK, CHUNK), 1)
        triu = (c_i >= r_i).astype(F32)
        last_row = lax.broadcasted_iota(jnp.int32, (CHUNK, GLA_DK_HEAD), 0) == CHUNK - 1

        def chunk(cc, carry):
            c = ncb - 1 - cc
            rows = pl.ds(pl.multiple_of(c * CHUNK, CHUNK), CHUNK)
            b = jnp.dot(tri, la_scr[rows, :], precision=HIGHEST, preferred_element_type=F32)
            for h in range(GLA_HEADS):
                kc = slice(h * GLA_DK_HEAD, (h + 1) * GLA_DK_HEAD)
                vc = slice(h * GLA_DV_HEAD, (h + 1) * GLA_DV_HEAD)
                ep, en, ee, dec, q, k, v, g = _gla_chunk_common(proj_ref, rows, b, h)
                qf = q * ep
                kn = k * en
                qn = q * en
                kp = k * ep
                ke = k * ee
                qf_b, kn_b, qn_b, kp_b, ke_b = (t.astype(BF16) for t in (qf, kn, qn, kp, ke))
                vb = v.astype(BF16)
                amat = jnp.where(lower, _dot_nt(qf_b, kn_b), _dot_nt(qn_b, kp_b)).astype(BF16)
                st = st_ref[c, h]
                st_b = st.astype(BF16)
                o = _dot(amat, vb) + _dot_nt(qf_b, st_b)
                r = lax.rsqrt(jnp.mean(o * o, axis=-1, keepdims=True) + RMS_EPS)
                oh = o * r
                gn = gn_ref[:, vc]
                sg = _sigmoid(g)
                dz = dzg_ref[rows, vc]
                don = dz * (g * sg)
                dg = dz * (oh * gn) * (sg * (1.0 + g * (1.0 - sg)))
                dgn_ref[:, vc] += jnp.sum(don * oh, axis=0, keepdims=True)
                doh = don * gn
                do = r * (doh - oh * jnp.mean(doh * oh, axis=-1, keepdims=True))
                do_b = do.astype(BF16)
                dst = dstate_scr[h]
                dst_b = dst.astype(BF16)
                dv = _dot_tn(amat, do_b) + _dot_nt(ke_b, dst_b)
                da = _dot_nt(do_b, vb)
                da_f = jnp.where(lower, da, 0.0).astype(BF16)
                da_b = jnp.where(lower, 0.0, da).astype(BF16)
                dqf = _dot(da_f, kn_b) + _dot(do_b, st_b)
                dkn = _dot_tn(da_f, qf_b)
                dqn = _dot(da_b, kp_b)
                dkp = _dot_tn(da_b, qn_b)
                dke = _dot(vb, dst_b)
                ddec = jnp.sum(dst * st, axis=0, keepdims=True)
                dstate_scr[h] = dst * dec + _dot_tn(do_b, qf_b)
                dq = (dqf * ep + dqn * en) * GLA_SCALE
                dk = dkn * en + dkp * ep + dke * ee
                db = dqf * qf - dkn * kn - dqn * qn + dkp * kp - dke * ke
                dbl = jnp.sum(dke * ke, axis=0, keepdims=True) + ddec * dec
                db = db + jnp.where(last_row, dbl, 0.0)
                dla = jnp.dot(triu, db, precision=HIGHEST, preferred_element_type=F32)
                dgk_scr[rows, kc] = dla * (1.0 / 16.0) * _sigmoid(-gk_scr[rows, kc])
                dproj_ref[rows, kc] = dq.astype(BF16)
                dproj_ref[rows, GLA_DK + h * GLA_DK_HEAD:GLA_DK + (h + 1) * GLA_DK_HEAD] = dk.astype(BF16)
                dproj_ref[rows, 2 * GLA_DK + h * GLA_DV_HEAD:2 * GLA_DK + (h + 1) * GLA_DV_HEAD] = dv.astype(BF16)
                dproj_ref[rows, 2 * GLA_DK + GLA_DV + h * GLA_DV_HEAD:
                          2 * GLA_DK + GLA_DV + (h + 1) * GLA_DV_HEAD] = dg.astype(BF16)
            return carry

        lax.fori_loop(0, ncb, chunk, 0)
        dgk = dgk_scr[...]
        dgk_b = dgk.astype(BF16)
        dproj_ref[:, GLA_LR_OFF:GLA_IN_PAD] = _dot_nt(dgk_b, wgk_ref[...]).astype(BF16)
        dwgk_ref[...] += _dot_tn(lr, dgk_b)
        dbgk_ref[...] += jnp.sum(dgk, axis=0, keepdims=True)

    rev = lambda i: (nb - 1 - i, 0)
    return hosted_call(
        kern, side, name=name, grid=(nb,),
        in_specs=[pl.BlockSpec((TB, GLA_IN_PAD), rev),
                  pl.BlockSpec((ncb, GLA_HEADS, GLA_DV_HEAD, GLA_DK_HEAD), lambda i: (nb - 1 - i, 0, 0, 0)),
                  pl.BlockSpec((TB, GLA_DV), rev),
                  pl.BlockSpec((128, GLA_DK), lambda i: (0, 0)), _row_spec(GLA_DK), _row_spec(GLA_DV)],
        out_specs=[pl.BlockSpec((TB, GLA_IN_PAD), rev),
                   pl.BlockSpec((128, GLA_DK), lambda i: (0, 0)), _row_spec(GLA_DK), _row_spec(GLA_DV)],
        out_shape=[jax.ShapeDtypeStruct((S, GLA_IN_PAD), BF16), jax.ShapeDtypeStruct((128, GLA_DK), F32),
                   jax.ShapeDtypeStruct((1, GLA_DK), F32), jax.ShapeDtypeStruct((1, GLA_DV), F32)],
        scratch_shapes=[pltpu.VMEM((GLA_HEADS, GLA_DV_HEAD, GLA_DK_HEAD), F32), pltpu.VMEM((TB, GLA_DK), F32),
                        pltpu.VMEM((TB, GLA_DK), F32), pltpu.VMEM((TB, GLA_DK), F32)],
        dims=("arbitrary",), args=(proj, states, dzg, wgk_p, bgk, gnorm))


ATT_TW = 1024
ATT_CLASSES = 3


def _att_window(i):
    return pl.multiple_of(jnp.maximum(i * ATT_TQ - LEFT_CHUNKS * CHUNK, 0), ATT_TQ)


def _att_rel_index():
    e = jnp.arange(ATT_TW)[None, :]
    d = jnp.where(e < ATT_KW, e, e - ATT_TW)
    off = (jnp.arange(ATT_CLASSES) * ATT_TQ)[:, None]
    return jnp.clip(off - d, -MAX_REL, MAX_REL) + MAX_REL


def _row_bits():
    return lax.broadcasted_iota(jnp.int32, (ATT_TQ, ATT_TW), 0)


def att_bias_tiles(rel_bias, name):
    tab = jnp.take(rel_bias, _att_rel_index(), axis=1).reshape(ATT_HEADS * ATT_CLASSES, 1, ATT_TW)

    def kern(t_ref, o_ref):
        cls = pl.program_id(0) % ATT_CLASSES
        x = jnp.broadcast_to(t_ref[...], (ATT_TQ, ATT_TW))
        row = _row_bits()
        for b in range(8):
            x = jnp.where((row & (1 << b)) != 0, pltpu.roll(x, 1 << b, axis=1), x)
        x = x[:, :ATT_KW]
        qc = cls * (ATT_TQ // CHUNK) + lax.shift_right_arithmetic(
            lax.broadcasted_iota(jnp.int32, (ATT_TQ, ATT_KW), 0), 6)
        kc = lax.shift_right_arithmetic(lax.broadcasted_iota(jnp.int32, (ATT_TQ, ATT_KW), 1), 6)
        o_ref[...] = jnp.where((kc <= qc) & (kc >= qc - LEFT_CHUNKS), x, NEG_INF)

    return pl.pallas_call(
        kern, name=name, grid=(ATT_HEADS * ATT_CLASSES,),
        in_specs=[pl.BlockSpec((None, 1, ATT_TW), lambda i: (i, 0, 0))],
        out_specs=pl.BlockSpec((None, ATT_TQ, ATT_KW), lambda i: (i, 0, 0)),
        out_shape=jax.ShapeDtypeStruct((ATT_HEADS * ATT_CLASSES, ATT_TQ, ATT_KW), F32),
        compiler_params=_cp(("parallel",)),
    )(tab)


def att_bias_grad(dbt, name):
    def kern(d_ref, o_ref):
        x = jnp.concatenate([d_ref[...], jnp.zeros((ATT_TQ, ATT_TW - ATT_KW), F32)], axis=1)
        row = _row_bits()
        for b in range(8):
            x = jnp.where((row & (1 << b)) != 0, pltpu.roll(x, ATT_TW - (1 << b), axis=1), x)
        o_ref[...] = jnp.sum(x, axis=0, keepdims=True)

    diag = pl.pallas_call(
        kern, name=name + "_diag", grid=(ATT_HEADS * ATT_CLASSES,),
        in_specs=[pl.BlockSpec((None, ATT_TQ, ATT_KW), lambda i: (i, 0, 0))],
        out_specs=pl.BlockSpec((None, 1, ATT_TW), lambda i: (i, 0, 0)),
        out_shape=jax.ShapeDtypeStruct((ATT_HEADS * ATT_CLASSES, 1, ATT_TW), F32),
        compiler_params=_cp(("parallel",)),
    )(dbt)
    diag = diag.reshape(ATT_HEADS, ATT_CLASSES * ATT_TW)
    onehot = (_att_rel_index().reshape(-1)[:, None] == jnp.arange(384)[None, :]).astype(F32)
    return mm_f32(diag, onehot, name + "_bins")[:, :N_REL]


def _att_scores(q_ref, kw, bias_ref, hh):
    hs = slice(hh * ATT_HD, (hh + 1) * ATT_HD)
    q = q_ref[:, hs] * ATT_SCALE
    k = kw[:, hs]
    s = _dot_nt(q, k) + bias_ref[hh]
    e = jnp.exp(s - jnp.max(s, axis=-1, keepdims=True))
    inv = 1.0 / jnp.sum(e, axis=-1, keepdims=True)
    return q, k, e, inv


def _att_specs(S):
    nq = D_MODEL // 128
    q_spec = pl.BlockSpec((ATT_TQ, 128), lambda p, i: (i, p))
    k_spec = pl.BlockSpec((S, 128), lambda p, i: (0, nq + p))
    v_spec = pl.BlockSpec((S, 128), lambda p, i: (0, 2 * nq + p))
    b_spec = pl.BlockSpec((2, None, ATT_TQ, ATT_KW), lambda p, i: (p, jnp.minimum(i, ATT_CLASSES - 1), 0, 0))
    return q_spec, k_spec, v_spec, b_spec


def attn_fwd(qkv, bias, name, side=None):
    S = qkv.shape[0]
    q_spec, k_spec, v_spec, b_spec = _att_specs(S)

    def kern(q_ref, k_ref, v_ref, bias_ref, o_ref):
        ws = _att_window(pl.program_id(1))
        kw = k_ref[pl.ds(ws, ATT_KW), :]
        vw = v_ref[pl.ds(ws, ATT_KW), :]
        outs = []
        for hh in range(2):
            _, _, e, inv = _att_scores(q_ref, kw, bias_ref, hh)
            outs.append(_dot(e.astype(BF16), vw[:, hh * ATT_HD:(hh + 1) * ATT_HD]) * inv)
        o_ref[...] = jnp.concatenate(outs, axis=1).astype(BF16)

    return hosted_call(
        kern, side, name=name, grid=(ATT_HEADS // 2, S // ATT_TQ),
        in_specs=[q_spec, k_spec, v_spec, b_spec],
        out_specs=[pl.BlockSpec((ATT_TQ, 128), lambda p, i: (i, p))],
        out_shape=[jax.ShapeDtypeStruct((S, D_MODEL), BF16)],
        scratch_shapes=[], dims=("parallel", "arbitrary"), args=(qkv, qkv, qkv, bias))


def attn_bwd(qkv, bias, do, name, side=None):
    S = qkv.shape[0]
    nblk = S // ATT_TQ
    q_spec, k_spec, v_spec, b_spec = _att_specs(S)

    def kern(q_ref, k_ref, v_ref, bias_ref, do_ref, dqkv_ref, db_ref, dk_scr, dv_scr):
        i = pl.program_id(1)

        @pl.when(i == 0)
        def _():
            dk_scr[...] = jnp.zeros_like(dk_scr)
            dv_scr[...] = jnp.zeros_like(dv_scr)
            db_ref[...] = jnp.zeros_like(db_ref)
        ws = _att_window(i)
        win = pl.ds(ws, ATT_KW)
        kw = k_ref[win, :]
        vw = v_ref[win, :]
        o_cls = jnp.minimum(i, ATT_CLASSES - 1)
        dqs, dks, dvs = [], [], []
        for hh in range(2):
            hs = slice(hh * ATT_HD, (hh + 1) * ATT_HD)
            q, k, e, inv = _att_scores(q_ref, kw, bias_ref, hh)
            p = e * inv
            do_h = do_ref[:, hs]
            dp = _dot_nt(do_h, vw[:, hs])
            ds = p * (dp - jnp.sum(p * dp, axis=-1, keepdims=True))
            db_ref[hh, o_cls] += ds
            ds_b = ds.astype(BF16)
            dqs.append(_dot(ds_b, k) * ATT_SCALE)
            dks.append(_dot_tn(ds_b, q))
            dvs.append(_dot_tn(p.astype(BF16), do_h))
        dqkv_ref[0, pl.ds(pl.multiple_of(i * ATT_TQ, ATT_TQ), ATT_TQ), :] = jnp.concatenate(dqs, axis=1).astype(BF16)
        dk_scr[win, :] += jnp.concatenate(dks, axis=1)
        dv_scr[win, :] += jnp.concatenate(dvs, axis=1)

        @pl.when(i == nblk - 1)
        def _():
            dqkv_ref[1] = dk_scr[...].astype(BF16)
            dqkv_ref[2] = dv_scr[...].astype(BF16)

    return hosted_call(
        kern, side, name=name, grid=(ATT_HEADS // 2, nblk),
        in_specs=[q_spec, k_spec, v_spec, b_spec, pl.BlockSpec((ATT_TQ, 128), lambda p, i: (i, p))],
        out_specs=[pl.BlockSpec((3, S, 128), lambda p, i: (0, 0, p)),
                   pl.BlockSpec((2, ATT_CLASSES, ATT_TQ, ATT_KW), lambda p, i: (p, 0, 0, 0))],
        out_shape=[jax.ShapeDtypeStruct((3, S, D_MODEL), BF16),
                   jax.ShapeDtypeStruct((ATT_HEADS, ATT_CLASSES, ATT_TQ, ATT_KW), F32)],
        scratch_shapes=[pltpu.VMEM((S, 128), F32), pltpu.VMEM((S, 128), F32)],
        dims=("parallel", "arbitrary"), args=(qkv, qkv, qkv, bias, do))


def colsum3(a3, name):
    P, S, N = a3.shape
    tm = min(512, S)

    def kern(a_ref, o_ref):
        @pl.when(pl.program_id(1) == 0)
        def _():
            o_ref[...] = jnp.zeros_like(o_ref)
        o_ref[...] += jnp.sum(a_ref[...].astype(F32), axis=0, keepdims=True)

    return pl.pallas_call(
        kern, name=name, grid=(P, S // tm),
        in_specs=[pl.BlockSpec((None, tm, N), lambda p, i: (p, i, 0))],
        out_specs=pl.BlockSpec((None, 1, N), lambda p, i: (p, 0, 0)),
        out_shape=jax.ShapeDtypeStruct((P, 1, N), F32),
        compiler_params=_cp(("parallel", "arbitrary")),
    )(a3)


def _me():
    return lax.axis_index("x"), lax.axis_index("y"), lax.axis_index("c")


def _other_chips(x, y):
    return [(1 - x, y), (x, 1 - y), (1 - x, 1 - y)]


def all_gather8(x_shard, name):
    m_per, n = x_shard.shape

    def body(x_ref, out_ref, send_sems, recv_sems, local_sem):
        x, y, c = _me()
        me, sibling = (x, y, c), (x, y, 1 - c)
        chips = _other_chips(x, y)

        def rows(px, py, pc):
            return out_ref.at[pl.ds((4 * px + 2 * py + pc) * m_per, m_per), :]

        def copy(k, block, to, src=None):
            return pltpu.make_async_remote_copy(
                src_ref=rows(*block) if src is None else src, dst_ref=rows(*block),
                send_sem=send_sems.at[k], recv_sem=recv_sems.at[k], device_id=to, device_id_type=MESH)

        mine = pltpu.make_async_copy(x_ref, rows(*me), local_sem)
        mine.start()
        first = [copy(0, me, sibling, src=x_ref)]
        first += [copy(1 + j, me, (*chip, c), src=x_ref) for j, chip in enumerate(chips)]
        for cp in first:
            cp.start()
        passed = [copy(4 + j, (*chip, c), sibling) for j, chip in enumerate(chips)]
        for j, chip in enumerate(chips):
            copy(1 + j, (*chip, c), me).wait_recv()
            passed[j].start()
        copy(0, sibling, me).wait_recv()
        for j, chip in enumerate(chips):
            copy(4 + j, (*chip, 1 - c), me).wait_recv()
        for cp in first + passed:
            cp.wait_send()
        mine.wait()

    return pl.pallas_call(
        body, name=name,
        out_shape=jax.ShapeDtypeStruct((N_DEV * m_per, n), x_shard.dtype),
        in_specs=[pl.BlockSpec(memory_space=pltpu.VMEM)],
        out_specs=pl.BlockSpec(memory_space=pltpu.VMEM),
        scratch_shapes=[pltpu.SemaphoreType.DMA((7,)), pltpu.SemaphoreType.DMA((7,)), pltpu.SemaphoreType.DMA],
        compiler_params=pltpu.CompilerParams(vmem_limit_bytes=VMEM_LIMIT),
    )(x_shard)


def _half_rows(n_rows, c):
    h = n_rows // 2
    return pl.ds(c * h, h)


def _gathered_shape(shape, flavour):
    L, a, b = shape
    return {"col": (L, a, N_CHIPS * b), "row": (L, N_CHIPS * a, b), "lead": (N_CHIPS, L, a, b)}[flavour]


def _gathered_part(out_ref, shape, flavour, s, rows):
    L, a, b = shape
    if flavour == "col":
        return out_ref.at[:, rows, pl.ds(s * b, b)]
    if flavour == "row":
        return out_ref.at[:, pl.ds(s * a + rows.start, rows.size), :]
    return out_ref.at[s, :, rows, :]


def gather_side(shards, flavours):
    n = len(shards)
    shapes = [w.shape for w in shards]

    def copies(w_refs, out_refs, send_sems, recv_sems, local_sems):
        x, y, c = _me()
        sibling = (x, y, 1 - c)
        chips = _other_chips(x, y)
        me_s = 2 * x + y

        def copy(k, src, dst, to):
            return pltpu.make_async_remote_copy(src_ref=src, dst_ref=dst, send_sem=send_sems.at[k],
                                                recv_sem=recv_sems.at[k], device_id=to, device_id_type=MESH)

        first, landed, passed, passed_in = [], [], [], []
        for w in range(n):
            shp, fl = shapes[w], flavours[w]
            my_half = _half_rows(shp[1], c)
            sib_half = _half_rows(shp[1], 1 - c)
            for j, chip in enumerate(chips):
                s = 2 * chip[0] + chip[1]
                first.append(copy(6 * w + j, w_refs[w].at[:, my_half, :],
                                  _gathered_part(out_refs[w], shp, fl, me_s, my_half), (*chip, c)))
                part = _gathered_part(out_refs[w], shp, fl, s, my_half)
                landed.append(copy(6 * w + j, part, part, (*chip, c)))
                passed.append(copy(6 * w + 3 + j, part, part, sibling))
                theirs = _gathered_part(out_refs[w], shp, fl, s, sib_half)
                passed_in.append(copy(6 * w + 3 + j, theirs, theirs, sibling))
        return first, landed, passed, passed_in

    def start(*refs):
        for cp in copies(*refs)[0]:
            cp.start()

    def wait(*refs):
        first, landed, passed, passed_in = copies(*refs)
        for arrived, onward in zip(landed, passed):
            arrived.wait_recv()
            onward.start()
        for cp in passed_in:
            cp.wait_recv()
        for cp in first + passed:
            cp.wait_send()

    out_shapes = [jax.ShapeDtypeStruct(_gathered_shape(s, f), w.dtype) for w, s, f in zip(shards, shapes, flavours)]
    return Side(shards, out_shapes, 6 * n, 0, start, wait)


def place_own_shard(full, shard, flavour, chip):
    L, a, b = shard.shape
    zero = jnp.zeros((), jnp.int32)
    if flavour == "col":
        return lax.dynamic_update_slice(full, shard, (zero, zero, chip * b))
    if flavour == "row":
        return lax.dynamic_update_slice(full, shard, (zero, chip * a, zero))
    return lax.dynamic_update_slice(full, shard[None], (chip, zero, zero, zero))


def swap_halves(gs, name):
    n = len(gs)

    def body(*refs):
        g_refs, out_refs = refs[:n], refs[n:2 * n]
        send_sems, recv_sems = refs[2 * n:]
        x, y, c = _me()
        cps = [pltpu.make_async_remote_copy(
            src_ref=g_refs[w].at[:, _half_rows(gs[w].shape[1], 1 - c), :], dst_ref=out_refs[w],
            send_sem=send_sems.at[w], recv_sem=recv_sems.at[w], device_id=(x, y, 1 - c), device_id_type=MESH)
            for w in range(n)]
        for cp in cps:
            cp.start()
        for cp in cps:
            cp.wait()

    any_spec = pl.BlockSpec(memory_space=pl.ANY)
    return pl.pallas_call(
        body, name=name,
        out_shape=[jax.ShapeDtypeStruct((g.shape[0], g.shape[1] // 2, g.shape[2]), g.dtype) for g in gs],
        in_specs=[any_spec] * n, out_specs=[any_spec] * n,
        scratch_shapes=[pltpu.SemaphoreType.DMA((n,)), pltpu.SemaphoreType.DMA((n,))],
    )(*gs)


def add_half(g, r1, c_idx, name):
    n, R, C = g.shape
    half = R // 2
    tr = _rows_block(half, C)
    nbh = half // tr

    def kern(c_ref, g_ref, r_ref, o_ref):
        o_ref[...] = g_ref[...] + r_ref[...]

    return pl.pallas_call(
        kern, name=name,
        grid_spec=pltpu.PrefetchScalarGridSpec(
            num_scalar_prefetch=1, grid=(n, nbh),
            in_specs=[pl.BlockSpec((1, tr, C), lambda d, r, c_ref: (d, c_ref[0] * nbh + r, 0)),
                      pl.BlockSpec((1, tr, C), lambda d, r, c_ref: (d, r, 0))],
            out_specs=pl.BlockSpec((1, tr, C), lambda d, r, c_ref: (d, r, 0))),
        out_shape=jax.ShapeDtypeStruct((n, half, C), F32),
        compiler_params=_cp(("parallel", "parallel")),
    )(c_idx, g, r1)


def exchange_side(ps):
    n = len(ps)

    def copies(p_refs, out_refs, send_sems, recv_sems, local_sems):
        x, y, c = _me()
        return [pltpu.make_async_remote_copy(
            src_ref=p_refs[w].at[2 * chip[0] + chip[1]], dst_ref=out_refs[w].at[j],
            send_sem=send_sems.at[3 * w + j], recv_sem=recv_sems.at[3 * w + j],
            device_id=(*chip, c), device_id_type=MESH)
            for w in range(n) for j, chip in enumerate(_other_chips(x, y))]

    def start(*refs):
        for cp in copies(*refs):
            cp.start()

    def wait(*refs):
        for cp in copies(*refs):
            cp.wait()

    return Side(ps, [jax.ShapeDtypeStruct((3,) + p.shape[1:], p.dtype) for p in ps], 3 * n, 0, start, wait)


def add_chips(p, r2, chip_idx, name):
    n, H, C = p.shape
    tr = _rows_block(H, C)

    def kern(s_ref, p_ref, r_ref, o_ref):
        o_ref[...] = ((p_ref[0] + r_ref[0]) + r_ref[1]) + r_ref[2]

    return pl.pallas_call(
        kern, name=name,
        grid_spec=pltpu.PrefetchScalarGridSpec(
            num_scalar_prefetch=1, grid=(H // tr,),
            in_specs=[pl.BlockSpec((1, tr, C), lambda r, s_ref: (s_ref[0], r, 0)),
                      pl.BlockSpec((3, tr, C), lambda r, s_ref: (0, r, 0))],
            out_specs=pl.BlockSpec((tr, C), lambda r, s_ref: (r, 0))),
        out_shape=jax.ShapeDtypeStruct((H, C), F32),
        compiler_params=_cp(("parallel",)),
    )(chip_idx, p, r2)


def swap_reduced(ss, name):
    n = len(ss)

    def body(*refs):
        s_refs, out_refs = refs[:n], refs[n:2 * n]
        send_sems, recv_sems = refs[2 * n:]
        x, y, c = _me()
        cps = [pltpu.make_async_remote_copy(src_ref=s_refs[w], dst_ref=out_refs[w], send_sem=send_sems.at[w],
                                            recv_sem=recv_sems.at[w], device_id=(x, y, 1 - c), device_id_type=MESH)
               for w in range(n)]
        for cp in cps:
            cp.start()
        for cp in cps:
            cp.wait()

    any_spec = pl.BlockSpec(memory_space=pl.ANY)
    return pl.pallas_call(
        body, name=name, out_shape=[jax.ShapeDtypeStruct(s.shape, s.dtype) for s in ss],
        in_specs=[any_spec] * n, out_specs=[any_spec] * n,
        scratch_shapes=[pltpu.SemaphoreType.DMA((n,)), pltpu.SemaphoreType.DMA((n,))],
    )(*ss)


BIG = (("gla_w_in", 2, (1024, GLA_IN // N_CHIPS), "lead"), ("gla_w_out", 2, (256, 1024), "row"),
       ("att_w_in", 2, (1024, 768), "col"), ("att_w_out", 2, (256, 1024), "row"),
       ("ff_w1", 4, (1024, 1024), "col"), ("ff_w2", 4, (1024, 1024), "row"))
FLAVOUR = {n: f for n, _, _, f in BIG}


def layer_weights(i):
    mixer = "gla" if i % 2 == 0 else "att"
    return (("in", mixer + "_w_in", i // 2), ("out", mixer + "_w_out", i // 2), ("w1", "ff_w1", i), ("w2", "ff_w2", i))


class Comm:
    def __init__(self, weights, core, chip):
        self.weights, self.core, self.chip = weights, core, chip
        self.c_idx = jnp.reshape(core, (1,)).astype(jnp.int32)
        self.chip_idx = jnp.reshape(chip, (1,)).astype(jnp.int32)
        self.reduced = {}

    def _shards(self, i):
        return [self.weights[n][l:l + 1].astype(BF16) for _, n, l in layer_weights(i)]

    def gather(self, i):
        return gather_side(self._shards(i), [FLAVOUR[n] for _, n, _ in layer_weights(i)])

    def full_weights(self, i, gathered):
        W = {}
        for (role, n, _), w, shard in zip(layer_weights(i), gathered, self._shards(i)):
            w = place_own_shard(w, shard, FLAVOUR[n], self.chip)
            if n == "gla_w_in":
                w = jnp.pad(w.transpose(1, 2, 0, 3).reshape(1, D_MODEL, GLA_IN), ((0, 0), (0, 0), (0, GLA_IN_PAD - GLA_IN)))
            W[role] = (w, 0)
        return W

    def first_weights(self):
        return self.full_weights(0, run_side(self.gather(0), "gather_w0"))

    def reduce_begin(self, tag, items):
        grads = [g for _, _, g in items]
        r1 = swap_halves(grads, f"rs_swap_{tag}")
        ps = [add_half(g, r, self.c_idx, f"rs_add2_{tag}_{w}") for w, (g, r) in enumerate(zip(grads, r1))]
        return tag, [(n, l) for n, l, _ in items], ps

    def exchange(self, pending):
        return exchange_side(pending[2])

    def reduce_mid(self, pending, landed):
        tag, keys, ps = pending
        for w, (key, p, r) in enumerate(zip(keys, ps, landed)):
            self.reduced[key] = add_chips(p, r, self.chip_idx, f"rs_add4_{tag}_{w}")

    def reduce_last(self, pending):
        self.reduce_mid(pending, run_side(self.exchange(pending), f"rs_xchg_{pending[0]}"))

    def reduce_end(self):
        keys = [(n, l) for n, L, _, _ in BIG for l in range(L)]
        mine = [self.reduced[k] for k in keys]
        theirs = swap_reduced(mine, "rs_join")
        low = self.core == 0
        full = {k: jnp.concatenate([jnp.where(low, m, t), jnp.where(low, t, m)], axis=0)
                for k, m, t in zip(keys, mine, theirs)}
        return {n: jnp.stack([full[(n, l)] for l in range(L)]) for n, L, _, _ in BIG}


def local_step(x, target, mods, comm, small):
    S, D = x.shape
    row = lambda v: v.reshape(1, -1)
    saved = []
    tiles = [att_bias_tiles(small["att_rel_bias"][j], f"att_tiles_{j}").reshape(ATT_HEADS, ATT_CLASSES, ATT_TQ, ATT_KW)
             for j in range(2)]
    wgk_p = [jnp.pad(small["gla_w_gk2"][j], ((0, 128 - GLA_RANK), (0, 0))).astype(BF16) for j in range(2)]

    u1 = modulate(x, row(mods[0, 1]), row(mods[0, 0]), "mod_first")
    Ws = [comm.first_weights()]
    for i in range(DEPTH):
        j = i // 2
        W = Ws[i]
        sh1, sc1, g1, sh2, sc2, g2 = (row(mods[i, k]) for k in range(6))
        nxt = min(i + 1, DEPTH - 1)
        side = comm.gather(i + 1) if i + 1 < DEPTH else None
        if i % 2 == 0:
            proj = mm_plain(u1, *W["in"], f"gla_in_{i}")
            (zmix, states), landed = gla_fwd(proj, wgk_p[j], row(small["gla_b_gk"][j]), row(small["gla_g_norm"][j]),
                                             f"gla_fwd_{i}", side)
        else:
            proj = mm_plain(u1, *W["in"], f"att_in_{i}", mode="bf16", bias=row(small["att_b_in"][j]))
            (zmix,), landed = attn_fwd(proj, tiles[j], f"att_fwd_{i}", side)
            states = None
        if i + 1 < DEPTH:
            Ws.append(comm.full_weights(i + 1, landed))
        y1, x_mid, u2 = mm_down_ln(zmix, *W["out"], x, 1.0 + g1, row(small["ln_g"][i, 0]), row(small["ln_b"][i, 0]),
                                   sc2, sh2, f"mix_out_{i}")
        act = mm_plain(u2, *W["w1"], f"ff_up_{i}", mode="mlp_up")
        y2, x_out, u_next = mm_down_ln(act, *W["w2"], x_mid, 1.0 + g2, row(small["ln_g"][i, 1]),
                                       row(small["ln_b"][i, 1]), row(mods[nxt, 1]), row(mods[nxt, 0]), f"ff_out_{i}")
        saved.append(dict(x_in=x, u1=u1, proj=proj, zmix=zmix, states=states, y1=y1, x_mid=x_mid, u2=u2,
                          act=act, y2=y2))
        x, u1 = x_out, u_next

    dx, sq = loss_head(x, target, "loss_head")

    g_small = dict(ln_g=[None] * DEPTH, ln_b=[None] * DEPTH, gla_w_gk2=[None] * 2, gla_b_gk=[None] * 2,
                   gla_g_norm=[None] * 2, att_b_in=[None] * 2, att_rel_bias=[None] * 2)
    dmods = [None] * DEPTH
    later = []

    for i in reversed(range(DEPTH)):
        j = i // 2
        sv = saved[i]
        W = Ws[i]
        sh1, sc1, g1, sh2, sc2, g2 = (row(mods[i, k]) for k in range(6))
        dz2, dy2, s_ln2 = ln_bwd(dx, sv["x_mid"], sv["y2"], 1.0 + g2, row(small["ln_g"][i, 1]), f"ln2_bwd_{i}")
        dh = mm_plain(dy2, *W["w2"], f"ff_dn_{i}", mode="mlp_dn", nt=True, h=sv["act"])
        g_w2 = mm_w(sv["act"], dy2, f"ff_w2g_{i}").reshape(N_CHIPS, D_FF // N_CHIPS, D)
        g_w1 = mm_w(sv["u2"], dh, f"ff_w1g_{i}", chips_out=True)
        dx_mid, s_m2 = mm_down_comb(dh, *W["w1"], dz2, sv["x_mid"], 1.0 + sc2, f"ff_dx_{i}")
        dz1, dy1, s_ln1 = ln_bwd(dx_mid, sv["x_in"], sv["y1"], 1.0 + g1, row(small["ln_g"][i, 0]), f"ln1_bwd_{i}")
        pending = comm.reduce_begin(i, [("ff_w1", i, g_w1), ("ff_w2", i, g_w2)] + later)
        side = comm.exchange(pending)
        mixer = "gla" if i % 2 == 0 else "att"
        if i % 2 == 0:
            g_out = mm_w(sv["zmix"], dy1, f"gla_wog_{i}").reshape(N_CHIPS, D // N_CHIPS, D)
            dzg = mm_plain(dy1, *W["out"], f"gla_dz_{i}", nt=True)
            (dproj, dwgk, dbgk, dgn), landed = gla_bwd(sv["proj"], sv["states"], dzg, wgk_p[j],
                                                       row(small["gla_b_gk"][j]), row(small["gla_g_norm"][j]),
                                                       f"gla_bwd_{i}", side)
            g_small["gla_w_gk2"][j] = dwgk[:GLA_RANK]
            g_small["gla_b_gk"][j] = dbgk[0]
            g_small["gla_g_norm"][j] = dgn[0].reshape(GLA_HEADS, GLA_DV_HEAD)
            gwi = mm_w(sv["u1"], dproj, f"gla_wig_{i}")[:, :GLA_IN]
            g_in = gwi.reshape(D, N_CHIPS, GLA_IN // N_CHIPS).transpose(1, 0, 2)
            dx, s_m1 = mm_down_comb(dproj, *W["in"], dz1, sv["x_in"], 1.0 + sc1, f"mix_dx_{i}")
        else:
            g_out = mm_w(sv["zmix"], dy1, f"att_wog_{i}").reshape(N_CHIPS, D // N_CHIPS, D)
            do = mm_plain(dy1, *W["out"], f"att_do_{i}", mode="bf16", nt=True)
            (dqkv, dbt), landed = attn_bwd(sv["proj"], tiles[j], do, f"att_bwd_{i}", side)
            g_small["att_rel_bias"][j] = att_bias_grad(dbt.reshape(ATT_HEADS * ATT_CLASSES, ATT_TQ, ATT_KW),
                                                       f"att_bias_{i}")
            g_small["att_b_in"][j] = colsum3(dqkv, f"att_bin_{i}").reshape(3 * D)
            g_in = mm_w_chips3(sv["u1"], dqkv, f"att_wig_{i}")
            dx, s_m1 = mm_down_comb(dqkv, *W["in"], dz1, sv["x_in"], 1.0 + sc1, f"mix_dx_{i}", parts=3)
        comm.reduce_mid(pending, landed)
        later = [(mixer + "_w_in", j, g_in), (mixer + "_w_out", j, g_out)]
        g_small["ln_g"][i] = jnp.stack([s_ln1[0], s_ln2[0]])
        g_small["ln_b"][i] = jnp.stack([s_ln1[1], s_ln2[1]])
        dmods[i] = jnp.stack([s_m1[1], s_m1[0], s_ln1[2], s_m2[1], s_m2[0], s_ln2[2]])
    comm.reduce_last(comm.reduce_begin("last", later))

    g_small = {n: jnp.stack(v) for n, v in g_small.items()}
    return sq, dx, jnp.stack(dmods), g_small


SMALL_SHARDED = (("ln_g", (4, 2, 256)), ("ln_b", (4, 2, 256)), ("gla_g_norm", (2, 4, 64)),
                 ("gla_w_gk2", (2, 16, 128)), ("att_b_in", (2, 768)))
SMALL_FULL = dict(ln_g=(4, 2, 1024), ln_b=(4, 2, 1024), gla_g_norm=(2, 4, 256), gla_w_gk2=(2, 16, 512),
                  att_b_in=(2, 3072), gla_b_gk=(2, 512), att_rel_bias=(2, 16, 257))
SMALL_GRAD_ORDER = ("ln_g", "ln_b", "gla_g_norm", "gla_w_gk2", "att_b_in", "gla_b_gk", "att_rel_bias")


def _pack_small(arrs, rows_total):
    parts = []
    for a in arrs:
        flat = a.reshape(-1)
        pad = (-flat.shape[0]) % PACK_W
        parts.append(jnp.pad(flat, (0, pad)).reshape(-1, PACK_W))
    buf = jnp.concatenate(parts, axis=0)
    return jnp.pad(buf, ((0, rows_total - buf.shape[0]), (0, 0)))


def _unpack_small(buf, shapes):
    out, r = [], 0
    for shp in shapes:
        n = 1
        for s in shp:
            n *= s
        nr = (n + PACK_W - 1) // PACK_W
        out.append(buf[..., r:r + nr, :].reshape(buf.shape[:-2] + (nr * PACK_W,))[..., :n].reshape(buf.shape[:-2] + shp))
        r += nr
    return out


def _unshard_last(g4):
    nd = g4.ndim
    perm = tuple(range(1, nd - 1)) + (0, nd - 1)
    t = g4.transpose(perm)
    return t.reshape(t.shape[:-2] + (-1,))


def _shard_last(full, s):
    n = full.shape[-1] // N_CHIPS
    return lax.dynamic_slice_in_dim(full, s * n, n, axis=full.ndim - 1)


WEIGHT_NAMES = ("w_ada", "b_ada", "ln_g", "ln_b", "gla_w_in", "gla_w_gk2", "gla_b_gk", "gla_g_norm", "gla_w_out",
                "att_w_in", "att_b_in", "att_rel_bias", "att_w_out", "ff_w1", "ff_w2")


def kernel(x, c, w_ada, b_ada, ln_g, ln_b, gla_w_in, gla_w_gk2, gla_b_gk, gla_g_norm, gla_w_out, att_w_in, att_b_in, att_rel_bias, att_w_out, ff_w1, ff_w2, loss_target, m_w_ada, m_b_ada, m_ln_g, m_ln_b, m_gla_w_in, m_gla_w_gk2, m_gla_b_gk, m_gla_g_norm, m_gla_w_out, m_att_w_in, m_att_b_in, m_att_rel_bias, m_att_w_out, m_ff_w1, m_ff_w2, v_w_ada, v_b_ada, v_ln_g, v_ln_b, v_gla_w_in, v_gla_w_gk2, v_gla_b_gk, v_gla_g_norm, v_gla_w_out, v_att_w_in, v_att_b_in, v_att_rel_bias, v_att_w_out, v_ff_w1, v_ff_w2):
    weights = dict(w_ada=w_ada, b_ada=b_ada, ln_g=ln_g, ln_b=ln_b, gla_w_in=gla_w_in, gla_w_gk2=gla_w_gk2,
                   gla_b_gk=gla_b_gk, gla_g_norm=gla_g_norm, gla_w_out=gla_w_out, att_w_in=att_w_in,
                   att_b_in=att_b_in, att_rel_bias=att_rel_bias, att_w_out=att_w_out, ff_w1=ff_w1, ff_w2=ff_w2)
    mom1 = dict(w_ada=m_w_ada, b_ada=m_b_ada, ln_g=m_ln_g, ln_b=m_ln_b, gla_w_in=m_gla_w_in, gla_w_gk2=m_gla_w_gk2,
                gla_b_gk=m_gla_b_gk, gla_g_norm=m_gla_g_norm, gla_w_out=m_gla_w_out, att_w_in=m_att_w_in,
                att_b_in=m_att_b_in, att_rel_bias=m_att_rel_bias, att_w_out=m_att_w_out, ff_w1=m_ff_w1, ff_w2=m_ff_w2)
    mom2 = dict(w_ada=v_w_ada, b_ada=v_b_ada, ln_g=v_ln_g, ln_b=v_ln_b, gla_w_in=v_gla_w_in, gla_w_gk2=v_gla_w_gk2,
                gla_b_gk=v_gla_b_gk, gla_g_norm=v_gla_g_norm, gla_w_out=v_gla_w_out, att_w_in=v_att_w_in,
                att_b_in=v_att_b_in, att_rel_bias=v_att_rel_bias, att_w_out=v_att_w_out, ff_w1=v_ff_w1, ff_w2=v_ff_w2)

    ax, ay, ac = lax.axis_index("x"), lax.axis_index("y"), lax.axis_index("c")
    chip = 2 * ax + ay
    dev = 2 * chip + ac
    S = x.shape[1]
    x2 = x.reshape(S, D_MODEL)
    t2 = loss_target.reshape(S, D_MODEL)

    comm = Comm(weights, ac, chip)

    small_rows = 16
    spack = _pack_small([c] + [weights[n] for n, _ in SMALL_SHARDED], small_rows)
    sg = all_gather8(spack, "gather_small").reshape(N_DEV, small_rows, PACK_W)
    parts = _unpack_small(sg, [(1, D_MODEL)] + [shp for _, shp in SMALL_SHARDED])
    c_all = parts[0].reshape(N_DEV, D_MODEL)
    small = {n: _unshard_last(p[0::2]) for (n, _), p in zip(SMALL_SHARDED, parts[1:])}
    small["gla_b_gk"] = gla_b_gk
    small["att_rel_bias"] = att_rel_bias

    c_act = silu_rows(jnp.pad(c_all, ((0, 128 - N_DEV), (0, 0))), "silu_c")
    wa = w_ada.astype(BF16).transpose(1, 0, 2).reshape(1, D_MODEL, DEPTH * 6 * D_MODEL // N_CHIPS)
    mods_part = mm_plain(c_act, wa, 0, "ada_fwd", tm=128)[:N_DEV]
    mg = all_gather8(mods_part, "gather_mods").reshape(N_CHIPS, 2, N_DEV, DEPTH, 6 * D_MODEL // N_CHIPS)
    mods_mine = lax.dynamic_index_in_dim(mg[:, 0], dev, axis=1, keepdims=False)
    mods = mods_mine.transpose(1, 0, 2).reshape(DEPTH, 6 * D_MODEL) + b_ada
    mods = mods.reshape(DEPTH, 6, D_MODEL)

    sq, grad_x, dmods, g_small = local_step(x2, t2, mods, comm, small)
    loss = lax.psum(0.5 * sq[0, 0] / D_MODEL, ("x", "y", "c"))

    g_shard = comm.reduce_end()

    dm_flat = dmods.reshape(DEPTH, 6 * D_MODEL)
    g_rows = 80
    gpack = _pack_small([dm_flat] + [g_small[n] for n in SMALL_GRAD_ORDER], g_rows)
    gg = all_gather8(gpack, "gather_small_grads").reshape(N_DEV, g_rows, PACK_W)
    gsum = sum_over_devices(gg, "sum_small_grads")
    sums = _unpack_small(gsum, [(DEPTH, 6 * D_MODEL)] + [SMALL_FULL[n] for n in SMALL_GRAD_ORDER])
    grads = dict(b_ada=sums[0])
    for n, full_g in zip(SMALL_GRAD_ORDER, sums[1:]):
        grads[n] = full_g if n in ("gla_b_gk", "att_rel_bias") else _shard_last(full_g, chip)
    dm_all = _unpack_small(gg, [(DEPTH, 6 * D_MODEL)])[0]
    dm_cols = _shard_last(dm_all, chip).reshape(N_DEV, DEPTH * 6 * D_MODEL // N_CHIPS)
    dm_cols = jnp.pad(dm_cols, ((0, 128 - N_DEV), (0, 0))).astype(BF16)
    gwa = mm_w(c_act, dm_cols, "ada_bwd", ts=128)
    grads["w_ada"] = gwa.reshape(D_MODEL, DEPTH, 6 * D_MODEL // N_CHIPS).transpose(1, 0, 2)
    grads.update(g_shard)

    deltas, new_m, new_v = {}, {}, {}
    for n in WEIGHT_NAMES:
        deltas[n], new_m[n], new_v[n] = adamw(weights[n], grads[n], mom1[n], mom2[n], "adamw_" + n)

    return (loss, grad_x.reshape(1, S, D_MODEL), *[grads[n] for n in WEIGHT_NAMES], *[deltas[n] for n in WEIGHT_NAMES],
            *[new_m[n] for n in WEIGHT_NAMES], *[new_v[n] for n in WEIGHT_NAMES])
```

```python
import functools

import jax
import jax.numpy as jnp
from jax import lax
from jax.experimental import pallas as pl
from jax.experimental.pallas import tpu as pltpu

F32 = jnp.float32
BF16 = jnp.bfloat16
HIGHEST = lax.Precision.HIGHEST
MESH = pl.DeviceIdType.MESH

D_MODEL = 1024
DEPTH = 4
CHUNK = 64
GLA_HEADS = 4
GLA_DK = 512
GLA_DV = 1024
GLA_DK_HEAD = 128
GLA_DV_HEAD = 256
GLA_RANK = 16
GLA_IN = 3088
GLA_IN_PAD = 3200
GLA_LR_OFF = 3072
ATT_HEADS = 16
ATT_HD = 64
LEFT_CHUNKS = 8
MAX_REL = 128
N_REL = 257
D_FF = 4096
ALPHA = (2.0 * DEPTH) ** 0.25
LN_EPS = 1e-5
RMS_EPS = 1e-6
NEG_INF = -1e30
GLA_SCALE = GLA_DK_HEAD ** -0.5
ATT_SCALE = ATT_HD ** -0.5
ADAM_LR = 0.001
ADAM_B1 = 0.9
ADAM_B2 = 0.999
ADAM_EPS = 1e-08
ADAM_WD = 0.01
ADAM_STEP = 10

ATT_TQ = 256
ATT_KW = 768
GLA_TB = 256
VMEM_LIMIT = 56 * 1024 * 1024
N_CHIPS = 4
N_DEV = 8
PACK_W = 1024


def _dot(a, b):
    return jnp.dot(a, b, preferred_element_type=F32)


def _dot_nt(a, b):
    return lax.dot_general(a, b, (((1,), (1,)), ((), ())), preferred_element_type=F32)


def _dot_tn(a, b):
    return lax.dot_general(a, b, (((0,), (0,)), ((), ())), preferred_element_type=F32)


def _cp(sem, vmem=VMEM_LIMIT):
    return pltpu.CompilerParams(dimension_semantics=sem, vmem_limit_bytes=vmem)


def _row_spec(n):
    return pl.BlockSpec((1, n), lambda *_: (0, 0))


def _sigmoid(x):
    return 1.0 / (1.0 + jnp.exp(-x))


def _log_sigmoid(x):
    return jnp.minimum(x, 0.0) - jnp.log1p(jnp.exp(-jnp.abs(x)))


class Side:
    def __init__(self, ins, out_shapes, n_sems, n_local, start, wait):
        self.ins, self.out_shapes, self.n_sems, self.n_local = list(ins), list(out_shapes), n_sems, n_local
        self.start, self.wait = start, wait

    def sem_shapes(self):
        return [pltpu.SemaphoreType.DMA((self.n_sems,)), pltpu.SemaphoreType.DMA((self.n_sems,)),
                pltpu.SemaphoreType.DMA((max(self.n_local, 1),))]


def run_side(side, name):
    n_in = len(side.ins)
    n_out = len(side.out_shapes)

    def body(*refs):
        ins, outs, sems = refs[:n_in], refs[n_in:n_in + n_out], refs[n_in + n_out:]
        side.start(ins, outs, *sems)
        side.wait(ins, outs, *sems)

    any_spec = pl.BlockSpec(memory_space=pl.ANY)
    return pl.pallas_call(body, name=name, out_shape=side.out_shapes, in_specs=[any_spec] * n_in,
                          out_specs=[any_spec] * n_out, scratch_shapes=side.sem_shapes())(*side.ins)


def hosted_call(main, side, *, name, grid, in_specs, out_specs, out_shape, scratch_shapes, dims, args):
    if side is None:
        outs = pl.pallas_call(main, name=name, grid=grid, in_specs=in_specs, out_specs=out_specs,
                              out_shape=out_shape, scratch_shapes=scratch_shapes, compiler_params=_cp(dims))(*args)
        return list(outs), []
    n_mi, n_mo, n_ms = len(in_specs), len(out_specs), len(scratch_shapes)
    n_si, n_so = len(side.ins), len(side.out_shapes)

    def kern(*refs):
        mi, si = refs[:n_mi], refs[n_mi:n_mi + n_si]
        o0 = n_mi + n_si
        mo, so = refs[o0:o0 + n_mo], refs[o0 + n_mo:o0 + n_mo + n_so]
        s0 = o0 + n_mo + n_so
        ms, sems = refs[s0:s0 + n_ms], refs[s0 + n_ms:]
        ids = [pl.program_id(d) for d in range(len(grid))]
        first = functools.reduce(jnp.logical_and, [i == 0 for i in ids])
        last = functools.reduce(jnp.logical_and, [i == g - 1 for i, g in zip(ids, grid)])

        @pl.when(first)
        def _():
            side.start(si, so, *sems)
        main(*mi, *mo, *ms)

        @pl.when(last)
        def _():
            side.wait(si, so, *sems)

    any_spec = pl.BlockSpec(memory_space=pl.ANY)
    outs = pl.pallas_call(
        kern, name=name, grid=grid, in_specs=list(in_specs) + [any_spec] * n_si,
        out_specs=list(out_specs) + [any_spec] * n_so, out_shape=list(out_shape) + side.out_shapes,
        scratch_shapes=list(scratch_shapes) + side.sem_shapes(),
        compiler_params=_cp(("arbitrary",) * len(grid)))(*args, *side.ins)
    return list(outs[:n_mo]), list(outs[n_mo:])


def modulate(x, sc, sh, name):
    S, D = x.shape
    tm = min(512, S)

    def kern(x_ref, sc_ref, sh_ref, u_ref):
        u_ref[...] = (x_ref[...] * (1.0 + sc_ref[...]) + sh_ref[...]).astype(BF16)

    return pl.pallas_call(
        kern, name=name, grid=(S // tm,),
        in_specs=[pl.BlockSpec((tm, D), lambda i: (i, 0)), _row_spec(D), _row_spec(D)],
        out_specs=pl.BlockSpec((tm, D), lambda i: (i, 0)),
        out_shape=jax.ShapeDtypeStruct((S, D), BF16),
        compiler_params=_cp(("parallel",)),
    )(x, sc, sh)


def loss_head(x, t, name):
    S, D = x.shape
    tm = min(512, S)

    def kern(x_ref, t_ref, dx_ref, l_ref):
        @pl.when(pl.program_id(0) == 0)
        def _():
            l_ref[...] = jnp.zeros_like(l_ref)
        e = x_ref[...] - t_ref[...]
        dx_ref[...] = e * (1.0 / D)
        l_ref[...] += jnp.sum(e * e)

    return pl.pallas_call(
        kern, name=name, grid=(S // tm,),
        in_specs=[pl.BlockSpec((tm, D), lambda i: (i, 0)), pl.BlockSpec((tm, D), lambda i: (i, 0))],
        out_specs=[pl.BlockSpec((tm, D), lambda i: (i, 0)), pl.BlockSpec((8, 128), lambda i: (0, 0))],
        out_shape=[jax.ShapeDtypeStruct((S, D), F32), jax.ShapeDtypeStruct((8, 128), F32)],
        compiler_params=_cp(("arbitrary",)),
    )(x, t)


def silu_rows(c_all, name):
    def kern(c_ref, o_ref):
        c = c_ref[...]
        o_ref[...] = (c * _sigmoid(c)).astype(BF16)

    return pl.pallas_call(kern, name=name, out_shape=jax.ShapeDtypeStruct(c_all.shape, BF16))(c_all)


def sum_over_devices(g, name):
    n, R, C = g.shape

    def kern(g_ref, o_ref):
        acc = g_ref[0]
        for d in range(1, n):
            acc = acc + g_ref[d]
        o_ref[...] = acc

    return pl.pallas_call(kern, name=name, out_shape=jax.ShapeDtypeStruct((R, C), F32))(g)


def _rows_block(R, C, budget=1 << 20):
    if R * C * 4 <= budget or R % 8:
        return R
    tr = max(8, (budget // (C * 4)) // 8 * 8)
    while R % tr:
        tr -= 8
    return tr


def adamw(w, g, m, v, name):
    shape = w.shape
    C = shape[-1]
    R = w.size // C
    w2, g2, m2, v2 = (t.reshape(R, C) for t in (w, g, m, v))
    tr = _rows_block(R, C)
    c1 = 1.0 - ADAM_B1 ** ADAM_STEP
    c2 = 1.0 - ADAM_B2 ** ADAM_STEP

    def kern(w_ref, g_ref, m_ref, v_ref, d_ref, nm_ref, nv_ref):
        gg = g_ref[...]
        nm = ADAM_B1 * m_ref[...] + (1.0 - ADAM_B1) * gg
        nv = ADAM_B2 * v_ref[...] + (1.0 - ADAM_B2) * (gg * gg)
        m_hat = nm / c1
        v_hat = nv / c2
        d_ref[...] = -ADAM_LR * (m_hat / (jnp.sqrt(v_hat) + ADAM_EPS) + ADAM_WD * w_ref[...])
        nm_ref[...] = nm
        nv_ref[...] = nv

    spec = pl.BlockSpec((tr, C), lambda i: (i, 0))
    outs = pl.pallas_call(
        kern, name=name, grid=(R // tr,),
        in_specs=[spec] * 4, out_specs=[spec] * 3,
        out_shape=[jax.ShapeDtypeStruct((R, C), F32)] * 3,
        compiler_params=_cp(("parallel",)),
    )(w2, g2, m2, v2)
    return tuple(o.reshape(shape) for o in outs)


def _tn_for(N):
    for tn in (1024, 768, 640, 512, 384, 256, 128):
        if N % tn == 0:
            return tn
    return N


def mm_plain(a, b3, layer, name, *, mode="f32", nt=False, bias=None, h=None, tm=512):
    M, K = a.shape
    N = b3.shape[1] if nt else b3.shape[2]
    tm = min(tm, M)
    tn = _tn_for(N)
    a_spec = pl.BlockSpec((tm, K), lambda j, i: (i, 0))
    if nt:
        b_spec = pl.BlockSpec((None, tn, K), lambda j, i: (layer, j, 0))
        dot = _dot_nt
    else:
        b_spec = pl.BlockSpec((None, K, tn), lambda j, i: (layer, 0, j))
        dot = _dot
    o_spec = pl.BlockSpec((tm, tn), lambda j, i: (i, j))
    ins, in_specs = [a, b3], [a_spec, b_spec]

    if mode in ("f32", "bf16"):
        odt = F32 if mode == "f32" else BF16
        if bias is not None:
            ins.append(bias)
            in_specs.append(pl.BlockSpec((1, tn), lambda j, i: (0, j)))

            def kern(a_ref, b_ref, bias_ref, o_ref):
                o_ref[...] = (dot(a_ref[...], b_ref[...]) + bias_ref[...]).astype(odt)
        else:
            def kern(a_ref, b_ref, o_ref):
                o_ref[...] = dot(a_ref[...], b_ref[...]).astype(odt)
        out_specs, out_shape = o_spec, jax.ShapeDtypeStruct((M, N), odt)
    elif mode == "mlp_up":
        def kern(a_ref, b_ref, act_ref):
            r = jnp.maximum(dot(a_ref[...], b_ref[...]), 0.0)
            act_ref[...] = (r * r).astype(BF16)
        out_specs, out_shape = o_spec, jax.ShapeDtypeStruct((M, N), BF16)
    elif mode == "mlp_dn":
        ins.append(h)
        in_specs.append(o_spec)

        def kern(a_ref, b_ref, h_ref, o_ref):
            acc = dot(a_ref[...], b_ref[...])
            o_ref[...] = (acc * (2.0 * jnp.sqrt(h_ref[...].astype(F32)))).astype(BF16)
        out_specs, out_shape = o_spec, jax.ShapeDtypeStruct((M, N), BF16)
    else:
        raise ValueError(mode)

    return pl.pallas_call(
        kern, name=name, grid=(N // tn, M // tm), in_specs=in_specs, out_specs=out_specs,
        out_shape=out_shape, compiler_params=_cp(("parallel", "parallel")),
    )(*ins)


def mm_down_ln(a, b3, layer, x_in, gate1p, ln_g, ln_b, sc_next, sh_next, name, *, tm=256):
    M, K = a.shape
    D = b3.shape[2]
    tm = min(tm, M)

    def kern(a_ref, b_ref, x_ref, gp_ref, lg_ref, lb_ref, sc_ref, sh_ref, y_ref, xo_ref, u_ref):
        y = _dot(a_ref[...], b_ref[...])
        y_ref[...] = y
        z = ALPHA * x_ref[...] + gp_ref[...] * y
        mu = jnp.mean(z, axis=-1, keepdims=True)
        zc = z - mu
        var = jnp.mean(zc * zc, axis=-1, keepdims=True)
        xo = (zc * lax.rsqrt(var + LN_EPS)) * lg_ref[...] + lb_ref[...]
        xo_ref[...] = xo
        u_ref[...] = (xo * (1.0 + sc_ref[...]) + sh_ref[...]).astype(BF16)

    tile = pl.BlockSpec((tm, D), lambda i: (i, 0))
    return pl.pallas_call(
        kern, name=name, grid=(M // tm,),
        in_specs=[pl.BlockSpec((tm, K), lambda i: (i, 0)), pl.BlockSpec((None, K, D), lambda i: (layer, 0, 0)), tile]
        + [_row_spec(D)] * 5,
        out_specs=[tile, tile, tile],
        out_shape=[jax.ShapeDtypeStruct((M, D), F32)] * 2 + [jax.ShapeDtypeStruct((M, D), BF16)],
        compiler_params=_cp(("parallel",)),
    )(a, b3, x_in, gate1p, ln_g, ln_b, sc_next, sh_next)


def mm_down_comb(a, b3, layer, dz, x_in, sc1p, name, *, parts=1, tm=256):
    D, K = b3.shape[1], b3.shape[2]
    M = a.shape[-2]
    kp = K // parts
    tm = min(tm, M)

    def kern(*refs):
        a_refs = refs[:parts]
        b_ref, dz_ref, x_ref, sp_ref, dx_ref, s_ref = refs[parts:]

        @pl.when(pl.program_id(0) == 0)
        def _():
            s_ref[...] = jnp.zeros_like(s_ref)
        if parts == 1:
            du = _dot_nt(a_refs[0][...], b_ref[...])
        else:
            du = _dot_nt(a_refs[0][...], b_ref[:, 0:kp])
            for p in range(1, parts):
                du = du + _dot_nt(a_refs[p][...], b_ref[:, p * kp:(p + 1) * kp])
        dx_ref[...] = ALPHA * dz_ref[...] + du * sp_ref[...]
        s_ref[0:1, :] += jnp.sum(du * x_ref[...], axis=0, keepdims=True)
        s_ref[1:2, :] += jnp.sum(du, axis=0, keepdims=True)

    tile = pl.BlockSpec((tm, D), lambda i: (i, 0))
    if parts == 1:
        a_ins, a_specs = [a], [pl.BlockSpec((tm, K), lambda i: (i, 0))]
    else:
        a_ins = [a] * parts
        a_specs = [pl.BlockSpec((None, tm, kp), functools.partial(lambda i, p: (p, i, 0), p=p)) for p in range(parts)]
    return pl.pallas_call(
        kern, name=name, grid=(M // tm,),
        in_specs=a_specs + [pl.BlockSpec((None, D, K), lambda i: (layer, 0, 0)), tile, tile, _row_spec(D)],
        out_specs=[tile, pl.BlockSpec((8, D), lambda i: (0, 0))],
        out_shape=[jax.ShapeDtypeStruct((M, D), F32), jax.ShapeDtypeStruct((8, D), F32)],
        compiler_params=_cp(("arbitrary",)),
    )(*a_ins, b3, dz, x_in, sc1p)


def mm_w(a, b, name, *, ts=1024, tk=512, chips_out=False, b_parts=1, tn=None):
    S, K = a.shape
    npart = b.shape[-1]
    N = npart * b_parts
    ts = min(ts, S)
    tk = min(tk, K)
    n_chip = N // N_CHIPS
    if tn is None:
        tn = _tn_for(n_chip if chips_out else npart)
    assert npart % tn == 0 and (not chips_out or n_chip % tn == 0)

    def kern(a_ref, b_ref, o_ref):
        @pl.when(pl.program_id(2) == 0)
        def _():
            o_ref[...] = jnp.zeros_like(o_ref)
        o_ref[...] += _dot_tn(a_ref[...], b_ref[...])

    if b_parts == 1:
        b_spec = pl.BlockSpec((ts, tn), lambda k, n, s: (s, n))
    else:
        per = npart // tn
        b_spec = pl.BlockSpec((None, ts, tn), lambda k, n, s: (n // per, s, n % per))
    if chips_out:
        per_chip = n_chip // tn
        o_spec = pl.BlockSpec((None, tk, tn), lambda k, n, s: (n // per_chip, k, n % per_chip))
        out_shape = jax.ShapeDtypeStruct((N_CHIPS, K, n_chip), F32)
    else:
        o_spec = pl.BlockSpec((tk, tn), lambda k, n, s: (k, n))
        out_shape = jax.ShapeDtypeStruct((K, N), F32)
    return pl.pallas_call(
        kern, name=name, grid=(K // tk, N // tn, S // ts),
        in_specs=[pl.BlockSpec((ts, tk), lambda k, n, s: (s, k)), b_spec],
        out_specs=o_spec, out_shape=out_shape,
        compiler_params=_cp(("parallel", "parallel", "arbitrary")),
    )(a, b)


def mm_w_chips3(a, b3, name, *, ts=512):
    S, K = a.shape
    P = b3.shape[2]
    n_chip = 3 * P // N_CHIPS
    ts = min(ts, S)
    pieces = []
    for chip in range(N_CHIPS):
        lo, hi = chip * n_chip, (chip + 1) * n_chip
        while lo < hi:
            part = lo // P
            w = min(hi, (part + 1) * P) - lo
            pieces.append((chip, lo - chip * n_chip, part, lo - part * P, w))
            lo += w

    def kern(a_ref, b_ref, o_ref):
        @pl.when(pl.program_id(0) == 0)
        def _():
            o_ref[...] = jnp.zeros_like(o_ref)
        at = a_ref[...].T
        for chip, oc, part, pc, w in pieces:
            o_ref[chip, :, oc:oc + w] += _dot(at, b_ref[part, :, pc:pc + w])

    return pl.pallas_call(
        kern, name=name, grid=(S // ts,),
        in_specs=[pl.BlockSpec((ts, K), lambda s: (s, 0)), pl.BlockSpec((3, ts, P), lambda s: (0, s, 0))],
        out_specs=pl.BlockSpec((N_CHIPS, K, n_chip), lambda s: (0, 0, 0)),
        out_shape=jax.ShapeDtypeStruct((N_CHIPS, K, n_chip), F32),
        compiler_params=_cp(("arbitrary",)),
    )(a, b3)


def mm_f32(a, b, name):
    def kern(a_ref, b_ref, o_ref):
        o_ref[...] = jnp.dot(a_ref[...], b_ref[...], precision=HIGHEST, preferred_element_type=F32)

    return pl.pallas_call(kern, name=name, out_shape=jax.ShapeDtypeStruct((a.shape[0], b.shape[1]), F32),
                          compiler_params=pltpu.CompilerParams(vmem_limit_bytes=VMEM_LIMIT))(a, b)


def ln_bwd(dxo, x_in, y, gate1p, ln_g, name, *, tm=256):
    S, D = dxo.shape
    tm = min(tm, S)

    def kern(dxo_ref, x_ref, y_ref, gp_ref, lg_ref, dz_ref, dy_ref, s_ref):
        @pl.when(pl.program_id(0) == 0)
        def _():
            s_ref[...] = jnp.zeros_like(s_ref)
        dxo_t = dxo_ref[...]
        yv = y_ref[...]
        z = ALPHA * x_ref[...] + gp_ref[...] * yv
        mu = jnp.mean(z, axis=-1, keepdims=True)
        zc = z - mu
        var = jnp.mean(zc * zc, axis=-1, keepdims=True)
        rstd = lax.rsqrt(var + LN_EPS)
        xhat = zc * rstd
        dxh = dxo_t * lg_ref[...]
        dz = rstd * (dxh - jnp.mean(dxh, axis=-1, keepdims=True)
                     - xhat * jnp.mean(dxh * xhat, axis=-1, keepdims=True))
        dz_ref[...] = dz
        dy_ref[...] = (gp_ref[...] * dz).astype(BF16)
        s_ref[0:1, :] += jnp.sum(dxo_t * xhat, axis=0, keepdims=True)
        s_ref[1:2, :] += jnp.sum(dxo_t, axis=0, keepdims=True)
        s_ref[2:3, :] += jnp.sum(dz * yv, axis=0, keepdims=True)

    tile = pl.BlockSpec((tm, D), lambda i: (i, 0))
    return pl.pallas_call(
        kern, name=name, grid=(S // tm,),
        in_specs=[tile, tile, tile, _row_spec(D), _row_spec(D)],
        out_specs=[tile, tile, pl.BlockSpec((8, D), lambda i: (0, 0))],
        out_shape=[jax.ShapeDtypeStruct((S, D), F32), jax.ShapeDtypeStruct((S, D), BF16),
                   jax.ShapeDtypeStruct((8, D), F32)],
        compiler_params=_cp(("arbitrary",)),
    )(dxo, x_in, y, gate1p, ln_g)


def _tri64():
    r = lax.broadcasted_iota(jnp.int32, (CHUNK, CHUNK), 0)
    c = lax.broadcasted_iota(jnp.int32, (CHUNK, CHUNK), 1)
    return r >= c


def _gla_chunk_common(proj_ref, rows, b, h):
    kc = slice(h * GLA_DK_HEAD, (h + 1) * GLA_DK_HEAD)
    bh = b[:, kc]
    ep = jnp.exp(bh)
    en = jnp.exp(-bh)
    bl = bh[CHUNK - 1:CHUNK, :]
    ee = jnp.exp(bl - bh)
    dec = jnp.exp(bl)
    q = proj_ref[rows, h * GLA_DK_HEAD:(h + 1) * GLA_DK_HEAD] * GLA_SCALE
    k = proj_ref[rows, GLA_DK + h * GLA_DK_HEAD:GLA_DK + (h + 1) * GLA_DK_HEAD]
    v = proj_ref[rows, 2 * GLA_DK + h * GLA_DV_HEAD:2 * GLA_DK + (h + 1) * GLA_DV_HEAD]
    g = proj_ref[rows, 2 * GLA_DK + GLA_DV + h * GLA_DV_HEAD:2 * GLA_DK + GLA_DV + (h + 1) * GLA_DV_HEAD]
    return ep, en, ee, dec, q, k, v, g


def gla_fwd(proj, wgk_p, bgk, gnorm, name, side=None):
    S = proj.shape[0]
    TB = min(GLA_TB, S)
    ncb = TB // CHUNK

    def kern(proj_ref, wgk_ref, bgk_ref, gn_ref, zg_ref, st_ref, state_scr, la_scr):
        @pl.when(pl.program_id(0) == 0)
        def _():
            state_scr[...] = jnp.zeros_like(state_scr)
        lr = proj_ref[:, GLA_LR_OFF:GLA_IN_PAD].astype(BF16)
        gk = _dot(lr, wgk_ref[...]) + bgk_ref[...]
        la_scr[...] = _log_sigmoid(gk) * (1.0 / 16.0)
        lower = _tri64()
        tri = lower.astype(F32)

        def chunk(c, carry):
            rows = pl.ds(pl.multiple_of(c * CHUNK, CHUNK), CHUNK)
            b = jnp.dot(tri, la_scr[rows, :], precision=HIGHEST, preferred_element_type=F32)
            H = range(GLA_HEADS)
            cm = [_gla_chunk_common(proj_ref, rows, b, h) for h in H]
            qf = [(m[4] * m[0]).astype(BF16) for m in cm]
            kn = [(m[5] * m[1]).astype(BF16) for m in cm]
            qn = [(m[4] * m[1]).astype(BF16) for m in cm]
            kp = [(m[5] * m[0]).astype(BF16) for m in cm]
            ke = [(m[5] * m[2]).astype(BF16) for m in cm]
            vb = [m[6].astype(BF16) for m in cm]
            st = [state_scr[h] for h in H]
            a_f = [_dot_nt(qf[h], kn[h]) for h in H]
            a_b = [_dot_nt(qn[h], kp[h]) for h in H]
            o_st = [_dot_nt(qf[h], st[h].astype(BF16)) for h in H]
            upd = [_dot_tn(vb[h], ke[h]) for h in H]
            amat = [jnp.where(lower, a_f[h], a_b[h]).astype(BF16) for h in H]
            o = [_dot(amat[h], vb[h]) + o_st[h] for h in H]
            for h in H:
                st_ref[c, h] = st[h]
                state_scr[h] = st[h] * cm[h][3] + upd[h]
            for h in H:
                g = cm[h][7]
                vc = slice(h * GLA_DV_HEAD, (h + 1) * GLA_DV_HEAD)
                r = lax.rsqrt(jnp.mean(o[h] * o[h], axis=-1, keepdims=True) + RMS_EPS)
                on = (o[h] * r) * gn_ref[:, vc]
                zg_ref[rows, vc] = (on * (g * _sigmoid(g))).astype(BF16)
            return carry

        lax.fori_loop(0, ncb, chunk, 0)

    return hosted_call(
        kern, side, name=name, grid=(S // TB,),
        in_specs=[pl.BlockSpec((TB, GLA_IN_PAD), lambda i: (i, 0)),
                  pl.BlockSpec((128, GLA_DK), lambda i: (0, 0)), _row_spec(GLA_DK), _row_spec(GLA_DV)],
        out_specs=[pl.BlockSpec((TB, GLA_DV), lambda i: (i, 0)),
                   pl.BlockSpec((ncb, GLA_HEADS, GLA_DV_HEAD, GLA_DK_HEAD), lambda i: (i, 0, 0, 0))],
        out_shape=[jax.ShapeDtypeStruct((S, GLA_DV), BF16),
                   jax.ShapeDtypeStruct((S // CHUNK, GLA_HEADS, GLA_DV_HEAD, GLA_DK_HEAD), F32)],
        scratch_shapes=[pltpu.VMEM((GLA_HEADS, GLA_DV_HEAD, GLA_DK_HEAD), F32), pltpu.VMEM((TB, GLA_DK), F32)],
        dims=("arbitrary",), args=(proj, wgk_p, bgk, gnorm))


def gla_bwd(proj, states, dzg, wgk_p, bgk, gnorm, name, side=None):
    S = proj.shape[0]
    TB = min(GLA_TB, S)
    ncb = TB // CHUNK
    nb = S // TB

    def kern(proj_ref, st_ref, dzg_ref, wgk_ref, bgk_ref, gn_ref,
             dproj_ref, dwgk_ref, dbgk_ref, dgn_ref, dstate_scr, la_scr, gk_scr, dgk_scr):
        @pl.when(pl.program_id(0) == 0)
        def _():
            dstate_scr[...] = jnp.zeros_like(dstate_scr)
            dwgk_ref[...] = jnp.zeros_like(dwgk_ref)
            dbgk_ref[...] = jnp.zeros_like(dbgk_ref)
            dgn_ref[...] = jnp.zeros_like(dgn_ref)
        lr = proj_ref[:, GLA_LR_OFF:GLA_IN_PAD].astype(BF16)
        gk = _dot(lr, wgk_ref[...]) + bgk_ref[...]
        gk_scr[...] = gk
        la_scr[...] = _log_sigmoid(gk) * (1.0 / 16.0)
        lower = _tri64()
        tri = lower.astype(F32)
        r_i = lax.broadcasted_iota(jnp.int32, (CHUNK, CHUNK), 0)
        c_i = lax.broadcasted_iota(jnp.int32, (CHUNK, CHUNK), 1)
        triu = (c_i >= r_i).astype(F32)
        last_row = lax.broadcasted_iota(jnp.int32, (CHUNK, GLA_DK_HEAD), 0) == CHUNK - 1

        def chunk(cc, carry):
            c = ncb - 1 - cc
            rows = pl.ds(pl.multiple_of(c * CHUNK, CHUNK), CHUNK)
            b = jnp.dot(tri, la_scr[rows, :], precision=HIGHEST, preferred_element_type=F32)
            H = range(GLA_HEADS)
            kcs = [slice(h * GLA_DK_HEAD, (h + 1) * GLA_DK_HEAD) for h in H]
            vcs = [slice(h * GLA_DV_HEAD, (h + 1) * GLA_DV_HEAD) for h in H]
            cm = [_gla_chunk_common(proj_ref, rows, b, h) for h in H]
            ep, en, ee, dec = ([m[i] for m in cm] for i in range(4))
            gs = [m[7] for m in cm]
            qf = [m[4] * m[0] for m in cm]
            kn = [m[5] * m[1] for m in cm]
            qn = [m[4] * m[1] for m in cm]
            kp = [m[5] * m[0] for m in cm]
            ke = [m[5] * m[2] for m in cm]
            qf_b, kn_b, qn_b, kp_b, ke_b = ([t.astype(BF16) for t in ts] for ts in (qf, kn, qn, kp, ke))
            vb = [m[6].astype(BF16) for m in cm]
            st = [st_ref[c, h] for h in H]
            st_b = [s.astype(BF16) for s in st]
            dst = [dstate_scr[h] for h in H]
            dst_b = [s.astype(BF16) for s in dst]
            a_f = [_dot_nt(qf_b[h], kn_b[h]) for h in H]
            a_b = [_dot_nt(qn_b[h], kp_b[h]) for h in H]
            o_st = [_dot_nt(qf_b[h], st_b[h]) for h in H]
            dv_st = [_dot_nt(ke_b[h], dst_b[h]) for h in H]
            dke = [_dot(vb[h], dst_b[h]) for h in H]
            amat = [jnp.where(lower, a_f[h], a_b[h]).astype(BF16) for h in H]
            o = [_dot(amat[h], vb[h]) + o_st[h] for h in H]
            do_b, dgs = [], []
            for h in H:
                r = lax.rsqrt(jnp.mean(o[h] * o[h], axis=-1, keepdims=True) + RMS_EPS)
                oh = o[h] * r
                gn = gn_ref[:, vcs[h]]
                g = gs[h]
                sg = _sigmoid(g)
                dz = dzg_ref[rows, vcs[h]]
                don = dz * (g * sg)
                dgs.append(dz * (oh * gn) * (sg * (1.0 + g * (1.0 - sg))))
                dgn_ref[:, vcs[h]] += jnp.sum(don * oh, axis=0, keepdims=True)
                doh = don * gn
                do_b.append((r * (doh - oh * jnp.mean(doh * oh, axis=-1, keepdims=True))).astype(BF16))
            da = [_dot_nt(do_b[h], vb[h]) for h in H]
            dv = [_dot_tn(amat[h], do_b[h]) + dv_st[h] for h in H]
            dqf_st = [_dot(do_b[h], st_b[h]) for h in H]
            dst_upd = [_dot_tn(do_b[h], qf_b[h]) for h in H]
            da_f = [jnp.where(lower, da[h], 0.0).astype(BF16) for h in H]
            da_b = [jnp.where(lower, 0.0, da[h]).astype(BF16) for h in H]
            dqf = [_dot(da_f[h], kn_b[h]) + dqf_st[h] for h in H]
            dkn = [_dot_tn(da_f[h], qf_b[h]) for h in H]
            dqn = [_dot(da_b[h], kp_b[h]) for h in H]
            dkp = [_dot_tn(da_b[h], qn_b[h]) for h in H]
            dbs = []
            for h in H:
                ddec = jnp.sum(dst[h] * st[h], axis=0, keepdims=True)
                dstate_scr[h] = dst[h] * dec[h] + dst_upd[h]
                db = dqf[h] * qf[h] - dkn[h] * kn[h] - dqn[h] * qn[h] + dkp[h] * kp[h] - dke[h] * ke[h]
                dbl = jnp.sum(dke[h] * ke[h], axis=0, keepdims=True) + ddec * dec[h]
                dbs.append(db + jnp.where(last_row, dbl, 0.0))
            dla = [jnp.dot(triu, dbs[h], precision=HIGHEST, preferred_element_type=F32) for h in H]
            for h in H:
                dq = (dqf[h] * ep[h] + dqn[h] * en[h]) * GLA_SCALE
                dk = dkn[h] * en[h] + dkp[h] * ep[h] + dke[h] * ee[h]
                dgk_scr[rows, kcs[h]] = dla[h] * (1.0 / 16.0) * _sigmoid(-gk_scr[rows, kcs[h]])
                dproj_ref[rows, kcs[h]] = dq.astype(BF16)
                dproj_ref[rows, GLA_DK + h * GLA_DK_HEAD:GLA_DK + (h + 1) * GLA_DK_HEAD] = dk.astype(BF16)
                dproj_ref[rows, 2 * GLA_DK + h * GLA_DV_HEAD:2 * GLA_DK + (h + 1) * GLA_DV_HEAD] = dv[h].astype(BF16)
                dproj_ref[rows, 2 * GLA_DK + GLA_DV + h * GLA_DV_HEAD:
                          2 * GLA_DK + GLA_DV + (h + 1) * GLA_DV_HEAD] = dgs[h].astype(BF16)
            return carry

        lax.fori_loop(0, ncb, chunk, 0)
        dgk = dgk_scr[...]
        dgk_b = dgk.astype(BF16)
        dproj_ref[:, GLA_LR_OFF:GLA_IN_PAD] = _dot_nt(dgk_b, wgk_ref[...]).astype(BF16)
        dwgk_ref[...] += _dot_tn(lr, dgk_b)
        dbgk_ref[...] += jnp.sum(dgk, axis=0, keepdims=True)

    rev = lambda i: (nb - 1 - i, 0)
    return hosted_call(
        kern, side, name=name, grid=(nb,),
        in_specs=[pl.BlockSpec((TB, GLA_IN_PAD), rev),
                  pl.BlockSpec((ncb, GLA_HEADS, GLA_DV_HEAD, GLA_DK_HEAD), lambda i: (nb - 1 - i, 0, 0, 0)),
                  pl.BlockSpec((TB, GLA_DV), rev),
                  pl.BlockSpec((128, GLA_DK), lambda i: (0, 0)), _row_spec(GLA_DK), _row_spec(GLA_DV)],
        out_specs=[pl.BlockSpec((TB, GLA_IN_PAD), rev),
                   pl.BlockSpec((128, GLA_DK), lambda i: (0, 0)), _row_spec(GLA_DK), _row_spec(GLA_DV)],
        out_shape=[jax.ShapeDtypeStruct((S, GLA_IN_PAD), BF16), jax.ShapeDtypeStruct((128, GLA_DK), F32),
                   jax.ShapeDtypeStruct((1, GLA_DK), F32), jax.ShapeDtypeStruct((1, GLA_DV), F32)],
        scratch_shapes=[pltpu.VMEM((GLA_HEADS, GLA_DV_HEAD, GLA_DK_HEAD), F32), pltpu.VMEM((TB, GLA_DK), F32),
                        pltpu.VMEM((TB, GLA_DK), F32), pltpu.VMEM((TB, GLA_DK), F32)],
        dims=("arbitrary",), args=(proj, states, dzg, wgk_p, bgk, gnorm))


ATT_TW = 1024
ATT_CLASSES = 3


def _att_window(i):
    return pl.multiple_of(jnp.maximum(i * ATT_TQ - LEFT_CHUNKS * CHUNK, 0), ATT_TQ)


def _att_rel_index():
    e = jnp.arange(ATT_TW)[None, :]
    d = jnp.where(e < ATT_KW, e, e - ATT_TW)
    off = (jnp.arange(ATT_CLASSES) * ATT_TQ)[:, None]
    return jnp.clip(off - d, -MAX_REL, MAX_REL) + MAX_REL


def _row_bits():
    return lax.broadcasted_iota(jnp.int32, (ATT_TQ, ATT_TW), 0)


def att_bias_tiles(rel_bias, name):
    pick = (jnp.arange(384)[:, None] == _att_rel_index().reshape(-1)[None, :]).astype(F32)
    tab = mm_f32(jnp.pad(rel_bias, ((0, 0), (0, 384 - N_REL))), pick, name + "_tab")
    tab = tab.reshape(ATT_HEADS * ATT_CLASSES, 1, ATT_TW)

    def kern(t_ref, o_ref):
        cls = pl.program_id(0) % ATT_CLASSES
        x = jnp.broadcast_to(t_ref[...], (ATT_TQ, ATT_TW))
        row = _row_bits()
        for b in range(8):
            x = jnp.where((row & (1 << b)) != 0, pltpu.roll(x, 1 << b, axis=1), x)
        x = x[:, :ATT_KW]
        qc = cls * (ATT_TQ // CHUNK) + lax.shift_right_arithmetic(
            lax.broadcasted_iota(jnp.int32, (ATT_TQ, ATT_KW), 0), 6)
        kc = lax.shift_right_arithmetic(lax.broadcasted_iota(jnp.int32, (ATT_TQ, ATT_KW), 1), 6)
        o_ref[...] = jnp.where((kc <= qc) & (kc >= qc - LEFT_CHUNKS), x, NEG_INF)

    return pl.pallas_call(
        kern, name=name, grid=(ATT_HEADS * ATT_CLASSES,),
        in_specs=[pl.BlockSpec((None, 1, ATT_TW), lambda i: (i, 0, 0))],
        out_specs=pl.BlockSpec((None, ATT_TQ, ATT_KW), lambda i: (i, 0, 0)),
        out_shape=jax.ShapeDtypeStruct((ATT_HEADS * ATT_CLASSES, ATT_TQ, ATT_KW), F32),
        compiler_params=_cp(("parallel",)),
    )(tab)


def att_bias_grad(dbt, name):
    def kern(d_ref, o_ref):
        x = jnp.concatenate([d_ref[...], jnp.zeros((ATT_TQ, ATT_TW - ATT_KW), F32)], axis=1)
        row = _row_bits()
        for b in range(8):
            x = jnp.where((row & (1 << b)) != 0, pltpu.roll(x, ATT_TW - (1 << b), axis=1), x)
        o_ref[...] = jnp.sum(x, axis=0, keepdims=True)

    diag = pl.pallas_call(
        kern, name=name + "_diag", grid=(ATT_HEADS * ATT_CLASSES,),
        in_specs=[pl.BlockSpec((None, ATT_TQ, ATT_KW), lambda i: (i, 0, 0))],
        out_specs=pl.BlockSpec((None, 1, ATT_TW), lambda i: (i, 0, 0)),
        out_shape=jax.ShapeDtypeStruct((ATT_HEADS * ATT_CLASSES, 1, ATT_TW), F32),
        compiler_params=_cp(("parallel",)),
    )(dbt)
    diag = diag.reshape(ATT_HEADS, ATT_CLASSES * ATT_TW)
    onehot = (_att_rel_index().reshape(-1)[:, None] == jnp.arange(384)[None, :]).astype(F32)
    return mm_f32(diag, onehot, name + "_bins")[:, :N_REL]


def _att_scores(q_ref, kw, bias_ref):
    hs = [slice(hh * ATT_HD, (hh + 1) * ATT_HD) for hh in range(2)]
    q = [q_ref[:, h] * ATT_SCALE for h in hs]
    k = [kw[:, h] for h in hs]
    s = [_dot_nt(q[hh], k[hh]) + bias_ref[hh] for hh in range(2)]
    e = [jnp.exp(t - jnp.max(t, axis=-1, keepdims=True)) for t in s]
    inv = [1.0 / jnp.sum(t, axis=-1, keepdims=True) for t in e]
    return hs, q, k, e, inv


def _att_specs(S):
    nq = D_MODEL // 128
    q_spec = pl.BlockSpec((ATT_TQ, 128), lambda p, i: (i, p))
    k_spec = pl.BlockSpec((S, 128), lambda p, i: (0, nq + p))
    v_spec = pl.BlockSpec((S, 128), lambda p, i: (0, 2 * nq + p))
    b_spec = pl.BlockSpec((2, None, ATT_TQ, ATT_KW), lambda p, i: (p, jnp.minimum(i, ATT_CLASSES - 1), 0, 0))
    return q_spec, k_spec, v_spec, b_spec


def attn_fwd(qkv, bias, name, side=None):
    S = qkv.shape[0]
    q_spec, k_spec, v_spec, b_spec = _att_specs(S)

    def kern(q_ref, k_ref, v_ref, bias_ref, o_ref):
        ws = _att_window(pl.program_id(1))
        kw = k_ref[pl.ds(ws, ATT_KW), :]
        vw = v_ref[pl.ds(ws, ATT_KW), :]
        hs, _, _, e, inv = _att_scores(q_ref, kw, bias_ref)
        outs = [_dot(e[hh].astype(BF16), vw[:, hs[hh]]) * inv[hh] for hh in range(2)]
        o_ref[...] = jnp.concatenate(outs, axis=1).astype(BF16)

    return hosted_call(
        kern, side, name=name, grid=(ATT_HEADS // 2, S // ATT_TQ),
        in_specs=[q_spec, k_spec, v_spec, b_spec],
        out_specs=[pl.BlockSpec((ATT_TQ, 128), lambda p, i: (i, p))],
        out_shape=[jax.ShapeDtypeStruct((S, D_MODEL), BF16)],
        scratch_shapes=[], dims=("parallel", "arbitrary"), args=(qkv, qkv, qkv, bias))


def attn_bwd(qkv, bias, do, name, side=None):
    S = qkv.shape[0]
    nblk = S // ATT_TQ
    q_spec, k_spec, v_spec, b_spec = _att_specs(S)

    def kern(q_ref, k_ref, v_ref, bias_ref, do_ref, dqkv_ref, db_ref, dk_scr, dv_scr):
        i = pl.program_id(1)

        @pl.when(i == 0)
        def _():
            dk_scr[...] = jnp.zeros_like(dk_scr)
            dv_scr[...] = jnp.zeros_like(dv_scr)
            db_ref[...] = jnp.zeros_like(db_ref)
        ws = _att_window(i)
        win = pl.ds(ws, ATT_KW)
        kw = k_ref[win, :]
        vw = v_ref[win, :]
        o_cls = jnp.minimum(i, ATT_CLASSES - 1)
        R2 = range(2)
        hs, q, k, e, inv = _att_scores(q_ref, kw, bias_ref)
        do_h = [do_ref[:, h] for h in hs]
        dp = [_dot_nt(do_h[hh], vw[:, hs[hh]]) for hh in R2]
        p = [e[hh] * inv[hh] for hh in R2]
        dvs = [_dot_tn(p[hh].astype(BF16), do_h[hh]) for hh in R2]
        ds = [p[hh] * (dp[hh] - jnp.sum(p[hh] * dp[hh], axis=-1, keepdims=True)) for hh in R2]
        ds_b = [t.astype(BF16) for t in ds]
        dqs = [_dot(ds_b[hh], k[hh]) * ATT_SCALE for hh in R2]
        dks = [_dot_tn(ds_b[hh], q[hh]) for hh in R2]
        for hh in R2:
            db_ref[hh, o_cls] += ds[hh]
        dqkv_ref[0, pl.ds(pl.multiple_of(i * ATT_TQ, ATT_TQ), ATT_TQ), :] = jnp.concatenate(dqs, axis=1).astype(BF16)
        dk_scr[win, :] += jnp.concatenate(dks, axis=1)
        dv_scr[win, :] += jnp.concatenate(dvs, axis=1)

        @pl.when(i == nblk - 1)
        def _():
            dqkv_ref[1] = dk_scr[...].astype(BF16)
            dqkv_ref[2] = dv_scr[...].astype(BF16)

    return hosted_call(
        kern, side, name=name, grid=(ATT_HEADS // 2, nblk),
        in_specs=[q_spec, k_spec, v_spec, b_spec, pl.BlockSpec((ATT_TQ, 128), lambda p, i: (i, p))],
        out_specs=[pl.BlockSpec((3, S, 128), lambda p, i: (0, 0, p)),
                   pl.BlockSpec((2, ATT_CLASSES, ATT_TQ, ATT_KW), lambda p, i: (p, 0, 0, 0))],
        out_shape=[jax.ShapeDtypeStruct((3, S, D_MODEL), BF16),
                   jax.ShapeDtypeStruct((ATT_HEADS, ATT_CLASSES, ATT_TQ, ATT_KW), F32)],
        scratch_shapes=[pltpu.VMEM((S, 128), F32), pltpu.VMEM((S, 128), F32)],
        dims=("parallel", "arbitrary"), args=(qkv, qkv, qkv, bias, do))


def colsum3(a3, name):
    P, S, N = a3.shape
    tm = min(512, S)

    def kern(a_ref, o_ref):
        @pl.when(pl.program_id(1) == 0)
        def _():
            o_ref[...] = jnp.zeros_like(o_ref)
        o_ref[...] += jnp.sum(a_ref[...].astype(F32), axis=0, keepdims=True)

    return pl.pallas_call(
        kern, name=name, grid=(P, S // tm),
        in_specs=[pl.BlockSpec((None, tm, N), lambda p, i: (p, i, 0))],
        out_specs=pl.BlockSpec((None, 1, N), lambda p, i: (p, 0, 0)),
        out_shape=jax.ShapeDtypeStruct((P, 1, N), F32),
        compiler_params=_cp(("parallel", "arbitrary")),
    )(a3)


def _me():
    return lax.axis_index("x"), lax.axis_index("y"), lax.axis_index("c")


def _other_chips(x, y):
    return [(1 - x, y), (x, 1 - y), (1 - x, 1 - y)]


def all_gather8(x_shard, name):
    m_per, n = x_shard.shape

    def body(x_ref, out_ref, send_sems, recv_sems, local_sem):
        x, y, c = _me()
        me, sibling = (x, y, c), (x, y, 1 - c)
        chips = _other_chips(x, y)

        def rows(px, py, pc):
            return out_ref.at[pl.ds((4 * px + 2 * py + pc) * m_per, m_per), :]

        def copy(k, block, to, src=None):
            return pltpu.make_async_remote_copy(
                src_ref=rows(*block) if src is None else src, dst_ref=rows(*block),
                send_sem=send_sems.at[k], recv_sem=recv_sems.at[k], device_id=to, device_id_type=MESH)

        mine = pltpu.make_async_copy(x_ref, rows(*me), local_sem)
        mine.start()
        first = [copy(0, me, sibling, src=x_ref)]
        first += [copy(1 + j, me, (*chip, c), src=x_ref) for j, chip in enumerate(chips)]
        for cp in first:
            cp.start()
        passed = [copy(4 + j, (*chip, c), sibling) for j, chip in enumerate(chips)]
        for j, chip in enumerate(chips):
            copy(1 + j, (*chip, c), me).wait_recv()
            passed[j].start()
        copy(0, sibling, me).wait_recv()
        for j, chip in enumerate(chips):
            copy(4 + j, (*chip, 1 - c), me).wait_recv()
        for cp in first + passed:
            cp.wait_send()
        mine.wait()

    return pl.pallas_call(
        body, name=name,
        out_shape=jax.ShapeDtypeStruct((N_DEV * m_per, n), x_shard.dtype),
        in_specs=[pl.BlockSpec(memory_space=pltpu.VMEM)],
        out_specs=pl.BlockSpec(memory_space=pltpu.VMEM),
        scratch_shapes=[pltpu.SemaphoreType.DMA((7,)), pltpu.SemaphoreType.DMA((7,)), pltpu.SemaphoreType.DMA],
        compiler_params=pltpu.CompilerParams(vmem_limit_bytes=VMEM_LIMIT),
    )(x_shard)


def _half_rows(n_rows, c):
    h = n_rows // 2
    return pl.ds(c * h, h)


def _gathered_shape(shape, flavour):
    L, a, b = shape
    return {"col": (L, a, N_CHIPS * b), "row": (L, N_CHIPS * a, b), "lead": (N_CHIPS, L, a, b)}[flavour]


def _gathered_part(out_ref, shape, flavour, s, rows):
    L, a, b = shape
    if flavour == "col":
        return out_ref.at[:, rows, pl.ds(s * b, b)]
    if flavour == "row":
        return out_ref.at[:, pl.ds(s * a + rows.start, rows.size), :]
    return out_ref.at[s, :, rows, :]


def gather_side(shards, flavours):
    n = len(shards)
    shapes = [w.shape for w in shards]

    def copies(w_refs, out_refs, send_sems, recv_sems, local_sems):
        x, y, c = _me()
        sibling = (x, y, 1 - c)
        chips = _other_chips(x, y)
        me_s = 2 * x + y

        def copy(k, src, dst, to):
            return pltpu.make_async_remote_copy(src_ref=src, dst_ref=dst, send_sem=send_sems.at[k],
                                                recv_sem=recv_sems.at[k], device_id=to, device_id_type=MESH)

        own, first, landed, passed, passed_in = [], [], [], [], []
        for w in range(n):
            shp, fl = shapes[w], flavours[w]
            my_half = _half_rows(shp[1], c)
            sib_half = _half_rows(shp[1], 1 - c)
            own.append(copy(7 * w + 6, w_refs[w], _gathered_part(out_refs[w], shp, fl, me_s, pl.ds(0, shp[1])), sibling))
            for j, chip in enumerate(chips):
                s = 2 * chip[0] + chip[1]
                first.append(copy(7 * w + j, w_refs[w].at[:, my_half, :],
                                  _gathered_part(out_refs[w], shp, fl, me_s, my_half), (*chip, c)))
                part = _gathered_part(out_refs[w], shp, fl, s, my_half)
                landed.append(copy(7 * w + j, part, part, (*chip, c)))
                passed.append(copy(7 * w + 3 + j, part, part, sibling))
                theirs = _gathered_part(out_refs[w], shp, fl, s, sib_half)
                passed_in.append(copy(7 * w + 3 + j, theirs, theirs, sibling))
        return own, first, landed, passed, passed_in

    def start(*refs):
        own, first, _, _, _ = copies(*refs)
        for cp in first + own:
            cp.start()

    def wait(*refs):
        own, first, landed, passed, passed_in = copies(*refs)
        for arrived, onward in zip(landed, passed):
            arrived.wait_recv()
            onward.start()
        for cp in passed_in:
            cp.wait_recv()
        for cp in own:
            cp.wait()
        for cp in first + passed:
            cp.wait_send()

    out_shapes = [jax.ShapeDtypeStruct(_gathered_shape(s, f), w.dtype) for w, s, f in zip(shards, shapes, flavours)]
    return Side(shards, out_shapes, 7 * n, 0, start, wait)


def swap_halves(gs, name):
    n = len(gs)

    def body(*refs):
        g_refs, out_refs = refs[:n], refs[n:2 * n]
        send_sems, recv_sems = refs[2 * n:]
        x, y, c = _me()
        cps = [pltpu.make_async_remote_copy(
            src_ref=g_refs[w].at[:, _half_rows(gs[w].shape[1], 1 - c), :], dst_ref=out_refs[w],
            send_sem=send_sems.at[w], recv_sem=recv_sems.at[w], device_id=(x, y, 1 - c), device_id_type=MESH)
            for w in range(n)]
        for cp in cps:
            cp.start()
        for cp in cps:
            cp.wait()

    any_spec = pl.BlockSpec(memory_space=pl.ANY)
    return pl.pallas_call(
        body, name=name,
        out_shape=[jax.ShapeDtypeStruct((g.shape[0], g.shape[1] // 2, g.shape[2]), g.dtype) for g in gs],
        in_specs=[any_spec] * n, out_specs=[any_spec] * n,
        scratch_shapes=[pltpu.SemaphoreType.DMA((n,)), pltpu.SemaphoreType.DMA((n,))],
    )(*gs)


def add_half(g, r1, c_idx, name):
    n, R, C = g.shape
    half = R // 2
    tr = _rows_block(half, C)
    nbh = half // tr

    def kern(c_ref, g_ref, r_ref, o_ref):
        o_ref[...] = g_ref[...] + r_ref[...]

    return pl.pallas_call(
        kern, name=name,
        grid_spec=pltpu.PrefetchScalarGridSpec(
            num_scalar_prefetch=1, grid=(n, nbh),
            in_specs=[pl.BlockSpec((1, tr, C), lambda d, r, c_ref: (d, c_ref[0] * nbh + r, 0)),
                      pl.BlockSpec((1, tr, C), lambda d, r, c_ref: (d, r, 0))],
            out_specs=pl.BlockSpec((1, tr, C), lambda d, r, c_ref: (d, r, 0))),
        out_shape=jax.ShapeDtypeStruct((n, half, C), F32),
        compiler_params=_cp(("parallel", "parallel")),
    )(c_idx, g, r1)


def exchange_side(ps):
    n = len(ps)

    def copies(p_refs, out_refs, send_sems, recv_sems, local_sems):
        x, y, c = _me()
        return [pltpu.make_async_remote_copy(
            src_ref=p_refs[w].at[2 * chip[0] + chip[1]], dst_ref=out_refs[w].at[j],
            send_sem=send_sems.at[3 * w + j], recv_sem=recv_sems.at[3 * w + j],
            device_id=(*chip, c), device_id_type=MESH)
            for w in range(n) for j, chip in enumerate(_other_chips(x, y))]

    def start(*refs):
        for cp in copies(*refs):
            cp.start()

    def wait(*refs):
        for cp in copies(*refs):
            cp.wait()

    return Side(ps, [jax.ShapeDtypeStruct((3,) + p.shape[1:], p.dtype) for p in ps], 3 * n, 0, start, wait)


def add_chips(p, r2, chip_idx, name):
    n, H, C = p.shape
    tr = _rows_block(H, C)

    def kern(s_ref, p_ref, r_ref, o_ref):
        o_ref[...] = ((p_ref[0] + r_ref[0]) + r_ref[1]) + r_ref[2]

    return pl.pallas_call(
        kern, name=name,
        grid_spec=pltpu.PrefetchScalarGridSpec(
            num_scalar_prefetch=1, grid=(H // tr,),
            in_specs=[pl.BlockSpec((1, tr, C), lambda r, s_ref: (s_ref[0], r, 0)),
                      pl.BlockSpec((3, tr, C), lambda r, s_ref: (0, r, 0))],
            out_specs=pl.BlockSpec((tr, C), lambda r, s_ref: (r, 0))),
        out_shape=jax.ShapeDtypeStruct((H, C), F32),
        compiler_params=_cp(("parallel",)),
    )(chip_idx, p, r2)


def swap_reduced(ss, name):
    n = len(ss)

    def body(*refs):
        s_refs, out_refs = refs[:n], refs[n:2 * n]
        send_sems, recv_sems = refs[2 * n:]
        x, y, c = _me()
        cps = [pltpu.make_async_remote_copy(src_ref=s_refs[w], dst_ref=out_refs[w], send_sem=send_sems.at[w],
                                            recv_sem=recv_sems.at[w], device_id=(x, y, 1 - c), device_id_type=MESH)
               for w in range(n)]
        for cp in cps:
            cp.start()
        for cp in cps:
            cp.wait()

    any_spec = pl.BlockSpec(memory_space=pl.ANY)
    return pl.pallas_call(
        body, name=name, out_shape=[jax.ShapeDtypeStruct(s.shape, s.dtype) for s in ss],
        in_specs=[any_spec] * n, out_specs=[any_spec] * n,
        scratch_shapes=[pltpu.SemaphoreType.DMA((n,)), pltpu.SemaphoreType.DMA((n,))],
    )(*ss)


BIG = (("gla_w_in", 2, (1024, GLA_IN // N_CHIPS), "lead"), ("gla_w_out", 2, (256, 1024), "row"),
       ("att_w_in", 2, (1024, 768), "col"), ("att_w_out", 2, (256, 1024), "row"),
       ("ff_w1", 4, (1024, 1024), "col"), ("ff_w2", 4, (1024, 1024), "row"))
FLAVOUR = {n: f for n, _, _, f in BIG}


def layer_weights(i):
    mixer = "gla" if i % 2 == 0 else "att"
    return (("in", mixer + "_w_in", i // 2), ("out", mixer + "_w_out", i // 2), ("w1", "ff_w1", i), ("w2", "ff_w2", i))


class Comm:
    def __init__(self, weights, core, chip):
        self.weights, self.core, self.chip = weights, core, chip
        self.c_idx = jnp.reshape(core, (1,)).astype(jnp.int32)
        self.chip_idx = jnp.reshape(chip, (1,)).astype(jnp.int32)
        self.reduced = {}

    def _shards(self, i):
        return [self.weights[n][l:l + 1].astype(BF16) for _, n, l in layer_weights(i)]

    def gather(self, i):
        return gather_side(self._shards(i), [FLAVOUR[n] for _, n, _ in layer_weights(i)])

    def full_weights(self, i, gathered):
        W = {}
        for (role, n, _), w in zip(layer_weights(i), gathered):
            if n == "gla_w_in":
                w = jnp.pad(w.transpose(1, 2, 0, 3).reshape(1, D_MODEL, GLA_IN), ((0, 0), (0, 0), (0, GLA_IN_PAD - GLA_IN)))
            W[role] = (w, 0)
        return W

    def first_weights(self):
        return self.full_weights(0, run_side(self.gather(0), "gather_w0"))

    def reduce_begin(self, tag, items):
        grads = [g for _, _, g in items]
        r1 = swap_halves(grads, f"rs_swap_{tag}")
        ps = [add_half(g, r, self.c_idx, f"rs_add2_{tag}_{w}") for w, (g, r) in enumerate(zip(grads, r1))]
        return tag, [(n, l) for n, l, _ in items], ps

    def exchange(self, pending):
        return exchange_side(pending[2])

    def reduce_mid(self, pending, landed):
        tag, keys, ps = pending
        for w, (key, p, r) in enumerate(zip(keys, ps, landed)):
            self.reduced[key] = add_chips(p, r, self.chip_idx, f"rs_add4_{tag}_{w}")

    def reduce_last(self, pending):
        self.reduce_mid(pending, run_side(self.exchange(pending), f"rs_xchg_{pending[0]}"))

    def reduce_end(self):
        keys = [(n, l) for n, L, _, _ in BIG for l in range(L)]
        mine = [self.reduced[k] for k in keys]
        theirs = swap_reduced(mine, "rs_join")
        low = self.core == 0
        full = {k: jnp.concatenate([jnp.where(low, m, t), jnp.where(low, t, m)], axis=0)
                for k, m, t in zip(keys, mine, theirs)}
        return {n: jnp.stack([full[(n, l)] for l in range(L)]) for n, L, _, _ in BIG}


def local_step(x, target, mods, comm, small):
    S, D = x.shape
    row = lambda v: v.reshape(1, -1)
    saved = []
    tiles = [att_bias_tiles(small["att_rel_bias"][j], f"att_tiles_{j}").reshape(ATT_HEADS, ATT_CLASSES, ATT_TQ, ATT_KW)
             for j in range(2)]
    wgk_p = [jnp.pad(small["gla_w_gk2"][j], ((0, 128 - GLA_RANK), (0, 0))).astype(BF16) for j in range(2)]

    u1 = modulate(x, row(mods[0, 1]), row(mods[0, 0]), "mod_first")
    Ws = [comm.first_weights()]
    for i in range(DEPTH):
        j = i // 2
        W = Ws[i]
        sh1, sc1, g1, sh2, sc2, g2 = (row(mods[i, k]) for k in range(6))
        nxt = min(i + 1, DEPTH - 1)
        side = comm.gather(i + 1) if i + 1 < DEPTH else None
        if i % 2 == 0:
            proj = mm_plain(u1, *W["in"], f"gla_in_{i}")
            (zmix, states), landed = gla_fwd(proj, wgk_p[j], row(small["gla_b_gk"][j]), row(small["gla_g_norm"][j]),
                                             f"gla_fwd_{i}", side)
        else:
            proj = mm_plain(u1, *W["in"], f"att_in_{i}", mode="bf16", bias=row(small["att_b_in"][j]))
            (zmix,), landed = attn_fwd(proj, tiles[j], f"att_fwd_{i}", side)
            states = None
        if i + 1 < DEPTH:
            Ws.append(comm.full_weights(i + 1, landed))
        y1, x_mid, u2 = mm_down_ln(zmix, *W["out"], x, 1.0 + g1, row(small["ln_g"][i, 0]), row(small["ln_b"][i, 0]),
                                   sc2, sh2, f"mix_out_{i}")
        act = mm_plain(u2, *W["w1"], f"ff_up_{i}", mode="mlp_up")
        y2, x_out, u_next = mm_down_ln(act, *W["w2"], x_mid, 1.0 + g2, row(small["ln_g"][i, 1]),
                                       row(small["ln_b"][i, 1]), row(mods[nxt, 1]), row(mods[nxt, 0]), f"ff_out_{i}")
        saved.append(dict(x_in=x, u1=u1, proj=proj, zmix=zmix, states=states, y1=y1, x_mid=x_mid, u2=u2,
                          act=act, y2=y2))
        x, u1 = x_out, u_next

    dx, sq = loss_head(x, target, "loss_head")

    g_small = dict(ln_g=[None] * DEPTH, ln_b=[None] * DEPTH, gla_w_gk2=[None] * 2, gla_b_gk=[None] * 2,
                   gla_g_norm=[None] * 2, att_b_in=[None] * 2, att_rel_bias=[None] * 2)
    dmods = [None] * DEPTH
    later = []

    for i in reversed(range(DEPTH)):
        j = i // 2
        sv = saved[i]
        W = Ws[i]
        sh1, sc1, g1, sh2, sc2, g2 = (row(mods[i, k]) for k in range(6))
        dz2, dy2, s_ln2 = ln_bwd(dx, sv["x_mid"], sv["y2"], 1.0 + g2, row(small["ln_g"][i, 1]), f"ln2_bwd_{i}")
        dh = mm_plain(dy2, *W["w2"], f"ff_dn_{i}", mode="mlp_dn", nt=True, h=sv["act"])
        g_w2 = mm_w(sv["act"], dy2, f"ff_w2g_{i}").reshape(N_CHIPS, D_FF // N_CHIPS, D)
        g_w1 = mm_w(sv["u2"], dh, f"ff_w1g_{i}", chips_out=True)
        dx_mid, s_m2 = mm_down_comb(dh, *W["w1"], dz2, sv["x_mid"], 1.0 + sc2, f"ff_dx_{i}")
        dz1, dy1, s_ln1 = ln_bwd(dx_mid, sv["x_in"], sv["y1"], 1.0 + g1, row(small["ln_g"][i, 0]), f"ln1_bwd_{i}")
        pending = comm.reduce_begin(i, [("ff_w1", i, g_w1), ("ff_w2", i, g_w2)] + later)
        side = comm.exchange(pending)
        mixer = "gla" if i % 2 == 0 else "att"
        if i % 2 == 0:
            g_out = mm_w(sv["zmix"], dy1, f"gla_wog_{i}").reshape(N_CHIPS, D // N_CHIPS, D)
            dzg = mm_plain(dy1, *W["out"], f"gla_dz_{i}", nt=True)
            (dproj, dwgk, dbgk, dgn), landed = gla_bwd(sv["proj"], sv["states"], dzg, wgk_p[j],
                                                       row(small["gla_b_gk"][j]), row(small["gla_g_norm"][j]),
                                                       f"gla_bwd_{i}", side)
            g_small["gla_w_gk2"][j] = dwgk[:GLA_RANK]
            g_small["gla_b_gk"][j] = dbgk[0]
            g_small["gla_g_norm"][j] = dgn[0].reshape(GLA_HEADS, GLA_DV_HEAD)
            gwi = mm_w(sv["u1"], dproj, f"gla_wig_{i}")[:, :GLA_IN]
            g_in = gwi.reshape(D, N_CHIPS, GLA_IN // N_CHIPS).transpose(1, 0, 2)
            dx, s_m1 = mm_down_comb(dproj, *W["in"], dz1, sv["x_in"], 1.0 + sc1, f"mix_dx_{i}")
        else:
            g_out = mm_w(sv["zmix"], dy1, f"att_wog_{i}").reshape(N_CHIPS, D // N_CHIPS, D)
            do = mm_plain(dy1, *W["out"], f"att_do_{i}", mode="bf16", nt=True)
            (dqkv, dbt), landed = attn_bwd(sv["proj"], tiles[j], do, f"att_bwd_{i}", side)
            g_small["att_rel_bias"][j] = att_bias_grad(dbt.reshape(ATT_HEADS * ATT_CLASSES, ATT_TQ, ATT_KW),
                                                       f"att_bias_{i}")
            g_small["att_b_in"][j] = colsum3(dqkv, f"att_bin_{i}").reshape(3 * D)
            g_in = mm_w_chips3(sv["u1"], dqkv, f"att_wig_{i}")
            dx, s_m1 = mm_down_comb(dqkv, *W["in"], dz1, sv["x_in"], 1.0 + sc1, f"mix_dx_{i}", parts=3)
        comm.reduce_mid(pending, landed)
        later = [(mixer + "_w_in", j, g_in), (mixer + "_w_out", j, g_out)]
        g_small["ln_g"][i] = jnp.stack([s_ln1[0], s_ln2[0]])
        g_small["ln_b"][i] = jnp.stack([s_ln1[1], s_ln2[1]])
        dmods[i] = jnp.stack([s_m1[1], s_m1[0], s_ln1[2], s_m2[1], s_m2[0], s_ln2[2]])
    comm.reduce_last(comm.reduce_begin("last", later))

    g_small = {n: jnp.stack(v) for n, v in g_small.items()}
    return sq, dx, jnp.stack(dmods), g_small


SMALL_SHARDED = (("ln_g", (4, 2, 256)), ("ln_b", (4, 2, 256)), ("gla_g_norm", (2, 4, 64)),
                 ("gla_w_gk2", (2, 16, 128)), ("att_b_in", (2, 768)))
SMALL_FULL = dict(ln_g=(4, 2, 1024), ln_b=(4, 2, 1024), gla_g_norm=(2, 4, 256), gla_w_gk2=(2, 16, 512),
                  att_b_in=(2, 3072), gla_b_gk=(2, 512), att_rel_bias=(2, 16, 257))
SMALL_GRAD_ORDER = ("ln_g", "ln_b", "gla_g_norm", "gla_w_gk2", "att_b_in", "gla_b_gk", "att_rel_bias")


def _pack_small(arrs, rows_total):
    parts = []
    for a in arrs:
        flat = a.reshape(-1)
        pad = (-flat.shape[0]) % PACK_W
        parts.append(jnp.pad(flat, (0, pad)).reshape(-1, PACK_W))
    buf = jnp.concatenate(parts, axis=0)
    return jnp.pad(buf, ((0, rows_total - buf.shape[0]), (0, 0)))


def _unpack_small(buf, shapes):
    out, r = [], 0
    for shp in shapes:
        n = 1
        for s in shp:
            n *= s
        nr = (n + PACK_W - 1) // PACK_W
        out.append(buf[..., r:r + nr, :].reshape(buf.shape[:-2] + (nr * PACK_W,))[..., :n].reshape(buf.shape[:-2] + shp))
        r += nr
    return out


def _unshard_last(g4):
    nd = g4.ndim
    perm = tuple(range(1, nd - 1)) + (0, nd - 1)
    t = g4.transpose(perm)
    return t.reshape(t.shape[:-2] + (-1,))


def _shard_last(full, s):
    n = full.shape[-1] // N_CHIPS
    return lax.dynamic_slice_in_dim(full, s * n, n, axis=full.ndim - 1)


WEIGHT_NAMES = ("w_ada", "b_ada", "ln_g", "ln_b", "gla_w_in", "gla_w_gk2", "gla_b_gk", "gla_g_norm", "gla_w_out",
                "att_w_in", "att_b_in", "att_rel_bias", "att_w_out", "ff_w1", "ff_w2")


def kernel(x, c, w_ada, b_ada, ln_g, ln_b, gla_w_in, gla_w_gk2, gla_b_gk, gla_g_norm, gla_w_out, att_w_in, att_b_in, att_rel_bias, att_w_out, ff_w1, ff_w2, loss_target, m_w_ada, m_b_ada, m_ln_g, m_ln_b, m_gla_w_in, m_gla_w_gk2, m_gla_b_gk, m_gla_g_norm, m_gla_w_out, m_att_w_in, m_att_b_in, m_att_rel_bias, m_att_w_out, m_ff_w1, m_ff_w2, v_w_ada, v_b_ada, v_ln_g, v_ln_b, v_gla_w_in, v_gla_w_gk2, v_gla_b_gk, v_gla_g_norm, v_gla_w_out, v_att_w_in, v_att_b_in, v_att_rel_bias, v_att_w_out, v_ff_w1, v_ff_w2):
    weights = dict(w_ada=w_ada, b_ada=b_ada, ln_g=ln_g, ln_b=ln_b, gla_w_in=gla_w_in, gla_w_gk2=gla_w_gk2,
                   gla_b_gk=gla_b_gk, gla_g_norm=gla_g_norm, gla_w_out=gla_w_out, att_w_in=att_w_in,
                   att_b_in=att_b_in, att_rel_bias=att_rel_bias, att_w_out=att_w_out, ff_w1=ff_w1, ff_w2=ff_w2)
    mom1 = dict(w_ada=m_w_ada, b_ada=m_b_ada, ln_g=m_ln_g, ln_b=m_ln_b, gla_w_in=m_gla_w_in, gla_w_gk2=m_gla_w_gk2,
                gla_b_gk=m_gla_b_gk, gla_g_norm=m_gla_g_norm, gla_w_out=m_gla_w_out, att_w_in=m_att_w_in,
                att_b_in=m_att_b_in, att_rel_bias=m_att_rel_bias, att_w_out=m_att_w_out, ff_w1=m_ff_w1, ff_w2=m_ff_w2)
    mom2 = dict(w_ada=v_w_ada, b_ada=v_b_ada, ln_g=v_ln_g, ln_b=v_ln_b, gla_w_in=v_gla_w_in, gla_w_gk2=v_gla_w_gk2,
                gla_b_gk=v_gla_b_gk, gla_g_norm=v_gla_g_norm, gla_w_out=v_gla_w_out, att_w_in=v_att_w_in,
                att_b_in=v_att_b_in, att_rel_bias=v_att_rel_bias, att_w_out=v_att_w_out, ff_w1=v_ff_w1, ff_w2=v_ff_w2)

    ax, ay, ac = lax.axis_index("x"), lax.axis_index("y"), lax.axis_index("c")
    chip = 2 * ax + ay
    dev = 2 * chip + ac
    S = x.shape[1]
    x2 = x.reshape(S, D_MODEL)
    t2 = loss_target.reshape(S, D_MODEL)

    comm = Comm(weights, ac, chip)

    small_rows = 16
    spack = _pack_small([c] + [weights[n] for n, _ in SMALL_SHARDED], small_rows)
    sg = all_gather8(spack, "gather_small").reshape(N_DEV, small_rows, PACK_W)
    parts = _unpack_small(sg, [(1, D_MODEL)] + [shp for _, shp in SMALL_SHARDED])
    c_all = parts[0].reshape(N_DEV, D_MODEL)
    small = {n: _unshard_last(p[0::2]) for (n, _), p in zip(SMALL_SHARDED, parts[1:])}
    small["gla_b_gk"] = gla_b_gk
    small["att_rel_bias"] = att_rel_bias

    c_act = silu_rows(jnp.pad(c_all, ((0, 128 - N_DEV), (0, 0))), "silu_c")
    wa = w_ada.astype(BF16).transpose(1, 0, 2).reshape(1, D_MODEL, DEPTH * 6 * D_MODEL // N_CHIPS)
    mods_part = mm_plain(c_act, wa, 0, "ada_fwd", tm=128)[:N_DEV]
    mg = all_gather8(mods_part, "gather_mods").reshape(N_CHIPS, 2, N_DEV, DEPTH, 6 * D_MODEL // N_CHIPS)
    mods_mine = lax.dynamic_index_in_dim(mg[:, 0], dev, axis=1, keepdims=False)
    mods = mods_mine.transpose(1, 0, 2).reshape(DEPTH, 6 * D_MODEL) + b_ada
    mods = mods.reshape(DEPTH, 6, D_MODEL)

    sq, grad_x, dmods, g_small = local_step(x2, t2, mods, comm, small)
    loss = lax.psum(0.5 * sq[0, 0] / D_MODEL, ("x", "y", "c"))

    g_shard = comm.reduce_end()

    dm_flat = dmods.reshape(DEPTH, 6 * D_MODEL)
    g_rows = 80
    gpack = _pack_small([dm_flat] + [g_small[n] for n in SMALL_GRAD_ORDER], g_rows)
    gg = all_gather8(gpack, "gather_small_grads").reshape(N_DEV, g_rows, PACK_W)
    gsum = sum_over_devices(gg, "sum_small_grads")
    sums = _unpack_small(gsum, [(DEPTH, 6 * D_MODEL)] + [SMALL_FULL[n] for n in SMALL_GRAD_ORDER])
    grads = dict(b_ada=sums[0])
    for n, full_g in zip(SMALL_GRAD_ORDER, sums[1:]):
        grads[n] = full_g if n in ("gla_b_gk", "att_rel_bias") else _shard_last(full_g, chip)
    dm_all = _unpack_small(gg, [(DEPTH, 6 * D_MODEL)])[0]
    dm_cols = _shard_last(dm_all, chip).reshape(N_DEV, DEPTH * 6 * D_MODEL // N_CHIPS)
    dm_cols = jnp.pad(dm_cols, ((0, 128 - N_DEV), (0, 0))).astype(BF16)
    gwa = mm_w(c_act, dm_cols, "ada_bwd", ts=128)
    grads["w_ada"] = gwa.reshape(D_MODEL, DEPTH, 6 * D_MODEL // N_CHIPS).transpose(1, 0, 2)
    grads.update(g_shard)

    deltas, new_m, new_v = {}, {}, {}
    for n in WEIGHT_NAMES:
        deltas[n], new_m[n], new_v[n] = adamw(weights[n], grads[n], mom1[n], mom2[n], "adamw_" + n)

    return (loss, grad_x.reshape(1, S, D_MODEL), *[grads[n] for n in WEIGHT_NAMES], *[deltas[n] for n in WEIGHT_NAMES],
            *[new_m[n] for n in WEIGHT_NAMES], *[new_v[n] for n in WEIGHT_NAMES])
```

```python
import functools

import jax
import jax.numpy as jnp
from jax import lax
from jax.experimental import pallas as pl
from jax.experimental.pallas import tpu as pltpu

F32 = jnp.float32
BF16 = jnp.bfloat16
HIGHEST = lax.Precision.HIGHEST
MESH = pl.DeviceIdType.MESH

D_MODEL = 1024
DEPTH = 4
CHUNK = 64
GLA_HEADS = 4
GLA_DK = 512
GLA_DV = 1024
GLA_DK_HEAD = 128
GLA_DV_HEAD = 256
GLA_RANK = 16
GLA_IN = 3088
GLA_IN_PAD = 3200
GLA_LR_OFF = 3072
ATT_HEADS = 16
ATT_HD = 64
LEFT_CHUNKS = 8
MAX_REL = 128
N_REL = 257
D_FF = 4096
ALPHA = (2.0 * DEPTH) ** 0.25
LN_EPS = 1e-5
RMS_EPS = 1e-6
NEG_INF = -1e30
GLA_SCALE = GLA_DK_HEAD ** -0.5
ATT_SCALE = ATT_HD ** -0.5
ADAM_LR = 0.001
ADAM_B1 = 0.9
ADAM_B2 = 0.999
ADAM_EPS = 1e-08
ADAM_WD = 0.01
ADAM_STEP = 10

ATT_TQ = 256
ATT_KW = 768
GLA_TB = 256
VMEM_LIMIT = 56 * 1024 * 1024
N_CHIPS = 4
N_DEV = 8
PACK_W = 1024


def _dot(a, b):
    return jnp.dot(a, b, preferred_element_type=F32)


def _dot_nt(a, b):
    return lax.dot_general(a, b, (((1,), (1,)), ((), ())), preferred_element_type=F32)


def _dot_tn(a, b):
    return lax.dot_general(a, b, (((0,), (0,)), ((), ())), preferred_element_type=F32)


def _cp(sem, vmem=VMEM_LIMIT):
    return pltpu.CompilerParams(dimension_semantics=sem, vmem_limit_bytes=vmem)


def _row_spec(n):
    return pl.BlockSpec((1, n), lambda *_: (0, 0))


def _sigmoid(x):
    return 1.0 / (1.0 + jnp.exp(-x))


def _log_sigmoid(x):
    return jnp.minimum(x, 0.0) - jnp.log1p(jnp.exp(-jnp.abs(x)))


class Side:
    def __init__(self, ins, out_shapes, n_sems, n_local, start, wait):
        self.ins, self.out_shapes, self.n_sems, self.n_local = list(ins), list(out_shapes), n_sems, n_local
        self.start, self.wait = start, wait

    def sem_shapes(self):
        return [pltpu.SemaphoreType.DMA((self.n_sems,)), pltpu.SemaphoreType.DMA((self.n_sems,)),
                pltpu.SemaphoreType.DMA((max(self.n_local, 1),))]


def run_side(side, name):
    n_in = len(side.ins)
    n_out = len(side.out_shapes)

    def body(*refs):
        ins, outs, sems = refs[:n_in], refs[n_in:n_in + n_out], refs[n_in + n_out:]
        side.start(ins, outs, *sems)
        side.wait(ins, outs, *sems)

    any_spec = pl.BlockSpec(memory_space=pl.ANY)
    return pl.pallas_call(body, name=name, out_shape=side.out_shapes, in_specs=[any_spec] * n_in,
                          out_specs=[any_spec] * n_out, scratch_shapes=side.sem_shapes())(*side.ins)


def hosted_call(main, side, *, name, grid, in_specs, out_specs, out_shape, scratch_shapes, dims, args):
    if side is None:
        outs = pl.pallas_call(main, name=name, grid=grid, in_specs=in_specs, out_specs=out_specs,
                              out_shape=out_shape, scratch_shapes=scratch_shapes, compiler_params=_cp(dims))(*args)
        return list(outs), []
    n_mi, n_mo, n_ms = len(in_specs), len(out_specs), len(scratch_shapes)
    n_si, n_so = len(side.ins), len(side.out_shapes)

    def kern(*refs):
        mi, si = refs[:n_mi], refs[n_mi:n_mi + n_si]
        o0 = n_mi + n_si
        mo, so = refs[o0:o0 + n_mo], refs[o0 + n_mo:o0 + n_mo + n_so]
        s0 = o0 + n_mo + n_so
        ms, sems = refs[s0:s0 + n_ms], refs[s0 + n_ms:]
        ids = [pl.program_id(d) for d in range(len(grid))]
        first = functools.reduce(jnp.logical_and, [i == 0 for i in ids])
        last = functools.reduce(jnp.logical_and, [i == g - 1 for i, g in zip(ids, grid)])

        @pl.when(first)
        def _():
            side.start(si, so, *sems)
        main(*mi, *mo, *ms)

        @pl.when(last)
        def _():
            side.wait(si, so, *sems)

    any_spec = pl.BlockSpec(memory_space=pl.ANY)
    outs = pl.pallas_call(
        kern, name=name, grid=grid, in_specs=list(in_specs) + [any_spec] * n_si,
        out_specs=list(out_specs) + [any_spec] * n_so, out_shape=list(out_shape) + side.out_shapes,
        scratch_shapes=list(scratch_shapes) + side.sem_shapes(),
        compiler_params=_cp(("arbitrary",) * len(grid)))(*args, *side.ins)
    return list(outs[:n_mo]), list(outs[n_mo:])


def modulate(x, sc, sh, name):
    S, D = x.shape
    tm = min(512, S)

    def kern(x_ref, sc_ref, sh_ref, u_ref):
        u_ref[...] = (x_ref[...] * (1.0 + sc_ref[...]) + sh_ref[...]).astype(BF16)

    return pl.pallas_call(
        kern, name=name, grid=(S // tm,),
        in_specs=[pl.BlockSpec((tm, D), lambda i: (i, 0)), _row_spec(D), _row_spec(D)],
        out_specs=pl.BlockSpec((tm, D), lambda i: (i, 0)),
        out_shape=jax.ShapeDtypeStruct((S, D), BF16),
        compiler_params=_cp(("parallel",)),
    )(x, sc, sh)


def loss_head(x, t, name):
    S, D = x.shape
    tm = min(512, S)

    def kern(x_ref, t_ref, dx_ref, l_ref):
        @pl.when(pl.program_id(0) == 0)
        def _():
            l_ref[...] = jnp.zeros_like(l_ref)
        e = x_ref[...] - t_ref[...]
        dx_ref[...] = e * (1.0 / D)
        l_ref[...] += jnp.sum(e * e)

    return pl.pallas_call(
        kern, name=name, grid=(S // tm,),
        in_specs=[pl.BlockSpec((tm, D), lambda i: (i, 0)), pl.BlockSpec((tm, D), lambda i: (i, 0))],
        out_specs=[pl.BlockSpec((tm, D), lambda i: (i, 0)), pl.BlockSpec((8, 128), lambda i: (0, 0))],
        out_shape=[jax.ShapeDtypeStruct((S, D), F32), jax.ShapeDtypeStruct((8, 128), F32)],
        compiler_params=_cp(("arbitrary",)),
    )(x, t)


def silu_rows(c_all, name):
    def kern(c_ref, o_ref):
        c = c_ref[...]
        o_ref[...] = (c * _sigmoid(c)).astype(BF16)

    return pl.pallas_call(kern, name=name, out_shape=jax.ShapeDtypeStruct(c_all.shape, BF16))(c_all)


def sum_over_devices(g, name):
    n, R, C = g.shape

    def kern(g_ref, o_ref):
        acc = g_ref[0]
        for d in range(1, n):
            acc = acc + g_ref[d]
        o_ref[...] = acc

    return pl.pallas_call(kern, name=name, out_shape=jax.ShapeDtypeStruct((R, C), F32))(g)


def _rows_block(R, C, budget=1 << 20):
    if R * C * 4 <= budget or R % 8:
        return R
    tr = max(8, (budget // (C * 4)) // 8 * 8)
    while R % tr:
        tr -= 8
    return tr


def adamw(w, g, m, v, name):
    shape = w.shape
    C = shape[-1]
    R = w.size // C
    w2, g2, m2, v2 = (t.reshape(R, C) for t in (w, g, m, v))
    tr = _rows_block(R, C)
    c1 = 1.0 - ADAM_B1 ** ADAM_STEP
    c2 = 1.0 - ADAM_B2 ** ADAM_STEP

    def kern(w_ref, g_ref, m_ref, v_ref, d_ref, nm_ref, nv_ref):
        gg = g_ref[...]
        nm = ADAM_B1 * m_ref[...] + (1.0 - ADAM_B1) * gg
        nv = ADAM_B2 * v_ref[...] + (1.0 - ADAM_B2) * (gg * gg)
        m_hat = nm / c1
        v_hat = nv / c2
        d_ref[...] = -ADAM_LR * (m_hat / (jnp.sqrt(v_hat) + ADAM_EPS) + ADAM_WD * w_ref[...])
        nm_ref[...] = nm
        nv_ref[...] = nv

    spec = pl.BlockSpec((tr, C), lambda i: (i, 0))
    outs = pl.pallas_call(
        kern, name=name, grid=(R // tr,),
        in_specs=[spec] * 4, out_specs=[spec] * 3,
        out_shape=[jax.ShapeDtypeStruct((R, C), F32)] * 3,
        compiler_params=_cp(("parallel",)),
    )(w2, g2, m2, v2)
    return tuple(o.reshape(shape) for o in outs)


def _tn_for(N):
    for tn in (1024, 768, 640, 512, 384, 256, 128):
        if N % tn == 0:
            return tn
    return N


def mm_plain(a, b3, layer, name, *, mode="f32", nt=False, bias=None, h=None, tm=1024, side=None):
    M, K = a.shape
    N = b3.shape[1] if nt else b3.shape[2]
    tm = min(tm, M)
    tn = _tn_for(N)
    a_spec = pl.BlockSpec((tm, K), lambda j, i: (i, 0))
    if nt:
        b_spec = pl.BlockSpec((None, tn, K), lambda j, i: (layer, j, 0))
    else:
        b_spec = pl.BlockSpec((None, K, tn), lambda j, i: (layer, 0, j))
    o_spec = pl.BlockSpec((tm, tn), lambda j, i: (i, j))
    ins, in_specs = [a, b3], [a_spec, b_spec]
    if bias is not None:
        ins.append(bias)
        in_specs.append(pl.BlockSpec((1, tn), lambda j, i: (0, j)))
    if mode == "mlp_dn":
        ins.append(h)
        in_specs.append(o_spec)
    elif mode not in ("f32", "bf16", "mlp_up"):
        raise ValueError(mode)
    odt = F32 if mode == "f32" else BF16

    def kern(a_ref, b_ref, *rest):
        rest = list(rest)
        bias_ref = rest.pop(0) if bias is not None else None
        h_ref = rest.pop(0) if mode == "mlp_dn" else None
        o_ref = rest.pop(0)
        if nt:
            bt_ref = rest.pop(0)

            @pl.when(pl.program_id(1) == 0)
            def _():
                bt_ref[...] = b_ref[...].T
            acc = _dot(a_ref[...], bt_ref[...])
        else:
            acc = _dot(a_ref[...], b_ref[...])
        if bias_ref is not None:
            acc = acc + bias_ref[...]
        if mode == "mlp_up":
            r = jnp.maximum(acc, 0.0)
            acc = r * r
        elif mode == "mlp_dn":
            acc = acc * (2.0 * jnp.sqrt(h_ref[...].astype(F32)))
        o_ref[...] = acc.astype(odt)

    outs, landed = hosted_call(
        kern, side, name=name, grid=(N // tn, M // tm), in_specs=in_specs, out_specs=[o_spec],
        out_shape=[jax.ShapeDtypeStruct((M, N), odt)],
        scratch_shapes=[pltpu.VMEM((K, tn), BF16)] if nt else [], dims=("parallel", "arbitrary"), args=tuple(ins))
    return outs[0] if side is None else (outs[0], landed)


def mm_down_ln(a, b3, layer, x_in, gate1p, ln_g, ln_b, sc_next, sh_next, name, *, tm=256):
    M, K = a.shape
    D = b3.shape[2]
    tm = min(tm, M)

    def kern(a_ref, b_ref, x_ref, gp_ref, lg_ref, lb_ref, sc_ref, sh_ref, y_ref, xo_ref, u_ref):
        y = _dot(a_ref[...], b_ref[...])
        y_ref[...] = y
        z = ALPHA * x_ref[...] + gp_ref[...] * y
        mu = jnp.mean(z, axis=-1, keepdims=True)
        zc = z - mu
        var = jnp.mean(zc * zc, axis=-1, keepdims=True)
        xo = (zc * lax.rsqrt(var + LN_EPS)) * lg_ref[...] + lb_ref[...]
        xo_ref[...] = xo
        u_ref[...] = (xo * (1.0 + sc_ref[...]) + sh_ref[...]).astype(BF16)

    tile = pl.BlockSpec((tm, D), lambda i: (i, 0))
    return pl.pallas_call(
        kern, name=name, grid=(M // tm,),
        in_specs=[pl.BlockSpec((tm, K), lambda i: (i, 0)), pl.BlockSpec((None, K, D), lambda i: (layer, 0, 0)), tile]
        + [_row_spec(D)] * 5,
        out_specs=[tile, tile, tile],
        out_shape=[jax.ShapeDtypeStruct((M, D), F32)] * 2 + [jax.ShapeDtypeStruct((M, D), BF16)],
        compiler_params=_cp(("parallel",)),
    )(a, b3, x_in, gate1p, ln_g, ln_b, sc_next, sh_next)


def mm_down_comb(a, b3, layer, dz, x_in, sc1p, name, *, parts=1, tm=256):
    D, K = b3.shape[1], b3.shape[2]
    M = a.shape[-2]
    kp = K // parts
    tm = min(tm, M)

    def kern(*refs):
        a_refs = refs[:parts]
        b_ref, dz_ref, x_ref, sp_ref, dx_ref, s_ref = refs[parts:]

        @pl.when(pl.program_id(0) == 0)
        def _():
            s_ref[...] = jnp.zeros_like(s_ref)
        if parts == 1:
            du = _dot_nt(a_refs[0][...], b_ref[...])
        else:
            du = _dot_nt(a_refs[0][...], b_ref[:, 0:kp])
            for p in range(1, parts):
                du = du + _dot_nt(a_refs[p][...], b_ref[:, p * kp:(p + 1) * kp])
        dx_ref[...] = ALPHA * dz_ref[...] + du * sp_ref[...]
        s_ref[0:1, :] += jnp.sum(du * x_ref[...], axis=0, keepdims=True)
        s_ref[1:2, :] += jnp.sum(du, axis=0, keepdims=True)

    tile = pl.BlockSpec((tm, D), lambda i: (i, 0))
    if parts == 1:
        a_ins, a_specs = [a], [pl.BlockSpec((tm, K), lambda i: (i, 0))]
    else:
        a_ins = [a] * parts
        a_specs = [pl.BlockSpec((None, tm, kp), functools.partial(lambda i, p: (p, i, 0), p=p)) for p in range(parts)]
    return pl.pallas_call(
        kern, name=name, grid=(M // tm,),
        in_specs=a_specs + [pl.BlockSpec((None, D, K), lambda i: (layer, 0, 0)), tile, tile, _row_spec(D)],
        out_specs=[tile, pl.BlockSpec((8, D), lambda i: (0, 0))],
        out_shape=[jax.ShapeDtypeStruct((M, D), F32), jax.ShapeDtypeStruct((8, D), F32)],
        compiler_params=_cp(("arbitrary",)),
    )(*a_ins, b3, dz, x_in, sc1p)


def mm_w(a, b, name, *, ts=2048, tk=512, chips_out=False, b_parts=1, tn=None):
    S, K = a.shape
    npart = b.shape[-1]
    N = npart * b_parts
    ts = min(ts, S)
    tk = min(tk, K)
    n_chip = N // N_CHIPS
    if tn is None:
        tn = _tn_for(n_chip if chips_out else npart)
    assert npart % tn == 0 and (not chips_out or n_chip % tn == 0)

    def kern(a_ref, b_ref, o_ref):
        @pl.when(pl.program_id(2) == 0)
        def _():
            o_ref[...] = jnp.zeros_like(o_ref)
        o_ref[...] += _dot_tn(a_ref[...], b_ref[...])

    if b_parts == 1:
        b_spec = pl.BlockSpec((ts, tn), lambda k, n, s: (s, n))
    else:
        per = npart // tn
        b_spec = pl.BlockSpec((None, ts, tn), lambda k, n, s: (n // per, s, n % per))
    if chips_out:
        per_chip = n_chip // tn
        o_spec = pl.BlockSpec((None, tk, tn), lambda k, n, s: (n // per_chip, k, n % per_chip))
        out_shape = jax.ShapeDtypeStruct((N_CHIPS, K, n_chip), F32)
    else:
        o_spec = pl.BlockSpec((tk, tn), lambda k, n, s: (k, n))
        out_shape = jax.ShapeDtypeStruct((K, N), F32)
    return pl.pallas_call(
        kern, name=name, grid=(K // tk, N // tn, S // ts),
        in_specs=[pl.BlockSpec((ts, tk), lambda k, n, s: (s, k)), b_spec],
        out_specs=o_spec, out_shape=out_shape,
        compiler_params=_cp(("parallel", "parallel", "arbitrary")),
    )(a, b)


def mm_w_chips3(a, b3, name, *, ts=512):
    S, K = a.shape
    P = b3.shape[2]
    n_chip = 3 * P // N_CHIPS
    ts = min(ts, S)
    pieces = []
    for chip in range(N_CHIPS):
        lo, hi = chip * n_chip, (chip + 1) * n_chip
        while lo < hi:
            part = lo // P
            w = min(hi, (part + 1) * P) - lo
            pieces.append((chip, lo - chip * n_chip, part, lo - part * P, w))
            lo += w

    def kern(a_ref, b_ref, o_ref):
        @pl.when(pl.program_id(0) == 0)
        def _():
            o_ref[...] = jnp.zeros_like(o_ref)
        at = a_ref[...].T
        for chip, oc, part, pc, w in pieces:
            o_ref[chip, :, oc:oc + w] += _dot(at, b_ref[part, :, pc:pc + w])

    return pl.pallas_call(
        kern, name=name, grid=(S // ts,),
        in_specs=[pl.BlockSpec((ts, K), lambda s: (s, 0)), pl.BlockSpec((3, ts, P), lambda s: (0, s, 0))],
        out_specs=pl.BlockSpec((N_CHIPS, K, n_chip), lambda s: (0, 0, 0)),
        out_shape=jax.ShapeDtypeStruct((N_CHIPS, K, n_chip), F32),
        compiler_params=_cp(("arbitrary",)),
    )(a, b3)


def mm_f32(a, b, name):
    def kern(a_ref, b_ref, o_ref):
        o_ref[...] = jnp.dot(a_ref[...], b_ref[...], precision=HIGHEST, preferred_element_type=F32)

    return pl.pallas_call(kern, name=name, out_shape=jax.ShapeDtypeStruct((a.shape[0], b.shape[1]), F32),
                          compiler_params=pltpu.CompilerParams(vmem_limit_bytes=VMEM_LIMIT))(a, b)


def ln_bwd(dxo, x_in, y, gate1p, ln_g, name, *, tm=256):
    S, D = dxo.shape
    tm = min(tm, S)

    def kern(dxo_ref, x_ref, y_ref, gp_ref, lg_ref, dz_ref, dy_ref, s_ref):
        @pl.when(pl.program_id(0) == 0)
        def _():
            s_ref[...] = jnp.zeros_like(s_ref)
        dxo_t = dxo_ref[...]
        yv = y_ref[...]
        z = ALPHA * x_ref[...] + gp_ref[...] * yv
        mu = jnp.mean(z, axis=-1, keepdims=True)
        zc = z - mu
        var = jnp.mean(zc * zc, axis=-1, keepdims=True)
        rstd = lax.rsqrt(var + LN_EPS)
        xhat = zc * rstd
        dxh = dxo_t * lg_ref[...]
        dz = rstd * (dxh - jnp.mean(dxh, axis=-1, keepdims=True)
                     - xhat * jnp.mean(dxh * xhat, axis=-1, keepdims=True))
        dz_ref[...] = dz
        dy_ref[...] = (gp_ref[...] * dz).astype(BF16)
        s_ref[0:1, :] += jnp.sum(dxo_t * xhat, axis=0, keepdims=True)
        s_ref[1:2, :] += jnp.sum(dxo_t, axis=0, keepdims=True)
        s_ref[2:3, :] += jnp.sum(dz * yv, axis=0, keepdims=True)

    tile = pl.BlockSpec((tm, D), lambda i: (i, 0))
    return pl.pallas_call(
        kern, name=name, grid=(S // tm,),
        in_specs=[tile, tile, tile, _row_spec(D), _row_spec(D)],
        out_specs=[tile, tile, pl.BlockSpec((8, D), lambda i: (0, 0))],
        out_shape=[jax.ShapeDtypeStruct((S, D), F32), jax.ShapeDtypeStruct((S, D), BF16),
                   jax.ShapeDtypeStruct((8, D), F32)],
        compiler_params=_cp(("arbitrary",)),
    )(dxo, x_in, y, gate1p, ln_g)


def _tri64():
    r = lax.broadcasted_iota(jnp.int32, (CHUNK, CHUNK), 0)
    c = lax.broadcasted_iota(jnp.int32, (CHUNK, CHUNK), 1)
    return r >= c


def _gla_chunk_common(proj_ref, rows, b, h):
    kc = slice(h * GLA_DK_HEAD, (h + 1) * GLA_DK_HEAD)
    bh = b[:, kc]
    ep = jnp.exp(bh)
    en = jnp.exp(-bh)
    bl = bh[CHUNK - 1:CHUNK, :]
    ee = jnp.exp(bl - bh)
    dec = jnp.exp(bl)
    q = proj_ref[rows, h * GLA_DK_HEAD:(h + 1) * GLA_DK_HEAD] * GLA_SCALE
    k = proj_ref[rows, GLA_DK + h * GLA_DK_HEAD:GLA_DK + (h + 1) * GLA_DK_HEAD]
    v = proj_ref[rows, 2 * GLA_DK + h * GLA_DV_HEAD:2 * GLA_DK + (h + 1) * GLA_DV_HEAD]
    g = proj_ref[rows, 2 * GLA_DK + GLA_DV + h * GLA_DV_HEAD:2 * GLA_DK + GLA_DV + (h + 1) * GLA_DV_HEAD]
    return ep, en, ee, dec, q, k, v, g


def gla_fwd(proj, wgk_p, bgk, gnorm, name, side=None):
    S = proj.shape[0]
    TB = min(GLA_TB, S)
    ncb = TB // CHUNK

    def kern(proj_ref, wgk_ref, bgk_ref, gn_ref, zg_ref, st_ref, state_scr, la_scr):
        @pl.when(pl.program_id(0) == 0)
        def _():
            state_scr[...] = jnp.zeros_like(state_scr)
        lr = proj_ref[:, GLA_LR_OFF:GLA_IN_PAD].astype(BF16)
        gk = _dot(lr, wgk_ref[...]) + bgk_ref[...]
        la_scr[...] = _log_sigmoid(gk) * (1.0 / 16.0)
        lower = _tri64()
        tri = lower.astype(F32)

        def chunk(c, carry):
            rows = pl.ds(pl.multiple_of(c * CHUNK, CHUNK), CHUNK)
            b = jnp.dot(tri, la_scr[rows, :], precision=HIGHEST, preferred_element_type=F32)
            H = range(GLA_HEADS)
            cm = [_gla_chunk_common(proj_ref, rows, b, h) for h in H]
            qf = [(m[4] * m[0]).astype(BF16) for m in cm]
            kn = [(m[5] * m[1]).astype(BF16) for m in cm]
            qn = [(m[4] * m[1]).astype(BF16) for m in cm]
            kp = [(m[5] * m[0]).astype(BF16) for m in cm]
            ke = [(m[5] * m[2]).astype(BF16) for m in cm]
            vb = [m[6].astype(BF16) for m in cm]
            st = [state_scr[h] for h in H]
            a_f = [_dot_nt(qf[h], kn[h]) for h in H]
            a_b = [_dot_nt(qn[h], kp[h]) for h in H]
            o_st = [_dot_nt(qf[h], st[h].astype(BF16)) for h in H]
            upd = [_dot_tn(vb[h], ke[h]) for h in H]
            amat = [jnp.where(lower, a_f[h], a_b[h]).astype(BF16) for h in H]
            o = [_dot(amat[h], vb[h]) + o_st[h] for h in H]
            for h in H:
                st_ref[c, h] = st[h]
                state_scr[h] = st[h] * cm[h][3] + upd[h]
            for h in H:
                g = cm[h][7]
                vc = slice(h * GLA_DV_HEAD, (h + 1) * GLA_DV_HEAD)
                r = lax.rsqrt(jnp.mean(o[h] * o[h], axis=-1, keepdims=True) + RMS_EPS)
                on = (o[h] * r) * gn_ref[:, vc]
                zg_ref[rows, vc] = (on * (g * _sigmoid(g))).astype(BF16)
            return carry

        lax.fori_loop(0, ncb, chunk, 0)

    return hosted_call(
        kern, side, name=name, grid=(S // TB,),
        in_specs=[pl.BlockSpec((TB, GLA_IN_PAD), lambda i: (i, 0)),
                  pl.BlockSpec((128, GLA_DK), lambda i: (0, 0)), _row_spec(GLA_DK), _row_spec(GLA_DV)],
        out_specs=[pl.BlockSpec((TB, GLA_DV), lambda i: (i, 0)),
                   pl.BlockSpec((ncb, GLA_HEADS, GLA_DV_HEAD, GLA_DK_HEAD), lambda i: (i, 0, 0, 0))],
        out_shape=[jax.ShapeDtypeStruct((S, GLA_DV), BF16),
                   jax.ShapeDtypeStruct((S // CHUNK, GLA_HEADS, GLA_DV_HEAD, GLA_DK_HEAD), F32)],
        scratch_shapes=[pltpu.VMEM((GLA_HEADS, GLA_DV_HEAD, GLA_DK_HEAD), F32), pltpu.VMEM((TB, GLA_DK), F32)],
        dims=("arbitrary",), args=(proj, wgk_p, bgk, gnorm))


def gla_bwd(proj, states, dzg, wgk_p, bgk, gnorm, name, side=None):
    S = proj.shape[0]
    TB = min(GLA_TB, S)
    ncb = TB // CHUNK
    nb = S // TB

    def kern(proj_ref, st_ref, dzg_ref, wgk_ref, bgk_ref, gn_ref,
             dproj_ref, dwgk_ref, dbgk_ref, dgn_ref, dstate_scr, la_scr, gk_scr, dgk_scr):
        @pl.when(pl.program_id(0) == 0)
        def _():
            dstate_scr[...] = jnp.zeros_like(dstate_scr)
            dwgk_ref[...] = jnp.zeros_like(dwgk_ref)
            dbgk_ref[...] = jnp.zeros_like(dbgk_ref)
            dgn_ref[...] = jnp.zeros_like(dgn_ref)
        lr = proj_ref[:, GLA_LR_OFF:GLA_IN_PAD].astype(BF16)
        gk = _dot(lr, wgk_ref[...]) + bgk_ref[...]
        gk_scr[...] = gk
        la_scr[...] = _log_sigmoid(gk) * (1.0 / 16.0)
        lower = _tri64()
        tri = lower.astype(F32)
        r_i = lax.broadcasted_iota(jnp.int32, (CHUNK, CHUNK), 0)
        c_i = lax.broadcasted_iota(jnp.int32, (CHUNK, CHUNK), 1)
        triu = (c_i >= r_i).astype(F32)
        last_row = lax.broadcasted_iota(jnp.int32, (CHUNK, GLA_DK_HEAD), 0) == CHUNK - 1

        def chunk(cc, carry):
            c = ncb - 1 - cc
            rows = pl.ds(pl.multiple_of(c * CHUNK, CHUNK), CHUNK)
            b = jnp.dot(tri, la_scr[rows, :], precision=HIGHEST, preferred_element_type=F32)
            H = range(GLA_HEADS)
            kcs = [slice(h * GLA_DK_HEAD, (h + 1) * GLA_DK_HEAD) for h in H]
            vcs = [slice(h * GLA_DV_HEAD, (h + 1) * GLA_DV_HEAD) for h in H]
            cm = [_gla_chunk_common(proj_ref, rows, b, h) for h in H]
            ep, en, ee, dec = ([m[i] for m in cm] for i in range(4))
            gs = [m[7] for m in cm]
            qf = [m[4] * m[0] for m in cm]
            kn = [m[5] * m[1] for m in cm]
            qn = [m[4] * m[1] for m in cm]
            kp = [m[5] * m[0] for m in cm]
            ke = [m[5] * m[2] for m in cm]
            qf_b, kn_b, qn_b, kp_b, ke_b = ([t.astype(BF16) for t in ts] for ts in (qf, kn, qn, kp, ke))
            vb = [m[6].astype(BF16) for m in cm]
            st = [st_ref[c, h] for h in H]
            st_b = [s.astype(BF16) for s in st]
            dst = [dstate_scr[h] for h in H]
            dst_b = [s.astype(BF16) for s in dst]
            a_f = [_dot_nt(qf_b[h], kn_b[h]) for h in H]
            a_b = [_dot_nt(qn_b[h], kp_b[h]) for h in H]
            o_st = [_dot_nt(qf_b[h], st_b[h]) for h in H]
            dv_st = [_dot_nt(ke_b[h], dst_b[h]) for h in H]
            dke = [_dot(vb[h], dst_b[h]) for h in H]
            amat = [jnp.where(lower, a_f[h], a_b[h]).astype(BF16) for h in H]
            o = [_dot(amat[h], vb[h]) + o_st[h] for h in H]
            do_b, dgs = [], []
            for h in H:
                r = lax.rsqrt(jnp.mean(o[h] * o[h], axis=-1, keepdims=True) + RMS_EPS)
                oh = o[h] * r
                gn = gn_ref[:, vcs[h]]
                g = gs[h]
                sg = _sigmoid(g)
                dz = dzg_ref[rows, vcs[h]]
                don = dz * (g * sg)
                dgs.append(dz * (oh * gn) * (sg * (1.0 + g * (1.0 - sg))))
                dgn_ref[:, vcs[h]] += jnp.sum(don * oh, axis=0, keepdims=True)
                doh = don * gn
                do_b.append((r * (doh - oh * jnp.mean(doh * oh, axis=-1, keepdims=True))).astype(BF16))
            da = [_dot_nt(do_b[h], vb[h]) for h in H]
            dv = [_dot_tn(amat[h], do_b[h]) + dv_st[h] for h in H]
            dqf_st = [_dot(do_b[h], st_b[h]) for h in H]
            dst_upd = [_dot_tn(do_b[h], qf_b[h]) for h in H]
            da_f = [jnp.where(lower, da[h], 0.0).astype(BF16) for h in H]
            da_b = [jnp.where(lower, 0.0, da[h]).astype(BF16) for h in H]
            dqf = [_dot(da_f[h], kn_b[h]) + dqf_st[h] for h in H]
            dkn = [_dot_tn(da_f[h], qf_b[h]) for h in H]
            dqn = [_dot(da_b[h], kp_b[h]) for h in H]
            dkp = [_dot_tn(da_b[h], qn_b[h]) for h in H]
            dbs = []
            for h in H:
                ddec = jnp.sum(dst[h] * st[h], axis=0, keepdims=True)
                dstate_scr[h] = dst[h] * dec[h] + dst_upd[h]
                db = dqf[h] * qf[h] - dkn[h] * kn[h] - dqn[h] * qn[h] + dkp[h] * kp[h] - dke[h] * ke[h]
                dbl = jnp.sum(dke[h] * ke[h], axis=0, keepdims=True) + ddec * dec[h]
                dbs.append(db + jnp.where(last_row, dbl, 0.0))
            dla = [jnp.dot(triu, dbs[h], precision=HIGHEST, preferred_element_type=F32) for h in H]
            for h in H:
                dq = (dqf[h] * ep[h] + dqn[h] * en[h]) * GLA_SCALE
                dk = dkn[h] * en[h] + dkp[h] * ep[h] + dke[h] * ee[h]
                dgk_scr[rows, kcs[h]] = dla[h] * (1.0 / 16.0) * _sigmoid(-gk_scr[rows, kcs[h]])
                dproj_ref[rows, kcs[h]] = dq.astype(BF16)
                dproj_ref[rows, GLA_DK + h * GLA_DK_HEAD:GLA_DK + (h + 1) * GLA_DK_HEAD] = dk.astype(BF16)
                dproj_ref[rows, 2 * GLA_DK + h * GLA_DV_HEAD:2 * GLA_DK + (h + 1) * GLA_DV_HEAD] = dv[h].astype(BF16)
                dproj_ref[rows, 2 * GLA_DK + GLA_DV + h * GLA_DV_HEAD:
                          2 * GLA_DK + GLA_DV + (h + 1) * GLA_DV_HEAD] = dgs[h].astype(BF16)
            return carry

        lax.fori_loop(0, ncb, chunk, 0)
        dgk = dgk_scr[...]
        dgk_b = dgk.astype(BF16)
        dproj_ref[:, GLA_LR_OFF:GLA_IN_PAD] = _dot_nt(dgk_b, wgk_ref[...]).astype(BF16)
        dwgk_ref[...] += _dot_tn(lr, dgk_b)
        dbgk_ref[...] += jnp.sum(dgk, axis=0, keepdims=True)

    rev = lambda i: (nb - 1 - i, 0)
    return hosted_call(
        kern, side, name=name, grid=(nb,),
        in_specs=[pl.BlockSpec((TB, GLA_IN_PAD), rev),
                  pl.BlockSpec((ncb, GLA_HEADS, GLA_DV_HEAD, GLA_DK_HEAD), lambda i: (nb - 1 - i, 0, 0, 0)),
                  pl.BlockSpec((TB, GLA_DV), rev),
                  pl.BlockSpec((128, GLA_DK), lambda i: (0, 0)), _row_spec(GLA_DK), _row_spec(GLA_DV)],
        out_specs=[pl.BlockSpec((TB, GLA_IN_PAD), rev),
                   pl.BlockSpec((128, GLA_DK), lambda i: (0, 0)), _row_spec(GLA_DK), _row_spec(GLA_DV)],
        out_shape=[jax.ShapeDtypeStruct((S, GLA_IN_PAD), BF16), jax.ShapeDtypeStruct((128, GLA_DK), F32),
                   jax.ShapeDtypeStruct((1, GLA_DK), F32), jax.ShapeDtypeStruct((1, GLA_DV), F32)],
        scratch_shapes=[pltpu.VMEM((GLA_HEADS, GLA_DV_HEAD, GLA_DK_HEAD), F32), pltpu.VMEM((TB, GLA_DK), F32),
                        pltpu.VMEM((TB, GLA_DK), F32), pltpu.VMEM((TB, GLA_DK), F32)],
        dims=("arbitrary",), args=(proj, states, dzg, wgk_p, bgk, gnorm))


ATT_TW = 1024
ATT_CLASSES = 3


def _att_window(i):
    return pl.multiple_of(jnp.maximum(i * ATT_TQ - LEFT_CHUNKS * CHUNK, 0), ATT_TQ)


def _att_rel_index():
    e = jnp.arange(ATT_TW)[None, :]
    d = jnp.where(e < ATT_KW, e, e - ATT_TW)
    off = (jnp.arange(ATT_CLASSES) * ATT_TQ)[:, None]
    return jnp.clip(off - d, -MAX_REL, MAX_REL) + MAX_REL


def _row_bits():
    return lax.broadcasted_iota(jnp.int32, (ATT_TQ, ATT_TW), 0)


def att_bias_tiles(rel_bias, name):
    pick = (jnp.arange(384)[:, None] == _att_rel_index().reshape(-1)[None, :]).astype(F32)
    tab = mm_f32(jnp.pad(rel_bias, ((0, 0), (0, 384 - N_REL))), pick, name + "_tab")
    tab = tab.reshape(ATT_HEADS * ATT_CLASSES, 1, ATT_TW)

    def kern(t_ref, o_ref):
        cls = pl.program_id(0) % ATT_CLASSES
        x = jnp.broadcast_to(t_ref[...], (ATT_TQ, ATT_TW))
        row = _row_bits()
        for b in range(8):
            x = jnp.where((row & (1 << b)) != 0, pltpu.roll(x, 1 << b, axis=1), x)
        x = x[:, :ATT_KW]
        qc = cls * (ATT_TQ // CHUNK) + lax.shift_right_arithmetic(
            lax.broadcasted_iota(jnp.int32, (ATT_TQ, ATT_KW), 0), 6)
        kc = lax.shift_right_arithmetic(lax.broadcasted_iota(jnp.int32, (ATT_TQ, ATT_KW), 1), 6)
        o_ref[...] = jnp.where((kc <= qc) & (kc >= qc - LEFT_CHUNKS), x, NEG_INF)

    return pl.pallas_call(
        kern, name=name, grid=(ATT_HEADS * ATT_CLASSES,),
        in_specs=[pl.BlockSpec((None, 1, ATT_TW), lambda i: (i, 0, 0))],
        out_specs=pl.BlockSpec((None, ATT_TQ, ATT_KW), lambda i: (i, 0, 0)),
        out_shape=jax.ShapeDtypeStruct((ATT_HEADS * ATT_CLASSES, ATT_TQ, ATT_KW), F32),
        compiler_params=_cp(("parallel",)),
    )(tab)


def att_bias_grad(dbt, name):
    def kern(d_ref, o_ref):
        x = jnp.concatenate([d_ref[...], jnp.zeros((ATT_TQ, ATT_TW - ATT_KW), F32)], axis=1)
        row = _row_bits()
        for b in range(8):
            x = jnp.where((row & (1 << b)) != 0, pltpu.roll(x, ATT_TW - (1 << b), axis=1), x)
        o_ref[...] = jnp.sum(x, axis=0, keepdims=True)

    diag = pl.pallas_call(
        kern, name=name + "_diag", grid=(ATT_HEADS * ATT_CLASSES,),
        in_specs=[pl.BlockSpec((None, ATT_TQ, ATT_KW), lambda i: (i, 0, 0))],
        out_specs=pl.BlockSpec((None, 1, ATT_TW), lambda i: (i, 0, 0)),
        out_shape=jax.ShapeDtypeStruct((ATT_HEADS * ATT_CLASSES, 1, ATT_TW), F32),
        compiler_params=_cp(("parallel",)),
    )(dbt)
    diag = diag.reshape(ATT_HEADS, ATT_CLASSES * ATT_TW)
    onehot = (_att_rel_index().reshape(-1)[:, None] == jnp.arange(384)[None, :]).astype(F32)
    return mm_f32(diag, onehot, name + "_bins")[:, :N_REL]


def _att_scores(q_ref, kw, bias_ref):
    hs = [slice(hh * ATT_HD, (hh + 1) * ATT_HD) for hh in range(2)]
    q = [q_ref[:, h] * ATT_SCALE for h in hs]
    k = [kw[:, h] for h in hs]
    s = [_dot_nt(q[hh], k[hh]) + bias_ref[hh] for hh in range(2)]
    e = [jnp.exp(t - jnp.max(t, axis=-1, keepdims=True)) for t in s]
    inv = [1.0 / jnp.sum(t, axis=-1, keepdims=True) for t in e]
    return hs, q, k, e, inv


def _att_specs(S):
    nq = D_MODEL // 128
    q_spec = pl.BlockSpec((ATT_TQ, 128), lambda p, i: (i, p))
    k_spec = pl.BlockSpec((S, 128), lambda p, i: (0, nq + p))
    v_spec = pl.BlockSpec((S, 128), lambda p, i: (0, 2 * nq + p))
    b_spec = pl.BlockSpec((2, None, ATT_TQ, ATT_KW), lambda p, i: (p, jnp.minimum(i, ATT_CLASSES - 1), 0, 0))
    return q_spec, k_spec, v_spec, b_spec


def attn_fwd(qkv, bias, name, side=None):
    S = qkv.shape[0]
    q_spec, k_spec, v_spec, b_spec = _att_specs(S)

    def kern(q_ref, k_ref, v_ref, bias_ref, o_ref):
        ws = _att_window(pl.program_id(1))
        kw = k_ref[pl.ds(ws, ATT_KW), :]
        vw = v_ref[pl.ds(ws, ATT_KW), :]
        hs, _, _, e, inv = _att_scores(q_ref, kw, bias_ref)
        outs = [_dot(e[hh].astype(BF16), vw[:, hs[hh]]) * inv[hh] for hh in range(2)]
        o_ref[...] = jnp.concatenate(outs, axis=1).astype(BF16)

    return hosted_call(
        kern, side, name=name, grid=(ATT_HEADS // 2, S // ATT_TQ),
        in_specs=[q_spec, k_spec, v_spec, b_spec],
        out_specs=[pl.BlockSpec((ATT_TQ, 128), lambda p, i: (i, p))],
        out_shape=[jax.ShapeDtypeStruct((S, D_MODEL), BF16)],
        scratch_shapes=[], dims=("parallel", "arbitrary"), args=(qkv, qkv, qkv, bias))


def attn_bwd(qkv, bias, do, name, side=None):
    S = qkv.shape[0]
    nblk = S // ATT_TQ
    q_spec, k_spec, v_spec, b_spec = _att_specs(S)

    def kern(q_ref, k_ref, v_ref, bias_ref, do_ref, dqkv_ref, db_ref, dk_scr, dv_scr):
        i = pl.program_id(1)

        @pl.when(i == 0)
        def _():
            dk_scr[...] = jnp.zeros_like(dk_scr)
            dv_scr[...] = jnp.zeros_like(dv_scr)
            db_ref[...] = jnp.zeros_like(db_ref)
        ws = _att_window(i)
        win = pl.ds(ws, ATT_KW)
        kw = k_ref[win, :]
        vw = v_ref[win, :]
        o_cls = jnp.minimum(i, ATT_CLASSES - 1)
        R2 = range(2)
        hs, q, k, e, inv = _att_scores(q_ref, kw, bias_ref)
        do_h = [do_ref[:, h] for h in hs]
        dp = [_dot_nt(do_h[hh], vw[:, hs[hh]]) for hh in R2]
        p = [e[hh] * inv[hh] for hh in R2]
        dvs = [_dot_tn(p[hh].astype(BF16), do_h[hh]) for hh in R2]
        ds = [p[hh] * (dp[hh] - jnp.sum(p[hh] * dp[hh], axis=-1, keepdims=True)) for hh in R2]
        ds_b = [t.astype(BF16) for t in ds]
        dqs = [_dot(ds_b[hh], k[hh]) * ATT_SCALE for hh in R2]
        dks = [_dot_tn(ds_b[hh], q[hh]) for hh in R2]
        for hh in R2:
            db_ref[hh, o_cls] += ds[hh]
        dqkv_ref[0, pl.ds(pl.multiple_of(i * ATT_TQ, ATT_TQ), ATT_TQ), :] = jnp.concatenate(dqs, axis=1).astype(BF16)
        dk_scr[win, :] += jnp.concatenate(dks, axis=1)
        dv_scr[win, :] += jnp.concatenate(dvs, axis=1)

        @pl.when(i == nblk - 1)
        def _():
            dqkv_ref[1] = dk_scr[...].astype(BF16)
            dqkv_ref[2] = dv_scr[...].astype(BF16)

    return hosted_call(
        kern, side, name=name, grid=(ATT_HEADS // 2, nblk),
        in_specs=[q_spec, k_spec, v_spec, b_spec, pl.BlockSpec((ATT_TQ, 128), lambda p, i: (i, p))],
        out_specs=[pl.BlockSpec((3, S, 128), lambda p, i: (0, 0, p)),
                   pl.BlockSpec((2, ATT_CLASSES, ATT_TQ, ATT_KW), lambda p, i: (p, 0, 0, 0))],
        out_shape=[jax.ShapeDtypeStruct((3, S, D_MODEL), BF16),
                   jax.ShapeDtypeStruct((ATT_HEADS, ATT_CLASSES, ATT_TQ, ATT_KW), F32)],
        scratch_shapes=[pltpu.VMEM((S, 128), F32), pltpu.VMEM((S, 128), F32)],
        dims=("parallel", "arbitrary"), args=(qkv, qkv, qkv, bias, do))


def colsum3(a3, name):
    P, S, N = a3.shape
    tm = min(512, S)

    def kern(a_ref, o_ref):
        @pl.when(pl.program_id(1) == 0)
        def _():
            o_ref[...] = jnp.zeros_like(o_ref)
        o_ref[...] += jnp.sum(a_ref[...].astype(F32), axis=0, keepdims=True)

    return pl.pallas_call(
        kern, name=name, grid=(P, S // tm),
        in_specs=[pl.BlockSpec((None, tm, N), lambda p, i: (p, i, 0))],
        out_specs=pl.BlockSpec((None, 1, N), lambda p, i: (p, 0, 0)),
        out_shape=jax.ShapeDtypeStruct((P, 1, N), F32),
        compiler_params=_cp(("parallel", "arbitrary")),
    )(a3)


def _me():
    return lax.axis_index("x"), lax.axis_index("y"), lax.axis_index("c")


def _other_chips(x, y):
    return [(1 - x, y), (x, 1 - y), (1 - x, 1 - y)]


def all_gather8(x_shard, name):
    m_per, n = x_shard.shape

    def body(x_ref, out_ref, send_sems, recv_sems, local_sem):
        x, y, c = _me()
        me, sibling = (x, y, c), (x, y, 1 - c)
        chips = _other_chips(x, y)

        def rows(px, py, pc):
            return out_ref.at[pl.ds((4 * px + 2 * py + pc) * m_per, m_per), :]

        def copy(k, block, to, src=None):
            return pltpu.make_async_remote_copy(
                src_ref=rows(*block) if src is None else src, dst_ref=rows(*block),
                send_sem=send_sems.at[k], recv_sem=recv_sems.at[k], device_id=to, device_id_type=MESH)

        mine = pltpu.make_async_copy(x_ref, rows(*me), local_sem)
        mine.start()
        first = [copy(0, me, sibling, src=x_ref)]
        first += [copy(1 + j, me, (*chip, c), src=x_ref) for j, chip in enumerate(chips)]
        for cp in first:
            cp.start()
        passed = [copy(4 + j, (*chip, c), sibling) for j, chip in enumerate(chips)]
        for j, chip in enumerate(chips):
            copy(1 + j, (*chip, c), me).wait_recv()
            passed[j].start()
        copy(0, sibling, me).wait_recv()
        for j, chip in enumerate(chips):
            copy(4 + j, (*chip, 1 - c), me).wait_recv()
        for cp in first + passed:
            cp.wait_send()
        mine.wait()

    return pl.pallas_call(
        body, name=name,
        out_shape=jax.ShapeDtypeStruct((N_DEV * m_per, n), x_shard.dtype),
        in_specs=[pl.BlockSpec(memory_space=pltpu.VMEM)],
        out_specs=pl.BlockSpec(memory_space=pltpu.VMEM),
        scratch_shapes=[pltpu.SemaphoreType.DMA((7,)), pltpu.SemaphoreType.DMA((7,)), pltpu.SemaphoreType.DMA],
        compiler_params=pltpu.CompilerParams(vmem_limit_bytes=VMEM_LIMIT),
    )(x_shard)


def _half_rows(n_rows, c):
    h = n_rows // 2
    return pl.ds(c * h, h)


def _gathered_shape(shape, flavour):
    L, a, b = shape
    return {"col": (L, a, N_CHIPS * b), "row": (L, N_CHIPS * a, b), "lead": (N_CHIPS, L, a, b)}[flavour]


def _gathered_part(out_ref, shape, flavour, s, rows):
    L, a, b = shape
    if flavour == "col":
        return out_ref.at[:, rows, pl.ds(s * b, b)]
    if flavour == "row":
        return out_ref.at[:, pl.ds(s * a + rows.start, rows.size), :]
    return out_ref.at[s, :, rows, :]


def gather_side(shards, flavours):
    n = len(shards)
    shapes = [w.shape for w in shards]

    def copies(w_refs, out_refs, send_sems, recv_sems, local_sems):
        x, y, c = _me()
        sibling = (x, y, 1 - c)
        chips = _other_chips(x, y)
        me_s = 2 * x + y

        def copy(k, src, dst, to):
            return pltpu.make_async_remote_copy(src_ref=src, dst_ref=dst, send_sem=send_sems.at[k],
                                                recv_sem=recv_sems.at[k], device_id=to, device_id_type=MESH)

        own, first, landed, passed, passed_in = [], [], [], [], []
        for w in range(n):
            shp, fl = shapes[w], flavours[w]
            my_half = _half_rows(shp[1], c)
            sib_half = _half_rows(shp[1], 1 - c)
            own.append(copy(7 * w + 6, w_refs[w], _gathered_part(out_refs[w], shp, fl, me_s, pl.ds(0, shp[1])), sibling))
            for j, chip in enumerate(chips):
                s = 2 * chip[0] + chip[1]
                first.append(copy(7 * w + j, w_refs[w].at[:, my_half, :],
                                  _gathered_part(out_refs[w], shp, fl, me_s, my_half), (*chip, c)))
                part = _gathered_part(out_refs[w], shp, fl, s, my_half)
                landed.append(copy(7 * w + j, part, part, (*chip, c)))
                passed.append(copy(7 * w + 3 + j, part, part, sibling))
                theirs = _gathered_part(out_refs[w], shp, fl, s, sib_half)
                passed_in.append(copy(7 * w + 3 + j, theirs, theirs, sibling))
        return own, first, landed, passed, passed_in

    def start(*refs):
        own, first, _, _, _ = copies(*refs)
        for cp in first + own:
            cp.start()

    def wait(*refs):
        own, first, landed, passed, passed_in = copies(*refs)
        for arrived, onward in zip(landed, passed):
            arrived.wait_recv()
            onward.start()
        for cp in passed_in:
            cp.wait_recv()
        for cp in own:
            cp.wait()
        for cp in first + passed:
            cp.wait_send()

    out_shapes = [jax.ShapeDtypeStruct(_gathered_shape(s, f), w.dtype) for w, s, f in zip(shards, shapes, flavours)]
    return Side(shards, out_shapes, 7 * n, 0, start, wait)


def swap_halves(gs, name):
    n = len(gs)

    def body(*refs):
        g_refs, out_refs = refs[:n], refs[n:2 * n]
        send_sems, recv_sems = refs[2 * n:]
        x, y, c = _me()
        cps = [pltpu.make_async_remote_copy(
            src_ref=g_refs[w].at[:, _half_rows(gs[w].shape[1], 1 - c), :], dst_ref=out_refs[w],
            send_sem=send_sems.at[w], recv_sem=recv_sems.at[w], device_id=(x, y, 1 - c), device_id_type=MESH)
            for w in range(n)]
        for cp in cps:
            cp.start()
        for cp in cps:
            cp.wait()

    any_spec = pl.BlockSpec(memory_space=pl.ANY)
    return pl.pallas_call(
        body, name=name,
        out_shape=[jax.ShapeDtypeStruct((g.shape[0], g.shape[1] // 2, g.shape[2]), g.dtype) for g in gs],
        in_specs=[any_spec] * n, out_specs=[any_spec] * n,
        scratch_shapes=[pltpu.SemaphoreType.DMA((n,)), pltpu.SemaphoreType.DMA((n,))],
    )(*gs)


def add_half(g, r1, c_idx, name):
    n, R, C = g.shape
    half = R // 2
    tr = _rows_block(half, C)
    nbh = half // tr

    def kern(c_ref, g_ref, r_ref, o_ref):
        o_ref[...] = g_ref[...] + r_ref[...]

    return pl.pallas_call(
        kern, name=name,
        grid_spec=pltpu.PrefetchScalarGridSpec(
            num_scalar_prefetch=1, grid=(n, nbh),
            in_specs=[pl.BlockSpec((1, tr, C), lambda d, r, c_ref: (d, c_ref[0] * nbh + r, 0)),
                      pl.BlockSpec((1, tr, C), lambda d, r, c_ref: (d, r, 0))],
            out_specs=pl.BlockSpec((1, tr, C), lambda d, r, c_ref: (d, r, 0))),
        out_shape=jax.ShapeDtypeStruct((n, half, C), F32),
        compiler_params=_cp(("parallel", "parallel")),
    )(c_idx, g, r1)


def exchange_side(ps):
    n = len(ps)

    def copies(p_refs, out_refs, send_sems, recv_sems, local_sems):
        x, y, c = _me()
        return [pltpu.make_async_remote_copy(
            src_ref=p_refs[w].at[2 * chip[0] + chip[1]], dst_ref=out_refs[w].at[j],
            send_sem=send_sems.at[3 * w + j], recv_sem=recv_sems.at[3 * w + j],
            device_id=(*chip, c), device_id_type=MESH)
            for w in range(n) for j, chip in enumerate(_other_chips(x, y))]

    def start(*refs):
        for cp in copies(*refs):
            cp.start()

    def wait(*refs):
        for cp in copies(*refs):
            cp.wait()

    return Side(ps, [jax.ShapeDtypeStruct((3,) + p.shape[1:], p.dtype) for p in ps], 3 * n, 0, start, wait)


def add_chips(p, r2, chip_idx, name):
    n, H, C = p.shape
    tr = _rows_block(H, C)

    def kern(s_ref, p_ref, r_ref, o_ref):
        o_ref[...] = ((p_ref[0] + r_ref[0]) + r_ref[1]) + r_ref[2]

    return pl.pallas_call(
        kern, name=name,
        grid_spec=pltpu.PrefetchScalarGridSpec(
            num_scalar_prefetch=1, grid=(H // tr,),
            in_specs=[pl.BlockSpec((1, tr, C), lambda r, s_ref: (s_ref[0], r, 0)),
                      pl.BlockSpec((3, tr, C), lambda r, s_ref: (0, r, 0))],
            out_specs=pl.BlockSpec((tr, C), lambda r, s_ref: (r, 0))),
        out_shape=jax.ShapeDtypeStruct((H, C), F32),
        compiler_params=_cp(("parallel",)),
    )(chip_idx, p, r2)


def swap_reduced(ss, name):
    n = len(ss)

    def body(*refs):
        s_refs, out_refs = refs[:n], refs[n:2 * n]
        send_sems, recv_sems = refs[2 * n:]
        x, y, c = _me()
        cps = [pltpu.make_async_remote_copy(src_ref=s_refs[w], dst_ref=out_refs[w], send_sem=send_sems.at[w],
                                            recv_sem=recv_sems.at[w], device_id=(x, y, 1 - c), device_id_type=MESH)
               for w in range(n)]
        for cp in cps:
            cp.start()
        for cp in cps:
            cp.wait()

    any_spec = pl.BlockSpec(memory_space=pl.ANY)
    return pl.pallas_call(
        body, name=name, out_shape=[jax.ShapeDtypeStruct(s.shape, s.dtype) for s in ss],
        in_specs=[any_spec] * n, out_specs=[any_spec] * n,
        scratch_shapes=[pltpu.SemaphoreType.DMA((n,)), pltpu.SemaphoreType.DMA((n,))],
    )(*ss)


BIG = (("gla_w_in", 2, (1024, GLA_IN // N_CHIPS), "lead"), ("gla_w_out", 2, (256, 1024), "row"),
       ("att_w_in", 2, (1024, 768), "col"), ("att_w_out", 2, (256, 1024), "row"),
       ("ff_w1", 4, (1024, 1024), "col"), ("ff_w2", 4, (1024, 1024), "row"))
FLAVOUR = {n: f for n, _, _, f in BIG}


def layer_weights(i):
    mixer = "gla" if i % 2 == 0 else "att"
    return (("in", mixer + "_w_in", i // 2), ("out", mixer + "_w_out", i // 2), ("w1", "ff_w1", i), ("w2", "ff_w2", i))


class Comm:
    def __init__(self, weights, core, chip):
        self.weights, self.core, self.chip = weights, core, chip
        self.c_idx = jnp.reshape(core, (1,)).astype(jnp.int32)
        self.chip_idx = jnp.reshape(chip, (1,)).astype(jnp.int32)
        self.reduced = {}

    def gather(self, items):
        shards = [self.weights[n][l:l + 1].astype(BF16) for _, n, l in items]
        return gather_side(shards, [FLAVOUR[n] for _, n, _ in items])

    def full_weights(self, items, gathered):
        W = {}
        for (role, n, _), w in zip(items, gathered):
            if n == "gla_w_in":
                w = jnp.pad(w.transpose(1, 2, 0, 3).reshape(1, D_MODEL, GLA_IN), ((0, 0), (0, 0), (0, GLA_IN_PAD - GLA_IN)))
            W[role] = (w, 0)
        return W

    def gather_now(self, items, name):
        return self.full_weights(items, run_side(self.gather(items), name))

    def reduce_begin(self, tag, items):
        grads = [g for _, _, g in items]
        r1 = swap_halves(grads, f"rs_swap_{tag}")
        ps = [add_half(g, r, self.c_idx, f"rs_add2_{tag}_{w}") for w, (g, r) in enumerate(zip(grads, r1))]
        return tag, [(n, l) for n, l, _ in items], ps

    def exchange(self, pending):
        return exchange_side(pending[2])

    def reduce_mid(self, pending, landed):
        tag, keys, ps = pending
        for w, (key, p, r) in enumerate(zip(keys, ps, landed)):
            self.reduced[key] = add_chips(p, r, self.chip_idx, f"rs_add4_{tag}_{w}")

    def reduce_last(self, pending):
        self.reduce_mid(pending, run_side(self.exchange(pending), f"rs_xchg_{pending[0]}"))

    def reduce_end(self):
        keys = [(n, l) for n, L, _, _ in BIG for l in range(L)]
        mine = [self.reduced[k] for k in keys]
        theirs = swap_reduced(mine, "rs_join")
        low = self.core == 0
        full = {k: jnp.concatenate([jnp.where(low, m, t), jnp.where(low, t, m)], axis=0)
                for k, m, t in zip(keys, mine, theirs)}
        return {n: jnp.stack([full[(n, l)] for l in range(L)]) for n, L, _, _ in BIG}


def local_step(x, target, mods, comm, small):
    S, D = x.shape
    row = lambda v: v.reshape(1, -1)
    saved = []
    tiles = [att_bias_tiles(small["att_rel_bias"][j], f"att_tiles_{j}").reshape(ATT_HEADS, ATT_CLASSES, ATT_TQ, ATT_KW)
             for j in range(2)]
    wgk_p = [jnp.pad(small["gla_w_gk2"][j], ((0, 128 - GLA_RANK), (0, 0))).astype(BF16) for j in range(2)]

    u1 = modulate(x, row(mods[0, 1]), row(mods[0, 0]), "mod_first")
    Ws = [dict() for _ in range(DEPTH)]
    items0 = layer_weights(0)
    Ws[0].update(comm.gather_now(items0[:1], "gather_w0"))
    for i in range(DEPTH):
        j = i // 2
        W = Ws[i]
        sh1, sc1, g1, sh2, sc2, g2 = (row(mods[i, k]) for k in range(6))
        nxt = min(i + 1, DEPTH - 1)
        more = i + 1 < DEPTH
        nxt_items = layer_weights(nxt)
        side_in = comm.gather(items0[1:]) if i == 0 else None
        side_mix = comm.gather(nxt_items[:2]) if more else None
        side_ff = comm.gather(nxt_items[2:]) if more else None
        if i % 2 == 0:
            proj = mm_plain(u1, *W["in"], f"gla_in_{i}", side=side_in)
        else:
            proj = mm_plain(u1, *W["in"], f"att_in_{i}", mode="bf16", bias=row(small["att_b_in"][j]), side=side_in)
        proj, landed = proj if side_in is not None else (proj, [])
        if i == 0:
            W.update(comm.full_weights(items0[1:], landed))
        if i % 2 == 0:
            (zmix, states), landed = gla_fwd(proj, wgk_p[j], row(small["gla_b_gk"][j]), row(small["gla_g_norm"][j]),
                                             f"gla_fwd_{i}", side_mix)
        else:
            (zmix,), landed = attn_fwd(proj, tiles[j], f"att_fwd_{i}", side_mix)
            states = None
        if more:
            Ws[nxt].update(comm.full_weights(nxt_items[:2], landed))
        y1, x_mid, u2 = mm_down_ln(zmix, *W["out"], x, 1.0 + g1, row(small["ln_g"][i, 0]), row(small["ln_b"][i, 0]),
                                   sc2, sh2, f"mix_out_{i}")
        act = mm_plain(u2, *W["w1"], f"ff_up_{i}", mode="mlp_up", side=side_ff)
        act, landed = act if side_ff is not None else (act, [])
        if more:
            Ws[nxt].update(comm.full_weights(nxt_items[2:], landed))
        y2, x_out, u_next = mm_down_ln(act, *W["w2"], x_mid, 1.0 + g2, row(small["ln_g"][i, 1]),
                                       row(small["ln_b"][i, 1]), row(mods[nxt, 1]), row(mods[nxt, 0]), f"ff_out_{i}")
        saved.append(dict(x_in=x, u1=u1, proj=proj, zmix=zmix, states=states, y1=y1, x_mid=x_mid, u2=u2,
                          act=act, y2=y2))
        x, u1 = x_out, u_next

    dx, sq = loss_head(x, target, "loss_head")

    g_small = dict(ln_g=[None] * DEPTH, ln_b=[None] * DEPTH, gla_w_gk2=[None] * 2, gla_b_gk=[None] * 2,
                   gla_g_norm=[None] * 2, att_b_in=[None] * 2, att_rel_bias=[None] * 2)
    dmods = [None] * DEPTH
    later = []

    for i in reversed(range(DEPTH)):
        j = i // 2
        sv = saved[i]
        W = Ws[i]
        sh1, sc1, g1, sh2, sc2, g2 = (row(mods[i, k]) for k in range(6))
        dz2, dy2, s_ln2 = ln_bwd(dx, sv["x_mid"], sv["y2"], 1.0 + g2, row(small["ln_g"][i, 1]), f"ln2_bwd_{i}")
        dh = mm_plain(dy2, *W["w2"], f"ff_dn_{i}", mode="mlp_dn", nt=True, h=sv["act"])
        g_w2 = mm_w(sv["act"], dy2, f"ff_w2g_{i}").reshape(N_CHIPS, D_FF // N_CHIPS, D)
        g_w1 = mm_w(sv["u2"], dh, f"ff_w1g_{i}", chips_out=True)
        dx_mid, s_m2 = mm_down_comb(dh, *W["w1"], dz2, sv["x_mid"], 1.0 + sc2, f"ff_dx_{i}")
        dz1, dy1, s_ln1 = ln_bwd(dx_mid, sv["x_in"], sv["y1"], 1.0 + g1, row(small["ln_g"][i, 0]), f"ln1_bwd_{i}")
        pending = comm.reduce_begin(i, [("ff_w1", i, g_w1), ("ff_w2", i, g_w2)] + later)
        side = comm.exchange(pending)
        mixer = "gla" if i % 2 == 0 else "att"
        if i % 2 == 0:
            g_out = mm_w(sv["zmix"], dy1, f"gla_wog_{i}").reshape(N_CHIPS, D // N_CHIPS, D)
            dzg = mm_plain(dy1, *W["out"], f"gla_dz_{i}", nt=True)
            (dproj, dwgk, dbgk, dgn), landed = gla_bwd(sv["proj"], sv["states"], dzg, wgk_p[j],
                                                       row(small["gla_b_gk"][j]), row(small["gla_g_norm"][j]),
                                                       f"gla_bwd_{i}", side)
            g_small["gla_w_gk2"][j] = dwgk[:GLA_RANK]
            g_small["gla_b_gk"][j] = dbgk[0]
            g_small["gla_g_norm"][j] = dgn[0].reshape(GLA_HEADS, GLA_DV_HEAD)
            gwi = mm_w(sv["u1"], dproj, f"gla_wig_{i}")[:, :GLA_IN]
            g_in = gwi.reshape(D, N_CHIPS, GLA_IN // N_CHIPS).transpose(1, 0, 2)
            dx, s_m1 = mm_down_comb(dproj, *W["in"], dz1, sv["x_in"], 1.0 + sc1, f"mix_dx_{i}")
        else:
            g_out = mm_w(sv["zmix"], dy1, f"att_wog_{i}").reshape(N_CHIPS, D // N_CHIPS, D)
            do = mm_plain(dy1, *W["out"], f"att_do_{i}", mode="bf16", nt=True)
            (dqkv, dbt), landed = attn_bwd(sv["proj"], tiles[j], do, f"att_bwd_{i}", side)
            g_small["att_rel_bias"][j] = att_bias_grad(dbt.reshape(ATT_HEADS * ATT_CLASSES, ATT_TQ, ATT_KW),
                                                       f"att_bias_{i}")
            g_small["att_b_in"][j] = colsum3(dqkv, f"att_bin_{i}").reshape(3 * D)
            g_in = mm_w_chips3(sv["u1"], dqkv, f"att_wig_{i}")
            dx, s_m1 = mm_down_comb(dqkv, *W["in"], dz1, sv["x_in"], 1.0 + sc1, f"mix_dx_{i}", parts=3)
        comm.reduce_mid(pending, landed)
        later = [(mixer + "_w_in", j, g_in), (mixer + "_w_out", j, g_out)]
        g_small["ln_g"][i] = jnp.stack([s_ln1[0], s_ln2[0]])
        g_small["ln_b"][i] = jnp.stack([s_ln1[1], s_ln2[1]])
        dmods[i] = jnp.stack([s_m1[1], s_m1[0], s_ln1[2], s_m2[1], s_m2[0], s_ln2[2]])
    comm.reduce_last(comm.reduce_begin("last", later))

    g_small = {n: jnp.stack(v) for n, v in g_small.items()}
    return sq, dx, jnp.stack(dmods), g_small


SMALL_SHARDED = (("ln_g", (4, 2, 256)), ("ln_b", (4, 2, 256)), ("gla_g_norm", (2, 4, 64)),
                 ("gla_w_gk2", (2, 16, 128)), ("att_b_in", (2, 768)))
SMALL_FULL = dict(ln_g=(4, 2, 1024), ln_b=(4, 2, 1024), gla_g_norm=(2, 4, 256), gla_w_gk2=(2, 16, 512),
                  att_b_in=(2, 3072), gla_b_gk=(2, 512), att_rel_bias=(2, 16, 257))
SMALL_GRAD_ORDER = ("ln_g", "ln_b", "gla_g_norm", "gla_w_gk2", "att_b_in", "gla_b_gk", "att_rel_bias")


def _pack_small(arrs, rows_total):
    parts = []
    for a in arrs:
        flat = a.reshape(-1)
        pad = (-flat.shape[0]) % PACK_W
        parts.append(jnp.pad(flat, (0, pad)).reshape(-1, PACK_W))
    buf = jnp.concatenate(parts, axis=0)
    return jnp.pad(buf, ((0, rows_total - buf.shape[0]), (0, 0)))


def _unpack_small(buf, shapes):
    out, r = [], 0
    for shp in shapes:
        n = 1
        for s in shp:
            n *= s
        nr = (n + PACK_W - 1) // PACK_W
        out.append(buf[..., r:r + nr, :].reshape(buf.shape[:-2] + (nr * PACK_W,))[..., :n].reshape(buf.shape[:-2] + shp))
        r += nr
    return out


def _unshard_last(g4):
    nd = g4.ndim
    perm = tuple(range(1, nd - 1)) + (0, nd - 1)
    t = g4.transpose(perm)
    return t.reshape(t.shape[:-2] + (-1,))


def _shard_last(full, s):
    n = full.shape[-1] // N_CHIPS
    return lax.dynamic_slice_in_dim(full, s * n, n, axis=full.ndim - 1)


WEIGHT_NAMES = ("w_ada", "b_ada", "ln_g", "ln_b", "gla_w_in", "gla_w_gk2", "gla_b_gk", "gla_g_norm", "gla_w_out",
                "att_w_in", "att_b_in", "att_rel_bias", "att_w_out", "ff_w1", "ff_w2")


def kernel(x, c, w_ada, b_ada, ln_g, ln_b, gla_w_in, gla_w_gk2, gla_b_gk, gla_g_norm, gla_w_out, att_w_in, att_b_in, att_rel_bias, att_w_out, ff_w1, ff_w2, loss_target, m_w_ada, m_b_ada, m_ln_g, m_ln_b, m_gla_w_in, m_gla_w_gk2, m_gla_b_gk, m_gla_g_norm, m_gla_w_out, m_att_w_in, m_att_b_in, m_att_rel_bias, m_att_w_out, m_ff_w1, m_ff_w2, v_w_ada, v_b_ada, v_ln_g, v_ln_b, v_gla_w_in, v_gla_w_gk2, v_gla_b_gk, v_gla_g_norm, v_gla_w_out, v_att_w_in, v_att_b_in, v_att_rel_bias, v_att_w_out, v_ff_w1, v_ff_w2):
    weights = dict(w_ada=w_ada, b_ada=b_ada, ln_g=ln_g, ln_b=ln_b, gla_w_in=gla_w_in, gla_w_gk2=gla_w_gk2,
                   gla_b_gk=gla_b_gk, gla_g_norm=gla_g_norm, gla_w_out=gla_w_out, att_w_in=att_w_in,
                   att_b_in=att_b_in, att_rel_bias=att_rel_bias, att_w_out=att_w_out, ff_w1=ff_w1, ff_w2=ff_w2)
    mom1 = dict(w_ada=m_w_ada, b_ada=m_b_ada, ln_g=m_ln_g, ln_b=m_ln_b, gla_w_in=m_gla_w_in, gla_w_gk2=m_gla_w_gk2,
                gla_b_gk=m_gla_b_gk, gla_g_norm=m_gla_g_norm, gla_w_out=m_gla_w_out, att_w_in=m_att_w_in,
                att_b_in=m_att_b_in, att_rel_bias=m_att_rel_bias, att_w_out=m_att_w_out, ff_w1=m_ff_w1, ff_w2=m_ff_w2)
    mom2 = dict(w_ada=v_w_ada, b_ada=v_b_ada, ln_g=v_ln_g, ln_b=v_ln_b, gla_w_in=v_gla_w_in, gla_w_gk2=v_gla_w_gk2,
                gla_b_gk=v_gla_b_gk, gla_g_norm=v_gla_g_norm, gla_w_out=v_gla_w_out, att_w_in=v_att_w_in,
                att_b_in=v_att_b_in, att_rel_bias=v_att_rel_bias, att_w_out=v_att_w_out, ff_w1=v_ff_w1, ff_w2=v_ff_w2)

    ax, ay, ac = lax.axis_index("x"), lax.axis_index("y"), lax.axis_index("c")
    chip = 2 * ax + ay
    dev = 2 * chip + ac
    S = x.shape[1]
    x2 = x.reshape(S, D_MODEL)
    t2 = loss_target.reshape(S, D_MODEL)

    comm = Comm(weights, ac, chip)

    small_rows = 16
    spack = _pack_small([c] + [weights[n] for n, _ in SMALL_SHARDED], small_rows)
    sg = all_gather8(spack, "gather_small").reshape(N_DEV, small_rows, PACK_W)
    parts = _unpack_small(sg, [(1, D_MODEL)] + [shp for _, shp in SMALL_SHARDED])
    c_all = parts[0].reshape(N_DEV, D_MODEL)
    small = {n: _unshard_last(p[0::2]) for (n, _), p in zip(SMALL_SHARDED, parts[1:])}
    small["gla_b_gk"] = gla_b_gk
    small["att_rel_bias"] = att_rel_bias

    c_act = silu_rows(jnp.pad(c_all, ((0, 128 - N_DEV), (0, 0))), "silu_c")
    wa = w_ada.astype(BF16).transpose(1, 0, 2).reshape(1, D_MODEL, DEPTH * 6 * D_MODEL // N_CHIPS)
    mods_part = mm_plain(c_act, wa, 0, "ada_fwd", tm=128)[:N_DEV]
    mg = all_gather8(mods_part, "gather_mods").reshape(N_CHIPS, 2, N_DEV, DEPTH, 6 * D_MODEL // N_CHIPS)
    mods_mine = lax.dynamic_index_in_dim(mg[:, 0], dev, axis=1, keepdims=False)
    mods = mods_mine.transpose(1, 0, 2).reshape(DEPTH, 6 * D_MODEL) + b_ada
    mods = mods.reshape(DEPTH, 6, D_MODEL)

    sq, grad_x, dmods, g_small = local_step(x2, t2, mods, comm, small)
    loss = lax.psum(0.5 * sq[0, 0] / D_MODEL, ("x", "y", "c"))

    g_shard = comm.reduce_end()

    dm_flat = dmods.reshape(DEPTH, 6 * D_MODEL)
    g_rows = 80
    gpack = _pack_small([dm_flat] + [g_small[n] for n in SMALL_GRAD_ORDER], g_rows)
    gg = all_gather8(gpack, "gather_small_grads").reshape(N_DEV, g_rows, PACK_W)
    gsum = sum_over_devices(gg, "sum_small_grads")
    sums = _unpack_small(gsum, [(DEPTH, 6 * D_MODEL)] + [SMALL_FULL[n] for n in SMALL_GRAD_ORDER])
    grads = dict(b_ada=sums[0])
    for n, full_g in zip(SMALL_GRAD_ORDER, sums[1:]):
        grads[n] = full_g if n in ("gla_b_gk", "att_rel_bias") else _shard_last(full_g, chip)
    dm_all = _unpack_small(gg, [(DEPTH, 6 * D_MODEL)])[0]
    dm_cols = _shard_last(dm_all, chip).reshape(N_DEV, DEPTH * 6 * D_MODEL // N_CHIPS)
    dm_cols = jnp.pad(dm_cols, ((0, 128 - N_DEV), (0, 0))).astype(BF16)
    gwa = mm_w(c_act, dm_cols, "ada_bwd", ts=128)
    grads["w_ada"] = gwa.reshape(D_MODEL, DEPTH, 6 * D_MODEL // N_CHIPS).transpose(1, 0, 2)
    grads.update(g_shard)

    deltas, new_m, new_v = {}, {}, {}
    for n in WEIGHT_NAMES:
        deltas[n], new_m[n], new_v[n] = adamw(weights[n], grads[n], mom1[n], mom2[n], "adamw_" + n)

    return (loss, grad_x.reshape(1, S, D_MODEL), *[grads[n] for n in WEIGHT_NAMES], *[deltas[n] for n in WEIGHT_NAMES],
            *[new_m[n] for n in WEIGHT_NAMES], *[new_v[n] for n in WEIGHT_NAMES])
```

```python
import functools

import jax
import jax.numpy as jnp
from jax import lax
from jax.experimental import pallas as pl
from jax.experimental.pallas import tpu as pltpu

F32 = jnp.float32
BF16 = jnp.bfloat16
HIGHEST = lax.Precision.HIGHEST
MESH = pl.DeviceIdType.MESH

D_MODEL = 1024
DEPTH = 4
CHUNK = 64
GLA_HEADS = 4
GLA_DK = 512
GLA_DV = 1024
GLA_DK_HEAD = 128
GLA_DV_HEAD = 256
GLA_RANK = 16
GLA_IN = 3088
GLA_IN_PAD = 3200
GLA_LR_OFF = 3072
ATT_HEADS = 16
ATT_HD = 64
LEFT_CHUNKS = 8
MAX_REL = 128
N_REL = 257
D_FF = 4096
ALPHA = (2.0 * DEPTH) ** 0.25
LN_EPS = 1e-5
RMS_EPS = 1e-6
NEG_INF = -1e30
GLA_SCALE = GLA_DK_HEAD ** -0.5
ATT_SCALE = ATT_HD ** -0.5
ADAM_LR = 0.001
ADAM_B1 = 0.9
ADAM_B2 = 0.999
ADAM_EPS = 1e-08
ADAM_WD = 0.01
ADAM_STEP = 10

ATT_TQ = 256
ATT_KW = 768
GLA_TB = 256
VMEM_LIMIT = 56 * 1024 * 1024
N_CHIPS = 4
N_DEV = 8
PACK_W = 1024


def _dot(a, b):
    return jnp.dot(a, b, preferred_element_type=F32)


def _dot_nt(a, b):
    return lax.dot_general(a, b, (((1,), (1,)), ((), ())), preferred_element_type=F32)


def _dot_tn(a, b):
    return lax.dot_general(a, b, (((0,), (0,)), ((), ())), preferred_element_type=F32)


def _cp(sem, vmem=VMEM_LIMIT):
    return pltpu.CompilerParams(dimension_semantics=sem, vmem_limit_bytes=vmem)


def _row_spec(n):
    return pl.BlockSpec((1, n), lambda *_: (0, 0))


def _sigmoid(x):
    return 1.0 / (1.0 + jnp.exp(-x))


def _log_sigmoid(x):
    return jnp.minimum(x, 0.0) - jnp.log1p(jnp.exp(-jnp.abs(x)))


class Side:
    def __init__(self, ins, out_shapes, n_sems, n_local, start, wait):
        self.ins, self.out_shapes, self.n_sems, self.n_local = list(ins), list(out_shapes), n_sems, n_local
        self.start, self.wait = start, wait

    def sem_shapes(self):
        return [pltpu.SemaphoreType.DMA((self.n_sems,)), pltpu.SemaphoreType.DMA((self.n_sems,)),
                pltpu.SemaphoreType.DMA((max(self.n_local, 1),))]


def run_side(side, name):
    n_in = len(side.ins)
    n_out = len(side.out_shapes)

    def body(*refs):
        ins, outs, sems = refs[:n_in], refs[n_in:n_in + n_out], refs[n_in + n_out:]
        side.start(ins, outs, *sems)
        side.wait(ins, outs, *sems)

    any_spec = pl.BlockSpec(memory_space=pl.ANY)
    return pl.pallas_call(body, name=name, out_shape=side.out_shapes, in_specs=[any_spec] * n_in,
                          out_specs=[any_spec] * n_out, scratch_shapes=side.sem_shapes())(*side.ins)


def hosted_call(main, side, *, name, grid, in_specs, out_specs, out_shape, scratch_shapes, dims, args):
    if side is None:
        outs = pl.pallas_call(main, name=name, grid=grid, in_specs=in_specs, out_specs=out_specs,
                              out_shape=out_shape, scratch_shapes=scratch_shapes, compiler_params=_cp(dims))(*args)
        return list(outs), []
    n_mi, n_mo, n_ms = len(in_specs), len(out_specs), len(scratch_shapes)
    n_si, n_so = len(side.ins), len(side.out_shapes)

    def kern(*refs):
        mi, si = refs[:n_mi], refs[n_mi:n_mi + n_si]
        o0 = n_mi + n_si
        mo, so = refs[o0:o0 + n_mo], refs[o0 + n_mo:o0 + n_mo + n_so]
        s0 = o0 + n_mo + n_so
        ms, sems = refs[s0:s0 + n_ms], refs[s0 + n_ms:]
        ids = [pl.program_id(d) for d in range(len(grid))]
        first = functools.reduce(jnp.logical_and, [i == 0 for i in ids])
        last = functools.reduce(jnp.logical_and, [i == g - 1 for i, g in zip(ids, grid)])

        @pl.when(first)
        def _():
            side.start(si, so, *sems)
        main(*mi, *mo, *ms)

        @pl.when(last)
        def _():
            side.wait(si, so, *sems)

    any_spec = pl.BlockSpec(memory_space=pl.ANY)
    outs = pl.pallas_call(
        kern, name=name, grid=grid, in_specs=list(in_specs) + [any_spec] * n_si,
        out_specs=list(out_specs) + [any_spec] * n_so, out_shape=list(out_shape) + side.out_shapes,
        scratch_shapes=list(scratch_shapes) + side.sem_shapes(),
        compiler_params=_cp(("arbitrary",) * len(grid)))(*args, *side.ins)
    return list(outs[:n_mo]), list(outs[n_mo:])


def modulate(x, sc, sh, name):
    S, D = x.shape
    tm = min(512, S)

    def kern(x_ref, sc_ref, sh_ref, u_ref):
        u_ref[...] = (x_ref[...] * (1.0 + sc_ref[...]) + sh_ref[...]).astype(BF16)

    return pl.pallas_call(
        kern, name=name, grid=(S // tm,),
        in_specs=[pl.BlockSpec((tm, D), lambda i: (i, 0)), _row_spec(D), _row_spec(D)],
        out_specs=pl.BlockSpec((tm, D), lambda i: (i, 0)),
        out_shape=jax.ShapeDtypeStruct((S, D), BF16),
        compiler_params=_cp(("parallel",)),
    )(x, sc, sh)


def loss_head(x, t, name):
    S, D = x.shape
    tm = min(512, S)

    def kern(x_ref, t_ref, dx_ref, l_ref):
        @pl.when(pl.program_id(0) == 0)
        def _():
            l_ref[...] = jnp.zeros_like(l_ref)
        e = x_ref[...] - t_ref[...]
        dx_ref[...] = e * (1.0 / D)
        l_ref[...] += jnp.sum(e * e)

    return pl.pallas_call(
        kern, name=name, grid=(S // tm,),
        in_specs=[pl.BlockSpec((tm, D), lambda i: (i, 0)), pl.BlockSpec((tm, D), lambda i: (i, 0))],
        out_specs=[pl.BlockSpec((tm, D), lambda i: (i, 0)), pl.BlockSpec((8, 128), lambda i: (0, 0))],
        out_shape=[jax.ShapeDtypeStruct((S, D), F32), jax.ShapeDtypeStruct((8, 128), F32)],
        compiler_params=_cp(("arbitrary",)),
    )(x, t)


def silu_rows(c_all, name):
    def kern(c_ref, o_ref):
        c = c_ref[...]
        o_ref[...] = (c * _sigmoid(c)).astype(BF16)

    return pl.pallas_call(kern, name=name, out_shape=jax.ShapeDtypeStruct(c_all.shape, BF16))(c_all)


def sum_over_devices(g, name):
    n, R, C = g.shape

    def kern(g_ref, o_ref):
        acc = g_ref[0]
        for d in range(1, n):
            acc = acc + g_ref[d]
        o_ref[...] = acc

    return pl.pallas_call(kern, name=name, out_shape=jax.ShapeDtypeStruct((R, C), F32))(g)


def _rows_block(R, C, budget=1 << 20):
    if R * C * 4 <= budget or R % 8:
        return R
    tr = max(8, (budget // (C * 4)) // 8 * 8)
    while R % tr:
        tr -= 8
    return tr


def adamw(w, g, m, v, name):
    shape = w.shape
    C = shape[-1]
    R = w.size // C
    w2, g2, m2, v2 = (t.reshape(R, C) for t in (w, g, m, v))
    tr = _rows_block(R, C)
    c1 = 1.0 - ADAM_B1 ** ADAM_STEP
    c2 = 1.0 - ADAM_B2 ** ADAM_STEP

    def kern(w_ref, g_ref, m_ref, v_ref, d_ref, nm_ref, nv_ref):
        gg = g_ref[...]
        nm = ADAM_B1 * m_ref[...] + (1.0 - ADAM_B1) * gg
        nv = ADAM_B2 * v_ref[...] + (1.0 - ADAM_B2) * (gg * gg)
        m_hat = nm / c1
        v_hat = nv / c2
        d_ref[...] = -ADAM_LR * (m_hat / (jnp.sqrt(v_hat) + ADAM_EPS) + ADAM_WD * w_ref[...])
        nm_ref[...] = nm
        nv_ref[...] = nv

    spec = pl.BlockSpec((tr, C), lambda i: (i, 0))
    outs = pl.pallas_call(
        kern, name=name, grid=(R // tr,),
        in_specs=[spec] * 4, out_specs=[spec] * 3,
        out_shape=[jax.ShapeDtypeStruct((R, C), F32)] * 3,
        compiler_params=_cp(("parallel",)),
    )(w2, g2, m2, v2)
    return tuple(o.reshape(shape) for o in outs)


def _tn_for(N):
    for tn in (1024, 768, 640, 512, 384, 256, 128):
        if N % tn == 0:
            return tn
    return N


def mm_plain(a, b3, layer, name, *, mode="f32", nt=False, bias=None, h=None, tm=1024, side=None):
    M, K = a.shape
    N = b3.shape[1] if nt else b3.shape[2]
    tm = min(tm, M)
    tn = _tn_for(N)
    a_spec = pl.BlockSpec((tm, K), lambda j, i: (i, 0))
    if nt:
        b_spec = pl.BlockSpec((None, tn, K), lambda j, i: (layer, j, 0))
    else:
        b_spec = pl.BlockSpec((None, K, tn), lambda j, i: (layer, 0, j))
    o_spec = pl.BlockSpec((tm, tn), lambda j, i: (i, j))
    ins, in_specs = [a, b3], [a_spec, b_spec]
    if bias is not None:
        ins.append(bias)
        in_specs.append(pl.BlockSpec((1, tn), lambda j, i: (0, j)))
    if mode == "mlp_dn":
        ins.append(h)
        in_specs.append(o_spec)
    elif mode not in ("f32", "bf16", "mlp_up"):
        raise ValueError(mode)
    odt = F32 if mode == "f32" else BF16

    def kern(a_ref, b_ref, *rest):
        rest = list(rest)
        bias_ref = rest.pop(0) if bias is not None else None
        h_ref = rest.pop(0) if mode == "mlp_dn" else None
        o_ref = rest.pop(0)
        if nt:
            bt_ref = rest.pop(0)

            @pl.when(pl.program_id(1) == 0)
            def _():
                bt_ref[...] = b_ref[...].T
            acc = _dot(a_ref[...], bt_ref[...])
        else:
            acc = _dot(a_ref[...], b_ref[...])
        if bias_ref is not None:
            acc = acc + bias_ref[...]
        if mode == "mlp_up":
            r = jnp.maximum(acc, 0.0)
            acc = r * r
        elif mode == "mlp_dn":
            acc = acc * (2.0 * jnp.sqrt(h_ref[...].astype(F32)))
        o_ref[...] = acc.astype(odt)

    outs, landed = hosted_call(
        kern, side, name=name, grid=(N // tn, M // tm), in_specs=in_specs, out_specs=[o_spec],
        out_shape=[jax.ShapeDtypeStruct((M, N), odt)],
        scratch_shapes=[pltpu.VMEM((K, tn), BF16)] if nt else [], dims=("parallel", "arbitrary"), args=tuple(ins))
    return outs[0] if side is None else (outs[0], landed)


def mm_down_ln(a, b3, layer, x_in, gate1p, ln_g, ln_b, sc_next, sh_next, name, *, tm=256):
    M, K = a.shape
    D = b3.shape[2]
    tm = min(tm, M)

    def kern(a_ref, b_ref, x_ref, gp_ref, lg_ref, lb_ref, sc_ref, sh_ref, y_ref, xo_ref, u_ref):
        y = _dot(a_ref[...], b_ref[...])
        y_ref[...] = y
        z = ALPHA * x_ref[...] + gp_ref[...] * y
        mu = jnp.mean(z, axis=-1, keepdims=True)
        zc = z - mu
        var = jnp.mean(zc * zc, axis=-1, keepdims=True)
        xo = (zc * lax.rsqrt(var + LN_EPS)) * lg_ref[...] + lb_ref[...]
        xo_ref[...] = xo
        u_ref[...] = (xo * (1.0 + sc_ref[...]) + sh_ref[...]).astype(BF16)

    tile = pl.BlockSpec((tm, D), lambda i: (i, 0))
    return pl.pallas_call(
        kern, name=name, grid=(M // tm,),
        in_specs=[pl.BlockSpec((tm, K), lambda i: (i, 0)), pl.BlockSpec((None, K, D), lambda i: (layer, 0, 0)), tile]
        + [_row_spec(D)] * 5,
        out_specs=[tile, tile, tile],
        out_shape=[jax.ShapeDtypeStruct((M, D), F32)] * 2 + [jax.ShapeDtypeStruct((M, D), BF16)],
        compiler_params=_cp(("parallel",)),
    )(a, b3, x_in, gate1p, ln_g, ln_b, sc_next, sh_next)


def mm_down_comb(a, b3, layer, dz, x_in, sc1p, name, *, parts=1, ln=None, side=None, tm=256):
    D, K = b3.shape[1], b3.shape[2]
    M = a.shape[-2]
    kp = K // parts
    tm = min(tm, M)
    n_ln = 0 if ln is None else 4

    def kern(*refs):
        a_refs = refs[:parts]
        b_ref, dz_ref, x_ref, sp_ref = refs[parts:parts + 4]
        ln_refs = refs[parts + 4:parts + 4 + n_ln]
        outs = refs[parts + 4 + n_ln:]

        @pl.when(pl.program_id(0) == 0)
        def _():
            for o in outs:
                if o.shape[0] == 8:
                    o[...] = jnp.zeros_like(o)
        if parts == 1:
            du = _dot_nt(a_refs[0][...], b_ref[...])
        else:
            du = _dot_nt(a_refs[0][...], b_ref[:, 0:kp])
            for p in range(1, parts):
                du = du + _dot_nt(a_refs[p][...], b_ref[:, p * kp:(p + 1) * kp])
        dx = ALPHA * dz_ref[...] + du * sp_ref[...]
        if ln is None:
            dx_ref, s_ref = outs
            dx_ref[...] = dx
        else:
            dzl_ref, dyl_ref, s_ref, sl_ref = outs
            _ln_bwd_tile(dx, *ln_refs, dzl_ref, dyl_ref, sl_ref)
        s_ref[0:1, :] += jnp.sum(du * x_ref[...], axis=0, keepdims=True)
        s_ref[1:2, :] += jnp.sum(du, axis=0, keepdims=True)
        if parts > 1:
            for p in range(parts):
                s_ref[2 + p:3 + p, :] += jnp.sum(a_refs[p][...].astype(F32), axis=0, keepdims=True)

    tile = pl.BlockSpec((tm, D), lambda i: (i, 0))
    sums = pl.BlockSpec((8, D), lambda i: (0, 0))
    if parts == 1:
        a_ins, a_specs = [a], [pl.BlockSpec((tm, K), lambda i: (i, 0))]
    else:
        assert kp == D and parts <= 6
        a_ins = [a] * parts
        a_specs = [pl.BlockSpec((None, tm, kp), functools.partial(lambda i, p: (p, i, 0), p=p)) for p in range(parts)]
    in_specs = a_specs + [pl.BlockSpec((None, D, K), lambda i: (layer, 0, 0)), tile, tile, _row_spec(D)]
    args = a_ins + [b3, dz, x_in, sc1p]
    if ln is None:
        out_specs = [tile, sums]
        out_shape = [jax.ShapeDtypeStruct((M, D), F32), jax.ShapeDtypeStruct((8, D), F32)]
    else:
        in_specs += [tile, tile, _row_spec(D), _row_spec(D)]
        args += list(ln)
        out_specs = [tile, tile, sums, sums]
        out_shape = [jax.ShapeDtypeStruct((M, D), F32), jax.ShapeDtypeStruct((M, D), BF16),
                     jax.ShapeDtypeStruct((8, D), F32), jax.ShapeDtypeStruct((8, D), F32)]
    return hosted_call(kern, side, name=name, grid=(M // tm,), in_specs=in_specs, out_specs=out_specs,
                       out_shape=out_shape, scratch_shapes=[], dims=("arbitrary",), args=tuple(args))


def mm_w(a, b, name, *, ts=2048, tk=512, chips_out=False, b_parts=1, tn=None):
    S, K = a.shape
    npart = b.shape[-1]
    N = npart * b_parts
    ts = min(ts, S)
    tk = min(tk, K)
    n_chip = N // N_CHIPS
    if tn is None:
        tn = _tn_for(n_chip if chips_out else npart)
    assert npart % tn == 0 and (not chips_out or n_chip % tn == 0)

    def kern(a_ref, b_ref, o_ref):
        @pl.when(pl.program_id(2) == 0)
        def _():
            o_ref[...] = jnp.zeros_like(o_ref)
        o_ref[...] += _dot_tn(a_ref[...], b_ref[...])

    if b_parts == 1:
        b_spec = pl.BlockSpec((ts, tn), lambda k, n, s: (s, n))
    else:
        per = npart // tn
        b_spec = pl.BlockSpec((None, ts, tn), lambda k, n, s: (n // per, s, n % per))
    if chips_out:
        per_chip = n_chip // tn
        o_spec = pl.BlockSpec((None, tk, tn), lambda k, n, s: (n // per_chip, k, n % per_chip))
        out_shape = jax.ShapeDtypeStruct((N_CHIPS, K, n_chip), F32)
    else:
        o_spec = pl.BlockSpec((tk, tn), lambda k, n, s: (k, n))
        out_shape = jax.ShapeDtypeStruct((K, N), F32)
    return pl.pallas_call(
        kern, name=name, grid=(K // tk, N // tn, S // ts),
        in_specs=[pl.BlockSpec((ts, tk), lambda k, n, s: (s, k)), b_spec],
        out_specs=o_spec, out_shape=out_shape,
        compiler_params=_cp(("parallel", "parallel", "arbitrary")),
    )(a, b)


def mm_w_chips3(a, b3, name, *, ts=512):
    S, K = a.shape
    P = b3.shape[2]
    n_chip = 3 * P // N_CHIPS
    ts = min(ts, S)
    pieces = []
    for chip in range(N_CHIPS):
        lo, hi = chip * n_chip, (chip + 1) * n_chip
        while lo < hi:
            part = lo // P
            w = min(hi, (part + 1) * P) - lo
            pieces.append((chip, lo - chip * n_chip, part, lo - part * P, w))
            lo += w

    def kern(a_ref, b_ref, o_ref):
        @pl.when(pl.program_id(0) == 0)
        def _():
            o_ref[...] = jnp.zeros_like(o_ref)
        at = a_ref[...].T
        for chip, oc, part, pc, w in pieces:
            o_ref[chip, :, oc:oc + w] += _dot(at, b_ref[part, :, pc:pc + w])

    return pl.pallas_call(
        kern, name=name, grid=(S // ts,),
        in_specs=[pl.BlockSpec((ts, K), lambda s: (s, 0)), pl.BlockSpec((3, ts, P), lambda s: (0, s, 0))],
        out_specs=pl.BlockSpec((N_CHIPS, K, n_chip), lambda s: (0, 0, 0)),
        out_shape=jax.ShapeDtypeStruct((N_CHIPS, K, n_chip), F32),
        compiler_params=_cp(("arbitrary",)),
    )(a, b3)


def mm_f32(a, b, name):
    def kern(a_ref, b_ref, o_ref):
        o_ref[...] = jnp.dot(a_ref[...], b_ref[...], precision=HIGHEST, preferred_element_type=F32)

    return pl.pallas_call(kern, name=name, out_shape=jax.ShapeDtypeStruct((a.shape[0], b.shape[1]), F32),
                          compiler_params=pltpu.CompilerParams(vmem_limit_bytes=VMEM_LIMIT))(a, b)


def _ln_bwd_tile(dxo_t, x_ref, y_ref, gp_ref, lg_ref, dz_ref, dy_ref, s_ref):
    yv = y_ref[...]
    z = ALPHA * x_ref[...] + gp_ref[...] * yv
    mu = jnp.mean(z, axis=-1, keepdims=True)
    zc = z - mu
    var = jnp.mean(zc * zc, axis=-1, keepdims=True)
    rstd = lax.rsqrt(var + LN_EPS)
    xhat = zc * rstd
    dxh = dxo_t * lg_ref[...]
    dz = rstd * (dxh - jnp.mean(dxh, axis=-1, keepdims=True)
                 - xhat * jnp.mean(dxh * xhat, axis=-1, keepdims=True))
    dz_ref[...] = dz
    dy_ref[...] = (gp_ref[...] * dz).astype(BF16)
    s_ref[0:1, :] += jnp.sum(dxo_t * xhat, axis=0, keepdims=True)
    s_ref[1:2, :] += jnp.sum(dxo_t, axis=0, keepdims=True)
    s_ref[2:3, :] += jnp.sum(dz * yv, axis=0, keepdims=True)


def ln_bwd(dxo, x_in, y, gate1p, ln_g, name, *, tm=256):
    S, D = dxo.shape
    tm = min(tm, S)

    def kern(dxo_ref, x_ref, y_ref, gp_ref, lg_ref, dz_ref, dy_ref, s_ref):
        @pl.when(pl.program_id(0) == 0)
        def _():
            s_ref[...] = jnp.zeros_like(s_ref)
        _ln_bwd_tile(dxo_ref[...], x_ref, y_ref, gp_ref, lg_ref, dz_ref, dy_ref, s_ref)

    tile = pl.BlockSpec((tm, D), lambda i: (i, 0))
    return pl.pallas_call(
        kern, name=name, grid=(S // tm,),
        in_specs=[tile, tile, tile, _row_spec(D), _row_spec(D)],
        out_specs=[tile, tile, pl.BlockSpec((8, D), lambda i: (0, 0))],
        out_shape=[jax.ShapeDtypeStruct((S, D), F32), jax.ShapeDtypeStruct((S, D), BF16),
                   jax.ShapeDtypeStruct((8, D), F32)],
        compiler_params=_cp(("arbitrary",)),
    )(dxo, x_in, y, gate1p, ln_g)


def _tri64():
    r = lax.broadcasted_iota(jnp.int32, (CHUNK, CHUNK), 0)
    c = lax.broadcasted_iota(jnp.int32, (CHUNK, CHUNK), 1)
    return r >= c


def _gla_chunk_common(proj_ref, rows, b, h):
    kc = slice(h * GLA_DK_HEAD, (h + 1) * GLA_DK_HEAD)
    bh = b[:, kc]
    ep = jnp.exp(bh)
    en = jnp.exp(-bh)
    bl = bh[CHUNK - 1:CHUNK, :]
    ee = jnp.exp(bl - bh)
    dec = jnp.exp(bl)
    q = proj_ref[rows, h * GLA_DK_HEAD:(h + 1) * GLA_DK_HEAD] * GLA_SCALE
    k = proj_ref[rows, GLA_DK + h * GLA_DK_HEAD:GLA_DK + (h + 1) * GLA_DK_HEAD]
    v = proj_ref[rows, 2 * GLA_DK + h * GLA_DV_HEAD:2 * GLA_DK + (h + 1) * GLA_DV_HEAD]
    g = proj_ref[rows, 2 * GLA_DK + GLA_DV + h * GLA_DV_HEAD:2 * GLA_DK + GLA_DV + (h + 1) * GLA_DV_HEAD]
    return ep, en, ee, dec, q, k, v, g


def gla_fwd(proj, wgk_p, bgk, gnorm, name, side=None):
    S = proj.shape[0]
    TB = min(GLA_TB, S)
    ncb = TB // CHUNK

    def kern(proj_ref, wgk_ref, bgk_ref, gn_ref, zg_ref, st_ref, state_scr, la_scr):
        @pl.when(pl.program_id(0) == 0)
        def _():
            state_scr[...] = jnp.zeros_like(state_scr)
        lr = proj_ref[:, GLA_LR_OFF:GLA_IN_PAD].astype(BF16)
        gk = _dot(lr, wgk_ref[...]) + bgk_ref[...]
        la_scr[...] = _log_sigmoid(gk) * (1.0 / 16.0)
        lower = _tri64()
        tri = lower.astype(F32)

        def chunk(c, carry):
            rows = pl.ds(pl.multiple_of(c * CHUNK, CHUNK), CHUNK)
            b = jnp.dot(tri, la_scr[rows, :], precision=HIGHEST, preferred_element_type=F32)
            H = range(GLA_HEADS)
            cm = [_gla_chunk_common(proj_ref, rows, b, h) for h in H]
            qf = [(m[4] * m[0]).astype(BF16) for m in cm]
            kn = [(m[5] * m[1]).astype(BF16) for m in cm]
            qn = [(m[4] * m[1]).astype(BF16) for m in cm]
            kp = [(m[5] * m[0]).astype(BF16) for m in cm]
            ke = [(m[5] * m[2]).astype(BF16) for m in cm]
            vb = [m[6].astype(BF16) for m in cm]
            st = [state_scr[h] for h in H]
            a_f = [_dot_nt(qf[h], kn[h]) for h in H]
            a_b = [_dot_nt(qn[h], kp[h]) for h in H]
            o_st = [_dot_nt(qf[h], st[h].astype(BF16)) for h in H]
            upd = [_dot_tn(vb[h], ke[h]) for h in H]
            amat = [jnp.where(lower, a_f[h], a_b[h]).astype(BF16) for h in H]
            o = [_dot(amat[h], vb[h]) + o_st[h] for h in H]
            for h in H:
                st_ref[c, h] = st[h]
                state_scr[h] = st[h] * cm[h][3] + upd[h]
            for h in H:
                g = cm[h][7]
                vc = slice(h * GLA_DV_HEAD, (h + 1) * GLA_DV_HEAD)
                r = lax.rsqrt(jnp.mean(o[h] * o[h], axis=-1, keepdims=True) + RMS_EPS)
                on = (o[h] * r) * gn_ref[:, vc]
                zg_ref[rows, vc] = (on * (g * _sigmoid(g))).astype(BF16)
            return carry

        lax.fori_loop(0, ncb, chunk, 0)

    return hosted_call(
        kern, side, name=name, grid=(S // TB,),
        in_specs=[pl.BlockSpec((TB, GLA_IN_PAD), lambda i: (i, 0)),
                  pl.BlockSpec((128, GLA_DK), lambda i: (0, 0)), _row_spec(GLA_DK), _row_spec(GLA_DV)],
        out_specs=[pl.BlockSpec((TB, GLA_DV), lambda i: (i, 0)),
                   pl.BlockSpec((ncb, GLA_HEADS, GLA_DV_HEAD, GLA_DK_HEAD), lambda i: (i, 0, 0, 0))],
        out_shape=[jax.ShapeDtypeStruct((S, GLA_DV), BF16),
                   jax.ShapeDtypeStruct((S // CHUNK, GLA_HEADS, GLA_DV_HEAD, GLA_DK_HEAD), F32)],
        scratch_shapes=[pltpu.VMEM((GLA_HEADS, GLA_DV_HEAD, GLA_DK_HEAD), F32), pltpu.VMEM((TB, GLA_DK), F32)],
        dims=("arbitrary",), args=(proj, wgk_p, bgk, gnorm))


def gla_bwd(proj, states, dzg, wgk_p, bgk, gnorm, name, side=None):
    S = proj.shape[0]
    TB = min(GLA_TB, S)
    ncb = TB // CHUNK
    nb = S // TB

    def kern(proj_ref, st_ref, dzg_ref, wgk_ref, bgk_ref, gn_ref,
             dproj_ref, dwgk_ref, dbgk_ref, dgn_ref, dstate_scr, la_scr, gk_scr, dgk_scr):
        @pl.when(pl.program_id(0) == 0)
        def _():
            dstate_scr[...] = jnp.zeros_like(dstate_scr)
            dwgk_ref[...] = jnp.zeros_like(dwgk_ref)
            dbgk_ref[...] = jnp.zeros_like(dbgk_ref)
            dgn_ref[...] = jnp.zeros_like(dgn_ref)
        lr = proj_ref[:, GLA_LR_OFF:GLA_IN_PAD].astype(BF16)
        gk = _dot(lr, wgk_ref[...]) + bgk_ref[...]
        gk_scr[...] = gk
        la_scr[...] = _log_sigmoid(gk) * (1.0 / 16.0)
        lower = _tri64()
        tri = lower.astype(F32)
        r_i = lax.broadcasted_iota(jnp.int32, (CHUNK, CHUNK), 0)
        c_i = lax.broadcasted_iota(jnp.int32, (CHUNK, CHUNK), 1)
        triu = (c_i >= r_i).astype(F32)
        last_row = lax.broadcasted_iota(jnp.int32, (CHUNK, GLA_DK_HEAD), 0) == CHUNK - 1

        def chunk(cc, carry):
            c = ncb - 1 - cc
            rows = pl.ds(pl.multiple_of(c * CHUNK, CHUNK), CHUNK)
            b = jnp.dot(tri, la_scr[rows, :], precision=HIGHEST, preferred_element_type=F32)
            H = range(GLA_HEADS)
            kcs = [slice(h * GLA_DK_HEAD, (h + 1) * GLA_DK_HEAD) for h in H]
            vcs = [slice(h * GLA_DV_HEAD, (h + 1) * GLA_DV_HEAD) for h in H]
            cm = [_gla_chunk_common(proj_ref, rows, b, h) for h in H]
            ep, en, ee, dec = ([m[i] for m in cm] for i in range(4))
            gs = [m[7] for m in cm]
            qf = [m[4] * m[0] for m in cm]
            kn = [m[5] * m[1] for m in cm]
            qn = [m[4] * m[1] for m in cm]
            kp = [m[5] * m[0] for m in cm]
            ke = [m[5] * m[2] for m in cm]
            qf_b, kn_b, qn_b, kp_b, ke_b = ([t.astype(BF16) for t in ts] for ts in (qf, kn, qn, kp, ke))
            vb = [m[6].astype(BF16) for m in cm]
            st = [st_ref[c, h] for h in H]
            st_b = [s.astype(BF16) for s in st]
            dst = [dstate_scr[h] for h in H]
            dst_b = [s.astype(BF16) for s in dst]
            a_f = [_dot_nt(qf_b[h], kn_b[h]) for h in H]
            a_b = [_dot_nt(qn_b[h], kp_b[h]) for h in H]
            o_st = [_dot_nt(qf_b[h], st_b[h]) for h in H]
            dv_st = [_dot_nt(ke_b[h], dst_b[h]) for h in H]
            dke = [_dot(vb[h], dst_b[h]) for h in H]
            amat = [jnp.where(lower, a_f[h], a_b[h]).astype(BF16) for h in H]
            o = [_dot(amat[h], vb[h]) + o_st[h] for h in H]
            do_b, dgs = [], []
            for h in H:
                r = lax.rsqrt(jnp.mean(o[h] * o[h], axis=-1, keepdims=True) + RMS_EPS)
                oh = o[h] * r
                gn = gn_ref[:, vcs[h]]
                g = gs[h]
                sg = _sigmoid(g)
                dz = dzg_ref[rows, vcs[h]]
                don = dz * (g * sg)
                dgs.append(dz * (oh * gn) * (sg * (1.0 + g * (1.0 - sg))))
                dgn_ref[:, vcs[h]] += jnp.sum(don * oh, axis=0, keepdims=True)
                doh = don * gn
                do_b.append((r * (doh - oh * jnp.mean(doh * oh, axis=-1, keepdims=True))).astype(BF16))
            da = [_dot_nt(do_b[h], vb[h]) for h in H]
            dv = [_dot_tn(amat[h], do_b[h]) + dv_st[h] for h in H]
            dqf_st = [_dot(do_b[h], st_b[h]) for h in H]
            dst_upd = [_dot_tn(do_b[h], qf_b[h]) for h in H]
            da_f = [jnp.where(lower, da[h], 0.0).astype(BF16) for h in H]
            da_b = [jnp.where(lower, 0.0, da[h]).astype(BF16) for h in H]
            dqf = [_dot(da_f[h], kn_b[h]) + dqf_st[h] for h in H]
            dkn = [_dot_tn(da_f[h], qf_b[h]) for h in H]
            dqn = [_dot(da_b[h], kp_b[h]) for h in H]
            dkp = [_dot_tn(da_b[h], qn_b[h]) for h in H]
            dbs = []
            for h in H:
                ddec = jnp.sum(dst[h] * st[h], axis=0, keepdims=True)
                dstate_scr[h] = dst[h] * dec[h] + dst_upd[h]
                db = dqf[h] * qf[h] - dkn[h] * kn[h] - dqn[h] * qn[h] + dkp[h] * kp[h] - dke[h] * ke[h]
                dbl = jnp.sum(dke[h] * ke[h], axis=0, keepdims=True) + ddec * dec[h]
                dbs.append(db + jnp.where(last_row, dbl, 0.0))
            dla = [jnp.dot(triu, dbs[h], precision=HIGHEST, preferred_element_type=F32) for h in H]
            for h in H:
                dq = (dqf[h] * ep[h] + dqn[h] * en[h]) * GLA_SCALE
                dk = dkn[h] * en[h] + dkp[h] * ep[h] + dke[h] * ee[h]
                dgk_scr[rows, kcs[h]] = dla[h] * (1.0 / 16.0) * _sigmoid(-gk_scr[rows, kcs[h]])
                dproj_ref[rows, kcs[h]] = dq.astype(BF16)
                dproj_ref[rows, GLA_DK + h * GLA_DK_HEAD:GLA_DK + (h + 1) * GLA_DK_HEAD] = dk.astype(BF16)
                dproj_ref[rows, 2 * GLA_DK + h * GLA_DV_HEAD:2 * GLA_DK + (h + 1) * GLA_DV_HEAD] = dv[h].astype(BF16)
                dproj_ref[rows, 2 * GLA_DK + GLA_DV + h * GLA_DV_HEAD:
                          2 * GLA_DK + GLA_DV + (h + 1) * GLA_DV_HEAD] = dgs[h].astype(BF16)
            return carry

        lax.fori_loop(0, ncb, chunk, 0)
        dgk = dgk_scr[...]
        dgk_b = dgk.astype(BF16)
        dproj_ref[:, GLA_LR_OFF:GLA_IN_PAD] = _dot_nt(dgk_b, wgk_ref[...]).astype(BF16)
        dwgk_ref[...] += _dot_tn(lr, dgk_b)
        dbgk_ref[...] += jnp.sum(dgk, axis=0, keepdims=True)

    rev = lambda i: (nb - 1 - i, 0)
    return hosted_call(
        kern, side, name=name, grid=(nb,),
        in_specs=[pl.BlockSpec((TB, GLA_IN_PAD), rev),
                  pl.BlockSpec((ncb, GLA_HEADS, GLA_DV_HEAD, GLA_DK_HEAD), lambda i: (nb - 1 - i, 0, 0, 0)),
                  pl.BlockSpec((TB, GLA_DV), rev),
                  pl.BlockSpec((128, GLA_DK), lambda i: (0, 0)), _row_spec(GLA_DK), _row_spec(GLA_DV)],
        out_specs=[pl.BlockSpec((TB, GLA_IN_PAD), rev),
                   pl.BlockSpec((128, GLA_DK), lambda i: (0, 0)), _row_spec(GLA_DK), _row_spec(GLA_DV)],
        out_shape=[jax.ShapeDtypeStruct((S, GLA_IN_PAD), BF16), jax.ShapeDtypeStruct((128, GLA_DK), F32),
                   jax.ShapeDtypeStruct((1, GLA_DK), F32), jax.ShapeDtypeStruct((1, GLA_DV), F32)],
        scratch_shapes=[pltpu.VMEM((GLA_HEADS, GLA_DV_HEAD, GLA_DK_HEAD), F32), pltpu.VMEM((TB, GLA_DK), F32),
                        pltpu.VMEM((TB, GLA_DK), F32), pltpu.VMEM((TB, GLA_DK), F32)],
        dims=("arbitrary",), args=(proj, states, dzg, wgk_p, bgk, gnorm))


ATT_TW = 1024
ATT_CLASSES = 3


def _att_window(i):
    return pl.multiple_of(jnp.maximum(i * ATT_TQ - LEFT_CHUNKS * CHUNK, 0), ATT_TQ)


def _att_rel_index():
    e = jnp.arange(ATT_TW)[None, :]
    d = jnp.where(e < ATT_KW, e, e - ATT_TW)
    off = (jnp.arange(ATT_CLASSES) * ATT_TQ)[:, None]
    return jnp.clip(off - d, -MAX_REL, MAX_REL) + MAX_REL


def _row_bits():
    return lax.broadcasted_iota(jnp.int32, (ATT_TQ, ATT_TW), 0)


def att_bias_tiles(rel_bias, name):
    pick = (jnp.arange(384)[:, None] == _att_rel_index().reshape(-1)[None, :]).astype(F32)
    tab = mm_f32(jnp.pad(rel_bias, ((0, 0), (0, 384 - N_REL))), pick, name + "_tab")
    tab = tab.reshape(ATT_HEADS * ATT_CLASSES, 1, ATT_TW)

    def kern(t_ref, o_ref):
        cls = pl.program_id(0) % ATT_CLASSES
        x = jnp.broadcast_to(t_ref[...], (ATT_TQ, ATT_TW))
        row = _row_bits()
        for b in range(8):
            x = jnp.where((row & (1 << b)) != 0, pltpu.roll(x, 1 << b, axis=1), x)
        x = x[:, :ATT_KW]
        qc = cls * (ATT_TQ // CHUNK) + lax.shift_right_arithmetic(
            lax.broadcasted_iota(jnp.int32, (ATT_TQ, ATT_KW), 0), 6)
        kc = lax.shift_right_arithmetic(lax.broadcasted_iota(jnp.int32, (ATT_TQ, ATT_KW), 1), 6)
        o_ref[...] = jnp.where((kc <= qc) & (kc >= qc - LEFT_CHUNKS), x, NEG_INF)

    return pl.pallas_call(
        kern, name=name, grid=(ATT_HEADS * ATT_CLASSES,),
        in_specs=[pl.BlockSpec((None, 1, ATT_TW), lambda i: (i, 0, 0))],
        out_specs=pl.BlockSpec((None, ATT_TQ, ATT_KW), lambda i: (i, 0, 0)),
        out_shape=jax.ShapeDtypeStruct((ATT_HEADS * ATT_CLASSES, ATT_TQ, ATT_KW), F32),
        compiler_params=_cp(("parallel",)),
    )(tab)


def att_bias_grad(dbt, name):
    def kern(d_ref, o_ref):
        x = jnp.concatenate([d_ref[...], jnp.zeros((ATT_TQ, ATT_TW - ATT_KW), F32)], axis=1)
        row = _row_bits()
        for b in range(8):
            x = jnp.where((row & (1 << b)) != 0, pltpu.roll(x, ATT_TW - (1 << b), axis=1), x)
        o_ref[...] = jnp.sum(x, axis=0, keepdims=True)

    diag = pl.pallas_call(
        kern, name=name + "_diag", grid=(ATT_HEADS * ATT_CLASSES,),
        in_specs=[pl.BlockSpec((None, ATT_TQ, ATT_KW), lambda i: (i, 0, 0))],
        out_specs=pl.BlockSpec((None, 1, ATT_TW), lambda i: (i, 0, 0)),
        out_shape=jax.ShapeDtypeStruct((ATT_HEADS * ATT_CLASSES, 1, ATT_TW), F32),
        compiler_params=_cp(("parallel",)),
    )(dbt)
    diag = diag.reshape(ATT_HEADS, ATT_CLASSES * ATT_TW)
    onehot = (_att_rel_index().reshape(-1)[:, None] == jnp.arange(384)[None, :]).astype(F32)
    return mm_f32(diag, onehot, name + "_bins")[:, :N_REL]


def _att_scores(q_ref, kw, bias_ref):
    hs = [slice(hh * ATT_HD, (hh + 1) * ATT_HD) for hh in range(2)]
    q = [q_ref[:, h] * ATT_SCALE for h in hs]
    k = [kw[:, h] for h in hs]
    s = [_dot_nt(q[hh], k[hh]) + bias_ref[hh] for hh in range(2)]
    e = [jnp.exp(t - jnp.max(t, axis=-1, keepdims=True)) for t in s]
    inv = [1.0 / jnp.sum(t, axis=-1, keepdims=True) for t in e]
    return hs, q, k, e, inv


def _att_specs(S):
    nq = D_MODEL // 128
    q_spec = pl.BlockSpec((ATT_TQ, 128), lambda p, i: (i, p))
    k_spec = pl.BlockSpec((S, 128), lambda p, i: (0, nq + p))
    v_spec = pl.BlockSpec((S, 128), lambda p, i: (0, 2 * nq + p))
    b_spec = pl.BlockSpec((2, None, ATT_TQ, ATT_KW), lambda p, i: (p, jnp.minimum(i, ATT_CLASSES - 1), 0, 0))
    return q_spec, k_spec, v_spec, b_spec


def attn_fwd(qkv, bias, name, side=None):
    S = qkv.shape[0]
    q_spec, k_spec, v_spec, b_spec = _att_specs(S)

    def kern(q_ref, k_ref, v_ref, bias_ref, o_ref):
        ws = _att_window(pl.program_id(1))
        kw = k_ref[pl.ds(ws, ATT_KW), :]
        vw = v_ref[pl.ds(ws, ATT_KW), :]
        hs, _, _, e, inv = _att_scores(q_ref, kw, bias_ref)
        outs = [_dot(e[hh].astype(BF16), vw[:, hs[hh]]) * inv[hh] for hh in range(2)]
        o_ref[...] = jnp.concatenate(outs, axis=1).astype(BF16)

    return hosted_call(
        kern, side, name=name, grid=(ATT_HEADS // 2, S // ATT_TQ),
        in_specs=[q_spec, k_spec, v_spec, b_spec],
        out_specs=[pl.BlockSpec((ATT_TQ, 128), lambda p, i: (i, p))],
        out_shape=[jax.ShapeDtypeStruct((S, D_MODEL), BF16)],
        scratch_shapes=[], dims=("parallel", "arbitrary"), args=(qkv, qkv, qkv, bias))


def attn_bwd(qkv, bias, do, name, side=None):
    S = qkv.shape[0]
    nblk = S // ATT_TQ
    q_spec, k_spec, v_spec, b_spec = _att_specs(S)

    def kern(q_ref, k_ref, v_ref, bias_ref, do_ref, dqkv_ref, db_ref, dk_scr, dv_scr):
        i = pl.program_id(1)

        @pl.when(i == 0)
        def _():
            dk_scr[...] = jnp.zeros_like(dk_scr)
            dv_scr[...] = jnp.zeros_like(dv_scr)
            db_ref[...] = jnp.zeros_like(db_ref)
        ws = _att_window(i)
        win = pl.ds(ws, ATT_KW)
        kw = k_ref[win, :]
        vw = v_ref[win, :]
        o_cls = jnp.minimum(i, ATT_CLASSES - 1)
        R2 = range(2)
        hs, q, k, e, inv = _att_scores(q_ref, kw, bias_ref)
        do_h = [do_ref[:, h] for h in hs]
        dp = [_dot_nt(do_h[hh], vw[:, hs[hh]]) for hh in R2]
        p = [e[hh] * inv[hh] for hh in R2]
        dvs = [_dot_tn(p[hh].astype(BF16), do_h[hh]) for hh in R2]
        ds = [p[hh] * (dp[hh] - jnp.sum(p[hh] * dp[hh], axis=-1, keepdims=True)) for hh in R2]
        ds_b = [t.astype(BF16) for t in ds]
        dqs = [_dot(ds_b[hh], k[hh]) * ATT_SCALE for hh in R2]
        dks = [_dot_tn(ds_b[hh], q[hh]) for hh in R2]
        for hh in R2:
            db_ref[hh, o_cls] += ds[hh]
        dqkv_ref[0, pl.ds(pl.multiple_of(i * ATT_TQ, ATT_TQ), ATT_TQ), :] = jnp.concatenate(dqs, axis=1).astype(BF16)
        dk_scr[win, :] += jnp.concatenate(dks, axis=1)
        dv_scr[win, :] += jnp.concatenate(dvs, axis=1)

        @pl.when(i == nblk - 1)
        def _():
            dqkv_ref[1] = dk_scr[...].astype(BF16)
            dqkv_ref[2] = dv_scr[...].astype(BF16)

    return hosted_call(
        kern, side, name=name, grid=(ATT_HEADS // 2, nblk),
        in_specs=[q_spec, k_spec, v_spec, b_spec, pl.BlockSpec((ATT_TQ, 128), lambda p, i: (i, p))],
        out_specs=[pl.BlockSpec((3, S, 128), lambda p, i: (0, 0, p)),
                   pl.BlockSpec((2, ATT_CLASSES, ATT_TQ, ATT_KW), lambda p, i: (p, 0, 0, 0))],
        out_shape=[jax.ShapeDtypeStruct((3, S, D_MODEL), BF16),
                   jax.ShapeDtypeStruct((ATT_HEADS, ATT_CLASSES, ATT_TQ, ATT_KW), F32)],
        scratch_shapes=[pltpu.VMEM((S, 128), F32), pltpu.VMEM((S, 128), F32)],
        dims=("parallel", "arbitrary"), args=(qkv, qkv, qkv, bias, do))


def colsum3(a3, name):
    P, S, N = a3.shape
    tm = min(512, S)

    def kern(a_ref, o_ref):
        @pl.when(pl.program_id(1) == 0)
        def _():
            o_ref[...] = jnp.zeros_like(o_ref)
        o_ref[...] += jnp.sum(a_ref[...].astype(F32), axis=0, keepdims=True)

    return pl.pallas_call(
        kern, name=name, grid=(P, S // tm),
        in_specs=[pl.BlockSpec((None, tm, N), lambda p, i: (p, i, 0))],
        out_specs=pl.BlockSpec((None, 1, N), lambda p, i: (p, 0, 0)),
        out_shape=jax.ShapeDtypeStruct((P, 1, N), F32),
        compiler_params=_cp(("parallel", "arbitrary")),
    )(a3)


def _me():
    return lax.axis_index("x"), lax.axis_index("y"), lax.axis_index("c")


def _other_chips(x, y):
    return [(1 - x, y), (x, 1 - y), (1 - x, 1 - y)]


def all_gather8(x_shard, name):
    m_per, n = x_shard.shape

    def body(x_ref, out_ref, send_sems, recv_sems, local_sem):
        x, y, c = _me()
        me, sibling = (x, y, c), (x, y, 1 - c)
        chips = _other_chips(x, y)

        def rows(px, py, pc):
            return out_ref.at[pl.ds((4 * px + 2 * py + pc) * m_per, m_per), :]

        def copy(k, block, to, src=None):
            return pltpu.make_async_remote_copy(
                src_ref=rows(*block) if src is None else src, dst_ref=rows(*block),
                send_sem=send_sems.at[k], recv_sem=recv_sems.at[k], device_id=to, device_id_type=MESH)

        mine = pltpu.make_async_copy(x_ref, rows(*me), local_sem)
        mine.start()
        first = [copy(0, me, sibling, src=x_ref)]
        first += [copy(1 + j, me, (*chip, c), src=x_ref) for j, chip in enumerate(chips)]
        for cp in first:
            cp.start()
        passed = [copy(4 + j, (*chip, c), sibling) for j, chip in enumerate(chips)]
        for j, chip in enumerate(chips):
            copy(1 + j, (*chip, c), me).wait_recv()
            passed[j].start()
        copy(0, sibling, me).wait_recv()
        for j, chip in enumerate(chips):
            copy(4 + j, (*chip, 1 - c), me).wait_recv()
        for cp in first + passed:
            cp.wait_send()
        mine.wait()

    return pl.pallas_call(
        body, name=name,
        out_shape=jax.ShapeDtypeStruct((N_DEV * m_per, n), x_shard.dtype),
        in_specs=[pl.BlockSpec(memory_space=pltpu.VMEM)],
        out_specs=pl.BlockSpec(memory_space=pltpu.VMEM),
        scratch_shapes=[pltpu.SemaphoreType.DMA((7,)), pltpu.SemaphoreType.DMA((7,)), pltpu.SemaphoreType.DMA],
        compiler_params=pltpu.CompilerParams(vmem_limit_bytes=VMEM_LIMIT),
    )(x_shard)


def _half_rows(n_rows, c):
    h = n_rows // 2
    return pl.ds(c * h, h)


def _gathered_shape(shape, flavour):
    L, a, b = shape
    return {"col": (L, a, N_CHIPS * b), "row": (L, N_CHIPS * a, b), "lead": (N_CHIPS, L, a, b)}[flavour]


def _gathered_part(out_ref, shape, flavour, s, rows):
    L, a, b = shape
    if flavour == "col":
        return out_ref.at[:, rows, pl.ds(s * b, b)]
    if flavour == "row":
        return out_ref.at[:, pl.ds(s * a + rows.start, rows.size), :]
    return out_ref.at[s, :, rows, :]


def gather_side(shards, flavours):
    n = len(shards)
    shapes = [w.shape for w in shards]

    def copies(w_refs, out_refs, send_sems, recv_sems, local_sems):
        x, y, c = _me()
        sibling = (x, y, 1 - c)
        chips = _other_chips(x, y)
        me_s = 2 * x + y

        def copy(k, src, dst, to):
            return pltpu.make_async_remote_copy(src_ref=src, dst_ref=dst, send_sem=send_sems.at[k],
                                                recv_sem=recv_sems.at[k], device_id=to, device_id_type=MESH)

        own, first, landed, passed, passed_in = [], [], [], [], []
        for w in range(n):
            shp, fl = shapes[w], flavours[w]
            my_half = _half_rows(shp[1], c)
            sib_half = _half_rows(shp[1], 1 - c)
            own.append(copy(7 * w + 6, w_refs[w], _gathered_part(out_refs[w], shp, fl, me_s, pl.ds(0, shp[1])), sibling))
            for j, chip in enumerate(chips):
                s = 2 * chip[0] + chip[1]
                first.append(copy(7 * w + j, w_refs[w].at[:, my_half, :],
                                  _gathered_part(out_refs[w], shp, fl, me_s, my_half), (*chip, c)))
                part = _gathered_part(out_refs[w], shp, fl, s, my_half)
                landed.append(copy(7 * w + j, part, part, (*chip, c)))
                passed.append(copy(7 * w + 3 + j, part, part, sibling))
                theirs = _gathered_part(out_refs[w], shp, fl, s, sib_half)
                passed_in.append(copy(7 * w + 3 + j, theirs, theirs, sibling))
        return own, first, landed, passed, passed_in

    def start(*refs):
        own, first, _, _, _ = copies(*refs)
        for cp in first + own:
            cp.start()

    def wait(*refs):
        own, first, landed, passed, passed_in = copies(*refs)
        for arrived, onward in zip(landed, passed):
            arrived.wait_recv()
            onward.start()
        for cp in passed_in:
            cp.wait_recv()
        for cp in own:
            cp.wait()
        for cp in first + passed:
            cp.wait_send()

    out_shapes = [jax.ShapeDtypeStruct(_gathered_shape(s, f), w.dtype) for w, s, f in zip(shards, shapes, flavours)]
    return Side(shards, out_shapes, 7 * n, 0, start, wait)


def swap_side(gs):
    n = len(gs)

    def copies(g_refs, out_refs, send_sems, recv_sems, local_sems):
        x, y, c = _me()
        return [pltpu.make_async_remote_copy(
            src_ref=g_refs[w].at[:, _half_rows(gs[w].shape[1], 1 - c), :], dst_ref=out_refs[w],
            send_sem=send_sems.at[w], recv_sem=recv_sems.at[w], device_id=(x, y, 1 - c), device_id_type=MESH)
            for w in range(n)]

    def start(*refs):
        for cp in copies(*refs):
            cp.start()

    def wait(*refs):
        for cp in copies(*refs):
            cp.wait()

    out_shapes = [jax.ShapeDtypeStruct((g.shape[0], g.shape[1] // 2, g.shape[2]), g.dtype) for g in gs]
    return Side(gs, out_shapes, n, 0, start, wait)


def add_half(g, r1, c_idx, name):
    n, R, C = g.shape
    half = R // 2
    tr = _rows_block(half, C)
    nbh = half // tr

    def kern(c_ref, g_ref, r_ref, o_ref):
        o_ref[...] = g_ref[...] + r_ref[...]

    return pl.pallas_call(
        kern, name=name,
        grid_spec=pltpu.PrefetchScalarGridSpec(
            num_scalar_prefetch=1, grid=(n, nbh),
            in_specs=[pl.BlockSpec((1, tr, C), lambda d, r, c_ref: (d, c_ref[0] * nbh + r, 0)),
                      pl.BlockSpec((1, tr, C), lambda d, r, c_ref: (d, r, 0))],
            out_specs=pl.BlockSpec((1, tr, C), lambda d, r, c_ref: (d, r, 0))),
        out_shape=jax.ShapeDtypeStruct((n, half, C), F32),
        compiler_params=_cp(("parallel", "parallel")),
    )(c_idx, g, r1)


def exchange_side(ps):
    n = len(ps)

    def copies(p_refs, out_refs, send_sems, recv_sems, local_sems):
        x, y, c = _me()
        return [pltpu.make_async_remote_copy(
            src_ref=p_refs[w].at[2 * chip[0] + chip[1]], dst_ref=out_refs[w].at[j],
            send_sem=send_sems.at[3 * w + j], recv_sem=recv_sems.at[3 * w + j],
            device_id=(*chip, c), device_id_type=MESH)
            for w in range(n) for j, chip in enumerate(_other_chips(x, y))]

    def start(*refs):
        for cp in copies(*refs):
            cp.start()

    def wait(*refs):
        for cp in copies(*refs):
            cp.wait()

    return Side(ps, [jax.ShapeDtypeStruct((3,) + p.shape[1:], p.dtype) for p in ps], 3 * n, 0, start, wait)


def add_chips(p, r2, chip_idx, name):
    n, H, C = p.shape
    tr = _rows_block(H, C)

    def kern(s_ref, p_ref, r_ref, o_ref):
        o_ref[...] = ((p_ref[0] + r_ref[0]) + r_ref[1]) + r_ref[2]

    return pl.pallas_call(
        kern, name=name,
        grid_spec=pltpu.PrefetchScalarGridSpec(
            num_scalar_prefetch=1, grid=(H // tr,),
            in_specs=[pl.BlockSpec((1, tr, C), lambda r, s_ref: (s_ref[0], r, 0)),
                      pl.BlockSpec((3, tr, C), lambda r, s_ref: (0, r, 0))],
            out_specs=pl.BlockSpec((tr, C), lambda r, s_ref: (r, 0))),
        out_shape=jax.ShapeDtypeStruct((H, C), F32),
        compiler_params=_cp(("parallel",)),
    )(chip_idx, p, r2)


def swap_reduced(ss, name):
    n = len(ss)

    def body(*refs):
        s_refs, out_refs = refs[:n], refs[n:2 * n]
        send_sems, recv_sems = refs[2 * n:]
        x, y, c = _me()
        cps = [pltpu.make_async_remote_copy(src_ref=s_refs[w], dst_ref=out_refs[w], send_sem=send_sems.at[w],
                                            recv_sem=recv_sems.at[w], device_id=(x, y, 1 - c), device_id_type=MESH)
               for w in range(n)]
        for cp in cps:
            cp.start()
        for cp in cps:
            cp.wait()

    any_spec = pl.BlockSpec(memory_space=pl.ANY)
    return pl.pallas_call(
        body, name=name, out_shape=[jax.ShapeDtypeStruct(s.shape, s.dtype) for s in ss],
        in_specs=[any_spec] * n, out_specs=[any_spec] * n,
        scratch_shapes=[pltpu.SemaphoreType.DMA((n,)), pltpu.SemaphoreType.DMA((n,))],
    )(*ss)


BIG = (("gla_w_in", 2, (1024, GLA_IN // N_CHIPS), "lead"), ("gla_w_out", 2, (256, 1024), "row"),
       ("att_w_in", 2, (1024, 768), "col"), ("att_w_out", 2, (256, 1024), "row"),
       ("ff_w1", 4, (1024, 1024), "col"), ("ff_w2", 4, (1024, 1024), "row"))
FLAVOUR = {n: f for n, _, _, f in BIG}


def layer_weights(i):
    mixer = "gla" if i % 2 == 0 else "att"
    return (("in", mixer + "_w_in", i // 2), ("out", mixer + "_w_out", i // 2), ("w1", "ff_w1", i), ("w2", "ff_w2", i))


class Comm:
    def __init__(self, weights, core, chip):
        self.weights, self.core, self.chip = weights, core, chip
        self.c_idx = jnp.reshape(core, (1,)).astype(jnp.int32)
        self.chip_idx = jnp.reshape(chip, (1,)).astype(jnp.int32)
        self.reduced = {}

    def gather(self, items):
        shards = [self.weights[n][l:l + 1].astype(BF16) for _, n, l in items]
        return gather_side(shards, [FLAVOUR[n] for _, n, _ in items])

    def full_weights(self, items, gathered):
        W = {}
        for (role, n, _), w in zip(items, gathered):
            if n == "gla_w_in":
                w = jnp.pad(w.transpose(1, 2, 0, 3).reshape(1, D_MODEL, GLA_IN), ((0, 0), (0, 0), (0, GLA_IN_PAD - GLA_IN)))
            W[role] = (w, 0)
        return W

    def gather_now(self, items, name):
        return self.full_weights(items, run_side(self.gather(items), name))

    def swap(self, items):
        return swap_side([g for _, _, g in items])

    def reduce_begin(self, tag, items, swapped):
        ps = [add_half(g, r, self.c_idx, f"rs_add2_{tag}_{w}") for w, ((_, _, g), r) in enumerate(zip(items, swapped))]
        return tag, [(n, l) for n, l, _ in items], ps

    def exchange(self, pending):
        return exchange_side(pending[2])

    def reduce_mid(self, pending, landed):
        tag, keys, ps = pending
        for w, (key, p, r) in enumerate(zip(keys, ps, landed)):
            self.reduced[key] = add_chips(p, r, self.chip_idx, f"rs_add4_{tag}_{w}")

    def reduce_tail(self, tag, items):
        pending = self.reduce_begin(tag, items, run_side(self.swap(items), f"rs_swap_{tag}"))
        self.reduce_mid(pending, run_side(self.exchange(pending), f"rs_xchg_{tag}"))

    def reduce_end(self):
        keys = [(n, l) for n, L, _, _ in BIG for l in range(L)]
        mine = [self.reduced[k] for k in keys]
        theirs = swap_reduced(mine, "rs_join")
        low = self.core == 0
        full = {k: jnp.concatenate([jnp.where(low, m, t), jnp.where(low, t, m)], axis=0)
                for k, m, t in zip(keys, mine, theirs)}
        return {n: jnp.stack([full[(n, l)] for l in range(L)]) for n, L, _, _ in BIG}


def local_step(x, target, mods, comm, small):
    S, D = x.shape
    row = lambda v: v.reshape(1, -1)
    saved = []
    tiles = [att_bias_tiles(small["att_rel_bias"][j], f"att_tiles_{j}").reshape(ATT_HEADS, ATT_CLASSES, ATT_TQ, ATT_KW)
             for j in range(2)]
    wgk_p = [jnp.pad(small["gla_w_gk2"][j], ((0, 128 - GLA_RANK), (0, 0))).astype(BF16) for j in range(2)]

    u1 = modulate(x, row(mods[0, 1]), row(mods[0, 0]), "mod_first")
    Ws = [dict() for _ in range(DEPTH)]
    items0 = layer_weights(0)
    Ws[0].update(comm.gather_now(items0[:1], "gather_w0"))
    for i in range(DEPTH):
        j = i // 2
        W = Ws[i]
        sh1, sc1, g1, sh2, sc2, g2 = (row(mods[i, k]) for k in range(6))
        nxt = min(i + 1, DEPTH - 1)
        more = i + 1 < DEPTH
        nxt_items = layer_weights(nxt)
        side_in = comm.gather(items0[1:]) if i == 0 else None
        side_mix = comm.gather(nxt_items[:2]) if more else None
        side_ff = comm.gather(nxt_items[2:]) if more else None
        if i % 2 == 0:
            proj = mm_plain(u1, *W["in"], f"gla_in_{i}", side=side_in)
        else:
            proj = mm_plain(u1, *W["in"], f"att_in_{i}", mode="bf16", bias=row(small["att_b_in"][j]), side=side_in)
        proj, landed = proj if side_in is not None else (proj, [])
        if i == 0:
            W.update(comm.full_weights(items0[1:], landed))
        if i % 2 == 0:
            (zmix, states), landed = gla_fwd(proj, wgk_p[j], row(small["gla_b_gk"][j]), row(small["gla_g_norm"][j]),
                                             f"gla_fwd_{i}", side_mix)
        else:
            (zmix,), landed = attn_fwd(proj, tiles[j], f"att_fwd_{i}", side_mix)
            states = None
        if more:
            Ws[nxt].update(comm.full_weights(nxt_items[:2], landed))
        y1, x_mid, u2 = mm_down_ln(zmix, *W["out"], x, 1.0 + g1, row(small["ln_g"][i, 0]), row(small["ln_b"][i, 0]),
                                   sc2, sh2, f"mix_out_{i}")
        act = mm_plain(u2, *W["w1"], f"ff_up_{i}", mode="mlp_up", side=side_ff)
        act, landed = act if side_ff is not None else (act, [])
        if more:
            Ws[nxt].update(comm.full_weights(nxt_items[2:], landed))
        y2, x_out, u_next = mm_down_ln(act, *W["w2"], x_mid, 1.0 + g2, row(small["ln_g"][i, 1]),
                                       row(small["ln_b"][i, 1]), row(mods[nxt, 1]), row(mods[nxt, 0]), f"ff_out_{i}")
        saved.append(dict(x_in=x, u1=u1, proj=proj, zmix=zmix, states=states, y1=y1, x_mid=x_mid, u2=u2,
                          act=act, y2=y2))
        x, u1 = x_out, u_next

    dx, sq = loss_head(x, target, "loss_head")

    g_small = dict(ln_g=[None] * DEPTH, ln_b=[None] * DEPTH, gla_w_gk2=[None] * 2, gla_b_gk=[None] * 2,
                   gla_g_norm=[None] * 2, att_b_in=[None] * 2, att_rel_bias=[None] * 2)
    dmods = [None] * DEPTH
    later = []
    top = saved[DEPTH - 1]
    dz2, dy2, s_ln2 = ln_bwd(dx, top["x_mid"], top["y2"], 1.0 + row(mods[DEPTH - 1, 5]),
                             row(small["ln_g"][DEPTH - 1, 1]), "ln2_bwd_top")

    for i in reversed(range(DEPTH)):
        j = i // 2
        sv = saved[i]
        W = Ws[i]
        sh1, sc1, g1, sh2, sc2, g2 = (row(mods[i, k]) for k in range(6))
        dh = mm_plain(dy2, *W["w2"], f"ff_dn_{i}", mode="mlp_dn", nt=True, h=sv["act"])
        g_w2 = mm_w(sv["act"], dy2, f"ff_w2g_{i}").reshape(N_CHIPS, D_FF // N_CHIPS, D)
        g_w1 = mm_w(sv["u2"], dh, f"ff_w1g_{i}", chips_out=True)
        items = [("ff_w1", i, g_w1), ("ff_w2", i, g_w2)] + later
        (dz1, dy1, s_m2, s_ln1), swapped = mm_down_comb(
            dh, *W["w1"], dz2, sv["x_mid"], 1.0 + sc2, f"ff_dx_{i}",
            ln=(sv["x_in"], sv["y1"], 1.0 + g1, row(small["ln_g"][i, 0])), side=comm.swap(items))
        pending = comm.reduce_begin(i, items, swapped)
        side = comm.exchange(pending)
        mixer = "gla" if i % 2 == 0 else "att"
        below = None
        if i > 0:
            below = (saved[i - 1]["x_mid"], saved[i - 1]["y2"], 1.0 + row(mods[i - 1, 5]), row(small["ln_g"][i - 1, 1]))
        if i % 2 == 0:
            g_out = mm_w(sv["zmix"], dy1, f"gla_wog_{i}").reshape(N_CHIPS, D // N_CHIPS, D)
            dzg = mm_plain(dy1, *W["out"], f"gla_dz_{i}", nt=True)
            (dproj, dwgk, dbgk, dgn), landed = gla_bwd(sv["proj"], sv["states"], dzg, wgk_p[j],
                                                       row(small["gla_b_gk"][j]), row(small["gla_g_norm"][j]),
                                                       f"gla_bwd_{i}", side)
            g_small["gla_w_gk2"][j] = dwgk[:GLA_RANK]
            g_small["gla_b_gk"][j] = dbgk[0]
            g_small["gla_g_norm"][j] = dgn[0].reshape(GLA_HEADS, GLA_DV_HEAD)
            gwi = mm_w(sv["u1"], dproj, f"gla_wig_{i}")[:, :GLA_IN]
            g_in = gwi.reshape(D, N_CHIPS, GLA_IN // N_CHIPS).transpose(1, 0, 2)
            outs, _ = mm_down_comb(dproj, *W["in"], dz1, sv["x_in"], 1.0 + sc1, f"mix_dx_{i}", ln=below)
        else:
            g_out = mm_w(sv["zmix"], dy1, f"att_wog_{i}").reshape(N_CHIPS, D // N_CHIPS, D)
            do = mm_plain(dy1, *W["out"], f"att_do_{i}", mode="bf16", nt=True)
            (dqkv, dbt), landed = attn_bwd(sv["proj"], tiles[j], do, f"att_bwd_{i}", side)
            g_small["att_rel_bias"][j] = att_bias_grad(dbt.reshape(ATT_HEADS * ATT_CLASSES, ATT_TQ, ATT_KW),
                                                       f"att_bias_{i}")
            g_in = mm_w_chips3(sv["u1"], dqkv, f"att_wig_{i}")
            outs, _ = mm_down_comb(dqkv, *W["in"], dz1, sv["x_in"], 1.0 + sc1, f"mix_dx_{i}", parts=3, ln=below)
        s_m1 = outs[1] if below is None else outs[2]
        if i % 2 == 1:
            g_small["att_b_in"][j] = s_m1[2:5].reshape(3 * D)
        comm.reduce_mid(pending, landed)
        later = [(mixer + "_w_in", j, g_in), (mixer + "_w_out", j, g_out)]
        g_small["ln_g"][i] = jnp.stack([s_ln1[0], s_ln2[0]])
        g_small["ln_b"][i] = jnp.stack([s_ln1[1], s_ln2[1]])
        dmods[i] = jnp.stack([s_m1[1], s_m1[0], s_ln1[2], s_m2[1], s_m2[0], s_ln2[2]])
        if below is None:
            dx = outs[0]
        else:
            dz2, dy2, s_ln2 = outs[0], outs[1], outs[3]
    comm.reduce_tail("last", later)

    g_small = {n: jnp.stack(v) for n, v in g_small.items()}
    return sq, dx, jnp.stack(dmods), g_small


SMALL_SHARDED = (("ln_g", (4, 2, 256)), ("ln_b", (4, 2, 256)), ("gla_g_norm", (2, 4, 64)),
                 ("gla_w_gk2", (2, 16, 128)), ("att_b_in", (2, 768)))
SMALL_FULL = dict(ln_g=(4, 2, 1024), ln_b=(4, 2, 1024), gla_g_norm=(2, 4, 256), gla_w_gk2=(2, 16, 512),
                  att_b_in=(2, 3072), gla_b_gk=(2, 512), att_rel_bias=(2, 16, 257))
SMALL_GRAD_ORDER = ("ln_g", "ln_b", "gla_g_norm", "gla_w_gk2", "att_b_in", "gla_b_gk", "att_rel_bias")


def _pack_small(arrs, rows_total):
    parts = []
    for a in arrs:
        flat = a.reshape(-1)
        pad = (-flat.shape[0]) % PACK_W
        parts.append(jnp.pad(flat, (0, pad)).reshape(-1, PACK_W))
    buf = jnp.concatenate(parts, axis=0)
    return jnp.pad(buf, ((0, rows_total - buf.shape[0]), (0, 0)))


def _unpack_small(buf, shapes):
    out, r = [], 0
    for shp in shapes:
        n = 1
        for s in shp:
            n *= s
        nr = (n + PACK_W - 1) // PACK_W
        out.append(buf[..., r:r + nr, :].reshape(buf.shape[:-2] + (nr * PACK_W,))[..., :n].reshape(buf.shape[:-2] + shp))
        r += nr
    return out


def _unshard_last(g4):
    nd = g4.ndim
    perm = tuple(range(1, nd - 1)) + (0, nd - 1)
    t = g4.transpose(perm)
    return t.reshape(t.shape[:-2] + (-1,))


def _shard_last(full, s):
    n = full.shape[-1] // N_CHIPS
    return lax.dynamic_slice_in_dim(full, s * n, n, axis=full.ndim - 1)


WEIGHT_NAMES = ("w_ada", "b_ada", "ln_g", "ln_b", "gla_w_in", "gla_w_gk2", "gla_b_gk", "gla_g_norm", "gla_w_out",
                "att_w_in", "att_b_in", "att_rel_bias", "att_w_out", "ff_w1", "ff_w2")


def kernel(x, c, w_ada, b_ada, ln_g, ln_b, gla_w_in, gla_w_gk2, gla_b_gk, gla_g_norm, gla_w_out, att_w_in, att_b_in, att_rel_bias, att_w_out, ff_w1, ff_w2, loss_target, m_w_ada, m_b_ada, m_ln_g, m_ln_b, m_gla_w_in, m_gla_w_gk2, m_gla_b_gk, m_gla_g_norm, m_gla_w_out, m_att_w_in, m_att_b_in, m_att_rel_bias, m_att_w_out, m_ff_w1, m_ff_w2, v_w_ada, v_b_ada, v_ln_g, v_ln_b, v_gla_w_in, v_gla_w_gk2, v_gla_b_gk, v_gla_g_norm, v_gla_w_out, v_att_w_in, v_att_b_in, v_att_rel_bias, v_att_w_out, v_ff_w1, v_ff_w2):
    weights = dict(w_ada=w_ada, b_ada=b_ada, ln_g=ln_g, ln_b=ln_b, gla_w_in=gla_w_in, gla_w_gk2=gla_w_gk2,
                   gla_b_gk=gla_b_gk, gla_g_norm=gla_g_norm, gla_w_out=gla_w_out, att_w_in=att_w_in,
                   att_b_in=att_b_in, att_rel_bias=att_rel_bias, att_w_out=att_w_out, ff_w1=ff_w1, ff_w2=ff_w2)
    mom1 = dict(w_ada=m_w_ada, b_ada=m_b_ada, ln_g=m_ln_g, ln_b=m_ln_b, gla_w_in=m_gla_w_in, gla_w_gk2=m_gla_w_gk2,
                gla_b_gk=m_gla_b_gk, gla_g_norm=m_gla_g_norm, gla_w_out=m_gla_w_out, att_w_in=m_att_w_in,
                att_b_in=m_att_b_in, att_rel_bias=m_att_rel_bias, att_w_out=m_att_w_out, ff_w1=m_ff_w1, ff_w2=m_ff_w2)
    mom2 = dict(w_ada=v_w_ada, b_ada=v_b_ada, ln_g=v_ln_g, ln_b=v_ln_b, gla_w_in=v_gla_w_in, gla_w_gk2=v_gla_w_gk2,
                gla_b_gk=v_gla_b_gk, gla_g_norm=v_gla_g_norm, gla_w_out=v_gla_w_out, att_w_in=v_att_w_in,
                att_b_in=v_att_b_in, att_rel_bias=v_att_rel_bias, att_w_out=v_att_w_out, ff_w1=v_ff_w1, ff_w2=v_ff_w2)

    ax, ay, ac = lax.axis_index("x"), lax.axis_index("y"), lax.axis_index("c")
    chip = 2 * ax + ay
    dev = 2 * chip + ac
    S = x.shape[1]
    x2 = x.reshape(S, D_MODEL)
    t2 = loss_target.reshape(S, D_MODEL)

    comm = Comm(weights, ac, chip)

    small_rows = 16
    spack = _pack_small([c] + [weights[n] for n, _ in SMALL_SHARDED], small_rows)
    sg = all_gather8(spack, "gather_small").reshape(N_DEV, small_rows, PACK_W)
    parts = _unpack_small(sg, [(1, D_MODEL)] + [shp for _, shp in SMALL_SHARDED])
    c_all = parts[0].reshape(N_DEV, D_MODEL)
    small = {n: _unshard_last(p[0::2]) for (n, _), p in zip(SMALL_SHARDED, parts[1:])}
    small["gla_b_gk"] = gla_b_gk
    small["att_rel_bias"] = att_rel_bias

    c_act = silu_rows(jnp.pad(c_all, ((0, 128 - N_DEV), (0, 0))), "silu_c")
    wa = w_ada.astype(BF16).transpose(1, 0, 2).reshape(1, D_MODEL, DEPTH * 6 * D_MODEL // N_CHIPS)
    mods_part = mm_plain(c_act, wa, 0, "ada_fwd", tm=128)[:N_DEV]
    mg = all_gather8(mods_part, "gather_mods").reshape(N_CHIPS, 2, N_DEV, DEPTH, 6 * D_MODEL // N_CHIPS)
    mods_mine = lax.dynamic_index_in_dim(mg[:, 0], dev, axis=1, keepdims=False)
    mods = mods_mine.transpose(1, 0, 2).reshape(DEPTH, 6 * D_MODEL) + b_ada
    mods = mods.reshape(DEPTH, 6, D_MODEL)

    sq, grad_x, dmods, g_small = local_step(x2, t2, mods, comm, small)
    loss = lax.psum(0.5 * sq[0, 0] / D_MODEL, ("x", "y", "c"))

    g_shard = comm.reduce_end()

    dm_flat = dmods.reshape(DEPTH, 6 * D_MODEL)
    g_rows = 80
    gpack = _pack_small([dm_flat] + [g_small[n] for n in SMALL_GRAD_ORDER], g_rows)
    gg = all_gather8(gpack, "gather_small_grads").reshape(N_DEV, g_rows, PACK_W)
    gsum = sum_over_devices(gg, "sum_small_grads")
    sums = _unpack_small(gsum, [(DEPTH, 6 * D_MODEL)] + [SMALL_FULL[n] for n in SMALL_GRAD_ORDER])
    grads = dict(b_ada=sums[0])
    for n, full_g in zip(SMALL_GRAD_ORDER, sums[1:]):
        grads[n] = full_g if n in ("gla_b_gk", "att_rel_bias") else _shard_last(full_g, chip)
    dm_all = _unpack_small(gg, [(DEPTH, 6 * D_MODEL)])[0]
    dm_cols = _shard_last(dm_all, chip).reshape(N_DEV, DEPTH * 6 * D_MODEL // N_CHIPS)
    dm_cols = jnp.pad(dm_cols, ((0, 128 - N_DEV), (0, 0))).astype(BF16)
    gwa = mm_w(c_act, dm_cols, "ada_bwd", ts=128)
    grads["w_ada"] = gwa.reshape(D_MODEL, DEPTH, 6 * D_MODEL // N_CHIPS).transpose(1, 0, 2)
    grads.update(g_shard)

    deltas, new_m, new_v = {}, {}, {}
    for n in WEIGHT_NAMES:
        deltas[n], new_m[n], new_v[n] = adamw(weights[n], grads[n], mom1[n], mom2[n], "adamw_" + n)

    return (loss, grad_x.reshape(1, S, D_MODEL), *[grads[n] for n in WEIGHT_NAMES], *[deltas[n] for n in WEIGHT_NAMES],
            *[new_m[n] for n in WEIGHT_NAMES], *[new_v[n] for n in WEIGHT_NAMES])
```

```python
import functools

import jax
import jax.numpy as jnp
from jax import lax
from jax.experimental import pallas as pl
from jax.experimental.pallas import tpu as pltpu

F32 = jnp.float32
BF16 = jnp.bfloat16
HIGHEST = lax.Precision.HIGHEST
MESH = pl.DeviceIdType.MESH

D_MODEL = 1024
DEPTH = 4
CHUNK = 64
GLA_HEADS = 4
GLA_DK = 512
GLA_DV = 1024
GLA_DK_HEAD = 128
GLA_DV_HEAD = 256
GLA_RANK = 16
GLA_IN = 3088
GLA_IN_PAD = 3200
GLA_LR_OFF = 3072
ATT_HEADS = 16
ATT_HD = 64
LEFT_CHUNKS = 8
MAX_REL = 128
N_REL = 257
D_FF = 4096
ALPHA = (2.0 * DEPTH) ** 0.25
LN_EPS = 1e-5
RMS_EPS = 1e-6
NEG_INF = -1e30
GLA_SCALE = GLA_DK_HEAD ** -0.5
ATT_SCALE = ATT_HD ** -0.5
ADAM_LR = 0.001
ADAM_B1 = 0.9
ADAM_B2 = 0.999
ADAM_EPS = 1e-08
ADAM_WD = 0.01
ADAM_STEP = 10

ATT_TQ = 256
ATT_KW = 768
GLA_TB = 256
GLA_GROUP = 2
VMEM_LIMIT = 56 * 1024 * 1024
N_CHIPS = 4
N_DEV = 8
PACK_W = 1024


def _dot(a, b):
    return jnp.dot(a, b, preferred_element_type=F32)


def _dot_nt(a, b):
    return lax.dot_general(a, b, (((1,), (1,)), ((), ())), preferred_element_type=F32)


def _dot_tn(a, b):
    return lax.dot_general(a, b, (((0,), (0,)), ((), ())), preferred_element_type=F32)


def _cp(sem, vmem=VMEM_LIMIT):
    return pltpu.CompilerParams(dimension_semantics=sem, vmem_limit_bytes=vmem)


def _row_spec(n):
    return pl.BlockSpec((1, n), lambda *_: (0, 0))


def _sigmoid(x):
    return 1.0 / (1.0 + jnp.exp(-x))


def _log_sigmoid(x):
    return jnp.minimum(x, 0.0) - jnp.log1p(jnp.exp(-jnp.abs(x)))


class Side:
    def __init__(self, ins, out_shapes, n_sems, n_local, start, wait):
        self.ins, self.out_shapes, self.n_sems, self.n_local = list(ins), list(out_shapes), n_sems, n_local
        self.start, self.wait = start, wait

    def sem_shapes(self):
        return [pltpu.SemaphoreType.DMA((self.n_sems,)), pltpu.SemaphoreType.DMA((self.n_sems,)),
                pltpu.SemaphoreType.DMA((max(self.n_local, 1),))]


def run_side(side, name):
    n_in = len(side.ins)
    n_out = len(side.out_shapes)

    def body(*refs):
        ins, outs, sems = refs[:n_in], refs[n_in:n_in + n_out], refs[n_in + n_out:]
        side.start(ins, outs, *sems)
        side.wait(ins, outs, *sems)

    any_spec = pl.BlockSpec(memory_space=pl.ANY)
    return pl.pallas_call(body, name=name, out_shape=side.out_shapes, in_specs=[any_spec] * n_in,
                          out_specs=[any_spec] * n_out, scratch_shapes=side.sem_shapes())(*side.ins)


def hosted_call(main, side, *, name, grid, in_specs, out_specs, out_shape, scratch_shapes, dims, args):
    if side is None:
        outs = pl.pallas_call(main, name=name, grid=grid, in_specs=in_specs, out_specs=out_specs,
                              out_shape=out_shape, scratch_shapes=scratch_shapes, compiler_params=_cp(dims))(*args)
        return list(outs), []
    n_mi, n_mo, n_ms = len(in_specs), len(out_specs), len(scratch_shapes)
    n_si, n_so = len(side.ins), len(side.out_shapes)

    def kern(*refs):
        mi, si = refs[:n_mi], refs[n_mi:n_mi + n_si]
        o0 = n_mi + n_si
        mo, so = refs[o0:o0 + n_mo], refs[o0 + n_mo:o0 + n_mo + n_so]
        s0 = o0 + n_mo + n_so
        ms, sems = refs[s0:s0 + n_ms], refs[s0 + n_ms:]
        ids = [pl.program_id(d) for d in range(len(grid))]
        first = functools.reduce(jnp.logical_and, [i == 0 for i in ids])
        last = functools.reduce(jnp.logical_and, [i == g - 1 for i, g in zip(ids, grid)])

        @pl.when(first)
        def _():
            side.start(si, so, *sems)
        main(*mi, *mo, *ms)

        @pl.when(last)
        def _():
            side.wait(si, so, *sems)

    any_spec = pl.BlockSpec(memory_space=pl.ANY)
    outs = pl.pallas_call(
        kern, name=name, grid=grid, in_specs=list(in_specs) + [any_spec] * n_si,
        out_specs=list(out_specs) + [any_spec] * n_so, out_shape=list(out_shape) + side.out_shapes,
        scratch_shapes=list(scratch_shapes) + side.sem_shapes(),
        compiler_params=_cp(("arbitrary",) * len(grid)))(*args, *side.ins)
    return list(outs[:n_mo]), list(outs[n_mo:])


def modulate(x, sc, sh, name):
    S, D = x.shape
    tm = min(512, S)

    def kern(x_ref, sc_ref, sh_ref, u_ref):
        u_ref[...] = (x_ref[...] * (1.0 + sc_ref[...]) + sh_ref[...]).astype(BF16)

    return pl.pallas_call(
        kern, name=name, grid=(S // tm,),
        in_specs=[pl.BlockSpec((tm, D), lambda i: (i, 0)), _row_spec(D), _row_spec(D)],
        out_specs=pl.BlockSpec((tm, D), lambda i: (i, 0)),
        out_shape=jax.ShapeDtypeStruct((S, D), BF16),
        compiler_params=_cp(("parallel",)),
    )(x, sc, sh)


def loss_head(x, t, name):
    S, D = x.shape
    tm = min(512, S)

    def kern(x_ref, t_ref, dx_ref, l_ref):
        @pl.when(pl.program_id(0) == 0)
        def _():
            l_ref[...] = jnp.zeros_like(l_ref)
        e = x_ref[...] - t_ref[...]
        dx_ref[...] = e * (1.0 / D)
        l_ref[...] += jnp.sum(e * e)

    return pl.pallas_call(
        kern, name=name, grid=(S // tm,),
        in_specs=[pl.BlockSpec((tm, D), lambda i: (i, 0)), pl.BlockSpec((tm, D), lambda i: (i, 0))],
        out_specs=[pl.BlockSpec((tm, D), lambda i: (i, 0)), pl.BlockSpec((8, 128), lambda i: (0, 0))],
        out_shape=[jax.ShapeDtypeStruct((S, D), F32), jax.ShapeDtypeStruct((8, 128), F32)],
        compiler_params=_cp(("arbitrary",)),
    )(x, t)


def silu_rows(c_all, name):
    def kern(c_ref, o_ref):
        c = c_ref[...]
        o_ref[...] = (c * _sigmoid(c)).astype(BF16)

    return pl.pallas_call(kern, name=name, out_shape=jax.ShapeDtypeStruct(c_all.shape, BF16))(c_all)


def sum_over_devices(g, name):
    n, R, C = g.shape

    def kern(g_ref, o_ref):
        acc = g_ref[0]
        for d in range(1, n):
            acc = acc + g_ref[d]
        o_ref[...] = acc

    return pl.pallas_call(kern, name=name, out_shape=jax.ShapeDtypeStruct((R, C), F32))(g)


def _rows_block(R, C, budget=1 << 20):
    if R * C * 4 <= budget or R % 8:
        return R
    tr = max(8, (budget // (C * 4)) // 8 * 8)
    while R % tr:
        tr -= 8
    return tr


def adamw(w, g, m, v, name):
    shape = w.shape
    C = shape[-1]
    R = w.size // C
    w2, g2, m2, v2 = (t.reshape(R, C) for t in (w, g, m, v))
    tr = _rows_block(R, C)
    c1 = 1.0 - ADAM_B1 ** ADAM_STEP
    c2 = 1.0 - ADAM_B2 ** ADAM_STEP

    def kern(w_ref, g_ref, m_ref, v_ref, d_ref, nm_ref, nv_ref):
        gg = g_ref[...]
        nm = ADAM_B1 * m_ref[...] + (1.0 - ADAM_B1) * gg
        nv = ADAM_B2 * v_ref[...] + (1.0 - ADAM_B2) * (gg * gg)
        m_hat = nm / c1
        v_hat = nv / c2
        d_ref[...] = -ADAM_LR * (m_hat / (jnp.sqrt(v_hat) + ADAM_EPS) + ADAM_WD * w_ref[...])
        nm_ref[...] = nm
        nv_ref[...] = nv

    spec = pl.BlockSpec((tr, C), lambda i: (i, 0))
    outs = pl.pallas_call(
        kern, name=name, grid=(R // tr,),
        in_specs=[spec] * 4, out_specs=[spec] * 3,
        out_shape=[jax.ShapeDtypeStruct((R, C), F32)] * 3,
        compiler_params=_cp(("parallel",)),
    )(w2, g2, m2, v2)
    return tuple(o.reshape(shape) for o in outs)


def _tn_for(N):
    for tn in (1024, 768, 640, 512, 384, 256, 128):
        if N % tn == 0:
            return tn
    return N


def mm_plain(a, b3, layer, name, *, mode="f32", nt=False, bias=None, h=None, tm=1024, side=None):
    M, K = a.shape
    N = b3.shape[1] if nt else b3.shape[2]
    tm = min(tm, M)
    tn = _tn_for(N)
    a_spec = pl.BlockSpec((tm, K), lambda j, i: (i, 0))
    if nt:
        b_spec = pl.BlockSpec((None, tn, K), lambda j, i: (layer, j, 0))
    else:
        b_spec = pl.BlockSpec((None, K, tn), lambda j, i: (layer, 0, j))
    o_spec = pl.BlockSpec((tm, tn), lambda j, i: (i, j))
    ins, in_specs = [a, b3], [a_spec, b_spec]
    if bias is not None:
        ins.append(bias)
        in_specs.append(pl.BlockSpec((1, tn), lambda j, i: (0, j)))
    if mode == "mlp_dn":
        ins.append(h)
        in_specs.append(o_spec)
    elif mode not in ("f32", "bf16", "mlp_up"):
        raise ValueError(mode)
    odt = F32 if mode == "f32" else BF16

    def kern(a_ref, b_ref, *rest):
        rest = list(rest)
        bias_ref = rest.pop(0) if bias is not None else None
        h_ref = rest.pop(0) if mode == "mlp_dn" else None
        o_ref = rest.pop(0)
        if nt:
            bt_ref = rest.pop(0)

            @pl.when(pl.program_id(1) == 0)
            def _():
                bt_ref[...] = b_ref[...].T
            acc = _dot(a_ref[...], bt_ref[...])
        else:
            acc = _dot(a_ref[...], b_ref[...])
        if bias_ref is not None:
            acc = acc + bias_ref[...]
        if mode == "mlp_up":
            r = jnp.maximum(acc, 0.0)
            acc = r * r
        elif mode == "mlp_dn":
            acc = acc * (2.0 * jnp.sqrt(h_ref[...].astype(F32)))
        o_ref[...] = acc.astype(odt)

    outs, landed = hosted_call(
        kern, side, name=name, grid=(N // tn, M // tm), in_specs=in_specs, out_specs=[o_spec],
        out_shape=[jax.ShapeDtypeStruct((M, N), odt)],
        scratch_shapes=[pltpu.VMEM((K, tn), BF16)] if nt else [], dims=("parallel", "arbitrary"), args=tuple(ins))
    return outs[0] if side is None else (outs[0], landed)


def mm_down_ln(a, b3, layer, x_in, gate1p, ln_g, ln_b, sc_next, sh_next, name, *, tm=256):
    M, K = a.shape
    D = b3.shape[2]
    tm = min(tm, M)

    def kern(a_ref, b_ref, x_ref, gp_ref, lg_ref, lb_ref, sc_ref, sh_ref, y_ref, xo_ref, u_ref):
        y = _dot(a_ref[...], b_ref[...])
        y_ref[...] = y
        z = ALPHA * x_ref[...] + gp_ref[...] * y
        mu = jnp.mean(z, axis=-1, keepdims=True)
        zc = z - mu
        var = jnp.mean(zc * zc, axis=-1, keepdims=True)
        xo = (zc * lax.rsqrt(var + LN_EPS)) * lg_ref[...] + lb_ref[...]
        xo_ref[...] = xo
        u_ref[...] = (xo * (1.0 + sc_ref[...]) + sh_ref[...]).astype(BF16)

    tile = pl.BlockSpec((tm, D), lambda i: (i, 0))
    return pl.pallas_call(
        kern, name=name, grid=(M // tm,),
        in_specs=[pl.BlockSpec((tm, K), lambda i: (i, 0)), pl.BlockSpec((None, K, D), lambda i: (layer, 0, 0)), tile]
        + [_row_spec(D)] * 5,
        out_specs=[tile, tile, tile],
        out_shape=[jax.ShapeDtypeStruct((M, D), F32)] * 2 + [jax.ShapeDtypeStruct((M, D), BF16)],
        compiler_params=_cp(("parallel",)),
    )(a, b3, x_in, gate1p, ln_g, ln_b, sc_next, sh_next)


def mm_down_comb(a, b3, layer, dz, x_in, sc1p, name, *, parts=1, ln=None, side=None, tm=256):
    D, K = b3.shape[1], b3.shape[2]
    M = a.shape[-2]
    kp = K // parts
    tm = min(tm, M)
    n_ln = 0 if ln is None else 4

    def kern(*refs):
        a_refs = refs[:parts]
        b_ref, dz_ref, x_ref, sp_ref = refs[parts:parts + 4]
        ln_refs = refs[parts + 4:parts + 4 + n_ln]
        outs = refs[parts + 4 + n_ln:]

        @pl.when(pl.program_id(0) == 0)
        def _():
            for o in outs:
                if o.shape[0] == 8:
                    o[...] = jnp.zeros_like(o)
        if parts == 1:
            du = _dot_nt(a_refs[0][...], b_ref[...])
        else:
            du = _dot_nt(a_refs[0][...], b_ref[:, 0:kp])
            for p in range(1, parts):
                du = du + _dot_nt(a_refs[p][...], b_ref[:, p * kp:(p + 1) * kp])
        dx = ALPHA * dz_ref[...] + du * sp_ref[...]
        if ln is None:
            dx_ref, s_ref = outs
            dx_ref[...] = dx
        else:
            dzl_ref, dyl_ref, s_ref, sl_ref = outs
            _ln_bwd_tile(dx, *ln_refs, dzl_ref, dyl_ref, sl_ref)
        s_ref[0:1, :] += jnp.sum(du * x_ref[...], axis=0, keepdims=True)
        s_ref[1:2, :] += jnp.sum(du, axis=0, keepdims=True)
        if parts > 1:
            for p in range(parts):
                s_ref[2 + p:3 + p, :] += jnp.sum(a_refs[p][...].astype(F32), axis=0, keepdims=True)

    tile = pl.BlockSpec((tm, D), lambda i: (i, 0))
    sums = pl.BlockSpec((8, D), lambda i: (0, 0))
    if parts == 1:
        a_ins, a_specs = [a], [pl.BlockSpec((tm, K), lambda i: (i, 0))]
    else:
        assert kp == D and parts <= 6
        a_ins = [a] * parts
        a_specs = [pl.BlockSpec((None, tm, kp), functools.partial(lambda i, p: (p, i, 0), p=p)) for p in range(parts)]
    in_specs = a_specs + [pl.BlockSpec((None, D, K), lambda i: (layer, 0, 0)), tile, tile, _row_spec(D)]
    args = a_ins + [b3, dz, x_in, sc1p]
    if ln is None:
        out_specs = [tile, sums]
        out_shape = [jax.ShapeDtypeStruct((M, D), F32), jax.ShapeDtypeStruct((8, D), F32)]
    else:
        in_specs += [tile, tile, _row_spec(D), _row_spec(D)]
        args += list(ln)
        out_specs = [tile, tile, sums, sums]
        out_shape = [jax.ShapeDtypeStruct((M, D), F32), jax.ShapeDtypeStruct((M, D), BF16),
                     jax.ShapeDtypeStruct((8, D), F32), jax.ShapeDtypeStruct((8, D), F32)]
    return hosted_call(kern, side, name=name, grid=(M // tm,), in_specs=in_specs, out_specs=out_specs,
                       out_shape=out_shape, scratch_shapes=[], dims=("arbitrary",), args=tuple(args))


def mm_w(a, b, name, *, ts=2048, tk=512, chips_out=False, b_parts=1, tn=None):
    S, K = a.shape
    npart = b.shape[-1]
    N = npart * b_parts
    ts = min(ts, S)
    tk = min(tk, K)
    n_chip = N // N_CHIPS
    if tn is None:
        tn = _tn_for(n_chip if chips_out else npart)
    assert npart % tn == 0 and (not chips_out or n_chip % tn == 0)

    def kern(a_ref, b_ref, o_ref):
        @pl.when(pl.program_id(2) == 0)
        def _():
            o_ref[...] = jnp.zeros_like(o_ref)
        o_ref[...] += _dot_tn(a_ref[...], b_ref[...])

    if b_parts == 1:
        b_spec = pl.BlockSpec((ts, tn), lambda k, n, s: (s, n))
    else:
        per = npart // tn
        b_spec = pl.BlockSpec((None, ts, tn), lambda k, n, s: (n // per, s, n % per))
    if chips_out:
        per_chip = n_chip // tn
        o_spec = pl.BlockSpec((None, tk, tn), lambda k, n, s: (n // per_chip, k, n % per_chip))
        out_shape = jax.ShapeDtypeStruct((N_CHIPS, K, n_chip), F32)
    else:
        o_spec = pl.BlockSpec((tk, tn), lambda k, n, s: (k, n))
        out_shape = jax.ShapeDtypeStruct((K, N), F32)
    return pl.pallas_call(
        kern, name=name, grid=(K // tk, N // tn, S // ts),
        in_specs=[pl.BlockSpec((ts, tk), lambda k, n, s: (s, k)), b_spec],
        out_specs=o_spec, out_shape=out_shape,
        compiler_params=_cp(("parallel", "parallel", "arbitrary")),
    )(a, b)


def mm_w_chips3(a, b3, name, *, ts=512):
    S, K = a.shape
    P = b3.shape[2]
    n_chip = 3 * P // N_CHIPS
    ts = min(ts, S)
    pieces = []
    for chip in range(N_CHIPS):
        lo, hi = chip * n_chip, (chip + 1) * n_chip
        while lo < hi:
            part = lo // P
            w = min(hi, (part + 1) * P) - lo
            pieces.append((chip, lo - chip * n_chip, part, lo - part * P, w))
            lo += w

    def kern(a_ref, b_ref, o_ref):
        @pl.when(pl.program_id(0) == 0)
        def _():
            o_ref[...] = jnp.zeros_like(o_ref)
        at = a_ref[...].T
        for chip, oc, part, pc, w in pieces:
            o_ref[chip, :, oc:oc + w] += _dot(at, b_ref[part, :, pc:pc + w])

    return pl.pallas_call(
        kern, name=name, grid=(S // ts,),
        in_specs=[pl.BlockSpec((ts, K), lambda s: (s, 0)), pl.BlockSpec((3, ts, P), lambda s: (0, s, 0))],
        out_specs=pl.BlockSpec((N_CHIPS, K, n_chip), lambda s: (0, 0, 0)),
        out_shape=jax.ShapeDtypeStruct((N_CHIPS, K, n_chip), F32),
        compiler_params=_cp(("arbitrary",)),
    )(a, b3)


def mm_f32(a, b, name):
    def kern(a_ref, b_ref, o_ref):
        o_ref[...] = jnp.dot(a_ref[...], b_ref[...], precision=HIGHEST, preferred_element_type=F32)

    return pl.pallas_call(kern, name=name, out_shape=jax.ShapeDtypeStruct((a.shape[0], b.shape[1]), F32),
                          compiler_params=pltpu.CompilerParams(vmem_limit_bytes=VMEM_LIMIT))(a, b)


def _ln_bwd_tile(dxo_t, x_ref, y_ref, gp_ref, lg_ref, dz_ref, dy_ref, s_ref):
    yv = y_ref[...]
    z = ALPHA * x_ref[...] + gp_ref[...] * yv
    mu = jnp.mean(z, axis=-1, keepdims=True)
    zc = z - mu
    var = jnp.mean(zc * zc, axis=-1, keepdims=True)
    rstd = lax.rsqrt(var + LN_EPS)
    xhat = zc * rstd
    dxh = dxo_t * lg_ref[...]
    dz = rstd * (dxh - jnp.mean(dxh, axis=-1, keepdims=True)
                 - xhat * jnp.mean(dxh * xhat, axis=-1, keepdims=True))
    dz_ref[...] = dz
    dy_ref[...] = (gp_ref[...] * dz).astype(BF16)
    s_ref[0:1, :] += jnp.sum(dxo_t * xhat, axis=0, keepdims=True)
    s_ref[1:2, :] += jnp.sum(dxo_t, axis=0, keepdims=True)
    s_ref[2:3, :] += jnp.sum(dz * yv, axis=0, keepdims=True)


def ln_bwd(dxo, x_in, y, gate1p, ln_g, name, *, tm=256):
    S, D = dxo.shape
    tm = min(tm, S)

    def kern(dxo_ref, x_ref, y_ref, gp_ref, lg_ref, dz_ref, dy_ref, s_ref):
        @pl.when(pl.program_id(0) == 0)
        def _():
            s_ref[...] = jnp.zeros_like(s_ref)
        _ln_bwd_tile(dxo_ref[...], x_ref, y_ref, gp_ref, lg_ref, dz_ref, dy_ref, s_ref)

    tile = pl.BlockSpec((tm, D), lambda i: (i, 0))
    return pl.pallas_call(
        kern, name=name, grid=(S // tm,),
        in_specs=[tile, tile, tile, _row_spec(D), _row_spec(D)],
        out_specs=[tile, tile, pl.BlockSpec((8, D), lambda i: (0, 0))],
        out_shape=[jax.ShapeDtypeStruct((S, D), F32), jax.ShapeDtypeStruct((S, D), BF16),
                   jax.ShapeDtypeStruct((8, D), F32)],
        compiler_params=_cp(("arbitrary",)),
    )(dxo, x_in, y, gate1p, ln_g)


def _tri64():
    r = lax.broadcasted_iota(jnp.int32, (CHUNK, CHUNK), 0)
    c = lax.broadcasted_iota(jnp.int32, (CHUNK, CHUNK), 1)
    return r >= c


def _gla_chunk_common(proj_ref, rows, b, h):
    kc = slice(h * GLA_DK_HEAD, (h + 1) * GLA_DK_HEAD)
    bh = b[:, kc]
    ep = jnp.exp(bh)
    en = jnp.exp(-bh)
    bl = bh[CHUNK - 1:CHUNK, :]
    ee = jnp.exp(bl - bh)
    dec = jnp.exp(bl)
    q = proj_ref[rows, h * GLA_DK_HEAD:(h + 1) * GLA_DK_HEAD] * GLA_SCALE
    k = proj_ref[rows, GLA_DK + h * GLA_DK_HEAD:GLA_DK + (h + 1) * GLA_DK_HEAD]
    v = proj_ref[rows, 2 * GLA_DK + h * GLA_DV_HEAD:2 * GLA_DK + (h + 1) * GLA_DV_HEAD]
    g = proj_ref[rows, 2 * GLA_DK + GLA_DV + h * GLA_DV_HEAD:2 * GLA_DK + GLA_DV + (h + 1) * GLA_DV_HEAD]
    return ep, en, ee, dec, q, k, v, g


def gla_fwd(proj, wgk_p, bgk, gnorm, name, side=None):
    S = proj.shape[0]
    TB = min(GLA_TB, S)
    ncb = TB // CHUNK

    def kern(proj_ref, wgk_ref, bgk_ref, gn_ref, zg_ref, st_ref, state_scr, la_scr):
        @pl.when(pl.program_id(0) == 0)
        def _():
            state_scr[...] = jnp.zeros_like(state_scr)
        lr = proj_ref[:, GLA_LR_OFF:GLA_IN_PAD].astype(BF16)
        gk = _dot(lr, wgk_ref[...]) + bgk_ref[...]
        la_scr[...] = _log_sigmoid(gk) * (1.0 / 16.0)
        lower = _tri64()
        tri = lower.astype(F32)

        def group(gi, carry):
            rows = [pl.ds(pl.multiple_of((gi * GLA_GROUP + g) * CHUNK, CHUNK), CHUNK) for g in range(GLA_GROUP)]
            b = [jnp.dot(tri, la_scr[r, :], precision=HIGHEST, preferred_element_type=F32) for r in rows]
            P = [(g, h) for g in range(GLA_GROUP) for h in range(GLA_HEADS)]
            cm = {p: _gla_chunk_common(proj_ref, rows[p[0]], b[p[0]], p[1]) for p in P}
            qf = {p: (cm[p][4] * cm[p][0]).astype(BF16) for p in P}
            kn = {p: (cm[p][5] * cm[p][1]).astype(BF16) for p in P}
            qn = {p: (cm[p][4] * cm[p][1]).astype(BF16) for p in P}
            kp = {p: (cm[p][5] * cm[p][0]).astype(BF16) for p in P}
            ke = {p: (cm[p][5] * cm[p][2]).astype(BF16) for p in P}
            vb = {p: cm[p][6].astype(BF16) for p in P}
            a_f = {p: _dot_nt(qf[p], kn[p]) for p in P}
            a_b = {p: _dot_nt(qn[p], kp[p]) for p in P}
            upd = {p: _dot_tn(vb[p], ke[p]) for p in P}
            st = {(0, h): state_scr[h] for h in range(GLA_HEADS)}
            for g in range(GLA_GROUP):
                for h in range(GLA_HEADS):
                    st[(g + 1, h)] = st[(g, h)] * cm[(g, h)][3] + upd[(g, h)]
            o_st = {p: _dot_nt(qf[p], st[p].astype(BF16)) for p in P}
            amat = {p: jnp.where(lower, a_f[p], a_b[p]).astype(BF16) for p in P}
            o = {p: _dot(amat[p], vb[p]) + o_st[p] for p in P}
            for g, h in P:
                st_ref[gi * GLA_GROUP + g, h] = st[(g, h)]
            for h in range(GLA_HEADS):
                state_scr[h] = st[(GLA_GROUP, h)]
            for g, h in P:
                gate = cm[(g, h)][7]
                vc = slice(h * GLA_DV_HEAD, (h + 1) * GLA_DV_HEAD)
                r = lax.rsqrt(jnp.mean(o[(g, h)] * o[(g, h)], axis=-1, keepdims=True) + RMS_EPS)
                on = (o[(g, h)] * r) * gn_ref[:, vc]
                zg_ref[rows[g], vc] = (on * (gate * _sigmoid(gate))).astype(BF16)
            return carry

        lax.fori_loop(0, ncb // GLA_GROUP, group, 0)

    return hosted_call(
        kern, side, name=name, grid=(S // TB,),
        in_specs=[pl.BlockSpec((TB, GLA_IN_PAD), lambda i: (i, 0)),
                  pl.BlockSpec((128, GLA_DK), lambda i: (0, 0)), _row_spec(GLA_DK), _row_spec(GLA_DV)],
        out_specs=[pl.BlockSpec((TB, GLA_DV), lambda i: (i, 0)),
                   pl.BlockSpec((ncb, GLA_HEADS, GLA_DV_HEAD, GLA_DK_HEAD), lambda i: (i, 0, 0, 0))],
        out_shape=[jax.ShapeDtypeStruct((S, GLA_DV), BF16),
                   jax.ShapeDtypeStruct((S // CHUNK, GLA_HEADS, GLA_DV_HEAD, GLA_DK_HEAD), F32)],
        scratch_shapes=[pltpu.VMEM((GLA_HEADS, GLA_DV_HEAD, GLA_DK_HEAD), F32), pltpu.VMEM((TB, GLA_DK), F32)],
        dims=("arbitrary",), args=(proj, wgk_p, bgk, gnorm))


def gla_bwd(proj, states, dzg, wgk_p, bgk, gnorm, name, side=None):
    S = proj.shape[0]
    TB = min(GLA_TB, S)
    ncb = TB // CHUNK
    nb = S // TB

    def kern(proj_ref, st_ref, dzg_ref, wgk_ref, bgk_ref, gn_ref,
             dproj_ref, dwgk_ref, dbgk_ref, dgn_ref, dstate_scr, la_scr, gk_scr, dgk_scr):
        @pl.when(pl.program_id(0) == 0)
        def _():
            dstate_scr[...] = jnp.zeros_like(dstate_scr)
            dwgk_ref[...] = jnp.zeros_like(dwgk_ref)
            dbgk_ref[...] = jnp.zeros_like(dbgk_ref)
            dgn_ref[...] = jnp.zeros_like(dgn_ref)
        lr = proj_ref[:, GLA_LR_OFF:GLA_IN_PAD].astype(BF16)
        gk = _dot(lr, wgk_ref[...]) + bgk_ref[...]
        gk_scr[...] = gk
        la_scr[...] = _log_sigmoid(gk) * (1.0 / 16.0)
        lower = _tri64()
        tri = lower.astype(F32)
        r_i = lax.broadcasted_iota(jnp.int32, (CHUNK, CHUNK), 0)
        c_i = lax.broadcasted_iota(jnp.int32, (CHUNK, CHUNK), 1)
        triu = (c_i >= r_i).astype(F32)
        last_row = lax.broadcasted_iota(jnp.int32, (CHUNK, GLA_DK_HEAD), 0) == CHUNK - 1

        def group(gi, carry):
            cs = [ncb - 1 - (gi * GLA_GROUP + g) for g in range(GLA_GROUP)]
            rows = [pl.ds(pl.multiple_of(c * CHUNK, CHUNK), CHUNK) for c in cs]
            b = [jnp.dot(tri, la_scr[r, :], precision=HIGHEST, preferred_element_type=F32) for r in rows]
            P = [(g, h) for g in range(GLA_GROUP) for h in range(GLA_HEADS)]
            kcs = [slice(h * GLA_DK_HEAD, (h + 1) * GLA_DK_HEAD) for h in range(GLA_HEADS)]
            vcs = [slice(h * GLA_DV_HEAD, (h + 1) * GLA_DV_HEAD) for h in range(GLA_HEADS)]
            cm = {p: _gla_chunk_common(proj_ref, rows[p[0]], b[p[0]], p[1]) for p in P}
            ep, en, ee, dec = ({p: cm[p][i] for p in P} for i in range(4))
            qf = {p: cm[p][4] * cm[p][0] for p in P}
            kn = {p: cm[p][5] * cm[p][1] for p in P}
            qn = {p: cm[p][4] * cm[p][1] for p in P}
            kp = {p: cm[p][5] * cm[p][0] for p in P}
            ke = {p: cm[p][5] * cm[p][2] for p in P}
            qf_b, kn_b, qn_b, kp_b, ke_b = ({p: t[p].astype(BF16) for p in P} for t in (qf, kn, qn, kp, ke))
            vb = {p: cm[p][6].astype(BF16) for p in P}
            st = {p: st_ref[cs[p[0]], p[1]] for p in P}
            st_b = {p: st[p].astype(BF16) for p in P}
            a_f = {p: _dot_nt(qf_b[p], kn_b[p]) for p in P}
            a_b = {p: _dot_nt(qn_b[p], kp_b[p]) for p in P}
            o_st = {p: _dot_nt(qf_b[p], st_b[p]) for p in P}
            amat = {p: jnp.where(lower, a_f[p], a_b[p]).astype(BF16) for p in P}
            o = {p: _dot(amat[p], vb[p]) + o_st[p] for p in P}
            do_b, dgs = {}, {}
            for p in P:
                g, h = p
                r = lax.rsqrt(jnp.mean(o[p] * o[p], axis=-1, keepdims=True) + RMS_EPS)
                oh = o[p] * r
                gn = gn_ref[:, vcs[h]]
                gate = cm[p][7]
                sg = _sigmoid(gate)
                dz = dzg_ref[rows[g], vcs[h]]
                don = dz * (gate * sg)
                dgs[p] = dz * (oh * gn) * (sg * (1.0 + gate * (1.0 - sg)))
                dgn_ref[:, vcs[h]] += jnp.sum(don * oh, axis=0, keepdims=True)
                doh = don * gn
                do_b[p] = (r * (doh - oh * jnp.mean(doh * oh, axis=-1, keepdims=True))).astype(BF16)
            da = {p: _dot_nt(do_b[p], vb[p]) for p in P}
            dv_a = {p: _dot_tn(amat[p], do_b[p]) for p in P}
            dqf_st = {p: _dot(do_b[p], st_b[p]) for p in P}
            dst_upd = {p: _dot_tn(do_b[p], qf_b[p]) for p in P}
            dst = {(0, h): dstate_scr[h] for h in range(GLA_HEADS)}
            for g in range(GLA_GROUP):
                for h in range(GLA_HEADS):
                    dst[(g + 1, h)] = dst[(g, h)] * dec[(g, h)] + dst_upd[(g, h)]
            for h in range(GLA_HEADS):
                dstate_scr[h] = dst[(GLA_GROUP, h)]
            dst_b = {p: dst[p].astype(BF16) for p in P}
            dv = {p: dv_a[p] + _dot_nt(ke_b[p], dst_b[p]) for p in P}
            dke = {p: _dot(vb[p], dst_b[p]) for p in P}
            da_f = {p: jnp.where(lower, da[p], 0.0).astype(BF16) for p in P}
            da_b = {p: jnp.where(lower, 0.0, da[p]).astype(BF16) for p in P}
            dqf = {p: _dot(da_f[p], kn_b[p]) + dqf_st[p] for p in P}
            dkn = {p: _dot_tn(da_f[p], qf_b[p]) for p in P}
            dqn = {p: _dot(da_b[p], kp_b[p]) for p in P}
            dkp = {p: _dot_tn(da_b[p], qn_b[p]) for p in P}
            dbs = {}
            for p in P:
                ddec = jnp.sum(dst[p] * st[p], axis=0, keepdims=True)
                db = dqf[p] * qf[p] - dkn[p] * kn[p] - dqn[p] * qn[p] + dkp[p] * kp[p] - dke[p] * ke[p]
                dbl = jnp.sum(dke[p] * ke[p], axis=0, keepdims=True) + ddec * dec[p]
                dbs[p] = db + jnp.where(last_row, dbl, 0.0)
            dla = {p: jnp.dot(triu, dbs[p], precision=HIGHEST, preferred_element_type=F32) for p in P}
            for p in P:
                g, h = p
                dq = (dqf[p] * ep[p] + dqn[p] * en[p]) * GLA_SCALE
                dk = dkn[p] * en[p] + dkp[p] * ep[p] + dke[p] * ee[p]
                dgk_scr[rows[g], kcs[h]] = dla[p] * (1.0 / 16.0) * _sigmoid(-gk_scr[rows[g], kcs[h]])
                dproj_ref[rows[g], kcs[h]] = dq.astype(BF16)
                dproj_ref[rows[g], GLA_DK + h * GLA_DK_HEAD:GLA_DK + (h + 1) * GLA_DK_HEAD] = dk.astype(BF16)
                dproj_ref[rows[g], 2 * GLA_DK + h * GLA_DV_HEAD:2 * GLA_DK + (h + 1) * GLA_DV_HEAD] = dv[p].astype(BF16)
                dproj_ref[rows[g], 2 * GLA_DK + GLA_DV + h * GLA_DV_HEAD:
                          2 * GLA_DK + GLA_DV + (h + 1) * GLA_DV_HEAD] = dgs[p].astype(BF16)
            return carry

        lax.fori_loop(0, ncb // GLA_GROUP, group, 0)
        dgk = dgk_scr[...]
        dgk_b = dgk.astype(BF16)
        dproj_ref[:, GLA_LR_OFF:GLA_IN_PAD] = _dot_nt(dgk_b, wgk_ref[...]).astype(BF16)
        dwgk_ref[...] += _dot_tn(lr, dgk_b)
        dbgk_ref[...] += jnp.sum(dgk, axis=0, keepdims=True)

    rev = lambda i: (nb - 1 - i, 0)
    return hosted_call(
        kern, side, name=name, grid=(nb,),
        in_specs=[pl.BlockSpec((TB, GLA_IN_PAD), rev),
                  pl.BlockSpec((ncb, GLA_HEADS, GLA_DV_HEAD, GLA_DK_HEAD), lambda i: (nb - 1 - i, 0, 0, 0)),
                  pl.BlockSpec((TB, GLA_DV), rev),
                  pl.BlockSpec((128, GLA_DK), lambda i: (0, 0)), _row_spec(GLA_DK), _row_spec(GLA_DV)],
        out_specs=[pl.BlockSpec((TB, GLA_IN_PAD), rev),
                   pl.BlockSpec((128, GLA_DK), lambda i: (0, 0)), _row_spec(GLA_DK), _row_spec(GLA_DV)],
        out_shape=[jax.ShapeDtypeStruct((S, GLA_IN_PAD), BF16), jax.ShapeDtypeStruct((128, GLA_DK), F32),
                   jax.ShapeDtypeStruct((1, GLA_DK), F32), jax.ShapeDtypeStruct((1, GLA_DV), F32)],
        scratch_shapes=[pltpu.VMEM((GLA_HEADS, GLA_DV_HEAD, GLA_DK_HEAD), F32), pltpu.VMEM((TB, GLA_DK), F32),
                        pltpu.VMEM((TB, GLA_DK), F32), pltpu.VMEM((TB, GLA_DK), F32)],
        dims=("arbitrary",), args=(proj, states, dzg, wgk_p, bgk, gnorm))


ATT_TW = 1024
ATT_CLASSES = 3


def _att_window(i):
    return pl.multiple_of(jnp.maximum(i * ATT_TQ - LEFT_CHUNKS * CHUNK, 0), ATT_TQ)


def _att_rel_index():
    e = jnp.arange(ATT_TW)[None, :]
    d = jnp.where(e < ATT_KW, e, e - ATT_TW)
    off = (jnp.arange(ATT_CLASSES) * ATT_TQ)[:, None]
    return jnp.clip(off - d, -MAX_REL, MAX_REL) + MAX_REL


def _row_bits():
    return lax.broadcasted_iota(jnp.int32, (ATT_TQ, ATT_TW), 0)


def att_bias_tiles(rel_bias, name):
    pick = (jnp.arange(384)[:, None] == _att_rel_index().reshape(-1)[None, :]).astype(F32)
    tab = mm_f32(jnp.pad(rel_bias, ((0, 0), (0, 384 - N_REL))), pick, name + "_tab")
    tab = tab.reshape(ATT_HEADS * ATT_CLASSES, 1, ATT_TW)

    def kern(t_ref, o_ref):
        cls = pl.program_id(0) % ATT_CLASSES
        x = jnp.broadcast_to(t_ref[...], (ATT_TQ, ATT_TW))
        row = _row_bits()
        for b in range(8):
            x = jnp.where((row & (1 << b)) != 0, pltpu.roll(x, 1 << b, axis=1), x)
        x = x[:, :ATT_KW]
        qc = cls * (ATT_TQ // CHUNK) + lax.shift_right_arithmetic(
            lax.broadcasted_iota(jnp.int32, (ATT_TQ, ATT_KW), 0), 6)
        kc = lax.shift_right_arithmetic(lax.broadcasted_iota(jnp.int32, (ATT_TQ, ATT_KW), 1), 6)
        o_ref[...] = jnp.where((kc <= qc) & (kc >= qc - LEFT_CHUNKS), x, NEG_INF)

    return pl.pallas_call(
        kern, name=name, grid=(ATT_HEADS * ATT_CLASSES,),
        in_specs=[pl.BlockSpec((None, 1, ATT_TW), lambda i: (i, 0, 0))],
        out_specs=pl.BlockSpec((None, ATT_TQ, ATT_KW), lambda i: (i, 0, 0)),
        out_shape=jax.ShapeDtypeStruct((ATT_HEADS * ATT_CLASSES, ATT_TQ, ATT_KW), F32),
        compiler_params=_cp(("parallel",)),
    )(tab)


def att_bias_grad(dbt, name):
    def kern(d_ref, o_ref):
        x = jnp.concatenate([d_ref[...], jnp.zeros((ATT_TQ, ATT_TW - ATT_KW), F32)], axis=1)
        row = _row_bits()
        for b in range(8):
            x = jnp.where((row & (1 << b)) != 0, pltpu.roll(x, ATT_TW - (1 << b), axis=1), x)
        o_ref[...] = jnp.sum(x, axis=0, keepdims=True)

    diag = pl.pallas_call(
        kern, name=name + "_diag", grid=(ATT_HEADS * ATT_CLASSES,),
        in_specs=[pl.BlockSpec((None, ATT_TQ, ATT_KW), lambda i: (i, 0, 0))],
        out_specs=pl.BlockSpec((None, 1, ATT_TW), lambda i: (i, 0, 0)),
        out_shape=jax.ShapeDtypeStruct((ATT_HEADS * ATT_CLASSES, 1, ATT_TW), F32),
        compiler_params=_cp(("parallel",)),
    )(dbt)
    diag = diag.reshape(ATT_HEADS, ATT_CLASSES * ATT_TW)
    onehot = (_att_rel_index().reshape(-1)[:, None] == jnp.arange(384)[None, :]).astype(F32)
    return mm_f32(diag, onehot, name + "_bins")[:, :N_REL]


def _att_scores(q_ref, kw, bias_ref):
    hs = [slice(hh * ATT_HD, (hh + 1) * ATT_HD) for hh in range(2)]
    q = [q_ref[:, h] * ATT_SCALE for h in hs]
    k = [kw[:, h] for h in hs]
    s = [_dot_nt(q[hh], k[hh]) + bias_ref[hh] for hh in range(2)]
    e = [jnp.exp(t - jnp.max(t, axis=-1, keepdims=True)) for t in s]
    inv = [1.0 / jnp.sum(t, axis=-1, keepdims=True) for t in e]
    return hs, q, k, e, inv


def _att_specs(S):
    nq = D_MODEL // 128
    q_spec = pl.BlockSpec((ATT_TQ, 128), lambda p, i: (i, p))
    k_spec = pl.BlockSpec((S, 128), lambda p, i: (0, nq + p))
    v_spec = pl.BlockSpec((S, 128), lambda p, i: (0, 2 * nq + p))
    b_spec = pl.BlockSpec((2, None, ATT_TQ, ATT_KW), lambda p, i: (p, jnp.minimum(i, ATT_CLASSES - 1), 0, 0))
    return q_spec, k_spec, v_spec, b_spec


def attn_fwd(qkv, bias, name, side=None):
    S = qkv.shape[0]
    q_spec, k_spec, v_spec, b_spec = _att_specs(S)

    def kern(q_ref, k_ref, v_ref, bias_ref, o_ref):
        ws = _att_window(pl.program_id(1))
        kw = k_ref[pl.ds(ws, ATT_KW), :]
        vw = v_ref[pl.ds(ws, ATT_KW), :]
        hs, _, _, e, inv = _att_scores(q_ref, kw, bias_ref)
        outs = [_dot(e[hh].astype(BF16), vw[:, hs[hh]]) * inv[hh] for hh in range(2)]
        o_ref[...] = jnp.concatenate(outs, axis=1).astype(BF16)

    return hosted_call(
        kern, side, name=name, grid=(ATT_HEADS // 2, S // ATT_TQ),
        in_specs=[q_spec, k_spec, v_spec, b_spec],
        out_specs=[pl.BlockSpec((ATT_TQ, 128), lambda p, i: (i, p))],
        out_shape=[jax.ShapeDtypeStruct((S, D_MODEL), BF16)],
        scratch_shapes=[], dims=("parallel", "arbitrary"), args=(qkv, qkv, qkv, bias))


def attn_bwd(qkv, bias, do, name, side=None):
    S = qkv.shape[0]
    nblk = S // ATT_TQ
    q_spec, k_spec, v_spec, b_spec = _att_specs(S)

    def kern(q_ref, k_ref, v_ref, bias_ref, do_ref, dqkv_ref, db_ref, dk_scr, dv_scr):
        i = pl.program_id(1)

        @pl.when(i == 0)
        def _():
            dk_scr[...] = jnp.zeros_like(dk_scr)
            dv_scr[...] = jnp.zeros_like(dv_scr)
            db_ref[...] = jnp.zeros_like(db_ref)
        ws = _att_window(i)
        win = pl.ds(ws, ATT_KW)
        kw = k_ref[win, :]
        vw = v_ref[win, :]
        o_cls = jnp.minimum(i, ATT_CLASSES - 1)
        R2 = range(2)
        hs, q, k, e, inv = _att_scores(q_ref, kw, bias_ref)
        do_h = [do_ref[:, h] for h in hs]
        dp = [_dot_nt(do_h[hh], vw[:, hs[hh]]) for hh in R2]
        p = [e[hh] * inv[hh] for hh in R2]
        dvs = [_dot_tn(p[hh].astype(BF16), do_h[hh]) for hh in R2]
        ds = [p[hh] * (dp[hh] - jnp.sum(p[hh] * dp[hh], axis=-1, keepdims=True)) for hh in R2]
        ds_b = [t.astype(BF16) for t in ds]
        dqs = [_dot(ds_b[hh], k[hh]) * ATT_SCALE for hh in R2]
        dks = [_dot_tn(ds_b[hh], q[hh]) for hh in R2]
        for hh in R2:
            db_ref[hh, o_cls] += ds[hh]
        dqkv_ref[0, pl.ds(pl.multiple_of(i * ATT_TQ, ATT_TQ), ATT_TQ), :] = jnp.concatenate(dqs, axis=1).astype(BF16)
        dk_scr[win, :] += jnp.concatenate(dks, axis=1)
        dv_scr[win, :] += jnp.concatenate(dvs, axis=1)

        @pl.when(i == nblk - 1)
        def _():
            dqkv_ref[1] = dk_scr[...].astype(BF16)
            dqkv_ref[2] = dv_scr[...].astype(BF16)

    return hosted_call(
        kern, side, name=name, grid=(ATT_HEADS // 2, nblk),
        in_specs=[q_spec, k_spec, v_spec, b_spec, pl.BlockSpec((ATT_TQ, 128), lambda p, i: (i, p))],
        out_specs=[pl.BlockSpec((3, S, 128), lambda p, i: (0, 0, p)),
                   pl.BlockSpec((2, ATT_CLASSES, ATT_TQ, ATT_KW), lambda p, i: (p, 0, 0, 0))],
        out_shape=[jax.ShapeDtypeStruct((3, S, D_MODEL), BF16),
                   jax.ShapeDtypeStruct((ATT_HEADS, ATT_CLASSES, ATT_TQ, ATT_KW), F32)],
        scratch_shapes=[pltpu.VMEM((S, 128), F32), pltpu.VMEM((S, 128), F32)],
        dims=("parallel", "arbitrary"), args=(qkv, qkv, qkv, bias, do))


def colsum3(a3, name):
    P, S, N = a3.shape
    tm = min(512, S)

    def kern(a_ref, o_ref):
        @pl.when(pl.program_id(1) == 0)
        def _():
            o_ref[...] = jnp.zeros_like(o_ref)
        o_ref[...] += jnp.sum(a_ref[...].astype(F32), axis=0, keepdims=True)

    return pl.pallas_call(
        kern, name=name, grid=(P, S // tm),
        in_specs=[pl.BlockSpec((None, tm, N), lambda p, i: (p, i, 0))],
        out_specs=pl.BlockSpec((None, 1, N), lambda p, i: (p, 0, 0)),
        out_shape=jax.ShapeDtypeStruct((P, 1, N), F32),
        compiler_params=_cp(("parallel", "arbitrary")),
    )(a3)


def _me():
    return lax.axis_index("x"), lax.axis_index("y"), lax.axis_index("c")


def _other_chips(x, y):
    return [(1 - x, y), (x, 1 - y), (1 - x, 1 - y)]


def all_gather8(x_shard, name):
    m_per, n = x_shard.shape

    def body(x_ref, out_ref, send_sems, recv_sems, local_sem):
        x, y, c = _me()
        me, sibling = (x, y, c), (x, y, 1 - c)
        chips = _other_chips(x, y)

        def rows(px, py, pc):
            return out_ref.at[pl.ds((4 * px + 2 * py + pc) * m_per, m_per), :]

        def copy(k, block, to, src=None):
            return pltpu.make_async_remote_copy(
                src_ref=rows(*block) if src is None else src, dst_ref=rows(*block),
                send_sem=send_sems.at[k], recv_sem=recv_sems.at[k], device_id=to, device_id_type=MESH)

        mine = pltpu.make_async_copy(x_ref, rows(*me), local_sem)
        mine.start()
        first = [copy(0, me, sibling, src=x_ref)]
        first += [copy(1 + j, me, (*chip, c), src=x_ref) for j, chip in enumerate(chips)]
        for cp in first:
            cp.start()
        passed = [copy(4 + j, (*chip, c), sibling) for j, chip in enumerate(chips)]
        for j, chip in enumerate(chips):
            copy(1 + j, (*chip, c), me).wait_recv()
            passed[j].start()
        copy(0, sibling, me).wait_recv()
        for j, chip in enumerate(chips):
            copy(4 + j, (*chip, 1 - c), me).wait_recv()
        for cp in first + passed:
            cp.wait_send()
        mine.wait()

    return pl.pallas_call(
        body, name=name,
        out_shape=jax.ShapeDtypeStruct((N_DEV * m_per, n), x_shard.dtype),
        in_specs=[pl.BlockSpec(memory_space=pltpu.VMEM)],
        out_specs=pl.BlockSpec(memory_space=pltpu.VMEM),
        scratch_shapes=[pltpu.SemaphoreType.DMA((7,)), pltpu.SemaphoreType.DMA((7,)), pltpu.SemaphoreType.DMA],
        compiler_params=pltpu.CompilerParams(vmem_limit_bytes=VMEM_LIMIT),
    )(x_shard)


def _half_rows(n_rows, c):
    h = n_rows // 2
    return pl.ds(c * h, h)


def _gathered_shape(shape, flavour):
    L, a, b = shape
    return {"col": (L, a, N_CHIPS * b), "row": (L, N_CHIPS * a, b), "lead": (N_CHIPS, L, a, b)}[flavour]


def _gathered_part(out_ref, shape, flavour, s, rows):
    L, a, b = shape
    if flavour == "col":
        return out_ref.at[:, rows, pl.ds(s * b, b)]
    if flavour == "row":
        return out_ref.at[:, pl.ds(s * a + rows.start, rows.size), :]
    return out_ref.at[s, :, rows, :]


def gather_side(shards, flavours):
    n = len(shards)
    shapes = [w.shape for w in shards]

    def copies(w_refs, out_refs, send_sems, recv_sems, local_sems):
        x, y, c = _me()
        sibling = (x, y, 1 - c)
        chips = _other_chips(x, y)
        me_s = 2 * x + y

        def copy(k, src, dst, to):
            return pltpu.make_async_remote_copy(src_ref=src, dst_ref=dst, send_sem=send_sems.at[k],
                                                recv_sem=recv_sems.at[k], device_id=to, device_id_type=MESH)

        own, first, landed, passed, passed_in = [], [], [], [], []
        for w in range(n):
            shp, fl = shapes[w], flavours[w]
            my_half = _half_rows(shp[1], c)
            sib_half = _half_rows(shp[1], 1 - c)
            own.append(copy(7 * w + 6, w_refs[w], _gathered_part(out_refs[w], shp, fl, me_s, pl.ds(0, shp[1])), sibling))
            for j, chip in enumerate(chips):
                s = 2 * chip[0] + chip[1]
                first.append(copy(7 * w + j, w_refs[w].at[:, my_half, :],
                                  _gathered_part(out_refs[w], shp, fl, me_s, my_half), (*chip, c)))
                part = _gathered_part(out_refs[w], shp, fl, s, my_half)
                landed.append(copy(7 * w + j, part, part, (*chip, c)))
                passed.append(copy(7 * w + 3 + j, part, part, sibling))
                theirs = _gathered_part(out_refs[w], shp, fl, s, sib_half)
                passed_in.append(copy(7 * w + 3 + j, theirs, theirs, sibling))
        return own, first, landed, passed, passed_in

    def start(*refs):
        own, first, _, _, _ = copies(*refs)
        for cp in first + own:
            cp.start()

    def wait(*refs):
        own, first, landed, passed, passed_in = copies(*refs)
        for arrived, onward in zip(landed, passed):
            arrived.wait_recv()
            onward.start()
        for cp in passed_in:
            cp.wait_recv()
        for cp in own:
            cp.wait()
        for cp in first + passed:
            cp.wait_send()

    out_shapes = [jax.ShapeDtypeStruct(_gathered_shape(s, f), w.dtype) for w, s, f in zip(shards, shapes, flavours)]
    return Side(shards, out_shapes, 7 * n, 0, start, wait)


def swap_side(gs):
    n = len(gs)

    def copies(g_refs, out_refs, send_sems, recv_sems, local_sems):
        x, y, c = _me()
        return [pltpu.make_async_remote_copy(
            src_ref=g_refs[w].at[:, _half_rows(gs[w].shape[1], 1 - c), :], dst_ref=out_refs[w],
            send_sem=send_sems.at[w], recv_sem=recv_sems.at[w], device_id=(x, y, 1 - c), device_id_type=MESH)
            for w in range(n)]

    def start(*refs):
        for cp in copies(*refs):
            cp.start()

    def wait(*refs):
        for cp in copies(*refs):
            cp.wait()

    out_shapes = [jax.ShapeDtypeStruct((g.shape[0], g.shape[1] // 2, g.shape[2]), g.dtype) for g in gs]
    return Side(gs, out_shapes, n, 0, start, wait)


def add_half(g, r1, c_idx, name):
    n, R, C = g.shape
    half = R // 2
    tr = _rows_block(half, C)
    nbh = half // tr

    def kern(c_ref, g_ref, r_ref, o_ref):
        o_ref[...] = g_ref[...] + r_ref[...]

    return pl.pallas_call(
        kern, name=name,
        grid_spec=pltpu.PrefetchScalarGridSpec(
            num_scalar_prefetch=1, grid=(n, nbh),
            in_specs=[pl.BlockSpec((1, tr, C), lambda d, r, c_ref: (d, c_ref[0] * nbh + r, 0)),
                      pl.BlockSpec((1, tr, C), lambda d, r, c_ref: (d, r, 0))],
            out_specs=pl.BlockSpec((1, tr, C), lambda d, r, c_ref: (d, r, 0))),
        out_shape=jax.ShapeDtypeStruct((n, half, C), F32),
        compiler_params=_cp(("parallel", "parallel")),
    )(c_idx, g, r1)


def exchange_side(ps):
    n = len(ps)

    def copies(p_refs, out_refs, send_sems, recv_sems, local_sems):
        x, y, c = _me()
        return [pltpu.make_async_remote_copy(
            src_ref=p_refs[w].at[2 * chip[0] + chip[1]], dst_ref=out_refs[w].at[j],
            send_sem=send_sems.at[3 * w + j], recv_sem=recv_sems.at[3 * w + j],
            device_id=(*chip, c), device_id_type=MESH)
            for w in range(n) for j, chip in enumerate(_other_chips(x, y))]

    def start(*refs):
        for cp in copies(*refs):
            cp.start()

    def wait(*refs):
        for cp in copies(*refs):
            cp.wait()

    return Side(ps, [jax.ShapeDtypeStruct((3,) + p.shape[1:], p.dtype) for p in ps], 3 * n, 0, start, wait)


def add_chips(p, r2, chip_idx, name):
    n, H, C = p.shape
    tr = _rows_block(H, C)

    def kern(s_ref, p_ref, r_ref, o_ref):
        o_ref[...] = ((p_ref[0] + r_ref[0]) + r_ref[1]) + r_ref[2]

    return pl.pallas_call(
        kern, name=name,
        grid_spec=pltpu.PrefetchScalarGridSpec(
            num_scalar_prefetch=1, grid=(H // tr,),
            in_specs=[pl.BlockSpec((1, tr, C), lambda r, s_ref: (s_ref[0], r, 0)),
                      pl.BlockSpec((3, tr, C), lambda r, s_ref: (0, r, 0))],
            out_specs=pl.BlockSpec((tr, C), lambda r, s_ref: (r, 0))),
        out_shape=jax.ShapeDtypeStruct((H, C), F32),
        compiler_params=_cp(("parallel",)),
    )(chip_idx, p, r2)


def swap_reduced(ss, name):
    n = len(ss)

    def body(*refs):
        s_refs, out_refs = refs[:n], refs[n:2 * n]
        send_sems, recv_sems = refs[2 * n:]
        x, y, c = _me()
        cps = [pltpu.make_async_remote_copy(src_ref=s_refs[w], dst_ref=out_refs[w], send_sem=send_sems.at[w],
                                            recv_sem=recv_sems.at[w], device_id=(x, y, 1 - c), device_id_type=MESH)
               for w in range(n)]
        for cp in cps:
            cp.start()
        for cp in cps:
            cp.wait()

    any_spec = pl.BlockSpec(memory_space=pl.ANY)
    return pl.pallas_call(
        body, name=name, out_shape=[jax.ShapeDtypeStruct(s.shape, s.dtype) for s in ss],
        in_specs=[any_spec] * n, out_specs=[any_spec] * n,
        scratch_shapes=[pltpu.SemaphoreType.DMA((n,)), pltpu.SemaphoreType.DMA((n,))],
    )(*ss)


BIG = (("gla_w_in", 2, (1024, GLA_IN // N_CHIPS), "lead"), ("gla_w_out", 2, (256, 1024), "row"),
       ("att_w_in", 2, (1024, 768), "col"), ("att_w_out", 2, (256, 1024), "row"),
       ("ff_w1", 4, (1024, 1024), "col"), ("ff_w2", 4, (1024, 1024), "row"))
FLAVOUR = {n: f for n, _, _, f in BIG}


def layer_weights(i):
    mixer = "gla" if i % 2 == 0 else "att"
    return (("in", mixer + "_w_in", i // 2), ("out", mixer + "_w_out", i // 2), ("w1", "ff_w1", i), ("w2", "ff_w2", i))


class Comm:
    def __init__(self, weights, core, chip):
        self.weights, self.core, self.chip = weights, core, chip
        self.c_idx = jnp.reshape(core, (1,)).astype(jnp.int32)
        self.chip_idx = jnp.reshape(chip, (1,)).astype(jnp.int32)
        self.reduced = {}

    def gather(self, items):
        shards = [self.weights[n][l:l + 1].astype(BF16) for _, n, l in items]
        return gather_side(shards, [FLAVOUR[n] for _, n, _ in items])

    def full_weights(self, items, gathered):
        W = {}
        for (role, n, _), w in zip(items, gathered):
            if n == "gla_w_in":
                w = jnp.pad(w.transpose(1, 2, 0, 3).reshape(1, D_MODEL, GLA_IN), ((0, 0), (0, 0), (0, GLA_IN_PAD - GLA_IN)))
            W[role] = (w, 0)
        return W

    def gather_now(self, items, name):
        return self.full_weights(items, run_side(self.gather(items), name))

    def swap(self, items):
        return swap_side([g for _, _, g in items])

    def reduce_begin(self, tag, items, swapped):
        ps = [add_half(g, r, self.c_idx, f"rs_add2_{tag}_{w}") for w, ((_, _, g), r) in enumerate(zip(items, swapped))]
        return tag, [(n, l) for n, l, _ in items], ps

    def exchange(self, pending):
        return exchange_side(pending[2])

    def reduce_mid(self, pending, landed):
        tag, keys, ps = pending
        for w, (key, p, r) in enumerate(zip(keys, ps, landed)):
            self.reduced[key] = add_chips(p, r, self.chip_idx, f"rs_add4_{tag}_{w}")

    def reduce_tail(self, tag, items):
        pending = self.reduce_begin(tag, items, run_side(self.swap(items), f"rs_swap_{tag}"))
        self.reduce_mid(pending, run_side(self.exchange(pending), f"rs_xchg_{tag}"))

    def reduce_end(self):
        keys = [(n, l) for n, L, _, _ in BIG for l in range(L)]
        mine = [self.reduced[k] for k in keys]
        theirs = swap_reduced(mine, "rs_join")
        low = self.core == 0
        full = {k: jnp.concatenate([jnp.where(low, m, t), jnp.where(low, t, m)], axis=0)
                for k, m, t in zip(keys, mine, theirs)}
        return {n: jnp.stack([full[(n, l)] for l in range(L)]) for n, L, _, _ in BIG}


def local_step(x, target, mods, comm, small):
    S, D = x.shape
    row = lambda v: v.reshape(1, -1)
    saved = []
    tiles = [att_bias_tiles(small["att_rel_bias"][j], f"att_tiles_{j}").reshape(ATT_HEADS, ATT_CLASSES, ATT_TQ, ATT_KW)
             for j in range(2)]
    wgk_p = [jnp.pad(small["gla_w_gk2"][j], ((0, 128 - GLA_RANK), (0, 0))).astype(BF16) for j in range(2)]

    u1 = modulate(x, row(mods[0, 1]), row(mods[0, 0]), "mod_first")
    Ws = [dict() for _ in range(DEPTH)]
    items0 = layer_weights(0)
    Ws[0].update(comm.gather_now(items0[:1], "gather_w0"))
    for i in range(DEPTH):
        j = i // 2
        W = Ws[i]
        sh1, sc1, g1, sh2, sc2, g2 = (row(mods[i, k]) for k in range(6))
        nxt = min(i + 1, DEPTH - 1)
        more = i + 1 < DEPTH
        nxt_items = layer_weights(nxt)
        side_in = comm.gather(items0[1:]) if i == 0 else None
        side_mix = comm.gather(nxt_items[:2]) if more else None
        side_ff = comm.gather(nxt_items[2:]) if more else None
        if i % 2 == 0:
            proj = mm_plain(u1, *W["in"], f"gla_in_{i}", side=side_in)
        else:
            proj = mm_plain(u1, *W["in"], f"att_in_{i}", mode="bf16", bias=row(small["att_b_in"][j]), side=side_in)
        proj, landed = proj if side_in is not None else (proj, [])
        if i == 0:
            W.update(comm.full_weights(items0[1:], landed))
        if i % 2 == 0:
            (zmix, states), landed = gla_fwd(proj, wgk_p[j], row(small["gla_b_gk"][j]), row(small["gla_g_norm"][j]),
                                             f"gla_fwd_{i}", side_mix)
        else:
            (zmix,), landed = attn_fwd(proj, tiles[j], f"att_fwd_{i}", side_mix)
            states = None
        if more:
            Ws[nxt].update(comm.full_weights(nxt_items[:2], landed))
        y1, x_mid, u2 = mm_down_ln(zmix, *W["out"], x, 1.0 + g1, row(small["ln_g"][i, 0]), row(small["ln_b"][i, 0]),
                                   sc2, sh2, f"mix_out_{i}")
        act = mm_plain(u2, *W["w1"], f"ff_up_{i}", mode="mlp_up", side=side_ff)
        act, landed = act if side_ff is not None else (act, [])
        if more:
            Ws[nxt].update(comm.full_weights(nxt_items[2:], landed))
        y2, x_out, u_next = mm_down_ln(act, *W["w2"], x_mid, 1.0 + g2, row(small["ln_g"][i, 1]),
                                       row(small["ln_b"][i, 1]), row(mods[nxt, 1]), row(mods[nxt, 0]), f"ff_out_{i}")
        saved.append(dict(x_in=x, u1=u1, proj=proj, zmix=zmix, states=states, y1=y1, x_mid=x_mid, u2=u2,
                          act=act, y2=y2))
        x, u1 = x_out, u_next

    dx, sq = loss_head(x, target, "loss_head")

    g_small = dict(ln_g=[None] * DEPTH, ln_b=[None] * DEPTH, gla_w_gk2=[None] * 2, gla_b_gk=[None] * 2,
                   gla_g_norm=[None] * 2, att_b_in=[None] * 2, att_rel_bias=[None] * 2)
    dmods = [None] * DEPTH
    later = []
    top = saved[DEPTH - 1]
    dz2, dy2, s_ln2 = ln_bwd(dx, top["x_mid"], top["y2"], 1.0 + row(mods[DEPTH - 1, 5]),
                             row(small["ln_g"][DEPTH - 1, 1]), "ln2_bwd_top")

    for i in reversed(range(DEPTH)):
        j = i // 2
        sv = saved[i]
        W = Ws[i]
        sh1, sc1, g1, sh2, sc2, g2 = (row(mods[i, k]) for k in range(6))
        dh = mm_plain(dy2, *W["w2"], f"ff_dn_{i}", mode="mlp_dn", nt=True, h=sv["act"])
        g_w2 = mm_w(sv["act"], dy2, f"ff_w2g_{i}").reshape(N_CHIPS, D_FF // N_CHIPS, D)
        g_w1 = mm_w(sv["u2"], dh, f"ff_w1g_{i}", chips_out=True)
        items = [("ff_w1", i, g_w1), ("ff_w2", i, g_w2)] + later
        (dz1, dy1, s_m2, s_ln1), swapped = mm_down_comb(
            dh, *W["w1"], dz2, sv["x_mid"], 1.0 + sc2, f"ff_dx_{i}",
            ln=(sv["x_in"], sv["y1"], 1.0 + g1, row(small["ln_g"][i, 0])), side=comm.swap(items))
        pending = comm.reduce_begin(i, items, swapped)
        side = comm.exchange(pending)
        mixer = "gla" if i % 2 == 0 else "att"
        below = None
        if i > 0:
            below = (saved[i - 1]["x_mid"], saved[i - 1]["y2"], 1.0 + row(mods[i - 1, 5]), row(small["ln_g"][i - 1, 1]))
        if i % 2 == 0:
            g_out = mm_w(sv["zmix"], dy1, f"gla_wog_{i}").reshape(N_CHIPS, D // N_CHIPS, D)
            dzg = mm_plain(dy1, *W["out"], f"gla_dz_{i}", nt=True)
            (dproj, dwgk, dbgk, dgn), landed = gla_bwd(sv["proj"], sv["states"], dzg, wgk_p[j],
                                                       row(small["gla_b_gk"][j]), row(small["gla_g_norm"][j]),
                                                       f"gla_bwd_{i}", side)
            g_small["gla_w_gk2"][j] = dwgk[:GLA_RANK]
            g_small["gla_b_gk"][j] = dbgk[0]
            g_small["gla_g_norm"][j] = dgn[0].reshape(GLA_HEADS, GLA_DV_HEAD)
            gwi = mm_w(sv["u1"], dproj, f"gla_wig_{i}")[:, :GLA_IN]
            g_in = gwi.reshape(D, N_CHIPS, GLA_IN // N_CHIPS).transpose(1, 0, 2)
            outs, _ = mm_down_comb(dproj, *W["in"], dz1, sv["x_in"], 1.0 + sc1, f"mix_dx_{i}", ln=below)
        else:
            g_out = mm_w(sv["zmix"], dy1, f"att_wog_{i}").reshape(N_CHIPS, D // N_CHIPS, D)
            do = mm_plain(dy1, *W["out"], f"att_do_{i}", mode="bf16", nt=True)
            (dqkv, dbt), landed = attn_bwd(sv["proj"], tiles[j], do, f"att_bwd_{i}", side)
            g_small["att_rel_bias"][j] = att_bias_grad(dbt.reshape(ATT_HEADS * ATT_CLASSES, ATT_TQ, ATT_KW),
                                                       f"att_bias_{i}")
            g_in = mm_w_chips3(sv["u1"], dqkv, f"att_wig_{i}")
            outs, _ = mm_down_comb(dqkv, *W["in"], dz1, sv["x_in"], 1.0 + sc1, f"mix_dx_{i}", parts=3, ln=below)
        s_m1 = outs[1] if below is None else outs[2]
        if i % 2 == 1:
            g_small["att_b_in"][j] = s_m1[2:5].reshape(3 * D)
        comm.reduce_mid(pending, landed)
        later = [(mixer + "_w_in", j, g_in), (mixer + "_w_out", j, g_out)]
        g_small["ln_g"][i] = jnp.stack([s_ln1[0], s_ln2[0]])
        g_small["ln_b"][i] = jnp.stack([s_ln1[1], s_ln2[1]])
        dmods[i] = jnp.stack([s_m1[1], s_m1[0], s_ln1[2], s_m2[1], s_m2[0], s_ln2[2]])
        if below is None:
            dx = outs[0]
        else:
            dz2, dy2, s_ln2 = outs[0], outs[1], outs[3]
    comm.reduce_tail("last", later)

    g_small = {n: jnp.stack(v) for n, v in g_small.items()}
    return sq, dx, jnp.stack(dmods), g_small


SMALL_SHARDED = (("ln_g", (4, 2, 256)), ("ln_b", (4, 2, 256)), ("gla_g_norm", (2, 4, 64)),
                 ("gla_w_gk2", (2, 16, 128)), ("att_b_in", (2, 768)))
SMALL_FULL = dict(ln_g=(4, 2, 1024), ln_b=(4, 2, 1024), gla_g_norm=(2, 4, 256), gla_w_gk2=(2, 16, 512),
                  att_b_in=(2, 3072), gla_b_gk=(2, 512), att_rel_bias=(2, 16, 257))
SMALL_GRAD_ORDER = ("ln_g", "ln_b", "gla_g_norm", "gla_w_gk2", "att_b_in", "gla_b_gk", "att_rel_bias")


def _pack_small(arrs, rows_total):
    parts = []
    for a in arrs:
        flat = a.reshape(-1)
        pad = (-flat.shape[0]) % PACK_W
        parts.append(jnp.pad(flat, (0, pad)).reshape(-1, PACK_W))
    buf = jnp.concatenate(parts, axis=0)
    return jnp.pad(buf, ((0, rows_total - buf.shape[0]), (0, 0)))


def _unpack_small(buf, shapes):
    out, r = [], 0
    for shp in shapes:
        n = 1
        for s in shp:
            n *= s
        nr = (n + PACK_W - 1) // PACK_W
        out.append(buf[..., r:r + nr, :].reshape(buf.shape[:-2] + (nr * PACK_W,))[..., :n].reshape(buf.shape[:-2] + shp))
        r += nr
    return out


def _unshard_last(g4):
    nd = g4.ndim
    perm = tuple(range(1, nd - 1)) + (0, nd - 1)
    t = g4.transpose(perm)
    return t.reshape(t.shape[:-2] + (-1,))


def _shard_last(full, s):
    n = full.shape[-1] // N_CHIPS
    return lax.dynamic_slice_in_dim(full, s * n, n, axis=full.ndim - 1)


WEIGHT_NAMES = ("w_ada", "b_ada", "ln_g", "ln_b", "gla_w_in", "gla_w_gk2", "gla_b_gk", "gla_g_norm", "gla_w_out",
                "att_w_in", "att_b_in", "att_rel_bias", "att_w_out", "ff_w1", "ff_w2")


def kernel(x, c, w_ada, b_ada, ln_g, ln_b, gla_w_in, gla_w_gk2, gla_b_gk, gla_g_norm, gla_w_out, att_w_in, att_b_in, att_rel_bias, att_w_out, ff_w1, ff_w2, loss_target, m_w_ada, m_b_ada, m_ln_g, m_ln_b, m_gla_w_in, m_gla_w_gk2, m_gla_b_gk, m_gla_g_norm, m_gla_w_out, m_att_w_in, m_att_b_in, m_att_rel_bias, m_att_w_out, m_ff_w1, m_ff_w2, v_w_ada, v_b_ada, v_ln_g, v_ln_b, v_gla_w_in, v_gla_w_gk2, v_gla_b_gk, v_gla_g_norm, v_gla_w_out, v_att_w_in, v_att_b_in, v_att_rel_bias, v_att_w_out, v_ff_w1, v_ff_w2):
    weights = dict(w_ada=w_ada, b_ada=b_ada, ln_g=ln_g, ln_b=ln_b, gla_w_in=gla_w_in, gla_w_gk2=gla_w_gk2,
                   gla_b_gk=gla_b_gk, gla_g_norm=gla_g_norm, gla_w_out=gla_w_out, att_w_in=att_w_in,
                   att_b_in=att_b_in, att_rel_bias=att_rel_bias, att_w_out=att_w_out, ff_w1=ff_w1, ff_w2=ff_w2)
    mom1 = dict(w_ada=m_w_ada, b_ada=m_b_ada, ln_g=m_ln_g, ln_b=m_ln_b, gla_w_in=m_gla_w_in, gla_w_gk2=m_gla_w_gk2,
                gla_b_gk=m_gla_b_gk, gla_g_norm=m_gla_g_norm, gla_w_out=m_gla_w_out, att_w_in=m_att_w_in,
                att_b_in=m_att_b_in, att_rel_bias=m_att_rel_bias, att_w_out=m_att_w_out, ff_w1=m_ff_w1, ff_w2=m_ff_w2)
    mom2 = dict(w_ada=v_w_ada, b_ada=v_b_ada, ln_g=v_ln_g, ln_b=v_ln_b, gla_w_in=v_gla_w_in, gla_w_gk2=v_gla_w_gk2,
                gla_b_gk=v_gla_b_gk, gla_g_norm=v_gla_g_norm, gla_w_out=v_gla_w_out, att_w_in=v_att_w_in,
                att_b_in=v_att_b_in, att_rel_bias=v_att_rel_bias, att_w_out=v_att_w_out, ff_w1=v_ff_w1, ff_w2=v_ff_w2)

    ax, ay, ac = lax.axis_index("x"), lax.axis_index("y"), lax.axis_index("c")
    chip = 2 * ax + ay
    dev = 2 * chip + ac
    S = x.shape[1]
    x2 = x.reshape(S, D_MODEL)
    t2 = loss_target.reshape(S, D_MODEL)

    comm = Comm(weights, ac, chip)

    small_rows = 16
    spack = _pack_small([c] + [weights[n] for n, _ in SMALL_SHARDED], small_rows)
    sg = all_gather8(spack, "gather_small").reshape(N_DEV, small_rows, PACK_W)
    parts = _unpack_small(sg, [(1, D_MODEL)] + [shp for _, shp in SMALL_SHARDED])
    c_all = parts[0].reshape(N_DEV, D_MODEL)
    small = {n: _unshard_last(p[0::2]) for (n, _), p in zip(SMALL_SHARDED, parts[1:])}
    small["gla_b_gk"] = gla_b_gk
    small["att_rel_bias"] = att_rel_bias

    c_act = silu_rows(jnp.pad(c_all, ((0, 128 - N_DEV), (0, 0))), "silu_c")
    wa = w_ada.astype(BF16).transpose(1, 0, 2).reshape(1, D_MODEL, DEPTH * 6 * D_MODEL // N_CHIPS)
    mods_part = mm_plain(c_act, wa, 0, "ada_fwd", tm=128)[:N_DEV]
    mg = all_gather8(mods_part, "gather_mods").reshape(N_CHIPS, 2, N_DEV, DEPTH, 6 * D_MODEL // N_CHIPS)
    mods_mine = lax.dynamic_index_in_dim(mg[:, 0], dev, axis=1, keepdims=False)
    mods = mods_mine.transpose(1, 0, 2).reshape(DEPTH, 6 * D_MODEL) + b_ada
    mods = mods.reshape(DEPTH, 6, D_MODEL)

    sq, grad_x, dmods, g_small = local_step(x2, t2, mods, comm, small)
    loss = lax.psum(0.5 * sq[0, 0] / D_MODEL, ("x", "y", "c"))

    g_shard = comm.reduce_end()

    dm_flat = dmods.reshape(DEPTH, 6 * D_MODEL)
    g_rows = 80
    gpack = _pack_small([dm_flat] + [g_small[n] for n in SMALL_GRAD_ORDER], g_rows)
    gg = all_gather8(gpack, "gather_small_grads").reshape(N_DEV, g_rows, PACK_W)
    gsum = sum_over_devices(gg, "sum_small_grads")
    sums = _unpack_small(gsum, [(DEPTH, 6 * D_MODEL)] + [SMALL_FULL[n] for n in SMALL_GRAD_ORDER])
    grads = dict(b_ada=sums[0])
    for n, full_g in zip(SMALL_GRAD_ORDER, sums[1:]):
        grads[n] = full_g if n in ("gla_b_gk", "att_rel_bias") else _shard_last(full_g, chip)
    dm_all = _unpack_small(gg, [(DEPTH, 6 * D_MODEL)])[0]
    dm_cols = _shard_last(dm_all, chip).reshape(N_DEV, DEPTH * 6 * D_MODEL // N_CHIPS)
    dm_cols = jnp.pad(dm_cols, ((0, 128 - N_DEV), (0, 0))).astype(BF16)
    gwa = mm_w(c_act, dm_cols, "ada_bwd", ts=128)
    grads["w_ada"] = gwa.reshape(D_MODEL, DEPTH, 6 * D_MODEL // N_CHIPS).transpose(1, 0, 2)
    grads.update(g_shard)

    deltas, new_m, new_v = {}, {}, {}
    for n in WEIGHT_NAMES:
        deltas[n], new_m[n], new_v[n] = adamw(weights[n], grads[n], mom1[n], mom2[n], "adamw_" + n)

    return (loss, grad_x.reshape(1, S, D_MODEL), *[grads[n] for n in WEIGHT_NAMES], *[deltas[n] for n in WEIGHT_NAMES],
            *[new_m[n] for n in WEIGHT_NAMES], *[new_v[n] for n in WEIGHT_NAMES])
```

```python
import functools

import jax
import jax.numpy as jnp
from jax import lax
from jax.experimental import pallas as pl
from jax.experimental.pallas import tpu as pltpu

F32 = jnp.float32
BF16 = jnp.bfloat16
HIGHEST = lax.Precision.HIGHEST
MESH = pl.DeviceIdType.MESH

D_MODEL = 1024
DEPTH = 4
CHUNK = 64
GLA_HEADS = 4
GLA_DK = 512
GLA_DV = 1024
GLA_DK_HEAD = 128
GLA_DV_HEAD = 256
GLA_RANK = 16
GLA_IN = 3088
GLA_IN_PAD = 3200
GLA_LR_OFF = 3072
ATT_HEADS = 16
ATT_HD = 64
LEFT_CHUNKS = 8
MAX_REL = 128
N_REL = 257
D_FF = 4096
ALPHA = (2.0 * DEPTH) ** 0.25
LN_EPS = 1e-5
RMS_EPS = 1e-6
NEG_INF = -1e30
GLA_SCALE = GLA_DK_HEAD ** -0.5
ATT_SCALE = ATT_HD ** -0.5
ADAM_LR = 0.001
ADAM_B1 = 0.9
ADAM_B2 = 0.999
ADAM_EPS = 1e-08
ADAM_WD = 0.01
ADAM_STEP = 10

ATT_TQ = 256
ATT_KW = 768
GLA_TB = 256
GLA_GROUP = 2
VMEM_LIMIT = 56 * 1024 * 1024
N_CHIPS = 4
N_DEV = 8
PACK_W = 1024


def _dot(a, b):
    return jnp.dot(a, b, preferred_element_type=F32)


def _dot_nt(a, b):
    return lax.dot_general(a, b, (((1,), (1,)), ((), ())), preferred_element_type=F32)


def _dot_tn(a, b):
    return lax.dot_general(a, b, (((0,), (0,)), ((), ())), preferred_element_type=F32)


def _cp(sem, vmem=VMEM_LIMIT):
    return pltpu.CompilerParams(dimension_semantics=sem, vmem_limit_bytes=vmem)


def _row_spec(n):
    return pl.BlockSpec((1, n), lambda *_: (0, 0))


def _sigmoid(x):
    return 1.0 / (1.0 + jnp.exp(-x))


def _log_sigmoid(x):
    return jnp.minimum(x, 0.0) - jnp.log1p(jnp.exp(-jnp.abs(x)))


class Side:
    def __init__(self, ins, out_shapes, n_sems, n_local, start, wait):
        self.ins, self.out_shapes, self.n_sems, self.n_local = list(ins), list(out_shapes), n_sems, n_local
        self.start, self.wait = start, wait

    def sem_shapes(self):
        return [pltpu.SemaphoreType.DMA((self.n_sems,)), pltpu.SemaphoreType.DMA((self.n_sems,)),
                pltpu.SemaphoreType.DMA((max(self.n_local, 1),))]


def run_side(side, name):
    n_in = len(side.ins)
    n_out = len(side.out_shapes)

    def body(*refs):
        ins, outs, sems = refs[:n_in], refs[n_in:n_in + n_out], refs[n_in + n_out:]
        side.start(ins, outs, *sems)
        side.wait(ins, outs, *sems)

    any_spec = pl.BlockSpec(memory_space=pl.ANY)
    return pl.pallas_call(body, name=name, out_shape=side.out_shapes, in_specs=[any_spec] * n_in,
                          out_specs=[any_spec] * n_out, scratch_shapes=side.sem_shapes())(*side.ins)


def hosted_call(main, side, *, name, grid, in_specs, out_specs, out_shape, scratch_shapes, dims, args):
    if side is None:
        outs = pl.pallas_call(main, name=name, grid=grid, in_specs=in_specs, out_specs=out_specs,
                              out_shape=out_shape, scratch_shapes=scratch_shapes, compiler_params=_cp(dims))(*args)
        return list(outs), []
    n_mi, n_mo, n_ms = len(in_specs), len(out_specs), len(scratch_shapes)
    n_si, n_so = len(side.ins), len(side.out_shapes)

    def kern(*refs):
        mi, si = refs[:n_mi], refs[n_mi:n_mi + n_si]
        o0 = n_mi + n_si
        mo, so = refs[o0:o0 + n_mo], refs[o0 + n_mo:o0 + n_mo + n_so]
        s0 = o0 + n_mo + n_so
        ms, sems = refs[s0:s0 + n_ms], refs[s0 + n_ms:]
        ids = [pl.program_id(d) for d in range(len(grid))]
        first = functools.reduce(jnp.logical_and, [i == 0 for i in ids])
        last = functools.reduce(jnp.logical_and, [i == g - 1 for i, g in zip(ids, grid)])

        @pl.when(first)
        def _():
            side.start(si, so, *sems)
        main(*mi, *mo, *ms)

        @pl.when(last)
        def _():
            side.wait(si, so, *sems)

    any_spec = pl.BlockSpec(memory_space=pl.ANY)
    outs = pl.pallas_call(
        kern, name=name, grid=grid, in_specs=list(in_specs) + [any_spec] * n_si,
        out_specs=list(out_specs) + [any_spec] * n_so, out_shape=list(out_shape) + side.out_shapes,
        scratch_shapes=list(scratch_shapes) + side.sem_shapes(),
        compiler_params=_cp(("arbitrary",) * len(grid)))(*args, *side.ins)
    return list(outs[:n_mo]), list(outs[n_mo:])


def modulate(x, sc, sh, name):
    S, D = x.shape
    tm = min(512, S)

    def kern(x_ref, sc_ref, sh_ref, u_ref):
        u_ref[...] = (x_ref[...] * (1.0 + sc_ref[...]) + sh_ref[...]).astype(BF16)

    return pl.pallas_call(
        kern, name=name, grid=(S // tm,),
        in_specs=[pl.BlockSpec((tm, D), lambda i: (i, 0)), _row_spec(D), _row_spec(D)],
        out_specs=pl.BlockSpec((tm, D), lambda i: (i, 0)),
        out_shape=jax.ShapeDtypeStruct((S, D), BF16),
        compiler_params=_cp(("parallel",)),
    )(x, sc, sh)


def loss_head(x, t, name):
    S, D = x.shape
    tm = min(512, S)

    def kern(x_ref, t_ref, dx_ref, l_ref):
        @pl.when(pl.program_id(0) == 0)
        def _():
            l_ref[...] = jnp.zeros_like(l_ref)
        e = x_ref[...] - t_ref[...]
        dx_ref[...] = e * (1.0 / D)
        l_ref[...] += jnp.sum(e * e)

    return pl.pallas_call(
        kern, name=name, grid=(S // tm,),
        in_specs=[pl.BlockSpec((tm, D), lambda i: (i, 0)), pl.BlockSpec((tm, D), lambda i: (i, 0))],
        out_specs=[pl.BlockSpec((tm, D), lambda i: (i, 0)), pl.BlockSpec((8, 128), lambda i: (0, 0))],
        out_shape=[jax.ShapeDtypeStruct((S, D), F32), jax.ShapeDtypeStruct((8, 128), F32)],
        compiler_params=_cp(("arbitrary",)),
    )(x, t)


def silu_rows(c_all, name):
    def kern(c_ref, o_ref):
        c = c_ref[...]
        o_ref[...] = (c * _sigmoid(c)).astype(BF16)

    return pl.pallas_call(kern, name=name, out_shape=jax.ShapeDtypeStruct(c_all.shape, BF16))(c_all)


def sum_over_devices(g, name):
    n, R, C = g.shape

    def kern(g_ref, o_ref):
        acc = g_ref[0]
        for d in range(1, n):
            acc = acc + g_ref[d]
        o_ref[...] = acc

    return pl.pallas_call(kern, name=name, out_shape=jax.ShapeDtypeStruct((R, C), F32))(g)


def _rows_block(R, C, budget=1 << 20):
    if R * C * 4 <= budget or R % 8:
        return R
    tr = max(8, (budget // (C * 4)) // 8 * 8)
    while R % tr:
        tr -= 8
    return tr


def adamw(w, g, m, v, name):
    shape = w.shape
    C = shape[-1]
    R = w.size // C
    w2, g2, m2, v2 = (t.reshape(R, C) for t in (w, g, m, v))
    tr = _rows_block(R, C)
    c1 = 1.0 - ADAM_B1 ** ADAM_STEP
    c2 = 1.0 - ADAM_B2 ** ADAM_STEP

    def kern(w_ref, g_ref, m_ref, v_ref, d_ref, nm_ref, nv_ref):
        gg = g_ref[...]
        nm = ADAM_B1 * m_ref[...] + (1.0 - ADAM_B1) * gg
        nv = ADAM_B2 * v_ref[...] + (1.0 - ADAM_B2) * (gg * gg)
        m_hat = nm / c1
        v_hat = nv / c2
        d_ref[...] = -ADAM_LR * (m_hat / (jnp.sqrt(v_hat) + ADAM_EPS) + ADAM_WD * w_ref[...])
        nm_ref[...] = nm
        nv_ref[...] = nv

    spec = pl.BlockSpec((tr, C), lambda i: (i, 0))
    outs = pl.pallas_call(
        kern, name=name, grid=(R // tr,),
        in_specs=[spec] * 4, out_specs=[spec] * 3,
        out_shape=[jax.ShapeDtypeStruct((R, C), F32)] * 3,
        compiler_params=_cp(("parallel",)),
    )(w2, g2, m2, v2)
    return tuple(o.reshape(shape) for o in outs)


def _tn_for(N):
    for tn in (1024, 768, 640, 512, 384, 256, 128):
        if N % tn == 0:
            return tn
    return N


def mm_plain(a, b3, layer, name, *, mode="f32", nt=False, bias=None, h=None, tm=1024, side=None):
    M, K = a.shape
    N = b3.shape[1] if nt else b3.shape[2]
    tm = min(tm, M)
    tn = _tn_for(N)
    a_spec = pl.BlockSpec((tm, K), lambda j, i: (i, 0))
    if nt:
        b_spec = pl.BlockSpec((None, tn, K), lambda j, i: (layer, j, 0))
    else:
        b_spec = pl.BlockSpec((None, K, tn), lambda j, i: (layer, 0, j))
    o_spec = pl.BlockSpec((tm, tn), lambda j, i: (i, j))
    ins, in_specs = [a, b3], [a_spec, b_spec]
    if bias is not None:
        ins.append(bias)
        in_specs.append(pl.BlockSpec((1, tn), lambda j, i: (0, j)))
    if mode == "mlp_dn":
        ins.append(h)
        in_specs.append(o_spec)
    elif mode not in ("f32", "bf16", "mlp_up"):
        raise ValueError(mode)
    odt = F32 if mode == "f32" else BF16

    def kern(a_ref, b_ref, *rest):
        rest = list(rest)
        bias_ref = rest.pop(0) if bias is not None else None
        h_ref = rest.pop(0) if mode == "mlp_dn" else None
        o_ref = rest.pop(0)
        if nt:
            bt_ref = rest.pop(0)

            @pl.when(pl.program_id(1) == 0)
            def _():
                bt_ref[...] = b_ref[...].T
            acc = _dot(a_ref[...], bt_ref[...])
        else:
            acc = _dot(a_ref[...], b_ref[...])
        if bias_ref is not None:
            acc = acc + bias_ref[...]
        if mode == "mlp_up":
            r = jnp.maximum(acc, 0.0)
            acc = r * r
        elif mode == "mlp_dn":
            acc = acc * (2.0 * jnp.sqrt(h_ref[...].astype(F32)))
        o_ref[...] = acc.astype(odt)

    outs, landed = hosted_call(
        kern, side, name=name, grid=(N // tn, M // tm), in_specs=in_specs, out_specs=[o_spec],
        out_shape=[jax.ShapeDtypeStruct((M, N), odt)],
        scratch_shapes=[pltpu.VMEM((K, tn), BF16)] if nt else [], dims=("parallel", "arbitrary"), args=tuple(ins))
    return outs[0] if side is None else (outs[0], landed)


def mm_down_ln(a, b3, layer, x_in, gate1p, ln_g, ln_b, sc_next, sh_next, name, *, tm=256):
    M, K = a.shape
    D = b3.shape[2]
    tm = min(tm, M)

    def kern(a_ref, b_ref, x_ref, gp_ref, lg_ref, lb_ref, sc_ref, sh_ref, y_ref, xo_ref, u_ref):
        y = _dot(a_ref[...], b_ref[...])
        y_ref[...] = y
        z = ALPHA * x_ref[...] + gp_ref[...] * y
        mu = jnp.mean(z, axis=-1, keepdims=True)
        zc = z - mu
        var = jnp.mean(zc * zc, axis=-1, keepdims=True)
        xo = (zc * lax.rsqrt(var + LN_EPS)) * lg_ref[...] + lb_ref[...]
        xo_ref[...] = xo
        u_ref[...] = (xo * (1.0 + sc_ref[...]) + sh_ref[...]).astype(BF16)

    tile = pl.BlockSpec((tm, D), lambda i: (i, 0))
    return pl.pallas_call(
        kern, name=name, grid=(M // tm,),
        in_specs=[pl.BlockSpec((tm, K), lambda i: (i, 0)), pl.BlockSpec((None, K, D), lambda i: (layer, 0, 0)), tile]
        + [_row_spec(D)] * 5,
        out_specs=[tile, tile, tile],
        out_shape=[jax.ShapeDtypeStruct((M, D), F32)] * 2 + [jax.ShapeDtypeStruct((M, D), BF16)],
        compiler_params=_cp(("parallel",)),
    )(a, b3, x_in, gate1p, ln_g, ln_b, sc_next, sh_next)


def mm_down_comb(a, b3, layer, dz, x_in, sc1p, name, *, parts=1, ln=None, side=None, tm=256):
    D, K = b3.shape[1], b3.shape[2]
    M = a.shape[-2]
    kp = K // parts
    tm = min(tm, M)
    n_ln = 0 if ln is None else 4

    def kern(*refs):
        a_refs = refs[:parts]
        b_ref, dz_ref, x_ref, sp_ref = refs[parts:parts + 4]
        ln_refs = refs[parts + 4:parts + 4 + n_ln]
        outs = refs[parts + 4 + n_ln:]

        @pl.when(pl.program_id(0) == 0)
        def _():
            for o in outs:
                if o.shape[0] == 8:
                    o[...] = jnp.zeros_like(o)
        if parts == 1:
            du = _dot_nt(a_refs[0][...], b_ref[...])
        else:
            du = _dot_nt(a_refs[0][...], b_ref[:, 0:kp])
            for p in range(1, parts):
                du = du + _dot_nt(a_refs[p][...], b_ref[:, p * kp:(p + 1) * kp])
        dx = ALPHA * dz_ref[...] + du * sp_ref[...]
        if ln is None:
            dx_ref, s_ref = outs
            dx_ref[...] = dx
        else:
            dzl_ref, dyl_ref, s_ref, sl_ref = outs
            _ln_bwd_tile(dx, *ln_refs, dzl_ref, dyl_ref, sl_ref)
        s_ref[0:1, :] += jnp.sum(du * x_ref[...], axis=0, keepdims=True)
        s_ref[1:2, :] += jnp.sum(du, axis=0, keepdims=True)
        if parts > 1:
            for p in range(parts):
                s_ref[2 + p:3 + p, :] += jnp.sum(a_refs[p][...].astype(F32), axis=0, keepdims=True)

    tile = pl.BlockSpec((tm, D), lambda i: (i, 0))
    sums = pl.BlockSpec((8, D), lambda i: (0, 0))
    if parts == 1:
        a_ins, a_specs = [a], [pl.BlockSpec((tm, K), lambda i: (i, 0))]
    else:
        assert kp == D and parts <= 6
        a_ins = [a] * parts
        a_specs = [pl.BlockSpec((None, tm, kp), functools.partial(lambda i, p: (p, i, 0), p=p)) for p in range(parts)]
    in_specs = a_specs + [pl.BlockSpec((None, D, K), lambda i: (layer, 0, 0)), tile, tile, _row_spec(D)]
    args = a_ins + [b3, dz, x_in, sc1p]
    if ln is None:
        out_specs = [tile, sums]
        out_shape = [jax.ShapeDtypeStruct((M, D), F32), jax.ShapeDtypeStruct((8, D), F32)]
    else:
        in_specs += [tile, tile, _row_spec(D), _row_spec(D)]
        args += list(ln)
        out_specs = [tile, tile, sums, sums]
        out_shape = [jax.ShapeDtypeStruct((M, D), F32), jax.ShapeDtypeStruct((M, D), BF16),
                     jax.ShapeDtypeStruct((8, D), F32), jax.ShapeDtypeStruct((8, D), F32)]
    return hosted_call(kern, side, name=name, grid=(M // tm,), in_specs=in_specs, out_specs=out_specs,
                       out_shape=out_shape, scratch_shapes=[], dims=("arbitrary",), args=tuple(args))


def mm_w(a, b, name, *, ts=2048, tk=512, chips_out=False, b_parts=1, tn=None):
    S, K = a.shape
    npart = b.shape[-1]
    N = npart * b_parts
    ts = min(ts, S)
    tk = min(tk, K)
    n_chip = N // N_CHIPS
    if tn is None:
        tn = _tn_for(n_chip if chips_out else npart)
    assert npart % tn == 0 and (not chips_out or n_chip % tn == 0)

    def kern(a_ref, b_ref, o_ref):
        @pl.when(pl.program_id(2) == 0)
        def _():
            o_ref[...] = jnp.zeros_like(o_ref)
        o_ref[...] += _dot_tn(a_ref[...], b_ref[...])

    if b_parts == 1:
        b_spec = pl.BlockSpec((ts, tn), lambda k, n, s: (s, n))
    else:
        per = npart // tn
        b_spec = pl.BlockSpec((None, ts, tn), lambda k, n, s: (n // per, s, n % per))
    if chips_out:
        per_chip = n_chip // tn
        o_spec = pl.BlockSpec((None, tk, tn), lambda k, n, s: (n // per_chip, k, n % per_chip))
        out_shape = jax.ShapeDtypeStruct((N_CHIPS, K, n_chip), F32)
    else:
        o_spec = pl.BlockSpec((tk, tn), lambda k, n, s: (k, n))
        out_shape = jax.ShapeDtypeStruct((K, N), F32)
    return pl.pallas_call(
        kern, name=name, grid=(K // tk, N // tn, S // ts),
        in_specs=[pl.BlockSpec((ts, tk), lambda k, n, s: (s, k)), b_spec],
        out_specs=o_spec, out_shape=out_shape,
        compiler_params=_cp(("parallel", "parallel", "arbitrary")),
    )(a, b)


def mm_w_chips3(a, b3, name, *, ts=512):
    S, K = a.shape
    P = b3.shape[2]
    n_chip = 3 * P // N_CHIPS
    ts = min(ts, S)
    pieces = []
    for chip in range(N_CHIPS):
        lo, hi = chip * n_chip, (chip + 1) * n_chip
        while lo < hi:
            part = lo // P
            w = min(hi, (part + 1) * P) - lo
            pieces.append((chip, lo - chip * n_chip, part, lo - part * P, w))
            lo += w

    def kern(a_ref, b_ref, o_ref):
        @pl.when(pl.program_id(0) == 0)
        def _():
            o_ref[...] = jnp.zeros_like(o_ref)
        at = a_ref[...].T
        for chip, oc, part, pc, w in pieces:
            o_ref[chip, :, oc:oc + w] += _dot(at, b_ref[part, :, pc:pc + w])

    return pl.pallas_call(
        kern, name=name, grid=(S // ts,),
        in_specs=[pl.BlockSpec((ts, K), lambda s: (s, 0)), pl.BlockSpec((3, ts, P), lambda s: (0, s, 0))],
        out_specs=pl.BlockSpec((N_CHIPS, K, n_chip), lambda s: (0, 0, 0)),
        out_shape=jax.ShapeDtypeStruct((N_CHIPS, K, n_chip), F32),
        compiler_params=_cp(("arbitrary",)),
    )(a, b3)


def mm_f32(a, b, name):
    def kern(a_ref, b_ref, o_ref):
        o_ref[...] = jnp.dot(a_ref[...], b_ref[...], precision=HIGHEST, preferred_element_type=F32)

    return pl.pallas_call(kern, name=name, out_shape=jax.ShapeDtypeStruct((a.shape[0], b.shape[1]), F32),
                          compiler_params=pltpu.CompilerParams(vmem_limit_bytes=VMEM_LIMIT))(a, b)


def _ln_bwd_tile(dxo_t, x_ref, y_ref, gp_ref, lg_ref, dz_ref, dy_ref, s_ref):
    yv = y_ref[...]
    z = ALPHA * x_ref[...] + gp_ref[...] * yv
    mu = jnp.mean(z, axis=-1, keepdims=True)
    zc = z - mu
    var = jnp.mean(zc * zc, axis=-1, keepdims=True)
    rstd = lax.rsqrt(var + LN_EPS)
    xhat = zc * rstd
    dxh = dxo_t * lg_ref[...]
    dz = rstd * (dxh - jnp.mean(dxh, axis=-1, keepdims=True)
                 - xhat * jnp.mean(dxh * xhat, axis=-1, keepdims=True))
    dz_ref[...] = dz
    dy_ref[...] = (gp_ref[...] * dz).astype(BF16)
    s_ref[0:1, :] += jnp.sum(dxo_t * xhat, axis=0, keepdims=True)
    s_ref[1:2, :] += jnp.sum(dxo_t, axis=0, keepdims=True)
    s_ref[2:3, :] += jnp.sum(dz * yv, axis=0, keepdims=True)


def ln_bwd(dxo, x_in, y, gate1p, ln_g, name, *, tm=256):
    S, D = dxo.shape
    tm = min(tm, S)

    def kern(dxo_ref, x_ref, y_ref, gp_ref, lg_ref, dz_ref, dy_ref, s_ref):
        @pl.when(pl.program_id(0) == 0)
        def _():
            s_ref[...] = jnp.zeros_like(s_ref)
        _ln_bwd_tile(dxo_ref[...], x_ref, y_ref, gp_ref, lg_ref, dz_ref, dy_ref, s_ref)

    tile = pl.BlockSpec((tm, D), lambda i: (i, 0))
    return pl.pallas_call(
        kern, name=name, grid=(S // tm,),
        in_specs=[tile, tile, tile, _row_spec(D), _row_spec(D)],
        out_specs=[tile, tile, pl.BlockSpec((8, D), lambda i: (0, 0))],
        out_shape=[jax.ShapeDtypeStruct((S, D), F32), jax.ShapeDtypeStruct((S, D), BF16),
                   jax.ShapeDtypeStruct((8, D), F32)],
        compiler_params=_cp(("arbitrary",)),
    )(dxo, x_in, y, gate1p, ln_g)


def _tri64():
    r = lax.broadcasted_iota(jnp.int32, (CHUNK, CHUNK), 0)
    c = lax.broadcasted_iota(jnp.int32, (CHUNK, CHUNK), 1)
    return r >= c


def _gla_chunk_common(proj_ref, rows, b, h):
    kc = slice(h * GLA_DK_HEAD, (h + 1) * GLA_DK_HEAD)
    bh = b[:, kc]
    ep = jnp.exp(bh)
    en = jnp.exp(-bh)
    bl = bh[CHUNK - 1:CHUNK, :]
    ee = jnp.exp(bl - bh)
    dec = jnp.exp(bl)
    q = proj_ref[rows, h * GLA_DK_HEAD:(h + 1) * GLA_DK_HEAD] * GLA_SCALE
    k = proj_ref[rows, GLA_DK + h * GLA_DK_HEAD:GLA_DK + (h + 1) * GLA_DK_HEAD]
    v = proj_ref[rows, 2 * GLA_DK + h * GLA_DV_HEAD:2 * GLA_DK + (h + 1) * GLA_DV_HEAD]
    g = proj_ref[rows, 2 * GLA_DK + GLA_DV + h * GLA_DV_HEAD:2 * GLA_DK + GLA_DV + (h + 1) * GLA_DV_HEAD]
    return ep, en, ee, dec, q, k, v, g


def gla_fwd(proj, wgk_p, bgk, gnorm, name, side=None):
    S = proj.shape[0]
    TB = min(GLA_TB, S)
    ncb = TB // CHUNK

    def kern(proj_ref, wgk_ref, bgk_ref, gn_ref, zg_ref, st_ref, state_scr, la_scr):
        @pl.when(pl.program_id(0) == 0)
        def _():
            state_scr[...] = jnp.zeros_like(state_scr)
        lr = proj_ref[:, GLA_LR_OFF:GLA_IN_PAD].astype(BF16)
        gk = _dot(lr, wgk_ref[...]) + bgk_ref[...]
        la_scr[...] = _log_sigmoid(gk) * (1.0 / 16.0)
        lower = _tri64()
        tri = lower.astype(F32)

        def group(gi, carry):
            rows = [pl.ds(pl.multiple_of((gi * GLA_GROUP + g) * CHUNK, CHUNK), CHUNK) for g in range(GLA_GROUP)]
            b = [jnp.dot(tri, la_scr[r, :], precision=HIGHEST, preferred_element_type=F32) for r in rows]
            P = [(g, h) for g in range(GLA_GROUP) for h in range(GLA_HEADS)]
            cm = {p: _gla_chunk_common(proj_ref, rows[p[0]], b[p[0]], p[1]) for p in P}
            qf = {p: (cm[p][4] * cm[p][0]).astype(BF16) for p in P}
            kn = {p: (cm[p][5] * cm[p][1]).astype(BF16) for p in P}
            qn = {p: (cm[p][4] * cm[p][1]).astype(BF16) for p in P}
            kp = {p: (cm[p][5] * cm[p][0]).astype(BF16) for p in P}
            ke = {p: (cm[p][5] * cm[p][2]).astype(BF16) for p in P}
            vb = {p: cm[p][6].astype(BF16) for p in P}
            a_f = {p: _dot_nt(qf[p], kn[p]) for p in P}
            a_b = {p: _dot_nt(qn[p], kp[p]) for p in P}
            upd = {p: _dot_tn(vb[p], ke[p]) for p in P}
            st = {(0, h): state_scr[h] for h in range(GLA_HEADS)}
            for g in range(GLA_GROUP):
                for h in range(GLA_HEADS):
                    st[(g + 1, h)] = st[(g, h)] * cm[(g, h)][3] + upd[(g, h)]
            o_st = {p: _dot_nt(qf[p], st[p].astype(BF16)) for p in P}
            amat = {p: jnp.where(lower, a_f[p], a_b[p]).astype(BF16) for p in P}
            o = {p: _dot(amat[p], vb[p]) + o_st[p] for p in P}
            for g, h in P:
                st_ref[gi * GLA_GROUP + g, h] = st[(g, h)]
            for h in range(GLA_HEADS):
                state_scr[h] = st[(GLA_GROUP, h)]
            for g, h in P:
                gate = cm[(g, h)][7]
                vc = slice(h * GLA_DV_HEAD, (h + 1) * GLA_DV_HEAD)
                r = lax.rsqrt(jnp.mean(o[(g, h)] * o[(g, h)], axis=-1, keepdims=True) + RMS_EPS)
                on = (o[(g, h)] * r) * gn_ref[:, vc]
                zg_ref[rows[g], vc] = (on * (gate * _sigmoid(gate))).astype(BF16)
            return carry

        lax.fori_loop(0, ncb // GLA_GROUP, group, 0)

    return hosted_call(
        kern, side, name=name, grid=(S // TB,),
        in_specs=[pl.BlockSpec((TB, GLA_IN_PAD), lambda i: (i, 0)),
                  pl.BlockSpec((128, GLA_DK), lambda i: (0, 0)), _row_spec(GLA_DK), _row_spec(GLA_DV)],
        out_specs=[pl.BlockSpec((TB, GLA_DV), lambda i: (i, 0)),
                   pl.BlockSpec((ncb, GLA_HEADS, GLA_DV_HEAD, GLA_DK_HEAD), lambda i: (i, 0, 0, 0))],
        out_shape=[jax.ShapeDtypeStruct((S, GLA_DV), BF16),
                   jax.ShapeDtypeStruct((S // CHUNK, GLA_HEADS, GLA_DV_HEAD, GLA_DK_HEAD), F32)],
        scratch_shapes=[pltpu.VMEM((GLA_HEADS, GLA_DV_HEAD, GLA_DK_HEAD), F32), pltpu.VMEM((TB, GLA_DK), F32)],
        dims=("arbitrary",), args=(proj, wgk_p, bgk, gnorm))


def gla_bwd(proj, states, dzg, wgk_p, bgk, gnorm, name, side=None):
    S = proj.shape[0]
    TB = min(GLA_TB, S)
    ncb = TB // CHUNK
    nb = S // TB

    def kern(proj_ref, st_ref, dzg_ref, wgk_ref, bgk_ref, gn_ref,
             dproj_ref, dwgk_ref, dbgk_ref, dgn_ref, dstate_scr, la_scr, gk_scr, dgk_scr):
        @pl.when(pl.program_id(0) == 0)
        def _():
            dstate_scr[...] = jnp.zeros_like(dstate_scr)
            dwgk_ref[...] = jnp.zeros_like(dwgk_ref)
            dbgk_ref[...] = jnp.zeros_like(dbgk_ref)
            dgn_ref[...] = jnp.zeros_like(dgn_ref)
        lr = proj_ref[:, GLA_LR_OFF:GLA_IN_PAD].astype(BF16)
        gk = _dot(lr, wgk_ref[...]) + bgk_ref[...]
        gk_scr[...] = gk
        la_scr[...] = _log_sigmoid(gk) * (1.0 / 16.0)
        lower = _tri64()
        tri = lower.astype(F32)
        r_i = lax.broadcasted_iota(jnp.int32, (CHUNK, CHUNK), 0)
        c_i = lax.broadcasted_iota(jnp.int32, (CHUNK, CHUNK), 1)
        triu = (c_i >= r_i).astype(F32)
        last_row = lax.broadcasted_iota(jnp.int32, (CHUNK, GLA_DK_HEAD), 0) == CHUNK - 1

        def group(gi, carry):
            cs = [ncb - 1 - (gi * GLA_GROUP + g) for g in range(GLA_GROUP)]
            rows = [pl.ds(pl.multiple_of(c * CHUNK, CHUNK), CHUNK) for c in cs]
            b = [jnp.dot(tri, la_scr[r, :], precision=HIGHEST, preferred_element_type=F32) for r in rows]
            P = [(g, h) for g in range(GLA_GROUP) for h in range(GLA_HEADS)]
            kcs = [slice(h * GLA_DK_HEAD, (h + 1) * GLA_DK_HEAD) for h in range(GLA_HEADS)]
            vcs = [slice(h * GLA_DV_HEAD, (h + 1) * GLA_DV_HEAD) for h in range(GLA_HEADS)]
            cm = {p: _gla_chunk_common(proj_ref, rows[p[0]], b[p[0]], p[1]) for p in P}
            ep, en, ee, dec = ({p: cm[p][i] for p in P} for i in range(4))
            qf = {p: cm[p][4] * cm[p][0] for p in P}
            kn = {p: cm[p][5] * cm[p][1] for p in P}
            qn = {p: cm[p][4] * cm[p][1] for p in P}
            kp = {p: cm[p][5] * cm[p][0] for p in P}
            ke = {p: cm[p][5] * cm[p][2] for p in P}
            qf_b, kn_b, qn_b, kp_b, ke_b = ({p: t[p].astype(BF16) for p in P} for t in (qf, kn, qn, kp, ke))
            vb = {p: cm[p][6].astype(BF16) for p in P}
            st = {p: st_ref[cs[p[0]], p[1]] for p in P}
            st_b = {p: st[p].astype(BF16) for p in P}
            a_f = {p: _dot_nt(qf_b[p], kn_b[p]) for p in P}
            a_b = {p: _dot_nt(qn_b[p], kp_b[p]) for p in P}
            o_st = {p: _dot_nt(qf_b[p], st_b[p]) for p in P}
            amat = {p: jnp.where(lower, a_f[p], a_b[p]).astype(BF16) for p in P}
            o = {p: _dot(amat[p], vb[p]) + o_st[p] for p in P}
            do_b, dgs = {}, {}
            for p in P:
                g, h = p
                r = lax.rsqrt(jnp.mean(o[p] * o[p], axis=-1, keepdims=True) + RMS_EPS)
                oh = o[p] * r
                gn = gn_ref[:, vcs[h]]
                gate = cm[p][7]
                sg = _sigmoid(gate)
                dz = dzg_ref[rows[g], vcs[h]]
                don = dz * (gate * sg)
                dgs[p] = dz * (oh * gn) * (sg * (1.0 + gate * (1.0 - sg)))
                dgn_ref[:, vcs[h]] += jnp.sum(don * oh, axis=0, keepdims=True)
                doh = don * gn
                do_b[p] = (r * (doh - oh * jnp.mean(doh * oh, axis=-1, keepdims=True))).astype(BF16)
            da = {p: _dot_nt(do_b[p], vb[p]) for p in P}
            dv_a = {p: _dot_tn(amat[p], do_b[p]) for p in P}
            dqf_st = {p: _dot(do_b[p], st_b[p]) for p in P}
            dst_upd = {p: _dot_tn(do_b[p], qf_b[p]) for p in P}
            dst = {(0, h): dstate_scr[h] for h in range(GLA_HEADS)}
            for g in range(GLA_GROUP):
                for h in range(GLA_HEADS):
                    dst[(g + 1, h)] = dst[(g, h)] * dec[(g, h)] + dst_upd[(g, h)]
            for h in range(GLA_HEADS):
                dstate_scr[h] = dst[(GLA_GROUP, h)]
            dst_b = {p: dst[p].astype(BF16) for p in P}
            dv = {p: dv_a[p] + _dot_nt(ke_b[p], dst_b[p]) for p in P}
            dke = {p: _dot(vb[p], dst_b[p]) for p in P}
            da_f = {p: jnp.where(lower, da[p], 0.0).astype(BF16) for p in P}
            da_b = {p: jnp.where(lower, 0.0, da[p]).astype(BF16) for p in P}
            dqf = {p: _dot(da_f[p], kn_b[p]) + dqf_st[p] for p in P}
            dkn = {p: _dot_tn(da_f[p], qf_b[p]) for p in P}
            dqn = {p: _dot(da_b[p], kp_b[p]) for p in P}
            dkp = {p: _dot_tn(da_b[p], qn_b[p]) for p in P}
            dbs = {}
            for p in P:
                ddec = jnp.sum(dst[p] * st[p], axis=0, keepdims=True)
                db = dqf[p] * qf[p] - dkn[p] * kn[p] - dqn[p] * qn[p] + dkp[p] * kp[p] - dke[p] * ke[p]
                dbl = jnp.sum(dke[p] * ke[p], axis=0, keepdims=True) + ddec * dec[p]
                dbs[p] = db + jnp.where(last_row, dbl, 0.0)
            dla = {p: jnp.dot(triu, dbs[p], precision=HIGHEST, preferred_element_type=F32) for p in P}
            for p in P:
                g, h = p
                dq = (dqf[p] * ep[p] + dqn[p] * en[p]) * GLA_SCALE
                dk = dkn[p] * en[p] + dkp[p] * ep[p] + dke[p] * ee[p]
                dgk_scr[rows[g], kcs[h]] = dla[p] * (1.0 / 16.0) * _sigmoid(-gk_scr[rows[g], kcs[h]])
                dproj_ref[rows[g], kcs[h]] = dq.astype(BF16)
                dproj_ref[rows[g], GLA_DK + h * GLA_DK_HEAD:GLA_DK + (h + 1) * GLA_DK_HEAD] = dk.astype(BF16)
                dproj_ref[rows[g], 2 * GLA_DK + h * GLA_DV_HEAD:2 * GLA_DK + (h + 1) * GLA_DV_HEAD] = dv[p].astype(BF16)
                dproj_ref[rows[g], 2 * GLA_DK + GLA_DV + h * GLA_DV_HEAD:
                          2 * GLA_DK + GLA_DV + (h + 1) * GLA_DV_HEAD] = dgs[p].astype(BF16)
            return carry

        lax.fori_loop(0, ncb // GLA_GROUP, group, 0)
        dgk = dgk_scr[...]
        dgk_b = dgk.astype(BF16)
        dproj_ref[:, GLA_LR_OFF:GLA_IN_PAD] = _dot_nt(dgk_b, wgk_ref[...]).astype(BF16)
        dwgk_ref[...] += _dot_tn(lr, dgk_b)
        dbgk_ref[...] += jnp.sum(dgk, axis=0, keepdims=True)

    rev = lambda i: (nb - 1 - i, 0)
    return hosted_call(
        kern, side, name=name, grid=(nb,),
        in_specs=[pl.BlockSpec((TB, GLA_IN_PAD), rev),
                  pl.BlockSpec((ncb, GLA_HEADS, GLA_DV_HEAD, GLA_DK_HEAD), lambda i: (nb - 1 - i, 0, 0, 0)),
                  pl.BlockSpec((TB, GLA_DV), rev),
                  pl.BlockSpec((128, GLA_DK), lambda i: (0, 0)), _row_spec(GLA_DK), _row_spec(GLA_DV)],
        out_specs=[pl.BlockSpec((TB, GLA_IN_PAD), rev),
                   pl.BlockSpec((128, GLA_DK), lambda i: (0, 0)), _row_spec(GLA_DK), _row_spec(GLA_DV)],
        out_shape=[jax.ShapeDtypeStruct((S, GLA_IN_PAD), BF16), jax.ShapeDtypeStruct((128, GLA_DK), F32),
                   jax.ShapeDtypeStruct((1, GLA_DK), F32), jax.ShapeDtypeStruct((1, GLA_DV), F32)],
        scratch_shapes=[pltpu.VMEM((GLA_HEADS, GLA_DV_HEAD, GLA_DK_HEAD), F32), pltpu.VMEM((TB, GLA_DK), F32),
                        pltpu.VMEM((TB, GLA_DK), F32), pltpu.VMEM((TB, GLA_DK), F32)],
        dims=("arbitrary",), args=(proj, states, dzg, wgk_p, bgk, gnorm))


ATT_TW = 1024
ATT_CLASSES = 3


def _att_window(i):
    return pl.multiple_of(jnp.maximum(i * ATT_TQ - LEFT_CHUNKS * CHUNK, 0), ATT_TQ)


def _att_rel_index():
    e = jnp.arange(ATT_TW)[None, :]
    d = jnp.where(e < ATT_KW, e, e - ATT_TW)
    off = (jnp.arange(ATT_CLASSES) * ATT_TQ)[:, None]
    return jnp.clip(off - d, -MAX_REL, MAX_REL) + MAX_REL


def _row_bits():
    return lax.broadcasted_iota(jnp.int32, (ATT_TQ, ATT_TW), 0)


def att_bias_tiles(rel_bias, name):
    pick = (jnp.arange(384)[:, None] == _att_rel_index().reshape(-1)[None, :]).astype(F32)
    tab = mm_f32(jnp.pad(rel_bias, ((0, 0), (0, 384 - N_REL))), pick, name + "_tab")
    tab = tab.reshape(ATT_HEADS * ATT_CLASSES, 1, ATT_TW)

    def kern(t_ref, o_ref):
        cls = pl.program_id(0) % ATT_CLASSES
        x = jnp.broadcast_to(t_ref[...], (ATT_TQ, ATT_TW))
        row = _row_bits()
        for b in range(8):
            x = jnp.where((row & (1 << b)) != 0, pltpu.roll(x, 1 << b, axis=1), x)
        x = x[:, :ATT_KW]
        qc = cls * (ATT_TQ // CHUNK) + lax.shift_right_arithmetic(
            lax.broadcasted_iota(jnp.int32, (ATT_TQ, ATT_KW), 0), 6)
        kc = lax.shift_right_arithmetic(lax.broadcasted_iota(jnp.int32, (ATT_TQ, ATT_KW), 1), 6)
        o_ref[...] = jnp.where((kc <= qc) & (kc >= qc - LEFT_CHUNKS), x, NEG_INF)

    return pl.pallas_call(
        kern, name=name, grid=(ATT_HEADS * ATT_CLASSES,),
        in_specs=[pl.BlockSpec((None, 1, ATT_TW), lambda i: (i, 0, 0))],
        out_specs=pl.BlockSpec((None, ATT_TQ, ATT_KW), lambda i: (i, 0, 0)),
        out_shape=jax.ShapeDtypeStruct((ATT_HEADS * ATT_CLASSES, ATT_TQ, ATT_KW), F32),
        compiler_params=_cp(("parallel",)),
    )(tab)


def att_bias_grad(dbt, name):
    def kern(d_ref, o_ref):
        x = jnp.concatenate([d_ref[...], jnp.zeros((ATT_TQ, ATT_TW - ATT_KW), F32)], axis=1)
        row = _row_bits()
        for b in range(8):
            x = jnp.where((row & (1 << b)) != 0, pltpu.roll(x, ATT_TW - (1 << b), axis=1), x)
        o_ref[...] = jnp.sum(x, axis=0, keepdims=True)

    diag = pl.pallas_call(
        kern, name=name + "_diag", grid=(ATT_HEADS * ATT_CLASSES,),
        in_specs=[pl.BlockSpec((None, ATT_TQ, ATT_KW), lambda i: (i, 0, 0))],
        out_specs=pl.BlockSpec((None, 1, ATT_TW), lambda i: (i, 0, 0)),
        out_shape=jax.ShapeDtypeStruct((ATT_HEADS * ATT_CLASSES, 1, ATT_TW), F32),
        compiler_params=_cp(("parallel",)),
    )(dbt)
    diag = diag.reshape(ATT_HEADS, ATT_CLASSES * ATT_TW)
    onehot = (_att_rel_index().reshape(-1)[:, None] == jnp.arange(384)[None, :]).astype(F32)
    return mm_f32(diag, onehot, name + "_bins")[:, :N_REL]


def _att_scores(q_ref, kw, bias_ref):
    hs = [slice(hh * ATT_HD, (hh + 1) * ATT_HD) for hh in range(2)]
    q = [q_ref[:, h] * ATT_SCALE for h in hs]
    k = [kw[:, h] for h in hs]
    s = [_dot_nt(q[hh], k[hh]) + bias_ref[hh] for hh in range(2)]
    e = [jnp.exp(t - jnp.max(t, axis=-1, keepdims=True)) for t in s]
    inv = [1.0 / jnp.sum(t, axis=-1, keepdims=True) for t in e]
    return hs, q, k, e, inv


def _att_specs(S):
    nq = D_MODEL // 128
    q_spec = pl.BlockSpec((ATT_TQ, 128), lambda p, i: (i, p))
    k_spec = pl.BlockSpec((S, 128), lambda p, i: (0, nq + p))
    v_spec = pl.BlockSpec((S, 128), lambda p, i: (0, 2 * nq + p))
    b_spec = pl.BlockSpec((2, None, ATT_TQ, ATT_KW), lambda p, i: (p, jnp.minimum(i, ATT_CLASSES - 1), 0, 0))
    return q_spec, k_spec, v_spec, b_spec


def attn_fwd(qkv, bias, name, side=None):
    S = qkv.shape[0]
    q_spec, k_spec, v_spec, b_spec = _att_specs(S)

    def kern(q_ref, k_ref, v_ref, bias_ref, o_ref):
        ws = _att_window(pl.program_id(1))
        kw = k_ref[pl.ds(ws, ATT_KW), :]
        vw = v_ref[pl.ds(ws, ATT_KW), :]
        hs, _, _, e, inv = _att_scores(q_ref, kw, bias_ref)
        outs = [_dot(e[hh].astype(BF16), vw[:, hs[hh]]) * inv[hh] for hh in range(2)]
        o_ref[...] = jnp.concatenate(outs, axis=1).astype(BF16)

    return hosted_call(
        kern, side, name=name, grid=(ATT_HEADS // 2, S // ATT_TQ),
        in_specs=[q_spec, k_spec, v_spec, b_spec],
        out_specs=[pl.BlockSpec((ATT_TQ, 128), lambda p, i: (i, p))],
        out_shape=[jax.ShapeDtypeStruct((S, D_MODEL), BF16)],
        scratch_shapes=[], dims=("parallel", "arbitrary"), args=(qkv, qkv, qkv, bias))


def attn_bwd(qkv, bias, do, name, side=None):
    S = qkv.shape[0]
    nblk = S // ATT_TQ
    q_spec, k_spec, v_spec, b_spec = _att_specs(S)

    def kern(q_ref, k_ref, v_ref, bias_ref, do_ref, dqkv_ref, db_ref, dk_scr, dv_scr):
        i = pl.program_id(1)

        @pl.when(i == 0)
        def _():
            dk_scr[...] = jnp.zeros_like(dk_scr)
            dv_scr[...] = jnp.zeros_like(dv_scr)
            db_ref[...] = jnp.zeros_like(db_ref)
        ws = _att_window(i)
        win = pl.ds(ws, ATT_KW)
        kw = k_ref[win, :]
        vw = v_ref[win, :]
        o_cls = jnp.minimum(i, ATT_CLASSES - 1)
        R2 = range(2)
        hs, q, k, e, inv = _att_scores(q_ref, kw, bias_ref)
        do_h = [do_ref[:, h] for h in hs]
        dp = [_dot_nt(do_h[hh], vw[:, hs[hh]]) for hh in R2]
        p = [e[hh] * inv[hh] for hh in R2]
        dvs = [_dot_tn(p[hh].astype(BF16), do_h[hh]) for hh in R2]
        ds = [p[hh] * (dp[hh] - jnp.sum(p[hh] * dp[hh], axis=-1, keepdims=True)) for hh in R2]
        ds_b = [t.astype(BF16) for t in ds]
        dqs = [_dot(ds_b[hh], k[hh]) * ATT_SCALE for hh in R2]
        dks = [_dot_tn(ds_b[hh], q[hh]) for hh in R2]
        for hh in R2:
            db_ref[hh, o_cls] += ds[hh]
        dqkv_ref[0, pl.ds(pl.multiple_of(i * ATT_TQ, ATT_TQ), ATT_TQ), :] = jnp.concatenate(dqs, axis=1).astype(BF16)
        dk_scr[win, :] += jnp.concatenate(dks, axis=1)
        dv_scr[win, :] += jnp.concatenate(dvs, axis=1)

        @pl.when(i == nblk - 1)
        def _():
            dqkv_ref[1] = dk_scr[...].astype(BF16)
            dqkv_ref[2] = dv_scr[...].astype(BF16)

    return hosted_call(
        kern, side, name=name, grid=(ATT_HEADS // 2, nblk),
        in_specs=[q_spec, k_spec, v_spec, b_spec, pl.BlockSpec((ATT_TQ, 128), lambda p, i: (i, p))],
        out_specs=[pl.BlockSpec((3, S, 128), lambda p, i: (0, 0, p)),
                   pl.BlockSpec((2, ATT_CLASSES, ATT_TQ, ATT_KW), lambda p, i: (p, 0, 0, 0))],
        out_shape=[jax.ShapeDtypeStruct((3, S, D_MODEL), BF16),
                   jax.ShapeDtypeStruct((ATT_HEADS, ATT_CLASSES, ATT_TQ, ATT_KW), F32)],
        scratch_shapes=[pltpu.VMEM((S, 128), F32), pltpu.VMEM((S, 128), F32)],
        dims=("parallel", "arbitrary"), args=(qkv, qkv, qkv, bias, do))


def colsum3(a3, name):
    P, S, N = a3.shape
    tm = min(512, S)

    def kern(a_ref, o_ref):
        @pl.when(pl.program_id(1) == 0)
        def _():
            o_ref[...] = jnp.zeros_like(o_ref)
        o_ref[...] += jnp.sum(a_ref[...].astype(F32), axis=0, keepdims=True)

    return pl.pallas_call(
        kern, name=name, grid=(P, S // tm),
        in_specs=[pl.BlockSpec((None, tm, N), lambda p, i: (p, i, 0))],
        out_specs=pl.BlockSpec((None, 1, N), lambda p, i: (p, 0, 0)),
        out_shape=jax.ShapeDtypeStruct((P, 1, N), F32),
        compiler_params=_cp(("parallel", "arbitrary")),
    )(a3)


def _me():
    return lax.axis_index("x"), lax.axis_index("y"), lax.axis_index("c")


def _other_chips(x, y):
    return [(1 - x, y), (x, 1 - y), (1 - x, 1 - y)]


def all_gather8(x_shard, name):
    m_per, n = x_shard.shape

    def body(x_ref, out_ref, send_sems, recv_sems, local_sem):
        x, y, c = _me()
        me, sibling = (x, y, c), (x, y, 1 - c)
        chips = _other_chips(x, y)

        def rows(px, py, pc):
            return out_ref.at[pl.ds((4 * px + 2 * py + pc) * m_per, m_per), :]

        def copy(k, block, to, src=None):
            return pltpu.make_async_remote_copy(
                src_ref=rows(*block) if src is None else src, dst_ref=rows(*block),
                send_sem=send_sems.at[k], recv_sem=recv_sems.at[k], device_id=to, device_id_type=MESH)

        mine = pltpu.make_async_copy(x_ref, rows(*me), local_sem)
        mine.start()
        first = [copy(0, me, sibling, src=x_ref)]
        first += [copy(1 + j, me, (*chip, c), src=x_ref) for j, chip in enumerate(chips)]
        for cp in first:
            cp.start()
        passed = [copy(4 + j, (*chip, c), sibling) for j, chip in enumerate(chips)]
        for j, chip in enumerate(chips):
            copy(1 + j, (*chip, c), me).wait_recv()
            passed[j].start()
        copy(0, sibling, me).wait_recv()
        for j, chip in enumerate(chips):
            copy(4 + j, (*chip, 1 - c), me).wait_recv()
        for cp in first + passed:
            cp.wait_send()
        mine.wait()

    return pl.pallas_call(
        body, name=name,
        out_shape=jax.ShapeDtypeStruct((N_DEV * m_per, n), x_shard.dtype),
        in_specs=[pl.BlockSpec(memory_space=pltpu.VMEM)],
        out_specs=pl.BlockSpec(memory_space=pltpu.VMEM),
        scratch_shapes=[pltpu.SemaphoreType.DMA((7,)), pltpu.SemaphoreType.DMA((7,)), pltpu.SemaphoreType.DMA],
        compiler_params=pltpu.CompilerParams(vmem_limit_bytes=VMEM_LIMIT),
    )(x_shard)


def _half_rows(n_rows, c):
    h = n_rows // 2
    return pl.ds(c * h, h)


def _gathered_shape(shape, flavour):
    L, a, b = shape
    return {"col": (L, a, N_CHIPS * b), "row": (L, N_CHIPS * a, b), "lead": (N_CHIPS, L, a, b)}[flavour]


def _gathered_part(out_ref, shape, flavour, s, rows):
    L, a, b = shape
    if flavour == "col":
        return out_ref.at[:, rows, pl.ds(s * b, b)]
    if flavour == "row":
        return out_ref.at[:, pl.ds(s * a + rows.start, rows.size), :]
    return out_ref.at[s, :, rows, :]


def gather_side(shards, flavours):
    n = len(shards)
    shapes = [w.shape for w in shards]

    def copies(w_refs, out_refs, send_sems, recv_sems, local_sems):
        x, y, c = _me()
        sibling = (x, y, 1 - c)
        chips = _other_chips(x, y)
        me_s = 2 * x + y

        def copy(k, src, dst, to):
            return pltpu.make_async_remote_copy(src_ref=src, dst_ref=dst, send_sem=send_sems.at[k],
                                                recv_sem=recv_sems.at[k], device_id=to, device_id_type=MESH)

        own, first, landed, passed, passed_in = [], [], [], [], []
        for w in range(n):
            shp, fl = shapes[w], flavours[w]
            my_half = _half_rows(shp[1], c)
            sib_half = _half_rows(shp[1], 1 - c)
            own.append(copy(7 * w + 6, w_refs[w], _gathered_part(out_refs[w], shp, fl, me_s, pl.ds(0, shp[1])), sibling))
            for j, chip in enumerate(chips):
                s = 2 * chip[0] + chip[1]
                first.append(copy(7 * w + j, w_refs[w].at[:, my_half, :],
                                  _gathered_part(out_refs[w], shp, fl, me_s, my_half), (*chip, c)))
                part = _gathered_part(out_refs[w], shp, fl, s, my_half)
                landed.append(copy(7 * w + j, part, part, (*chip, c)))
                passed.append(copy(7 * w + 3 + j, part, part, sibling))
                theirs = _gathered_part(out_refs[w], shp, fl, s, sib_half)
                passed_in.append(copy(7 * w + 3 + j, theirs, theirs, sibling))
        return own, first, landed, passed, passed_in

    def start(*refs):
        own, first, _, _, _ = copies(*refs)
        for cp in first + own:
            cp.start()

    def wait(*refs):
        own, first, landed, passed, passed_in = copies(*refs)
        for arrived, onward in zip(landed, passed):
            arrived.wait_recv()
            onward.start()
        for cp in passed_in:
            cp.wait_recv()
        for cp in own:
            cp.wait()
        for cp in first + passed:
            cp.wait_send()

    out_shapes = [jax.ShapeDtypeStruct(_gathered_shape(s, f), w.dtype) for w, s, f in zip(shards, shapes, flavours)]
    return Side(shards, out_shapes, 7 * n, 0, start, wait)


def swap_side(gs):
    n = len(gs)

    def copies(g_refs, out_refs, send_sems, recv_sems, local_sems):
        x, y, c = _me()
        return [pltpu.make_async_remote_copy(
            src_ref=g_refs[w].at[:, _half_rows(gs[w].shape[1], 1 - c), :], dst_ref=out_refs[w],
            send_sem=send_sems.at[w], recv_sem=recv_sems.at[w], device_id=(x, y, 1 - c), device_id_type=MESH)
            for w in range(n)]

    def start(*refs):
        for cp in copies(*refs):
            cp.start()

    def wait(*refs):
        for cp in copies(*refs):
            cp.wait()

    out_shapes = [jax.ShapeDtypeStruct((g.shape[0], g.shape[1] // 2, g.shape[2]), g.dtype) for g in gs]
    return Side(gs, out_shapes, n, 0, start, wait)


def add_half(g, r1, c_idx, name):
    n, R, C = g.shape
    half = R // 2
    tr = _rows_block(half, C)
    nbh = half // tr

    def kern(c_ref, g_ref, r_ref, o_ref, ob_ref):
        p = g_ref[...] + r_ref[...]
        o_ref[...] = p
        ob_ref[...] = p.astype(BF16)

    spec = pl.BlockSpec((1, tr, C), lambda d, r, c_ref: (d, r, 0))
    return pl.pallas_call(
        kern, name=name,
        grid_spec=pltpu.PrefetchScalarGridSpec(
            num_scalar_prefetch=1, grid=(n, nbh),
            in_specs=[pl.BlockSpec((1, tr, C), lambda d, r, c_ref: (d, c_ref[0] * nbh + r, 0)), spec],
            out_specs=[spec, spec]),
        out_shape=[jax.ShapeDtypeStruct((n, half, C), F32), jax.ShapeDtypeStruct((n, half, C), BF16)],
        compiler_params=_cp(("parallel", "parallel")),
    )(c_idx, g, r1)


def exchange_side(ps):
    n = len(ps)

    def copies(p_refs, out_refs, send_sems, recv_sems, local_sems):
        x, y, c = _me()
        return [pltpu.make_async_remote_copy(
            src_ref=p_refs[w].at[2 * chip[0] + chip[1]], dst_ref=out_refs[w].at[j],
            send_sem=send_sems.at[3 * w + j], recv_sem=recv_sems.at[3 * w + j],
            device_id=(*chip, c), device_id_type=MESH)
            for w in range(n) for j, chip in enumerate(_other_chips(x, y))]

    def start(*refs):
        for cp in copies(*refs):
            cp.start()

    def wait(*refs):
        for cp in copies(*refs):
            cp.wait()

    return Side(ps, [jax.ShapeDtypeStruct((3,) + p.shape[1:], p.dtype) for p in ps], 3 * n, 0, start, wait)


def add_chips(p, r2, chip_idx, name):
    n, H, C = p.shape
    tr = _rows_block(H, C)

    def kern(s_ref, p_ref, r_ref, o_ref):
        o_ref[...] = ((p_ref[0] + r_ref[0].astype(F32)) + r_ref[1].astype(F32)) + r_ref[2].astype(F32)

    return pl.pallas_call(
        kern, name=name,
        grid_spec=pltpu.PrefetchScalarGridSpec(
            num_scalar_prefetch=1, grid=(H // tr,),
            in_specs=[pl.BlockSpec((1, tr, C), lambda r, s_ref: (s_ref[0], r, 0)),
                      pl.BlockSpec((3, tr, C), lambda r, s_ref: (0, r, 0))],
            out_specs=pl.BlockSpec((tr, C), lambda r, s_ref: (r, 0))),
        out_shape=jax.ShapeDtypeStruct((H, C), F32),
        compiler_params=_cp(("parallel",)),
    )(chip_idx, p, r2)


def swap_reduced(ss, name):
    n = len(ss)

    def body(*refs):
        s_refs, out_refs = refs[:n], refs[n:2 * n]
        send_sems, recv_sems = refs[2 * n:]
        x, y, c = _me()
        cps = [pltpu.make_async_remote_copy(src_ref=s_refs[w], dst_ref=out_refs[w], send_sem=send_sems.at[w],
                                            recv_sem=recv_sems.at[w], device_id=(x, y, 1 - c), device_id_type=MESH)
               for w in range(n)]
        for cp in cps:
            cp.start()
        for cp in cps:
            cp.wait()

    any_spec = pl.BlockSpec(memory_space=pl.ANY)
    return pl.pallas_call(
        body, name=name, out_shape=[jax.ShapeDtypeStruct(s.shape, s.dtype) for s in ss],
        in_specs=[any_spec] * n, out_specs=[any_spec] * n,
        scratch_shapes=[pltpu.SemaphoreType.DMA((n,)), pltpu.SemaphoreType.DMA((n,))],
    )(*ss)


BIG = (("gla_w_in", 2, (1024, GLA_IN // N_CHIPS), "lead"), ("gla_w_out", 2, (256, 1024), "row"),
       ("att_w_in", 2, (1024, 768), "col"), ("att_w_out", 2, (256, 1024), "row"),
       ("ff_w1", 4, (1024, 1024), "col"), ("ff_w2", 4, (1024, 1024), "row"))
FLAVOUR = {n: f for n, _, _, f in BIG}


def layer_weights(i):
    mixer = "gla" if i % 2 == 0 else "att"
    return (("in", mixer + "_w_in", i // 2), ("out", mixer + "_w_out", i // 2), ("w1", "ff_w1", i), ("w2", "ff_w2", i))


class Comm:
    def __init__(self, weights, core, chip):
        self.weights, self.core, self.chip = weights, core, chip
        self.c_idx = jnp.reshape(core, (1,)).astype(jnp.int32)
        self.chip_idx = jnp.reshape(chip, (1,)).astype(jnp.int32)
        self.reduced = {}

    def gather(self, items):
        shards = [self.weights[n][l:l + 1].astype(BF16) for _, n, l in items]
        return gather_side(shards, [FLAVOUR[n] for _, n, _ in items])

    def full_weights(self, items, gathered):
        W = {}
        for (role, n, _), w in zip(items, gathered):
            if n == "gla_w_in":
                w = jnp.pad(w.transpose(1, 2, 0, 3).reshape(1, D_MODEL, GLA_IN), ((0, 0), (0, 0), (0, GLA_IN_PAD - GLA_IN)))
            W[role] = (w, 0)
        return W

    def gather_now(self, items, name):
        return self.full_weights(items, run_side(self.gather(items), name))

    def swap(self, items):
        return swap_side([g for _, _, g in items])

    def reduce_begin(self, tag, items, swapped):
        ps = [add_half(g, r, self.c_idx, f"rs_add2_{tag}_{w}") for w, ((_, _, g), r) in enumerate(zip(items, swapped))]
        return tag, [(n, l) for n, l, _ in items], ps

    def exchange(self, pending):
        return exchange_side([pb for _, pb in pending[2]])

    def reduce_mid(self, pending, landed):
        tag, keys, ps = pending
        for w, (key, (p, _), r) in enumerate(zip(keys, ps, landed)):
            self.reduced[key] = add_chips(p, r, self.chip_idx, f"rs_add4_{tag}_{w}")

    def reduce_tail(self, tag, items):
        pending = self.reduce_begin(tag, items, run_side(self.swap(items), f"rs_swap_{tag}"))
        self.reduce_mid(pending, run_side(self.exchange(pending), f"rs_xchg_{tag}"))

    def reduce_end(self):
        keys = [(n, l) for n, L, _, _ in BIG for l in range(L)]
        mine = [self.reduced[k] for k in keys]
        theirs = swap_reduced(mine, "rs_join")
        low = self.core == 0
        full = {k: jnp.concatenate([jnp.where(low, m, t), jnp.where(low, t, m)], axis=0)
                for k, m, t in zip(keys, mine, theirs)}
        return {n: jnp.stack([full[(n, l)] for l in range(L)]) for n, L, _, _ in BIG}


def local_step(x, target, mods, comm, small):
    S, D = x.shape
    row = lambda v: v.reshape(1, -1)
    saved = []
    tiles = [att_bias_tiles(small["att_rel_bias"][j], f"att_tiles_{j}").reshape(ATT_HEADS, ATT_CLASSES, ATT_TQ, ATT_KW)
             for j in range(2)]
    wgk_p = [jnp.pad(small["gla_w_gk2"][j], ((0, 128 - GLA_RANK), (0, 0))).astype(BF16) for j in range(2)]

    u1 = modulate(x, row(mods[0, 1]), row(mods[0, 0]), "mod_first")
    Ws = [dict() for _ in range(DEPTH)]
    items0 = layer_weights(0)
    Ws[0].update(comm.gather_now(items0[:1], "gather_w0"))
    for i in range(DEPTH):
        j = i // 2
        W = Ws[i]
        sh1, sc1, g1, sh2, sc2, g2 = (row(mods[i, k]) for k in range(6))
        nxt = min(i + 1, DEPTH - 1)
        more = i + 1 < DEPTH
        nxt_items = layer_weights(nxt)
        side_in = comm.gather(items0[1:]) if i == 0 else None
        side_mix = comm.gather(nxt_items[:2]) if more else None
        side_ff = comm.gather(nxt_items[2:]) if more else None
        if i % 2 == 0:
            proj = mm_plain(u1, *W["in"], f"gla_in_{i}", side=side_in)
        else:
            proj = mm_plain(u1, *W["in"], f"att_in_{i}", mode="bf16", bias=row(small["att_b_in"][j]), side=side_in)
        proj, landed = proj if side_in is not None else (proj, [])
        if i == 0:
            W.update(comm.full_weights(items0[1:], landed))
        if i % 2 == 0:
            (zmix, states), landed = gla_fwd(proj, wgk_p[j], row(small["gla_b_gk"][j]), row(small["gla_g_norm"][j]),
                                             f"gla_fwd_{i}", side_mix)
        else:
            (zmix,), landed = attn_fwd(proj, tiles[j], f"att_fwd_{i}", side_mix)
            states = None
        if more:
            Ws[nxt].update(comm.full_weights(nxt_items[:2], landed))
        y1, x_mid, u2 = mm_down_ln(zmix, *W["out"], x, 1.0 + g1, row(small["ln_g"][i, 0]), row(small["ln_b"][i, 0]),
                                   sc2, sh2, f"mix_out_{i}")
        act = mm_plain(u2, *W["w1"], f"ff_up_{i}", mode="mlp_up", side=side_ff)
        act, landed = act if side_ff is not None else (act, [])
        if more:
            Ws[nxt].update(comm.full_weights(nxt_items[2:], landed))
        y2, x_out, u_next = mm_down_ln(act, *W["w2"], x_mid, 1.0 + g2, row(small["ln_g"][i, 1]),
                                       row(small["ln_b"][i, 1]), row(mods[nxt, 1]), row(mods[nxt, 0]), f"ff_out_{i}")
        saved.append(dict(x_in=x, u1=u1, proj=proj, zmix=zmix, states=states, y1=y1, x_mid=x_mid, u2=u2,
                          act=act, y2=y2))
        x, u1 = x_out, u_next

    dx, sq = loss_head(x, target, "loss_head")

    g_small = dict(ln_g=[None] * DEPTH, ln_b=[None] * DEPTH, gla_w_gk2=[None] * 2, gla_b_gk=[None] * 2,
                   gla_g_norm=[None] * 2, att_b_in=[None] * 2, att_rel_bias=[None] * 2)
    dmods = [None] * DEPTH
    later = []
    top = saved[DEPTH - 1]
    dz2, dy2, s_ln2 = ln_bwd(dx, top["x_mid"], top["y2"], 1.0 + row(mods[DEPTH - 1, 5]),
                             row(small["ln_g"][DEPTH - 1, 1]), "ln2_bwd_top")

    for i in reversed(range(DEPTH)):
        j = i // 2
        sv = saved[i]
        W = Ws[i]
        sh1, sc1, g1, sh2, sc2, g2 = (row(mods[i, k]) for k in range(6))
        dh = mm_plain(dy2, *W["w2"], f"ff_dn_{i}", mode="mlp_dn", nt=True, h=sv["act"])
        g_w2 = mm_w(sv["act"], dy2, f"ff_w2g_{i}").reshape(N_CHIPS, D_FF // N_CHIPS, D)
        g_w1 = mm_w(sv["u2"], dh, f"ff_w1g_{i}", chips_out=True)
        items = [("ff_w1", i, g_w1), ("ff_w2", i, g_w2)] + later
        (dz1, dy1, s_m2, s_ln1), swapped = mm_down_comb(
            dh, *W["w1"], dz2, sv["x_mid"], 1.0 + sc2, f"ff_dx_{i}",
            ln=(sv["x_in"], sv["y1"], 1.0 + g1, row(small["ln_g"][i, 0])), side=comm.swap(items))
        pending = comm.reduce_begin(i, items, swapped)
        side = comm.exchange(pending)
        mixer = "gla" if i % 2 == 0 else "att"
        below = None
        if i > 0:
            below = (saved[i - 1]["x_mid"], saved[i - 1]["y2"], 1.0 + row(mods[i - 1, 5]), row(small["ln_g"][i - 1, 1]))
        if i % 2 == 0:
            g_out = mm_w(sv["zmix"], dy1, f"gla_wog_{i}").reshape(N_CHIPS, D // N_CHIPS, D)
            dzg = mm_plain(dy1, *W["out"], f"gla_dz_{i}", nt=True)
            (dproj, dwgk, dbgk, dgn), landed = gla_bwd(sv["proj"], sv["states"], dzg, wgk_p[j],
                                                       row(small["gla_b_gk"][j]), row(small["gla_g_norm"][j]),
                                                       f"gla_bwd_{i}", side)
            g_small["gla_w_gk2"][j] = dwgk[:GLA_RANK]
            g_small["gla_b_gk"][j] = dbgk[0]
            g_small["gla_g_norm"][j] = dgn[0].reshape(GLA_HEADS, GLA_DV_HEAD)
            gwi = mm_w(sv["u1"], dproj, f"gla_wig_{i}")[:, :GLA_IN]
            g_in = gwi.reshape(D, N_CHIPS, GLA_IN // N_CHIPS).transpose(1, 0, 2)
            outs, _ = mm_down_comb(dproj, *W["in"], dz1, sv["x_in"], 1.0 + sc1, f"mix_dx_{i}", ln=below)
        else:
            g_out = mm_w(sv["zmix"], dy1, f"att_wog_{i}").reshape(N_CHIPS, D // N_CHIPS, D)
            do = mm_plain(dy1, *W["out"], f"att_do_{i}", mode="bf16", nt=True)
            (dqkv, dbt), landed = attn_bwd(sv["proj"], tiles[j], do, f"att_bwd_{i}", side)
            g_small["att_rel_bias"][j] = att_bias_grad(dbt.reshape(ATT_HEADS * ATT_CLASSES, ATT_TQ, ATT_KW),
                                                       f"att_bias_{i}")
            g_in = mm_w_chips3(sv["u1"], dqkv, f"att_wig_{i}")
            outs, _ = mm_down_comb(dqkv, *W["in"], dz1, sv["x_in"], 1.0 + sc1, f"mix_dx_{i}", parts=3, ln=below)
        s_m1 = outs[1] if below is None else outs[2]
        if i % 2 == 1:
            g_small["att_b_in"][j] = s_m1[2:5].reshape(3 * D)
        comm.reduce_mid(pending, landed)
        later = [(mixer + "_w_in", j, g_in), (mixer + "_w_out", j, g_out)]
        g_small["ln_g"][i] = jnp.stack([s_ln1[0], s_ln2[0]])
        g_small["ln_b"][i] = jnp.stack([s_ln1[1], s_ln2[1]])
        dmods[i] = jnp.stack([s_m1[1], s_m1[0], s_ln1[2], s_m2[1], s_m2[0], s_ln2[2]])
        if below is None:
            dx = outs[0]
        else:
            dz2, dy2, s_ln2 = outs[0], outs[1], outs[3]
    comm.reduce_tail("last", later)

    g_small = {n: jnp.stack(v) for n, v in g_small.items()}
    return sq, dx, jnp.stack(dmods), g_small


SMALL_SHARDED = (("ln_g", (4, 2, 256)), ("ln_b", (4, 2, 256)), ("gla_g_norm", (2, 4, 64)),
                 ("gla_w_gk2", (2, 16, 128)), ("att_b_in", (2, 768)))
SMALL_FULL = dict(ln_g=(4, 2, 1024), ln_b=(4, 2, 1024), gla_g_norm=(2, 4, 256), gla_w_gk2=(2, 16, 512),
                  att_b_in=(2, 3072), gla_b_gk=(2, 512), att_rel_bias=(2, 16, 257))
SMALL_GRAD_ORDER = ("ln_g", "ln_b", "gla_g_norm", "gla_w_gk2", "att_b_in", "gla_b_gk", "att_rel_bias")


def _pack_small(arrs, rows_total):
    parts = []
    for a in arrs:
        flat = a.reshape(-1)
        pad = (-flat.shape[0]) % PACK_W
        parts.append(jnp.pad(flat, (0, pad)).reshape(-1, PACK_W))
    buf = jnp.concatenate(parts, axis=0)
    return jnp.pad(buf, ((0, rows_total - buf.shape[0]), (0, 0)))


def _unpack_small(buf, shapes):
    out, r = [], 0
    for shp in shapes:
        n = 1
        for s in shp:
            n *= s
        nr = (n + PACK_W - 1) // PACK_W
        out.append(buf[..., r:r + nr, :].reshape(buf.shape[:-2] + (nr * PACK_W,))[..., :n].reshape(buf.shape[:-2] + shp))
        r += nr
    return out


def _unshard_last(g4):
    nd = g4.ndim
    perm = tuple(range(1, nd - 1)) + (0, nd - 1)
    t = g4.transpose(perm)
    return t.reshape(t.shape[:-2] + (-1,))


def _shard_last(full, s):
    n = full.shape[-1] // N_CHIPS
    return lax.dynamic_slice_in_dim(full, s * n, n, axis=full.ndim - 1)


WEIGHT_NAMES = ("w_ada", "b_ada", "ln_g", "ln_b", "gla_w_in", "gla_w_gk2", "gla_b_gk", "gla_g_norm", "gla_w_out",
                "att_w_in", "att_b_in", "att_rel_bias", "att_w_out", "ff_w1", "ff_w2")


def kernel(x, c, w_ada, b_ada, ln_g, ln_b, gla_w_in, gla_w_gk2, gla_b_gk, gla_g_norm, gla_w_out, att_w_in, att_b_in, att_rel_bias, att_w_out, ff_w1, ff_w2, loss_target, m_w_ada, m_b_ada, m_ln_g, m_ln_b, m_gla_w_in, m_gla_w_gk2, m_gla_b_gk, m_gla_g_norm, m_gla_w_out, m_att_w_in, m_att_b_in, m_att_rel_bias, m_att_w_out, m_ff_w1, m_ff_w2, v_w_ada, v_b_ada, v_ln_g, v_ln_b, v_gla_w_in, v_gla_w_gk2, v_gla_b_gk, v_gla_g_norm, v_gla_w_out, v_att_w_in, v_att_b_in, v_att_rel_bias, v_att_w_out, v_ff_w1, v_ff_w2):
    weights = dict(w_ada=w_ada, b_ada=b_ada, ln_g=ln_g, ln_b=ln_b, gla_w_in=gla_w_in, gla_w_gk2=gla_w_gk2,
                   gla_b_gk=gla_b_gk, gla_g_norm=gla_g_norm, gla_w_out=gla_w_out, att_w_in=att_w_in,
                   att_b_in=att_b_in, att_rel_bias=att_rel_bias, att_w_out=att_w_out, ff_w1=ff_w1, ff_w2=ff_w2)
    mom1 = dict(w_ada=m_w_ada, b_ada=m_b_ada, ln_g=m_ln_g, ln_b=m_ln_b, gla_w_in=m_gla_w_in, gla_w_gk2=m_gla_w_gk2,
                gla_b_gk=m_gla_b_gk, gla_g_norm=m_gla_g_norm, gla_w_out=m_gla_w_out, att_w_in=m_att_w_in,
                att_b_in=m_att_b_in, att_rel_bias=m_att_rel_bias, att_w_out=m_att_w_out, ff_w1=m_ff_w1, ff_w2=m_ff_w2)
    mom2 = dict(w_ada=v_w_ada, b_ada=v_b_ada, ln_g=v_ln_g, ln_b=v_ln_b, gla_w_in=v_gla_w_in, gla_w_gk2=v_gla_w_gk2,
                gla_b_gk=v_gla_b_gk, gla_g_norm=v_gla_g_norm, gla_w_out=v_gla_w_out, att_w_in=v_att_w_in,
                att_b_in=v_att_b_in, att_rel_bias=v_att_rel_bias, att_w_out=v_att_w_out, ff_w1=v_ff_w1, ff_w2=v_ff_w2)

    ax, ay, ac = lax.axis_index("x"), lax.axis_index("y"), lax.axis_index("c")
    chip = 2 * ax + ay
    dev = 2 * chip + ac
    S = x.shape[1]
    x2 = x.reshape(S, D_MODEL)
    t2 = loss_target.reshape(S, D_MODEL)

    comm = Comm(weights, ac, chip)

    small_rows = 16
    spack = _pack_small([c] + [weights[n] for n, _ in SMALL_SHARDED], small_rows)
    sg = all_gather8(spack, "gather_small").reshape(N_DEV, small_rows, PACK_W)
    parts = _unpack_small(sg, [(1, D_MODEL)] + [shp for _, shp in SMALL_SHARDED])
    c_all = parts[0].reshape(N_DEV, D_MODEL)
    small = {n: _unshard_last(p[0::2]) for (n, _), p in zip(SMALL_SHARDED, parts[1:])}
    small["gla_b_gk"] = gla_b_gk
    small["att_rel_bias"] = att_rel_bias

    c_act = silu_rows(jnp.pad(c_all, ((0, 128 - N_DEV), (0, 0))), "silu_c")
    wa = w_ada.astype(BF16).transpose(1, 0, 2).reshape(1, D_MODEL, DEPTH * 6 * D_MODEL // N_CHIPS)
    mods_part = mm_plain(c_act, wa, 0, "ada_fwd", tm=128)[:N_DEV]
    mg = all_gather8(mods_part, "gather_mods").reshape(N_CHIPS, 2, N_DEV, DEPTH, 6 * D_MODEL // N_CHIPS)
    mods_mine = lax.dynamic_index_in_dim(mg[:, 0], dev, axis=1, keepdims=False)
    mods = mods_mine.transpose(1, 0, 2).reshape(DEPTH, 6 * D_MODEL) + b_ada
    mods = mods.reshape(DEPTH, 6, D_MODEL)

    sq, grad_x, dmods, g_small = local_step(x2, t2, mods, comm, small)
    loss = lax.psum(0.5 * sq[0, 0] / D_MODEL, ("x", "y", "c"))

    g_shard = comm.reduce_end()

    dm_flat = dmods.reshape(DEPTH, 6 * D_MODEL)
    g_rows = 80
    gpack = _pack_small([dm_flat] + [g_small[n] for n in SMALL_GRAD_ORDER], g_rows)
    gg = all_gather8(gpack, "gather_small_grads").reshape(N_DEV, g_rows, PACK_W)
    gsum = sum_over_devices(gg, "sum_small_grads")
    sums = _unpack_small(gsum, [(DEPTH, 6 * D_MODEL)] + [SMALL_FULL[n] for n in SMALL_GRAD_ORDER])
    grads = dict(b_ada=sums[0])
    for n, full_g in zip(SMALL_GRAD_ORDER, sums[1:]):
        grads[n] = full_g if n in ("gla_b_gk", "att_rel_bias") else _shard_last(full_g, chip)
    dm_all = _unpack_small(gg, [(DEPTH, 6 * D_MODEL)])[0]
    dm_cols = _shard_last(dm_all, chip).reshape(N_DEV, DEPTH * 6 * D_MODEL // N_CHIPS)
    dm_cols = jnp.pad(dm_cols, ((0, 128 - N_DEV), (0, 0))).astype(BF16)
    gwa = mm_w(c_act, dm_cols, "ada_bwd", ts=128)
    grads["w_ada"] = gwa.reshape(D_MODEL, DEPTH, 6 * D_MODEL // N_CHIPS).transpose(1, 0, 2)
    grads.update(g_shard)

    deltas, new_m, new_v = {}, {}, {}
    for n in WEIGHT_NAMES:
        deltas[n], new_m[n], new_v[n] = adamw(weights[n], grads[n], mom1[n], mom2[n], "adamw_" + n)

    return (loss, grad_x.reshape(1, S, D_MODEL), *[grads[n] for n in WEIGHT_NAMES], *[deltas[n] for n in WEIGHT_NAMES],
            *[new_m[n] for n in WEIGHT_NAMES], *[new_v[n] for n in WEIGHT_NAMES])
```

```python
import functools

import jax
import jax.numpy as jnp
from jax import lax
from jax.experimental import pallas as pl
from jax.experimental.pallas import tpu as pltpu

F32 = jnp.float32
BF16 = jnp.bfloat16
HIGHEST = lax.Precision.HIGHEST
MESH = pl.DeviceIdType.MESH

D_MODEL = 1024
DEPTH = 4
CHUNK = 64
GLA_HEADS = 4
GLA_DK = 512
GLA_DV = 1024
GLA_DK_HEAD = 128
GLA_DV_HEAD = 256
GLA_RANK = 16
GLA_IN = 3088
GLA_IN_PAD = 3200
GLA_LR_OFF = 3072
ATT_HEADS = 16
ATT_HD = 64
LEFT_CHUNKS = 8
MAX_REL = 128
N_REL = 257
D_FF = 4096
ALPHA = (2.0 * DEPTH) ** 0.25
LN_EPS = 1e-5
RMS_EPS = 1e-6
NEG_INF = -1e30
GLA_SCALE = GLA_DK_HEAD ** -0.5
ATT_SCALE = ATT_HD ** -0.5
ADAM_LR = 0.001
ADAM_B1 = 0.9
ADAM_B2 = 0.999
ADAM_EPS = 1e-08
ADAM_WD = 0.01
ADAM_STEP = 10

ATT_TQ = 256
ATT_KW = 768
GLA_TB = 256
GLA_GROUP = 2
VMEM_LIMIT = 56 * 1024 * 1024
WHOLE_WEIGHT_BYTES = 8 * 1024 * 1024
N_CHIPS = 4
N_DEV = 8
PACK_W = 1024


def _dot(a, b):
    return jnp.dot(a, b, preferred_element_type=F32)


def _dot_nt(a, b):
    return lax.dot_general(a, b, (((1,), (1,)), ((), ())), preferred_element_type=F32)


def _dot_tn(a, b):
    return lax.dot_general(a, b, (((0,), (0,)), ((), ())), preferred_element_type=F32)


def _cp(sem, vmem=VMEM_LIMIT):
    return pltpu.CompilerParams(dimension_semantics=sem, vmem_limit_bytes=vmem)


def _row_spec(n):
    return pl.BlockSpec((1, n), lambda *_: (0, 0))


def _sigmoid(x):
    return 1.0 / (1.0 + jnp.exp(-x))


def _log_sigmoid(x):
    return jnp.minimum(x, 0.0) - jnp.log1p(jnp.exp(-jnp.abs(x)))


class Side:
    def __init__(self, ins, out_shapes, n_sems, n_local, start, wait):
        self.ins, self.out_shapes, self.n_sems, self.n_local = list(ins), list(out_shapes), n_sems, n_local
        self.start, self.wait = start, wait

    def sem_shapes(self):
        return [pltpu.SemaphoreType.DMA((self.n_sems,)), pltpu.SemaphoreType.DMA((self.n_sems,)),
                pltpu.SemaphoreType.DMA((max(self.n_local, 1),))]


def run_side(side, name):
    n_in = len(side.ins)
    n_out = len(side.out_shapes)

    def body(*refs):
        ins, outs, sems = refs[:n_in], refs[n_in:n_in + n_out], refs[n_in + n_out:]
        side.start(ins, outs, *sems)
        side.wait(ins, outs, *sems)

    any_spec = pl.BlockSpec(memory_space=pl.ANY)
    return pl.pallas_call(body, name=name, out_shape=side.out_shapes, in_specs=[any_spec] * n_in,
                          out_specs=[any_spec] * n_out, scratch_shapes=side.sem_shapes())(*side.ins)


def hosted_call(main, side, *, name, grid, in_specs, out_specs, out_shape, scratch_shapes, dims, args):
    if side is None:
        outs = pl.pallas_call(main, name=name, grid=grid, in_specs=in_specs, out_specs=out_specs,
                              out_shape=out_shape, scratch_shapes=scratch_shapes, compiler_params=_cp(dims))(*args)
        return list(outs), []
    n_mi, n_mo, n_ms = len(in_specs), len(out_specs), len(scratch_shapes)
    n_si, n_so = len(side.ins), len(side.out_shapes)

    def kern(*refs):
        mi, si = refs[:n_mi], refs[n_mi:n_mi + n_si]
        o0 = n_mi + n_si
        mo, so = refs[o0:o0 + n_mo], refs[o0 + n_mo:o0 + n_mo + n_so]
        s0 = o0 + n_mo + n_so
        ms, sems = refs[s0:s0 + n_ms], refs[s0 + n_ms:]
        ids = [pl.program_id(d) for d in range(len(grid))]
        first = functools.reduce(jnp.logical_and, [i == 0 for i in ids])
        last = functools.reduce(jnp.logical_and, [i == g - 1 for i, g in zip(ids, grid)])

        @pl.when(first)
        def _():
            side.start(si, so, *sems)
        main(*mi, *mo, *ms)

        @pl.when(last)
        def _():
            side.wait(si, so, *sems)

    any_spec = pl.BlockSpec(memory_space=pl.ANY)
    outs = pl.pallas_call(
        kern, name=name, grid=grid, in_specs=list(in_specs) + [any_spec] * n_si,
        out_specs=list(out_specs) + [any_spec] * n_so, out_shape=list(out_shape) + side.out_shapes,
        scratch_shapes=list(scratch_shapes) + side.sem_shapes(),
        compiler_params=_cp(("arbitrary",) * len(grid)))(*args, *side.ins)
    return list(outs[:n_mo]), list(outs[n_mo:])


def modulate(x, sc, sh, name):
    S, D = x.shape
    tm = min(512, S)

    def kern(x_ref, sc_ref, sh_ref, u_ref):
        u_ref[...] = (x_ref[...] * (1.0 + sc_ref[...]) + sh_ref[...]).astype(BF16)

    return pl.pallas_call(
        kern, name=name, grid=(S // tm,),
        in_specs=[pl.BlockSpec((tm, D), lambda i: (i, 0)), _row_spec(D), _row_spec(D)],
        out_specs=pl.BlockSpec((tm, D), lambda i: (i, 0)),
        out_shape=jax.ShapeDtypeStruct((S, D), BF16),
        compiler_params=_cp(("parallel",)),
    )(x, sc, sh)


def loss_head(x, t, name):
    S, D = x.shape
    tm = min(512, S)

    def kern(x_ref, t_ref, dx_ref, l_ref):
        @pl.when(pl.program_id(0) == 0)
        def _():
            l_ref[...] = jnp.zeros_like(l_ref)
        e = x_ref[...] - t_ref[...]
        dx_ref[...] = e * (1.0 / D)
        l_ref[...] += jnp.sum(e * e)

    return pl.pallas_call(
        kern, name=name, grid=(S // tm,),
        in_specs=[pl.BlockSpec((tm, D), lambda i: (i, 0)), pl.BlockSpec((tm, D), lambda i: (i, 0))],
        out_specs=[pl.BlockSpec((tm, D), lambda i: (i, 0)), pl.BlockSpec((8, 128), lambda i: (0, 0))],
        out_shape=[jax.ShapeDtypeStruct((S, D), F32), jax.ShapeDtypeStruct((8, 128), F32)],
        compiler_params=_cp(("arbitrary",)),
    )(x, t)


def silu_rows(c_all, name):
    def kern(c_ref, o_ref):
        c = c_ref[...]
        o_ref[...] = (c * _sigmoid(c)).astype(BF16)

    return pl.pallas_call(kern, name=name, out_shape=jax.ShapeDtypeStruct(c_all.shape, BF16))(c_all)


def sum_over_devices(g, name):
    n, R, C = g.shape

    def kern(g_ref, o_ref):
        acc = g_ref[0]
        for d in range(1, n):
            acc = acc + g_ref[d]
        o_ref[...] = acc

    return pl.pallas_call(kern, name=name, out_shape=jax.ShapeDtypeStruct((R, C), F32))(g)


def _rows_block(R, C, budget=1 << 20):
    if R * C * 4 <= budget or R % 8:
        return R
    tr = max(8, (budget // (C * 4)) // 8 * 8)
    while R % tr:
        tr -= 8
    return tr


def adamw(w, g, m, v, name):
    shape = w.shape
    C = shape[-1]
    R = w.size // C
    w2, g2, m2, v2 = (t.reshape(R, C) for t in (w, g, m, v))
    tr = _rows_block(R, C)
    c1 = 1.0 - ADAM_B1 ** ADAM_STEP
    c2 = 1.0 - ADAM_B2 ** ADAM_STEP

    def kern(w_ref, g_ref, m_ref, v_ref, d_ref, nm_ref, nv_ref):
        gg = g_ref[...]
        nm = ADAM_B1 * m_ref[...] + (1.0 - ADAM_B1) * gg
        nv = ADAM_B2 * v_ref[...] + (1.0 - ADAM_B2) * (gg * gg)
        m_hat = nm / c1
        v_hat = nv / c2
        d_ref[...] = -ADAM_LR * (m_hat / (jnp.sqrt(v_hat) + ADAM_EPS) + ADAM_WD * w_ref[...])
        nm_ref[...] = nm
        nv_ref[...] = nv

    spec = pl.BlockSpec((tr, C), lambda i: (i, 0))
    outs = pl.pallas_call(
        kern, name=name, grid=(R // tr,),
        in_specs=[spec] * 4, out_specs=[spec] * 3,
        out_shape=[jax.ShapeDtypeStruct((R, C), F32)] * 3,
        compiler_params=_cp(("parallel",)),
    )(w2, g2, m2, v2)
    return tuple(o.reshape(shape) for o in outs)


def _tn_for(N):
    for tn in (1024, 768, 640, 512, 384, 256, 128):
        if N % tn == 0:
            return tn
    return N


def mm_plain(a, b3, layer, name, *, mode="f32", nt=False, bias=None, h=None, tm=1024, side=None):
    M, K = a.shape
    N = b3.shape[1] if nt else b3.shape[2]
    if K * N * 2 <= WHOLE_WEIGHT_BYTES:
        tn, tm = N, min(tm, 512)
    else:
        tn = _tn_for(N)
    tm = min(tm, M)
    a_spec = pl.BlockSpec((tm, K), lambda j, i: (i, 0))
    if nt:
        b_spec = pl.BlockSpec((None, tn, K), lambda j, i: (layer, j, 0))
    else:
        b_spec = pl.BlockSpec((None, K, tn), lambda j, i: (layer, 0, j))
    o_spec = pl.BlockSpec((tm, tn), lambda j, i: (i, j))
    ins, in_specs = [a, b3], [a_spec, b_spec]
    if bias is not None:
        ins.append(bias)
        in_specs.append(pl.BlockSpec((1, tn), lambda j, i: (0, j)))
    if mode == "mlp_dn":
        ins.append(h)
        in_specs.append(o_spec)
    elif mode not in ("f32", "bf16", "mlp_up"):
        raise ValueError(mode)
    odt = F32 if mode == "f32" else BF16

    def kern(a_ref, b_ref, *rest):
        rest = list(rest)
        bias_ref = rest.pop(0) if bias is not None else None
        h_ref = rest.pop(0) if mode == "mlp_dn" else None
        o_ref = rest.pop(0)
        if nt:
            bt_ref = rest.pop(0)

            @pl.when(pl.program_id(1) == 0)
            def _():
                bt_ref[...] = b_ref[...].T
            acc = _dot(a_ref[...], bt_ref[...])
        else:
            acc = _dot(a_ref[...], b_ref[...])
        if bias_ref is not None:
            acc = acc + bias_ref[...]
        if mode == "mlp_up":
            r = jnp.maximum(acc, 0.0)
            acc = r * r
        elif mode == "mlp_dn":
            acc = acc * (2.0 * jnp.sqrt(h_ref[...].astype(F32)))
        o_ref[...] = acc.astype(odt)

    outs, landed = hosted_call(
        kern, side, name=name, grid=(N // tn, M // tm), in_specs=in_specs, out_specs=[o_spec],
        out_shape=[jax.ShapeDtypeStruct((M, N), odt)],
        scratch_shapes=[pltpu.VMEM((K, tn), BF16)] if nt else [], dims=("parallel", "arbitrary"), args=tuple(ins))
    return outs[0] if side is None else (outs[0], landed)


def mm_down_ln(a, b3, layer, x_in, gate1p, ln_g, ln_b, sc_next, sh_next, name, *, tm=256):
    M, K = a.shape
    D = b3.shape[2]
    tm = min(tm, M)

    def kern(a_ref, b_ref, x_ref, gp_ref, lg_ref, lb_ref, sc_ref, sh_ref, y_ref, xo_ref, u_ref):
        y = _dot(a_ref[...], b_ref[...])
        y_ref[...] = y
        z = ALPHA * x_ref[...] + gp_ref[...] * y
        mu = jnp.mean(z, axis=-1, keepdims=True)
        zc = z - mu
        var = jnp.mean(zc * zc, axis=-1, keepdims=True)
        xo = (zc * lax.rsqrt(var + LN_EPS)) * lg_ref[...] + lb_ref[...]
        xo_ref[...] = xo
        u_ref[...] = (xo * (1.0 + sc_ref[...]) + sh_ref[...]).astype(BF16)

    tile = pl.BlockSpec((tm, D), lambda i: (i, 0))
    return pl.pallas_call(
        kern, name=name, grid=(M // tm,),
        in_specs=[pl.BlockSpec((tm, K), lambda i: (i, 0)), pl.BlockSpec((None, K, D), lambda i: (layer, 0, 0)), tile]
        + [_row_spec(D)] * 5,
        out_specs=[tile, tile, tile],
        out_shape=[jax.ShapeDtypeStruct((M, D), F32)] * 2 + [jax.ShapeDtypeStruct((M, D), BF16)],
        compiler_params=_cp(("parallel",)),
    )(a, b3, x_in, gate1p, ln_g, ln_b, sc_next, sh_next)


def mm_down_comb(a, b3, layer, dz, x_in, sc1p, name, *, parts=1, ln=None, side=None, tm=256):
    D, K = b3.shape[1], b3.shape[2]
    M = a.shape[-2]
    kp = K // parts
    tm = min(tm, M)
    n_ln = 0 if ln is None else 4

    def kern(*refs):
        a_refs = refs[:parts]
        b_ref, dz_ref, x_ref, sp_ref = refs[parts:parts + 4]
        ln_refs = refs[parts + 4:parts + 4 + n_ln]
        outs = refs[parts + 4 + n_ln:]

        @pl.when(pl.program_id(0) == 0)
        def _():
            for o in outs:
                if o.shape[0] == 8:
                    o[...] = jnp.zeros_like(o)
        if parts == 1:
            du = _dot_nt(a_refs[0][...], b_ref[...])
        else:
            du = _dot_nt(a_refs[0][...], b_ref[:, 0:kp])
            for p in range(1, parts):
                du = du + _dot_nt(a_refs[p][...], b_ref[:, p * kp:(p + 1) * kp])
        dx = ALPHA * dz_ref[...] + du * sp_ref[...]
        if ln is None:
            dx_ref, s_ref = outs
            dx_ref[...] = dx
        else:
            dzl_ref, dyl_ref, s_ref, sl_ref = outs
            _ln_bwd_tile(dx, *ln_refs, dzl_ref, dyl_ref, sl_ref)
        s_ref[0:1, :] += jnp.sum(du * x_ref[...], axis=0, keepdims=True)
        s_ref[1:2, :] += jnp.sum(du, axis=0, keepdims=True)
        if parts > 1:
            for p in range(parts):
                s_ref[2 + p:3 + p, :] += jnp.sum(a_refs[p][...].astype(F32), axis=0, keepdims=True)

    tile = pl.BlockSpec((tm, D), lambda i: (i, 0))
    sums = pl.BlockSpec((8, D), lambda i: (0, 0))
    if parts == 1:
        a_ins, a_specs = [a], [pl.BlockSpec((tm, K), lambda i: (i, 0))]
    else:
        assert kp == D and parts <= 6
        a_ins = [a] * parts
        a_specs = [pl.BlockSpec((None, tm, kp), functools.partial(lambda i, p: (p, i, 0), p=p)) for p in range(parts)]
    in_specs = a_specs + [pl.BlockSpec((None, D, K), lambda i: (layer, 0, 0)), tile, tile, _row_spec(D)]
    args = a_ins + [b3, dz, x_in, sc1p]
    if ln is None:
        out_specs = [tile, sums]
        out_shape = [jax.ShapeDtypeStruct((M, D), F32), jax.ShapeDtypeStruct((8, D), F32)]
    else:
        in_specs += [tile, tile, _row_spec(D), _row_spec(D)]
        args += list(ln)
        out_specs = [tile, tile, sums, sums]
        out_shape = [jax.ShapeDtypeStruct((M, D), F32), jax.ShapeDtypeStruct((M, D), BF16),
                     jax.ShapeDtypeStruct((8, D), F32), jax.ShapeDtypeStruct((8, D), F32)]
    return hosted_call(kern, side, name=name, grid=(M // tm,), in_specs=in_specs, out_specs=out_specs,
                       out_shape=out_shape, scratch_shapes=[], dims=("arbitrary",), args=tuple(args))


def mm_w(a, b, name, *, ts=2048, tk=512, chips_out=False, b_parts=1, tn=None):
    S, K = a.shape
    npart = b.shape[-1]
    N = npart * b_parts
    ts = min(ts, S)
    tk = min(tk, K)
    n_chip = N // N_CHIPS
    if tn is None:
        tn = _tn_for(n_chip if chips_out else npart)
    assert npart % tn == 0 and (not chips_out or n_chip % tn == 0)

    def kern(a_ref, b_ref, o_ref):
        @pl.when(pl.program_id(2) == 0)
        def _():
            o_ref[...] = jnp.zeros_like(o_ref)
        o_ref[...] += _dot_tn(a_ref[...], b_ref[...])

    if b_parts == 1:
        b_spec = pl.BlockSpec((ts, tn), lambda k, n, s: (s, n))
    else:
        per = npart // tn
        b_spec = pl.BlockSpec((None, ts, tn), lambda k, n, s: (n // per, s, n % per))
    if chips_out:
        per_chip = n_chip // tn
        o_spec = pl.BlockSpec((None, tk, tn), lambda k, n, s: (n // per_chip, k, n % per_chip))
        out_shape = jax.ShapeDtypeStruct((N_CHIPS, K, n_chip), F32)
    else:
        o_spec = pl.BlockSpec((tk, tn), lambda k, n, s: (k, n))
        out_shape = jax.ShapeDtypeStruct((K, N), F32)
    return pl.pallas_call(
        kern, name=name, grid=(K // tk, N // tn, S // ts),
        in_specs=[pl.BlockSpec((ts, tk), lambda k, n, s: (s, k)), b_spec],
        out_specs=o_spec, out_shape=out_shape,
        compiler_params=_cp(("parallel", "parallel", "arbitrary")),
    )(a, b)


def mm_w_chips3(a, b3, name, *, ts=512):
    S, K = a.shape
    P = b3.shape[2]
    n_chip = 3 * P // N_CHIPS
    ts = min(ts, S)
    pieces = []
    for chip in range(N_CHIPS):
        lo, hi = chip * n_chip, (chip + 1) * n_chip
        while lo < hi:
            part = lo // P
            w = min(hi, (part + 1) * P) - lo
            pieces.append((chip, lo - chip * n_chip, part, lo - part * P, w))
            lo += w

    def kern(a_ref, b_ref, o_ref):
        @pl.when(pl.program_id(0) == 0)
        def _():
            o_ref[...] = jnp.zeros_like(o_ref)
        at = a_ref[...].T
        for chip, oc, part, pc, w in pieces:
            o_ref[chip, :, oc:oc + w] += _dot(at, b_ref[part, :, pc:pc + w])

    return pl.pallas_call(
        kern, name=name, grid=(S // ts,),
        in_specs=[pl.BlockSpec((ts, K), lambda s: (s, 0)), pl.BlockSpec((3, ts, P), lambda s: (0, s, 0))],
        out_specs=pl.BlockSpec((N_CHIPS, K, n_chip), lambda s: (0, 0, 0)),
        out_shape=jax.ShapeDtypeStruct((N_CHIPS, K, n_chip), F32),
        compiler_params=_cp(("arbitrary",)),
    )(a, b3)


def mm_f32(a, b, name):
    def kern(a_ref, b_ref, o_ref):
        o_ref[...] = jnp.dot(a_ref[...], b_ref[...], precision=HIGHEST, preferred_element_type=F32)

    return pl.pallas_call(kern, name=name, out_shape=jax.ShapeDtypeStruct((a.shape[0], b.shape[1]), F32),
                          compiler_params=pltpu.CompilerParams(vmem_limit_bytes=VMEM_LIMIT))(a, b)


def _ln_bwd_tile(dxo_t, x_ref, y_ref, gp_ref, lg_ref, dz_ref, dy_ref, s_ref):
    yv = y_ref[...]
    z = ALPHA * x_ref[...] + gp_ref[...] * yv
    mu = jnp.mean(z, axis=-1, keepdims=True)
    zc = z - mu
    var = jnp.mean(zc * zc, axis=-1, keepdims=True)
    rstd = lax.rsqrt(var + LN_EPS)
    xhat = zc * rstd
    dxh = dxo_t * lg_ref[...]
    dz = rstd * (dxh - jnp.mean(dxh, axis=-1, keepdims=True)
                 - xhat * jnp.mean(dxh * xhat, axis=-1, keepdims=True))
    dz_ref[...] = dz
    dy_ref[...] = (gp_ref[...] * dz).astype(BF16)
    s_ref[0:1, :] += jnp.sum(dxo_t * xhat, axis=0, keepdims=True)
    s_ref[1:2, :] += jnp.sum(dxo_t, axis=0, keepdims=True)
    s_ref[2:3, :] += jnp.sum(dz * yv, axis=0, keepdims=True)


def ln_bwd(dxo, x_in, y, gate1p, ln_g, name, *, tm=256):
    S, D = dxo.shape
    tm = min(tm, S)

    def kern(dxo_ref, x_ref, y_ref, gp_ref, lg_ref, dz_ref, dy_ref, s_ref):
        @pl.when(pl.program_id(0) == 0)
        def _():
            s_ref[...] = jnp.zeros_like(s_ref)
        _ln_bwd_tile(dxo_ref[...], x_ref, y_ref, gp_ref, lg_ref, dz_ref, dy_ref, s_ref)

    tile = pl.BlockSpec((tm, D), lambda i: (i, 0))
    return pl.pallas_call(
        kern, name=name, grid=(S // tm,),
        in_specs=[tile, tile, tile, _row_spec(D), _row_spec(D)],
        out_specs=[tile, tile, pl.BlockSpec((8, D), lambda i: (0, 0))],
        out_shape=[jax.ShapeDtypeStruct((S, D), F32), jax.ShapeDtypeStruct((S, D), BF16),
                   jax.ShapeDtypeStruct((8, D), F32)],
        compiler_params=_cp(("arbitrary",)),
    )(dxo, x_in, y, gate1p, ln_g)


def _tri64():
    r = lax.broadcasted_iota(jnp.int32, (CHUNK, CHUNK), 0)
    c = lax.broadcasted_iota(jnp.int32, (CHUNK, CHUNK), 1)
    return r >= c


def _gla_chunk_common(proj_ref, rows, b, h):
    kc = slice(h * GLA_DK_HEAD, (h + 1) * GLA_DK_HEAD)
    bh = b[:, kc]
    ep = jnp.exp(bh)
    en = jnp.exp(-bh)
    bl = bh[CHUNK - 1:CHUNK, :]
    ee = jnp.exp(bl - bh)
    dec = jnp.exp(bl)
    q = proj_ref[rows, h * GLA_DK_HEAD:(h + 1) * GLA_DK_HEAD] * GLA_SCALE
    k = proj_ref[rows, GLA_DK + h * GLA_DK_HEAD:GLA_DK + (h + 1) * GLA_DK_HEAD]
    v = proj_ref[rows, 2 * GLA_DK + h * GLA_DV_HEAD:2 * GLA_DK + (h + 1) * GLA_DV_HEAD]
    g = proj_ref[rows, 2 * GLA_DK + GLA_DV + h * GLA_DV_HEAD:2 * GLA_DK + GLA_DV + (h + 1) * GLA_DV_HEAD]
    return ep, en, ee, dec, q, k, v, g


def gla_fwd(proj, wgk_p, bgk, gnorm, name, side=None):
    S = proj.shape[0]
    TB = min(GLA_TB, S)
    ncb = TB // CHUNK

    def kern(proj_ref, wgk_ref, bgk_ref, gn_ref, zg_ref, st_ref, state_scr, la_scr):
        @pl.when(pl.program_id(0) == 0)
        def _():
            state_scr[...] = jnp.zeros_like(state_scr)
        lr = proj_ref[:, GLA_LR_OFF:GLA_IN_PAD].astype(BF16)
        gk = _dot(lr, wgk_ref[...]) + bgk_ref[...]
        la_scr[...] = _log_sigmoid(gk) * (1.0 / 16.0)
        lower = _tri64()
        tri = lower.astype(F32)

        def group(gi, carry):
            rows = [pl.ds(pl.multiple_of((gi * GLA_GROUP + g) * CHUNK, CHUNK), CHUNK) for g in range(GLA_GROUP)]
            b = [jnp.dot(tri, la_scr[r, :], precision=HIGHEST, preferred_element_type=F32) for r in rows]
            P = [(g, h) for g in range(GLA_GROUP) for h in range(GLA_HEADS)]
            cm = {p: _gla_chunk_common(proj_ref, rows[p[0]], b[p[0]], p[1]) for p in P}
            qf = {p: (cm[p][4] * cm[p][0]).astype(BF16) for p in P}
            kn = {p: (cm[p][5] * cm[p][1]).astype(BF16) for p in P}
            qn = {p: (cm[p][4] * cm[p][1]).astype(BF16) for p in P}
            kp = {p: (cm[p][5] * cm[p][0]).astype(BF16) for p in P}
            ke = {p: (cm[p][5] * cm[p][2]).astype(BF16) for p in P}
            vb = {p: cm[p][6].astype(BF16) for p in P}
            a_f = {p: _dot_nt(qf[p], kn[p]) for p in P}
            a_b = {p: _dot_nt(qn[p], kp[p]) for p in P}
            upd = {p: _dot_tn(vb[p], ke[p]) for p in P}
            st = {(0, h): state_scr[h] for h in range(GLA_HEADS)}
            for g in range(GLA_GROUP):
                for h in range(GLA_HEADS):
                    st[(g + 1, h)] = st[(g, h)] * cm[(g, h)][3] + upd[(g, h)]
            o_st = {p: _dot_nt(qf[p], st[p].astype(BF16)) for p in P}
            amat = {p: jnp.where(lower, a_f[p], a_b[p]).astype(BF16) for p in P}
            o = {p: _dot(amat[p], vb[p]) + o_st[p] for p in P}
            for g, h in P:
                st_ref[gi * GLA_GROUP + g, h] = st[(g, h)]
            for h in range(GLA_HEADS):
                state_scr[h] = st[(GLA_GROUP, h)]
            for g, h in P:
                gate = cm[(g, h)][7]
                vc = slice(h * GLA_DV_HEAD, (h + 1) * GLA_DV_HEAD)
                r = lax.rsqrt(jnp.mean(o[(g, h)] * o[(g, h)], axis=-1, keepdims=True) + RMS_EPS)
                on = (o[(g, h)] * r) * gn_ref[:, vc]
                zg_ref[rows[g], vc] = (on * (gate * _sigmoid(gate))).astype(BF16)
            return carry

        lax.fori_loop(0, ncb // GLA_GROUP, group, 0)

    return hosted_call(
        kern, side, name=name, grid=(S // TB,),
        in_specs=[pl.BlockSpec((TB, GLA_IN_PAD), lambda i: (i, 0)),
                  pl.BlockSpec((128, GLA_DK), lambda i: (0, 0)), _row_spec(GLA_DK), _row_spec(GLA_DV)],
        out_specs=[pl.BlockSpec((TB, GLA_DV), lambda i: (i, 0)),
                   pl.BlockSpec((ncb, GLA_HEADS, GLA_DV_HEAD, GLA_DK_HEAD), lambda i: (i, 0, 0, 0))],
        out_shape=[jax.ShapeDtypeStruct((S, GLA_DV), BF16),
                   jax.ShapeDtypeStruct((S // CHUNK, GLA_HEADS, GLA_DV_HEAD, GLA_DK_HEAD), F32)],
        scratch_shapes=[pltpu.VMEM((GLA_HEADS, GLA_DV_HEAD, GLA_DK_HEAD), F32), pltpu.VMEM((TB, GLA_DK), F32)],
        dims=("arbitrary",), args=(proj, wgk_p, bgk, gnorm))


def gla_bwd(proj, states, dzg, wgk_p, bgk, gnorm, name, side=None):
    S = proj.shape[0]
    TB = min(GLA_TB, S)
    ncb = TB // CHUNK
    nb = S // TB

    def kern(proj_ref, st_ref, dzg_ref, wgk_ref, bgk_ref, gn_ref,
             dproj_ref, dwgk_ref, dbgk_ref, dgn_ref, dstate_scr, la_scr, gk_scr, dgk_scr):
        @pl.when(pl.program_id(0) == 0)
        def _():
            dstate_scr[...] = jnp.zeros_like(dstate_scr)
            dwgk_ref[...] = jnp.zeros_like(dwgk_ref)
            dbgk_ref[...] = jnp.zeros_like(dbgk_ref)
            dgn_ref[...] = jnp.zeros_like(dgn_ref)
        lr = proj_ref[:, GLA_LR_OFF:GLA_IN_PAD].astype(BF16)
        gk = _dot(lr, wgk_ref[...]) + bgk_ref[...]
        gk_scr[...] = gk
        la_scr[...] = _log_sigmoid(gk) * (1.0 / 16.0)
        lower = _tri64()
        tri = lower.astype(F32)
        r_i = lax.broadcasted_iota(jnp.int32, (CHUNK, CHUNK), 0)
        c_i = lax.broadcasted_iota(jnp.int32, (CHUNK, CHUNK), 1)
        triu = (c_i >= r_i).astype(F32)
        last_row = lax.broadcasted_iota(jnp.int32, (CHUNK, GLA_DK_HEAD), 0) == CHUNK - 1

        def group(gi, carry):
            cs = [ncb - 1 - (gi * GLA_GROUP + g) for g in range(GLA_GROUP)]
            rows = [pl.ds(pl.multiple_of(c * CHUNK, CHUNK), CHUNK) for c in cs]
            b = [jnp.dot(tri, la_scr[r, :], precision=HIGHEST, preferred_element_type=F32) for r in rows]
            P = [(g, h) for g in range(GLA_GROUP) for h in range(GLA_HEADS)]
            kcs = [slice(h * GLA_DK_HEAD, (h + 1) * GLA_DK_HEAD) for h in range(GLA_HEADS)]
            vcs = [slice(h * GLA_DV_HEAD, (h + 1) * GLA_DV_HEAD) for h in range(GLA_HEADS)]
            cm = {p: _gla_chunk_common(proj_ref, rows[p[0]], b[p[0]], p[1]) for p in P}
            ep, en, ee, dec = ({p: cm[p][i] for p in P} for i in range(4))
            qf = {p: cm[p][4] * cm[p][0] for p in P}
            kn = {p: cm[p][5] * cm[p][1] for p in P}
            qn = {p: cm[p][4] * cm[p][1] for p in P}
            kp = {p: cm[p][5] * cm[p][0] for p in P}
            ke = {p: cm[p][5] * cm[p][2] for p in P}
            qf_b, kn_b, qn_b, kp_b, ke_b = ({p: t[p].astype(BF16) for p in P} for t in (qf, kn, qn, kp, ke))
            vb = {p: cm[p][6].astype(BF16) for p in P}
            st = {p: st_ref[cs[p[0]], p[1]] for p in P}
            st_b = {p: st[p].astype(BF16) for p in P}
            a_f = {p: _dot_nt(qf_b[p], kn_b[p]) for p in P}
            a_b = {p: _dot_nt(qn_b[p], kp_b[p]) for p in P}
            o_st = {p: _dot_nt(qf_b[p], st_b[p]) for p in P}
            amat = {p: jnp.where(lower, a_f[p], a_b[p]).astype(BF16) for p in P}
            o = {p: _dot(amat[p], vb[p]) + o_st[p] for p in P}
            do_b, dgs = {}, {}
            for p in P:
                g, h = p
                r = lax.rsqrt(jnp.mean(o[p] * o[p], axis=-1, keepdims=True) + RMS_EPS)
                oh = o[p] * r
                gn = gn_ref[:, vcs[h]]
                gate = cm[p][7]
                sg = _sigmoid(gate)
                dz = dzg_ref[rows[g], vcs[h]]
                don = dz * (gate * sg)
                dgs[p] = dz * (oh * gn) * (sg * (1.0 + gate * (1.0 - sg)))
                dgn_ref[:, vcs[h]] += jnp.sum(don * oh, axis=0, keepdims=True)
                doh = don * gn
                do_b[p] = (r * (doh - oh * jnp.mean(doh * oh, axis=-1, keepdims=True))).astype(BF16)
            da = {p: _dot_nt(do_b[p], vb[p]) for p in P}
            dv_a = {p: _dot_tn(amat[p], do_b[p]) for p in P}
            dqf_st = {p: _dot(do_b[p], st_b[p]) for p in P}
            dst_upd = {p: _dot_tn(do_b[p], qf_b[p]) for p in P}
            dst = {(0, h): dstate_scr[h] for h in range(GLA_HEADS)}
            for g in range(GLA_GROUP):
                for h in range(GLA_HEADS):
                    dst[(g + 1, h)] = dst[(g, h)] * dec[(g, h)] + dst_upd[(g, h)]
            for h in range(GLA_HEADS):
                dstate_scr[h] = dst[(GLA_GROUP, h)]
            dst_b = {p: dst[p].astype(BF16) for p in P}
            dv = {p: dv_a[p] + _dot_nt(ke_b[p], dst_b[p]) for p in P}
            dke = {p: _dot(vb[p], dst_b[p]) for p in P}
            da_f = {p: jnp.where(lower, da[p], 0.0).astype(BF16) for p in P}
            da_b = {p: jnp.where(lower, 0.0, da[p]).astype(BF16) for p in P}
            dqf = {p: _dot(da_f[p], kn_b[p]) + dqf_st[p] for p in P}
            dkn = {p: _dot_tn(da_f[p], qf_b[p]) for p in P}
            dqn = {p: _dot(da_b[p], kp_b[p]) for p in P}
            dkp = {p: _dot_tn(da_b[p], qn_b[p]) for p in P}
            dbs = {}
            for p in P:
                ddec = jnp.sum(dst[p] * st[p], axis=0, keepdims=True)
                db = dqf[p] * qf[p] - dkn[p] * kn[p] - dqn[p] * qn[p] + dkp[p] * kp[p] - dke[p] * ke[p]
                dbl = jnp.sum(dke[p] * ke[p], axis=0, keepdims=True) + ddec * dec[p]
                dbs[p] = db + jnp.where(last_row, dbl, 0.0)
            dla = {p: jnp.dot(triu, dbs[p], precision=HIGHEST, preferred_element_type=F32) for p in P}
            for p in P:
                g, h = p
                dq = (dqf[p] * ep[p] + dqn[p] * en[p]) * GLA_SCALE
                dk = dkn[p] * en[p] + dkp[p] * ep[p] + dke[p] * ee[p]
                dgk_scr[rows[g], kcs[h]] = dla[p] * (1.0 / 16.0) * _sigmoid(-gk_scr[rows[g], kcs[h]])
                dproj_ref[rows[g], kcs[h]] = dq.astype(BF16)
                dproj_ref[rows[g], GLA_DK + h * GLA_DK_HEAD:GLA_DK + (h + 1) * GLA_DK_HEAD] = dk.astype(BF16)
                dproj_ref[rows[g], 2 * GLA_DK + h * GLA_DV_HEAD:2 * GLA_DK + (h + 1) * GLA_DV_HEAD] = dv[p].astype(BF16)
                dproj_ref[rows[g], 2 * GLA_DK + GLA_DV + h * GLA_DV_HEAD:
                          2 * GLA_DK + GLA_DV + (h + 1) * GLA_DV_HEAD] = dgs[p].astype(BF16)
            return carry

        lax.fori_loop(0, ncb // GLA_GROUP, group, 0)
        dgk = dgk_scr[...]
        dgk_b = dgk.astype(BF16)
        dproj_ref[:, GLA_LR_OFF:GLA_IN_PAD] = _dot_nt(dgk_b, wgk_ref[...]).astype(BF16)
        dwgk_ref[...] += _dot_tn(lr, dgk_b)
        dbgk_ref[...] += jnp.sum(dgk, axis=0, keepdims=True)

    rev = lambda i: (nb - 1 - i, 0)
    return hosted_call(
        kern, side, name=name, grid=(nb,),
        in_specs=[pl.BlockSpec((TB, GLA_IN_PAD), rev),
                  pl.BlockSpec((ncb, GLA_HEADS, GLA_DV_HEAD, GLA_DK_HEAD), lambda i: (nb - 1 - i, 0, 0, 0)),
                  pl.BlockSpec((TB, GLA_DV), rev),
                  pl.BlockSpec((128, GLA_DK), lambda i: (0, 0)), _row_spec(GLA_DK), _row_spec(GLA_DV)],
        out_specs=[pl.BlockSpec((TB, GLA_IN_PAD), rev),
                   pl.BlockSpec((128, GLA_DK), lambda i: (0, 0)), _row_spec(GLA_DK), _row_spec(GLA_DV)],
        out_shape=[jax.ShapeDtypeStruct((S, GLA_IN_PAD), BF16), jax.ShapeDtypeStruct((128, GLA_DK), F32),
                   jax.ShapeDtypeStruct((1, GLA_DK), F32), jax.ShapeDtypeStruct((1, GLA_DV), F32)],
        scratch_shapes=[pltpu.VMEM((GLA_HEADS, GLA_DV_HEAD, GLA_DK_HEAD), F32), pltpu.VMEM((TB, GLA_DK), F32),
                        pltpu.VMEM((TB, GLA_DK), F32), pltpu.VMEM((TB, GLA_DK), F32)],
        dims=("arbitrary",), args=(proj, states, dzg, wgk_p, bgk, gnorm))


ATT_TW = 1024
ATT_CLASSES = 3


def _att_window(i):
    return pl.multiple_of(jnp.maximum(i * ATT_TQ - LEFT_CHUNKS * CHUNK, 0), ATT_TQ)


def _att_rel_index():
    e = jnp.arange(ATT_TW)[None, :]
    d = jnp.where(e < ATT_KW, e, e - ATT_TW)
    off = (jnp.arange(ATT_CLASSES) * ATT_TQ)[:, None]
    return jnp.clip(off - d, -MAX_REL, MAX_REL) + MAX_REL


def _row_bits():
    return lax.broadcasted_iota(jnp.int32, (ATT_TQ, ATT_TW), 0)


def att_bias_tiles(rel_bias, name):
    pick = (jnp.arange(384)[:, None] == _att_rel_index().reshape(-1)[None, :]).astype(F32)
    tab = mm_f32(jnp.pad(rel_bias, ((0, 0), (0, 384 - N_REL))), pick, name + "_tab")
    tab = tab.reshape(ATT_HEADS * ATT_CLASSES, 1, ATT_TW)

    def kern(t_ref, o_ref):
        cls = pl.program_id(0) % ATT_CLASSES
        x = jnp.broadcast_to(t_ref[...], (ATT_TQ, ATT_TW))
        x = pltpu.roll(x, 0, 1, stride=1, stride_axis=0)
        x = x[:, :ATT_KW]
        qc = cls * (ATT_TQ // CHUNK) + lax.shift_right_arithmetic(
            lax.broadcasted_iota(jnp.int32, (ATT_TQ, ATT_KW), 0), 6)
        kc = lax.shift_right_arithmetic(lax.broadcasted_iota(jnp.int32, (ATT_TQ, ATT_KW), 1), 6)
        o_ref[...] = jnp.where((kc <= qc) & (kc >= qc - LEFT_CHUNKS), x, NEG_INF)

    return pl.pallas_call(
        kern, name=name, grid=(ATT_HEADS * ATT_CLASSES,),
        in_specs=[pl.BlockSpec((None, 1, ATT_TW), lambda i: (i, 0, 0))],
        out_specs=pl.BlockSpec((None, ATT_TQ, ATT_KW), lambda i: (i, 0, 0)),
        out_shape=jax.ShapeDtypeStruct((ATT_HEADS * ATT_CLASSES, ATT_TQ, ATT_KW), F32),
        compiler_params=_cp(("parallel",)),
    )(tab)


def att_bias_grad(dbt, name):
    def kern(d_ref, o_ref):
        x = jnp.concatenate([d_ref[...], jnp.zeros((ATT_TQ, ATT_TW - ATT_KW), F32)], axis=1)
        row = _row_bits()
        for b in range(8):
            x = jnp.where((row & (1 << b)) != 0, pltpu.roll(x, ATT_TW - (1 << b), axis=1), x)
        o_ref[...] = jnp.sum(x, axis=0, keepdims=True)

    diag = pl.pallas_call(
        kern, name=name + "_diag", grid=(ATT_HEADS * ATT_CLASSES,),
        in_specs=[pl.BlockSpec((None, ATT_TQ, ATT_KW), lambda i: (i, 0, 0))],
        out_specs=pl.BlockSpec((None, 1, ATT_TW), lambda i: (i, 0, 0)),
        out_shape=jax.ShapeDtypeStruct((ATT_HEADS * ATT_CLASSES, 1, ATT_TW), F32),
        compiler_params=_cp(("parallel",)),
    )(dbt)
    diag = diag.reshape(ATT_HEADS, ATT_CLASSES * ATT_TW)
    onehot = (_att_rel_index().reshape(-1)[:, None] == jnp.arange(384)[None, :]).astype(F32)
    return mm_f32(diag, onehot, name + "_bins")[:, :N_REL]


def _att_scores(q_ref, kw, bias_ref):
    hs = [slice(hh * ATT_HD, (hh + 1) * ATT_HD) for hh in range(2)]
    q = [q_ref[:, h] * ATT_SCALE for h in hs]
    k = [kw[:, h] for h in hs]
    s = [_dot_nt(q[hh], k[hh]) + bias_ref[hh] for hh in range(2)]
    e = [jnp.exp(t - jnp.max(t, axis=-1, keepdims=True)) for t in s]
    inv = [1.0 / jnp.sum(t, axis=-1, keepdims=True) for t in e]
    return hs, q, k, e, inv


def _att_specs(S):
    nq = D_MODEL // 128
    q_spec = pl.BlockSpec((ATT_TQ, 128), lambda p, i: (i, p))
    k_spec = pl.BlockSpec((S, 128), lambda p, i: (0, nq + p))
    v_spec = pl.BlockSpec((S, 128), lambda p, i: (0, 2 * nq + p))
    b_spec = pl.BlockSpec((2, None, ATT_TQ, ATT_KW), lambda p, i: (p, jnp.minimum(i, ATT_CLASSES - 1), 0, 0))
    return q_spec, k_spec, v_spec, b_spec


def attn_fwd(qkv, bias, name, side=None):
    S = qkv.shape[0]
    q_spec, k_spec, v_spec, b_spec = _att_specs(S)

    def kern(q_ref, k_ref, v_ref, bias_ref, o_ref):
        ws = _att_window(pl.program_id(1))
        kw = k_ref[pl.ds(ws, ATT_KW), :]
        vw = v_ref[pl.ds(ws, ATT_KW), :]
        hs, _, _, e, inv = _att_scores(q_ref, kw, bias_ref)
        outs = [_dot(e[hh].astype(BF16), vw[:, hs[hh]]) * inv[hh] for hh in range(2)]
        o_ref[...] = jnp.concatenate(outs, axis=1).astype(BF16)

    return hosted_call(
        kern, side, name=name, grid=(ATT_HEADS // 2, S // ATT_TQ),
        in_specs=[q_spec, k_spec, v_spec, b_spec],
        out_specs=[pl.BlockSpec((ATT_TQ, 128), lambda p, i: (i, p))],
        out_shape=[jax.ShapeDtypeStruct((S, D_MODEL), BF16)],
        scratch_shapes=[], dims=("parallel", "arbitrary"), args=(qkv, qkv, qkv, bias))


def attn_bwd(qkv, bias, do, name, side=None):
    S = qkv.shape[0]
    nblk = S // ATT_TQ
    q_spec, k_spec, v_spec, b_spec = _att_specs(S)

    def kern(q_ref, k_ref, v_ref, bias_ref, do_ref, dqkv_ref, db_ref, dk_scr, dv_scr):
        i = pl.program_id(1)

        @pl.when(i == 0)
        def _():
            dk_scr[...] = jnp.zeros_like(dk_scr)
            dv_scr[...] = jnp.zeros_like(dv_scr)
            db_ref[...] = jnp.zeros_like(db_ref)
        ws = _att_window(i)
        win = pl.ds(ws, ATT_KW)
        kw = k_ref[win, :]
        vw = v_ref[win, :]
        o_cls = jnp.minimum(i, ATT_CLASSES - 1)
        R2 = range(2)
        hs, q, k, e, inv = _att_scores(q_ref, kw, bias_ref)
        do_h = [do_ref[:, h] for h in hs]
        dp = [_dot_nt(do_h[hh], vw[:, hs[hh]]) for hh in R2]
        p = [e[hh] * inv[hh] for hh in R2]
        dvs = [_dot_tn(p[hh].astype(BF16), do_h[hh]) for hh in R2]
        ds = [p[hh] * (dp[hh] - jnp.sum(p[hh] * dp[hh], axis=-1, keepdims=True)) for hh in R2]
        ds_b = [t.astype(BF16) for t in ds]
        dqs = [_dot(ds_b[hh], k[hh]) * ATT_SCALE for hh in R2]
        dks = [_dot_tn(ds_b[hh], q[hh]) for hh in R2]
        for hh in R2:
            db_ref[hh, o_cls] += ds[hh]
        dqkv_ref[0, pl.ds(pl.multiple_of(i * ATT_TQ, ATT_TQ), ATT_TQ), :] = jnp.concatenate(dqs, axis=1).astype(BF16)
        dk_scr[win, :] += jnp.concatenate(dks, axis=1)
        dv_scr[win, :] += jnp.concatenate(dvs, axis=1)

        @pl.when(i == nblk - 1)
        def _():
            dqkv_ref[1] = dk_scr[...].astype(BF16)
            dqkv_ref[2] = dv_scr[...].astype(BF16)

    return hosted_call(
        kern, side, name=name, grid=(ATT_HEADS // 2, nblk),
        in_specs=[q_spec, k_spec, v_spec, b_spec, pl.BlockSpec((ATT_TQ, 128), lambda p, i: (i, p))],
        out_specs=[pl.BlockSpec((3, S, 128), lambda p, i: (0, 0, p)),
                   pl.BlockSpec((2, ATT_CLASSES, ATT_TQ, ATT_KW), lambda p, i: (p, 0, 0, 0))],
        out_shape=[jax.ShapeDtypeStruct((3, S, D_MODEL), BF16),
                   jax.ShapeDtypeStruct((ATT_HEADS, ATT_CLASSES, ATT_TQ, ATT_KW), F32)],
        scratch_shapes=[pltpu.VMEM((S, 128), F32), pltpu.VMEM((S, 128), F32)],
        dims=("parallel", "arbitrary"), args=(qkv, qkv, qkv, bias, do))


def colsum3(a3, name):
    P, S, N = a3.shape
    tm = min(512, S)

    def kern(a_ref, o_ref):
        @pl.when(pl.program_id(1) == 0)
        def _():
            o_ref[...] = jnp.zeros_like(o_ref)
        o_ref[...] += jnp.sum(a_ref[...].astype(F32), axis=0, keepdims=True)

    return pl.pallas_call(
        kern, name=name, grid=(P, S // tm),
        in_specs=[pl.BlockSpec((None, tm, N), lambda p, i: (p, i, 0))],
        out_specs=pl.BlockSpec((None, 1, N), lambda p, i: (p, 0, 0)),
        out_shape=jax.ShapeDtypeStruct((P, 1, N), F32),
        compiler_params=_cp(("parallel", "arbitrary")),
    )(a3)


def _me():
    return lax.axis_index("x"), lax.axis_index("y"), lax.axis_index("c")


def _other_chips(x, y):
    return [(1 - x, y), (x, 1 - y), (1 - x, 1 - y)]


def all_gather8(x_shard, name):
    m_per, n = x_shard.shape

    def body(x_ref, out_ref, send_sems, recv_sems, local_sem):
        x, y, c = _me()
        me, sibling = (x, y, c), (x, y, 1 - c)
        chips = _other_chips(x, y)

        def rows(px, py, pc):
            return out_ref.at[pl.ds((4 * px + 2 * py + pc) * m_per, m_per), :]

        def copy(k, block, to, src=None):
            return pltpu.make_async_remote_copy(
                src_ref=rows(*block) if src is None else src, dst_ref=rows(*block),
                send_sem=send_sems.at[k], recv_sem=recv_sems.at[k], device_id=to, device_id_type=MESH)

        mine = pltpu.make_async_copy(x_ref, rows(*me), local_sem)
        mine.start()
        first = [copy(0, me, sibling, src=x_ref)]
        first += [copy(1 + j, me, (*chip, c), src=x_ref) for j, chip in enumerate(chips)]
        for cp in first:
            cp.start()
        passed = [copy(4 + j, (*chip, c), sibling) for j, chip in enumerate(chips)]
        for j, chip in enumerate(chips):
            copy(1 + j, (*chip, c), me).wait_recv()
            passed[j].start()
        copy(0, sibling, me).wait_recv()
        for j, chip in enumerate(chips):
            copy(4 + j, (*chip, 1 - c), me).wait_recv()
        for cp in first + passed:
            cp.wait_send()
        mine.wait()

    return pl.pallas_call(
        body, name=name,
        out_shape=jax.ShapeDtypeStruct((N_DEV * m_per, n), x_shard.dtype),
        in_specs=[pl.BlockSpec(memory_space=pltpu.VMEM)],
        out_specs=pl.BlockSpec(memory_space=pltpu.VMEM),
        scratch_shapes=[pltpu.SemaphoreType.DMA((7,)), pltpu.SemaphoreType.DMA((7,)), pltpu.SemaphoreType.DMA],
        compiler_params=pltpu.CompilerParams(vmem_limit_bytes=VMEM_LIMIT),
    )(x_shard)


def _half_rows(n_rows, c):
    h = n_rows // 2
    return pl.ds(c * h, h)


def _gathered_shape(shape, flavour):
    L, a, b = shape
    return {"col": (L, a, N_CHIPS * b), "row": (L, N_CHIPS * a, b), "lead": (N_CHIPS, L, a, b)}[flavour]


def _gathered_part(out_ref, shape, flavour, s, rows):
    L, a, b = shape
    if flavour == "col":
        return out_ref.at[:, rows, pl.ds(s * b, b)]
    if flavour == "row":
        return out_ref.at[:, pl.ds(s * a + rows.start, rows.size), :]
    return out_ref.at[s, :, rows, :]


def gather_side(shards, flavours):
    n = len(shards)
    shapes = [w.shape for w in shards]

    def copies(w_refs, out_refs, send_sems, recv_sems, local_sems):
        x, y, c = _me()
        sibling = (x, y, 1 - c)
        chips = _other_chips(x, y)
        me_s = 2 * x + y

        def copy(k, src, dst, to):
            return pltpu.make_async_remote_copy(src_ref=src, dst_ref=dst, send_sem=send_sems.at[k],
                                                recv_sem=recv_sems.at[k], device_id=to, device_id_type=MESH)

        own, first, landed, passed, passed_in = [], [], [], [], []
        for w in range(n):
            shp, fl = shapes[w], flavours[w]
            my_half = _half_rows(shp[1], c)
            sib_half = _half_rows(shp[1], 1 - c)
            own.append(copy(7 * w + 6, w_refs[w], _gathered_part(out_refs[w], shp, fl, me_s, pl.ds(0, shp[1])), sibling))
            for j, chip in enumerate(chips):
                s = 2 * chip[0] + chip[1]
                first.append(copy(7 * w + j, w_refs[w].at[:, my_half, :],
                                  _gathered_part(out_refs[w], shp, fl, me_s, my_half), (*chip, c)))
                part = _gathered_part(out_refs[w], shp, fl, s, my_half)
                landed.append(copy(7 * w + j, part, part, (*chip, c)))
                passed.append(copy(7 * w + 3 + j, part, part, sibling))
                theirs = _gathered_part(out_refs[w], shp, fl, s, sib_half)
                passed_in.append(copy(7 * w + 3 + j, theirs, theirs, sibling))
        return own, first, landed, passed, passed_in

    def start(*refs):
        own, first, _, _, _ = copies(*refs)
        for cp in first + own:
            cp.start()

    def wait(*refs):
        own, first, landed, passed, passed_in = copies(*refs)
        for arrived, onward in zip(landed, passed):
            arrived.wait_recv()
            onward.start()
        for cp in passed_in:
            cp.wait_recv()
        for cp in own:
            cp.wait()
        for cp in first + passed:
            cp.wait_send()

    out_shapes = [jax.ShapeDtypeStruct(_gathered_shape(s, f), w.dtype) for w, s, f in zip(shards, shapes, flavours)]
    return Side(shards, out_shapes, 7 * n, 0, start, wait)


def swap_side(gs):
    n = len(gs)

    def copies(g_refs, out_refs, send_sems, recv_sems, local_sems):
        x, y, c = _me()
        return [pltpu.make_async_remote_copy(
            src_ref=g_refs[w].at[:, _half_rows(gs[w].shape[1], 1 - c), :], dst_ref=out_refs[w],
            send_sem=send_sems.at[w], recv_sem=recv_sems.at[w], device_id=(x, y, 1 - c), device_id_type=MESH)
            for w in range(n)]

    def start(*refs):
        for cp in copies(*refs):
            cp.start()

    def wait(*refs):
        for cp in copies(*refs):
            cp.wait()

    out_shapes = [jax.ShapeDtypeStruct((g.shape[0], g.shape[1] // 2, g.shape[2]), g.dtype) for g in gs]
    return Side(gs, out_shapes, n, 0, start, wait)


def add_half(g, r1, c_idx, name):
    n, R, C = g.shape
    half = R // 2
    tr = _rows_block(half, C)
    nbh = half // tr

    def kern(c_ref, g_ref, r_ref, o_ref, ob_ref):
        p = g_ref[...] + r_ref[...]
        o_ref[...] = p
        ob_ref[...] = p.astype(BF16)

    spec = pl.BlockSpec((1, tr, C), lambda d, r, c_ref: (d, r, 0))
    return pl.pallas_call(
        kern, name=name,
        grid_spec=pltpu.PrefetchScalarGridSpec(
            num_scalar_prefetch=1, grid=(n, nbh),
            in_specs=[pl.BlockSpec((1, tr, C), lambda d, r, c_ref: (d, c_ref[0] * nbh + r, 0)), spec],
            out_specs=[spec, spec]),
        out_shape=[jax.ShapeDtypeStruct((n, half, C), F32), jax.ShapeDtypeStruct((n, half, C), BF16)],
        compiler_params=_cp(("parallel", "parallel")),
    )(c_idx, g, r1)


def exchange_side(ps):
    n = len(ps)

    def copies(p_refs, out_refs, send_sems, recv_sems, local_sems):
        x, y, c = _me()
        return [pltpu.make_async_remote_copy(
            src_ref=p_refs[w].at[2 * chip[0] + chip[1]], dst_ref=out_refs[w].at[j],
            send_sem=send_sems.at[3 * w + j], recv_sem=recv_sems.at[3 * w + j],
            device_id=(*chip, c), device_id_type=MESH)
            for w in range(n) for j, chip in enumerate(_other_chips(x, y))]

    def start(*refs):
        for cp in copies(*refs):
            cp.start()

    def wait(*refs):
        for cp in copies(*refs):
            cp.wait()

    return Side(ps, [jax.ShapeDtypeStruct((3,) + p.shape[1:], p.dtype) for p in ps], 3 * n, 0, start, wait)


def add_chips(p, r2, chip_idx, name):
    n, H, C = p.shape
    tr = _rows_block(H, C)

    def kern(s_ref, p_ref, r_ref, o_ref):
        o_ref[...] = ((p_ref[0] + r_ref[0].astype(F32)) + r_ref[1].astype(F32)) + r_ref[2].astype(F32)

    return pl.pallas_call(
        kern, name=name,
        grid_spec=pltpu.PrefetchScalarGridSpec(
            num_scalar_prefetch=1, grid=(H // tr,),
            in_specs=[pl.BlockSpec((1, tr, C), lambda r, s_ref: (s_ref[0], r, 0)),
                      pl.BlockSpec((3, tr, C), lambda r, s_ref: (0, r, 0))],
            out_specs=pl.BlockSpec((tr, C), lambda r, s_ref: (r, 0))),
        out_shape=jax.ShapeDtypeStruct((H, C), F32),
        compiler_params=_cp(("parallel",)),
    )(chip_idx, p, r2)


def swap_reduced(ss, name):
    n = len(ss)

    def body(*refs):
        s_refs, out_refs = refs[:n], refs[n:2 * n]
        send_sems, recv_sems = refs[2 * n:]
        x, y, c = _me()
        cps = [pltpu.make_async_remote_copy(src_ref=s_refs[w], dst_ref=out_refs[w], send_sem=send_sems.at[w],
                                            recv_sem=recv_sems.at[w], device_id=(x, y, 1 - c), device_id_type=MESH)
               for w in range(n)]
        for cp in cps:
            cp.start()
        for cp in cps:
            cp.wait()

    any_spec = pl.BlockSpec(memory_space=pl.ANY)
    return pl.pallas_call(
        body, name=name, out_shape=[jax.ShapeDtypeStruct(s.shape, s.dtype) for s in ss],
        in_specs=[any_spec] * n, out_specs=[any_spec] * n,
        scratch_shapes=[pltpu.SemaphoreType.DMA((n,)), pltpu.SemaphoreType.DMA((n,))],
    )(*ss)


BIG = (("gla_w_in", 2, (1024, GLA_IN // N_CHIPS), "lead"), ("gla_w_out", 2, (256, 1024), "row"),
       ("att_w_in", 2, (1024, 768), "col"), ("att_w_out", 2, (256, 1024), "row"),
       ("ff_w1", 4, (1024, 1024), "col"), ("ff_w2", 4, (1024, 1024), "row"))
FLAVOUR = {n: f for n, _, _, f in BIG}


def layer_weights(i):
    mixer = "gla" if i % 2 == 0 else "att"
    return (("in", mixer + "_w_in", i // 2), ("out", mixer + "_w_out", i // 2), ("w1", "ff_w1", i), ("w2", "ff_w2", i))


class Comm:
    def __init__(self, weights, core, chip):
        self.weights, self.core, self.chip = weights, core, chip
        self.c_idx = jnp.reshape(core, (1,)).astype(jnp.int32)
        self.chip_idx = jnp.reshape(chip, (1,)).astype(jnp.int32)
        self.reduced = {}

    def gather(self, items):
        shards = [self.weights[n][l:l + 1].astype(BF16) for _, n, l in items]
        return gather_side(shards, [FLAVOUR[n] for _, n, _ in items])

    def full_weights(self, items, gathered):
        W = {}
        for (role, n, _), w in zip(items, gathered):
            if n == "gla_w_in":
                w = jnp.pad(w.transpose(1, 2, 0, 3).reshape(1, D_MODEL, GLA_IN), ((0, 0), (0, 0), (0, GLA_IN_PAD - GLA_IN)))
            W[role] = (w, 0)
        return W

    def gather_now(self, items, name):
        return self.full_weights(items, run_side(self.gather(items), name))

    def swap(self, items):
        return swap_side([g for _, _, g in items])

    def reduce_begin(self, tag, items, swapped):
        ps = [add_half(g, r, self.c_idx, f"rs_add2_{tag}_{w}") for w, ((_, _, g), r) in enumerate(zip(items, swapped))]
        return tag, [(n, l) for n, l, _ in items], ps

    def exchange(self, pending):
        return exchange_side([pb for _, pb in pending[2]])

    def reduce_mid(self, pending, landed):
        tag, keys, ps = pending
        for w, (key, (p, _), r) in enumerate(zip(keys, ps, landed)):
            self.reduced[key] = add_chips(p, r, self.chip_idx, f"rs_add4_{tag}_{w}")

    def reduce_tail(self, tag, items):
        pending = self.reduce_begin(tag, items, run_side(self.swap(items), f"rs_swap_{tag}"))
        self.reduce_mid(pending, run_side(self.exchange(pending), f"rs_xchg_{tag}"))

    def reduce_end(self):
        keys = [(n, l) for n, L, _, _ in BIG for l in range(L)]
        mine = [self.reduced[k] for k in keys]
        theirs = swap_reduced(mine, "rs_join")
        low = self.core == 0
        full = {k: jnp.concatenate([jnp.where(low, m, t), jnp.where(low, t, m)], axis=0)
                for k, m, t in zip(keys, mine, theirs)}
        return {n: jnp.stack([full[(n, l)] for l in range(L)]) for n, L, _, _ in BIG}


def local_step(x, target, mods, comm, small):
    S, D = x.shape
    row = lambda v: v.reshape(1, -1)
    saved = []
    tiles = [att_bias_tiles(small["att_rel_bias"][j], f"att_tiles_{j}").reshape(ATT_HEADS, ATT_CLASSES, ATT_TQ, ATT_KW)
             for j in range(2)]
    wgk_p = [jnp.pad(small["gla_w_gk2"][j], ((0, 128 - GLA_RANK), (0, 0))).astype(BF16) for j in range(2)]

    u1 = modulate(x, row(mods[0, 1]), row(mods[0, 0]), "mod_first")
    Ws = [dict() for _ in range(DEPTH)]
    items0 = layer_weights(0)
    Ws[0].update(comm.gather_now(items0[:1], "gather_w0"))
    for i in range(DEPTH):
        j = i // 2
        W = Ws[i]
        sh1, sc1, g1, sh2, sc2, g2 = (row(mods[i, k]) for k in range(6))
        nxt = min(i + 1, DEPTH - 1)
        more = i + 1 < DEPTH
        nxt_items = layer_weights(nxt)
        side_in = comm.gather(items0[1:]) if i == 0 else None
        side_mix = comm.gather(nxt_items[:2]) if more else None
        side_ff = comm.gather(nxt_items[2:]) if more else None
        if i % 2 == 0:
            proj = mm_plain(u1, *W["in"], f"gla_in_{i}", side=side_in)
        else:
            proj = mm_plain(u1, *W["in"], f"att_in_{i}", mode="bf16", bias=row(small["att_b_in"][j]), side=side_in)
        proj, landed = proj if side_in is not None else (proj, [])
        if i == 0:
            W.update(comm.full_weights(items0[1:], landed))
        if i % 2 == 0:
            (zmix, states), landed = gla_fwd(proj, wgk_p[j], row(small["gla_b_gk"][j]), row(small["gla_g_norm"][j]),
                                             f"gla_fwd_{i}", side_mix)
        else:
            (zmix,), landed = attn_fwd(proj, tiles[j], f"att_fwd_{i}", side_mix)
            states = None
        if more:
            Ws[nxt].update(comm.full_weights(nxt_items[:2], landed))
        y1, x_mid, u2 = mm_down_ln(zmix, *W["out"], x, 1.0 + g1, row(small["ln_g"][i, 0]), row(small["ln_b"][i, 0]),
                                   sc2, sh2, f"mix_out_{i}")
        act = mm_plain(u2, *W["w1"], f"ff_up_{i}", mode="mlp_up", side=side_ff)
        act, landed = act if side_ff is not None else (act, [])
        if more:
            Ws[nxt].update(comm.full_weights(nxt_items[2:], landed))
        y2, x_out, u_next = mm_down_ln(act, *W["w2"], x_mid, 1.0 + g2, row(small["ln_g"][i, 1]),
                                       row(small["ln_b"][i, 1]), row(mods[nxt, 1]), row(mods[nxt, 0]), f"ff_out_{i}")
        saved.append(dict(x_in=x, u1=u1, proj=proj, zmix=zmix, states=states, y1=y1, x_mid=x_mid, u2=u2,
                          act=act, y2=y2))
        x, u1 = x_out, u_next

    dx, sq = loss_head(x, target, "loss_head")

    g_small = dict(ln_g=[None] * DEPTH, ln_b=[None] * DEPTH, gla_w_gk2=[None] * 2, gla_b_gk=[None] * 2,
                   gla_g_norm=[None] * 2, att_b_in=[None] * 2, att_rel_bias=[None] * 2)
    dmods = [None] * DEPTH
    later = []
    top = saved[DEPTH - 1]
    dz2, dy2, s_ln2 = ln_bwd(dx, top["x_mid"], top["y2"], 1.0 + row(mods[DEPTH - 1, 5]),
                             row(small["ln_g"][DEPTH - 1, 1]), "ln2_bwd_top")

    for i in reversed(range(DEPTH)):
        j = i // 2
        sv = saved[i]
        W = Ws[i]
        sh1, sc1, g1, sh2, sc2, g2 = (row(mods[i, k]) for k in range(6))
        dh = mm_plain(dy2, *W["w2"], f"ff_dn_{i}", mode="mlp_dn", nt=True, h=sv["act"])
        g_w2 = mm_w(sv["act"], dy2, f"ff_w2g_{i}").reshape(N_CHIPS, D_FF // N_CHIPS, D)
        g_w1 = mm_w(sv["u2"], dh, f"ff_w1g_{i}", chips_out=True)
        items = [("ff_w1", i, g_w1), ("ff_w2", i, g_w2)] + later
        (dz1, dy1, s_m2, s_ln1), swapped = mm_down_comb(
            dh, *W["w1"], dz2, sv["x_mid"], 1.0 + sc2, f"ff_dx_{i}",
            ln=(sv["x_in"], sv["y1"], 1.0 + g1, row(small["ln_g"][i, 0])), side=comm.swap(items))
        pending = comm.reduce_begin(i, items, swapped)
        side = comm.exchange(pending)
        mixer = "gla" if i % 2 == 0 else "att"
        below = None
        if i > 0:
            below = (saved[i - 1]["x_mid"], saved[i - 1]["y2"], 1.0 + row(mods[i - 1, 5]), row(small["ln_g"][i - 1, 1]))
        if i % 2 == 0:
            g_out = mm_w(sv["zmix"], dy1, f"gla_wog_{i}").reshape(N_CHIPS, D // N_CHIPS, D)
            dzg = mm_plain(dy1, *W["out"], f"gla_dz_{i}", nt=True)
            (dproj, dwgk, dbgk, dgn), landed = gla_bwd(sv["proj"], sv["states"], dzg, wgk_p[j],
                                                       row(small["gla_b_gk"][j]), row(small["gla_g_norm"][j]),
                                                       f"gla_bwd_{i}", side)
            g_small["gla_w_gk2"][j] = dwgk[:GLA_RANK]
            g_small["gla_b_gk"][j] = dbgk[0]
            g_small["gla_g_norm"][j] = dgn[0].reshape(GLA_HEADS, GLA_DV_HEAD)
            gwi = mm_w(sv["u1"], dproj, f"gla_wig_{i}")[:, :GLA_IN]
            g_in = gwi.reshape(D, N_CHIPS, GLA_IN // N_CHIPS).transpose(1, 0, 2)
            outs, _ = mm_down_comb(dproj, *W["in"], dz1, sv["x_in"], 1.0 + sc1, f"mix_dx_{i}", ln=below)
        else:
            g_out = mm_w(sv["zmix"], dy1, f"att_wog_{i}").reshape(N_CHIPS, D // N_CHIPS, D)
            do = mm_plain(dy1, *W["out"], f"att_do_{i}", mode="bf16", nt=True)
            (dqkv, dbt), landed = attn_bwd(sv["proj"], tiles[j], do, f"att_bwd_{i}", side)
            g_small["att_rel_bias"][j] = att_bias_grad(dbt.reshape(ATT_HEADS * ATT_CLASSES, ATT_TQ, ATT_KW),
                                                       f"att_bias_{i}")
            g_in = mm_w_chips3(sv["u1"], dqkv, f"att_wig_{i}")
            outs, _ = mm_down_comb(dqkv, *W["in"], dz1, sv["x_in"], 1.0 + sc1, f"mix_dx_{i}", parts=3, ln=below)
        s_m1 = outs[1] if below is None else outs[2]
        if i % 2 == 1:
            g_small["att_b_in"][j] = s_m1[2:5].reshape(3 * D)
        comm.reduce_mid(pending, landed)
        later = [(mixer + "_w_in", j, g_in), (mixer + "_w_out", j, g_out)]
        g_small["ln_g"][i] = jnp.stack([s_ln1[0], s_ln2[0]])
        g_small["ln_b"][i] = jnp.stack([s_ln1[1], s_ln2[1]])
        dmods[i] = jnp.stack([s_m1[1], s_m1[0], s_ln1[2], s_m2[1], s_m2[0], s_ln2[2]])
        if below is None:
            dx = outs[0]
        else:
            dz2, dy2, s_ln2 = outs[0], outs[1], outs[3]
    comm.reduce_tail("last", later)

    g_small = {n: jnp.stack(v) for n, v in g_small.items()}
    return sq, dx, jnp.stack(dmods), g_small


SMALL_SHARDED = (("ln_g", (4, 2, 256)), ("ln_b", (4, 2, 256)), ("gla_g_norm", (2, 4, 64)),
                 ("gla_w_gk2", (2, 16, 128)), ("att_b_in", (2, 768)))
SMALL_FULL = dict(ln_g=(4, 2, 1024), ln_b=(4, 2, 1024), gla_g_norm=(2, 4, 256), gla_w_gk2=(2, 16, 512),
                  att_b_in=(2, 3072), gla_b_gk=(2, 512), att_rel_bias=(2, 16, 257))
SMALL_GRAD_ORDER = ("ln_g", "ln_b", "gla_g_norm", "gla_w_gk2", "att_b_in", "gla_b_gk", "att_rel_bias")


def _pack_small(arrs, rows_total):
    parts = []
    for a in arrs:
        flat = a.reshape(-1)
        pad = (-flat.shape[0]) % PACK_W
        parts.append(jnp.pad(flat, (0, pad)).reshape(-1, PACK_W))
    buf = jnp.concatenate(parts, axis=0)
    return jnp.pad(buf, ((0, rows_total - buf.shape[0]), (0, 0)))


def _unpack_small(buf, shapes):
    out, r = [], 0
    for shp in shapes:
        n = 1
        for s in shp:
            n *= s
        nr = (n + PACK_W - 1) // PACK_W
        out.append(buf[..., r:r + nr, :].reshape(buf.shape[:-2] + (nr * PACK_W,))[..., :n].reshape(buf.shape[:-2] + shp))
        r += nr
    return out


def _unshard_last(g4):
    nd = g4.ndim
    perm = tuple(range(1, nd - 1)) + (0, nd - 1)
    t = g4.transpose(perm)
    return t.reshape(t.shape[:-2] + (-1,))


def _shard_last(full, s):
    n = full.shape[-1] // N_CHIPS
    return lax.dynamic_slice_in_dim(full, s * n, n, axis=full.ndim - 1)


WEIGHT_NAMES = ("w_ada", "b_ada", "ln_g", "ln_b", "gla_w_in", "gla_w_gk2", "gla_b_gk", "gla_g_norm", "gla_w_out",
                "att_w_in", "att_b_in", "att_rel_bias", "att_w_out", "ff_w1", "ff_w2")


def kernel(x, c, w_ada, b_ada, ln_g, ln_b, gla_w_in, gla_w_gk2, gla_b_gk, gla_g_norm, gla_w_out, att_w_in, att_b_in, att_rel_bias, att_w_out, ff_w1, ff_w2, loss_target, m_w_ada, m_b_ada, m_ln_g, m_ln_b, m_gla_w_in, m_gla_w_gk2, m_gla_b_gk, m_gla_g_norm, m_gla_w_out, m_att_w_in, m_att_b_in, m_att_rel_bias, m_att_w_out, m_ff_w1, m_ff_w2, v_w_ada, v_b_ada, v_ln_g, v_ln_b, v_gla_w_in, v_gla_w_gk2, v_gla_b_gk, v_gla_g_norm, v_gla_w_out, v_att_w_in, v_att_b_in, v_att_rel_bias, v_att_w_out, v_ff_w1, v_ff_w2):
    weights = dict(w_ada=w_ada, b_ada=b_ada, ln_g=ln_g, ln_b=ln_b, gla_w_in=gla_w_in, gla_w_gk2=gla_w_gk2,
                   gla_b_gk=gla_b_gk, gla_g_norm=gla_g_norm, gla_w_out=gla_w_out, att_w_in=att_w_in,
                   att_b_in=att_b_in, att_rel_bias=att_rel_bias, att_w_out=att_w_out, ff_w1=ff_w1, ff_w2=ff_w2)
    mom1 = dict(w_ada=m_w_ada, b_ada=m_b_ada, ln_g=m_ln_g, ln_b=m_ln_b, gla_w_in=m_gla_w_in, gla_w_gk2=m_gla_w_gk2,
                gla_b_gk=m_gla_b_gk, gla_g_norm=m_gla_g_norm, gla_w_out=m_gla_w_out, att_w_in=m_att_w_in,
                att_b_in=m_att_b_in, att_rel_bias=m_att_rel_bias, att_w_out=m_att_w_out, ff_w1=m_ff_w1, ff_w2=m_ff_w2)
    mom2 = dict(w_ada=v_w_ada, b_ada=v_b_ada, ln_g=v_ln_g, ln_b=v_ln_b, gla_w_in=v_gla_w_in, gla_w_gk2=v_gla_w_gk2,
                gla_b_gk=v_gla_b_gk, gla_g_norm=v_gla_g_norm, gla_w_out=v_gla_w_out, att_w_in=v_att_w_in,
                att_b_in=v_att_b_in, att_rel_bias=v_att_rel_bias, att_w_out=v_att_w_out, ff_w1=v_ff_w1, ff_w2=v_ff_w2)

    ax, ay, ac = lax.axis_index("x"), lax.axis_index("y"), lax.axis_index("c")
    chip = 2 * ax + ay
    dev = 2 * chip + ac
    S = x.shape[1]
    x2 = x.reshape(S, D_MODEL)
    t2 = loss_target.reshape(S, D_MODEL)

    comm = Comm(weights, ac, chip)

    small_rows = 16
    spack = _pack_small([c] + [weights[n] for n, _ in SMALL_SHARDED], small_rows)
    sg = all_gather8(spack, "gather_small").reshape(N_DEV, small_rows, PACK_W)
    parts = _unpack_small(sg, [(1, D_MODEL)] + [shp for _, shp in SMALL_SHARDED])
    c_all = parts[0].reshape(N_DEV, D_MODEL)
    small = {n: _unshard_last(p[0::2]) for (n, _), p in zip(SMALL_SHARDED, parts[1:])}
    small["gla_b_gk"] = gla_b_gk
    small["att_rel_bias"] = att_rel_bias

    c_act = silu_rows(jnp.pad(c_all, ((0, 128 - N_DEV), (0, 0))), "silu_c")
    wa = w_ada.astype(BF16).transpose(1, 0, 2).reshape(1, D_MODEL, DEPTH * 6 * D_MODEL // N_CHIPS)
    mods_part = mm_plain(c_act, wa, 0, "ada_fwd", tm=128)[:N_DEV]
    mg = all_gather8(mods_part, "gather_mods").reshape(N_CHIPS, 2, N_DEV, DEPTH, 6 * D_MODEL // N_CHIPS)
    mods_mine = lax.dynamic_index_in_dim(mg[:, 0], dev, axis=1, keepdims=False)
    mods = mods_mine.transpose(1, 0, 2).reshape(DEPTH, 6 * D_MODEL) + b_ada
    mods = mods.reshape(DEPTH, 6, D_MODEL)

    sq, grad_x, dmods, g_small = local_step(x2, t2, mods, comm, small)
    loss = lax.psum(0.5 * sq[0, 0] / D_MODEL, ("x", "y", "c"))

    g_shard = comm.reduce_end()

    dm_flat = dmods.reshape(DEPTH, 6 * D_MODEL)
    g_rows = 80
    gpack = _pack_small([dm_flat] + [g_small[n] for n in SMALL_GRAD_ORDER], g_rows)
    gg = all_gather8(gpack, "gather_small_grads").reshape(N_DEV, g_rows, PACK_W)
    gsum = sum_over_devices(gg, "sum_small_grads")
    sums = _unpack_small(gsum, [(DEPTH, 6 * D_MODEL)] + [SMALL_FULL[n] for n in SMALL_GRAD_ORDER])
    grads = dict(b_ada=sums[0])
    for n, full_g in zip(SMALL_GRAD_ORDER, sums[1:]):
        grads[n] = full_g if n in ("gla_b_gk", "att_rel_bias") else _shard_last(full_g, chip)
    dm_all = _unpack_small(gg, [(DEPTH, 6 * D_MODEL)])[0]
    dm_cols = _shard_last(dm_all, chip).reshape(N_DEV, DEPTH * 6 * D_MODEL // N_CHIPS)
    dm_cols = jnp.pad(dm_cols, ((0, 128 - N_DEV), (0, 0))).astype(BF16)
    gwa = mm_w(c_act, dm_cols, "ada_bwd", ts=128)
    grads["w_ada"] = gwa.reshape(D_MODEL, DEPTH, 6 * D_MODEL // N_CHIPS).transpose(1, 0, 2)
    grads.update(g_shard)

    deltas, new_m, new_v = {}, {}, {}
    for n in WEIGHT_NAMES:
        deltas[n], new_m[n], new_v[n] = adamw(weights[n], grads[n], mom1[n], mom2[n], "adamw_" + n)

    return (loss, grad_x.reshape(1, S, D_MODEL), *[grads[n] for n in WEIGHT_NAMES], *[deltas[n] for n in WEIGHT_NAMES],
            *[new_m[n] for n in WEIGHT_NAMES], *[new_v[n] for n in WEIGHT_NAMES])
```

```python
import functools

import jax
import jax.numpy as jnp
from jax import lax
from jax.experimental import pallas as pl
from jax.experimental.pallas import tpu as pltpu

F32 = jnp.float32
BF16 = jnp.bfloat16
HIGHEST = lax.Precision.HIGHEST
MESH = pl.DeviceIdType.MESH

D_MODEL = 1024
DEPTH = 4
CHUNK = 64
GLA_HEADS = 4
GLA_DK = 512
GLA_DV = 1024
GLA_DK_HEAD = 128
GLA_DV_HEAD = 256
GLA_RANK = 16
GLA_IN = 3088
GLA_IN_PAD = 3200
GLA_LR_OFF = 3072
ATT_HEADS = 16
ATT_HD = 64
LEFT_CHUNKS = 8
MAX_REL = 128
N_REL = 257
D_FF = 4096
ALPHA = (2.0 * DEPTH) ** 0.25
LN_EPS = 1e-5
RMS_EPS = 1e-6
NEG_INF = -1e30
GLA_SCALE = GLA_DK_HEAD ** -0.5
ATT_SCALE = ATT_HD ** -0.5
ADAM_LR = 0.001
ADAM_B1 = 0.9
ADAM_B2 = 0.999
ADAM_EPS = 1e-08
ADAM_WD = 0.01
ADAM_STEP = 10

ATT_TQ = 256
ATT_KW = 768
GLA_TB = 256
GLA_GROUP = 2
VMEM_LIMIT = 56 * 1024 * 1024
WHOLE_WEIGHT_BYTES = 8 * 1024 * 1024
N_CHIPS = 4
N_DEV = 8
PACK_W = 1024


def _dot(a, b):
    return jnp.dot(a, b, preferred_element_type=F32)


def _dot_nt(a, b):
    return lax.dot_general(a, b, (((1,), (1,)), ((), ())), preferred_element_type=F32)


def _dot_tn(a, b):
    return lax.dot_general(a, b, (((0,), (0,)), ((), ())), preferred_element_type=F32)


def _cp(sem, vmem=VMEM_LIMIT):
    return pltpu.CompilerParams(dimension_semantics=sem, vmem_limit_bytes=vmem)


def _row_spec(n):
    return pl.BlockSpec((1, n), lambda *_: (0, 0))


def _sigmoid(x):
    return 1.0 / (1.0 + jnp.exp(-x))


def _log_sigmoid(x):
    return jnp.minimum(x, 0.0) - jnp.log1p(jnp.exp(-jnp.abs(x)))


class Side:
    def __init__(self, ins, out_shapes, n_sems, n_local, start, wait):
        self.ins, self.out_shapes, self.n_sems, self.n_local = list(ins), list(out_shapes), n_sems, n_local
        self.start, self.wait = start, wait

    def sem_shapes(self):
        return [pltpu.SemaphoreType.DMA((self.n_sems,)), pltpu.SemaphoreType.DMA((self.n_sems,)),
                pltpu.SemaphoreType.DMA((max(self.n_local, 1),))]


def run_side(side, name):
    n_in = len(side.ins)
    n_out = len(side.out_shapes)

    def body(*refs):
        ins, outs, sems = refs[:n_in], refs[n_in:n_in + n_out], refs[n_in + n_out:]
        side.start(ins, outs, *sems)
        side.wait(ins, outs, *sems)

    any_spec = pl.BlockSpec(memory_space=pl.ANY)
    return pl.pallas_call(body, name=name, out_shape=side.out_shapes, in_specs=[any_spec] * n_in,
                          out_specs=[any_spec] * n_out, scratch_shapes=side.sem_shapes())(*side.ins)


def hosted_call(main, side, *, name, grid, in_specs, out_specs, out_shape, scratch_shapes, dims, args):
    if side is None:
        outs = pl.pallas_call(main, name=name, grid=grid, in_specs=in_specs, out_specs=out_specs,
                              out_shape=out_shape, scratch_shapes=scratch_shapes, compiler_params=_cp(dims))(*args)
        return list(outs), []
    n_mi, n_mo, n_ms = len(in_specs), len(out_specs), len(scratch_shapes)
    n_si, n_so = len(side.ins), len(side.out_shapes)

    def kern(*refs):
        mi, si = refs[:n_mi], refs[n_mi:n_mi + n_si]
        o0 = n_mi + n_si
        mo, so = refs[o0:o0 + n_mo], refs[o0 + n_mo:o0 + n_mo + n_so]
        s0 = o0 + n_mo + n_so
        ms, sems = refs[s0:s0 + n_ms], refs[s0 + n_ms:]
        ids = [pl.program_id(d) for d in range(len(grid))]
        first = functools.reduce(jnp.logical_and, [i == 0 for i in ids])
        last = functools.reduce(jnp.logical_and, [i == g - 1 for i, g in zip(ids, grid)])

        @pl.when(first)
        def _():
            side.start(si, so, *sems)
        main(*mi, *mo, *ms)

        @pl.when(last)
        def _():
            side.wait(si, so, *sems)

    any_spec = pl.BlockSpec(memory_space=pl.ANY)
    outs = pl.pallas_call(
        kern, name=name, grid=grid, in_specs=list(in_specs) + [any_spec] * n_si,
        out_specs=list(out_specs) + [any_spec] * n_so, out_shape=list(out_shape) + side.out_shapes,
        scratch_shapes=list(scratch_shapes) + side.sem_shapes(),
        compiler_params=_cp(("arbitrary",) * len(grid)))(*args, *side.ins)
    return list(outs[:n_mo]), list(outs[n_mo:])


def modulate(x, sc, sh, name):
    S, D = x.shape
    tm = min(512, S)

    def kern(x_ref, sc_ref, sh_ref, u_ref):
        u_ref[...] = (x_ref[...] * (1.0 + sc_ref[...]) + sh_ref[...]).astype(BF16)

    return pl.pallas_call(
        kern, name=name, grid=(S // tm,),
        in_specs=[pl.BlockSpec((tm, D), lambda i: (i, 0)), _row_spec(D), _row_spec(D)],
        out_specs=pl.BlockSpec((tm, D), lambda i: (i, 0)),
        out_shape=jax.ShapeDtypeStruct((S, D), BF16),
        compiler_params=_cp(("parallel",)),
    )(x, sc, sh)


def loss_head(x, t, name):
    S, D = x.shape
    tm = min(512, S)

    def kern(x_ref, t_ref, dx_ref, l_ref):
        @pl.when(pl.program_id(0) == 0)
        def _():
            l_ref[...] = jnp.zeros_like(l_ref)
        e = x_ref[...] - t_ref[...]
        dx_ref[...] = e * (1.0 / D)
        l_ref[...] += jnp.sum(e * e)

    return pl.pallas_call(
        kern, name=name, grid=(S // tm,),
        in_specs=[pl.BlockSpec((tm, D), lambda i: (i, 0)), pl.BlockSpec((tm, D), lambda i: (i, 0))],
        out_specs=[pl.BlockSpec((tm, D), lambda i: (i, 0)), pl.BlockSpec((8, 128), lambda i: (0, 0))],
        out_shape=[jax.ShapeDtypeStruct((S, D), F32), jax.ShapeDtypeStruct((8, 128), F32)],
        compiler_params=_cp(("arbitrary",)),
    )(x, t)


def silu_rows(c_all, name):
    def kern(c_ref, o_ref):
        c = c_ref[...]
        o_ref[...] = (c * _sigmoid(c)).astype(BF16)

    return pl.pallas_call(kern, name=name, out_shape=jax.ShapeDtypeStruct(c_all.shape, BF16))(c_all)


def sum_over_devices(g, name):
    n, R, C = g.shape

    def kern(g_ref, o_ref):
        acc = g_ref[0]
        for d in range(1, n):
            acc = acc + g_ref[d]
        o_ref[...] = acc

    return pl.pallas_call(kern, name=name, out_shape=jax.ShapeDtypeStruct((R, C), F32))(g)


def _rows_block(R, C, budget=1 << 20):
    if R * C * 4 <= budget or R % 8:
        return R
    tr = max(8, (budget // (C * 4)) // 8 * 8)
    while R % tr:
        tr -= 8
    return tr


def adamw(w, g, m, v, name):
    shape = w.shape
    C = shape[-1]
    R = w.size // C
    w2, g2, m2, v2 = (t.reshape(R, C) for t in (w, g, m, v))
    tr = _rows_block(R, C)
    c1 = 1.0 - ADAM_B1 ** ADAM_STEP
    c2 = 1.0 - ADAM_B2 ** ADAM_STEP

    def kern(w_ref, g_ref, m_ref, v_ref, d_ref, nm_ref, nv_ref):
        gg = g_ref[...]
        nm = ADAM_B1 * m_ref[...] + (1.0 - ADAM_B1) * gg
        nv = ADAM_B2 * v_ref[...] + (1.0 - ADAM_B2) * (gg * gg)
        m_hat = nm / c1
        v_hat = nv / c2
        d_ref[...] = -ADAM_LR * (m_hat / (jnp.sqrt(v_hat) + ADAM_EPS) + ADAM_WD * w_ref[...])
        nm_ref[...] = nm
        nv_ref[...] = nv

    spec = pl.BlockSpec((tr, C), lambda i: (i, 0))
    outs = pl.pallas_call(
        kern, name=name, grid=(R // tr,),
        in_specs=[spec] * 4, out_specs=[spec] * 3,
        out_shape=[jax.ShapeDtypeStruct((R, C), F32)] * 3,
        compiler_params=_cp(("parallel",)),
    )(w2, g2, m2, v2)
    return tuple(o.reshape(shape) for o in outs)


def _tn_for(N):
    for tn in (1024, 768, 640, 512, 384, 256, 128):
        if N % tn == 0:
            return tn
    return N


def mm_plain(a, b3, layer, name, *, mode="f32", nt=False, bias=None, h=None, tm=1024, side=None):
    M, K = a.shape
    N = b3.shape[1] if nt else b3.shape[2]
    if K * N * 2 <= WHOLE_WEIGHT_BYTES:
        tn, tm = N, min(tm, 512)
    else:
        tn = _tn_for(N)
    tm = min(tm, M)
    a_spec = pl.BlockSpec((tm, K), lambda j, i: (i, 0))
    if nt:
        b_spec = pl.BlockSpec((None, tn, K), lambda j, i: (layer, j, 0))
    else:
        b_spec = pl.BlockSpec((None, K, tn), lambda j, i: (layer, 0, j))
    o_spec = pl.BlockSpec((tm, tn), lambda j, i: (i, j))
    ins, in_specs = [a, b3], [a_spec, b_spec]
    if bias is not None:
        ins.append(bias)
        in_specs.append(pl.BlockSpec((1, tn), lambda j, i: (0, j)))
    if mode == "mlp_dn":
        ins.append(h)
        in_specs.append(o_spec)
    elif mode not in ("f32", "bf16", "mlp_up"):
        raise ValueError(mode)
    odt = F32 if mode == "f32" else BF16

    def kern(a_ref, b_ref, *rest):
        rest = list(rest)
        bias_ref = rest.pop(0) if bias is not None else None
        h_ref = rest.pop(0) if mode == "mlp_dn" else None
        o_ref = rest.pop(0)
        if nt:
            bt_ref = rest.pop(0)

            @pl.when(pl.program_id(1) == 0)
            def _():
                bt_ref[...] = b_ref[...].T
            acc = _dot(a_ref[...], bt_ref[...])
        else:
            acc = _dot(a_ref[...], b_ref[...])
        if bias_ref is not None:
            acc = acc + bias_ref[...]
        if mode == "mlp_up":
            r = jnp.maximum(acc, 0.0)
            acc = r * r
        elif mode == "mlp_dn":
            acc = acc * (2.0 * jnp.sqrt(h_ref[...].astype(F32)))
        o_ref[...] = acc.astype(odt)

    outs, landed = hosted_call(
        kern, side, name=name, grid=(N // tn, M // tm), in_specs=in_specs, out_specs=[o_spec],
        out_shape=[jax.ShapeDtypeStruct((M, N), odt)],
        scratch_shapes=[pltpu.VMEM((K, tn), BF16)] if nt else [], dims=("parallel", "arbitrary"), args=tuple(ins))
    return outs[0] if side is None else (outs[0], landed)


def mm_down_ln(a, b3, layer, x_in, gate1p, ln_g, ln_b, sc_next, sh_next, name, *, side=None, tm=256):
    M, K = a.shape
    D = b3.shape[2]
    tm = min(tm, M)

    def kern(a_ref, b_ref, x_ref, gp_ref, lg_ref, lb_ref, sc_ref, sh_ref, y_ref, xo_ref, u_ref):
        y = _dot(a_ref[...], b_ref[...])
        y_ref[...] = y
        z = ALPHA * x_ref[...] + gp_ref[...] * y
        mu = jnp.mean(z, axis=-1, keepdims=True)
        zc = z - mu
        var = jnp.mean(zc * zc, axis=-1, keepdims=True)
        xo = (zc * lax.rsqrt(var + LN_EPS)) * lg_ref[...] + lb_ref[...]
        xo_ref[...] = xo
        u_ref[...] = (xo * (1.0 + sc_ref[...]) + sh_ref[...]).astype(BF16)

    tile = pl.BlockSpec((tm, D), lambda i: (i, 0))
    outs, landed = hosted_call(
        kern, side, name=name, grid=(M // tm,),
        in_specs=[pl.BlockSpec((tm, K), lambda i: (i, 0)), pl.BlockSpec((None, K, D), lambda i: (layer, 0, 0)), tile]
        + [_row_spec(D)] * 5,
        out_specs=[tile, tile, tile],
        out_shape=[jax.ShapeDtypeStruct((M, D), F32)] * 2 + [jax.ShapeDtypeStruct((M, D), BF16)],
        scratch_shapes=[], dims=("parallel",), args=(a, b3, x_in, gate1p, ln_g, ln_b, sc_next, sh_next))
    return tuple(outs), landed


def mm_down_comb(a, b3, layer, dz, x_in, sc1p, name, *, parts=1, ln=None, side=None, tm=256):
    D, K = b3.shape[1], b3.shape[2]
    M = a.shape[-2]
    kp = K // parts
    tm = min(tm, M)
    n_ln = 0 if ln is None else 4

    def kern(*refs):
        a_refs = refs[:parts]
        b_ref, dz_ref, x_ref, sp_ref = refs[parts:parts + 4]
        ln_refs = refs[parts + 4:parts + 4 + n_ln]
        outs = refs[parts + 4 + n_ln:]

        @pl.when(pl.program_id(0) == 0)
        def _():
            for o in outs:
                if o.shape[0] == 8:
                    o[...] = jnp.zeros_like(o)
        if parts == 1:
            du = _dot_nt(a_refs[0][...], b_ref[...])
        else:
            du = _dot_nt(a_refs[0][...], b_ref[:, 0:kp])
            for p in range(1, parts):
                du = du + _dot_nt(a_refs[p][...], b_ref[:, p * kp:(p + 1) * kp])
        dx = ALPHA * dz_ref[...] + du * sp_ref[...]
        if ln is None:
            dx_ref, s_ref = outs
            dx_ref[...] = dx
        else:
            dzl_ref, dyl_ref, s_ref, sl_ref = outs
            _ln_bwd_tile(dx, *ln_refs, dzl_ref, dyl_ref, sl_ref)
        s_ref[0:1, :] += jnp.sum(du * x_ref[...], axis=0, keepdims=True)
        s_ref[1:2, :] += jnp.sum(du, axis=0, keepdims=True)
        if parts > 1:
            for p in range(parts):
                s_ref[2 + p:3 + p, :] += jnp.sum(a_refs[p][...].astype(F32), axis=0, keepdims=True)

    tile = pl.BlockSpec((tm, D), lambda i: (i, 0))
    sums = pl.BlockSpec((8, D), lambda i: (0, 0))
    if parts == 1:
        a_ins, a_specs = [a], [pl.BlockSpec((tm, K), lambda i: (i, 0))]
    else:
        assert kp == D and parts <= 6
        a_ins = [a] * parts
        a_specs = [pl.BlockSpec((None, tm, kp), functools.partial(lambda i, p: (p, i, 0), p=p)) for p in range(parts)]
    in_specs = a_specs + [pl.BlockSpec((None, D, K), lambda i: (layer, 0, 0)), tile, tile, _row_spec(D)]
    args = a_ins + [b3, dz, x_in, sc1p]
    if ln is None:
        out_specs = [tile, sums]
        out_shape = [jax.ShapeDtypeStruct((M, D), F32), jax.ShapeDtypeStruct((8, D), F32)]
    else:
        in_specs += [tile, tile, _row_spec(D), _row_spec(D)]
        args += list(ln)
        out_specs = [tile, tile, sums, sums]
        out_shape = [jax.ShapeDtypeStruct((M, D), F32), jax.ShapeDtypeStruct((M, D), BF16),
                     jax.ShapeDtypeStruct((8, D), F32), jax.ShapeDtypeStruct((8, D), F32)]
    return hosted_call(kern, side, name=name, grid=(M // tm,), in_specs=in_specs, out_specs=out_specs,
                       out_shape=out_shape, scratch_shapes=[], dims=("arbitrary",), args=tuple(args))


def mm_w(a, b, name, *, ts=2048, tk=512, chips_out=False, b_parts=1, tn=None):
    S, K = a.shape
    npart = b.shape[-1]
    N = npart * b_parts
    ts = min(ts, S)
    tk = min(tk, K)
    n_chip = N // N_CHIPS
    if tn is None:
        tn = _tn_for(n_chip if chips_out else npart)
    assert npart % tn == 0 and (not chips_out or n_chip % tn == 0)

    def kern(a_ref, b_ref, o_ref):
        @pl.when(pl.program_id(2) == 0)
        def _():
            o_ref[...] = jnp.zeros_like(o_ref)
        o_ref[...] += _dot_tn(a_ref[...], b_ref[...])

    if b_parts == 1:
        b_spec = pl.BlockSpec((ts, tn), lambda k, n, s: (s, n))
    else:
        per = npart // tn
        b_spec = pl.BlockSpec((None, ts, tn), lambda k, n, s: (n // per, s, n % per))
    if chips_out:
        per_chip = n_chip // tn
        o_spec = pl.BlockSpec((None, tk, tn), lambda k, n, s: (n // per_chip, k, n % per_chip))
        out_shape = jax.ShapeDtypeStruct((N_CHIPS, K, n_chip), F32)
    else:
        o_spec = pl.BlockSpec((tk, tn), lambda k, n, s: (k, n))
        out_shape = jax.ShapeDtypeStruct((K, N), F32)
    return pl.pallas_call(
        kern, name=name, grid=(K // tk, N // tn, S // ts),
        in_specs=[pl.BlockSpec((ts, tk), lambda k, n, s: (s, k)), b_spec],
        out_specs=o_spec, out_shape=out_shape,
        compiler_params=_cp(("parallel", "parallel", "arbitrary")),
    )(a, b)


def mm_w_chips3(a, b3, name, *, ts=512):
    S, K = a.shape
    P = b3.shape[2]
    n_chip = 3 * P // N_CHIPS
    ts = min(ts, S)
    pieces = []
    for chip in range(N_CHIPS):
        lo, hi = chip * n_chip, (chip + 1) * n_chip
        while lo < hi:
            part = lo // P
            w = min(hi, (part + 1) * P) - lo
            pieces.append((chip, lo - chip * n_chip, part, lo - part * P, w))
            lo += w

    def kern(a_ref, b_ref, o_ref):
        @pl.when(pl.program_id(0) == 0)
        def _():
            o_ref[...] = jnp.zeros_like(o_ref)
        at = a_ref[...].T
        for chip, oc, part, pc, w in pieces:
            o_ref[chip, :, oc:oc + w] += _dot(at, b_ref[part, :, pc:pc + w])

    return pl.pallas_call(
        kern, name=name, grid=(S // ts,),
        in_specs=[pl.BlockSpec((ts, K), lambda s: (s, 0)), pl.BlockSpec((3, ts, P), lambda s: (0, s, 0))],
        out_specs=pl.BlockSpec((N_CHIPS, K, n_chip), lambda s: (0, 0, 0)),
        out_shape=jax.ShapeDtypeStruct((N_CHIPS, K, n_chip), F32),
        compiler_params=_cp(("arbitrary",)),
    )(a, b3)


def mm_f32(a, b, name):
    def kern(a_ref, b_ref, o_ref):
        o_ref[...] = jnp.dot(a_ref[...], b_ref[...], precision=HIGHEST, preferred_element_type=F32)

    return pl.pallas_call(kern, name=name, out_shape=jax.ShapeDtypeStruct((a.shape[0], b.shape[1]), F32),
                          compiler_params=pltpu.CompilerParams(vmem_limit_bytes=VMEM_LIMIT))(a, b)


def _ln_bwd_tile(dxo_t, x_ref, y_ref, gp_ref, lg_ref, dz_ref, dy_ref, s_ref):
    yv = y_ref[...]
    z = ALPHA * x_ref[...] + gp_ref[...] * yv
    mu = jnp.mean(z, axis=-1, keepdims=True)
    zc = z - mu
    var = jnp.mean(zc * zc, axis=-1, keepdims=True)
    rstd = lax.rsqrt(var + LN_EPS)
    xhat = zc * rstd
    dxh = dxo_t * lg_ref[...]
    dz = rstd * (dxh - jnp.mean(dxh, axis=-1, keepdims=True)
                 - xhat * jnp.mean(dxh * xhat, axis=-1, keepdims=True))
    dz_ref[...] = dz
    dy_ref[...] = (gp_ref[...] * dz).astype(BF16)
    s_ref[0:1, :] += jnp.sum(dxo_t * xhat, axis=0, keepdims=True)
    s_ref[1:2, :] += jnp.sum(dxo_t, axis=0, keepdims=True)
    s_ref[2:3, :] += jnp.sum(dz * yv, axis=0, keepdims=True)


def ln_bwd(dxo, x_in, y, gate1p, ln_g, name, *, tm=256):
    S, D = dxo.shape
    tm = min(tm, S)

    def kern(dxo_ref, x_ref, y_ref, gp_ref, lg_ref, dz_ref, dy_ref, s_ref):
        @pl.when(pl.program_id(0) == 0)
        def _():
            s_ref[...] = jnp.zeros_like(s_ref)
        _ln_bwd_tile(dxo_ref[...], x_ref, y_ref, gp_ref, lg_ref, dz_ref, dy_ref, s_ref)

    tile = pl.BlockSpec((tm, D), lambda i: (i, 0))
    return pl.pallas_call(
        kern, name=name, grid=(S // tm,),
        in_specs=[tile, tile, tile, _row_spec(D), _row_spec(D)],
        out_specs=[tile, tile, pl.BlockSpec((8, D), lambda i: (0, 0))],
        out_shape=[jax.ShapeDtypeStruct((S, D), F32), jax.ShapeDtypeStruct((S, D), BF16),
                   jax.ShapeDtypeStruct((8, D), F32)],
        compiler_params=_cp(("arbitrary",)),
    )(dxo, x_in, y, gate1p, ln_g)


def _tri64():
    r = lax.broadcasted_iota(jnp.int32, (CHUNK, CHUNK), 0)
    c = lax.broadcasted_iota(jnp.int32, (CHUNK, CHUNK), 1)
    return r >= c


def _gla_chunk_common(proj_ref, rows, b, h):
    kc = slice(h * GLA_DK_HEAD, (h + 1) * GLA_DK_HEAD)
    bh = b[:, kc]
    ep = jnp.exp(bh)
    en = jnp.exp(-bh)
    bl = bh[CHUNK - 1:CHUNK, :]
    ee = jnp.exp(bl - bh)
    dec = jnp.exp(bl)
    q = proj_ref[rows, h * GLA_DK_HEAD:(h + 1) * GLA_DK_HEAD] * GLA_SCALE
    k = proj_ref[rows, GLA_DK + h * GLA_DK_HEAD:GLA_DK + (h + 1) * GLA_DK_HEAD]
    v = proj_ref[rows, 2 * GLA_DK + h * GLA_DV_HEAD:2 * GLA_DK + (h + 1) * GLA_DV_HEAD]
    g = proj_ref[rows, 2 * GLA_DK + GLA_DV + h * GLA_DV_HEAD:2 * GLA_DK + GLA_DV + (h + 1) * GLA_DV_HEAD]
    return ep, en, ee, dec, q, k, v, g


def gla_fwd(proj, wgk_p, bgk, gnorm, name, side=None):
    S = proj.shape[0]
    TB = min(GLA_TB, S)
    ncb = TB // CHUNK

    def kern(proj_ref, wgk_ref, bgk_ref, gn_ref, zg_ref, st_ref, state_scr, la_scr):
        @pl.when(pl.program_id(0) == 0)
        def _():
            state_scr[...] = jnp.zeros_like(state_scr)
        lr = proj_ref[:, GLA_LR_OFF:GLA_IN_PAD].astype(BF16)
        gk = _dot(lr, wgk_ref[...]) + bgk_ref[...]
        la_scr[...] = _log_sigmoid(gk) * (1.0 / 16.0)
        lower = _tri64()
        tri = lower.astype(F32)

        def group(gi, carry):
            rows = [pl.ds(pl.multiple_of((gi * GLA_GROUP + g) * CHUNK, CHUNK), CHUNK) for g in range(GLA_GROUP)]
            b = [jnp.dot(tri, la_scr[r, :], precision=HIGHEST, preferred_element_type=F32) for r in rows]
            P = [(g, h) for g in range(GLA_GROUP) for h in range(GLA_HEADS)]
            cm = {p: _gla_chunk_common(proj_ref, rows[p[0]], b[p[0]], p[1]) for p in P}
            qf = {p: (cm[p][4] * cm[p][0]).astype(BF16) for p in P}
            kn = {p: (cm[p][5] * cm[p][1]).astype(BF16) for p in P}
            qn = {p: (cm[p][4] * cm[p][1]).astype(BF16) for p in P}
            kp = {p: (cm[p][5] * cm[p][0]).astype(BF16) for p in P}
            ke = {p: (cm[p][5] * cm[p][2]).astype(BF16) for p in P}
            vb = {p: cm[p][6].astype(BF16) for p in P}
            a_f = {p: _dot_nt(qf[p], kn[p]) for p in P}
            a_b = {p: _dot_nt(qn[p], kp[p]) for p in P}
            upd = {p: _dot_tn(vb[p], ke[p]) for p in P}
            st = {(0, h): state_scr[h] for h in range(GLA_HEADS)}
            for g in range(GLA_GROUP):
                for h in range(GLA_HEADS):
                    st[(g + 1, h)] = st[(g, h)] * cm[(g, h)][3] + upd[(g, h)]
            o_st = {p: _dot_nt(qf[p], st[p].astype(BF16)) for p in P}
            amat = {p: jnp.where(lower, a_f[p], a_b[p]).astype(BF16) for p in P}
            o = {p: _dot(amat[p], vb[p]) + o_st[p] for p in P}
            for g, h in P:
                st_ref[gi * GLA_GROUP + g, h] = st[(g, h)]
            for h in range(GLA_HEADS):
                state_scr[h] = st[(GLA_GROUP, h)]
            for g, h in P:
                gate = cm[(g, h)][7]
                vc = slice(h * GLA_DV_HEAD, (h + 1) * GLA_DV_HEAD)
                r = lax.rsqrt(jnp.mean(o[(g, h)] * o[(g, h)], axis=-1, keepdims=True) + RMS_EPS)
                on = (o[(g, h)] * r) * gn_ref[:, vc]
                zg_ref[rows[g], vc] = (on * (gate * _sigmoid(gate))).astype(BF16)
            return carry

        lax.fori_loop(0, ncb // GLA_GROUP, group, 0)

    return hosted_call(
        kern, side, name=name, grid=(S // TB,),
        in_specs=[pl.BlockSpec((TB, GLA_IN_PAD), lambda i: (i, 0)),
                  pl.BlockSpec((128, GLA_DK), lambda i: (0, 0)), _row_spec(GLA_DK), _row_spec(GLA_DV)],
        out_specs=[pl.BlockSpec((TB, GLA_DV), lambda i: (i, 0)),
                   pl.BlockSpec((ncb, GLA_HEADS, GLA_DV_HEAD, GLA_DK_HEAD), lambda i: (i, 0, 0, 0))],
        out_shape=[jax.ShapeDtypeStruct((S, GLA_DV), BF16),
                   jax.ShapeDtypeStruct((S // CHUNK, GLA_HEADS, GLA_DV_HEAD, GLA_DK_HEAD), F32)],
        scratch_shapes=[pltpu.VMEM((GLA_HEADS, GLA_DV_HEAD, GLA_DK_HEAD), F32), pltpu.VMEM((TB, GLA_DK), F32)],
        dims=("arbitrary",), args=(proj, wgk_p, bgk, gnorm))


def gla_bwd(proj, states, dzg, wgk_p, bgk, gnorm, name, side=None):
    S = proj.shape[0]
    TB = min(GLA_TB, S)
    ncb = TB // CHUNK
    nb = S // TB

    def kern(proj_ref, st_ref, dzg_ref, wgk_ref, bgk_ref, gn_ref,
             dproj_ref, dwgk_ref, dbgk_ref, dgn_ref, dstate_scr, la_scr, gk_scr, dgk_scr):
        @pl.when(pl.program_id(0) == 0)
        def _():
            dstate_scr[...] = jnp.zeros_like(dstate_scr)
            dwgk_ref[...] = jnp.zeros_like(dwgk_ref)
            dbgk_ref[...] = jnp.zeros_like(dbgk_ref)
            dgn_ref[...] = jnp.zeros_like(dgn_ref)
        lr = proj_ref[:, GLA_LR_OFF:GLA_IN_PAD].astype(BF16)
        gk = _dot(lr, wgk_ref[...]) + bgk_ref[...]
        gk_scr[...] = gk
        la_scr[...] = _log_sigmoid(gk) * (1.0 / 16.0)
        lower = _tri64()
        tri = lower.astype(F32)
        r_i = lax.broadcasted_iota(jnp.int32, (CHUNK, CHUNK), 0)
        c_i = lax.broadcasted_iota(jnp.int32, (CHUNK, CHUNK), 1)
        triu = (c_i >= r_i).astype(F32)
        last_row = lax.broadcasted_iota(jnp.int32, (CHUNK, GLA_DK_HEAD), 0) == CHUNK - 1

        def group(gi, carry):
            cs = [ncb - 1 - (gi * GLA_GROUP + g) for g in range(GLA_GROUP)]
            rows = [pl.ds(pl.multiple_of(c * CHUNK, CHUNK), CHUNK) for c in cs]
            b = [jnp.dot(tri, la_scr[r, :], precision=HIGHEST, preferred_element_type=F32) for r in rows]
            P = [(g, h) for g in range(GLA_GROUP) for h in range(GLA_HEADS)]
            kcs = [slice(h * GLA_DK_HEAD, (h + 1) * GLA_DK_HEAD) for h in range(GLA_HEADS)]
            vcs = [slice(h * GLA_DV_HEAD, (h + 1) * GLA_DV_HEAD) for h in range(GLA_HEADS)]
            cm = {p: _gla_chunk_common(proj_ref, rows[p[0]], b[p[0]], p[1]) for p in P}
            ep, en, ee, dec = ({p: cm[p][i] for p in P} for i in range(4))
            qf = {p: cm[p][4] * cm[p][0] for p in P}
            kn = {p: cm[p][5] * cm[p][1] for p in P}
            qn = {p: cm[p][4] * cm[p][1] for p in P}
            kp = {p: cm[p][5] * cm[p][0] for p in P}
            ke = {p: cm[p][5] * cm[p][2] for p in P}
            qf_b, kn_b, qn_b, kp_b, ke_b = ({p: t[p].astype(BF16) for p in P} for t in (qf, kn, qn, kp, ke))
            vb = {p: cm[p][6].astype(BF16) for p in P}
            st = {p: st_ref[cs[p[0]], p[1]] for p in P}
            st_b = {p: st[p].astype(BF16) for p in P}
            a_f = {p: _dot_nt(qf_b[p], kn_b[p]) for p in P}
            a_b = {p: _dot_nt(qn_b[p], kp_b[p]) for p in P}
            o_st = {p: _dot_nt(qf_b[p], st_b[p]) for p in P}
            amat = {p: jnp.where(lower, a_f[p], a_b[p]).astype(BF16) for p in P}
            o = {p: _dot(amat[p], vb[p]) + o_st[p] for p in P}
            do_b, dgs = {}, {}
            for p in P:
                g, h = p
                r = lax.rsqrt(jnp.mean(o[p] * o[p], axis=-1, keepdims=True) + RMS_EPS)
                oh = o[p] * r
                gn = gn_ref[:, vcs[h]]
                gate = cm[p][7]
                sg = _sigmoid(gate)
                dz = dzg_ref[rows[g], vcs[h]]
                don = dz * (gate * sg)
                dgs[p] = dz * (oh * gn) * (sg * (1.0 + gate * (1.0 - sg)))
                dgn_ref[:, vcs[h]] += jnp.sum(don * oh, axis=0, keepdims=True)
                doh = don * gn
                do_b[p] = (r * (doh - oh * jnp.mean(doh * oh, axis=-1, keepdims=True))).astype(BF16)
            da = {p: _dot_nt(do_b[p], vb[p]) for p in P}
            dv_a = {p: _dot_tn(amat[p], do_b[p]) for p in P}
            dqf_st = {p: _dot(do_b[p], st_b[p]) for p in P}
            dst_upd = {p: _dot_tn(do_b[p], qf_b[p]) for p in P}
            dst = {(0, h): dstate_scr[h] for h in range(GLA_HEADS)}
            for g in range(GLA_GROUP):
                for h in range(GLA_HEADS):
                    dst[(g + 1, h)] = dst[(g, h)] * dec[(g, h)] + dst_upd[(g, h)]
            for h in range(GLA_HEADS):
                dstate_scr[h] = dst[(GLA_GROUP, h)]
            dst_b = {p: dst[p].astype(BF16) for p in P}
            dv = {p: dv_a[p] + _dot_nt(ke_b[p], dst_b[p]) for p in P}
            dke = {p: _dot(vb[p], dst_b[p]) for p in P}
            da_f = {p: jnp.where(lower, da[p], 0.0).astype(BF16) for p in P}
            da_b = {p: jnp.where(lower, 0.0, da[p]).astype(BF16) for p in P}
            dqf = {p: _dot(da_f[p], kn_b[p]) + dqf_st[p] for p in P}
            dkn = {p: _dot_tn(da_f[p], qf_b[p]) for p in P}
            dqn = {p: _dot(da_b[p], kp_b[p]) for p in P}
            dkp = {p: _dot_tn(da_b[p], qn_b[p]) for p in P}
            dbs = {}
            for p in P:
                ddec = jnp.sum(dst[p] * st[p], axis=0, keepdims=True)
                db = dqf[p] * qf[p] - dkn[p] * kn[p] - dqn[p] * qn[p] + dkp[p] * kp[p] - dke[p] * ke[p]
                dbl = jnp.sum(dke[p] * ke[p], axis=0, keepdims=True) + ddec * dec[p]
                dbs[p] = db + jnp.where(last_row, dbl, 0.0)
            dla = {p: jnp.dot(triu, dbs[p], precision=HIGHEST, preferred_element_type=F32) for p in P}
            for p in P:
                g, h = p
                dq = (dqf[p] * ep[p] + dqn[p] * en[p]) * GLA_SCALE
                dk = dkn[p] * en[p] + dkp[p] * ep[p] + dke[p] * ee[p]
                dgk_scr[rows[g], kcs[h]] = dla[p] * (1.0 / 16.0) * _sigmoid(-gk_scr[rows[g], kcs[h]])
                dproj_ref[rows[g], kcs[h]] = dq.astype(BF16)
                dproj_ref[rows[g], GLA_DK + h * GLA_DK_HEAD:GLA_DK + (h + 1) * GLA_DK_HEAD] = dk.astype(BF16)
                dproj_ref[rows[g], 2 * GLA_DK + h * GLA_DV_HEAD:2 * GLA_DK + (h + 1) * GLA_DV_HEAD] = dv[p].astype(BF16)
                dproj_ref[rows[g], 2 * GLA_DK + GLA_DV + h * GLA_DV_HEAD:
                          2 * GLA_DK + GLA_DV + (h + 1) * GLA_DV_HEAD] = dgs[p].astype(BF16)
            return carry

        lax.fori_loop(0, ncb // GLA_GROUP, group, 0)
        dgk = dgk_scr[...]
        dgk_b = dgk.astype(BF16)
        dproj_ref[:, GLA_LR_OFF:GLA_IN_PAD] = _dot_nt(dgk_b, wgk_ref[...]).astype(BF16)
        dwgk_ref[...] += _dot_tn(lr, dgk_b)
        dbgk_ref[...] += jnp.sum(dgk, axis=0, keepdims=True)

    rev = lambda i: (nb - 1 - i, 0)
    return hosted_call(
        kern, side, name=name, grid=(nb,),
        in_specs=[pl.BlockSpec((TB, GLA_IN_PAD), rev),
                  pl.BlockSpec((ncb, GLA_HEADS, GLA_DV_HEAD, GLA_DK_HEAD), lambda i: (nb - 1 - i, 0, 0, 0)),
                  pl.BlockSpec((TB, GLA_DV), rev),
                  pl.BlockSpec((128, GLA_DK), lambda i: (0, 0)), _row_spec(GLA_DK), _row_spec(GLA_DV)],
        out_specs=[pl.BlockSpec((TB, GLA_IN_PAD), rev),
                   pl.BlockSpec((128, GLA_DK), lambda i: (0, 0)), _row_spec(GLA_DK), _row_spec(GLA_DV)],
        out_shape=[jax.ShapeDtypeStruct((S, GLA_IN_PAD), BF16), jax.ShapeDtypeStruct((128, GLA_DK), F32),
                   jax.ShapeDtypeStruct((1, GLA_DK), F32), jax.ShapeDtypeStruct((1, GLA_DV), F32)],
        scratch_shapes=[pltpu.VMEM((GLA_HEADS, GLA_DV_HEAD, GLA_DK_HEAD), F32), pltpu.VMEM((TB, GLA_DK), F32),
                        pltpu.VMEM((TB, GLA_DK), F32), pltpu.VMEM((TB, GLA_DK), F32)],
        dims=("arbitrary",), args=(proj, states, dzg, wgk_p, bgk, gnorm))


ATT_TW = 1024
ATT_CLASSES = 3


def _att_window(i):
    return pl.multiple_of(jnp.maximum(i * ATT_TQ - LEFT_CHUNKS * CHUNK, 0), ATT_TQ)


def _att_rel_index():
    e = jnp.arange(ATT_TW)[None, :]
    d = jnp.where(e < ATT_KW, e, e - ATT_TW)
    off = (jnp.arange(ATT_CLASSES) * ATT_TQ)[:, None]
    return jnp.clip(off - d, -MAX_REL, MAX_REL) + MAX_REL


def _row_bits():
    return lax.broadcasted_iota(jnp.int32, (ATT_TQ, ATT_TW), 0)


def att_bias_tiles(rel_bias, name):
    pick = (jnp.arange(384)[:, None] == _att_rel_index().reshape(-1)[None, :]).astype(F32)
    tab = mm_f32(jnp.pad(rel_bias, ((0, 0), (0, 384 - N_REL))), pick, name + "_tab")
    tab = tab.reshape(ATT_HEADS * ATT_CLASSES, 1, ATT_TW)

    def kern(t_ref, o_ref):
        cls = pl.program_id(0) % ATT_CLASSES
        x = jnp.broadcast_to(t_ref[...], (ATT_TQ, ATT_TW))
        x = pltpu.roll(x, 0, 1, stride=1, stride_axis=0)
        x = x[:, :ATT_KW]
        qc = cls * (ATT_TQ // CHUNK) + lax.shift_right_arithmetic(
            lax.broadcasted_iota(jnp.int32, (ATT_TQ, ATT_KW), 0), 6)
        kc = lax.shift_right_arithmetic(lax.broadcasted_iota(jnp.int32, (ATT_TQ, ATT_KW), 1), 6)
        o_ref[...] = jnp.where((kc <= qc) & (kc >= qc - LEFT_CHUNKS), x, NEG_INF)

    return pl.pallas_call(
        kern, name=name, grid=(ATT_HEADS * ATT_CLASSES,),
        in_specs=[pl.BlockSpec((None, 1, ATT_TW), lambda i: (i, 0, 0))],
        out_specs=pl.BlockSpec((None, ATT_TQ, ATT_KW), lambda i: (i, 0, 0)),
        out_shape=jax.ShapeDtypeStruct((ATT_HEADS * ATT_CLASSES, ATT_TQ, ATT_KW), F32),
        compiler_params=_cp(("parallel",)),
    )(tab)


def att_bias_grad(dbt, name):
    def kern(d_ref, o_ref):
        x = jnp.concatenate([d_ref[...], jnp.zeros((ATT_TQ, ATT_TW - ATT_KW), F32)], axis=1)
        row = _row_bits()
        for b in range(8):
            x = jnp.where((row & (1 << b)) != 0, pltpu.roll(x, ATT_TW - (1 << b), axis=1), x)
        o_ref[...] = jnp.sum(x, axis=0, keepdims=True)

    diag = pl.pallas_call(
        kern, name=name + "_diag", grid=(ATT_HEADS * ATT_CLASSES,),
        in_specs=[pl.BlockSpec((None, ATT_TQ, ATT_KW), lambda i: (i, 0, 0))],
        out_specs=pl.BlockSpec((None, 1, ATT_TW), lambda i: (i, 0, 0)),
        out_shape=jax.ShapeDtypeStruct((ATT_HEADS * ATT_CLASSES, 1, ATT_TW), F32),
        compiler_params=_cp(("parallel",)),
    )(dbt)
    diag = diag.reshape(ATT_HEADS, ATT_CLASSES * ATT_TW)
    onehot = (_att_rel_index().reshape(-1)[:, None] == jnp.arange(384)[None, :]).astype(F32)
    return mm_f32(diag, onehot, name + "_bins")[:, :N_REL]


def _att_scores(q_ref, kw, bias_ref):
    hs = [slice(hh * ATT_HD, (hh + 1) * ATT_HD) for hh in range(2)]
    q = [q_ref[:, h] * ATT_SCALE for h in hs]
    k = [kw[:, h] for h in hs]
    s = [_dot_nt(q[hh], k[hh]) + bias_ref[hh] for hh in range(2)]
    e = [jnp.exp(t - jnp.max(t, axis=-1, keepdims=True)) for t in s]
    inv = [1.0 / jnp.sum(t, axis=-1, keepdims=True) for t in e]
    return hs, q, k, e, inv


def _att_specs(S):
    nq = D_MODEL // 128
    q_spec = pl.BlockSpec((ATT_TQ, 128), lambda p, i: (i, p))
    k_spec = pl.BlockSpec((S, 128), lambda p, i: (0, nq + p))
    v_spec = pl.BlockSpec((S, 128), lambda p, i: (0, 2 * nq + p))
    b_spec = pl.BlockSpec((2, None, ATT_TQ, ATT_KW), lambda p, i: (p, jnp.minimum(i, ATT_CLASSES - 1), 0, 0))
    return q_spec, k_spec, v_spec, b_spec


def attn_fwd(qkv, bias, name, side=None):
    S = qkv.shape[0]
    q_spec, k_spec, v_spec, b_spec = _att_specs(S)

    def kern(q_ref, k_ref, v_ref, bias_ref, o_ref):
        ws = _att_window(pl.program_id(1))
        kw = k_ref[pl.ds(ws, ATT_KW), :]
        vw = v_ref[pl.ds(ws, ATT_KW), :]
        hs, _, _, e, inv = _att_scores(q_ref, kw, bias_ref)
        outs = [_dot(e[hh].astype(BF16), vw[:, hs[hh]]) * inv[hh] for hh in range(2)]
        o_ref[...] = jnp.concatenate(outs, axis=1).astype(BF16)

    return hosted_call(
        kern, side, name=name, grid=(ATT_HEADS // 2, S // ATT_TQ),
        in_specs=[q_spec, k_spec, v_spec, b_spec],
        out_specs=[pl.BlockSpec((ATT_TQ, 128), lambda p, i: (i, p))],
        out_shape=[jax.ShapeDtypeStruct((S, D_MODEL), BF16)],
        scratch_shapes=[], dims=("parallel", "arbitrary"), args=(qkv, qkv, qkv, bias))


def attn_bwd(qkv, bias, do, name, side=None):
    S = qkv.shape[0]
    nblk = S // ATT_TQ
    q_spec, k_spec, v_spec, b_spec = _att_specs(S)

    def kern(q_ref, k_ref, v_ref, bias_ref, do_ref, dqkv_ref, db_ref, dk_scr, dv_scr):
        i = pl.program_id(1)

        @pl.when(i == 0)
        def _():
            dk_scr[...] = jnp.zeros_like(dk_scr)
            dv_scr[...] = jnp.zeros_like(dv_scr)
            db_ref[...] = jnp.zeros_like(db_ref)
        ws = _att_window(i)
        win = pl.ds(ws, ATT_KW)
        kw = k_ref[win, :]
        vw = v_ref[win, :]
        o_cls = jnp.minimum(i, ATT_CLASSES - 1)
        R2 = range(2)
        hs, q, k, e, inv = _att_scores(q_ref, kw, bias_ref)
        do_h = [do_ref[:, h] for h in hs]
        dp = [_dot_nt(do_h[hh], vw[:, hs[hh]]) for hh in R2]
        p = [e[hh] * inv[hh] for hh in R2]
        dvs = [_dot_tn(p[hh].astype(BF16), do_h[hh]) for hh in R2]
        ds = [p[hh] * (dp[hh] - jnp.sum(p[hh] * dp[hh], axis=-1, keepdims=True)) for hh in R2]
        ds_b = [t.astype(BF16) for t in ds]
        dqs = [_dot(ds_b[hh], k[hh]) * ATT_SCALE for hh in R2]
        dks = [_dot_tn(ds_b[hh], q[hh]) for hh in R2]
        for hh in R2:
            db_ref[hh, o_cls] += ds[hh]
        dqkv_ref[0, pl.ds(pl.multiple_of(i * ATT_TQ, ATT_TQ), ATT_TQ), :] = jnp.concatenate(dqs, axis=1).astype(BF16)
        dk_scr[win, :] += jnp.concatenate(dks, axis=1)
        dv_scr[win, :] += jnp.concatenate(dvs, axis=1)

        @pl.when(i == nblk - 1)
        def _():
            dqkv_ref[1] = dk_scr[...].astype(BF16)
            dqkv_ref[2] = dv_scr[...].astype(BF16)

    return hosted_call(
        kern, side, name=name, grid=(ATT_HEADS // 2, nblk),
        in_specs=[q_spec, k_spec, v_spec, b_spec, pl.BlockSpec((ATT_TQ, 128), lambda p, i: (i, p))],
        out_specs=[pl.BlockSpec((3, S, 128), lambda p, i: (0, 0, p)),
                   pl.BlockSpec((2, ATT_CLASSES, ATT_TQ, ATT_KW), lambda p, i: (p, 0, 0, 0))],
        out_shape=[jax.ShapeDtypeStruct((3, S, D_MODEL), BF16),
                   jax.ShapeDtypeStruct((ATT_HEADS, ATT_CLASSES, ATT_TQ, ATT_KW), F32)],
        scratch_shapes=[pltpu.VMEM((S, 128), F32), pltpu.VMEM((S, 128), F32)],
        dims=("parallel", "arbitrary"), args=(qkv, qkv, qkv, bias, do))


def colsum3(a3, name):
    P, S, N = a3.shape
    tm = min(512, S)

    def kern(a_ref, o_ref):
        @pl.when(pl.program_id(1) == 0)
        def _():
            o_ref[...] = jnp.zeros_like(o_ref)
        o_ref[...] += jnp.sum(a_ref[...].astype(F32), axis=0, keepdims=True)

    return pl.pallas_call(
        kern, name=name, grid=(P, S // tm),
        in_specs=[pl.BlockSpec((None, tm, N), lambda p, i: (p, i, 0))],
        out_specs=pl.BlockSpec((None, 1, N), lambda p, i: (p, 0, 0)),
        out_shape=jax.ShapeDtypeStruct((P, 1, N), F32),
        compiler_params=_cp(("parallel", "arbitrary")),
    )(a3)


def _me():
    return lax.axis_index("x"), lax.axis_index("y"), lax.axis_index("c")


def _other_chips(x, y):
    return [(1 - x, y), (x, 1 - y), (1 - x, 1 - y)]


def all_gather8(x_shard, name):
    m_per, n = x_shard.shape

    def body(x_ref, out_ref, send_sems, recv_sems, local_sem):
        x, y, c = _me()
        me, sibling = (x, y, c), (x, y, 1 - c)
        chips = _other_chips(x, y)

        def rows(px, py, pc):
            return out_ref.at[pl.ds((4 * px + 2 * py + pc) * m_per, m_per), :]

        def copy(k, block, to, src=None):
            return pltpu.make_async_remote_copy(
                src_ref=rows(*block) if src is None else src, dst_ref=rows(*block),
                send_sem=send_sems.at[k], recv_sem=recv_sems.at[k], device_id=to, device_id_type=MESH)

        mine = pltpu.make_async_copy(x_ref, rows(*me), local_sem)
        mine.start()
        first = [copy(0, me, sibling, src=x_ref)]
        first += [copy(1 + j, me, (*chip, c), src=x_ref) for j, chip in enumerate(chips)]
        for cp in first:
            cp.start()
        passed = [copy(4 + j, (*chip, c), sibling) for j, chip in enumerate(chips)]
        for j, chip in enumerate(chips):
            copy(1 + j, (*chip, c), me).wait_recv()
            passed[j].start()
        copy(0, sibling, me).wait_recv()
        for j, chip in enumerate(chips):
            copy(4 + j, (*chip, 1 - c), me).wait_recv()
        for cp in first + passed:
            cp.wait_send()
        mine.wait()

    return pl.pallas_call(
        body, name=name,
        out_shape=jax.ShapeDtypeStruct((N_DEV * m_per, n), x_shard.dtype),
        in_specs=[pl.BlockSpec(memory_space=pltpu.VMEM)],
        out_specs=pl.BlockSpec(memory_space=pltpu.VMEM),
        scratch_shapes=[pltpu.SemaphoreType.DMA((7,)), pltpu.SemaphoreType.DMA((7,)), pltpu.SemaphoreType.DMA],
        compiler_params=pltpu.CompilerParams(vmem_limit_bytes=VMEM_LIMIT),
    )(x_shard)


def _half_rows(n_rows, c):
    h = n_rows // 2
    return pl.ds(c * h, h)


def _gathered_shape(shape, flavour):
    L, a, b = shape
    return {"col": (L, a, N_CHIPS * b), "row": (L, N_CHIPS * a, b), "lead": (N_CHIPS, L, a, b)}[flavour]


def _gathered_part(out_ref, shape, flavour, s, rows):
    L, a, b = shape
    if flavour == "col":
        return out_ref.at[:, rows, pl.ds(s * b, b)]
    if flavour == "row":
        return out_ref.at[:, pl.ds(s * a + rows.start, rows.size), :]
    return out_ref.at[s, :, rows, :]


def gather_side(shards, flavours):
    n = len(shards)
    shapes = [w.shape for w in shards]

    def copies(w_refs, out_refs, send_sems, recv_sems, local_sems):
        x, y, c = _me()
        sibling = (x, y, 1 - c)
        chips = _other_chips(x, y)
        me_s = 2 * x + y

        def copy(k, src, dst, to):
            return pltpu.make_async_remote_copy(src_ref=src, dst_ref=dst, send_sem=send_sems.at[k],
                                                recv_sem=recv_sems.at[k], device_id=to, device_id_type=MESH)

        own, first, landed, passed, passed_in = [], [], [], [], []
        for w in range(n):
            shp, fl = shapes[w], flavours[w]
            my_half = _half_rows(shp[1], c)
            sib_half = _half_rows(shp[1], 1 - c)
            own.append(copy(7 * w + 6, w_refs[w], _gathered_part(out_refs[w], shp, fl, me_s, pl.ds(0, shp[1])), sibling))
            for j, chip in enumerate(chips):
                s = 2 * chip[0] + chip[1]
                first.append(copy(7 * w + j, w_refs[w].at[:, my_half, :],
                                  _gathered_part(out_refs[w], shp, fl, me_s, my_half), (*chip, c)))
                part = _gathered_part(out_refs[w], shp, fl, s, my_half)
                landed.append(copy(7 * w + j, part, part, (*chip, c)))
                passed.append(copy(7 * w + 3 + j, part, part, sibling))
                theirs = _gathered_part(out_refs[w], shp, fl, s, sib_half)
                passed_in.append(copy(7 * w + 3 + j, theirs, theirs, sibling))
        return own, first, landed, passed, passed_in

    def start(*refs):
        own, first, _, _, _ = copies(*refs)
        for cp in first + own:
            cp.start()

    def wait(*refs):
        own, first, landed, passed, passed_in = copies(*refs)
        for arrived, onward in zip(landed, passed):
            arrived.wait_recv()
            onward.start()
        for cp in passed_in:
            cp.wait_recv()
        for cp in own:
            cp.wait()
        for cp in first + passed:
            cp.wait_send()

    out_shapes = [jax.ShapeDtypeStruct(_gathered_shape(s, f), w.dtype) for w, s, f in zip(shards, shapes, flavours)]
    return Side(shards, out_shapes, 7 * n, 0, start, wait)


def swap_side(gs):
    n = len(gs)

    def copies(g_refs, out_refs, send_sems, recv_sems, local_sems):
        x, y, c = _me()
        return [pltpu.make_async_remote_copy(
            src_ref=g_refs[w].at[:, _half_rows(gs[w].shape[1], 1 - c), :], dst_ref=out_refs[w],
            send_sem=send_sems.at[w], recv_sem=recv_sems.at[w], device_id=(x, y, 1 - c), device_id_type=MESH)
            for w in range(n)]

    def start(*refs):
        for cp in copies(*refs):
            cp.start()

    def wait(*refs):
        for cp in copies(*refs):
            cp.wait()

    out_shapes = [jax.ShapeDtypeStruct((g.shape[0], g.shape[1] // 2, g.shape[2]), g.dtype) for g in gs]
    return Side(gs, out_shapes, n, 0, start, wait)


def add_half(g, r1, c_idx, name):
    n, R, C = g.shape
    half = R // 2
    tr = _rows_block(half, C)
    nbh = half // tr

    def kern(c_ref, g_ref, r_ref, o_ref, ob_ref):
        p = g_ref[...] + r_ref[...]
        o_ref[...] = p
        ob_ref[...] = p.astype(BF16)

    spec = pl.BlockSpec((1, tr, C), lambda d, r, c_ref: (d, r, 0))
    return pl.pallas_call(
        kern, name=name,
        grid_spec=pltpu.PrefetchScalarGridSpec(
            num_scalar_prefetch=1, grid=(n, nbh),
            in_specs=[pl.BlockSpec((1, tr, C), lambda d, r, c_ref: (d, c_ref[0] * nbh + r, 0)), spec],
            out_specs=[spec, spec]),
        out_shape=[jax.ShapeDtypeStruct((n, half, C), F32), jax.ShapeDtypeStruct((n, half, C), BF16)],
        compiler_params=_cp(("parallel", "parallel")),
    )(c_idx, g, r1)


def exchange_side(ps):
    n = len(ps)

    def copies(p_refs, out_refs, send_sems, recv_sems, local_sems):
        x, y, c = _me()
        return [pltpu.make_async_remote_copy(
            src_ref=p_refs[w].at[2 * chip[0] + chip[1]], dst_ref=out_refs[w].at[j],
            send_sem=send_sems.at[3 * w + j], recv_sem=recv_sems.at[3 * w + j],
            device_id=(*chip, c), device_id_type=MESH)
            for w in range(n) for j, chip in enumerate(_other_chips(x, y))]

    def start(*refs):
        for cp in copies(*refs):
            cp.start()

    def wait(*refs):
        for cp in copies(*refs):
            cp.wait()

    return Side(ps, [jax.ShapeDtypeStruct((3,) + p.shape[1:], p.dtype) for p in ps], 3 * n, 0, start, wait)


def add_chips(p, r2, chip_idx, name):
    n, H, C = p.shape
    tr = _rows_block(H, C)

    def kern(s_ref, p_ref, r_ref, o_ref):
        o_ref[...] = ((p_ref[0] + r_ref[0].astype(F32)) + r_ref[1].astype(F32)) + r_ref[2].astype(F32)

    return pl.pallas_call(
        kern, name=name,
        grid_spec=pltpu.PrefetchScalarGridSpec(
            num_scalar_prefetch=1, grid=(H // tr,),
            in_specs=[pl.BlockSpec((1, tr, C), lambda r, s_ref: (s_ref[0], r, 0)),
                      pl.BlockSpec((3, tr, C), lambda r, s_ref: (0, r, 0))],
            out_specs=pl.BlockSpec((tr, C), lambda r, s_ref: (r, 0))),
        out_shape=jax.ShapeDtypeStruct((H, C), F32),
        compiler_params=_cp(("parallel",)),
    )(chip_idx, p, r2)


def swap_reduced(ss, name):
    n = len(ss)

    def body(*refs):
        s_refs, out_refs = refs[:n], refs[n:2 * n]
        send_sems, recv_sems = refs[2 * n:]
        x, y, c = _me()
        cps = [pltpu.make_async_remote_copy(src_ref=s_refs[w], dst_ref=out_refs[w], send_sem=send_sems.at[w],
                                            recv_sem=recv_sems.at[w], device_id=(x, y, 1 - c), device_id_type=MESH)
               for w in range(n)]
        for cp in cps:
            cp.start()
        for cp in cps:
            cp.wait()

    any_spec = pl.BlockSpec(memory_space=pl.ANY)
    return pl.pallas_call(
        body, name=name, out_shape=[jax.ShapeDtypeStruct(s.shape, s.dtype) for s in ss],
        in_specs=[any_spec] * n, out_specs=[any_spec] * n,
        scratch_shapes=[pltpu.SemaphoreType.DMA((n,)), pltpu.SemaphoreType.DMA((n,))],
    )(*ss)


BIG = (("gla_w_in", 2, (1024, GLA_IN // N_CHIPS), "lead"), ("gla_w_out", 2, (256, 1024), "row"),
       ("att_w_in", 2, (1024, 768), "col"), ("att_w_out", 2, (256, 1024), "row"),
       ("ff_w1", 4, (1024, 1024), "col"), ("ff_w2", 4, (1024, 1024), "row"))
FLAVOUR = {n: f for n, _, _, f in BIG}


def layer_weights(i):
    mixer = "gla" if i % 2 == 0 else "att"
    return (("in", mixer + "_w_in", i // 2), ("out", mixer + "_w_out", i // 2), ("w1", "ff_w1", i), ("w2", "ff_w2", i))


class Comm:
    def __init__(self, weights, core, chip):
        self.weights, self.core, self.chip = weights, core, chip
        self.c_idx = jnp.reshape(core, (1,)).astype(jnp.int32)
        self.chip_idx = jnp.reshape(chip, (1,)).astype(jnp.int32)
        self.reduced = {}

    def gather(self, items):
        shards = [self.weights[n][l:l + 1].astype(BF16) for _, n, l in items]
        return gather_side(shards, [FLAVOUR[n] for _, n, _ in items])

    def full_weights(self, items, gathered):
        W = {}
        for (role, n, _), w in zip(items, gathered):
            if n == "gla_w_in":
                w = jnp.pad(w.transpose(1, 2, 0, 3).reshape(1, D_MODEL, GLA_IN), ((0, 0), (0, 0), (0, GLA_IN_PAD - GLA_IN)))
            W[role] = (w, 0)
        return W

    def gather_now(self, items, name):
        return self.full_weights(items, run_side(self.gather(items), name))

    def swap(self, items):
        return swap_side([g for _, _, g in items])

    def reduce_begin(self, tag, items, swapped):
        ps = [add_half(g, r, self.c_idx, f"rs_add2_{tag}_{w}") for w, ((_, _, g), r) in enumerate(zip(items, swapped))]
        return tag, [(n, l) for n, l, _ in items], ps

    def exchange(self, pending):
        return exchange_side([pb for _, pb in pending[2]])

    def reduce_mid(self, pending, landed):
        tag, keys, ps = pending
        for w, (key, (p, _), r) in enumerate(zip(keys, ps, landed)):
            self.reduced[key] = add_chips(p, r, self.chip_idx, f"rs_add4_{tag}_{w}")

    def reduce_tail(self, tag, items):
        pending = self.reduce_begin(tag, items, run_side(self.swap(items), f"rs_swap_{tag}"))
        self.reduce_mid(pending, run_side(self.exchange(pending), f"rs_xchg_{tag}"))

    def reduce_end(self):
        keys = [(n, l) for n, L, _, _ in BIG for l in range(L)]
        mine = [self.reduced[k] for k in keys]
        theirs = swap_reduced(mine, "rs_join")
        low = self.core == 0
        full = {k: jnp.concatenate([jnp.where(low, m, t), jnp.where(low, t, m)], axis=0)
                for k, m, t in zip(keys, mine, theirs)}
        return {n: jnp.stack([full[(n, l)] for l in range(L)]) for n, L, _, _ in BIG}


def local_step(x, target, mods, comm, small):
    S, D = x.shape
    row = lambda v: v.reshape(1, -1)
    saved = []
    tiles = [att_bias_tiles(small["att_rel_bias"][j], f"att_tiles_{j}").reshape(ATT_HEADS, ATT_CLASSES, ATT_TQ, ATT_KW)
             for j in range(2)]
    wgk_p = [jnp.pad(small["gla_w_gk2"][j], ((0, 128 - GLA_RANK), (0, 0))).astype(BF16) for j in range(2)]

    u1 = modulate(x, row(mods[0, 1]), row(mods[0, 0]), "mod_first")
    Ws = [dict() for _ in range(DEPTH)]
    items0 = layer_weights(0)
    Ws[0].update(comm.gather_now(items0[:1], "gather_w0"))
    for i in range(DEPTH):
        j = i // 2
        W = Ws[i]
        sh1, sc1, g1, sh2, sc2, g2 = (row(mods[i, k]) for k in range(6))
        nxt = min(i + 1, DEPTH - 1)
        more = i + 1 < DEPTH
        nxt_items = layer_weights(nxt)
        in_items = list(items0[1:3]) if i == 0 else []
        mix_items = (list(items0[3:]) if i == 0 else []) + (list(nxt_items[:2]) if more else [])
        up_items = list(nxt_items[2:3]) if more else []
        down_items = list(nxt_items[3:]) if more else []

        def hosted(items):
            return comm.gather(items) if items else None

        def landed_weights(items, landed):
            for k, it in enumerate(items):
                layer = 0 if it in items0 and i == 0 else nxt
                Ws[layer].update(comm.full_weights([it], landed[k:k + 1]))

        side = hosted(in_items)
        if i % 2 == 0:
            proj = mm_plain(u1, *W["in"], f"gla_in_{i}", side=side)
        else:
            proj = mm_plain(u1, *W["in"], f"att_in_{i}", mode="bf16", bias=row(small["att_b_in"][j]), side=side)
        proj, landed = proj if side is not None else (proj, [])
        landed_weights(in_items, landed)
        if i % 2 == 0:
            (zmix, states), landed = gla_fwd(proj, wgk_p[j], row(small["gla_b_gk"][j]), row(small["gla_g_norm"][j]),
                                             f"gla_fwd_{i}", hosted(mix_items))
        else:
            (zmix,), landed = attn_fwd(proj, tiles[j], f"att_fwd_{i}", hosted(mix_items))
            states = None
        landed_weights(mix_items, landed)
        (y1, x_mid, u2), _ = mm_down_ln(zmix, *W["out"], x, 1.0 + g1, row(small["ln_g"][i, 0]),
                                        row(small["ln_b"][i, 0]), sc2, sh2, f"mix_out_{i}")
        side = hosted(up_items)
        act = mm_plain(u2, *W["w1"], f"ff_up_{i}", mode="mlp_up", side=side)
        act, landed = act if side is not None else (act, [])
        landed_weights(up_items, landed)
        (y2, x_out, u_next), landed = mm_down_ln(act, *W["w2"], x_mid, 1.0 + g2, row(small["ln_g"][i, 1]),
                                                 row(small["ln_b"][i, 1]), row(mods[nxt, 1]), row(mods[nxt, 0]),
                                                 f"ff_out_{i}", side=hosted(down_items))
        landed_weights(down_items, landed)
        saved.append(dict(x_in=x, u1=u1, proj=proj, zmix=zmix, states=states, y1=y1, x_mid=x_mid, u2=u2,
                          act=act, y2=y2))
        x, u1 = x_out, u_next

    dx, sq = loss_head(x, target, "loss_head")

    g_small = dict(ln_g=[None] * DEPTH, ln_b=[None] * DEPTH, gla_w_gk2=[None] * 2, gla_b_gk=[None] * 2,
                   gla_g_norm=[None] * 2, att_b_in=[None] * 2, att_rel_bias=[None] * 2)
    dmods = [None] * DEPTH
    later = []
    top = saved[DEPTH - 1]
    dz2, dy2, s_ln2 = ln_bwd(dx, top["x_mid"], top["y2"], 1.0 + row(mods[DEPTH - 1, 5]),
                             row(small["ln_g"][DEPTH - 1, 1]), "ln2_bwd_top")

    for i in reversed(range(DEPTH)):
        j = i // 2
        sv = saved[i]
        W = Ws[i]
        sh1, sc1, g1, sh2, sc2, g2 = (row(mods[i, k]) for k in range(6))
        dh = mm_plain(dy2, *W["w2"], f"ff_dn_{i}", mode="mlp_dn", nt=True, h=sv["act"])
        g_w2 = mm_w(sv["act"], dy2, f"ff_w2g_{i}").reshape(N_CHIPS, D_FF // N_CHIPS, D)
        g_w1 = mm_w(sv["u2"], dh, f"ff_w1g_{i}", chips_out=True)
        items = [("ff_w1", i, g_w1), ("ff_w2", i, g_w2)] + later
        (dz1, dy1, s_m2, s_ln1), swapped = mm_down_comb(
            dh, *W["w1"], dz2, sv["x_mid"], 1.0 + sc2, f"ff_dx_{i}",
            ln=(sv["x_in"], sv["y1"], 1.0 + g1, row(small["ln_g"][i, 0])), side=comm.swap(items))
        pending = comm.reduce_begin(i, items, swapped)
        side = comm.exchange(pending)
        mixer = "gla" if i % 2 == 0 else "att"
        below = None
        if i > 0:
            below = (saved[i - 1]["x_mid"], saved[i - 1]["y2"], 1.0 + row(mods[i - 1, 5]), row(small["ln_g"][i - 1, 1]))
        if i % 2 == 0:
            g_out = mm_w(sv["zmix"], dy1, f"gla_wog_{i}").reshape(N_CHIPS, D // N_CHIPS, D)
            dzg = mm_plain(dy1, *W["out"], f"gla_dz_{i}", nt=True)
            (dproj, dwgk, dbgk, dgn), landed = gla_bwd(sv["proj"], sv["states"], dzg, wgk_p[j],
                                                       row(small["gla_b_gk"][j]), row(small["gla_g_norm"][j]),
                                                       f"gla_bwd_{i}", side)
            g_small["gla_w_gk2"][j] = dwgk[:GLA_RANK]
            g_small["gla_b_gk"][j] = dbgk[0]
            g_small["gla_g_norm"][j] = dgn[0].reshape(GLA_HEADS, GLA_DV_HEAD)
            gwi = mm_w(sv["u1"], dproj, f"gla_wig_{i}")[:, :GLA_IN]
            g_in = gwi.reshape(D, N_CHIPS, GLA_IN // N_CHIPS).transpose(1, 0, 2)
            outs, _ = mm_down_comb(dproj, *W["in"], dz1, sv["x_in"], 1.0 + sc1, f"mix_dx_{i}", ln=below)
        else:
            g_out = mm_w(sv["zmix"], dy1, f"att_wog_{i}").reshape(N_CHIPS, D // N_CHIPS, D)
            do = mm_plain(dy1, *W["out"], f"att_do_{i}", mode="bf16", nt=True)
            (dqkv, dbt), landed = attn_bwd(sv["proj"], tiles[j], do, f"att_bwd_{i}", side)
            g_small["att_rel_bias"][j] = att_bias_grad(dbt.reshape(ATT_HEADS * ATT_CLASSES, ATT_TQ, ATT_KW),
                                                       f"att_bias_{i}")
            g_in = mm_w_chips3(sv["u1"], dqkv, f"att_wig_{i}")
            outs, _ = mm_down_comb(dqkv, *W["in"], dz1, sv["x_in"], 1.0 + sc1, f"mix_dx_{i}", parts=3, ln=below)
        s_m1 = outs[1] if below is None else outs[2]
        if i % 2 == 1:
            g_small["att_b_in"][j] = s_m1[2:5].reshape(3 * D)
        comm.reduce_mid(pending, landed)
        later = [(mixer + "_w_in", j, g_in), (mixer + "_w_out", j, g_out)]
        g_small["ln_g"][i] = jnp.stack([s_ln1[0], s_ln2[0]])
        g_small["ln_b"][i] = jnp.stack([s_ln1[1], s_ln2[1]])
        dmods[i] = jnp.stack([s_m1[1], s_m1[0], s_ln1[2], s_m2[1], s_m2[0], s_ln2[2]])
        if below is None:
            dx = outs[0]
        else:
            dz2, dy2, s_ln2 = outs[0], outs[1], outs[3]
    comm.reduce_tail("last", later)

    g_small = {n: jnp.stack(v) for n, v in g_small.items()}
    return sq, dx, jnp.stack(dmods), g_small


SMALL_SHARDED = (("ln_g", (4, 2, 256)), ("ln_b", (4, 2, 256)), ("gla_g_norm", (2, 4, 64)),
                 ("gla_w_gk2", (2, 16, 128)), ("att_b_in", (2, 768)))
SMALL_FULL = dict(ln_g=(4, 2, 1024), ln_b=(4, 2, 1024), gla_g_norm=(2, 4, 256), gla_w_gk2=(2, 16, 512),
                  att_b_in=(2, 3072), gla_b_gk=(2, 512), att_rel_bias=(2, 16, 257))
SMALL_GRAD_ORDER = ("ln_g", "ln_b", "gla_g_norm", "gla_w_gk2", "att_b_in", "gla_b_gk", "att_rel_bias")


def _pack_small(arrs, rows_total):
    parts = []
    for a in arrs:
        flat = a.reshape(-1)
        pad = (-flat.shape[0]) % PACK_W
        parts.append(jnp.pad(flat, (0, pad)).reshape(-1, PACK_W))
    buf = jnp.concatenate(parts, axis=0)
    return jnp.pad(buf, ((0, rows_total - buf.shape[0]), (0, 0)))


def _unpack_small(buf, shapes):
    out, r = [], 0
    for shp in shapes:
        n = 1
        for s in shp:
            n *= s
        nr = (n + PACK_W - 1) // PACK_W
        out.append(buf[..., r:r + nr, :].reshape(buf.shape[:-2] + (nr * PACK_W,))[..., :n].reshape(buf.shape[:-2] + shp))
        r += nr
    return out


def _unshard_last(g4):
    nd = g4.ndim
    perm = tuple(range(1, nd - 1)) + (0, nd - 1)
    t = g4.transpose(perm)
    return t.reshape(t.shape[:-2] + (-1,))


def _shard_last(full, s):
    n = full.shape[-1] // N_CHIPS
    return lax.dynamic_slice_in_dim(full, s * n, n, axis=full.ndim - 1)


WEIGHT_NAMES = ("w_ada", "b_ada", "ln_g", "ln_b", "gla_w_in", "gla_w_gk2", "gla_b_gk", "gla_g_norm", "gla_w_out",
                "att_w_in", "att_b_in", "att_rel_bias", "att_w_out", "ff_w1", "ff_w2")


def kernel(x, c, w_ada, b_ada, ln_g, ln_b, gla_w_in, gla_w_gk2, gla_b_gk, gla_g_norm, gla_w_out, att_w_in, att_b_in, att_rel_bias, att_w_out, ff_w1, ff_w2, loss_target, m_w_ada, m_b_ada, m_ln_g, m_ln_b, m_gla_w_in, m_gla_w_gk2, m_gla_b_gk, m_gla_g_norm, m_gla_w_out, m_att_w_in, m_att_b_in, m_att_rel_bias, m_att_w_out, m_ff_w1, m_ff_w2, v_w_ada, v_b_ada, v_ln_g, v_ln_b, v_gla_w_in, v_gla_w_gk2, v_gla_b_gk, v_gla_g_norm, v_gla_w_out, v_att_w_in, v_att_b_in, v_att_rel_bias, v_att_w_out, v_ff_w1, v_ff_w2):
    weights = dict(w_ada=w_ada, b_ada=b_ada, ln_g=ln_g, ln_b=ln_b, gla_w_in=gla_w_in, gla_w_gk2=gla_w_gk2,
                   gla_b_gk=gla_b_gk, gla_g_norm=gla_g_norm, gla_w_out=gla_w_out, att_w_in=att_w_in,
                   att_b_in=att_b_in, att_rel_bias=att_rel_bias, att_w_out=att_w_out, ff_w1=ff_w1, ff_w2=ff_w2)
    mom1 = dict(w_ada=m_w_ada, b_ada=m_b_ada, ln_g=m_ln_g, ln_b=m_ln_b, gla_w_in=m_gla_w_in, gla_w_gk2=m_gla_w_gk2,
                gla_b_gk=m_gla_b_gk, gla_g_norm=m_gla_g_norm, gla_w_out=m_gla_w_out, att_w_in=m_att_w_in,
                att_b_in=m_att_b_in, att_rel_bias=m_att_rel_bias, att_w_out=m_att_w_out, ff_w1=m_ff_w1, ff_w2=m_ff_w2)
    mom2 = dict(w_ada=v_w_ada, b_ada=v_b_ada, ln_g=v_ln_g, ln_b=v_ln_b, gla_w_in=v_gla_w_in, gla_w_gk2=v_gla_w_gk2,
                gla_b_gk=v_gla_b_gk, gla_g_norm=v_gla_g_norm, gla_w_out=v_gla_w_out, att_w_in=v_att_w_in,
                att_b_in=v_att_b_in, att_rel_bias=v_att_rel_bias, att_w_out=v_att_w_out, ff_w1=v_ff_w1, ff_w2=v_ff_w2)

    ax, ay, ac = lax.axis_index("x"), lax.axis_index("y"), lax.axis_index("c")
    chip = 2 * ax + ay
    dev = 2 * chip + ac
    S = x.shape[1]
    x2 = x.reshape(S, D_MODEL)
    t2 = loss_target.reshape(S, D_MODEL)

    comm = Comm(weights, ac, chip)

    small_rows = 16
    spack = _pack_small([c] + [weights[n] for n, _ in SMALL_SHARDED], small_rows)
    sg = all_gather8(spack, "gather_small").reshape(N_DEV, small_rows, PACK_W)
    parts = _unpack_small(sg, [(1, D_MODEL)] + [shp for _, shp in SMALL_SHARDED])
    c_all = parts[0].reshape(N_DEV, D_MODEL)
    small = {n: _unshard_last(p[0::2]) for (n, _), p in zip(SMALL_SHARDED, parts[1:])}
    small["gla_b_gk"] = gla_b_gk
    small["att_rel_bias"] = att_rel_bias

    c_act = silu_rows(jnp.pad(c_all, ((0, 128 - N_DEV), (0, 0))), "silu_c")
    wa = w_ada.astype(BF16).transpose(1, 0, 2).reshape(1, D_MODEL, DEPTH * 6 * D_MODEL // N_CHIPS)
    mods_part = mm_plain(c_act, wa, 0, "ada_fwd", tm=128)[:N_DEV]
    mg = all_gather8(mods_part, "gather_mods").reshape(N_CHIPS, 2, N_DEV, DEPTH, 6 * D_MODEL // N_CHIPS)
    mods_mine = lax.dynamic_index_in_dim(mg[:, 0], dev, axis=1, keepdims=False)
    mods = mods_mine.transpose(1, 0, 2).reshape(DEPTH, 6 * D_MODEL) + b_ada
    mods = mods.reshape(DEPTH, 6, D_MODEL)

    sq, grad_x, dmods, g_small = local_step(x2, t2, mods, comm, small)
    loss = lax.psum(0.5 * sq[0, 0] / D_MODEL, ("x", "y", "c"))

    g_shard = comm.reduce_end()

    dm_flat = dmods.reshape(DEPTH, 6 * D_MODEL)
    g_rows = 80
    gpack = _pack_small([dm_flat] + [g_small[n] for n in SMALL_GRAD_ORDER], g_rows)
    gg = all_gather8(gpack, "gather_small_grads").reshape(N_DEV, g_rows, PACK_W)
    gsum = sum_over_devices(gg, "sum_small_grads")
    sums = _unpack_small(gsum, [(DEPTH, 6 * D_MODEL)] + [SMALL_FULL[n] for n in SMALL_GRAD_ORDER])
    grads = dict(b_ada=sums[0])
    for n, full_g in zip(SMALL_GRAD_ORDER, sums[1:]):
        grads[n] = full_g if n in ("gla_b_gk", "att_rel_bias") else _shard_last(full_g, chip)
    dm_all = _unpack_small(gg, [(DEPTH, 6 * D_MODEL)])[0]
    dm_cols = _shard_last(dm_all, chip).reshape(N_DEV, DEPTH * 6 * D_MODEL // N_CHIPS)
    dm_cols = jnp.pad(dm_cols, ((0, 128 - N_DEV), (0, 0))).astype(BF16)
    gwa = mm_w(c_act, dm_cols, "ada_bwd", ts=128)
    grads["w_ada"] = gwa.reshape(D_MODEL, DEPTH, 6 * D_MODEL // N_CHIPS).transpose(1, 0, 2)
    grads.update(g_shard)

    deltas, new_m, new_v = {}, {}, {}
    for n in WEIGHT_NAMES:
        deltas[n], new_m[n], new_v[n] = adamw(weights[n], grads[n], mom1[n], mom2[n], "adamw_" + n)

    return (loss, grad_x.reshape(1, S, D_MODEL), *[grads[n] for n in WEIGHT_NAMES], *[deltas[n] for n in WEIGHT_NAMES],
            *[new_m[n] for n in WEIGHT_NAMES], *[new_v[n] for n in WEIGHT_NAMES])
```

```python
import functools

import jax
import jax.numpy as jnp
from jax import lax
from jax.experimental import pallas as pl
from jax.experimental.pallas import tpu as pltpu

F32 = jnp.float32
BF16 = jnp.bfloat16
HIGHEST = lax.Precision.HIGHEST
MESH = pl.DeviceIdType.MESH

D_MODEL = 1024
DEPTH = 4
CHUNK = 64
GLA_HEADS = 4
GLA_DK = 512
GLA_DV = 1024
GLA_DK_HEAD = 128
GLA_DV_HEAD = 256
GLA_RANK = 16
GLA_IN = 3088
GLA_IN_PAD = 3200
GLA_LR_OFF = 3072
ATT_HEADS = 16
ATT_HD = 64
LEFT_CHUNKS = 8
MAX_REL = 128
N_REL = 257
D_FF = 4096
ALPHA = (2.0 * DEPTH) ** 0.25
LN_EPS = 1e-5
RMS_EPS = 1e-6
NEG_INF = -1e30
GLA_SCALE = GLA_DK_HEAD ** -0.5
ATT_SCALE = ATT_HD ** -0.5
ADAM_LR = 0.001
ADAM_B1 = 0.9
ADAM_B2 = 0.999
ADAM_EPS = 1e-08
ADAM_WD = 0.01
ADAM_STEP = 10

ATT_TQ = 256
ATT_KW = 768
GLA_TB = 256
GLA_GROUP = 2
VMEM_LIMIT = 56 * 1024 * 1024
WHOLE_WEIGHT_BYTES = 8 * 1024 * 1024
N_CHIPS = 4
N_DEV = 8
PACK_W = 1024


def _dot(a, b):
    return jnp.dot(a, b, preferred_element_type=F32)


def _dot_nt(a, b):
    return lax.dot_general(a, b, (((1,), (1,)), ((), ())), preferred_element_type=F32)


def _dot_tn(a, b):
    return lax.dot_general(a, b, (((0,), (0,)), ((), ())), preferred_element_type=F32)


def _cp(sem, vmem=VMEM_LIMIT):
    return pltpu.CompilerParams(dimension_semantics=sem, vmem_limit_bytes=vmem)


def _row_spec(n):
    return pl.BlockSpec((1, n), lambda *_: (0, 0))


def _sigmoid(x):
    return 1.0 / (1.0 + jnp.exp(-x))


def _log_sigmoid(x):
    return jnp.minimum(x, 0.0) - jnp.log1p(jnp.exp(-jnp.abs(x)))


class Side:
    def __init__(self, ins, out_shapes, n_sems, n_local, start, wait):
        self.ins, self.out_shapes, self.n_sems, self.n_local = list(ins), list(out_shapes), n_sems, n_local
        self.start, self.wait = start, wait

    def sem_shapes(self):
        return [pltpu.SemaphoreType.DMA((self.n_sems,)), pltpu.SemaphoreType.DMA((self.n_sems,)),
                pltpu.SemaphoreType.DMA((max(self.n_local, 1),))]


def run_side(side, name):
    n_in = len(side.ins)
    n_out = len(side.out_shapes)

    def body(*refs):
        ins, outs, sems = refs[:n_in], refs[n_in:n_in + n_out], refs[n_in + n_out:]
        side.start(ins, outs, *sems)
        side.wait(ins, outs, *sems)

    any_spec = pl.BlockSpec(memory_space=pl.ANY)
    return pl.pallas_call(body, name=name, out_shape=side.out_shapes, in_specs=[any_spec] * n_in,
                          out_specs=[any_spec] * n_out, scratch_shapes=side.sem_shapes())(*side.ins)


def hosted_call(main, side, *, name, grid, in_specs, out_specs, out_shape, scratch_shapes, dims, args):
    if side is None:
        outs = pl.pallas_call(main, name=name, grid=grid, in_specs=in_specs, out_specs=out_specs,
                              out_shape=out_shape, scratch_shapes=scratch_shapes, compiler_params=_cp(dims))(*args)
        return list(outs), []
    n_mi, n_mo, n_ms = len(in_specs), len(out_specs), len(scratch_shapes)
    n_si, n_so = len(side.ins), len(side.out_shapes)

    def kern(*refs):
        mi, si = refs[:n_mi], refs[n_mi:n_mi + n_si]
        o0 = n_mi + n_si
        mo, so = refs[o0:o0 + n_mo], refs[o0 + n_mo:o0 + n_mo + n_so]
        s0 = o0 + n_mo + n_so
        ms, sems = refs[s0:s0 + n_ms], refs[s0 + n_ms:]
        ids = [pl.program_id(d) for d in range(len(grid))]
        first = functools.reduce(jnp.logical_and, [i == 0 for i in ids])
        last = functools.reduce(jnp.logical_and, [i == g - 1 for i, g in zip(ids, grid)])

        @pl.when(first)
        def _():
            side.start(si, so, *sems)
        main(*mi, *mo, *ms)

        @pl.when(last)
        def _():
            side.wait(si, so, *sems)

    any_spec = pl.BlockSpec(memory_space=pl.ANY)
    outs = pl.pallas_call(
        kern, name=name, grid=grid, in_specs=list(in_specs) + [any_spec] * n_si,
        out_specs=list(out_specs) + [any_spec] * n_so, out_shape=list(out_shape) + side.out_shapes,
        scratch_shapes=list(scratch_shapes) + side.sem_shapes(),
        compiler_params=_cp(("arbitrary",) * len(grid)))(*args, *side.ins)
    return list(outs[:n_mo]), list(outs[n_mo:])


def modulate(x, sc, sh, name):
    S, D = x.shape
    tm = min(512, S)

    def kern(x_ref, sc_ref, sh_ref, u_ref):
        u_ref[...] = (x_ref[...] * (1.0 + sc_ref[...]) + sh_ref[...]).astype(BF16)

    return pl.pallas_call(
        kern, name=name, grid=(S // tm,),
        in_specs=[pl.BlockSpec((tm, D), lambda i: (i, 0)), _row_spec(D), _row_spec(D)],
        out_specs=pl.BlockSpec((tm, D), lambda i: (i, 0)),
        out_shape=jax.ShapeDtypeStruct((S, D), BF16),
        compiler_params=_cp(("parallel",)),
    )(x, sc, sh)


def loss_head(x, t, name):
    S, D = x.shape
    tm = min(512, S)

    def kern(x_ref, t_ref, dx_ref, l_ref):
        @pl.when(pl.program_id(0) == 0)
        def _():
            l_ref[...] = jnp.zeros_like(l_ref)
        e = x_ref[...] - t_ref[...]
        dx_ref[...] = e * (1.0 / D)
        l_ref[...] += jnp.sum(e * e)

    return pl.pallas_call(
        kern, name=name, grid=(S // tm,),
        in_specs=[pl.BlockSpec((tm, D), lambda i: (i, 0)), pl.BlockSpec((tm, D), lambda i: (i, 0))],
        out_specs=[pl.BlockSpec((tm, D), lambda i: (i, 0)), pl.BlockSpec((8, 128), lambda i: (0, 0))],
        out_shape=[jax.ShapeDtypeStruct((S, D), F32), jax.ShapeDtypeStruct((8, 128), F32)],
        compiler_params=_cp(("arbitrary",)),
    )(x, t)


def silu_rows(c_all, name):
    def kern(c_ref, o_ref):
        c = c_ref[...]
        o_ref[...] = (c * _sigmoid(c)).astype(BF16)

    return pl.pallas_call(kern, name=name, out_shape=jax.ShapeDtypeStruct(c_all.shape, BF16))(c_all)


def sum_over_devices(g, name):
    n, R, C = g.shape

    def kern(g_ref, o_ref):
        acc = g_ref[0]
        for d in range(1, n):
            acc = acc + g_ref[d]
        o_ref[...] = acc

    return pl.pallas_call(kern, name=name, out_shape=jax.ShapeDtypeStruct((R, C), F32))(g)


def _rows_block(R, C, budget=1 << 20):
    if R * C * 4 <= budget or R % 8:
        return R
    tr = max(8, (budget // (C * 4)) // 8 * 8)
    while R % tr:
        tr -= 8
    return tr


def adamw(w, g, m, v, name):
    shape = w.shape
    C = shape[-1]
    R = w.size // C
    w2, g2, m2, v2 = (t.reshape(R, C) for t in (w, g, m, v))
    tr = _rows_block(R, C)
    c1 = 1.0 - ADAM_B1 ** ADAM_STEP
    c2 = 1.0 - ADAM_B2 ** ADAM_STEP

    def kern(w_ref, g_ref, m_ref, v_ref, d_ref, nm_ref, nv_ref):
        gg = g_ref[...]
        nm = ADAM_B1 * m_ref[...] + (1.0 - ADAM_B1) * gg
        nv = ADAM_B2 * v_ref[...] + (1.0 - ADAM_B2) * (gg * gg)
        m_hat = nm / c1
        v_hat = nv / c2
        d_ref[...] = -ADAM_LR * (m_hat / (jnp.sqrt(v_hat) + ADAM_EPS) + ADAM_WD * w_ref[...])
        nm_ref[...] = nm
        nv_ref[...] = nv

    spec = pl.BlockSpec((tr, C), lambda i: (i, 0))
    outs = pl.pallas_call(
        kern, name=name, grid=(R // tr,),
        in_specs=[spec] * 4, out_specs=[spec] * 3,
        out_shape=[jax.ShapeDtypeStruct((R, C), F32)] * 3,
        compiler_params=_cp(("parallel",)),
    )(w2, g2, m2, v2)
    return tuple(o.reshape(shape) for o in outs)


def _tn_for(N):
    for tn in (1024, 768, 640, 512, 384, 256, 128):
        if N % tn == 0:
            return tn
    return N


def mm_plain(a, b3, layer, name, *, mode="f32", nt=False, bias=None, h=None, tm=1024, side=None):
    M, K = a.shape
    N = b3.shape[1] if nt else b3.shape[2]
    if K * N * 2 <= WHOLE_WEIGHT_BYTES:
        tn, tm = N, min(tm, 512)
    else:
        tn = _tn_for(N)
    tm = min(tm, M)
    a_spec = pl.BlockSpec((tm, K), lambda j, i: (i, 0))
    if nt:
        b_spec = pl.BlockSpec((None, tn, K), lambda j, i: (layer, j, 0))
    else:
        b_spec = pl.BlockSpec((None, K, tn), lambda j, i: (layer, 0, j))
    o_spec = pl.BlockSpec((tm, tn), lambda j, i: (i, j))
    ins, in_specs = [a, b3], [a_spec, b_spec]
    if bias is not None:
        ins.append(bias)
        in_specs.append(pl.BlockSpec((1, tn), lambda j, i: (0, j)))
    if mode == "mlp_dn":
        ins.append(h)
        in_specs.append(o_spec)
    elif mode not in ("f32", "bf16", "mlp_up"):
        raise ValueError(mode)
    odt = F32 if mode == "f32" else BF16

    def kern(a_ref, b_ref, *rest):
        rest = list(rest)
        bias_ref = rest.pop(0) if bias is not None else None
        h_ref = rest.pop(0) if mode == "mlp_dn" else None
        o_ref = rest.pop(0)
        if nt:
            bt_ref = rest.pop(0)

            @pl.when(pl.program_id(1) == 0)
            def _():
                bt_ref[...] = b_ref[...].T
            acc = _dot(a_ref[...], bt_ref[...])
        else:
            acc = _dot(a_ref[...], b_ref[...])
        if bias_ref is not None:
            acc = acc + bias_ref[...]
        if mode == "mlp_up":
            r = jnp.maximum(acc, 0.0)
            acc = r * r
        elif mode == "mlp_dn":
            acc = acc * (2.0 * jnp.sqrt(h_ref[...].astype(F32)))
        o_ref[...] = acc.astype(odt)

    outs, landed = hosted_call(
        kern, side, name=name, grid=(N // tn, M // tm), in_specs=in_specs, out_specs=[o_spec],
        out_shape=[jax.ShapeDtypeStruct((M, N), odt)],
        scratch_shapes=[pltpu.VMEM((K, tn), BF16)] if nt else [], dims=("parallel", "arbitrary"), args=tuple(ins))
    return outs[0] if side is None else (outs[0], landed)


def mm_down_ln(a, b3, layer, x_in, gate1p, ln_g, ln_b, sc_next, sh_next, name, *, side=None, tm=256):
    M, K = a.shape
    D = b3.shape[2]
    tm = min(tm, M)

    def kern(a_ref, b_ref, x_ref, gp_ref, lg_ref, lb_ref, sc_ref, sh_ref, y_ref, xo_ref, u_ref):
        y = _dot(a_ref[...], b_ref[...])
        y_ref[...] = y.astype(BF16)
        z = ALPHA * x_ref[...] + gp_ref[...] * y
        mu = jnp.mean(z, axis=-1, keepdims=True)
        zc = z - mu
        var = jnp.mean(zc * zc, axis=-1, keepdims=True)
        xo = (zc * lax.rsqrt(var + LN_EPS)) * lg_ref[...] + lb_ref[...]
        xo_ref[...] = xo
        u_ref[...] = (xo * (1.0 + sc_ref[...]) + sh_ref[...]).astype(BF16)

    tile = pl.BlockSpec((tm, D), lambda i: (i, 0))
    outs, landed = hosted_call(
        kern, side, name=name, grid=(M // tm,),
        in_specs=[pl.BlockSpec((tm, K), lambda i: (i, 0)), pl.BlockSpec((None, K, D), lambda i: (layer, 0, 0)), tile]
        + [_row_spec(D)] * 5,
        out_specs=[tile, tile, tile],
        out_shape=[jax.ShapeDtypeStruct((M, D), BF16), jax.ShapeDtypeStruct((M, D), F32),
                   jax.ShapeDtypeStruct((M, D), BF16)],
        scratch_shapes=[], dims=("parallel",), args=(a, b3, x_in, gate1p, ln_g, ln_b, sc_next, sh_next))
    return tuple(outs), landed


def mm_down_comb(a, b3, layer, dz, x_in, sc1p, name, *, parts=1, ln=None, side=None, tm=256):
    D, K = b3.shape[1], b3.shape[2]
    M = a.shape[-2]
    kp = K // parts
    tm = min(tm, M)
    n_ln = 0 if ln is None else 4

    def kern(*refs):
        a_refs = refs[:parts]
        b_ref, dz_ref, x_ref, sp_ref = refs[parts:parts + 4]
        ln_refs = refs[parts + 4:parts + 4 + n_ln]
        outs = refs[parts + 4 + n_ln:]

        @pl.when(pl.program_id(0) == 0)
        def _():
            for o in outs:
                if o.shape[0] == 8:
                    o[...] = jnp.zeros_like(o)
        if parts == 1:
            du = _dot_nt(a_refs[0][...], b_ref[...])
        else:
            du = _dot_nt(a_refs[0][...], b_ref[:, 0:kp])
            for p in range(1, parts):
                du = du + _dot_nt(a_refs[p][...], b_ref[:, p * kp:(p + 1) * kp])
        dx = ALPHA * dz_ref[...] + du * sp_ref[...]
        if ln is None:
            dx_ref, s_ref = outs
            dx_ref[...] = dx
        else:
            dzl_ref, dyl_ref, s_ref, sl_ref = outs
            _ln_bwd_tile(dx, *ln_refs, dzl_ref, dyl_ref, sl_ref)
        s_ref[0:1, :] += jnp.sum(du * x_ref[...], axis=0, keepdims=True)
        s_ref[1:2, :] += jnp.sum(du, axis=0, keepdims=True)
        if parts > 1:
            for p in range(parts):
                s_ref[2 + p:3 + p, :] += jnp.sum(a_refs[p][...].astype(F32), axis=0, keepdims=True)

    tile = pl.BlockSpec((tm, D), lambda i: (i, 0))
    sums = pl.BlockSpec((8, D), lambda i: (0, 0))
    if parts == 1:
        a_ins, a_specs = [a], [pl.BlockSpec((tm, K), lambda i: (i, 0))]
    else:
        assert kp == D and parts <= 6
        a_ins = [a] * parts
        a_specs = [pl.BlockSpec((None, tm, kp), functools.partial(lambda i, p: (p, i, 0), p=p)) for p in range(parts)]
    in_specs = a_specs + [pl.BlockSpec((None, D, K), lambda i: (layer, 0, 0)), tile, tile, _row_spec(D)]
    args = a_ins + [b3, dz, x_in, sc1p]
    if ln is None:
        out_specs = [tile, sums]
        out_shape = [jax.ShapeDtypeStruct((M, D), F32), jax.ShapeDtypeStruct((8, D), F32)]
    else:
        in_specs += [tile, tile, _row_spec(D), _row_spec(D)]
        args += list(ln)
        out_specs = [tile, tile, sums, sums]
        out_shape = [jax.ShapeDtypeStruct((M, D), F32), jax.ShapeDtypeStruct((M, D), BF16),
                     jax.ShapeDtypeStruct((8, D), F32), jax.ShapeDtypeStruct((8, D), F32)]
    return hosted_call(kern, side, name=name, grid=(M // tm,), in_specs=in_specs, out_specs=out_specs,
                       out_shape=out_shape, scratch_shapes=[], dims=("arbitrary",), args=tuple(args))


def mm_w(a, b, name, *, ts=2048, tk=512, chips_out=False, b_parts=1, tn=None):
    S, K = a.shape
    npart = b.shape[-1]
    N = npart * b_parts
    ts = min(ts, S)
    tk = min(tk, K)
    n_chip = N // N_CHIPS
    if tn is None:
        tn = _tn_for(n_chip if chips_out else npart)
    assert npart % tn == 0 and (not chips_out or n_chip % tn == 0)

    def kern(a_ref, b_ref, o_ref):
        @pl.when(pl.program_id(2) == 0)
        def _():
            o_ref[...] = jnp.zeros_like(o_ref)
        o_ref[...] += _dot_tn(a_ref[...], b_ref[...])

    if b_parts == 1:
        b_spec = pl.BlockSpec((ts, tn), lambda k, n, s: (s, n))
    else:
        per = npart // tn
        b_spec = pl.BlockSpec((None, ts, tn), lambda k, n, s: (n // per, s, n % per))
    if chips_out:
        per_chip = n_chip // tn
        o_spec = pl.BlockSpec((None, tk, tn), lambda k, n, s: (n // per_chip, k, n % per_chip))
        out_shape = jax.ShapeDtypeStruct((N_CHIPS, K, n_chip), F32)
    else:
        o_spec = pl.BlockSpec((tk, tn), lambda k, n, s: (k, n))
        out_shape = jax.ShapeDtypeStruct((K, N), F32)
    return pl.pallas_call(
        kern, name=name, grid=(K // tk, N // tn, S // ts),
        in_specs=[pl.BlockSpec((ts, tk), lambda k, n, s: (s, k)), b_spec],
        out_specs=o_spec, out_shape=out_shape,
        compiler_params=_cp(("parallel", "parallel", "arbitrary")),
    )(a, b)


def mm_w_chips3(a, b3, name, *, ts=512):
    S, K = a.shape
    P = b3.shape[2]
    n_chip = 3 * P // N_CHIPS
    ts = min(ts, S)
    pieces = []
    for chip in range(N_CHIPS):
        lo, hi = chip * n_chip, (chip + 1) * n_chip
        while lo < hi:
            part = lo // P
            w = min(hi, (part + 1) * P) - lo
            pieces.append((chip, lo - chip * n_chip, part, lo - part * P, w))
            lo += w

    def kern(a_ref, b_ref, o_ref):
        @pl.when(pl.program_id(0) == 0)
        def _():
            o_ref[...] = jnp.zeros_like(o_ref)
        at = a_ref[...].T
        for chip, oc, part, pc, w in pieces:
            o_ref[chip, :, oc:oc + w] += _dot(at, b_ref[part, :, pc:pc + w])

    return pl.pallas_call(
        kern, name=name, grid=(S // ts,),
        in_specs=[pl.BlockSpec((ts, K), lambda s: (s, 0)), pl.BlockSpec((3, ts, P), lambda s: (0, s, 0))],
        out_specs=pl.BlockSpec((N_CHIPS, K, n_chip), lambda s: (0, 0, 0)),
        out_shape=jax.ShapeDtypeStruct((N_CHIPS, K, n_chip), F32),
        compiler_params=_cp(("arbitrary",)),
    )(a, b3)


def mm_w_res(a, b, name, *, chips_out=False, ts=512):
    S, K = a.shape
    N = b.shape[1]
    ts = min(ts, S)
    n_chip = N // N_CHIPS

    def kern(a_ref, b_ref, o_ref):
        @pl.when(pl.program_id(0) == 0)
        def _():
            o_ref[...] = jnp.zeros_like(o_ref)
        at = a_ref[...].T
        if chips_out:
            for chip in range(N_CHIPS):
                o_ref[chip] += _dot(at, b_ref[:, chip * n_chip:(chip + 1) * n_chip])
        else:
            o_ref[...] += _dot(at, b_ref[...])

    o_shape = (N_CHIPS, K, n_chip) if chips_out else (K, N)
    return pl.pallas_call(
        kern, name=name, grid=(S // ts,),
        in_specs=[pl.BlockSpec((ts, K), lambda s: (s, 0)), pl.BlockSpec((ts, N), lambda s: (s, 0))],
        out_specs=pl.BlockSpec(o_shape, lambda s: (0,) * len(o_shape)),
        out_shape=jax.ShapeDtypeStruct(o_shape, F32),
        compiler_params=_cp(("arbitrary",)),
    )(a, b)


def mm_f32(a, b, name):
    def kern(a_ref, b_ref, o_ref):
        o_ref[...] = jnp.dot(a_ref[...], b_ref[...], precision=HIGHEST, preferred_element_type=F32)

    return pl.pallas_call(kern, name=name, out_shape=jax.ShapeDtypeStruct((a.shape[0], b.shape[1]), F32),
                          compiler_params=pltpu.CompilerParams(vmem_limit_bytes=VMEM_LIMIT))(a, b)


def _ln_bwd_tile(dxo_t, x_ref, y_ref, gp_ref, lg_ref, dz_ref, dy_ref, s_ref):
    yv = y_ref[...].astype(F32)
    z = ALPHA * x_ref[...] + gp_ref[...] * yv
    mu = jnp.mean(z, axis=-1, keepdims=True)
    zc = z - mu
    var = jnp.mean(zc * zc, axis=-1, keepdims=True)
    rstd = lax.rsqrt(var + LN_EPS)
    xhat = zc * rstd
    dxh = dxo_t * lg_ref[...]
    dz = rstd * (dxh - jnp.mean(dxh, axis=-1, keepdims=True)
                 - xhat * jnp.mean(dxh * xhat, axis=-1, keepdims=True))
    dz_ref[...] = dz
    dy_ref[...] = (gp_ref[...] * dz).astype(BF16)
    s_ref[0:1, :] += jnp.sum(dxo_t * xhat, axis=0, keepdims=True)
    s_ref[1:2, :] += jnp.sum(dxo_t, axis=0, keepdims=True)
    s_ref[2:3, :] += jnp.sum(dz * yv, axis=0, keepdims=True)


def ln_bwd(dxo, x_in, y, gate1p, ln_g, name, *, tm=256):
    S, D = dxo.shape
    tm = min(tm, S)

    def kern(dxo_ref, x_ref, y_ref, gp_ref, lg_ref, dz_ref, dy_ref, s_ref):
        @pl.when(pl.program_id(0) == 0)
        def _():
            s_ref[...] = jnp.zeros_like(s_ref)
        _ln_bwd_tile(dxo_ref[...], x_ref, y_ref, gp_ref, lg_ref, dz_ref, dy_ref, s_ref)

    tile = pl.BlockSpec((tm, D), lambda i: (i, 0))
    return pl.pallas_call(
        kern, name=name, grid=(S // tm,),
        in_specs=[tile, tile, tile, _row_spec(D), _row_spec(D)],
        out_specs=[tile, tile, pl.BlockSpec((8, D), lambda i: (0, 0))],
        out_shape=[jax.ShapeDtypeStruct((S, D), F32), jax.ShapeDtypeStruct((S, D), BF16),
                   jax.ShapeDtypeStruct((8, D), F32)],
        compiler_params=_cp(("arbitrary",)),
    )(dxo, x_in, y, gate1p, ln_g)


def _tri64():
    r = lax.broadcasted_iota(jnp.int32, (CHUNK, CHUNK), 0)
    c = lax.broadcasted_iota(jnp.int32, (CHUNK, CHUNK), 1)
    return r >= c


def _gla_chunk_common(proj_ref, rows, b, h):
    kc = slice(h * GLA_DK_HEAD, (h + 1) * GLA_DK_HEAD)
    bh = b[:, kc]
    ep = jnp.exp(bh)
    en = jnp.exp(-bh)
    bl = bh[CHUNK - 1:CHUNK, :]
    ee = jnp.exp(bl - bh)
    dec = jnp.exp(bl)
    q = proj_ref[rows, h * GLA_DK_HEAD:(h + 1) * GLA_DK_HEAD] * GLA_SCALE
    k = proj_ref[rows, GLA_DK + h * GLA_DK_HEAD:GLA_DK + (h + 1) * GLA_DK_HEAD]
    v = proj_ref[rows, 2 * GLA_DK + h * GLA_DV_HEAD:2 * GLA_DK + (h + 1) * GLA_DV_HEAD]
    g = proj_ref[rows, 2 * GLA_DK + GLA_DV + h * GLA_DV_HEAD:2 * GLA_DK + GLA_DV + (h + 1) * GLA_DV_HEAD]
    return ep, en, ee, dec, q, k, v, g


def gla_fwd(proj, wgk_p, bgk, gnorm, name, side=None):
    S = proj.shape[0]
    TB = min(GLA_TB, S)
    ncb = TB // CHUNK

    def kern(proj_ref, wgk_ref, bgk_ref, gn_ref, zg_ref, st_ref, state_scr, la_scr):
        @pl.when(pl.program_id(0) == 0)
        def _():
            state_scr[...] = jnp.zeros_like(state_scr)
        lr = proj_ref[:, GLA_LR_OFF:GLA_IN_PAD].astype(BF16)
        gk = _dot(lr, wgk_ref[...]) + bgk_ref[...]
        la_scr[...] = _log_sigmoid(gk) * (1.0 / 16.0)
        lower = _tri64()
        tri = lower.astype(F32)

        def group(gi, carry):
            rows = [pl.ds(pl.multiple_of((gi * GLA_GROUP + g) * CHUNK, CHUNK), CHUNK) for g in range(GLA_GROUP)]
            b = [jnp.dot(tri, la_scr[r, :], precision=HIGHEST, preferred_element_type=F32) for r in rows]
            P = [(g, h) for g in range(GLA_GROUP) for h in range(GLA_HEADS)]
            cm = {p: _gla_chunk_common(proj_ref, rows[p[0]], b[p[0]], p[1]) for p in P}
            qf = {p: (cm[p][4] * cm[p][0]).astype(BF16) for p in P}
            kn = {p: (cm[p][5] * cm[p][1]).astype(BF16) for p in P}
            qn = {p: (cm[p][4] * cm[p][1]).astype(BF16) for p in P}
            kp = {p: (cm[p][5] * cm[p][0]).astype(BF16) for p in P}
            ke = {p: (cm[p][5] * cm[p][2]).astype(BF16) for p in P}
            vb = {p: cm[p][6].astype(BF16) for p in P}
            a_f = {p: _dot_nt(qf[p], kn[p]) for p in P}
            a_b = {p: _dot_nt(qn[p], kp[p]) for p in P}
            upd = {p: _dot_tn(vb[p], ke[p]) for p in P}
            st = {(0, h): state_scr[h] for h in range(GLA_HEADS)}
            for g in range(GLA_GROUP):
                for h in range(GLA_HEADS):
                    st[(g + 1, h)] = st[(g, h)] * cm[(g, h)][3] + upd[(g, h)]
            o_st = {p: _dot_nt(qf[p], st[p].astype(BF16)) for p in P}
            amat = {p: jnp.where(lower, a_f[p], a_b[p]).astype(BF16) for p in P}
            o = {p: _dot(amat[p], vb[p]) + o_st[p] for p in P}
            for g, h in P:
                st_ref[gi * GLA_GROUP + g, h] = st[(g, h)]
            for h in range(GLA_HEADS):
                state_scr[h] = st[(GLA_GROUP, h)]
            for g, h in P:
                gate = cm[(g, h)][7]
                vc = slice(h * GLA_DV_HEAD, (h + 1) * GLA_DV_HEAD)
                r = lax.rsqrt(jnp.mean(o[(g, h)] * o[(g, h)], axis=-1, keepdims=True) + RMS_EPS)
                on = (o[(g, h)] * r) * gn_ref[:, vc]
                zg_ref[rows[g], vc] = (on * (gate * _sigmoid(gate))).astype(BF16)
            return carry

        lax.fori_loop(0, ncb // GLA_GROUP, group, 0)

    return hosted_call(
        kern, side, name=name, grid=(S // TB,),
        in_specs=[pl.BlockSpec((TB, GLA_IN_PAD), lambda i: (i, 0)),
                  pl.BlockSpec((128, GLA_DK), lambda i: (0, 0)), _row_spec(GLA_DK), _row_spec(GLA_DV)],
        out_specs=[pl.BlockSpec((TB, GLA_DV), lambda i: (i, 0)),
                   pl.BlockSpec((ncb, GLA_HEADS, GLA_DV_HEAD, GLA_DK_HEAD), lambda i: (i, 0, 0, 0))],
        out_shape=[jax.ShapeDtypeStruct((S, GLA_DV), BF16),
                   jax.ShapeDtypeStruct((S // CHUNK, GLA_HEADS, GLA_DV_HEAD, GLA_DK_HEAD), F32)],
        scratch_shapes=[pltpu.VMEM((GLA_HEADS, GLA_DV_HEAD, GLA_DK_HEAD), F32), pltpu.VMEM((TB, GLA_DK), F32)],
        dims=("arbitrary",), args=(proj, wgk_p, bgk, gnorm))


def gla_bwd(proj, states, dzg, wgk_p, bgk, gnorm, name, side=None):
    S = proj.shape[0]
    TB = min(GLA_TB, S)
    ncb = TB // CHUNK
    nb = S // TB

    def kern(proj_ref, st_ref, dzg_ref, wgk_ref, bgk_ref, gn_ref,
             dproj_ref, dwgk_ref, dbgk_ref, dgn_ref, dstate_scr, la_scr, gk_scr, dgk_scr):
        @pl.when(pl.program_id(0) == 0)
        def _():
            dstate_scr[...] = jnp.zeros_like(dstate_scr)
            dwgk_ref[...] = jnp.zeros_like(dwgk_ref)
            dbgk_ref[...] = jnp.zeros_like(dbgk_ref)
            dgn_ref[...] = jnp.zeros_like(dgn_ref)
        lr = proj_ref[:, GLA_LR_OFF:GLA_IN_PAD].astype(BF16)
        gk = _dot(lr, wgk_ref[...]) + bgk_ref[...]
        gk_scr[...] = gk
        la_scr[...] = _log_sigmoid(gk) * (1.0 / 16.0)
        lower = _tri64()
        tri = lower.astype(F32)
        r_i = lax.broadcasted_iota(jnp.int32, (CHUNK, CHUNK), 0)
        c_i = lax.broadcasted_iota(jnp.int32, (CHUNK, CHUNK), 1)
        triu = (c_i >= r_i).astype(F32)
        last_row = lax.broadcasted_iota(jnp.int32, (CHUNK, GLA_DK_HEAD), 0) == CHUNK - 1

        def group(gi, carry):
            cs = [ncb - 1 - (gi * GLA_GROUP + g) for g in range(GLA_GROUP)]
            rows = [pl.ds(pl.multiple_of(c * CHUNK, CHUNK), CHUNK) for c in cs]
            b = [jnp.dot(tri, la_scr[r, :], precision=HIGHEST, preferred_element_type=F32) for r in rows]
            P = [(g, h) for g in range(GLA_GROUP) for h in range(GLA_HEADS)]
            kcs = [slice(h * GLA_DK_HEAD, (h + 1) * GLA_DK_HEAD) for h in range(GLA_HEADS)]
            vcs = [slice(h * GLA_DV_HEAD, (h + 1) * GLA_DV_HEAD) for h in range(GLA_HEADS)]
            cm = {p: _gla_chunk_common(proj_ref, rows[p[0]], b[p[0]], p[1]) for p in P}
            ep, en, ee, dec = ({p: cm[p][i] for p in P} for i in range(4))
            qf = {p: cm[p][4] * cm[p][0] for p in P}
            kn = {p: cm[p][5] * cm[p][1] for p in P}
            qn = {p: cm[p][4] * cm[p][1] for p in P}
            kp = {p: cm[p][5] * cm[p][0] for p in P}
            ke = {p: cm[p][5] * cm[p][2] for p in P}
            qf_b, kn_b, qn_b, kp_b, ke_b = ({p: t[p].astype(BF16) for p in P} for t in (qf, kn, qn, kp, ke))
            vb = {p: cm[p][6].astype(BF16) for p in P}
            st = {p: st_ref[cs[p[0]], p[1]] for p in P}
            st_b = {p: st[p].astype(BF16) for p in P}
            a_f = {p: _dot_nt(qf_b[p], kn_b[p]) for p in P}
            a_b = {p: _dot_nt(qn_b[p], kp_b[p]) for p in P}
            o_st = {p: _dot_nt(qf_b[p], st_b[p]) for p in P}
            amat = {p: jnp.where(lower, a_f[p], a_b[p]).astype(BF16) for p in P}
            o = {p: _dot(amat[p], vb[p]) + o_st[p] for p in P}
            do_b, dgs = {}, {}
            for p in P:
                g, h = p
                r = lax.rsqrt(jnp.mean(o[p] * o[p], axis=-1, keepdims=True) + RMS_EPS)
                oh = o[p] * r
                gn = gn_ref[:, vcs[h]]
                gate = cm[p][7]
                sg = _sigmoid(gate)
                dz = dzg_ref[rows[g], vcs[h]]
                don = dz * (gate * sg)
                dgs[p] = dz * (oh * gn) * (sg * (1.0 + gate * (1.0 - sg)))
                dgn_ref[:, vcs[h]] += jnp.sum(don * oh, axis=0, keepdims=True)
                doh = don * gn
                do_b[p] = (r * (doh - oh * jnp.mean(doh * oh, axis=-1, keepdims=True))).astype(BF16)
            da = {p: _dot_nt(do_b[p], vb[p]) for p in P}
            dv_a = {p: _dot_tn(amat[p], do_b[p]) for p in P}
            dqf_st = {p: _dot(do_b[p], st_b[p]) for p in P}
            dst_upd = {p: _dot_tn(do_b[p], qf_b[p]) for p in P}
            dst = {(0, h): dstate_scr[h] for h in range(GLA_HEADS)}
            for g in range(GLA_GROUP):
                for h in range(GLA_HEADS):
                    dst[(g + 1, h)] = dst[(g, h)] * dec[(g, h)] + dst_upd[(g, h)]
            for h in range(GLA_HEADS):
                dstate_scr[h] = dst[(GLA_GROUP, h)]
            dst_b = {p: dst[p].astype(BF16) for p in P}
            dv = {p: dv_a[p] + _dot_nt(ke_b[p], dst_b[p]) for p in P}
            dke = {p: _dot(vb[p], dst_b[p]) for p in P}
            da_f = {p: jnp.where(lower, da[p], 0.0).astype(BF16) for p in P}
            da_b = {p: jnp.where(lower, 0.0, da[p]).astype(BF16) for p in P}
            dqf = {p: _dot(da_f[p], kn_b[p]) + dqf_st[p] for p in P}
            dkn = {p: _dot_tn(da_f[p], qf_b[p]) for p in P}
            dqn = {p: _dot(da_b[p], kp_b[p]) for p in P}
            dkp = {p: _dot_tn(da_b[p], qn_b[p]) for p in P}
            dbs = {}
            for p in P:
                ddec = jnp.sum(dst[p] * st[p], axis=0, keepdims=True)
                db = dqf[p] * qf[p] - dkn[p] * kn[p] - dqn[p] * qn[p] + dkp[p] * kp[p] - dke[p] * ke[p]
                dbl = jnp.sum(dke[p] * ke[p], axis=0, keepdims=True) + ddec * dec[p]
                dbs[p] = db + jnp.where(last_row, dbl, 0.0)
            dla = {p: jnp.dot(triu, dbs[p], precision=HIGHEST, preferred_element_type=F32) for p in P}
            for p in P:
                g, h = p
                dq = (dqf[p] * ep[p] + dqn[p] * en[p]) * GLA_SCALE
                dk = dkn[p] * en[p] + dkp[p] * ep[p] + dke[p] * ee[p]
                dgk_scr[rows[g], kcs[h]] = dla[p] * (1.0 / 16.0) * _sigmoid(-gk_scr[rows[g], kcs[h]])
                dproj_ref[rows[g], kcs[h]] = dq.astype(BF16)
                dproj_ref[rows[g], GLA_DK + h * GLA_DK_HEAD:GLA_DK + (h + 1) * GLA_DK_HEAD] = dk.astype(BF16)
                dproj_ref[rows[g], 2 * GLA_DK + h * GLA_DV_HEAD:2 * GLA_DK + (h + 1) * GLA_DV_HEAD] = dv[p].astype(BF16)
                dproj_ref[rows[g], 2 * GLA_DK + GLA_DV + h * GLA_DV_HEAD:
                          2 * GLA_DK + GLA_DV + (h + 1) * GLA_DV_HEAD] = dgs[p].astype(BF16)
            return carry

        lax.fori_loop(0, ncb // GLA_GROUP, group, 0)
        dgk = dgk_scr[...]
        dgk_b = dgk.astype(BF16)
        dproj_ref[:, GLA_LR_OFF:GLA_IN_PAD] = _dot_nt(dgk_b, wgk_ref[...]).astype(BF16)
        dwgk_ref[...] += _dot_tn(lr, dgk_b)
        dbgk_ref[...] += jnp.sum(dgk, axis=0, keepdims=True)

    rev = lambda i: (nb - 1 - i, 0)
    return hosted_call(
        kern, side, name=name, grid=(nb,),
        in_specs=[pl.BlockSpec((TB, GLA_IN_PAD), rev),
                  pl.BlockSpec((ncb, GLA_HEADS, GLA_DV_HEAD, GLA_DK_HEAD), lambda i: (nb - 1 - i, 0, 0, 0)),
                  pl.BlockSpec((TB, GLA_DV), rev),
                  pl.BlockSpec((128, GLA_DK), lambda i: (0, 0)), _row_spec(GLA_DK), _row_spec(GLA_DV)],
        out_specs=[pl.BlockSpec((TB, GLA_IN_PAD), rev),
                   pl.BlockSpec((128, GLA_DK), lambda i: (0, 0)), _row_spec(GLA_DK), _row_spec(GLA_DV)],
        out_shape=[jax.ShapeDtypeStruct((S, GLA_IN_PAD), BF16), jax.ShapeDtypeStruct((128, GLA_DK), F32),
                   jax.ShapeDtypeStruct((1, GLA_DK), F32), jax.ShapeDtypeStruct((1, GLA_DV), F32)],
        scratch_shapes=[pltpu.VMEM((GLA_HEADS, GLA_DV_HEAD, GLA_DK_HEAD), F32), pltpu.VMEM((TB, GLA_DK), F32),
                        pltpu.VMEM((TB, GLA_DK), F32), pltpu.VMEM((TB, GLA_DK), F32)],
        dims=("arbitrary",), args=(proj, states, dzg, wgk_p, bgk, gnorm))


ATT_TW = 1024
ATT_CLASSES = 3


def _att_window(i):
    return pl.multiple_of(jnp.maximum(i * ATT_TQ - LEFT_CHUNKS * CHUNK, 0), ATT_TQ)


def _att_rel_index():
    e = jnp.arange(ATT_TW)[None, :]
    d = jnp.where(e < ATT_KW, e, e - ATT_TW)
    off = (jnp.arange(ATT_CLASSES) * ATT_TQ)[:, None]
    return jnp.clip(off - d, -MAX_REL, MAX_REL) + MAX_REL


def _row_bits():
    return lax.broadcasted_iota(jnp.int32, (ATT_TQ, ATT_TW), 0)


def att_bias_tiles(rel_bias, name):
    pick = (jnp.arange(384)[:, None] == _att_rel_index().reshape(-1)[None, :]).astype(F32)
    tab = mm_f32(jnp.pad(rel_bias, ((0, 0), (0, 384 - N_REL))), pick, name + "_tab")
    tab = tab.reshape(ATT_HEADS * ATT_CLASSES, 1, ATT_TW)

    def kern(t_ref, o_ref):
        cls = pl.program_id(0) % ATT_CLASSES
        x = jnp.broadcast_to(t_ref[...], (ATT_TQ, ATT_TW))
        x = pltpu.roll(x, 0, 1, stride=1, stride_axis=0)
        x = x[:, :ATT_KW]
        qc = cls * (ATT_TQ // CHUNK) + lax.shift_right_arithmetic(
            lax.broadcasted_iota(jnp.int32, (ATT_TQ, ATT_KW), 0), 6)
        kc = lax.shift_right_arithmetic(lax.broadcasted_iota(jnp.int32, (ATT_TQ, ATT_KW), 1), 6)
        o_ref[...] = jnp.where((kc <= qc) & (kc >= qc - LEFT_CHUNKS), x, NEG_INF)

    return pl.pallas_call(
        kern, name=name, grid=(ATT_HEADS * ATT_CLASSES,),
        in_specs=[pl.BlockSpec((None, 1, ATT_TW), lambda i: (i, 0, 0))],
        out_specs=pl.BlockSpec((None, ATT_TQ, ATT_KW), lambda i: (i, 0, 0)),
        out_shape=jax.ShapeDtypeStruct((ATT_HEADS * ATT_CLASSES, ATT_TQ, ATT_KW), F32),
        compiler_params=_cp(("parallel",)),
    )(tab)


def att_bias_grad(dbt, name):
    def kern(d_ref, o_ref):
        x = jnp.concatenate([d_ref[...], jnp.zeros((ATT_TQ, ATT_TW - ATT_KW), F32)], axis=1)
        row = _row_bits()
        for b in range(8):
            x = jnp.where((row & (1 << b)) != 0, pltpu.roll(x, ATT_TW - (1 << b), axis=1), x)
        o_ref[...] = jnp.sum(x, axis=0, keepdims=True)

    diag = pl.pallas_call(
        kern, name=name + "_diag", grid=(ATT_HEADS * ATT_CLASSES,),
        in_specs=[pl.BlockSpec((None, ATT_TQ, ATT_KW), lambda i: (i, 0, 0))],
        out_specs=pl.BlockSpec((None, 1, ATT_TW), lambda i: (i, 0, 0)),
        out_shape=jax.ShapeDtypeStruct((ATT_HEADS * ATT_CLASSES, 1, ATT_TW), F32),
        compiler_params=_cp(("parallel",)),
    )(dbt)
    diag = diag.reshape(ATT_HEADS, ATT_CLASSES * ATT_TW)
    onehot = (_att_rel_index().reshape(-1)[:, None] == jnp.arange(384)[None, :]).astype(F32)
    return mm_f32(diag, onehot, name + "_bins")[:, :N_REL]


def _att_scores(q_ref, kw, bias_ref):
    hs = [slice(hh * ATT_HD, (hh + 1) * ATT_HD) for hh in range(2)]
    q = [q_ref[:, h] * ATT_SCALE for h in hs]
    k = [kw[:, h] for h in hs]
    s = [_dot_nt(q[hh], k[hh]) + bias_ref[hh] for hh in range(2)]
    e = [jnp.exp(t - jnp.max(t, axis=-1, keepdims=True)) for t in s]
    inv = [1.0 / jnp.sum(t, axis=-1, keepdims=True) for t in e]
    return hs, q, k, e, inv


def _att_specs(S):
    nq = D_MODEL // 128
    q_spec = pl.BlockSpec((ATT_TQ, 128), lambda p, i: (i, p))
    k_spec = pl.BlockSpec((S, 128), lambda p, i: (0, nq + p))
    v_spec = pl.BlockSpec((S, 128), lambda p, i: (0, 2 * nq + p))
    b_spec = pl.BlockSpec((2, None, ATT_TQ, ATT_KW), lambda p, i: (p, jnp.minimum(i, ATT_CLASSES - 1), 0, 0))
    return q_spec, k_spec, v_spec, b_spec


def attn_fwd(qkv, bias, name, side=None):
    S = qkv.shape[0]
    q_spec, k_spec, v_spec, b_spec = _att_specs(S)

    def kern(q_ref, k_ref, v_ref, bias_ref, o_ref):
        ws = _att_window(pl.program_id(1))
        kw = k_ref[pl.ds(ws, ATT_KW), :]
        vw = v_ref[pl.ds(ws, ATT_KW), :]
        hs, _, _, e, inv = _att_scores(q_ref, kw, bias_ref)
        outs = [_dot(e[hh].astype(BF16), vw[:, hs[hh]]) * inv[hh] for hh in range(2)]
        o_ref[...] = jnp.concatenate(outs, axis=1).astype(BF16)

    return hosted_call(
        kern, side, name=name, grid=(ATT_HEADS // 2, S // ATT_TQ),
        in_specs=[q_spec, k_spec, v_spec, b_spec],
        out_specs=[pl.BlockSpec((ATT_TQ, 128), lambda p, i: (i, p))],
        out_shape=[jax.ShapeDtypeStruct((S, D_MODEL), BF16)],
        scratch_shapes=[], dims=("parallel", "arbitrary"), args=(qkv, qkv, qkv, bias))


def attn_bwd(qkv, bias, do, name, side=None):
    S = qkv.shape[0]
    nblk = S // ATT_TQ
    q_spec, k_spec, v_spec, b_spec = _att_specs(S)

    def kern(q_ref, k_ref, v_ref, bias_ref, do_ref, dqkv_ref, db_ref, dk_scr, dv_scr):
        i = pl.program_id(1)

        @pl.when(i == 0)
        def _():
            dk_scr[...] = jnp.zeros_like(dk_scr)
            dv_scr[...] = jnp.zeros_like(dv_scr)
            db_ref[...] = jnp.zeros_like(db_ref)
        ws = _att_window(i)
        win = pl.ds(ws, ATT_KW)
        kw = k_ref[win, :]
        vw = v_ref[win, :]
        o_cls = jnp.minimum(i, ATT_CLASSES - 1)
        R2 = range(2)
        hs, q, k, e, inv = _att_scores(q_ref, kw, bias_ref)
        do_h = [do_ref[:, h] for h in hs]
        dp = [_dot_nt(do_h[hh], vw[:, hs[hh]]) for hh in R2]
        p = [e[hh] * inv[hh] for hh in R2]
        dvs = [_dot_tn(p[hh].astype(BF16), do_h[hh]) for hh in R2]
        ds = [p[hh] * (dp[hh] - jnp.sum(p[hh] * dp[hh], axis=-1, keepdims=True)) for hh in R2]
        ds_b = [t.astype(BF16) for t in ds]
        dqs = [_dot(ds_b[hh], k[hh]) * ATT_SCALE for hh in R2]
        dks = [_dot_tn(ds_b[hh], q[hh]) for hh in R2]
        for hh in R2:
            db_ref[hh, o_cls] += ds[hh]
        dqkv_ref[0, pl.ds(pl.multiple_of(i * ATT_TQ, ATT_TQ), ATT_TQ), :] = jnp.concatenate(dqs, axis=1).astype(BF16)
        dk_scr[win, :] += jnp.concatenate(dks, axis=1)
        dv_scr[win, :] += jnp.concatenate(dvs, axis=1)

        @pl.when(i == nblk - 1)
        def _():
            dqkv_ref[1] = dk_scr[...].astype(BF16)
            dqkv_ref[2] = dv_scr[...].astype(BF16)

    return hosted_call(
        kern, side, name=name, grid=(ATT_HEADS // 2, nblk),
        in_specs=[q_spec, k_spec, v_spec, b_spec, pl.BlockSpec((ATT_TQ, 128), lambda p, i: (i, p))],
        out_specs=[pl.BlockSpec((3, S, 128), lambda p, i: (0, 0, p)),
                   pl.BlockSpec((2, ATT_CLASSES, ATT_TQ, ATT_KW), lambda p, i: (p, 0, 0, 0))],
        out_shape=[jax.ShapeDtypeStruct((3, S, D_MODEL), BF16),
                   jax.ShapeDtypeStruct((ATT_HEADS, ATT_CLASSES, ATT_TQ, ATT_KW), F32)],
        scratch_shapes=[pltpu.VMEM((S, 128), F32), pltpu.VMEM((S, 128), F32)],
        dims=("parallel", "arbitrary"), args=(qkv, qkv, qkv, bias, do))


def colsum3(a3, name):
    P, S, N = a3.shape
    tm = min(512, S)

    def kern(a_ref, o_ref):
        @pl.when(pl.program_id(1) == 0)
        def _():
            o_ref[...] = jnp.zeros_like(o_ref)
        o_ref[...] += jnp.sum(a_ref[...].astype(F32), axis=0, keepdims=True)

    return pl.pallas_call(
        kern, name=name, grid=(P, S // tm),
        in_specs=[pl.BlockSpec((None, tm, N), lambda p, i: (p, i, 0))],
        out_specs=pl.BlockSpec((None, 1, N), lambda p, i: (p, 0, 0)),
        out_shape=jax.ShapeDtypeStruct((P, 1, N), F32),
        compiler_params=_cp(("parallel", "arbitrary")),
    )(a3)


def _me():
    return lax.axis_index("x"), lax.axis_index("y"), lax.axis_index("c")


def _other_chips(x, y):
    return [(1 - x, y), (x, 1 - y), (1 - x, 1 - y)]


def all_gather8(x_shard, name):
    m_per, n = x_shard.shape

    def body(x_ref, out_ref, send_sems, recv_sems, local_sem):
        x, y, c = _me()
        me, sibling = (x, y, c), (x, y, 1 - c)
        chips = _other_chips(x, y)

        def rows(px, py, pc):
            return out_ref.at[pl.ds((4 * px + 2 * py + pc) * m_per, m_per), :]

        def copy(k, block, to, src=None):
            return pltpu.make_async_remote_copy(
                src_ref=rows(*block) if src is None else src, dst_ref=rows(*block),
                send_sem=send_sems.at[k], recv_sem=recv_sems.at[k], device_id=to, device_id_type=MESH)

        mine = pltpu.make_async_copy(x_ref, rows(*me), local_sem)
        mine.start()
        first = [copy(0, me, sibling, src=x_ref)]
        first += [copy(1 + j, me, (*chip, c), src=x_ref) for j, chip in enumerate(chips)]
        for cp in first:
            cp.start()
        passed = [copy(4 + j, (*chip, c), sibling) for j, chip in enumerate(chips)]
        for j, chip in enumerate(chips):
            copy(1 + j, (*chip, c), me).wait_recv()
            passed[j].start()
        copy(0, sibling, me).wait_recv()
        for j, chip in enumerate(chips):
            copy(4 + j, (*chip, 1 - c), me).wait_recv()
        for cp in first + passed:
            cp.wait_send()
        mine.wait()

    return pl.pallas_call(
        body, name=name,
        out_shape=jax.ShapeDtypeStruct((N_DEV * m_per, n), x_shard.dtype),
        in_specs=[pl.BlockSpec(memory_space=pltpu.VMEM)],
        out_specs=pl.BlockSpec(memory_space=pltpu.VMEM),
        scratch_shapes=[pltpu.SemaphoreType.DMA((7,)), pltpu.SemaphoreType.DMA((7,)), pltpu.SemaphoreType.DMA],
        compiler_params=pltpu.CompilerParams(vmem_limit_bytes=VMEM_LIMIT),
    )(x_shard)


def _half_rows(n_rows, c):
    h = n_rows // 2
    return pl.ds(c * h, h)


def _gathered_shape(shape, flavour):
    L, a, b = shape
    return {"col": (L, a, N_CHIPS * b), "row": (L, N_CHIPS * a, b), "lead": (N_CHIPS, L, a, b)}[flavour]


def _gathered_part(out_ref, shape, flavour, s, rows):
    L, a, b = shape
    if flavour == "col":
        return out_ref.at[:, rows, pl.ds(s * b, b)]
    if flavour == "row":
        return out_ref.at[:, pl.ds(s * a + rows.start, rows.size), :]
    return out_ref.at[s, :, rows, :]


def gather_side(shards, flavours):
    n = len(shards)
    shapes = [w.shape for w in shards]

    def copies(w_refs, out_refs, send_sems, recv_sems, local_sems):
        x, y, c = _me()
        sibling = (x, y, 1 - c)
        chips = _other_chips(x, y)
        me_s = 2 * x + y

        def copy(k, src, dst, to):
            return pltpu.make_async_remote_copy(src_ref=src, dst_ref=dst, send_sem=send_sems.at[k],
                                                recv_sem=recv_sems.at[k], device_id=to, device_id_type=MESH)

        own, first, landed, passed, passed_in = [], [], [], [], []
        for w in range(n):
            shp, fl = shapes[w], flavours[w]
            my_half = _half_rows(shp[1], c)
            sib_half = _half_rows(shp[1], 1 - c)
            own.append(copy(7 * w + 6, w_refs[w], _gathered_part(out_refs[w], shp, fl, me_s, pl.ds(0, shp[1])), sibling))
            for j, chip in enumerate(chips):
                s = 2 * chip[0] + chip[1]
                first.append(copy(7 * w + j, w_refs[w].at[:, my_half, :],
                                  _gathered_part(out_refs[w], shp, fl, me_s, my_half), (*chip, c)))
                part = _gathered_part(out_refs[w], shp, fl, s, my_half)
                landed.append(copy(7 * w + j, part, part, (*chip, c)))
                passed.append(copy(7 * w + 3 + j, part, part, sibling))
                theirs = _gathered_part(out_refs[w], shp, fl, s, sib_half)
                passed_in.append(copy(7 * w + 3 + j, theirs, theirs, sibling))
        return own, first, landed, passed, passed_in

    def start(*refs):
        own, first, _, _, _ = copies(*refs)
        for cp in first + own:
            cp.start()

    def wait(*refs):
        own, first, landed, passed, passed_in = copies(*refs)
        for arrived, onward in zip(landed, passed):
            arrived.wait_recv()
            onward.start()
        for cp in passed_in:
            cp.wait_recv()
        for cp in own:
            cp.wait()
        for cp in first + passed:
            cp.wait_send()

    out_shapes = [jax.ShapeDtypeStruct(_gathered_shape(s, f), w.dtype) for w, s, f in zip(shards, shapes, flavours)]
    return Side(shards, out_shapes, 7 * n, 0, start, wait)


def swap_side(gs):
    n = len(gs)

    def copies(g_refs, out_refs, send_sems, recv_sems, local_sems):
        x, y, c = _me()
        return [pltpu.make_async_remote_copy(
            src_ref=g_refs[w].at[:, _half_rows(gs[w].shape[1], 1 - c), :], dst_ref=out_refs[w],
            send_sem=send_sems.at[w], recv_sem=recv_sems.at[w], device_id=(x, y, 1 - c), device_id_type=MESH)
            for w in range(n)]

    def start(*refs):
        for cp in copies(*refs):
            cp.start()

    def wait(*refs):
        for cp in copies(*refs):
            cp.wait()

    out_shapes = [jax.ShapeDtypeStruct((g.shape[0], g.shape[1] // 2, g.shape[2]), g.dtype) for g in gs]
    return Side(gs, out_shapes, n, 0, start, wait)


def add_half(g, r1, c_idx, name):
    n, R, C = g.shape
    half = R // 2
    tr = _rows_block(half, C)
    nbh = half // tr

    def kern(c_ref, g_ref, r_ref, o_ref, ob_ref):
        p = g_ref[...] + r_ref[...]
        o_ref[...] = p
        ob_ref[...] = p.astype(BF16)

    spec = pl.BlockSpec((1, tr, C), lambda d, r, c_ref: (d, r, 0))
    return pl.pallas_call(
        kern, name=name,
        grid_spec=pltpu.PrefetchScalarGridSpec(
            num_scalar_prefetch=1, grid=(n, nbh),
            in_specs=[pl.BlockSpec((1, tr, C), lambda d, r, c_ref: (d, c_ref[0] * nbh + r, 0)), spec],
            out_specs=[spec, spec]),
        out_shape=[jax.ShapeDtypeStruct((n, half, C), F32), jax.ShapeDtypeStruct((n, half, C), BF16)],
        compiler_params=_cp(("parallel", "parallel")),
    )(c_idx, g, r1)


def exchange_side(ps):
    n = len(ps)

    def copies(p_refs, out_refs, send_sems, recv_sems, local_sems):
        x, y, c = _me()
        return [pltpu.make_async_remote_copy(
            src_ref=p_refs[w].at[2 * chip[0] + chip[1]], dst_ref=out_refs[w].at[j],
            send_sem=send_sems.at[3 * w + j], recv_sem=recv_sems.at[3 * w + j],
            device_id=(*chip, c), device_id_type=MESH)
            for w in range(n) for j, chip in enumerate(_other_chips(x, y))]

    def start(*refs):
        for cp in copies(*refs):
            cp.start()

    def wait(*refs):
        for cp in copies(*refs):
            cp.wait()

    return Side(ps, [jax.ShapeDtypeStruct((3,) + p.shape[1:], p.dtype) for p in ps], 3 * n, 0, start, wait)


def add_chips(p, r2, chip_idx, name):
    n, H, C = p.shape
    tr = _rows_block(H, C)

    def kern(s_ref, p_ref, r_ref, o_ref):
        o_ref[...] = ((p_ref[0] + r_ref[0].astype(F32)) + r_ref[1].astype(F32)) + r_ref[2].astype(F32)

    return pl.pallas_call(
        kern, name=name,
        grid_spec=pltpu.PrefetchScalarGridSpec(
            num_scalar_prefetch=1, grid=(H // tr,),
            in_specs=[pl.BlockSpec((1, tr, C), lambda r, s_ref: (s_ref[0], r, 0)),
                      pl.BlockSpec((3, tr, C), lambda r, s_ref: (0, r, 0))],
            out_specs=pl.BlockSpec((tr, C), lambda r, s_ref: (r, 0))),
        out_shape=jax.ShapeDtypeStruct((H, C), F32),
        compiler_params=_cp(("parallel",)),
    )(chip_idx, p, r2)


def swap_reduced(ss, name):
    n = len(ss)

    def body(*refs):
        s_refs, out_refs = refs[:n], refs[n:2 * n]
        send_sems, recv_sems = refs[2 * n:]
        x, y, c = _me()
        cps = [pltpu.make_async_remote_copy(src_ref=s_refs[w], dst_ref=out_refs[w], send_sem=send_sems.at[w],
                                            recv_sem=recv_sems.at[w], device_id=(x, y, 1 - c), device_id_type=MESH)
               for w in range(n)]
        for cp in cps:
            cp.start()
        for cp in cps:
            cp.wait()

    any_spec = pl.BlockSpec(memory_space=pl.ANY)
    return pl.pallas_call(
        body, name=name, out_shape=[jax.ShapeDtypeStruct(s.shape, s.dtype) for s in ss],
        in_specs=[any_spec] * n, out_specs=[any_spec] * n,
        scratch_shapes=[pltpu.SemaphoreType.DMA((n,)), pltpu.SemaphoreType.DMA((n,))],
    )(*ss)


BIG = (("gla_w_in", 2, (1024, GLA_IN // N_CHIPS), "lead"), ("gla_w_out", 2, (256, 1024), "row"),
       ("att_w_in", 2, (1024, 768), "col"), ("att_w_out", 2, (256, 1024), "row"),
       ("ff_w1", 4, (1024, 1024), "col"), ("ff_w2", 4, (1024, 1024), "row"))
FLAVOUR = {n: f for n, _, _, f in BIG}


def layer_weights(i):
    mixer = "gla" if i % 2 == 0 else "att"
    return (("in", mixer + "_w_in", i // 2), ("out", mixer + "_w_out", i // 2), ("w1", "ff_w1", i), ("w2", "ff_w2", i))


class Comm:
    def __init__(self, weights, core, chip):
        self.weights, self.core, self.chip = weights, core, chip
        self.c_idx = jnp.reshape(core, (1,)).astype(jnp.int32)
        self.chip_idx = jnp.reshape(chip, (1,)).astype(jnp.int32)
        self.reduced = {}

    def gather(self, items):
        shards = [self.weights[n][l:l + 1].astype(BF16) for _, n, l in items]
        return gather_side(shards, [FLAVOUR[n] for _, n, _ in items])

    def full_weights(self, items, gathered):
        W = {}
        for (role, n, _), w in zip(items, gathered):
            if n == "gla_w_in":
                w = jnp.pad(w.transpose(1, 2, 0, 3).reshape(1, D_MODEL, GLA_IN), ((0, 0), (0, 0), (0, GLA_IN_PAD - GLA_IN)))
            W[role] = (w, 0)
        return W

    def gather_now(self, items, name):
        return self.full_weights(items, run_side(self.gather(items), name))

    def swap(self, items):
        return swap_side([g for _, _, g in items])

    def reduce_begin(self, tag, items, swapped):
        ps = [add_half(g, r, self.c_idx, f"rs_add2_{tag}_{w}") for w, ((_, _, g), r) in enumerate(zip(items, swapped))]
        return tag, [(n, l) for n, l, _ in items], ps

    def exchange(self, pending):
        return exchange_side([pb for _, pb in pending[2]])

    def reduce_mid(self, pending, landed):
        tag, keys, ps = pending
        for w, (key, (p, _), r) in enumerate(zip(keys, ps, landed)):
            self.reduced[key] = add_chips(p, r, self.chip_idx, f"rs_add4_{tag}_{w}")

    def reduce_tail(self, tag, items):
        pending = self.reduce_begin(tag, items, run_side(self.swap(items), f"rs_swap_{tag}"))
        self.reduce_mid(pending, run_side(self.exchange(pending), f"rs_xchg_{tag}"))

    def reduce_end(self):
        keys = [(n, l) for n, L, _, _ in BIG for l in range(L)]
        mine = [self.reduced[k] for k in keys]
        theirs = swap_reduced(mine, "rs_join")
        low = self.core == 0
        full = {k: jnp.concatenate([jnp.where(low, m, t), jnp.where(low, t, m)], axis=0)
                for k, m, t in zip(keys, mine, theirs)}
        return {n: jnp.stack([full[(n, l)] for l in range(L)]) for n, L, _, _ in BIG}


def local_step(x, target, mods, comm, small):
    S, D = x.shape
    row = lambda v: v.reshape(1, -1)
    saved = []
    tiles = [att_bias_tiles(small["att_rel_bias"][j], f"att_tiles_{j}").reshape(ATT_HEADS, ATT_CLASSES, ATT_TQ, ATT_KW)
             for j in range(2)]
    wgk_p = [jnp.pad(small["gla_w_gk2"][j], ((0, 128 - GLA_RANK), (0, 0))).astype(BF16) for j in range(2)]

    u1 = modulate(x, row(mods[0, 1]), row(mods[0, 0]), "mod_first")
    Ws = [dict() for _ in range(DEPTH)]
    items0 = layer_weights(0)
    Ws[0].update(comm.gather_now(items0[:1], "gather_w0"))
    for i in range(DEPTH):
        j = i // 2
        W = Ws[i]
        sh1, sc1, g1, sh2, sc2, g2 = (row(mods[i, k]) for k in range(6))
        nxt = min(i + 1, DEPTH - 1)
        more = i + 1 < DEPTH
        nxt_items = layer_weights(nxt)
        in_items = list(items0[1:3]) if i == 0 else []
        mix_items = (list(items0[3:]) if i == 0 else []) + (list(nxt_items[:2]) if more else [])
        up_items = list(nxt_items[2:3]) if more else []
        down_items = list(nxt_items[3:]) if more else []

        def hosted(items):
            return comm.gather(items) if items else None

        def landed_weights(items, landed):
            for k, it in enumerate(items):
                layer = 0 if it in items0 and i == 0 else nxt
                Ws[layer].update(comm.full_weights([it], landed[k:k + 1]))

        side = hosted(in_items)
        if i % 2 == 0:
            proj = mm_plain(u1, *W["in"], f"gla_in_{i}", side=side)
        else:
            proj = mm_plain(u1, *W["in"], f"att_in_{i}", mode="bf16", bias=row(small["att_b_in"][j]), side=side)
        proj, landed = proj if side is not None else (proj, [])
        landed_weights(in_items, landed)
        if i % 2 == 0:
            (zmix, states), landed = gla_fwd(proj, wgk_p[j], row(small["gla_b_gk"][j]), row(small["gla_g_norm"][j]),
                                             f"gla_fwd_{i}", hosted(mix_items))
        else:
            (zmix,), landed = attn_fwd(proj, tiles[j], f"att_fwd_{i}", hosted(mix_items))
            states = None
        landed_weights(mix_items, landed)
        (y1, x_mid, u2), _ = mm_down_ln(zmix, *W["out"], x, 1.0 + g1, row(small["ln_g"][i, 0]),
                                        row(small["ln_b"][i, 0]), sc2, sh2, f"mix_out_{i}")
        side = hosted(up_items)
        act = mm_plain(u2, *W["w1"], f"ff_up_{i}", mode="mlp_up", side=side)
        act, landed = act if side is not None else (act, [])
        landed_weights(up_items, landed)
        (y2, x_out, u_next), landed = mm_down_ln(act, *W["w2"], x_mid, 1.0 + g2, row(small["ln_g"][i, 1]),
                                                 row(small["ln_b"][i, 1]), row(mods[nxt, 1]), row(mods[nxt, 0]),
                                                 f"ff_out_{i}", side=hosted(down_items))
        landed_weights(down_items, landed)
        saved.append(dict(x_in=x, u1=u1, proj=proj, zmix=zmix, states=states, y1=y1, x_mid=x_mid, u2=u2,
                          act=act, y2=y2))
        x, u1 = x_out, u_next

    dx, sq = loss_head(x, target, "loss_head")

    g_small = dict(ln_g=[None] * DEPTH, ln_b=[None] * DEPTH, gla_w_gk2=[None] * 2, gla_b_gk=[None] * 2,
                   gla_g_norm=[None] * 2, att_b_in=[None] * 2, att_rel_bias=[None] * 2)
    dmods = [None] * DEPTH
    later = []
    top = saved[DEPTH - 1]
    dz2, dy2, s_ln2 = ln_bwd(dx, top["x_mid"], top["y2"], 1.0 + row(mods[DEPTH - 1, 5]),
                             row(small["ln_g"][DEPTH - 1, 1]), "ln2_bwd_top")

    for i in reversed(range(DEPTH)):
        j = i // 2
        sv = saved[i]
        W = Ws[i]
        sh1, sc1, g1, sh2, sc2, g2 = (row(mods[i, k]) for k in range(6))
        dh = mm_plain(dy2, *W["w2"], f"ff_dn_{i}", mode="mlp_dn", nt=True, h=sv["act"])
        g_w2 = mm_w_res(sv["act"], dy2, f"ff_w2g_{i}").reshape(N_CHIPS, D_FF // N_CHIPS, D)
        g_w1 = mm_w_res(sv["u2"], dh, f"ff_w1g_{i}", chips_out=True)
        items = [("ff_w1", i, g_w1), ("ff_w2", i, g_w2)] + later
        (dz1, dy1, s_m2, s_ln1), swapped = mm_down_comb(
            dh, *W["w1"], dz2, sv["x_mid"], 1.0 + sc2, f"ff_dx_{i}",
            ln=(sv["x_in"], sv["y1"], 1.0 + g1, row(small["ln_g"][i, 0])), side=comm.swap(items))
        pending = comm.reduce_begin(i, items, swapped)
        side = comm.exchange(pending)
        mixer = "gla" if i % 2 == 0 else "att"
        below = None
        if i > 0:
            below = (saved[i - 1]["x_mid"], saved[i - 1]["y2"], 1.0 + row(mods[i - 1, 5]), row(small["ln_g"][i - 1, 1]))
        if i % 2 == 0:
            g_out = mm_w(sv["zmix"], dy1, f"gla_wog_{i}").reshape(N_CHIPS, D // N_CHIPS, D)
            dzg = mm_plain(dy1, *W["out"], f"gla_dz_{i}", nt=True)
            (dproj, dwgk, dbgk, dgn), landed = gla_bwd(sv["proj"], sv["states"], dzg, wgk_p[j],
                                                       row(small["gla_b_gk"][j]), row(small["gla_g_norm"][j]),
                                                       f"gla_bwd_{i}", side)
            g_small["gla_w_gk2"][j] = dwgk[:GLA_RANK]
            g_small["gla_b_gk"][j] = dbgk[0]
            g_small["gla_g_norm"][j] = dgn[0].reshape(GLA_HEADS, GLA_DV_HEAD)
            gwi = mm_w_res(sv["u1"], dproj, f"gla_wig_{i}")[:, :GLA_IN]
            g_in = gwi.reshape(D, N_CHIPS, GLA_IN // N_CHIPS).transpose(1, 0, 2)
            outs, _ = mm_down_comb(dproj, *W["in"], dz1, sv["x_in"], 1.0 + sc1, f"mix_dx_{i}", ln=below)
        else:
            g_out = mm_w(sv["zmix"], dy1, f"att_wog_{i}").reshape(N_CHIPS, D // N_CHIPS, D)
            do = mm_plain(dy1, *W["out"], f"att_do_{i}", mode="bf16", nt=True)
            (dqkv, dbt), landed = attn_bwd(sv["proj"], tiles[j], do, f"att_bwd_{i}", side)
            g_small["att_rel_bias"][j] = att_bias_grad(dbt.reshape(ATT_HEADS * ATT_CLASSES, ATT_TQ, ATT_KW),
                                                       f"att_bias_{i}")
            g_in = mm_w_chips3(sv["u1"], dqkv, f"att_wig_{i}")
            outs, _ = mm_down_comb(dqkv, *W["in"], dz1, sv["x_in"], 1.0 + sc1, f"mix_dx_{i}", parts=3, ln=below)
        s_m1 = outs[1] if below is None else outs[2]
        if i % 2 == 1:
            g_small["att_b_in"][j] = s_m1[2:5].reshape(3 * D)
        comm.reduce_mid(pending, landed)
        later = [(mixer + "_w_in", j, g_in), (mixer + "_w_out", j, g_out)]
        g_small["ln_g"][i] = jnp.stack([s_ln1[0], s_ln2[0]])
        g_small["ln_b"][i] = jnp.stack([s_ln1[1], s_ln2[1]])
        dmods[i] = jnp.stack([s_m1[1], s_m1[0], s_ln1[2], s_m2[1], s_m2[0], s_ln2[2]])
        if below is None:
            dx = outs[0]
        else:
            dz2, dy2, s_ln2 = outs[0], outs[1], outs[3]
    comm.reduce_tail("last", later)

    g_small = {n: jnp.stack(v) for n, v in g_small.items()}
    return sq, dx, jnp.stack(dmods), g_small


SMALL_SHARDED = (("ln_g", (4, 2, 256)), ("ln_b", (4, 2, 256)), ("gla_g_norm", (2, 4, 64)),
                 ("gla_w_gk2", (2, 16, 128)), ("att_b_in", (2, 768)))
SMALL_FULL = dict(ln_g=(4, 2, 1024), ln_b=(4, 2, 1024), gla_g_norm=(2, 4, 256), gla_w_gk2=(2, 16, 512),
                  att_b_in=(2, 3072), gla_b_gk=(2, 512), att_rel_bias=(2, 16, 257))
SMALL_GRAD_ORDER = ("ln_g", "ln_b", "gla_g_norm", "gla_w_gk2", "att_b_in", "gla_b_gk", "att_rel_bias")


def _pack_small(arrs, rows_total):
    parts = []
    for a in arrs:
        flat = a.reshape(-1)
        pad = (-flat.shape[0]) % PACK_W
        parts.append(jnp.pad(flat, (0, pad)).reshape(-1, PACK_W))
    buf = jnp.concatenate(parts, axis=0)
    return jnp.pad(buf, ((0, rows_total - buf.shape[0]), (0, 0)))


def _unpack_small(buf, shapes):
    out, r = [], 0
    for shp in shapes:
        n = 1
        for s in shp:
            n *= s
        nr = (n + PACK_W - 1) // PACK_W
        out.append(buf[..., r:r + nr, :].reshape(buf.shape[:-2] + (nr * PACK_W,))[..., :n].reshape(buf.shape[:-2] + shp))
        r += nr
    return out


def _unshard_last(g4):
    nd = g4.ndim
    perm = tuple(range(1, nd - 1)) + (0, nd - 1)
    t = g4.transpose(perm)
    return t.reshape(t.shape[:-2] + (-1,))


def _shard_last(full, s):
    n = full.shape[-1] // N_CHIPS
    return lax.dynamic_slice_in_dim(full, s * n, n, axis=full.ndim - 1)


WEIGHT_NAMES = ("w_ada", "b_ada", "ln_g", "ln_b", "gla_w_in", "gla_w_gk2", "gla_b_gk", "gla_g_norm", "gla_w_out",
                "att_w_in", "att_b_in", "att_rel_bias", "att_w_out", "ff_w1", "ff_w2")


def kernel(x, c, w_ada, b_ada, ln_g, ln_b, gla_w_in, gla_w_gk2, gla_b_gk, gla_g_norm, gla_w_out, att_w_in, att_b_in, att_rel_bias, att_w_out, ff_w1, ff_w2, loss_target, m_w_ada, m_b_ada, m_ln_g, m_ln_b, m_gla_w_in, m_gla_w_gk2, m_gla_b_gk, m_gla_g_norm, m_gla_w_out, m_att_w_in, m_att_b_in, m_att_rel_bias, m_att_w_out, m_ff_w1, m_ff_w2, v_w_ada, v_b_ada, v_ln_g, v_ln_b, v_gla_w_in, v_gla_w_gk2, v_gla_b_gk, v_gla_g_norm, v_gla_w_out, v_att_w_in, v_att_b_in, v_att_rel_bias, v_att_w_out, v_ff_w1, v_ff_w2):
    weights = dict(w_ada=w_ada, b_ada=b_ada, ln_g=ln_g, ln_b=ln_b, gla_w_in=gla_w_in, gla_w_gk2=gla_w_gk2,
                   gla_b_gk=gla_b_gk, gla_g_norm=gla_g_norm, gla_w_out=gla_w_out, att_w_in=att_w_in,
                   att_b_in=att_b_in, att_rel_bias=att_rel_bias, att_w_out=att_w_out, ff_w1=ff_w1, ff_w2=ff_w2)
    mom1 = dict(w_ada=m_w_ada, b_ada=m_b_ada, ln_g=m_ln_g, ln_b=m_ln_b, gla_w_in=m_gla_w_in, gla_w_gk2=m_gla_w_gk2,
                gla_b_gk=m_gla_b_gk, gla_g_norm=m_gla_g_norm, gla_w_out=m_gla_w_out, att_w_in=m_att_w_in,
                att_b_in=m_att_b_in, att_rel_bias=m_att_rel_bias, att_w_out=m_att_w_out, ff_w1=m_ff_w1, ff_w2=m_ff_w2)
    mom2 = dict(w_ada=v_w_ada, b_ada=v_b_ada, ln_g=v_ln_g, ln_b=v_ln_b, gla_w_in=v_gla_w_in, gla_w_gk2=v_gla_w_gk2,
                gla_b_gk=v_gla_b_gk, gla_g_norm=v_gla_g_norm, gla_w_out=v_gla_w_out, att_w_in=v_att_w_in,
                att_b_in=v_att_b_in, att_rel_bias=v_att_rel_bias, att_w_out=v_att_w_out, ff_w1=v_ff_w1, ff_w2=v_ff_w2)

    ax, ay, ac = lax.axis_index("x"), lax.axis_index("y"), lax.axis_index("c")
    chip = 2 * ax + ay
    dev = 2 * chip + ac
    S = x.shape[1]
    x2 = x.reshape(S, D_MODEL)
    t2 = loss_target.reshape(S, D_MODEL)

    comm = Comm(weights, ac, chip)

    small_rows = 16
    spack = _pack_small([c] + [weights[n] for n, _ in SMALL_SHARDED], small_rows)
    sg = all_gather8(spack, "gather_small").reshape(N_DEV, small_rows, PACK_W)
    parts = _unpack_small(sg, [(1, D_MODEL)] + [shp for _, shp in SMALL_SHARDED])
    c_all = parts[0].reshape(N_DEV, D_MODEL)
    small = {n: _unshard_last(p[0::2]) for (n, _), p in zip(SMALL_SHARDED, parts[1:])}
    small["gla_b_gk"] = gla_b_gk
    small["att_rel_bias"] = att_rel_bias

    c_act = silu_rows(jnp.pad(c_all, ((0, 128 - N_DEV), (0, 0))), "silu_c")
    wa = w_ada.astype(BF16).transpose(1, 0, 2).reshape(1, D_MODEL, DEPTH * 6 * D_MODEL // N_CHIPS)
    mods_part = mm_plain(c_act, wa, 0, "ada_fwd", tm=128)[:N_DEV]
    mg = all_gather8(mods_part, "gather_mods").reshape(N_CHIPS, 2, N_DEV, DEPTH, 6 * D_MODEL // N_CHIPS)
    mods_mine = lax.dynamic_index_in_dim(mg[:, 0], dev, axis=1, keepdims=False)
    mods = mods_mine.transpose(1, 0, 2).reshape(DEPTH, 6 * D_MODEL) + b_ada
    mods = mods.reshape(DEPTH, 6, D_MODEL)

    sq, grad_x, dmods, g_small = local_step(x2, t2, mods, comm, small)
    loss = lax.psum(0.5 * sq[0, 0] / D_MODEL, ("x", "y", "c"))

    g_shard = comm.reduce_end()

    dm_flat = dmods.reshape(DEPTH, 6 * D_MODEL)
    g_rows = 80
    gpack = _pack_small([dm_flat] + [g_small[n] for n in SMALL_GRAD_ORDER], g_rows)
    gg = all_gather8(gpack, "gather_small_grads").reshape(N_DEV, g_rows, PACK_W)
    gsum = sum_over_devices(gg, "sum_small_grads")
    sums = _unpack_small(gsum, [(DEPTH, 6 * D_MODEL)] + [SMALL_FULL[n] for n in SMALL_GRAD_ORDER])
    grads = dict(b_ada=sums[0])
    for n, full_g in zip(SMALL_GRAD_ORDER, sums[1:]):
        grads[n] = full_g if n in ("gla_b_gk", "att_rel_bias") else _shard_last(full_g, chip)
    dm_all = _unpack_small(gg, [(DEPTH, 6 * D_MODEL)])[0]
    dm_cols = _shard_last(dm_all, chip).reshape(N_DEV, DEPTH * 6 * D_MODEL // N_CHIPS)
    dm_cols = jnp.pad(dm_cols, ((0, 128 - N_DEV), (0, 0))).astype(BF16)
    gwa = mm_w(c_act, dm_cols, "ada_bwd", ts=128)
    grads["w_ada"] = gwa.reshape(D_MODEL, DEPTH, 6 * D_MODEL // N_CHIPS).transpose(1, 0, 2)
    grads.update(g_shard)

    deltas, new_m, new_v = {}, {}, {}
    for n in WEIGHT_NAMES:
        deltas[n], new_m[n], new_v[n] = adamw(weights[n], grads[n], mom1[n], mom2[n], "adamw_" + n)

    return (loss, grad_x.reshape(1, S, D_MODEL), *[grads[n] for n in WEIGHT_NAMES], *[deltas[n] for n in WEIGHT_NAMES],
            *[new_m[n] for n in WEIGHT_NAMES], *[new_v[n] for n in WEIGHT_NAMES])
```

```python
import functools

import jax
import jax.numpy as jnp
from jax import lax
from jax.experimental import pallas as pl
from jax.experimental.pallas import tpu as pltpu

F32 = jnp.float32
BF16 = jnp.bfloat16
HIGHEST = lax.Precision.HIGHEST
MESH = pl.DeviceIdType.MESH

D_MODEL = 1024
DEPTH = 4
CHUNK = 64
GLA_HEADS = 4
GLA_DK = 512
GLA_DV = 1024
GLA_DK_HEAD = 128
GLA_DV_HEAD = 256
GLA_RANK = 16
GLA_IN = 3088
GLA_IN_PAD = 3200
GLA_LR_OFF = 3072
ATT_HEADS = 16
ATT_HD = 64
LEFT_CHUNKS = 8
MAX_REL = 128
N_REL = 257
D_FF = 4096
ALPHA = (2.0 * DEPTH) ** 0.25
LN_EPS = 1e-5
RMS_EPS = 1e-6
NEG_INF = -1e30
GLA_SCALE = GLA_DK_HEAD ** -0.5
ATT_SCALE = ATT_HD ** -0.5
ADAM_LR = 0.001
ADAM_B1 = 0.9
ADAM_B2 = 0.999
ADAM_EPS = 1e-08
ADAM_WD = 0.01
ADAM_STEP = 10

ATT_TQ = 256
ATT_KW = 768
GLA_TB = 256
GLA_GROUP = 2
VMEM_LIMIT = 56 * 1024 * 1024
WHOLE_WEIGHT_BYTES = 8 * 1024 * 1024
N_CHIPS = 4
N_DEV = 8
PACK_W = 1024


def _dot(a, b):
    return jnp.dot(a, b, preferred_element_type=F32)


def _dot_nt(a, b):
    return lax.dot_general(a, b, (((1,), (1,)), ((), ())), preferred_element_type=F32)


def _dot_tn(a, b):
    return lax.dot_general(a, b, (((0,), (0,)), ((), ())), preferred_element_type=F32)


def _cp(sem, vmem=VMEM_LIMIT):
    return pltpu.CompilerParams(dimension_semantics=sem, vmem_limit_bytes=vmem)


def _row_spec(n):
    return pl.BlockSpec((1, n), lambda *_: (0, 0))


def _sigmoid(x):
    return 1.0 / (1.0 + jnp.exp(-x))


def _log_sigmoid(x):
    return jnp.minimum(x, 0.0) - jnp.log1p(jnp.exp(-jnp.abs(x)))


class Side:
    def __init__(self, ins, out_shapes, n_sems, n_local, start, wait):
        self.ins, self.out_shapes, self.n_sems, self.n_local = list(ins), list(out_shapes), n_sems, n_local
        self.start, self.wait = start, wait

    def sem_shapes(self):
        return [pltpu.SemaphoreType.DMA((self.n_sems,)), pltpu.SemaphoreType.DMA((self.n_sems,)),
                pltpu.SemaphoreType.DMA((max(self.n_local, 1),))]


def run_side(side, name):
    n_in = len(side.ins)
    n_out = len(side.out_shapes)

    def body(*refs):
        ins, outs, sems = refs[:n_in], refs[n_in:n_in + n_out], refs[n_in + n_out:]
        side.start(ins, outs, *sems)
        side.wait(ins, outs, *sems)

    any_spec = pl.BlockSpec(memory_space=pl.ANY)
    return pl.pallas_call(body, name=name, out_shape=side.out_shapes, in_specs=[any_spec] * n_in,
                          out_specs=[any_spec] * n_out, scratch_shapes=side.sem_shapes())(*side.ins)


def hosted_call(main, side, *, name, grid, in_specs, out_specs, out_shape, scratch_shapes, dims, args):
    if side is None:
        outs = pl.pallas_call(main, name=name, grid=grid, in_specs=in_specs, out_specs=out_specs,
                              out_shape=out_shape, scratch_shapes=scratch_shapes, compiler_params=_cp(dims))(*args)
        return list(outs), []
    n_mi, n_mo, n_ms = len(in_specs), len(out_specs), len(scratch_shapes)
    n_si, n_so = len(side.ins), len(side.out_shapes)

    def kern(*refs):
        mi, si = refs[:n_mi], refs[n_mi:n_mi + n_si]
        o0 = n_mi + n_si
        mo, so = refs[o0:o0 + n_mo], refs[o0 + n_mo:o0 + n_mo + n_so]
        s0 = o0 + n_mo + n_so
        ms, sems = refs[s0:s0 + n_ms], refs[s0 + n_ms:]
        ids = [pl.program_id(d) for d in range(len(grid))]
        first = functools.reduce(jnp.logical_and, [i == 0 for i in ids])
        last = functools.reduce(jnp.logical_and, [i == g - 1 for i, g in zip(ids, grid)])

        @pl.when(first)
        def _():
            side.start(si, so, *sems)
        main(*mi, *mo, *ms)

        @pl.when(last)
        def _():
            side.wait(si, so, *sems)

    any_spec = pl.BlockSpec(memory_space=pl.ANY)
    outs = pl.pallas_call(
        kern, name=name, grid=grid, in_specs=list(in_specs) + [any_spec] * n_si,
        out_specs=list(out_specs) + [any_spec] * n_so, out_shape=list(out_shape) + side.out_shapes,
        scratch_shapes=list(scratch_shapes) + side.sem_shapes(),
        compiler_params=_cp(("arbitrary",) * len(grid)))(*args, *side.ins)
    return list(outs[:n_mo]), list(outs[n_mo:])


def modulate(x, sc, sh, name):
    S, D = x.shape
    tm = min(512, S)

    def kern(x_ref, sc_ref, sh_ref, u_ref):
        u_ref[...] = (x_ref[...] * (1.0 + sc_ref[...]) + sh_ref[...]).astype(BF16)

    return pl.pallas_call(
        kern, name=name, grid=(S // tm,),
        in_specs=[pl.BlockSpec((tm, D), lambda i: (i, 0)), _row_spec(D), _row_spec(D)],
        out_specs=pl.BlockSpec((tm, D), lambda i: (i, 0)),
        out_shape=jax.ShapeDtypeStruct((S, D), BF16),
        compiler_params=_cp(("parallel",)),
    )(x, sc, sh)


def loss_head(x, t, name):
    S, D = x.shape
    tm = min(512, S)

    def kern(x_ref, t_ref, dx_ref, l_ref):
        @pl.when(pl.program_id(0) == 0)
        def _():
            l_ref[...] = jnp.zeros_like(l_ref)
        e = x_ref[...] - t_ref[...]
        dx_ref[...] = e * (1.0 / D)
        l_ref[...] += jnp.sum(e * e)

    return pl.pallas_call(
        kern, name=name, grid=(S // tm,),
        in_specs=[pl.BlockSpec((tm, D), lambda i: (i, 0)), pl.BlockSpec((tm, D), lambda i: (i, 0))],
        out_specs=[pl.BlockSpec((tm, D), lambda i: (i, 0)), pl.BlockSpec((8, 128), lambda i: (0, 0))],
        out_shape=[jax.ShapeDtypeStruct((S, D), F32), jax.ShapeDtypeStruct((8, 128), F32)],
        compiler_params=_cp(("arbitrary",)),
    )(x, t)


def silu_rows(c_all, name):
    def kern(c_ref, o_ref):
        c = c_ref[...]
        o_ref[...] = (c * _sigmoid(c)).astype(BF16)

    return pl.pallas_call(kern, name=name, out_shape=jax.ShapeDtypeStruct(c_all.shape, BF16))(c_all)


def sum_over_devices(g, name):
    n, R, C = g.shape

    def kern(g_ref, o_ref):
        acc = g_ref[0]
        for d in range(1, n):
            acc = acc + g_ref[d]
        o_ref[...] = acc

    return pl.pallas_call(kern, name=name, out_shape=jax.ShapeDtypeStruct((R, C), F32))(g)


def _rows_block(R, C, budget=1 << 20):
    if R * C * 4 <= budget or R % 8:
        return R
    tr = max(8, (budget // (C * 4)) // 8 * 8)
    while R % tr:
        tr -= 8
    return tr


def adamw(w, g, m, v, name):
    shape = w.shape
    C = shape[-1]
    R = w.size // C
    w2, g2, m2, v2 = (t.reshape(R, C) for t in (w, g, m, v))
    tr = _rows_block(R, C)
    c1 = 1.0 - ADAM_B1 ** ADAM_STEP
    c2 = 1.0 - ADAM_B2 ** ADAM_STEP

    def kern(w_ref, g_ref, m_ref, v_ref, d_ref, nm_ref, nv_ref):
        gg = g_ref[...]
        nm = ADAM_B1 * m_ref[...] + (1.0 - ADAM_B1) * gg
        nv = ADAM_B2 * v_ref[...] + (1.0 - ADAM_B2) * (gg * gg)
        m_hat = nm / c1
        v_hat = nv / c2
        d_ref[...] = -ADAM_LR * (m_hat / (jnp.sqrt(v_hat) + ADAM_EPS) + ADAM_WD * w_ref[...])
        nm_ref[...] = nm
        nv_ref[...] = nv

    spec = pl.BlockSpec((tr, C), lambda i: (i, 0))
    outs = pl.pallas_call(
        kern, name=name, grid=(R // tr,),
        in_specs=[spec] * 4, out_specs=[spec] * 3,
        out_shape=[jax.ShapeDtypeStruct((R, C), F32)] * 3,
        compiler_params=_cp(("parallel",)),
    )(w2, g2, m2, v2)
    return tuple(o.reshape(shape) for o in outs)


def _tn_for(N):
    for tn in (1024, 768, 640, 512, 384, 256, 128):
        if N % tn == 0:
            return tn
    return N


def mm_plain(a, b3, layer, name, *, mode="f32", nt=False, bias=None, h=None, tm=1024, side=None):
    M, K = a.shape
    N = b3.shape[1] if nt else b3.shape[2]
    if K * N * 2 <= WHOLE_WEIGHT_BYTES:
        tn, tm = N, min(tm, 512)
    else:
        tn = _tn_for(N)
    tm = min(tm, M)
    a_spec = pl.BlockSpec((tm, K), lambda j, i: (i, 0))
    if nt:
        b_spec = pl.BlockSpec((None, tn, K), lambda j, i: (layer, j, 0))
    else:
        b_spec = pl.BlockSpec((None, K, tn), lambda j, i: (layer, 0, j))
    o_spec = pl.BlockSpec((tm, tn), lambda j, i: (i, j))
    ins, in_specs = [a, b3], [a_spec, b_spec]
    if bias is not None:
        ins.append(bias)
        in_specs.append(pl.BlockSpec((1, tn), lambda j, i: (0, j)))
    if mode == "mlp_dn":
        ins.append(h)
        in_specs.append(o_spec)
    elif mode not in ("f32", "bf16", "mlp_up"):
        raise ValueError(mode)
    odt = F32 if mode == "f32" else BF16

    def kern(a_ref, b_ref, *rest):
        rest = list(rest)
        bias_ref = rest.pop(0) if bias is not None else None
        h_ref = rest.pop(0) if mode == "mlp_dn" else None
        o_ref = rest.pop(0)
        if nt:
            bt_ref = rest.pop(0)

            @pl.when(pl.program_id(1) == 0)
            def _():
                bt_ref[...] = b_ref[...].T
            acc = _dot(a_ref[...], bt_ref[...])
        else:
            acc = _dot(a_ref[...], b_ref[...])
        if bias_ref is not None:
            acc = acc + bias_ref[...]
        if mode == "mlp_up":
            r = jnp.maximum(acc, 0.0)
            acc = r * r
        elif mode == "mlp_dn":
            acc = acc * (2.0 * jnp.sqrt(h_ref[...].astype(F32)))
        o_ref[...] = acc.astype(odt)

    outs, landed = hosted_call(
        kern, side, name=name, grid=(N // tn, M // tm), in_specs=in_specs, out_specs=[o_spec],
        out_shape=[jax.ShapeDtypeStruct((M, N), odt)],
        scratch_shapes=[pltpu.VMEM((K, tn), BF16)] if nt else [], dims=("parallel", "arbitrary"), args=tuple(ins))
    return outs[0] if side is None else (outs[0], landed)


def mm_down_ln(a, b3, layer, x_in, gate1p, ln_g, ln_b, sc_next, sh_next, name, *, side=None, tm=256):
    M, K = a.shape
    D = b3.shape[2]
    tm = min(tm, M)

    def kern(a_ref, b_ref, x_ref, gp_ref, lg_ref, lb_ref, sc_ref, sh_ref, y_ref, xo_ref, u_ref):
        y = _dot(a_ref[...], b_ref[...])
        y_ref[...] = y.astype(BF16)
        z = ALPHA * x_ref[...] + gp_ref[...] * y
        mu = jnp.mean(z, axis=-1, keepdims=True)
        zc = z - mu
        var = jnp.mean(zc * zc, axis=-1, keepdims=True)
        xo = (zc * lax.rsqrt(var + LN_EPS)) * lg_ref[...] + lb_ref[...]
        xo_ref[...] = xo
        u_ref[...] = (xo * (1.0 + sc_ref[...]) + sh_ref[...]).astype(BF16)

    tile = pl.BlockSpec((tm, D), lambda i: (i, 0))
    outs, landed = hosted_call(
        kern, side, name=name, grid=(M // tm,),
        in_specs=[pl.BlockSpec((tm, K), lambda i: (i, 0)), pl.BlockSpec((None, K, D), lambda i: (layer, 0, 0)), tile]
        + [_row_spec(D)] * 5,
        out_specs=[tile, tile, tile],
        out_shape=[jax.ShapeDtypeStruct((M, D), BF16), jax.ShapeDtypeStruct((M, D), F32),
                   jax.ShapeDtypeStruct((M, D), BF16)],
        scratch_shapes=[], dims=("parallel",), args=(a, b3, x_in, gate1p, ln_g, ln_b, sc_next, sh_next))
    return tuple(outs), landed


def mm_down_comb(a, b3, layer, dz, x_in, sc1p, name, *, parts=1, ln=None, side=None, tm=256):
    D, K = b3.shape[1], b3.shape[2]
    M = a.shape[-2]
    kp = K // parts
    tm = min(tm, M)
    n_ln = 0 if ln is None else 4

    def kern(*refs):
        a_refs = refs[:parts]
        b_ref, dz_ref, x_ref, sp_ref = refs[parts:parts + 4]
        ln_refs = refs[parts + 4:parts + 4 + n_ln]
        outs = refs[parts + 4 + n_ln:]

        @pl.when(pl.program_id(0) == 0)
        def _():
            for o in outs:
                if o.shape[0] == 8:
                    o[...] = jnp.zeros_like(o)
        if parts == 1:
            du = _dot_nt(a_refs[0][...], b_ref[...])
        else:
            du = _dot_nt(a_refs[0][...], b_ref[:, 0:kp])
            for p in range(1, parts):
                du = du + _dot_nt(a_refs[p][...], b_ref[:, p * kp:(p + 1) * kp])
        dx = ALPHA * dz_ref[...] + du * sp_ref[...]
        if ln is None:
            dx_ref, s_ref = outs
            dx_ref[...] = dx
        else:
            dzl_ref, dyl_ref, s_ref, sl_ref = outs
            _ln_bwd_tile(dx, *ln_refs, dzl_ref, dyl_ref, sl_ref)
        s_ref[0:1, :] += jnp.sum(du * x_ref[...], axis=0, keepdims=True)
        s_ref[1:2, :] += jnp.sum(du, axis=0, keepdims=True)
        if parts > 1:
            for p in range(parts):
                s_ref[2 + p:3 + p, :] += jnp.sum(a_refs[p][...].astype(F32), axis=0, keepdims=True)

    tile = pl.BlockSpec((tm, D), lambda i: (i, 0))
    sums = pl.BlockSpec((8, D), lambda i: (0, 0))
    if parts == 1:
        a_ins, a_specs = [a], [pl.BlockSpec((tm, K), lambda i: (i, 0))]
    else:
        assert kp == D and parts <= 6
        a_ins = [a] * parts
        a_specs = [pl.BlockSpec((None, tm, kp), functools.partial(lambda i, p: (p, i, 0), p=p)) for p in range(parts)]
    in_specs = a_specs + [pl.BlockSpec((None, D, K), lambda i: (layer, 0, 0)), tile, tile, _row_spec(D)]
    args = a_ins + [b3, dz, x_in, sc1p]
    if ln is None:
        out_specs = [tile, sums]
        out_shape = [jax.ShapeDtypeStruct((M, D), F32), jax.ShapeDtypeStruct((8, D), F32)]
    else:
        in_specs += [tile, tile, _row_spec(D), _row_spec(D)]
        args += list(ln)
        out_specs = [tile, tile, sums, sums]
        out_shape = [jax.ShapeDtypeStruct((M, D), F32), jax.ShapeDtypeStruct((M, D), BF16),
                     jax.ShapeDtypeStruct((8, D), F32), jax.ShapeDtypeStruct((8, D), F32)]
    return hosted_call(kern, side, name=name, grid=(M // tm,), in_specs=in_specs, out_specs=out_specs,
                       out_shape=out_shape, scratch_shapes=[], dims=("arbitrary",), args=tuple(args))


def mm_w(a, b, name, *, ts=2048, tk=512, chips_out=False, b_parts=1, tn=None):
    S, K = a.shape
    npart = b.shape[-1]
    N = npart * b_parts
    ts = min(ts, S)
    tk = min(tk, K)
    n_chip = N // N_CHIPS
    if tn is None:
        tn = _tn_for(n_chip if chips_out else npart)
    assert npart % tn == 0 and (not chips_out or n_chip % tn == 0)

    def kern(a_ref, b_ref, o_ref):
        @pl.when(pl.program_id(2) == 0)
        def _():
            o_ref[...] = jnp.zeros_like(o_ref)
        o_ref[...] += _dot_tn(a_ref[...], b_ref[...])

    if b_parts == 1:
        b_spec = pl.BlockSpec((ts, tn), lambda k, n, s: (s, n))
    else:
        per = npart // tn
        b_spec = pl.BlockSpec((None, ts, tn), lambda k, n, s: (n // per, s, n % per))
    if chips_out:
        per_chip = n_chip // tn
        o_spec = pl.BlockSpec((None, tk, tn), lambda k, n, s: (n // per_chip, k, n % per_chip))
        out_shape = jax.ShapeDtypeStruct((N_CHIPS, K, n_chip), F32)
    else:
        o_spec = pl.BlockSpec((tk, tn), lambda k, n, s: (k, n))
        out_shape = jax.ShapeDtypeStruct((K, N), F32)
    return pl.pallas_call(
        kern, name=name, grid=(K // tk, N // tn, S // ts),
        in_specs=[pl.BlockSpec((ts, tk), lambda k, n, s: (s, k)), b_spec],
        out_specs=o_spec, out_shape=out_shape,
        compiler_params=_cp(("parallel", "parallel", "arbitrary")),
    )(a, b)


def mm_w_chips3(a, b3, name, *, ts=512):
    S, K = a.shape
    P = b3.shape[2]
    n_chip = 3 * P // N_CHIPS
    ts = min(ts, S)
    pieces = []
    for chip in range(N_CHIPS):
        lo, hi = chip * n_chip, (chip + 1) * n_chip
        while lo < hi:
            part = lo // P
            w = min(hi, (part + 1) * P) - lo
            pieces.append((chip, lo - chip * n_chip, part, lo - part * P, w))
            lo += w

    def kern(a_ref, b_ref, o_ref):
        @pl.when(pl.program_id(0) == 0)
        def _():
            o_ref[...] = jnp.zeros_like(o_ref)
        at = a_ref[...].T
        for chip, oc, part, pc, w in pieces:
            o_ref[chip, :, oc:oc + w] += _dot(at, b_ref[part, :, pc:pc + w])

    return pl.pallas_call(
        kern, name=name, grid=(S // ts,),
        in_specs=[pl.BlockSpec((ts, K), lambda s: (s, 0)), pl.BlockSpec((3, ts, P), lambda s: (0, s, 0))],
        out_specs=pl.BlockSpec((N_CHIPS, K, n_chip), lambda s: (0, 0, 0)),
        out_shape=jax.ShapeDtypeStruct((N_CHIPS, K, n_chip), F32),
        compiler_params=_cp(("arbitrary",)),
    )(a, b3)


def mm_w_res(a, b, name, *, chips_out=False, ts=512):
    S, K = a.shape
    N = b.shape[1]
    ts = min(ts, S)
    n_chip = N // N_CHIPS

    def kern(a_ref, b_ref, o_ref):
        @pl.when(pl.program_id(0) == 0)
        def _():
            o_ref[...] = jnp.zeros_like(o_ref)
        at = a_ref[...].T
        if chips_out:
            for chip in range(N_CHIPS):
                o_ref[chip] += _dot(at, b_ref[:, chip * n_chip:(chip + 1) * n_chip])
        else:
            o_ref[...] += _dot(at, b_ref[...])

    o_shape = (N_CHIPS, K, n_chip) if chips_out else (K, N)
    return pl.pallas_call(
        kern, name=name, grid=(S // ts,),
        in_specs=[pl.BlockSpec((ts, K), lambda s: (s, 0)), pl.BlockSpec((ts, N), lambda s: (s, 0))],
        out_specs=pl.BlockSpec(o_shape, lambda s: (0,) * len(o_shape)),
        out_shape=jax.ShapeDtypeStruct(o_shape, F32),
        compiler_params=_cp(("arbitrary",)),
    )(a, b)


def mm_f32(a, b, name):
    def kern(a_ref, b_ref, o_ref):
        o_ref[...] = jnp.dot(a_ref[...], b_ref[...], precision=HIGHEST, preferred_element_type=F32)

    return pl.pallas_call(kern, name=name, out_shape=jax.ShapeDtypeStruct((a.shape[0], b.shape[1]), F32),
                          compiler_params=pltpu.CompilerParams(vmem_limit_bytes=VMEM_LIMIT))(a, b)


def _ln_bwd_tile(dxo_t, x_ref, y_ref, gp_ref, lg_ref, dz_ref, dy_ref, s_ref):
    yv = y_ref[...].astype(F32)
    z = ALPHA * x_ref[...] + gp_ref[...] * yv
    mu = jnp.mean(z, axis=-1, keepdims=True)
    zc = z - mu
    var = jnp.mean(zc * zc, axis=-1, keepdims=True)
    rstd = lax.rsqrt(var + LN_EPS)
    xhat = zc * rstd
    dxh = dxo_t * lg_ref[...]
    dz = rstd * (dxh - jnp.mean(dxh, axis=-1, keepdims=True)
                 - xhat * jnp.mean(dxh * xhat, axis=-1, keepdims=True))
    dz_ref[...] = dz
    dy_ref[...] = (gp_ref[...] * dz).astype(BF16)
    s_ref[0:1, :] += jnp.sum(dxo_t * xhat, axis=0, keepdims=True)
    s_ref[1:2, :] += jnp.sum(dxo_t, axis=0, keepdims=True)
    s_ref[2:3, :] += jnp.sum(dz * yv, axis=0, keepdims=True)


def ln_bwd(dxo, x_in, y, gate1p, ln_g, name, *, tm=256):
    S, D = dxo.shape
    tm = min(tm, S)

    def kern(dxo_ref, x_ref, y_ref, gp_ref, lg_ref, dz_ref, dy_ref, s_ref):
        @pl.when(pl.program_id(0) == 0)
        def _():
            s_ref[...] = jnp.zeros_like(s_ref)
        _ln_bwd_tile(dxo_ref[...], x_ref, y_ref, gp_ref, lg_ref, dz_ref, dy_ref, s_ref)

    tile = pl.BlockSpec((tm, D), lambda i: (i, 0))
    return pl.pallas_call(
        kern, name=name, grid=(S // tm,),
        in_specs=[tile, tile, tile, _row_spec(D), _row_spec(D)],
        out_specs=[tile, tile, pl.BlockSpec((8, D), lambda i: (0, 0))],
        out_shape=[jax.ShapeDtypeStruct((S, D), F32), jax.ShapeDtypeStruct((S, D), BF16),
                   jax.ShapeDtypeStruct((8, D), F32)],
        compiler_params=_cp(("arbitrary",)),
    )(dxo, x_in, y, gate1p, ln_g)


def _tri64():
    r = lax.broadcasted_iota(jnp.int32, (CHUNK, CHUNK), 0)
    c = lax.broadcasted_iota(jnp.int32, (CHUNK, CHUNK), 1)
    return r >= c


def _gla_chunk_common(proj_ref, rows, b, h):
    kc = slice(h * GLA_DK_HEAD, (h + 1) * GLA_DK_HEAD)
    bh = b[:, kc]
    ep = jnp.exp(bh)
    en = jnp.exp(-bh)
    bl = bh[CHUNK - 1:CHUNK, :]
    ee = jnp.exp(bl - bh)
    dec = jnp.exp(bl)
    q = proj_ref[rows, h * GLA_DK_HEAD:(h + 1) * GLA_DK_HEAD] * GLA_SCALE
    k = proj_ref[rows, GLA_DK + h * GLA_DK_HEAD:GLA_DK + (h + 1) * GLA_DK_HEAD]
    v = proj_ref[rows, 2 * GLA_DK + h * GLA_DV_HEAD:2 * GLA_DK + (h + 1) * GLA_DV_HEAD]
    g = proj_ref[rows, 2 * GLA_DK + GLA_DV + h * GLA_DV_HEAD:2 * GLA_DK + GLA_DV + (h + 1) * GLA_DV_HEAD]
    return ep, en, ee, dec, q, k, v, g


def gla_fwd(proj, wgk_p, bgk, gnorm, name, side=None):
    S = proj.shape[0]
    TB = min(GLA_TB, S)
    ncb = TB // CHUNK

    def kern(proj_ref, wgk_ref, bgk_ref, gn_ref, zg_ref, st_ref, state_scr, la_scr):
        @pl.when(pl.program_id(0) == 0)
        def _():
            state_scr[...] = jnp.zeros_like(state_scr)
        lr = proj_ref[:, GLA_LR_OFF:GLA_IN_PAD].astype(BF16)
        gk = _dot(lr, wgk_ref[...]) + bgk_ref[...]
        la_scr[...] = _log_sigmoid(gk) * (1.0 / 16.0)
        lower = _tri64()
        tri = lower.astype(F32)

        def group(gi, carry):
            rows = [pl.ds(pl.multiple_of((gi * GLA_GROUP + g) * CHUNK, CHUNK), CHUNK) for g in range(GLA_GROUP)]
            b = [jnp.dot(tri, la_scr[r, :], precision=HIGHEST, preferred_element_type=F32) for r in rows]
            P = [(g, h) for g in range(GLA_GROUP) for h in range(GLA_HEADS)]
            cm = {p: _gla_chunk_common(proj_ref, rows[p[0]], b[p[0]], p[1]) for p in P}
            qf = {p: (cm[p][4] * cm[p][0]).astype(BF16) for p in P}
            kn = {p: (cm[p][5] * cm[p][1]).astype(BF16) for p in P}
            qn = {p: (cm[p][4] * cm[p][1]).astype(BF16) for p in P}
            kp = {p: (cm[p][5] * cm[p][0]).astype(BF16) for p in P}
            ke = {p: (cm[p][5] * cm[p][2]).astype(BF16) for p in P}
            vb = {p: cm[p][6].astype(BF16) for p in P}
            a_f = {p: _dot_nt(qf[p], kn[p]) for p in P}
            a_b = {p: _dot_nt(qn[p], kp[p]) for p in P}
            upd = {p: _dot_tn(vb[p], ke[p]) for p in P}
            st = {(0, h): state_scr[h] for h in range(GLA_HEADS)}
            for g in range(GLA_GROUP):
                for h in range(GLA_HEADS):
                    st[(g + 1, h)] = st[(g, h)] * cm[(g, h)][3] + upd[(g, h)]
            o_st = {p: _dot_nt(qf[p], st[p].astype(BF16)) for p in P}
            amat = {p: jnp.where(lower, a_f[p], a_b[p]).astype(BF16) for p in P}
            o = {p: _dot(amat[p], vb[p]) + o_st[p] for p in P}
            for g, h in P:
                st_ref[gi * GLA_GROUP + g, h] = st[(g, h)]
            for h in range(GLA_HEADS):
                state_scr[h] = st[(GLA_GROUP, h)]
            for g, h in P:
                gate = cm[(g, h)][7]
                vc = slice(h * GLA_DV_HEAD, (h + 1) * GLA_DV_HEAD)
                r = lax.rsqrt(jnp.mean(o[(g, h)] * o[(g, h)], axis=-1, keepdims=True) + RMS_EPS)
                on = (o[(g, h)] * r) * gn_ref[:, vc]
                zg_ref[rows[g], vc] = (on * (gate * _sigmoid(gate))).astype(BF16)
            return carry

        lax.fori_loop(0, ncb // GLA_GROUP, group, 0)

    return hosted_call(
        kern, side, name=name, grid=(S // TB,),
        in_specs=[pl.BlockSpec((TB, GLA_IN_PAD), lambda i: (i, 0)),
                  pl.BlockSpec((128, GLA_DK), lambda i: (0, 0)), _row_spec(GLA_DK), _row_spec(GLA_DV)],
        out_specs=[pl.BlockSpec((TB, GLA_DV), lambda i: (i, 0)),
                   pl.BlockSpec((ncb, GLA_HEADS, GLA_DV_HEAD, GLA_DK_HEAD), lambda i: (i, 0, 0, 0))],
        out_shape=[jax.ShapeDtypeStruct((S, GLA_DV), BF16),
                   jax.ShapeDtypeStruct((S // CHUNK, GLA_HEADS, GLA_DV_HEAD, GLA_DK_HEAD), F32)],
        scratch_shapes=[pltpu.VMEM((GLA_HEADS, GLA_DV_HEAD, GLA_DK_HEAD), F32), pltpu.VMEM((TB, GLA_DK), F32)],
        dims=("arbitrary",), args=(proj, wgk_p, bgk, gnorm))


def gla_bwd(proj, states, dzg, wgk_p, bgk, gnorm, name, side=None):
    S = proj.shape[0]
    TB = min(GLA_TB, S)
    ncb = TB // CHUNK
    nb = S // TB

    def kern(proj_ref, st_ref, dzg_ref, wgk_ref, bgk_ref, gn_ref,
             dproj_ref, dwgk_ref, dbgk_ref, dgn_ref, dstate_scr, la_scr, gk_scr, dgk_scr):
        @pl.when(pl.program_id(0) == 0)
        def _():
            dstate_scr[...] = jnp.zeros_like(dstate_scr)
            dwgk_ref[...] = jnp.zeros_like(dwgk_ref)
            dbgk_ref[...] = jnp.zeros_like(dbgk_ref)
            dgn_ref[...] = jnp.zeros_like(dgn_ref)
        lr = proj_ref[:, GLA_LR_OFF:GLA_IN_PAD].astype(BF16)
        gk = _dot(lr, wgk_ref[...]) + bgk_ref[...]
        gk_scr[...] = gk
        la_scr[...] = _log_sigmoid(gk) * (1.0 / 16.0)
        lower = _tri64()
        tri = lower.astype(F32)
        r_i = lax.broadcasted_iota(jnp.int32, (CHUNK, CHUNK), 0)
        c_i = lax.broadcasted_iota(jnp.int32, (CHUNK, CHUNK), 1)
        triu = (c_i >= r_i).astype(F32)
        last_row = lax.broadcasted_iota(jnp.int32, (CHUNK, GLA_DK_HEAD), 0) == CHUNK - 1

        def group(gi, carry):
            cs = [ncb - 1 - (gi * GLA_GROUP + g) for g in range(GLA_GROUP)]
            rows = [pl.ds(pl.multiple_of(c * CHUNK, CHUNK), CHUNK) for c in cs]
            b = [jnp.dot(tri, la_scr[r, :], precision=HIGHEST, preferred_element_type=F32) for r in rows]
            P = [(g, h) for g in range(GLA_GROUP) for h in range(GLA_HEADS)]
            kcs = [slice(h * GLA_DK_HEAD, (h + 1) * GLA_DK_HEAD) for h in range(GLA_HEADS)]
            vcs = [slice(h * GLA_DV_HEAD, (h + 1) * GLA_DV_HEAD) for h in range(GLA_HEADS)]
            cm = {p: _gla_chunk_common(proj_ref, rows[p[0]], b[p[0]], p[1]) for p in P}
            ep, en, ee, dec = ({p: cm[p][i] for p in P} for i in range(4))
            qf = {p: cm[p][4] * cm[p][0] for p in P}
            kn = {p: cm[p][5] * cm[p][1] for p in P}
            qn = {p: cm[p][4] * cm[p][1] for p in P}
            kp = {p: cm[p][5] * cm[p][0] for p in P}
            ke = {p: cm[p][5] * cm[p][2] for p in P}
            qf_b, kn_b, qn_b, kp_b, ke_b = ({p: t[p].astype(BF16) for p in P} for t in (qf, kn, qn, kp, ke))
            vb = {p: cm[p][6].astype(BF16) for p in P}
            st = {p: st_ref[cs[p[0]], p[1]] for p in P}
            st_b = {p: st[p].astype(BF16) for p in P}
            a_f = {p: _dot_nt(qf_b[p], kn_b[p]) for p in P}
            a_b = {p: _dot_nt(qn_b[p], kp_b[p]) for p in P}
            o_st = {p: _dot_nt(qf_b[p], st_b[p]) for p in P}
            amat = {p: jnp.where(lower, a_f[p], a_b[p]).astype(BF16) for p in P}
            o = {p: _dot(amat[p], vb[p]) + o_st[p] for p in P}
            do_b, dgs = {}, {}
            for p in P:
                g, h = p
                r = lax.rsqrt(jnp.mean(o[p] * o[p], axis=-1, keepdims=True) + RMS_EPS)
                oh = o[p] * r
                gn = gn_ref[:, vcs[h]]
                gate = cm[p][7]
                sg = _sigmoid(gate)
                dz = dzg_ref[rows[g], vcs[h]]
                don = dz * (gate * sg)
                dgs[p] = dz * (oh * gn) * (sg * (1.0 + gate * (1.0 - sg)))
                dgn_ref[:, vcs[h]] += jnp.sum(don * oh, axis=0, keepdims=True)
                doh = don * gn
                do_b[p] = (r * (doh - oh * jnp.mean(doh * oh, axis=-1, keepdims=True))).astype(BF16)
            da = {p: _dot_nt(do_b[p], vb[p]) for p in P}
            dv_a = {p: _dot_tn(amat[p], do_b[p]) for p in P}
            dqf_st = {p: _dot(do_b[p], st_b[p]) for p in P}
            dst_upd = {p: _dot_tn(do_b[p], qf_b[p]) for p in P}
            dst = {(0, h): dstate_scr[h] for h in range(GLA_HEADS)}
            for g in range(GLA_GROUP):
                for h in range(GLA_HEADS):
                    dst[(g + 1, h)] = dst[(g, h)] * dec[(g, h)] + dst_upd[(g, h)]
            for h in range(GLA_HEADS):
                dstate_scr[h] = dst[(GLA_GROUP, h)]
            dst_b = {p: dst[p].astype(BF16) for p in P}
            dv = {p: dv_a[p] + _dot_nt(ke_b[p], dst_b[p]) for p in P}
            dke = {p: _dot(vb[p], dst_b[p]) for p in P}
            da_f = {p: jnp.where(lower, da[p], 0.0).astype(BF16) for p in P}
            da_b = {p: jnp.where(lower, 0.0, da[p]).astype(BF16) for p in P}
            dqf = {p: _dot(da_f[p], kn_b[p]) + dqf_st[p] for p in P}
            dkn = {p: _dot_tn(da_f[p], qf_b[p]) for p in P}
            dqn = {p: _dot(da_b[p], kp_b[p]) for p in P}
            dkp = {p: _dot_tn(da_b[p], qn_b[p]) for p in P}
            dbs = {}
            for p in P:
                ddec = jnp.sum(dst[p] * st[p], axis=0, keepdims=True)
                db = dqf[p] * qf[p] - dkn[p] * kn[p] - dqn[p] * qn[p] + dkp[p] * kp[p] - dke[p] * ke[p]
                dbl = jnp.sum(dke[p] * ke[p], axis=0, keepdims=True) + ddec * dec[p]
                dbs[p] = db + jnp.where(last_row, dbl, 0.0)
            dla = {p: jnp.dot(triu, dbs[p], precision=HIGHEST, preferred_element_type=F32) for p in P}
            for p in P:
                g, h = p
                dq = (dqf[p] * ep[p] + dqn[p] * en[p]) * GLA_SCALE
                dk = dkn[p] * en[p] + dkp[p] * ep[p] + dke[p] * ee[p]
                dgk_scr[rows[g], kcs[h]] = dla[p] * (1.0 / 16.0) * _sigmoid(-gk_scr[rows[g], kcs[h]])
                dproj_ref[rows[g], kcs[h]] = dq.astype(BF16)
                dproj_ref[rows[g], GLA_DK + h * GLA_DK_HEAD:GLA_DK + (h + 1) * GLA_DK_HEAD] = dk.astype(BF16)
                dproj_ref[rows[g], 2 * GLA_DK + h * GLA_DV_HEAD:2 * GLA_DK + (h + 1) * GLA_DV_HEAD] = dv[p].astype(BF16)
                dproj_ref[rows[g], 2 * GLA_DK + GLA_DV + h * GLA_DV_HEAD:
                          2 * GLA_DK + GLA_DV + (h + 1) * GLA_DV_HEAD] = dgs[p].astype(BF16)
            return carry

        lax.fori_loop(0, ncb // GLA_GROUP, group, 0)
        dgk = dgk_scr[...]
        dgk_b = dgk.astype(BF16)
        dproj_ref[:, GLA_LR_OFF:GLA_IN_PAD] = _dot_nt(dgk_b, wgk_ref[...]).astype(BF16)
        dwgk_ref[...] += _dot_tn(lr, dgk_b)
        dbgk_ref[...] += jnp.sum(dgk, axis=0, keepdims=True)

    rev = lambda i: (nb - 1 - i, 0)
    return hosted_call(
        kern, side, name=name, grid=(nb,),
        in_specs=[pl.BlockSpec((TB, GLA_IN_PAD), rev),
                  pl.BlockSpec((ncb, GLA_HEADS, GLA_DV_HEAD, GLA_DK_HEAD), lambda i: (nb - 1 - i, 0, 0, 0)),
                  pl.BlockSpec((TB, GLA_DV), rev),
                  pl.BlockSpec((128, GLA_DK), lambda i: (0, 0)), _row_spec(GLA_DK), _row_spec(GLA_DV)],
        out_specs=[pl.BlockSpec((TB, GLA_IN_PAD), rev),
                   pl.BlockSpec((128, GLA_DK), lambda i: (0, 0)), _row_spec(GLA_DK), _row_spec(GLA_DV)],
        out_shape=[jax.ShapeDtypeStruct((S, GLA_IN_PAD), BF16), jax.ShapeDtypeStruct((128, GLA_DK), F32),
                   jax.ShapeDtypeStruct((1, GLA_DK), F32), jax.ShapeDtypeStruct((1, GLA_DV), F32)],
        scratch_shapes=[pltpu.VMEM((GLA_HEADS, GLA_DV_HEAD, GLA_DK_HEAD), F32), pltpu.VMEM((TB, GLA_DK), F32),
                        pltpu.VMEM((TB, GLA_DK), F32), pltpu.VMEM((TB, GLA_DK), F32)],
        dims=("arbitrary",), args=(proj, states, dzg, wgk_p, bgk, gnorm))


ATT_TW = 1024
ATT_CLASSES = 3


def _att_window(i):
    return pl.multiple_of(jnp.maximum(i * ATT_TQ - LEFT_CHUNKS * CHUNK, 0), ATT_TQ)


def _att_rel_index():
    e = jnp.arange(ATT_TW)[None, :]
    d = jnp.where(e < ATT_KW, e, e - ATT_TW)
    off = (jnp.arange(ATT_CLASSES) * ATT_TQ)[:, None]
    return jnp.clip(off - d, -MAX_REL, MAX_REL) + MAX_REL


def _row_bits():
    return lax.broadcasted_iota(jnp.int32, (ATT_TQ, ATT_TW), 0)


def att_bias_tiles(rel_bias, name):
    pick = (jnp.arange(384)[:, None] == _att_rel_index().reshape(-1)[None, :]).astype(F32)
    tab = mm_f32(jnp.pad(rel_bias, ((0, 0), (0, 384 - N_REL))), pick, name + "_tab")
    tab = tab.reshape(ATT_HEADS * ATT_CLASSES, 1, ATT_TW)

    def kern(t_ref, o_ref):
        cls = pl.program_id(0) % ATT_CLASSES
        x = jnp.broadcast_to(t_ref[...], (ATT_TQ, ATT_TW))
        x = pltpu.roll(x, 0, 1, stride=1, stride_axis=0)
        x = x[:, :ATT_KW]
        qc = cls * (ATT_TQ // CHUNK) + lax.shift_right_arithmetic(
            lax.broadcasted_iota(jnp.int32, (ATT_TQ, ATT_KW), 0), 6)
        kc = lax.shift_right_arithmetic(lax.broadcasted_iota(jnp.int32, (ATT_TQ, ATT_KW), 1), 6)
        o_ref[...] = jnp.where((kc <= qc) & (kc >= qc - LEFT_CHUNKS), x, NEG_INF)

    return pl.pallas_call(
        kern, name=name, grid=(ATT_HEADS * ATT_CLASSES,),
        in_specs=[pl.BlockSpec((None, 1, ATT_TW), lambda i: (i, 0, 0))],
        out_specs=pl.BlockSpec((None, ATT_TQ, ATT_KW), lambda i: (i, 0, 0)),
        out_shape=jax.ShapeDtypeStruct((ATT_HEADS * ATT_CLASSES, ATT_TQ, ATT_KW), F32),
        compiler_params=_cp(("parallel",)),
    )(tab)


def att_bias_grad(dbt, name):
    def kern(d_ref, o_ref):
        x = jnp.concatenate([d_ref[...], jnp.zeros((ATT_TQ, ATT_TW - ATT_KW), F32)], axis=1)
        row = _row_bits()
        for b in range(8):
            x = jnp.where((row & (1 << b)) != 0, pltpu.roll(x, ATT_TW - (1 << b), axis=1), x)
        o_ref[...] = jnp.sum(x, axis=0, keepdims=True)

    diag = pl.pallas_call(
        kern, name=name + "_diag", grid=(ATT_HEADS * ATT_CLASSES,),
        in_specs=[pl.BlockSpec((None, ATT_TQ, ATT_KW), lambda i: (i, 0, 0))],
        out_specs=pl.BlockSpec((None, 1, ATT_TW), lambda i: (i, 0, 0)),
        out_shape=jax.ShapeDtypeStruct((ATT_HEADS * ATT_CLASSES, 1, ATT_TW), F32),
        compiler_params=_cp(("parallel",)),
    )(dbt)
    diag = diag.reshape(ATT_HEADS, ATT_CLASSES * ATT_TW)
    onehot = (_att_rel_index().reshape(-1)[:, None] == jnp.arange(384)[None, :]).astype(F32)
    return mm_f32(diag, onehot, name + "_bins")[:, :N_REL]


ATT_GROUP = 2


def _att_scores(q_ref, k_ref, bias_refs, blk0):
    G = range(ATT_GROUP)
    hs = [slice(hh * ATT_HD, (hh + 1) * ATT_HD) for hh in range(2)]
    rows = [slice(g * ATT_TQ, (g + 1) * ATT_TQ) for g in G]
    wins = [pl.ds(_att_window(blk0 + g), ATT_KW) for g in G]
    kw = [k_ref[w, :] for w in wins]
    P = [(g, hh) for g in G for hh in range(2)]
    q = {p: q_ref[rows[p[0]], hs[p[1]]] * ATT_SCALE for p in P}
    k = {p: kw[p[0]][:, hs[p[1]]] for p in P}
    s = {p: _dot_nt(q[p], k[p]) + bias_refs[p[0]][p[1]] for p in P}
    e = {p: jnp.exp(s[p] - jnp.max(s[p], axis=-1, keepdims=True)) for p in P}
    inv = {p: 1.0 / jnp.sum(e[p], axis=-1, keepdims=True) for p in P}
    return P, rows, wins, hs, q, k, e, inv


def _att_specs(S):
    nq = D_MODEL // 128
    q_spec = pl.BlockSpec((ATT_GROUP * ATT_TQ, 128), lambda p, i: (i, p))
    k_spec = pl.BlockSpec((S, 128), lambda p, i: (0, nq + p))
    v_spec = pl.BlockSpec((S, 128), lambda p, i: (0, 2 * nq + p))
    b_specs = [pl.BlockSpec((2, None, ATT_TQ, ATT_KW),
                            functools.partial(lambda p, i, g: (p, jnp.minimum(ATT_GROUP * i + g, ATT_CLASSES - 1), 0, 0), g=g))
               for g in range(ATT_GROUP)]
    return q_spec, k_spec, v_spec, b_specs


def attn_fwd(qkv, bias, name, side=None):
    S = qkv.shape[0]
    q_spec, k_spec, v_spec, b_specs = _att_specs(S)

    def kern(q_ref, k_ref, v_ref, *rest):
        bias_refs, o_ref = rest[:ATT_GROUP], rest[ATT_GROUP]
        P, rows, wins, hs, _, _, e, inv = _att_scores(q_ref, k_ref, bias_refs, ATT_GROUP * pl.program_id(1))
        vw = [v_ref[w, :] for w in wins]
        o = {p: _dot(e[p].astype(BF16), vw[p[0]][:, hs[p[1]]]) * inv[p] for p in P}
        for g in range(ATT_GROUP):
            o_ref[rows[g], :] = jnp.concatenate([o[(g, 0)], o[(g, 1)]], axis=1).astype(BF16)

    return hosted_call(
        kern, side, name=name, grid=(ATT_HEADS // 2, S // (ATT_GROUP * ATT_TQ)),
        in_specs=[q_spec, k_spec, v_spec] + b_specs,
        out_specs=[pl.BlockSpec((ATT_GROUP * ATT_TQ, 128), lambda p, i: (i, p))],
        out_shape=[jax.ShapeDtypeStruct((S, D_MODEL), BF16)],
        scratch_shapes=[], dims=("parallel", "arbitrary"), args=(qkv, qkv, qkv) + (bias,) * ATT_GROUP)


def attn_bwd(qkv, bias, do, name, side=None):
    S = qkv.shape[0]
    nstep = S // (ATT_GROUP * ATT_TQ)
    q_spec, k_spec, v_spec, b_specs = _att_specs(S)

    def kern(q_ref, k_ref, v_ref, *rest):
        bias_refs = rest[:ATT_GROUP]
        do_ref, dqkv_ref, db_ref, dk_scr, dv_scr = rest[ATT_GROUP:]
        i = pl.program_id(1)

        @pl.when(i == 0)
        def _():
            dk_scr[...] = jnp.zeros_like(dk_scr)
            dv_scr[...] = jnp.zeros_like(dv_scr)
            db_ref[...] = jnp.zeros_like(db_ref)
        blk0 = ATT_GROUP * i
        P, rows, wins, hs, q, k, e, inv = _att_scores(q_ref, k_ref, bias_refs, blk0)
        vw = [v_ref[w, :] for w in wins]
        do_h = {p: do_ref[rows[p[0]], hs[p[1]]] for p in P}
        dp = {p: _dot_nt(do_h[p], vw[p[0]][:, hs[p[1]]]) for p in P}
        pr = {p: e[p] * inv[p] for p in P}
        dvs = {p: _dot_tn(pr[p].astype(BF16), do_h[p]) for p in P}
        ds = {p: pr[p] * (dp[p] - jnp.sum(pr[p] * dp[p], axis=-1, keepdims=True)) for p in P}
        ds_b = {p: ds[p].astype(BF16) for p in P}
        dqs = {p: _dot(ds_b[p], k[p]) * ATT_SCALE for p in P}
        dks = {p: _dot_tn(ds_b[p], q[p]) for p in P}
        for g, hh in P:
            db_ref[hh, jnp.minimum(blk0 + g, ATT_CLASSES - 1)] += ds[(g, hh)]
        for g in range(ATT_GROUP):
            first = pl.multiple_of((blk0 + g) * ATT_TQ, ATT_TQ)
            dqkv_ref[0, pl.ds(first, ATT_TQ), :] = jnp.concatenate([dqs[(g, 0)], dqs[(g, 1)]], axis=1).astype(BF16)
            dk_scr[wins[g], :] += jnp.concatenate([dks[(g, 0)], dks[(g, 1)]], axis=1)
            dv_scr[wins[g], :] += jnp.concatenate([dvs[(g, 0)], dvs[(g, 1)]], axis=1)

        @pl.when(i == nstep - 1)
        def _():
            dqkv_ref[1] = dk_scr[...].astype(BF16)
            dqkv_ref[2] = dv_scr[...].astype(BF16)

    return hosted_call(
        kern, side, name=name, grid=(ATT_HEADS // 2, nstep),
        in_specs=[q_spec, k_spec, v_spec] + b_specs + [pl.BlockSpec((ATT_GROUP * ATT_TQ, 128), lambda p, i: (i, p))],
        out_specs=[pl.BlockSpec((3, S, 128), lambda p, i: (0, 0, p)),
                   pl.BlockSpec((2, ATT_CLASSES, ATT_TQ, ATT_KW), lambda p, i: (p, 0, 0, 0))],
        out_shape=[jax.ShapeDtypeStruct((3, S, D_MODEL), BF16),
                   jax.ShapeDtypeStruct((ATT_HEADS, ATT_CLASSES, ATT_TQ, ATT_KW), F32)],
        scratch_shapes=[pltpu.VMEM((S, 128), F32), pltpu.VMEM((S, 128), F32)],
        dims=("parallel", "arbitrary"), args=(qkv, qkv, qkv) + (bias,) * ATT_GROUP + (do,))


def colsum3(a3, name):
    P, S, N = a3.shape
    tm = min(512, S)

    def kern(a_ref, o_ref):
        @pl.when(pl.program_id(1) == 0)
        def _():
            o_ref[...] = jnp.zeros_like(o_ref)
        o_ref[...] += jnp.sum(a_ref[...].astype(F32), axis=0, keepdims=True)

    return pl.pallas_call(
        kern, name=name, grid=(P, S // tm),
        in_specs=[pl.BlockSpec((None, tm, N), lambda p, i: (p, i, 0))],
        out_specs=pl.BlockSpec((None, 1, N), lambda p, i: (p, 0, 0)),
        out_shape=jax.ShapeDtypeStruct((P, 1, N), F32),
        compiler_params=_cp(("parallel", "arbitrary")),
    )(a3)


def _me():
    return lax.axis_index("x"), lax.axis_index("y"), lax.axis_index("c")


def _other_chips(x, y):
    return [(1 - x, y), (x, 1 - y), (1 - x, 1 - y)]


def all_gather8(x_shard, name):
    m_per, n = x_shard.shape

    def body(x_ref, out_ref, send_sems, recv_sems, local_sem):
        x, y, c = _me()
        me, sibling = (x, y, c), (x, y, 1 - c)
        chips = _other_chips(x, y)

        def rows(px, py, pc):
            return out_ref.at[pl.ds((4 * px + 2 * py + pc) * m_per, m_per), :]

        def copy(k, block, to, src=None):
            return pltpu.make_async_remote_copy(
                src_ref=rows(*block) if src is None else src, dst_ref=rows(*block),
                send_sem=send_sems.at[k], recv_sem=recv_sems.at[k], device_id=to, device_id_type=MESH)

        mine = pltpu.make_async_copy(x_ref, rows(*me), local_sem)
        mine.start()
        first = [copy(0, me, sibling, src=x_ref)]
        first += [copy(1 + j, me, (*chip, c), src=x_ref) for j, chip in enumerate(chips)]
        for cp in first:
            cp.start()
        passed = [copy(4 + j, (*chip, c), sibling) for j, chip in enumerate(chips)]
        for j, chip in enumerate(chips):
            copy(1 + j, (*chip, c), me).wait_recv()
            passed[j].start()
        copy(0, sibling, me).wait_recv()
        for j, chip in enumerate(chips):
            copy(4 + j, (*chip, 1 - c), me).wait_recv()
        for cp in first + passed:
            cp.wait_send()
        mine.wait()

    return pl.pallas_call(
        body, name=name,
        out_shape=jax.ShapeDtypeStruct((N_DEV * m_per, n), x_shard.dtype),
        in_specs=[pl.BlockSpec(memory_space=pltpu.VMEM)],
        out_specs=pl.BlockSpec(memory_space=pltpu.VMEM),
        scratch_shapes=[pltpu.SemaphoreType.DMA((7,)), pltpu.SemaphoreType.DMA((7,)), pltpu.SemaphoreType.DMA],
        compiler_params=pltpu.CompilerParams(vmem_limit_bytes=VMEM_LIMIT),
    )(x_shard)


def _half_rows(n_rows, c):
    h = n_rows // 2
    return pl.ds(c * h, h)


def _gathered_shape(shape, flavour):
    L, a, b = shape
    return {"col": (L, a, N_CHIPS * b), "row": (L, N_CHIPS * a, b), "lead": (N_CHIPS, L, a, b)}[flavour]


def _gathered_part(out_ref, shape, flavour, s, rows):
    L, a, b = shape
    if flavour == "col":
        return out_ref.at[:, rows, pl.ds(s * b, b)]
    if flavour == "row":
        return out_ref.at[:, pl.ds(s * a + rows.start, rows.size), :]
    return out_ref.at[s, :, rows, :]


def gather_side(shards, flavours):
    n = len(shards)
    shapes = [w.shape for w in shards]

    def copies(w_refs, out_refs, send_sems, recv_sems, local_sems):
        x, y, c = _me()
        sibling = (x, y, 1 - c)
        chips = _other_chips(x, y)
        me_s = 2 * x + y

        def copy(k, src, dst, to):
            return pltpu.make_async_remote_copy(src_ref=src, dst_ref=dst, send_sem=send_sems.at[k],
                                                recv_sem=recv_sems.at[k], device_id=to, device_id_type=MESH)

        own, first, landed, passed, passed_in = [], [], [], [], []
        for w in range(n):
            shp, fl = shapes[w], flavours[w]
            my_half = _half_rows(shp[1], c)
            sib_half = _half_rows(shp[1], 1 - c)
            own.append(copy(7 * w + 6, w_refs[w], _gathered_part(out_refs[w], shp, fl, me_s, pl.ds(0, shp[1])), sibling))
            for j, chip in enumerate(chips):
                s = 2 * chip[0] + chip[1]
                first.append(copy(7 * w + j, w_refs[w].at[:, my_half, :],
                                  _gathered_part(out_refs[w], shp, fl, me_s, my_half), (*chip, c)))
                part = _gathered_part(out_refs[w], shp, fl, s, my_half)
                landed.append(copy(7 * w + j, part, part, (*chip, c)))
                passed.append(copy(7 * w + 3 + j, part, part, sibling))
                theirs = _gathered_part(out_refs[w], shp, fl, s, sib_half)
                passed_in.append(copy(7 * w + 3 + j, theirs, theirs, sibling))
        return own, first, landed, passed, passed_in

    def start(*refs):
        own, first, _, _, _ = copies(*refs)
        for cp in first + own:
            cp.start()

    def wait(*refs):
        own, first, landed, passed, passed_in = copies(*refs)
        for arrived, onward in zip(landed, passed):
            arrived.wait_recv()
            onward.start()
        for cp in passed_in:
            cp.wait_recv()
        for cp in own:
            cp.wait()
        for cp in first + passed:
            cp.wait_send()

    out_shapes = [jax.ShapeDtypeStruct(_gathered_shape(s, f), w.dtype) for w, s, f in zip(shards, shapes, flavours)]
    return Side(shards, out_shapes, 7 * n, 0, start, wait)


def swap_side(gs):
    n = len(gs)

    def copies(g_refs, out_refs, send_sems, recv_sems, local_sems):
        x, y, c = _me()
        return [pltpu.make_async_remote_copy(
            src_ref=g_refs[w].at[:, _half_rows(gs[w].shape[1], 1 - c), :], dst_ref=out_refs[w],
            send_sem=send_sems.at[w], recv_sem=recv_sems.at[w], device_id=(x, y, 1 - c), device_id_type=MESH)
            for w in range(n)]

    def start(*refs):
        for cp in copies(*refs):
            cp.start()

    def wait(*refs):
        for cp in copies(*refs):
            cp.wait()

    out_shapes = [jax.ShapeDtypeStruct((g.shape[0], g.shape[1] // 2, g.shape[2]), g.dtype) for g in gs]
    return Side(gs, out_shapes, n, 0, start, wait)


def add_half(g, r1, c_idx, name):
    n, R, C = g.shape
    half = R // 2
    tr = _rows_block(half, C)
    nbh = half // tr

    def kern(c_ref, g_ref, r_ref, o_ref, ob_ref):
        p = g_ref[...] + r_ref[...]
        o_ref[...] = p
        ob_ref[...] = p.astype(BF16)

    spec = pl.BlockSpec((1, tr, C), lambda d, r, c_ref: (d, r, 0))
    return pl.pallas_call(
        kern, name=name,
        grid_spec=pltpu.PrefetchScalarGridSpec(
            num_scalar_prefetch=1, grid=(n, nbh),
            in_specs=[pl.BlockSpec((1, tr, C), lambda d, r, c_ref: (d, c_ref[0] * nbh + r, 0)), spec],
            out_specs=[spec, spec]),
        out_shape=[jax.ShapeDtypeStruct((n, half, C), F32), jax.ShapeDtypeStruct((n, half, C), BF16)],
        compiler_params=_cp(("parallel", "parallel")),
    )(c_idx, g, r1)


def exchange_side(ps):
    n = len(ps)

    def copies(p_refs, out_refs, send_sems, recv_sems, local_sems):
        x, y, c = _me()
        return [pltpu.make_async_remote_copy(
            src_ref=p_refs[w].at[2 * chip[0] + chip[1]], dst_ref=out_refs[w].at[j],
            send_sem=send_sems.at[3 * w + j], recv_sem=recv_sems.at[3 * w + j],
            device_id=(*chip, c), device_id_type=MESH)
            for w in range(n) for j, chip in enumerate(_other_chips(x, y))]

    def start(*refs):
        for cp in copies(*refs):
            cp.start()

    def wait(*refs):
        for cp in copies(*refs):
            cp.wait()

    return Side(ps, [jax.ShapeDtypeStruct((3,) + p.shape[1:], p.dtype) for p in ps], 3 * n, 0, start, wait)


def add_chips(p, r2, chip_idx, name):
    n, H, C = p.shape
    tr = _rows_block(H, C)

    def kern(s_ref, p_ref, r_ref, o_ref):
        o_ref[...] = ((p_ref[0] + r_ref[0].astype(F32)) + r_ref[1].astype(F32)) + r_ref[2].astype(F32)

    return pl.pallas_call(
        kern, name=name,
        grid_spec=pltpu.PrefetchScalarGridSpec(
            num_scalar_prefetch=1, grid=(H // tr,),
            in_specs=[pl.BlockSpec((1, tr, C), lambda r, s_ref: (s_ref[0], r, 0)),
                      pl.BlockSpec((3, tr, C), lambda r, s_ref: (0, r, 0))],
            out_specs=pl.BlockSpec((tr, C), lambda r, s_ref: (r, 0))),
        out_shape=jax.ShapeDtypeStruct((H, C), F32),
        compiler_params=_cp(("parallel",)),
    )(chip_idx, p, r2)


def swap_reduced(ss, name):
    n = len(ss)

    def body(*refs):
        s_refs, out_refs = refs[:n], refs[n:2 * n]
        send_sems, recv_sems = refs[2 * n:]
        x, y, c = _me()
        cps = [pltpu.make_async_remote_copy(src_ref=s_refs[w], dst_ref=out_refs[w], send_sem=send_sems.at[w],
                                            recv_sem=recv_sems.at[w], device_id=(x, y, 1 - c), device_id_type=MESH)
               for w in range(n)]
        for cp in cps:
            cp.start()
        for cp in cps:
            cp.wait()

    any_spec = pl.BlockSpec(memory_space=pl.ANY)
    return pl.pallas_call(
        body, name=name, out_shape=[jax.ShapeDtypeStruct(s.shape, s.dtype) for s in ss],
        in_specs=[any_spec] * n, out_specs=[any_spec] * n,
        scratch_shapes=[pltpu.SemaphoreType.DMA((n,)), pltpu.SemaphoreType.DMA((n,))],
    )(*ss)


BIG = (("gla_w_in", 2, (1024, GLA_IN // N_CHIPS), "lead"), ("gla_w_out", 2, (256, 1024), "row"),
       ("att_w_in", 2, (1024, 768), "col"), ("att_w_out", 2, (256, 1024), "row"),
       ("ff_w1", 4, (1024, 1024), "col"), ("ff_w2", 4, (1024, 1024), "row"))
FLAVOUR = {n: f for n, _, _, f in BIG}


def layer_weights(i):
    mixer = "gla" if i % 2 == 0 else "att"
    return (("in", mixer + "_w_in", i // 2), ("out", mixer + "_w_out", i // 2), ("w1", "ff_w1", i), ("w2", "ff_w2", i))


class Comm:
    def __init__(self, weights, core, chip):
        self.weights, self.core, self.chip = weights, core, chip
        self.c_idx = jnp.reshape(core, (1,)).astype(jnp.int32)
        self.chip_idx = jnp.reshape(chip, (1,)).astype(jnp.int32)
        self.reduced = {}

    def gather(self, items):
        shards = [self.weights[n][l:l + 1].astype(BF16) for _, n, l in items]
        return gather_side(shards, [FLAVOUR[n] for _, n, _ in items])

    def full_weights(self, items, gathered):
        W = {}
        for (role, n, _), w in zip(items, gathered):
            if n == "gla_w_in":
                w = jnp.pad(w.transpose(1, 2, 0, 3).reshape(1, D_MODEL, GLA_IN), ((0, 0), (0, 0), (0, GLA_IN_PAD - GLA_IN)))
            W[role] = (w, 0)
        return W

    def gather_now(self, items, name):
        return self.full_weights(items, run_side(self.gather(items), name))

    def swap(self, items):
        return swap_side([g for _, _, g in items])

    def reduce_begin(self, tag, items, swapped):
        ps = [add_half(g, r, self.c_idx, f"rs_add2_{tag}_{w}") for w, ((_, _, g), r) in enumerate(zip(items, swapped))]
        return tag, [(n, l) for n, l, _ in items], ps

    def exchange(self, pending):
        return exchange_side([pb for _, pb in pending[2]])

    def reduce_mid(self, pending, landed):
        tag, keys, ps = pending
        for w, (key, (p, _), r) in enumerate(zip(keys, ps, landed)):
            self.reduced[key] = add_chips(p, r, self.chip_idx, f"rs_add4_{tag}_{w}")

    def reduce_tail(self, tag, items):
        pending = self.reduce_begin(tag, items, run_side(self.swap(items), f"rs_swap_{tag}"))
        self.reduce_mid(pending, run_side(self.exchange(pending), f"rs_xchg_{tag}"))

    def reduce_end(self):
        keys = [(n, l) for n, L, _, _ in BIG for l in range(L)]
        mine = [self.reduced[k] for k in keys]
        theirs = swap_reduced(mine, "rs_join")
        low = self.core == 0
        full = {k: jnp.concatenate([jnp.where(low, m, t), jnp.where(low, t, m)], axis=0)
                for k, m, t in zip(keys, mine, theirs)}
        return {n: jnp.stack([full[(n, l)] for l in range(L)]) for n, L, _, _ in BIG}


def local_step(x, target, mods, comm, small):
    S, D = x.shape
    row = lambda v: v.reshape(1, -1)
    saved = []
    tiles = [att_bias_tiles(small["att_rel_bias"][j], f"att_tiles_{j}").reshape(ATT_HEADS, ATT_CLASSES, ATT_TQ, ATT_KW)
             for j in range(2)]
    wgk_p = [jnp.pad(small["gla_w_gk2"][j], ((0, 128 - GLA_RANK), (0, 0))).astype(BF16) for j in range(2)]

    u1 = modulate(x, row(mods[0, 1]), row(mods[0, 0]), "mod_first")
    Ws = [dict() for _ in range(DEPTH)]
    items0 = layer_weights(0)
    Ws[0].update(comm.gather_now(items0[:1], "gather_w0"))
    for i in range(DEPTH):
        j = i // 2
        W = Ws[i]
        sh1, sc1, g1, sh2, sc2, g2 = (row(mods[i, k]) for k in range(6))
        nxt = min(i + 1, DEPTH - 1)
        more = i + 1 < DEPTH
        nxt_items = layer_weights(nxt)
        in_items = list(items0[1:3]) if i == 0 else []
        mix_items = (list(items0[3:]) if i == 0 else []) + (list(nxt_items[:2]) if more else [])
        up_items = list(nxt_items[2:3]) if more else []
        down_items = list(nxt_items[3:]) if more else []

        def hosted(items):
            return comm.gather(items) if items else None

        def landed_weights(items, landed):
            for k, it in enumerate(items):
                layer = 0 if it in items0 and i == 0 else nxt
                Ws[layer].update(comm.full_weights([it], landed[k:k + 1]))

        side = hosted(in_items)
        if i % 2 == 0:
            proj = mm_plain(u1, *W["in"], f"gla_in_{i}", side=side)
        else:
            proj = mm_plain(u1, *W["in"], f"att_in_{i}", mode="bf16", bias=row(small["att_b_in"][j]), side=side)
        proj, landed = proj if side is not None else (proj, [])
        landed_weights(in_items, landed)
        if i % 2 == 0:
            (zmix, states), landed = gla_fwd(proj, wgk_p[j], row(small["gla_b_gk"][j]), row(small["gla_g_norm"][j]),
                                             f"gla_fwd_{i}", hosted(mix_items))
        else:
            (zmix,), landed = attn_fwd(proj, tiles[j], f"att_fwd_{i}", hosted(mix_items))
            states = None
        landed_weights(mix_items, landed)
        (y1, x_mid, u2), _ = mm_down_ln(zmix, *W["out"], x, 1.0 + g1, row(small["ln_g"][i, 0]),
                                        row(small["ln_b"][i, 0]), sc2, sh2, f"mix_out_{i}")
        side = hosted(up_items)
        act = mm_plain(u2, *W["w1"], f"ff_up_{i}", mode="mlp_up", side=side)
        act, landed = act if side is not None else (act, [])
        landed_weights(up_items, landed)
        (y2, x_out, u_next), landed = mm_down_ln(act, *W["w2"], x_mid, 1.0 + g2, row(small["ln_g"][i, 1]),
                                                 row(small["ln_b"][i, 1]), row(mods[nxt, 1]), row(mods[nxt, 0]),
                                                 f"ff_out_{i}", side=hosted(down_items))
        landed_weights(down_items, landed)
        saved.append(dict(x_in=x, u1=u1, proj=proj, zmix=zmix, states=states, y1=y1, x_mid=x_mid, u2=u2,
                          act=act, y2=y2))
        x, u1 = x_out, u_next

    dx, sq = loss_head(x, target, "loss_head")

    g_small = dict(ln_g=[None] * DEPTH, ln_b=[None] * DEPTH, gla_w_gk2=[None] * 2, gla_b_gk=[None] * 2,
                   gla_g_norm=[None] * 2, att_b_in=[None] * 2, att_rel_bias=[None] * 2)
    dmods = [None] * DEPTH
    later = []
    top = saved[DEPTH - 1]
    dz2, dy2, s_ln2 = ln_bwd(dx, top["x_mid"], top["y2"], 1.0 + row(mods[DEPTH - 1, 5]),
                             row(small["ln_g"][DEPTH - 1, 1]), "ln2_bwd_top")

    for i in reversed(range(DEPTH)):
        j = i // 2
        sv = saved[i]
        W = Ws[i]
        sh1, sc1, g1, sh2, sc2, g2 = (row(mods[i, k]) for k in range(6))
        dh = mm_plain(dy2, *W["w2"], f"ff_dn_{i}", mode="mlp_dn", nt=True, h=sv["act"])
        g_w2 = mm_w_res(sv["act"], dy2, f"ff_w2g_{i}").reshape(N_CHIPS, D_FF // N_CHIPS, D)
        g_w1 = mm_w_res(sv["u2"], dh, f"ff_w1g_{i}", chips_out=True)
        items = [("ff_w1", i, g_w1), ("ff_w2", i, g_w2)] + later
        (dz1, dy1, s_m2, s_ln1), swapped = mm_down_comb(
            dh, *W["w1"], dz2, sv["x_mid"], 1.0 + sc2, f"ff_dx_{i}",
            ln=(sv["x_in"], sv["y1"], 1.0 + g1, row(small["ln_g"][i, 0])), side=comm.swap(items))
        pending = comm.reduce_begin(i, items, swapped)
        side = comm.exchange(pending)
        mixer = "gla" if i % 2 == 0 else "att"
        below = None
        if i > 0:
            below = (saved[i - 1]["x_mid"], saved[i - 1]["y2"], 1.0 + row(mods[i - 1, 5]), row(small["ln_g"][i - 1, 1]))
        if i % 2 == 0:
            g_out = mm_w(sv["zmix"], dy1, f"gla_wog_{i}").reshape(N_CHIPS, D // N_CHIPS, D)
            dzg = mm_plain(dy1, *W["out"], f"gla_dz_{i}", nt=True)
            (dproj, dwgk, dbgk, dgn), landed = gla_bwd(sv["proj"], sv["states"], dzg, wgk_p[j],
                                                       row(small["gla_b_gk"][j]), row(small["gla_g_norm"][j]),
                                                       f"gla_bwd_{i}", side)
            g_small["gla_w_gk2"][j] = dwgk[:GLA_RANK]
            g_small["gla_b_gk"][j] = dbgk[0]
            g_small["gla_g_norm"][j] = dgn[0].reshape(GLA_HEADS, GLA_DV_HEAD)
            gwi = mm_w_res(sv["u1"], dproj, f"gla_wig_{i}")[:, :GLA_IN]
            g_in = gwi.reshape(D, N_CHIPS, GLA_IN // N_CHIPS).transpose(1, 0, 2)
            outs, _ = mm_down_comb(dproj, *W["in"], dz1, sv["x_in"], 1.0 + sc1, f"mix_dx_{i}", ln=below)
        else:
            g_out = mm_w(sv["zmix"], dy1, f"att_wog_{i}").reshape(N_CHIPS, D // N_CHIPS, D)
            do = mm_plain(dy1, *W["out"], f"att_do_{i}", mode="bf16", nt=True)
            (dqkv, dbt), landed = attn_bwd(sv["proj"], tiles[j], do, f"att_bwd_{i}", side)
            g_small["att_rel_bias"][j] = att_bias_grad(dbt.reshape(ATT_HEADS * ATT_CLASSES, ATT_TQ, ATT_KW),
                                                       f"att_bias_{i}")
            g_in = mm_w_chips3(sv["u1"], dqkv, f"att_wig_{i}")
            outs, _ = mm_down_comb(dqkv, *W["in"], dz1, sv["x_in"], 1.0 + sc1, f"mix_dx_{i}", parts=3, ln=below)
        s_m1 = outs[1] if below is None else outs[2]
        if i % 2 == 1:
            g_small["att_b_in"][j] = s_m1[2:5].reshape(3 * D)
        comm.reduce_mid(pending, landed)
        later = [(mixer + "_w_in", j, g_in), (mixer + "_w_out", j, g_out)]
        g_small["ln_g"][i] = jnp.stack([s_ln1[0], s_ln2[0]])
        g_small["ln_b"][i] = jnp.stack([s_ln1[1], s_ln2[1]])
        dmods[i] = jnp.stack([s_m1[1], s_m1[0], s_ln1[2], s_m2[1], s_m2[0], s_ln2[2]])
        if below is None:
            dx = outs[0]
        else:
            dz2, dy2, s_ln2 = outs[0], outs[1], outs[3]
    comm.reduce_tail("last", later)

    g_small = {n: jnp.stack(v) for n, v in g_small.items()}
    return sq, dx, jnp.stack(dmods), g_small


SMALL_SHARDED = (("ln_g", (4, 2, 256)), ("ln_b", (4, 2, 256)), ("gla_g_norm", (2, 4, 64)),
                 ("gla_w_gk2", (2, 16, 128)), ("att_b_in", (2, 768)))
SMALL_FULL = dict(ln_g=(4, 2, 1024), ln_b=(4, 2, 1024), gla_g_norm=(2, 4, 256), gla_w_gk2=(2, 16, 512),
                  att_b_in=(2, 3072), gla_b_gk=(2, 512), att_rel_bias=(2, 16, 257))
SMALL_GRAD_ORDER = ("ln_g", "ln_b", "gla_g_norm", "gla_w_gk2", "att_b_in", "gla_b_gk", "att_rel_bias")


def _pack_small(arrs, rows_total):
    parts = []
    for a in arrs:
        flat = a.reshape(-1)
        pad = (-flat.shape[0]) % PACK_W
        parts.append(jnp.pad(flat, (0, pad)).reshape(-1, PACK_W))
    buf = jnp.concatenate(parts, axis=0)
    return jnp.pad(buf, ((0, rows_total - buf.shape[0]), (0, 0)))


def _unpack_small(buf, shapes):
    out, r = [], 0
    for shp in shapes:
        n = 1
        for s in shp:
            n *= s
        nr = (n + PACK_W - 1) // PACK_W
        out.append(buf[..., r:r + nr, :].reshape(buf.shape[:-2] + (nr * PACK_W,))[..., :n].reshape(buf.shape[:-2] + shp))
        r += nr
    return out


def _unshard_last(g4):
    nd = g4.ndim
    perm = tuple(range(1, nd - 1)) + (0, nd - 1)
    t = g4.transpose(perm)
    return t.reshape(t.shape[:-2] + (-1,))


def _shard_last(full, s):
    n = full.shape[-1] // N_CHIPS
    return lax.dynamic_slice_in_dim(full, s * n, n, axis=full.ndim - 1)


WEIGHT_NAMES = ("w_ada", "b_ada", "ln_g", "ln_b", "gla_w_in", "gla_w_gk2", "gla_b_gk", "gla_g_norm", "gla_w_out",
                "att_w_in", "att_b_in", "att_rel_bias", "att_w_out", "ff_w1", "ff_w2")


def kernel(x, c, w_ada, b_ada, ln_g, ln_b, gla_w_in, gla_w_gk2, gla_b_gk, gla_g_norm, gla_w_out, att_w_in, att_b_in, att_rel_bias, att_w_out, ff_w1, ff_w2, loss_target, m_w_ada, m_b_ada, m_ln_g, m_ln_b, m_gla_w_in, m_gla_w_gk2, m_gla_b_gk, m_gla_g_norm, m_gla_w_out, m_att_w_in, m_att_b_in, m_att_rel_bias, m_att_w_out, m_ff_w1, m_ff_w2, v_w_ada, v_b_ada, v_ln_g, v_ln_b, v_gla_w_in, v_gla_w_gk2, v_gla_b_gk, v_gla_g_norm, v_gla_w_out, v_att_w_in, v_att_b_in, v_att_rel_bias, v_att_w_out, v_ff_w1, v_ff_w2):
    weights = dict(w_ada=w_ada, b_ada=b_ada, ln_g=ln_g, ln_b=ln_b, gla_w_in=gla_w_in, gla_w_gk2=gla_w_gk2,
                   gla_b_gk=gla_b_gk, gla_g_norm=gla_g_norm, gla_w_out=gla_w_out, att_w_in=att_w_in,
                   att_b_in=att_b_in, att_rel_bias=att_rel_bias, att_w_out=att_w_out, ff_w1=ff_w1, ff_w2=ff_w2)
    mom1 = dict(w_ada=m_w_ada, b_ada=m_b_ada, ln_g=m_ln_g, ln_b=m_ln_b, gla_w_in=m_gla_w_in, gla_w_gk2=m_gla_w_gk2,
                gla_b_gk=m_gla_b_gk, gla_g_norm=m_gla_g_norm, gla_w_out=m_gla_w_out, att_w_in=m_att_w_in,
                att_b_in=m_att_b_in, att_rel_bias=m_att_rel_bias, att_w_out=m_att_w_out, ff_w1=m_ff_w1, ff_w2=m_ff_w2)
    mom2 = dict(w_ada=v_w_ada, b_ada=v_b_ada, ln_g=v_ln_g, ln_b=v_ln_b, gla_w_in=v_gla_w_in, gla_w_gk2=v_gla_w_gk2,
                gla_b_gk=v_gla_b_gk, gla_g_norm=v_gla_g_norm, gla_w_out=v_gla_w_out, att_w_in=v_att_w_in,
                att_b_in=v_att_b_in, att_rel_bias=v_att_rel_bias, att_w_out=v_att_w_out, ff_w1=v_ff_w1, ff_w2=v_ff_w2)

    ax, ay, ac = lax.axis_index("x"), lax.axis_index("y"), lax.axis_index("c")
    chip = 2 * ax + ay
    dev = 2 * chip + ac
    S = x.shape[1]
    x2 = x.reshape(S, D_MODEL)
    t2 = loss_target.reshape(S, D_MODEL)

    comm = Comm(weights, ac, chip)

    small_rows = 16
    spack = _pack_small([c] + [weights[n] for n, _ in SMALL_SHARDED], small_rows)
    sg = all_gather8(spack, "gather_small").reshape(N_DEV, small_rows, PACK_W)
    parts = _unpack_small(sg, [(1, D_MODEL)] + [shp for _, shp in SMALL_SHARDED])
    c_all = parts[0].reshape(N_DEV, D_MODEL)
    small = {n: _unshard_last(p[0::2]) for (n, _), p in zip(SMALL_SHARDED, parts[1:])}
    small["gla_b_gk"] = gla_b_gk
    small["att_rel_bias"] = att_rel_bias

    c_act = silu_rows(jnp.pad(c_all, ((0, 128 - N_DEV), (0, 0))), "silu_c")
    wa = w_ada.astype(BF16).transpose(1, 0, 2).reshape(1, D_MODEL, DEPTH * 6 * D_MODEL // N_CHIPS)
    mods_part = mm_plain(c_act, wa, 0, "ada_fwd", tm=128)[:N_DEV]
    mg = all_gather8(mods_part, "gather_mods").reshape(N_CHIPS, 2, N_DEV, DEPTH, 6 * D_MODEL // N_CHIPS)
    mods_mine = lax.dynamic_index_in_dim(mg[:, 0], dev, axis=1, keepdims=False)
    mods = mods_mine.transpose(1, 0, 2).reshape(DEPTH, 6 * D_MODEL) + b_ada
    mods = mods.reshape(DEPTH, 6, D_MODEL)

    sq, grad_x, dmods, g_small = local_step(x2, t2, mods, comm, small)
    loss = lax.psum(0.5 * sq[0, 0] / D_MODEL, ("x", "y", "c"))

    g_shard = comm.reduce_end()

    dm_flat = dmods.reshape(DEPTH, 6 * D_MODEL)
    g_rows = 80
    gpack = _pack_small([dm_flat] + [g_small[n] for n in SMALL_GRAD_ORDER], g_rows)
    gg = all_gather8(gpack, "gather_small_grads").reshape(N_DEV, g_rows, PACK_W)
    gsum = sum_over_devices(gg, "sum_small_grads")
    sums = _unpack_small(gsum, [(DEPTH, 6 * D_MODEL)] + [SMALL_FULL[n] for n in SMALL_GRAD_ORDER])
    grads = dict(b_ada=sums[0])
    for n, full_g in zip(SMALL_GRAD_ORDER, sums[1:]):
        grads[n] = full_g if n in ("gla_b_gk", "att_rel_bias") else _shard_last(full_g, chip)
    dm_all = _unpack_small(gg, [(DEPTH, 6 * D_MODEL)])[0]
    dm_cols = _shard_last(dm_all, chip).reshape(N_DEV, DEPTH * 6 * D_MODEL // N_CHIPS)
    dm_cols = jnp.pad(dm_cols, ((0, 128 - N_DEV), (0, 0))).astype(BF16)
    gwa = mm_w(c_act, dm_cols, "ada_bwd", ts=128)
    grads["w_ada"] = gwa.reshape(D_MODEL, DEPTH, 6 * D_MODEL // N_CHIPS).transpose(1, 0, 2)
    grads.update(g_shard)

    deltas, new_m, new_v = {}, {}, {}
    for n in WEIGHT_NAMES:
        deltas[n], new_m[n], new_v[n] = adamw(weights[n], grads[n], mom1[n], mom2[n], "adamw_" + n)

    return (loss, grad_x.reshape(1, S, D_MODEL), *[grads[n] for n in WEIGHT_NAMES], *[deltas[n] for n in WEIGHT_NAMES],
            *[new_m[n] for n in WEIGHT_NAMES], *[new_v[n] for n in WEIGHT_NAMES])
```

```python
import functools

import jax
import jax.numpy as jnp
from jax import lax
from jax.experimental import pallas as pl
from jax.experimental.pallas import tpu as pltpu

F32 = jnp.float32
BF16 = jnp.bfloat16
HIGHEST = lax.Precision.HIGHEST
MESH = pl.DeviceIdType.MESH

D_MODEL = 1024
DEPTH = 4
CHUNK = 64
GLA_HEADS = 4
GLA_DK = 512
GLA_DV = 1024
GLA_DK_HEAD = 128
GLA_DV_HEAD = 256
GLA_RANK = 16
GLA_IN = 3088
GLA_IN_PAD = 3200
GLA_LR_OFF = 3072
ATT_HEADS = 16
ATT_HD = 64
LEFT_CHUNKS = 8
MAX_REL = 128
N_REL = 257
D_FF = 4096
ALPHA = (2.0 * DEPTH) ** 0.25
LN_EPS = 1e-5
RMS_EPS = 1e-6
NEG_INF = -1e30
GLA_SCALE = GLA_DK_HEAD ** -0.5
ATT_SCALE = ATT_HD ** -0.5
ADAM_LR = 0.001
ADAM_B1 = 0.9
ADAM_B2 = 0.999
ADAM_EPS = 1e-08
ADAM_WD = 0.01
ADAM_STEP = 10

ATT_TQ = 256
ATT_KW = 768
GLA_TB = 256
GLA_GROUP = 2
VMEM_LIMIT = 56 * 1024 * 1024
WHOLE_WEIGHT_BYTES = 8 * 1024 * 1024
N_CHIPS = 4
N_DEV = 8
PACK_W = 1024


def _dot(a, b):
    return jnp.dot(a, b, preferred_element_type=F32)


def _dot_nt(a, b):
    return lax.dot_general(a, b, (((1,), (1,)), ((), ())), preferred_element_type=F32)


def _dot_tn(a, b):
    return lax.dot_general(a, b, (((0,), (0,)), ((), ())), preferred_element_type=F32)


def _cp(sem, vmem=VMEM_LIMIT):
    return pltpu.CompilerParams(dimension_semantics=sem, vmem_limit_bytes=vmem)


def _row_spec(n):
    return pl.BlockSpec((1, n), lambda *_: (0, 0))


def _sigmoid(x):
    return 1.0 / (1.0 + jnp.exp(-x))


def _log_sigmoid(x):
    return jnp.minimum(x, 0.0) - jnp.log1p(jnp.exp(-jnp.abs(x)))


class Side:
    def __init__(self, ins, out_shapes, n_sems, n_local, start, wait):
        self.ins, self.out_shapes, self.n_sems, self.n_local = list(ins), list(out_shapes), n_sems, n_local
        self.start, self.wait = start, wait

    def sem_shapes(self):
        return [pltpu.SemaphoreType.DMA((self.n_sems,)), pltpu.SemaphoreType.DMA((self.n_sems,)),
                pltpu.SemaphoreType.DMA((max(self.n_local, 1),))]


def run_side(side, name):
    n_in = len(side.ins)
    n_out = len(side.out_shapes)

    def body(*refs):
        ins, outs, sems = refs[:n_in], refs[n_in:n_in + n_out], refs[n_in + n_out:]
        side.start(ins, outs, *sems)
        side.wait(ins, outs, *sems)

    any_spec = pl.BlockSpec(memory_space=pl.ANY)
    return pl.pallas_call(body, name=name, out_shape=side.out_shapes, in_specs=[any_spec] * n_in,
                          out_specs=[any_spec] * n_out, scratch_shapes=side.sem_shapes())(*side.ins)


def hosted_call(main, side, *, name, grid, in_specs, out_specs, out_shape, scratch_shapes, dims, args):
    if side is None:
        outs = pl.pallas_call(main, name=name, grid=grid, in_specs=in_specs, out_specs=out_specs,
                              out_shape=out_shape, scratch_shapes=scratch_shapes, compiler_params=_cp(dims))(*args)
        return list(outs), []
    n_mi, n_mo, n_ms = len(in_specs), len(out_specs), len(scratch_shapes)
    n_si, n_so = len(side.ins), len(side.out_shapes)

    def kern(*refs):
        mi, si = refs[:n_mi], refs[n_mi:n_mi + n_si]
        o0 = n_mi + n_si
        mo, so = refs[o0:o0 + n_mo], refs[o0 + n_mo:o0 + n_mo + n_so]
        s0 = o0 + n_mo + n_so
        ms, sems = refs[s0:s0 + n_ms], refs[s0 + n_ms:]
        ids = [pl.program_id(d) for d in range(len(grid))]
        first = functools.reduce(jnp.logical_and, [i == 0 for i in ids])
        last = functools.reduce(jnp.logical_and, [i == g - 1 for i, g in zip(ids, grid)])

        @pl.when(first)
        def _():
            side.start(si, so, *sems)
        main(*mi, *mo, *ms)

        @pl.when(last)
        def _():
            side.wait(si, so, *sems)

    any_spec = pl.BlockSpec(memory_space=pl.ANY)
    outs = pl.pallas_call(
        kern, name=name, grid=grid, in_specs=list(in_specs) + [any_spec] * n_si,
        out_specs=list(out_specs) + [any_spec] * n_so, out_shape=list(out_shape) + side.out_shapes,
        scratch_shapes=list(scratch_shapes) + side.sem_shapes(),
        compiler_params=_cp(("arbitrary",) * len(grid)))(*args, *side.ins)
    return list(outs[:n_mo]), list(outs[n_mo:])


def modulate(x, sc, sh, name):
    S, D = x.shape
    tm = min(512, S)

    def kern(x_ref, sc_ref, sh_ref, u_ref):
        u_ref[...] = (x_ref[...] * (1.0 + sc_ref[...]) + sh_ref[...]).astype(BF16)

    return pl.pallas_call(
        kern, name=name, grid=(S // tm,),
        in_specs=[pl.BlockSpec((tm, D), lambda i: (i, 0)), _row_spec(D), _row_spec(D)],
        out_specs=pl.BlockSpec((tm, D), lambda i: (i, 0)),
        out_shape=jax.ShapeDtypeStruct((S, D), BF16),
        compiler_params=_cp(("parallel",)),
    )(x, sc, sh)


def silu_rows(c_all, name):
    def kern(c_ref, o_ref):
        c = c_ref[...]
        o_ref[...] = (c * _sigmoid(c)).astype(BF16)

    return pl.pallas_call(kern, name=name, out_shape=jax.ShapeDtypeStruct(c_all.shape, BF16))(c_all)


def sum_over_devices(g, name):
    n, R, C = g.shape

    def kern(g_ref, o_ref):
        acc = g_ref[0]
        for d in range(1, n):
            acc = acc + g_ref[d]
        o_ref[...] = acc

    return pl.pallas_call(kern, name=name, out_shape=jax.ShapeDtypeStruct((R, C), F32))(g)


def _rows_block(R, C, budget=1 << 20):
    if R * C * 4 <= budget or R % 8:
        return R
    tr = max(8, (budget // (C * 4)) // 8 * 8)
    while R % tr:
        tr -= 8
    return tr


def adamw(w, g, m, v, name):
    shape = w.shape
    C = shape[-1]
    R = w.size // C
    w2, g2, m2, v2 = (t.reshape(R, C) for t in (w, g, m, v))
    tr = _rows_block(R, C)
    c1 = 1.0 - ADAM_B1 ** ADAM_STEP
    c2 = 1.0 - ADAM_B2 ** ADAM_STEP

    def kern(w_ref, g_ref, m_ref, v_ref, d_ref, nm_ref, nv_ref):
        gg = g_ref[...]
        nm = ADAM_B1 * m_ref[...] + (1.0 - ADAM_B1) * gg
        nv = ADAM_B2 * v_ref[...] + (1.0 - ADAM_B2) * (gg * gg)
        m_hat = nm / c1
        v_hat = nv / c2
        d_ref[...] = -ADAM_LR * (m_hat / (jnp.sqrt(v_hat) + ADAM_EPS) + ADAM_WD * w_ref[...])
        nm_ref[...] = nm
        nv_ref[...] = nv

    spec = pl.BlockSpec((tr, C), lambda i: (i, 0))
    outs = pl.pallas_call(
        kern, name=name, grid=(R // tr,),
        in_specs=[spec] * 4, out_specs=[spec] * 3,
        out_shape=[jax.ShapeDtypeStruct((R, C), F32)] * 3,
        compiler_params=_cp(("parallel",)),
    )(w2, g2, m2, v2)
    return tuple(o.reshape(shape) for o in outs)


def _tn_for(N):
    for tn in (1024, 768, 640, 512, 384, 256, 128):
        if N % tn == 0:
            return tn
    return N


def mm_plain(a, b3, layer, name, *, mode="f32", nt=False, bias=None, h=None, tm=1024, side=None):
    M, K = a.shape
    N = b3.shape[1] if nt else b3.shape[2]
    if K * N * 2 <= WHOLE_WEIGHT_BYTES:
        tn, tm = N, min(tm, 512)
    else:
        tn = _tn_for(N)
    tm = min(tm, M)
    a_spec = pl.BlockSpec((tm, K), lambda j, i: (i, 0))
    if nt:
        b_spec = pl.BlockSpec((None, tn, K), lambda j, i: (layer, j, 0))
    else:
        b_spec = pl.BlockSpec((None, K, tn), lambda j, i: (layer, 0, j))
    o_spec = pl.BlockSpec((tm, tn), lambda j, i: (i, j))
    ins, in_specs = [a, b3], [a_spec, b_spec]
    if bias is not None:
        ins.append(bias)
        in_specs.append(pl.BlockSpec((1, tn), lambda j, i: (0, j)))
    if mode == "mlp_dn":
        ins.append(h)
        in_specs.append(o_spec)
    elif mode not in ("f32", "bf16", "mlp_up"):
        raise ValueError(mode)
    odt = F32 if mode == "f32" else BF16

    def kern(a_ref, b_ref, *rest):
        rest = list(rest)
        bias_ref = rest.pop(0) if bias is not None else None
        h_ref = rest.pop(0) if mode == "mlp_dn" else None
        o_ref = rest.pop(0)
        if nt:
            bt_ref = rest.pop(0)

            @pl.when(pl.program_id(1) == 0)
            def _():
                bt_ref[...] = b_ref[...].T
            acc = _dot(a_ref[...], bt_ref[...])
        else:
            acc = _dot(a_ref[...], b_ref[...].astype(BF16))
        if bias_ref is not None:
            acc = acc + bias_ref[...]
        if mode == "mlp_up":
            r = jnp.maximum(acc, 0.0)
            acc = r * r
        elif mode == "mlp_dn":
            acc = acc * (2.0 * jnp.sqrt(h_ref[...].astype(F32)))
        o_ref[...] = acc.astype(odt)

    outs, landed = hosted_call(
        kern, side, name=name, grid=(N // tn, M // tm), in_specs=in_specs, out_specs=[o_spec],
        out_shape=[jax.ShapeDtypeStruct((M, N), odt)],
        scratch_shapes=[pltpu.VMEM((K, tn), BF16)] if nt else [], dims=("parallel", "arbitrary"), args=tuple(ins))
    return outs[0] if side is None else (outs[0], landed)


def mm_down_ln(a, b3, layer, x_in, gate1p, ln_g, ln_b, sc_next, sh_next, name, *, side=None, tm=256):
    M, K = a.shape
    D = b3.shape[2]
    tm = min(tm, M)

    def kern(a_ref, b_ref, x_ref, gp_ref, lg_ref, lb_ref, sc_ref, sh_ref, y_ref, xo_ref, u_ref):
        y = _dot(a_ref[...], b_ref[...])
        y_ref[...] = y.astype(BF16)
        z = ALPHA * x_ref[...] + gp_ref[...] * y
        mu = jnp.mean(z, axis=-1, keepdims=True)
        zc = z - mu
        var = jnp.mean(zc * zc, axis=-1, keepdims=True)
        xo = (zc * lax.rsqrt(var + LN_EPS)) * lg_ref[...] + lb_ref[...]
        xo_ref[...] = xo
        u_ref[...] = (xo * (1.0 + sc_ref[...]) + sh_ref[...]).astype(BF16)

    tile = pl.BlockSpec((tm, D), lambda i: (i, 0))
    outs, landed = hosted_call(
        kern, side, name=name, grid=(M // tm,),
        in_specs=[pl.BlockSpec((tm, K), lambda i: (i, 0)), pl.BlockSpec((None, K, D), lambda i: (layer, 0, 0)), tile]
        + [_row_spec(D)] * 5,
        out_specs=[tile, tile, tile],
        out_shape=[jax.ShapeDtypeStruct((M, D), BF16), jax.ShapeDtypeStruct((M, D), F32),
                   jax.ShapeDtypeStruct((M, D), BF16)],
        scratch_shapes=[], dims=("parallel",), args=(a, b3, x_in, gate1p, ln_g, ln_b, sc_next, sh_next))
    return tuple(outs), landed


def mm_down_comb(a, b3, layer, dz, x_in, sc1p, name, *, parts=1, ln=None, side=None, tm=256):
    D, K = b3.shape[1], b3.shape[2]
    M = a.shape[-2]
    kp = K // parts
    tm = min(tm, M)
    n_ln = 0 if ln is None else 4

    def kern(*refs):
        a_refs = refs[:parts]
        b_ref, dz_ref, x_ref, sp_ref = refs[parts:parts + 4]
        ln_refs = refs[parts + 4:parts + 4 + n_ln]
        outs = refs[parts + 4 + n_ln:]

        @pl.when(pl.program_id(0) == 0)
        def _():
            for o in outs:
                if o.shape[0] == 8:
                    o[...] = jnp.zeros_like(o)
        if parts == 1:
            du = _dot_nt(a_refs[0][...], b_ref[...])
        else:
            du = _dot_nt(a_refs[0][...], b_ref[:, 0:kp])
            for p in range(1, parts):
                du = du + _dot_nt(a_refs[p][...], b_ref[:, p * kp:(p + 1) * kp])
        dx = ALPHA * dz_ref[...] + du * sp_ref[...]
        if ln is None:
            dx_ref, s_ref = outs
            dx_ref[...] = dx
        else:
            dzl_ref, dyl_ref, s_ref, sl_ref = outs
            _ln_bwd_tile(dx, *ln_refs, dzl_ref, dyl_ref, sl_ref)
        s_ref[0:1, :] += jnp.sum(du * x_ref[...], axis=0, keepdims=True)
        s_ref[1:2, :] += jnp.sum(du, axis=0, keepdims=True)
        if parts > 1:
            for p in range(parts):
                s_ref[2 + p:3 + p, :] += jnp.sum(a_refs[p][...].astype(F32), axis=0, keepdims=True)

    tile = pl.BlockSpec((tm, D), lambda i: (i, 0))
    sums = pl.BlockSpec((8, D), lambda i: (0, 0))
    if parts == 1:
        a_ins, a_specs = [a], [pl.BlockSpec((tm, K), lambda i: (i, 0))]
    else:
        assert kp == D and parts <= 6
        a_ins = [a] * parts
        a_specs = [pl.BlockSpec((None, tm, kp), functools.partial(lambda i, p: (p, i, 0), p=p)) for p in range(parts)]
    in_specs = a_specs + [pl.BlockSpec((None, D, K), lambda i: (layer, 0, 0)), tile, tile, _row_spec(D)]
    args = a_ins + [b3, dz, x_in, sc1p]
    if ln is None:
        out_specs = [tile, sums]
        out_shape = [jax.ShapeDtypeStruct((M, D), F32), jax.ShapeDtypeStruct((8, D), F32)]
    else:
        in_specs += [tile, tile, _row_spec(D), _row_spec(D)]
        args += list(ln)
        out_specs = [tile, tile, sums, sums]
        out_shape = [jax.ShapeDtypeStruct((M, D), F32), jax.ShapeDtypeStruct((M, D), BF16),
                     jax.ShapeDtypeStruct((8, D), F32), jax.ShapeDtypeStruct((8, D), F32)]
    return hosted_call(kern, side, name=name, grid=(M // tm,), in_specs=in_specs, out_specs=out_specs,
                       out_shape=out_shape, scratch_shapes=[], dims=("arbitrary",), args=tuple(args))


def mm_w(a, b, name, *, ts=2048, tk=512, chips_out=False, b_parts=1, tn=None):
    S, K = a.shape
    npart = b.shape[-1]
    N = npart * b_parts
    ts = min(ts, S)
    tk = min(tk, K)
    n_chip = N // N_CHIPS
    if tn is None:
        tn = _tn_for(n_chip if chips_out else npart)
    assert npart % tn == 0 and (not chips_out or n_chip % tn == 0)

    def kern(a_ref, b_ref, o_ref):
        @pl.when(pl.program_id(2) == 0)
        def _():
            o_ref[...] = jnp.zeros_like(o_ref)
        o_ref[...] += _dot_tn(a_ref[...], b_ref[...])

    if b_parts == 1:
        b_spec = pl.BlockSpec((ts, tn), lambda k, n, s: (s, n))
    else:
        per = npart // tn
        b_spec = pl.BlockSpec((None, ts, tn), lambda k, n, s: (n // per, s, n % per))
    if chips_out:
        per_chip = n_chip // tn
        o_spec = pl.BlockSpec((None, tk, tn), lambda k, n, s: (n // per_chip, k, n % per_chip))
        out_shape = jax.ShapeDtypeStruct((N_CHIPS, K, n_chip), F32)
    else:
        o_spec = pl.BlockSpec((tk, tn), lambda k, n, s: (k, n))
        out_shape = jax.ShapeDtypeStruct((K, N), F32)
    return pl.pallas_call(
        kern, name=name, grid=(K // tk, N // tn, S // ts),
        in_specs=[pl.BlockSpec((ts, tk), lambda k, n, s: (s, k)), b_spec],
        out_specs=o_spec, out_shape=out_shape,
        compiler_params=_cp(("parallel", "parallel", "arbitrary")),
    )(a, b)


def mm_w_chips3(a, b3, name, *, ts=512):
    S, K = a.shape
    P = b3.shape[2]
    n_chip = 3 * P // N_CHIPS
    ts = min(ts, S)
    pieces = []
    for chip in range(N_CHIPS):
        lo, hi = chip * n_chip, (chip + 1) * n_chip
        while lo < hi:
            part = lo // P
            w = min(hi, (part + 1) * P) - lo
            pieces.append((chip, lo - chip * n_chip, part, lo - part * P, w))
            lo += w

    def kern(a_ref, b_ref, o_ref):
        @pl.when(pl.program_id(0) == 0)
        def _():
            o_ref[...] = jnp.zeros_like(o_ref)
        at = a_ref[...].T
        for chip, oc, part, pc, w in pieces:
            o_ref[chip, :, oc:oc + w] += _dot(at, b_ref[part, :, pc:pc + w])

    return pl.pallas_call(
        kern, name=name, grid=(S // ts,),
        in_specs=[pl.BlockSpec((ts, K), lambda s: (s, 0)), pl.BlockSpec((3, ts, P), lambda s: (0, s, 0))],
        out_specs=pl.BlockSpec((N_CHIPS, K, n_chip), lambda s: (0, 0, 0)),
        out_shape=jax.ShapeDtypeStruct((N_CHIPS, K, n_chip), F32),
        compiler_params=_cp(("arbitrary",)),
    )(a, b3)


def mm_w_res(a, b, name, *, chips_out=False, ts=512):
    S, K = a.shape
    N = b.shape[1]
    ts = min(ts, S)
    n_chip = N // N_CHIPS

    def kern(a_ref, b_ref, o_ref):
        @pl.when(pl.program_id(0) == 0)
        def _():
            o_ref[...] = jnp.zeros_like(o_ref)
        at = a_ref[...].T
        if chips_out:
            for chip in range(N_CHIPS):
                o_ref[chip] += _dot(at, b_ref[:, chip * n_chip:(chip + 1) * n_chip])
        else:
            o_ref[...] += _dot(at, b_ref[...])

    o_shape = (N_CHIPS, K, n_chip) if chips_out else (K, N)
    return pl.pallas_call(
        kern, name=name, grid=(S // ts,),
        in_specs=[pl.BlockSpec((ts, K), lambda s: (s, 0)), pl.BlockSpec((ts, N), lambda s: (s, 0))],
        out_specs=pl.BlockSpec(o_shape, lambda s: (0,) * len(o_shape)),
        out_shape=jax.ShapeDtypeStruct(o_shape, F32),
        compiler_params=_cp(("arbitrary",)),
    )(a, b)


def mm_f32(a, b, name):
    def kern(a_ref, b_ref, o_ref):
        o_ref[...] = jnp.dot(a_ref[...], b_ref[...], precision=HIGHEST, preferred_element_type=F32)

    return pl.pallas_call(kern, name=name, out_shape=jax.ShapeDtypeStruct((a.shape[0], b.shape[1]), F32),
                          compiler_params=pltpu.CompilerParams(vmem_limit_bytes=VMEM_LIMIT))(a, b)


def _ln_bwd_tile(dxo_t, x_ref, y_ref, gp_ref, lg_ref, dz_ref, dy_ref, s_ref):
    yv = y_ref[...].astype(F32)
    z = ALPHA * x_ref[...] + gp_ref[...] * yv
    mu = jnp.mean(z, axis=-1, keepdims=True)
    zc = z - mu
    var = jnp.mean(zc * zc, axis=-1, keepdims=True)
    rstd = lax.rsqrt(var + LN_EPS)
    xhat = zc * rstd
    dxh = dxo_t * lg_ref[...]
    dz = rstd * (dxh - jnp.mean(dxh, axis=-1, keepdims=True)
                 - xhat * jnp.mean(dxh * xhat, axis=-1, keepdims=True))
    dz_ref[...] = dz
    dy_ref[...] = (gp_ref[...] * dz).astype(BF16)
    s_ref[0:1, :] += jnp.sum(dxo_t * xhat, axis=0, keepdims=True)
    s_ref[1:2, :] += jnp.sum(dxo_t, axis=0, keepdims=True)
    s_ref[2:3, :] += jnp.sum(dz * yv, axis=0, keepdims=True)


def loss_ln_bwd(x_out, target, x_in, y, gate1p, ln_g, name, *, tm=256):
    S, D = x_out.shape
    tm = min(tm, S)

    def kern(xo_ref, t_ref, x_ref, y_ref, gp_ref, lg_ref, dz_ref, dy_ref, s_ref, l_ref):
        @pl.when(pl.program_id(0) == 0)
        def _():
            s_ref[...] = jnp.zeros_like(s_ref)
            l_ref[...] = jnp.zeros_like(l_ref)
        e = xo_ref[...] - t_ref[...]
        l_ref[...] += jnp.sum(e * e)
        _ln_bwd_tile(e * (1.0 / D), x_ref, y_ref, gp_ref, lg_ref, dz_ref, dy_ref, s_ref)

    tile = pl.BlockSpec((tm, D), lambda i: (i, 0))
    return pl.pallas_call(
        kern, name=name, grid=(S // tm,),
        in_specs=[tile, tile, tile, tile, _row_spec(D), _row_spec(D)],
        out_specs=[tile, tile, pl.BlockSpec((8, D), lambda i: (0, 0)), pl.BlockSpec((8, 128), lambda i: (0, 0))],
        out_shape=[jax.ShapeDtypeStruct((S, D), F32), jax.ShapeDtypeStruct((S, D), BF16),
                   jax.ShapeDtypeStruct((8, D), F32), jax.ShapeDtypeStruct((8, 128), F32)],
        compiler_params=_cp(("arbitrary",)),
    )(x_out, target, x_in, y, gate1p, ln_g)


def _tri64():
    r = lax.broadcasted_iota(jnp.int32, (CHUNK, CHUNK), 0)
    c = lax.broadcasted_iota(jnp.int32, (CHUNK, CHUNK), 1)
    return r >= c


def _gla_chunk_common(proj_ref, rows, b, h):
    kc = slice(h * GLA_DK_HEAD, (h + 1) * GLA_DK_HEAD)
    bh = b[:, kc]
    ep = jnp.exp(bh)
    en = jnp.exp(-bh)
    bl = bh[CHUNK - 1:CHUNK, :]
    ee = jnp.exp(bl - bh)
    dec = jnp.exp(bl)
    q = proj_ref[rows, h * GLA_DK_HEAD:(h + 1) * GLA_DK_HEAD] * GLA_SCALE
    k = proj_ref[rows, GLA_DK + h * GLA_DK_HEAD:GLA_DK + (h + 1) * GLA_DK_HEAD]
    v = proj_ref[rows, 2 * GLA_DK + h * GLA_DV_HEAD:2 * GLA_DK + (h + 1) * GLA_DV_HEAD]
    g = proj_ref[rows, 2 * GLA_DK + GLA_DV + h * GLA_DV_HEAD:2 * GLA_DK + GLA_DV + (h + 1) * GLA_DV_HEAD]
    return ep, en, ee, dec, q, k, v, g


def gla_fwd(proj, wgk_p, bgk, gnorm, name, side=None):
    S = proj.shape[0]
    TB = min(GLA_TB, S)
    ncb = TB // CHUNK

    def kern(proj_ref, wgk_ref, bgk_ref, gn_ref, zg_ref, st_ref, state_scr, la_scr):
        @pl.when(pl.program_id(0) == 0)
        def _():
            state_scr[...] = jnp.zeros_like(state_scr)
        lr = proj_ref[:, GLA_LR_OFF:GLA_IN_PAD].astype(BF16)
        gk = _dot(lr, wgk_ref[...]) + bgk_ref[...]
        la_scr[...] = _log_sigmoid(gk) * (1.0 / 16.0)
        lower = _tri64()
        tri = lower.astype(F32)

        def group(gi, carry):
            rows = [pl.ds(pl.multiple_of((gi * GLA_GROUP + g) * CHUNK, CHUNK), CHUNK) for g in range(GLA_GROUP)]
            b = [jnp.dot(tri, la_scr[r, :], precision=HIGHEST, preferred_element_type=F32) for r in rows]
            P = [(g, h) for g in range(GLA_GROUP) for h in range(GLA_HEADS)]
            cm = {p: _gla_chunk_common(proj_ref, rows[p[0]], b[p[0]], p[1]) for p in P}
            qf = {p: (cm[p][4] * cm[p][0]).astype(BF16) for p in P}
            kn = {p: (cm[p][5] * cm[p][1]).astype(BF16) for p in P}
            qn = {p: (cm[p][4] * cm[p][1]).astype(BF16) for p in P}
            kp = {p: (cm[p][5] * cm[p][0]).astype(BF16) for p in P}
            ke = {p: (cm[p][5] * cm[p][2]).astype(BF16) for p in P}
            vb = {p: cm[p][6].astype(BF16) for p in P}
            a_f = {p: _dot_nt(qf[p], kn[p]) for p in P}
            a_b = {p: _dot_nt(qn[p], kp[p]) for p in P}
            upd = {p: _dot_tn(vb[p], ke[p]) for p in P}
            st = {(0, h): state_scr[h] for h in range(GLA_HEADS)}
            for g in range(GLA_GROUP):
                for h in range(GLA_HEADS):
                    st[(g + 1, h)] = st[(g, h)] * cm[(g, h)][3] + upd[(g, h)]
            o_st = {p: _dot_nt(qf[p], st[p].astype(BF16)) for p in P}
            amat = {p: jnp.where(lower, a_f[p], a_b[p]).astype(BF16) for p in P}
            o = {p: _dot(amat[p], vb[p]) + o_st[p] for p in P}
            for g, h in P:
                st_ref[gi * GLA_GROUP + g, h] = st[(g, h)]
            for h in range(GLA_HEADS):
                state_scr[h] = st[(GLA_GROUP, h)]
            for g, h in P:
                gate = cm[(g, h)][7]
                vc = slice(h * GLA_DV_HEAD, (h + 1) * GLA_DV_HEAD)
                r = lax.rsqrt(jnp.mean(o[(g, h)] * o[(g, h)], axis=-1, keepdims=True) + RMS_EPS)
                on = (o[(g, h)] * r) * gn_ref[:, vc]
                zg_ref[rows[g], vc] = (on * (gate * _sigmoid(gate))).astype(BF16)
            return carry

        lax.fori_loop(0, ncb // GLA_GROUP, group, 0)

    return hosted_call(
        kern, side, name=name, grid=(S // TB,),
        in_specs=[pl.BlockSpec((TB, GLA_IN_PAD), lambda i: (i, 0)),
                  pl.BlockSpec((128, GLA_DK), lambda i: (0, 0)), _row_spec(GLA_DK), _row_spec(GLA_DV)],
        out_specs=[pl.BlockSpec((TB, GLA_DV), lambda i: (i, 0)),
                   pl.BlockSpec((ncb, GLA_HEADS, GLA_DV_HEAD, GLA_DK_HEAD), lambda i: (i, 0, 0, 0))],
        out_shape=[jax.ShapeDtypeStruct((S, GLA_DV), BF16),
                   jax.ShapeDtypeStruct((S // CHUNK, GLA_HEADS, GLA_DV_HEAD, GLA_DK_HEAD), F32)],
        scratch_shapes=[pltpu.VMEM((GLA_HEADS, GLA_DV_HEAD, GLA_DK_HEAD), F32), pltpu.VMEM((TB, GLA_DK), F32)],
        dims=("arbitrary",), args=(proj, wgk_p, bgk, gnorm))


def gla_bwd(proj, states, dzg, wgk_p, bgk, gnorm, name, side=None):
    S = proj.shape[0]
    TB = min(GLA_TB, S)
    ncb = TB // CHUNK
    nb = S // TB

    def kern(proj_ref, st_ref, dzg_ref, wgk_ref, bgk_ref, gn_ref,
             dproj_ref, dwgk_ref, dbgk_ref, dgn_ref, dstate_scr, la_scr, gk_scr, dgk_scr):
        @pl.when(pl.program_id(0) == 0)
        def _():
            dstate_scr[...] = jnp.zeros_like(dstate_scr)
            dwgk_ref[...] = jnp.zeros_like(dwgk_ref)
            dbgk_ref[...] = jnp.zeros_like(dbgk_ref)
            dgn_ref[...] = jnp.zeros_like(dgn_ref)
        lr = proj_ref[:, GLA_LR_OFF:GLA_IN_PAD].astype(BF16)
        gk = _dot(lr, wgk_ref[...]) + bgk_ref[...]
        gk_scr[...] = gk
        la_scr[...] = _log_sigmoid(gk) * (1.0 / 16.0)
        lower = _tri64()
        tri = lower.astype(F32)
        r_i = lax.broadcasted_iota(jnp.int32, (CHUNK, CHUNK), 0)
        c_i = lax.broadcasted_iota(jnp.int32, (CHUNK, CHUNK), 1)
        triu = (c_i >= r_i).astype(F32)
        last_row = lax.broadcasted_iota(jnp.int32, (CHUNK, GLA_DK_HEAD), 0) == CHUNK - 1

        def group(gi, carry):
            cs = [ncb - 1 - (gi * GLA_GROUP + g) for g in range(GLA_GROUP)]
            rows = [pl.ds(pl.multiple_of(c * CHUNK, CHUNK), CHUNK) for c in cs]
            b = [jnp.dot(tri, la_scr[r, :], precision=HIGHEST, preferred_element_type=F32) for r in rows]
            P = [(g, h) for g in range(GLA_GROUP) for h in range(GLA_HEADS)]
            kcs = [slice(h * GLA_DK_HEAD, (h + 1) * GLA_DK_HEAD) for h in range(GLA_HEADS)]
            vcs = [slice(h * GLA_DV_HEAD, (h + 1) * GLA_DV_HEAD) for h in range(GLA_HEADS)]
            cm = {p: _gla_chunk_common(proj_ref, rows[p[0]], b[p[0]], p[1]) for p in P}
            ep, en, ee, dec = ({p: cm[p][i] for p in P} for i in range(4))
            qf = {p: cm[p][4] * cm[p][0] for p in P}
            kn = {p: cm[p][5] * cm[p][1] for p in P}
            qn = {p: cm[p][4] * cm[p][1] for p in P}
            kp = {p: cm[p][5] * cm[p][0] for p in P}
            ke = {p: cm[p][5] * cm[p][2] for p in P}
            qf_b, kn_b, qn_b, kp_b, ke_b = ({p: t[p].astype(BF16) for p in P} for t in (qf, kn, qn, kp, ke))
            vb = {p: cm[p][6].astype(BF16) for p in P}
            st = {p: st_ref[cs[p[0]], p[1]] for p in P}
            st_b = {p: st[p].astype(BF16) for p in P}
            a_f = {p: _dot_nt(qf_b[p], kn_b[p]) for p in P}
            a_b = {p: _dot_nt(qn_b[p], kp_b[p]) for p in P}
            o_st = {p: _dot_nt(qf_b[p], st_b[p]) for p in P}
            amat = {p: jnp.where(lower, a_f[p], a_b[p]).astype(BF16) for p in P}
            o = {p: _dot(amat[p], vb[p]) + o_st[p] for p in P}
            do_b, dgs = {}, {}
            for p in P:
                g, h = p
                r = lax.rsqrt(jnp.mean(o[p] * o[p], axis=-1, keepdims=True) + RMS_EPS)
                oh = o[p] * r
                gn = gn_ref[:, vcs[h]]
                gate = cm[p][7]
                sg = _sigmoid(gate)
                dz = dzg_ref[rows[g], vcs[h]]
                don = dz * (gate * sg)
                dgs[p] = dz * (oh * gn) * (sg * (1.0 + gate * (1.0 - sg)))
                dgn_ref[:, vcs[h]] += jnp.sum(don * oh, axis=0, keepdims=True)
                doh = don * gn
                do_b[p] = (r * (doh - oh * jnp.mean(doh * oh, axis=-1, keepdims=True))).astype(BF16)
            da = {p: _dot_nt(do_b[p], vb[p]) for p in P}
            dv_a = {p: _dot_tn(amat[p], do_b[p]) for p in P}
            dqf_st = {p: _dot(do_b[p], st_b[p]) for p in P}
            dst_upd = {p: _dot_tn(do_b[p], qf_b[p]) for p in P}
            dst = {(0, h): dstate_scr[h] for h in range(GLA_HEADS)}
            for g in range(GLA_GROUP):
                for h in range(GLA_HEADS):
                    dst[(g + 1, h)] = dst[(g, h)] * dec[(g, h)] + dst_upd[(g, h)]
            for h in range(GLA_HEADS):
                dstate_scr[h] = dst[(GLA_GROUP, h)]
            dst_b = {p: dst[p].astype(BF16) for p in P}
            dv = {p: dv_a[p] + _dot_nt(ke_b[p], dst_b[p]) for p in P}
            dke = {p: _dot(vb[p], dst_b[p]) for p in P}
            da_f = {p: jnp.where(lower, da[p], 0.0).astype(BF16) for p in P}
            da_b = {p: jnp.where(lower, 0.0, da[p]).astype(BF16) for p in P}
            dqf = {p: _dot(da_f[p], kn_b[p]) + dqf_st[p] for p in P}
            dkn = {p: _dot_tn(da_f[p], qf_b[p]) for p in P}
            dqn = {p: _dot(da_b[p], kp_b[p]) for p in P}
            dkp = {p: _dot_tn(da_b[p], qn_b[p]) for p in P}
            dbs = {}
            for p in P:
                ddec = jnp.sum(dst[p] * st[p], axis=0, keepdims=True)
                db = dqf[p] * qf[p] - dkn[p] * kn[p] - dqn[p] * qn[p] + dkp[p] * kp[p] - dke[p] * ke[p]
                dbl = jnp.sum(dke[p] * ke[p], axis=0, keepdims=True) + ddec * dec[p]
                dbs[p] = db + jnp.where(last_row, dbl, 0.0)
            dla = {p: jnp.dot(triu, dbs[p], precision=HIGHEST, preferred_element_type=F32) for p in P}
            for p in P:
                g, h = p
                dq = (dqf[p] * ep[p] + dqn[p] * en[p]) * GLA_SCALE
                dk = dkn[p] * en[p] + dkp[p] * ep[p] + dke[p] * ee[p]
                dgk_scr[rows[g], kcs[h]] = dla[p] * (1.0 / 16.0) * _sigmoid(-gk_scr[rows[g], kcs[h]])
                dproj_ref[rows[g], kcs[h]] = dq.astype(BF16)
                dproj_ref[rows[g], GLA_DK + h * GLA_DK_HEAD:GLA_DK + (h + 1) * GLA_DK_HEAD] = dk.astype(BF16)
                dproj_ref[rows[g], 2 * GLA_DK + h * GLA_DV_HEAD:2 * GLA_DK + (h + 1) * GLA_DV_HEAD] = dv[p].astype(BF16)
                dproj_ref[rows[g], 2 * GLA_DK + GLA_DV + h * GLA_DV_HEAD:
                          2 * GLA_DK + GLA_DV + (h + 1) * GLA_DV_HEAD] = dgs[p].astype(BF16)
            return carry

        lax.fori_loop(0, ncb // GLA_GROUP, group, 0)
        dgk = dgk_scr[...]
        dgk_b = dgk.astype(BF16)
        dproj_ref[:, GLA_LR_OFF:GLA_IN_PAD] = _dot_nt(dgk_b, wgk_ref[...]).astype(BF16)
        dwgk_ref[...] += _dot_tn(lr, dgk_b)
        dbgk_ref[...] += jnp.sum(dgk, axis=0, keepdims=True)

    rev = lambda i: (nb - 1 - i, 0)
    return hosted_call(
        kern, side, name=name, grid=(nb,),
        in_specs=[pl.BlockSpec((TB, GLA_IN_PAD), rev),
                  pl.BlockSpec((ncb, GLA_HEADS, GLA_DV_HEAD, GLA_DK_HEAD), lambda i: (nb - 1 - i, 0, 0, 0)),
                  pl.BlockSpec((TB, GLA_DV), rev),
                  pl.BlockSpec((128, GLA_DK), lambda i: (0, 0)), _row_spec(GLA_DK), _row_spec(GLA_DV)],
        out_specs=[pl.BlockSpec((TB, GLA_IN_PAD), rev),
                   pl.BlockSpec((128, GLA_DK), lambda i: (0, 0)), _row_spec(GLA_DK), _row_spec(GLA_DV)],
        out_shape=[jax.ShapeDtypeStruct((S, GLA_IN_PAD), BF16), jax.ShapeDtypeStruct((128, GLA_DK), F32),
                   jax.ShapeDtypeStruct((1, GLA_DK), F32), jax.ShapeDtypeStruct((1, GLA_DV), F32)],
        scratch_shapes=[pltpu.VMEM((GLA_HEADS, GLA_DV_HEAD, GLA_DK_HEAD), F32), pltpu.VMEM((TB, GLA_DK), F32),
                        pltpu.VMEM((TB, GLA_DK), F32), pltpu.VMEM((TB, GLA_DK), F32)],
        dims=("arbitrary",), args=(proj, states, dzg, wgk_p, bgk, gnorm))


ATT_TW = 1024
ATT_CLASSES = 3


def _att_window(i):
    return pl.multiple_of(jnp.maximum(i * ATT_TQ - LEFT_CHUNKS * CHUNK, 0), ATT_TQ)


def _att_rel_index():
    e = jnp.arange(ATT_TW)[None, :]
    d = jnp.where(e < ATT_KW, e, e - ATT_TW)
    off = (jnp.arange(ATT_CLASSES) * ATT_TQ)[:, None]
    return jnp.clip(off - d, -MAX_REL, MAX_REL) + MAX_REL


def _row_bits():
    return lax.broadcasted_iota(jnp.int32, (ATT_TQ, ATT_TW), 0)


def att_bias_tiles(rel_bias, name):
    pick = (jnp.arange(384)[:, None] == _att_rel_index().reshape(-1)[None, :]).astype(F32)
    tab = mm_f32(jnp.pad(rel_bias, ((0, 0), (0, 384 - N_REL))), pick, name + "_tab")
    tab = tab.reshape(ATT_HEADS * ATT_CLASSES, 1, ATT_TW)

    def kern(t_ref, o_ref):
        cls = pl.program_id(0) % ATT_CLASSES
        x = jnp.broadcast_to(t_ref[...], (ATT_TQ, ATT_TW))
        x = pltpu.roll(x, 0, 1, stride=1, stride_axis=0)
        x = x[:, :ATT_KW]
        qc = cls * (ATT_TQ // CHUNK) + lax.shift_right_arithmetic(
            lax.broadcasted_iota(jnp.int32, (ATT_TQ, ATT_KW), 0), 6)
        kc = lax.shift_right_arithmetic(lax.broadcasted_iota(jnp.int32, (ATT_TQ, ATT_KW), 1), 6)
        o_ref[...] = jnp.where((kc <= qc) & (kc >= qc - LEFT_CHUNKS), x, NEG_INF)

    return pl.pallas_call(
        kern, name=name, grid=(ATT_HEADS * ATT_CLASSES,),
        in_specs=[pl.BlockSpec((None, 1, ATT_TW), lambda i: (i, 0, 0))],
        out_specs=pl.BlockSpec((None, ATT_TQ, ATT_KW), lambda i: (i, 0, 0)),
        out_shape=jax.ShapeDtypeStruct((ATT_HEADS * ATT_CLASSES, ATT_TQ, ATT_KW), F32),
        compiler_params=_cp(("parallel",)),
    )(tab)


def att_bias_grad(dbt, name):
    def kern(d_ref, o_ref):
        x = jnp.concatenate([d_ref[...], jnp.zeros((ATT_TQ, ATT_TW - ATT_KW), F32)], axis=1)
        row = _row_bits()
        for b in range(8):
            x = jnp.where((row & (1 << b)) != 0, pltpu.roll(x, ATT_TW - (1 << b), axis=1), x)
        o_ref[...] = jnp.sum(x, axis=0, keepdims=True)

    diag = pl.pallas_call(
        kern, name=name + "_diag", grid=(ATT_HEADS * ATT_CLASSES,),
        in_specs=[pl.BlockSpec((None, ATT_TQ, ATT_KW), lambda i: (i, 0, 0))],
        out_specs=pl.BlockSpec((None, 1, ATT_TW), lambda i: (i, 0, 0)),
        out_shape=jax.ShapeDtypeStruct((ATT_HEADS * ATT_CLASSES, 1, ATT_TW), F32),
        compiler_params=_cp(("parallel",)),
    )(dbt)
    diag = diag.reshape(ATT_HEADS, ATT_CLASSES * ATT_TW)
    onehot = (_att_rel_index().reshape(-1)[:, None] == jnp.arange(384)[None, :]).astype(F32)
    return mm_f32(diag, onehot, name + "_bins")[:, :N_REL]


ATT_GROUP = 2


def _att_scores(q_ref, k_ref, bias_refs, blk0):
    G = range(ATT_GROUP)
    hs = [slice(hh * ATT_HD, (hh + 1) * ATT_HD) for hh in range(2)]
    rows = [slice(g * ATT_TQ, (g + 1) * ATT_TQ) for g in G]
    wins = [pl.ds(_att_window(blk0 + g), ATT_KW) for g in G]
    kw = [k_ref[w, :] for w in wins]
    P = [(g, hh) for g in G for hh in range(2)]
    q = {p: q_ref[rows[p[0]], hs[p[1]]] * ATT_SCALE for p in P}
    k = {p: kw[p[0]][:, hs[p[1]]] for p in P}
    s = {p: _dot_nt(q[p], k[p]) + bias_refs[p[0]][p[1]] for p in P}
    e = {p: jnp.exp(s[p] - jnp.max(s[p], axis=-1, keepdims=True)) for p in P}
    inv = {p: 1.0 / jnp.sum(e[p], axis=-1, keepdims=True) for p in P}
    return P, rows, wins, hs, q, k, e, inv


def _att_specs(S):
    nq = D_MODEL // 128
    q_spec = pl.BlockSpec((ATT_GROUP * ATT_TQ, 128), lambda p, i: (i, p))
    k_spec = pl.BlockSpec((S, 128), lambda p, i: (0, nq + p))
    v_spec = pl.BlockSpec((S, 128), lambda p, i: (0, 2 * nq + p))
    b_specs = [pl.BlockSpec((2, None, ATT_TQ, ATT_KW),
                            functools.partial(lambda p, i, g: (p, jnp.minimum(ATT_GROUP * i + g, ATT_CLASSES - 1), 0, 0), g=g))
               for g in range(ATT_GROUP)]
    return q_spec, k_spec, v_spec, b_specs


def attn_fwd(qkv, bias, name, side=None):
    S = qkv.shape[0]
    q_spec, k_spec, v_spec, b_specs = _att_specs(S)

    def kern(q_ref, k_ref, v_ref, *rest):
        bias_refs, o_ref = rest[:ATT_GROUP], rest[ATT_GROUP]
        P, rows, wins, hs, _, _, e, inv = _att_scores(q_ref, k_ref, bias_refs, ATT_GROUP * pl.program_id(1))
        vw = [v_ref[w, :] for w in wins]
        o = {p: _dot(e[p].astype(BF16), vw[p[0]][:, hs[p[1]]]) * inv[p] for p in P}
        for g in range(ATT_GROUP):
            o_ref[rows[g], :] = jnp.concatenate([o[(g, 0)], o[(g, 1)]], axis=1).astype(BF16)

    return hosted_call(
        kern, side, name=name, grid=(ATT_HEADS // 2, S // (ATT_GROUP * ATT_TQ)),
        in_specs=[q_spec, k_spec, v_spec] + b_specs,
        out_specs=[pl.BlockSpec((ATT_GROUP * ATT_TQ, 128), lambda p, i: (i, p))],
        out_shape=[jax.ShapeDtypeStruct((S, D_MODEL), BF16)],
        scratch_shapes=[], dims=("parallel", "arbitrary"), args=(qkv, qkv, qkv) + (bias,) * ATT_GROUP)


def attn_bwd(qkv, bias, do, name, side=None):
    S = qkv.shape[0]
    nstep = S // (ATT_GROUP * ATT_TQ)
    q_spec, k_spec, v_spec, b_specs = _att_specs(S)

    def kern(q_ref, k_ref, v_ref, *rest):
        bias_refs = rest[:ATT_GROUP]
        do_ref, dqkv_ref, db_ref, dk_scr, dv_scr = rest[ATT_GROUP:]
        i = pl.program_id(1)

        @pl.when(i == 0)
        def _():
            dk_scr[...] = jnp.zeros_like(dk_scr)
            dv_scr[...] = jnp.zeros_like(dv_scr)
            db_ref[...] = jnp.zeros_like(db_ref)
        blk0 = ATT_GROUP * i
        P, rows, wins, hs, q, k, e, inv = _att_scores(q_ref, k_ref, bias_refs, blk0)
        vw = [v_ref[w, :] for w in wins]
        do_h = {p: do_ref[rows[p[0]], hs[p[1]]] for p in P}
        dp = {p: _dot_nt(do_h[p], vw[p[0]][:, hs[p[1]]]) for p in P}
        pr = {p: e[p] * inv[p] for p in P}
        dvs = {p: _dot_tn(pr[p].astype(BF16), do_h[p]) for p in P}
        ds = {p: pr[p] * (dp[p] - jnp.sum(pr[p] * dp[p], axis=-1, keepdims=True)) for p in P}
        ds_b = {p: ds[p].astype(BF16) for p in P}
        dqs = {p: _dot(ds_b[p], k[p]) * ATT_SCALE for p in P}
        dks = {p: _dot_tn(ds_b[p], q[p]) for p in P}
        for g, hh in P:
            db_ref[hh, jnp.minimum(blk0 + g, ATT_CLASSES - 1)] += ds[(g, hh)]
        for g in range(ATT_GROUP):
            first = pl.multiple_of((blk0 + g) * ATT_TQ, ATT_TQ)
            dqkv_ref[0, pl.ds(first, ATT_TQ), :] = jnp.concatenate([dqs[(g, 0)], dqs[(g, 1)]], axis=1).astype(BF16)
            dk_scr[wins[g], :] += jnp.concatenate([dks[(g, 0)], dks[(g, 1)]], axis=1)
            dv_scr[wins[g], :] += jnp.concatenate([dvs[(g, 0)], dvs[(g, 1)]], axis=1)

        @pl.when(i == nstep - 1)
        def _():
            dqkv_ref[1] = dk_scr[...].astype(BF16)
            dqkv_ref[2] = dv_scr[...].astype(BF16)

    return hosted_call(
        kern, side, name=name, grid=(ATT_HEADS // 2, nstep),
        in_specs=[q_spec, k_spec, v_spec] + b_specs + [pl.BlockSpec((ATT_GROUP * ATT_TQ, 128), lambda p, i: (i, p))],
        out_specs=[pl.BlockSpec((3, S, 128), lambda p, i: (0, 0, p)),
                   pl.BlockSpec((2, ATT_CLASSES, ATT_TQ, ATT_KW), lambda p, i: (p, 0, 0, 0))],
        out_shape=[jax.ShapeDtypeStruct((3, S, D_MODEL), BF16),
                   jax.ShapeDtypeStruct((ATT_HEADS, ATT_CLASSES, ATT_TQ, ATT_KW), F32)],
        scratch_shapes=[pltpu.VMEM((S, 128), F32), pltpu.VMEM((S, 128), F32)],
        dims=("parallel", "arbitrary"), args=(qkv, qkv, qkv) + (bias,) * ATT_GROUP + (do,))


def _me():
    return lax.axis_index("x"), lax.axis_index("y"), lax.axis_index("c")


def _other_chips(x, y):
    return [(1 - x, y), (x, 1 - y), (1 - x, 1 - y)]


def all_gather8(x_shard, name):
    m_per, n = x_shard.shape

    def body(x_ref, out_ref, send_sems, recv_sems, local_sem):
        x, y, c = _me()
        me, sibling = (x, y, c), (x, y, 1 - c)
        chips = _other_chips(x, y)

        def rows(px, py, pc):
            return out_ref.at[pl.ds((4 * px + 2 * py + pc) * m_per, m_per), :]

        def copy(k, block, to, src=None):
            return pltpu.make_async_remote_copy(
                src_ref=rows(*block) if src is None else src, dst_ref=rows(*block),
                send_sem=send_sems.at[k], recv_sem=recv_sems.at[k], device_id=to, device_id_type=MESH)

        mine = pltpu.make_async_copy(x_ref, rows(*me), local_sem)
        mine.start()
        first = [copy(0, me, sibling, src=x_ref)]
        first += [copy(1 + j, me, (*chip, c), src=x_ref) for j, chip in enumerate(chips)]
        for cp in first:
            cp.start()
        passed = [copy(4 + j, (*chip, c), sibling) for j, chip in enumerate(chips)]
        for j, chip in enumerate(chips):
            copy(1 + j, (*chip, c), me).wait_recv()
            passed[j].start()
        copy(0, sibling, me).wait_recv()
        for j, chip in enumerate(chips):
            copy(4 + j, (*chip, 1 - c), me).wait_recv()
        for cp in first + passed:
            cp.wait_send()
        mine.wait()

    return pl.pallas_call(
        body, name=name,
        out_shape=jax.ShapeDtypeStruct((N_DEV * m_per, n), x_shard.dtype),
        in_specs=[pl.BlockSpec(memory_space=pltpu.VMEM)],
        out_specs=pl.BlockSpec(memory_space=pltpu.VMEM),
        scratch_shapes=[pltpu.SemaphoreType.DMA((7,)), pltpu.SemaphoreType.DMA((7,)), pltpu.SemaphoreType.DMA],
        compiler_params=pltpu.CompilerParams(vmem_limit_bytes=VMEM_LIMIT),
    )(x_shard)


def _half_rows(n_rows, c):
    h = n_rows // 2
    return pl.ds(c * h, h)


def _gathered_shape(shape, flavour):
    L, a, b = shape
    return {"col": (L, a, N_CHIPS * b), "row": (L, N_CHIPS * a, b), "lead": (N_CHIPS, L, a, b)}[flavour]


def _gathered_part(out_ref, shape, flavour, s, rows):
    L, a, b = shape
    if flavour == "col":
        return out_ref.at[:, rows, pl.ds(s * b, b)]
    if flavour == "row":
        return out_ref.at[:, pl.ds(s * a + rows.start, rows.size), :]
    return out_ref.at[s, :, rows, :]


def gather_side(shards, flavours):
    n = len(shards)
    shapes = [w.shape for w in shards]

    def copies(w_refs, out_refs, send_sems, recv_sems, local_sems):
        x, y, c = _me()
        sibling = (x, y, 1 - c)
        chips = _other_chips(x, y)
        me_s = 2 * x + y

        def copy(k, src, dst, to):
            return pltpu.make_async_remote_copy(src_ref=src, dst_ref=dst, send_sem=send_sems.at[k],
                                                recv_sem=recv_sems.at[k], device_id=to, device_id_type=MESH)

        own, first, landed, passed, passed_in = [], [], [], [], []
        for w in range(n):
            shp, fl = shapes[w], flavours[w]
            my_half = _half_rows(shp[1], c)
            sib_half = _half_rows(shp[1], 1 - c)
            own.append(copy(7 * w + 6, w_refs[w], _gathered_part(out_refs[w], shp, fl, me_s, pl.ds(0, shp[1])), sibling))
            for j, chip in enumerate(chips):
                s = 2 * chip[0] + chip[1]
                first.append(copy(7 * w + j, w_refs[w].at[:, my_half, :],
                                  _gathered_part(out_refs[w], shp, fl, me_s, my_half), (*chip, c)))
                part = _gathered_part(out_refs[w], shp, fl, s, my_half)
                landed.append(copy(7 * w + j, part, part, (*chip, c)))
                passed.append(copy(7 * w + 3 + j, part, part, sibling))
                theirs = _gathered_part(out_refs[w], shp, fl, s, sib_half)
                passed_in.append(copy(7 * w + 3 + j, theirs, theirs, sibling))
        return own, first, landed, passed, passed_in

    def start(*refs):
        own, first, _, _, _ = copies(*refs)
        for cp in first + own:
            cp.start()

    def wait(*refs):
        own, first, landed, passed, passed_in = copies(*refs)
        for arrived, onward in zip(landed, passed):
            arrived.wait_recv()
            onward.start()
        for cp in passed_in:
            cp.wait_recv()
        for cp in own:
            cp.wait()
        for cp in first + passed:
            cp.wait_send()

    out_shapes = [jax.ShapeDtypeStruct(_gathered_shape(s, f), w.dtype) for w, s, f in zip(shards, shapes, flavours)]
    return Side(shards, out_shapes, 7 * n, 0, start, wait)


def swap_side(gs):
    n = len(gs)

    def copies(g_refs, out_refs, send_sems, recv_sems, local_sems):
        x, y, c = _me()
        return [pltpu.make_async_remote_copy(
            src_ref=g_refs[w].at[:, _half_rows(gs[w].shape[1], 1 - c), :], dst_ref=out_refs[w],
            send_sem=send_sems.at[w], recv_sem=recv_sems.at[w], device_id=(x, y, 1 - c), device_id_type=MESH)
            for w in range(n)]

    def start(*refs):
        for cp in copies(*refs):
            cp.start()

    def wait(*refs):
        for cp in copies(*refs):
            cp.wait()

    out_shapes = [jax.ShapeDtypeStruct((g.shape[0], g.shape[1] // 2, g.shape[2]), g.dtype) for g in gs]
    return Side(gs, out_shapes, n, 0, start, wait)


def add_half(g, r1, c_idx, name):
    n, R, C = g.shape
    half = R // 2
    tr = _rows_block(half, C)
    nbh = half // tr

    def kern(c_ref, g_ref, r_ref, o_ref, ob_ref):
        p = g_ref[...] + r_ref[...]
        o_ref[...] = p
        ob_ref[...] = p.astype(BF16)

    spec = pl.BlockSpec((1, tr, C), lambda d, r, c_ref: (d, r, 0))
    return pl.pallas_call(
        kern, name=name,
        grid_spec=pltpu.PrefetchScalarGridSpec(
            num_scalar_prefetch=1, grid=(n, nbh),
            in_specs=[pl.BlockSpec((1, tr, C), lambda d, r, c_ref: (d, c_ref[0] * nbh + r, 0)), spec],
            out_specs=[spec, spec]),
        out_shape=[jax.ShapeDtypeStruct((n, half, C), F32), jax.ShapeDtypeStruct((n, half, C), BF16)],
        compiler_params=_cp(("parallel", "parallel")),
    )(c_idx, g, r1)


def exchange_side(ps):
    n = len(ps)

    def copies(p_refs, out_refs, send_sems, recv_sems, local_sems):
        x, y, c = _me()
        return [pltpu.make_async_remote_copy(
            src_ref=p_refs[w].at[2 * chip[0] + chip[1]], dst_ref=out_refs[w].at[j],
            send_sem=send_sems.at[3 * w + j], recv_sem=recv_sems.at[3 * w + j],
            device_id=(*chip, c), device_id_type=MESH)
            for w in range(n) for j, chip in enumerate(_other_chips(x, y))]

    def start(*refs):
        for cp in copies(*refs):
            cp.start()

    def wait(*refs):
        for cp in copies(*refs):
            cp.wait()

    return Side(ps, [jax.ShapeDtypeStruct((3,) + p.shape[1:], p.dtype) for p in ps], 3 * n, 0, start, wait)


def add_chips(p, r2, chip_idx, name):
    n, H, C = p.shape
    tr = _rows_block(H, C)

    def kern(s_ref, p_ref, r_ref, o_ref):
        o_ref[...] = ((p_ref[0] + r_ref[0].astype(F32)) + r_ref[1].astype(F32)) + r_ref[2].astype(F32)

    return pl.pallas_call(
        kern, name=name,
        grid_spec=pltpu.PrefetchScalarGridSpec(
            num_scalar_prefetch=1, grid=(H // tr,),
            in_specs=[pl.BlockSpec((1, tr, C), lambda r, s_ref: (s_ref[0], r, 0)),
                      pl.BlockSpec((3, tr, C), lambda r, s_ref: (0, r, 0))],
            out_specs=pl.BlockSpec((tr, C), lambda r, s_ref: (r, 0))),
        out_shape=jax.ShapeDtypeStruct((H, C), F32),
        compiler_params=_cp(("parallel",)),
    )(chip_idx, p, r2)


def swap_reduced(ss, name):
    n = len(ss)

    def body(*refs):
        s_refs, out_refs = refs[:n], refs[n:2 * n]
        send_sems, recv_sems = refs[2 * n:]
        x, y, c = _me()
        cps = [pltpu.make_async_remote_copy(src_ref=s_refs[w], dst_ref=out_refs[w], send_sem=send_sems.at[w],
                                            recv_sem=recv_sems.at[w], device_id=(x, y, 1 - c), device_id_type=MESH)
               for w in range(n)]
        for cp in cps:
            cp.start()
        for cp in cps:
            cp.wait()

    any_spec = pl.BlockSpec(memory_space=pl.ANY)
    return pl.pallas_call(
        body, name=name, out_shape=[jax.ShapeDtypeStruct(s.shape, s.dtype) for s in ss],
        in_specs=[any_spec] * n, out_specs=[any_spec] * n,
        scratch_shapes=[pltpu.SemaphoreType.DMA((n,)), pltpu.SemaphoreType.DMA((n,))],
    )(*ss)


BIG = (("gla_w_in", 2, (1024, GLA_IN // N_CHIPS), "lead"), ("gla_w_out", 2, (256, 1024), "row"),
       ("att_w_in", 2, (1024, 768), "col"), ("att_w_out", 2, (256, 1024), "row"),
       ("ff_w1", 4, (1024, 1024), "col"), ("ff_w2", 4, (1024, 1024), "row"))
FLAVOUR = {n: f for n, _, _, f in BIG}


def layer_weights(i):
    mixer = "gla" if i % 2 == 0 else "att"
    return (("in", mixer + "_w_in", i // 2), ("out", mixer + "_w_out", i // 2), ("w1", "ff_w1", i), ("w2", "ff_w2", i))


class Comm:
    def __init__(self, weights, core, chip):
        self.weights, self.core, self.chip = weights, core, chip
        self.c_idx = jnp.reshape(core, (1,)).astype(jnp.int32)
        self.chip_idx = jnp.reshape(chip, (1,)).astype(jnp.int32)
        self.reduced = {}

    def gather(self, items):
        shards = [self.weights[n][l:l + 1].astype(BF16) for _, n, l in items]
        return gather_side(shards, [FLAVOUR[n] for _, n, _ in items])

    def full_weights(self, items, gathered):
        W = {}
        for (role, n, _), w in zip(items, gathered):
            if n == "gla_w_in":
                w = jnp.pad(w.transpose(1, 2, 0, 3).reshape(1, D_MODEL, GLA_IN), ((0, 0), (0, 0), (0, GLA_IN_PAD - GLA_IN)))
            W[role] = (w, 0)
        return W

    def gather_now(self, items, name):
        return self.full_weights(items, run_side(self.gather(items), name))

    def swap(self, items):
        return swap_side([g for _, _, g in items])

    def reduce_begin(self, tag, items, swapped):
        ps = [add_half(g, r, self.c_idx, f"rs_add2_{tag}_{w}") for w, ((_, _, g), r) in enumerate(zip(items, swapped))]
        return tag, [(n, l) for n, l, _ in items], ps

    def exchange(self, pending):
        return exchange_side([pb for _, pb in pending[2]])

    def reduce_mid(self, pending, landed):
        tag, keys, ps = pending
        for w, (key, (p, _), r) in enumerate(zip(keys, ps, landed)):
            self.reduced[key] = add_chips(p, r, self.chip_idx, f"rs_add4_{tag}_{w}")

    def reduce_tail(self, tag, items):
        pending = self.reduce_begin(tag, items, run_side(self.swap(items), f"rs_swap_{tag}"))
        self.reduce_mid(pending, run_side(self.exchange(pending), f"rs_xchg_{tag}"))

    def reduce_end(self):
        keys = [(n, l) for n, L, _, _ in BIG for l in range(L)]
        mine = [self.reduced[k] for k in keys]
        theirs = swap_reduced(mine, "rs_join")
        low = self.core == 0
        full = {k: jnp.concatenate([jnp.where(low, m, t), jnp.where(low, t, m)], axis=0)
                for k, m, t in zip(keys, mine, theirs)}
        return {n: jnp.stack([full[(n, l)] for l in range(L)]) for n, L, _, _ in BIG}


def local_step(x, target, mods, comm, small):
    S, D = x.shape
    row = lambda v: v.reshape(1, -1)
    saved = []
    tiles = [att_bias_tiles(small["att_rel_bias"][j], f"att_tiles_{j}").reshape(ATT_HEADS, ATT_CLASSES, ATT_TQ, ATT_KW)
             for j in range(2)]
    wgk_p = [jnp.pad(small["gla_w_gk2"][j], ((0, 128 - GLA_RANK), (0, 0))).astype(BF16) for j in range(2)]

    u1 = modulate(x, row(mods[0, 1]), row(mods[0, 0]), "mod_first")
    Ws = [dict() for _ in range(DEPTH)]
    items0 = layer_weights(0)
    Ws[0].update(comm.gather_now(items0[:1], "gather_w0"))
    for i in range(DEPTH):
        j = i // 2
        W = Ws[i]
        sh1, sc1, g1, sh2, sc2, g2 = (row(mods[i, k]) for k in range(6))
        nxt = min(i + 1, DEPTH - 1)
        more = i + 1 < DEPTH
        nxt_items = layer_weights(nxt)
        in_items = list(items0[1:3]) if i == 0 else []
        mix_items = (list(items0[3:]) if i == 0 else []) + (list(nxt_items[:2]) if more else [])
        up_items = list(nxt_items[2:3]) if more else []
        down_items = list(nxt_items[3:]) if more else []

        def hosted(items):
            return comm.gather(items) if items else None

        def landed_weights(items, landed):
            for k, it in enumerate(items):
                layer = 0 if it in items0 and i == 0 else nxt
                Ws[layer].update(comm.full_weights([it], landed[k:k + 1]))

        side = hosted(in_items)
        if i % 2 == 0:
            proj = mm_plain(u1, *W["in"], f"gla_in_{i}", side=side)
        else:
            proj = mm_plain(u1, *W["in"], f"att_in_{i}", mode="bf16", bias=row(small["att_b_in"][j]), side=side)
        proj, landed = proj if side is not None else (proj, [])
        landed_weights(in_items, landed)
        if i % 2 == 0:
            (zmix, states), landed = gla_fwd(proj, wgk_p[j], row(small["gla_b_gk"][j]), row(small["gla_g_norm"][j]),
                                             f"gla_fwd_{i}", hosted(mix_items))
        else:
            (zmix,), landed = attn_fwd(proj, tiles[j], f"att_fwd_{i}", hosted(mix_items))
            states = None
        landed_weights(mix_items, landed)
        (y1, x_mid, u2), _ = mm_down_ln(zmix, *W["out"], x, 1.0 + g1, row(small["ln_g"][i, 0]),
                                        row(small["ln_b"][i, 0]), sc2, sh2, f"mix_out_{i}")
        side = hosted(up_items)
        act = mm_plain(u2, *W["w1"], f"ff_up_{i}", mode="mlp_up", side=side)
        act, landed = act if side is not None else (act, [])
        landed_weights(up_items, landed)
        (y2, x_out, u_next), landed = mm_down_ln(act, *W["w2"], x_mid, 1.0 + g2, row(small["ln_g"][i, 1]),
                                                 row(small["ln_b"][i, 1]), row(mods[nxt, 1]), row(mods[nxt, 0]),
                                                 f"ff_out_{i}", side=hosted(down_items))
        landed_weights(down_items, landed)
        saved.append(dict(x_in=x, u1=u1, proj=proj, zmix=zmix, states=states, y1=y1, x_mid=x_mid, u2=u2,
                          act=act, y2=y2))
        x, u1 = x_out, u_next

    g_small = dict(ln_g=[None] * DEPTH, ln_b=[None] * DEPTH, gla_w_gk2=[None] * 2, gla_b_gk=[None] * 2,
                   gla_g_norm=[None] * 2, att_b_in=[None] * 2, att_rel_bias=[None] * 2)
    dmods = [None] * DEPTH
    later = []
    top = saved[DEPTH - 1]
    dz2, dy2, s_ln2, sq = loss_ln_bwd(x, target, top["x_mid"], top["y2"], 1.0 + row(mods[DEPTH - 1, 5]),
                                      row(small["ln_g"][DEPTH - 1, 1]), "loss_ln_bwd")

    for i in reversed(range(DEPTH)):
        j = i // 2
        sv = saved[i]
        W = Ws[i]
        sh1, sc1, g1, sh2, sc2, g2 = (row(mods[i, k]) for k in range(6))
        dh = mm_plain(dy2, *W["w2"], f"ff_dn_{i}", mode="mlp_dn", nt=True, h=sv["act"])
        g_w2 = mm_w_res(sv["act"], dy2, f"ff_w2g_{i}").reshape(N_CHIPS, D_FF // N_CHIPS, D)
        g_w1 = mm_w_res(sv["u2"], dh, f"ff_w1g_{i}", chips_out=True)
        items = [("ff_w1", i, g_w1), ("ff_w2", i, g_w2)] + later
        (dz1, dy1, s_m2, s_ln1), swapped = mm_down_comb(
            dh, *W["w1"], dz2, sv["x_mid"], 1.0 + sc2, f"ff_dx_{i}",
            ln=(sv["x_in"], sv["y1"], 1.0 + g1, row(small["ln_g"][i, 0])), side=comm.swap(items))
        pending = comm.reduce_begin(i, items, swapped)
        side = comm.exchange(pending)
        mixer = "gla" if i % 2 == 0 else "att"
        below = None
        if i > 0:
            below = (saved[i - 1]["x_mid"], saved[i - 1]["y2"], 1.0 + row(mods[i - 1, 5]), row(small["ln_g"][i - 1, 1]))
        if i % 2 == 0:
            g_out = mm_w(sv["zmix"], dy1, f"gla_wog_{i}").reshape(N_CHIPS, D // N_CHIPS, D)
            dzg = mm_plain(dy1, *W["out"], f"gla_dz_{i}", nt=True)
            (dproj, dwgk, dbgk, dgn), landed = gla_bwd(sv["proj"], sv["states"], dzg, wgk_p[j],
                                                       row(small["gla_b_gk"][j]), row(small["gla_g_norm"][j]),
                                                       f"gla_bwd_{i}", side)
            g_small["gla_w_gk2"][j] = dwgk[:GLA_RANK]
            g_small["gla_b_gk"][j] = dbgk[0]
            g_small["gla_g_norm"][j] = dgn[0].reshape(GLA_HEADS, GLA_DV_HEAD)
            gwi = mm_w_res(sv["u1"], dproj, f"gla_wig_{i}")[:, :GLA_IN]
            g_in = gwi.reshape(D, N_CHIPS, GLA_IN // N_CHIPS).transpose(1, 0, 2)
            outs, _ = mm_down_comb(dproj, *W["in"], dz1, sv["x_in"], 1.0 + sc1, f"mix_dx_{i}", ln=below)
        else:
            g_out = mm_w(sv["zmix"], dy1, f"att_wog_{i}").reshape(N_CHIPS, D // N_CHIPS, D)
            do = mm_plain(dy1, *W["out"], f"att_do_{i}", mode="bf16", nt=True)
            (dqkv, dbt), landed = attn_bwd(sv["proj"], tiles[j], do, f"att_bwd_{i}", side)
            g_small["att_rel_bias"][j] = att_bias_grad(dbt.reshape(ATT_HEADS * ATT_CLASSES, ATT_TQ, ATT_KW),
                                                       f"att_bias_{i}")
            g_in = mm_w_chips3(sv["u1"], dqkv, f"att_wig_{i}")
            outs, _ = mm_down_comb(dqkv, *W["in"], dz1, sv["x_in"], 1.0 + sc1, f"mix_dx_{i}", parts=3, ln=below)
        s_m1 = outs[1] if below is None else outs[2]
        if i % 2 == 1:
            g_small["att_b_in"][j] = s_m1[2:5].reshape(3 * D)
        comm.reduce_mid(pending, landed)
        later = [(mixer + "_w_in", j, g_in), (mixer + "_w_out", j, g_out)]
        g_small["ln_g"][i] = jnp.stack([s_ln1[0], s_ln2[0]])
        g_small["ln_b"][i] = jnp.stack([s_ln1[1], s_ln2[1]])
        dmods[i] = jnp.stack([s_m1[1], s_m1[0], s_ln1[2], s_m2[1], s_m2[0], s_ln2[2]])
        if below is None:
            dx = outs[0]
        else:
            dz2, dy2, s_ln2 = outs[0], outs[1], outs[3]
    comm.reduce_tail("last", later)

    g_small = {n: jnp.stack(v) for n, v in g_small.items()}
    return sq, dx, jnp.stack(dmods), g_small


SMALL_SHARDED = (("ln_g", (4, 2, 256)), ("ln_b", (4, 2, 256)), ("gla_g_norm", (2, 4, 64)),
                 ("gla_w_gk2", (2, 16, 128)), ("att_b_in", (2, 768)))
SMALL_FULL = dict(ln_g=(4, 2, 1024), ln_b=(4, 2, 1024), gla_g_norm=(2, 4, 256), gla_w_gk2=(2, 16, 512),
                  att_b_in=(2, 3072), gla_b_gk=(2, 512), att_rel_bias=(2, 16, 257))
SMALL_GRAD_ORDER = ("ln_g", "ln_b", "gla_g_norm", "gla_w_gk2", "att_b_in", "gla_b_gk", "att_rel_bias")


def _pack_small(arrs, rows_total):
    parts = []
    for a in arrs:
        flat = a.reshape(-1)
        pad = (-flat.shape[0]) % PACK_W
        parts.append(jnp.pad(flat, (0, pad)).reshape(-1, PACK_W))
    buf = jnp.concatenate(parts, axis=0)
    return jnp.pad(buf, ((0, rows_total - buf.shape[0]), (0, 0)))


def _unpack_small(buf, shapes):
    out, r = [], 0
    for shp in shapes:
        n = 1
        for s in shp:
            n *= s
        nr = (n + PACK_W - 1) // PACK_W
        out.append(buf[..., r:r + nr, :].reshape(buf.shape[:-2] + (nr * PACK_W,))[..., :n].reshape(buf.shape[:-2] + shp))
        r += nr
    return out


def _unshard_last(g4):
    nd = g4.ndim
    perm = tuple(range(1, nd - 1)) + (0, nd - 1)
    t = g4.transpose(perm)
    return t.reshape(t.shape[:-2] + (-1,))


def _shard_last(full, s):
    n = full.shape[-1] // N_CHIPS
    return lax.dynamic_slice_in_dim(full, s * n, n, axis=full.ndim - 1)


WEIGHT_NAMES = ("w_ada", "b_ada", "ln_g", "ln_b", "gla_w_in", "gla_w_gk2", "gla_b_gk", "gla_g_norm", "gla_w_out",
                "att_w_in", "att_b_in", "att_rel_bias", "att_w_out", "ff_w1", "ff_w2")


def kernel(x, c, w_ada, b_ada, ln_g, ln_b, gla_w_in, gla_w_gk2, gla_b_gk, gla_g_norm, gla_w_out, att_w_in, att_b_in, att_rel_bias, att_w_out, ff_w1, ff_w2, loss_target, m_w_ada, m_b_ada, m_ln_g, m_ln_b, m_gla_w_in, m_gla_w_gk2, m_gla_b_gk, m_gla_g_norm, m_gla_w_out, m_att_w_in, m_att_b_in, m_att_rel_bias, m_att_w_out, m_ff_w1, m_ff_w2, v_w_ada, v_b_ada, v_ln_g, v_ln_b, v_gla_w_in, v_gla_w_gk2, v_gla_b_gk, v_gla_g_norm, v_gla_w_out, v_att_w_in, v_att_b_in, v_att_rel_bias, v_att_w_out, v_ff_w1, v_ff_w2):
    weights = dict(w_ada=w_ada, b_ada=b_ada, ln_g=ln_g, ln_b=ln_b, gla_w_in=gla_w_in, gla_w_gk2=gla_w_gk2,
                   gla_b_gk=gla_b_gk, gla_g_norm=gla_g_norm, gla_w_out=gla_w_out, att_w_in=att_w_in,
                   att_b_in=att_b_in, att_rel_bias=att_rel_bias, att_w_out=att_w_out, ff_w1=ff_w1, ff_w2=ff_w2)
    mom1 = dict(w_ada=m_w_ada, b_ada=m_b_ada, ln_g=m_ln_g, ln_b=m_ln_b, gla_w_in=m_gla_w_in, gla_w_gk2=m_gla_w_gk2,
                gla_b_gk=m_gla_b_gk, gla_g_norm=m_gla_g_norm, gla_w_out=m_gla_w_out, att_w_in=m_att_w_in,
                att_b_in=m_att_b_in, att_rel_bias=m_att_rel_bias, att_w_out=m_att_w_out, ff_w1=m_ff_w1, ff_w2=m_ff_w2)
    mom2 = dict(w_ada=v_w_ada, b_ada=v_b_ada, ln_g=v_ln_g, ln_b=v_ln_b, gla_w_in=v_gla_w_in, gla_w_gk2=v_gla_w_gk2,
                gla_b_gk=v_gla_b_gk, gla_g_norm=v_gla_g_norm, gla_w_out=v_gla_w_out, att_w_in=v_att_w_in,
                att_b_in=v_att_b_in, att_rel_bias=v_att_rel_bias, att_w_out=v_att_w_out, ff_w1=v_ff_w1, ff_w2=v_ff_w2)

    ax, ay, ac = lax.axis_index("x"), lax.axis_index("y"), lax.axis_index("c")
    chip = 2 * ax + ay
    dev = 2 * chip + ac
    S = x.shape[1]
    x2 = x.reshape(S, D_MODEL)
    t2 = loss_target.reshape(S, D_MODEL)

    comm = Comm(weights, ac, chip)

    small_rows = 16
    spack = _pack_small([c] + [weights[n] for n, _ in SMALL_SHARDED], small_rows)
    sg = all_gather8(spack, "gather_small").reshape(N_DEV, small_rows, PACK_W)
    parts = _unpack_small(sg, [(1, D_MODEL)] + [shp for _, shp in SMALL_SHARDED])
    c_all = parts[0].reshape(N_DEV, D_MODEL)
    small = {n: _unshard_last(p[0::2]) for (n, _), p in zip(SMALL_SHARDED, parts[1:])}
    small["gla_b_gk"] = gla_b_gk
    small["att_rel_bias"] = att_rel_bias

    c_act = silu_rows(jnp.pad(c_all, ((0, 128 - N_DEV), (0, 0))), "silu_c")
    mods_part = jnp.stack([mm_plain(c_act, w_ada, l, f"ada_fwd_{l}", tm=128)[:N_DEV] for l in range(DEPTH)], axis=1)
    mods_part = mods_part.reshape(N_DEV, DEPTH * 6 * D_MODEL // N_CHIPS)
    mg = all_gather8(mods_part, "gather_mods").reshape(N_CHIPS, 2, N_DEV, DEPTH, 6 * D_MODEL // N_CHIPS)
    mods_mine = lax.dynamic_index_in_dim(mg[:, 0], dev, axis=1, keepdims=False)
    mods = mods_mine.transpose(1, 0, 2).reshape(DEPTH, 6 * D_MODEL) + b_ada
    mods = mods.reshape(DEPTH, 6, D_MODEL)

    sq, grad_x, dmods, g_small = local_step(x2, t2, mods, comm, small)
    loss = lax.psum(0.5 * sq[0, 0] / D_MODEL, ("x", "y", "c"))

    g_shard = comm.reduce_end()

    dm_flat = dmods.reshape(DEPTH, 6 * D_MODEL)
    g_rows = 80
    gpack = _pack_small([dm_flat] + [g_small[n] for n in SMALL_GRAD_ORDER], g_rows)
    gg = all_gather8(gpack, "gather_small_grads").reshape(N_DEV, g_rows, PACK_W)
    gsum = sum_over_devices(gg, "sum_small_grads")
    sums = _unpack_small(gsum, [(DEPTH, 6 * D_MODEL)] + [SMALL_FULL[n] for n in SMALL_GRAD_ORDER])
    grads = dict(b_ada=sums[0])
    for n, full_g in zip(SMALL_GRAD_ORDER, sums[1:]):
        grads[n] = full_g if n in ("gla_b_gk", "att_rel_bias") else _shard_last(full_g, chip)
    dm_all = _unpack_small(gg, [(DEPTH, 6 * D_MODEL)])[0]
    dm_cols = _shard_last(dm_all, chip).reshape(N_DEV, DEPTH * 6 * D_MODEL // N_CHIPS)
    dm_cols = jnp.pad(dm_cols, ((0, 128 - N_DEV), (0, 0))).astype(BF16)
    gwa = mm_w(c_act, dm_cols, "ada_bwd", ts=128)
    grads["w_ada"] = gwa.reshape(D_MODEL, DEPTH, 6 * D_MODEL // N_CHIPS).transpose(1, 0, 2)
    grads.update(g_shard)

    deltas, new_m, new_v = {}, {}, {}
    for n in WEIGHT_NAMES:
        deltas[n], new_m[n], new_v[n] = adamw(weights[n], grads[n], mom1[n], mom2[n], "adamw_" + n)

    return (loss, grad_x.reshape(1, S, D_MODEL), *[grads[n] for n in WEIGHT_NAMES], *[deltas[n] for n in WEIGHT_NAMES],
            *[new_m[n] for n in WEIGHT_NAMES], *[new_v[n] for n in WEIGHT_NAMES])
```

```python
import functools

import jax
import jax.numpy as jnp
from jax import lax
from jax.experimental import pallas as pl
from jax.experimental.pallas import tpu as pltpu

F32 = jnp.float32
BF16 = jnp.bfloat16
HIGHEST = lax.Precision.HIGHEST
MESH = pl.DeviceIdType.MESH

D_MODEL = 1024
DEPTH = 4
CHUNK = 64
GLA_HEADS = 4
GLA_DK = 512
GLA_DV = 1024
GLA_DK_HEAD = 128
GLA_DV_HEAD = 256
GLA_RANK = 16
GLA_IN = 3088
GLA_IN_PAD = 3200
GLA_LR_OFF = 3072
ATT_HEADS = 16
ATT_HD = 64
LEFT_CHUNKS = 8
MAX_REL = 128
N_REL = 257
D_FF = 4096
ALPHA = (2.0 * DEPTH) ** 0.25
LN_EPS = 1e-5
RMS_EPS = 1e-6
NEG_INF = -1e30
GLA_SCALE = GLA_DK_HEAD ** -0.5
ATT_SCALE = ATT_HD ** -0.5
ADAM_LR = 0.001
ADAM_B1 = 0.9
ADAM_B2 = 0.999
ADAM_EPS = 1e-08
ADAM_WD = 0.01
ADAM_STEP = 10

ATT_TQ = 256
ATT_KW = 768
GLA_TB = 256
GLA_GROUP = 2
VMEM_LIMIT = 56 * 1024 * 1024
WHOLE_WEIGHT_BYTES = 8 * 1024 * 1024
N_CHIPS = 4
N_DEV = 8
PACK_W = 1024


def _dot(a, b):
    return jnp.dot(a, b, preferred_element_type=F32)


def _dot_nt(a, b):
    return lax.dot_general(a, b, (((1,), (1,)), ((), ())), preferred_element_type=F32)


def _dot_tn(a, b):
    return lax.dot_general(a, b, (((0,), (0,)), ((), ())), preferred_element_type=F32)


def _cp(sem, vmem=VMEM_LIMIT):
    return pltpu.CompilerParams(dimension_semantics=sem, vmem_limit_bytes=vmem)


def _row_spec(n):
    return pl.BlockSpec((1, n), lambda *_: (0, 0))


def _sigmoid(x):
    return 1.0 / (1.0 + jnp.exp(-x))


def _log_sigmoid(x):
    return jnp.minimum(x, 0.0) - jnp.log1p(jnp.exp(-jnp.abs(x)))


class Side:
    def __init__(self, ins, out_shapes, n_sems, n_local, start, wait):
        self.ins, self.out_shapes, self.n_sems, self.n_local = list(ins), list(out_shapes), n_sems, n_local
        self.start, self.wait = start, wait

    def sem_shapes(self):
        return [pltpu.SemaphoreType.DMA((self.n_sems,)), pltpu.SemaphoreType.DMA((self.n_sems,)),
                pltpu.SemaphoreType.DMA((max(self.n_local, 1),))]


def run_side(side, name):
    n_in = len(side.ins)
    n_out = len(side.out_shapes)

    def body(*refs):
        ins, outs, sems = refs[:n_in], refs[n_in:n_in + n_out], refs[n_in + n_out:]
        side.start(ins, outs, *sems)
        side.wait(ins, outs, *sems)

    any_spec = pl.BlockSpec(memory_space=pl.ANY)
    return pl.pallas_call(body, name=name, out_shape=side.out_shapes, in_specs=[any_spec] * n_in,
                          out_specs=[any_spec] * n_out, scratch_shapes=side.sem_shapes())(*side.ins)


def hosted_call(main, side, *, name, grid, in_specs, out_specs, out_shape, scratch_shapes, dims, args):
    if side is None:
        outs = pl.pallas_call(main, name=name, grid=grid, in_specs=in_specs, out_specs=out_specs,
                              out_shape=out_shape, scratch_shapes=scratch_shapes, compiler_params=_cp(dims))(*args)
        return list(outs), []
    n_mi, n_mo, n_ms = len(in_specs), len(out_specs), len(scratch_shapes)
    n_si, n_so = len(side.ins), len(side.out_shapes)

    def kern(*refs):
        mi, si = refs[:n_mi], refs[n_mi:n_mi + n_si]
        o0 = n_mi + n_si
        mo, so = refs[o0:o0 + n_mo], refs[o0 + n_mo:o0 + n_mo + n_so]
        s0 = o0 + n_mo + n_so
        ms, sems = refs[s0:s0 + n_ms], refs[s0 + n_ms:]
        ids = [pl.program_id(d) for d in range(len(grid))]
        first = functools.reduce(jnp.logical_and, [i == 0 for i in ids])
        last = functools.reduce(jnp.logical_and, [i == g - 1 for i, g in zip(ids, grid)])

        @pl.when(first)
        def _():
            side.start(si, so, *sems)
        main(*mi, *mo, *ms)

        @pl.when(last)
        def _():
            side.wait(si, so, *sems)

    any_spec = pl.BlockSpec(memory_space=pl.ANY)
    outs = pl.pallas_call(
        kern, name=name, grid=grid, in_specs=list(in_specs) + [any_spec] * n_si,
        out_specs=list(out_specs) + [any_spec] * n_so, out_shape=list(out_shape) + side.out_shapes,
        scratch_shapes=list(scratch_shapes) + side.sem_shapes(),
        compiler_params=_cp(("arbitrary",) * len(grid)))(*args, *side.ins)
    return list(outs[:n_mo]), list(outs[n_mo:])


def modulate(x, sc, sh, name):
    S, D = x.shape
    tm = min(512, S)

    def kern(x_ref, sc_ref, sh_ref, u_ref):
        u_ref[...] = (x_ref[...] * (1.0 + sc_ref[...]) + sh_ref[...]).astype(BF16)

    return pl.pallas_call(
        kern, name=name, grid=(S // tm,),
        in_specs=[pl.BlockSpec((tm, D), lambda i: (i, 0)), _row_spec(D), _row_spec(D)],
        out_specs=pl.BlockSpec((tm, D), lambda i: (i, 0)),
        out_shape=jax.ShapeDtypeStruct((S, D), BF16),
        compiler_params=_cp(("parallel",)),
    )(x, sc, sh)


def silu_rows(c_all, name):
    def kern(c_ref, o_ref):
        c = c_ref[...]
        o_ref[...] = (c * _sigmoid(c)).astype(BF16)

    return pl.pallas_call(kern, name=name, out_shape=jax.ShapeDtypeStruct(c_all.shape, BF16))(c_all)


def sum_over_devices(g, name):
    n, R, C = g.shape

    def kern(g_ref, o_ref):
        acc = g_ref[0]
        for d in range(1, n):
            acc = acc + g_ref[d]
        o_ref[...] = acc

    return pl.pallas_call(kern, name=name, out_shape=jax.ShapeDtypeStruct((R, C), F32))(g)


def _rows_block(R, C, budget=1 << 20):
    if R * C * 4 <= budget or R % 8:
        return R
    tr = max(8, (budget // (C * 4)) // 8 * 8)
    while R % tr:
        tr -= 8
    return tr


def adamw(w, g, m, v, name):
    shape = w.shape
    C = shape[-1]
    R = w.size // C
    w2, g2, m2, v2 = (t.reshape(R, C) for t in (w, g, m, v))
    tr = _rows_block(R, C)
    c1 = 1.0 - ADAM_B1 ** ADAM_STEP
    c2 = 1.0 - ADAM_B2 ** ADAM_STEP

    def kern(w_ref, g_ref, m_ref, v_ref, d_ref, nm_ref, nv_ref):
        gg = g_ref[...]
        nm = ADAM_B1 * m_ref[...] + (1.0 - ADAM_B1) * gg
        nv = ADAM_B2 * v_ref[...] + (1.0 - ADAM_B2) * (gg * gg)
        m_hat = nm / c1
        v_hat = nv / c2
        d_ref[...] = -ADAM_LR * (m_hat / (jnp.sqrt(v_hat) + ADAM_EPS) + ADAM_WD * w_ref[...])
        nm_ref[...] = nm
        nv_ref[...] = nv

    spec = pl.BlockSpec((tr, C), lambda i: (i, 0))
    outs = pl.pallas_call(
        kern, name=name, grid=(R // tr,),
        in_specs=[spec] * 4, out_specs=[spec] * 3,
        out_shape=[jax.ShapeDtypeStruct((R, C), F32)] * 3,
        compiler_params=_cp(("parallel",)),
    )(w2, g2, m2, v2)
    return tuple(o.reshape(shape) for o in outs)


def _tn_for(N):
    for tn in (1024, 768, 640, 512, 384, 256, 128):
        if N % tn == 0:
            return tn
    return N


def mm_plain(a, b3, layer, name, *, mode="f32", nt=False, bias=None, h=None, tm=1024, side=None):
    M, K = a.shape
    N = b3.shape[1] if nt else b3.shape[2]
    if K * N * 2 <= WHOLE_WEIGHT_BYTES:
        tn, tm = N, min(tm, 512)
    else:
        tn = _tn_for(N)
    tm = min(tm, M)
    a_spec = pl.BlockSpec((tm, K), lambda j, i: (i, 0))
    if nt:
        b_spec = pl.BlockSpec((None, tn, K), lambda j, i: (layer, j, 0))
    else:
        b_spec = pl.BlockSpec((None, K, tn), lambda j, i: (layer, 0, j))
    o_spec = pl.BlockSpec((tm, tn), lambda j, i: (i, j))
    ins, in_specs = [a, b3], [a_spec, b_spec]
    if bias is not None:
        ins.append(bias)
        in_specs.append(pl.BlockSpec((1, tn), lambda j, i: (0, j)))
    if mode == "mlp_dn":
        ins.append(h)
        in_specs.append(o_spec)
    elif mode not in ("f32", "bf16", "mlp_up"):
        raise ValueError(mode)
    odt = F32 if mode == "f32" else BF16

    def kern(a_ref, b_ref, *rest):
        rest = list(rest)
        bias_ref = rest.pop(0) if bias is not None else None
        h_ref = rest.pop(0) if mode == "mlp_dn" else None
        o_ref = rest.pop(0)
        if nt:
            bt_ref = rest.pop(0)

            @pl.when(pl.program_id(1) == 0)
            def _():
                bt_ref[...] = b_ref[...].T
            acc = _dot(a_ref[...], bt_ref[...])
        else:
            acc = _dot(a_ref[...], b_ref[...].astype(BF16))
        if bias_ref is not None:
            acc = acc + bias_ref[...]
        if mode == "mlp_up":
            r = jnp.maximum(acc, 0.0)
            acc = r * r
        elif mode == "mlp_dn":
            acc = acc * (2.0 * jnp.sqrt(h_ref[...].astype(F32)))
        o_ref[...] = acc.astype(odt)

    outs, landed = hosted_call(
        kern, side, name=name, grid=(N // tn, M // tm), in_specs=in_specs, out_specs=[o_spec],
        out_shape=[jax.ShapeDtypeStruct((M, N), odt)],
        scratch_shapes=[pltpu.VMEM((K, tn), BF16)] if nt else [], dims=("parallel", "arbitrary"), args=tuple(ins))
    return outs[0] if side is None else (outs[0], landed)


def mm_down_ln(a, b3, layer, x_in, gate1p, ln_g, ln_b, sc_next, sh_next, name, *, side=None, tm=256):
    M, K = a.shape
    D = b3.shape[2]
    tm = min(tm, M)

    def kern(a_ref, b_ref, x_ref, gp_ref, lg_ref, lb_ref, sc_ref, sh_ref, y_ref, xo_ref, u_ref):
        y = _dot(a_ref[...], b_ref[...])
        y_ref[...] = y.astype(BF16)
        z = ALPHA * x_ref[...] + gp_ref[...] * y
        mu = jnp.mean(z, axis=-1, keepdims=True)
        zc = z - mu
        var = jnp.mean(zc * zc, axis=-1, keepdims=True)
        xo = (zc * lax.rsqrt(var + LN_EPS)) * lg_ref[...] + lb_ref[...]
        xo_ref[...] = xo
        u_ref[...] = (xo * (1.0 + sc_ref[...]) + sh_ref[...]).astype(BF16)

    tile = pl.BlockSpec((tm, D), lambda i: (i, 0))
    outs, landed = hosted_call(
        kern, side, name=name, grid=(M // tm,),
        in_specs=[pl.BlockSpec((tm, K), lambda i: (i, 0)), pl.BlockSpec((None, K, D), lambda i: (layer, 0, 0)), tile]
        + [_row_spec(D)] * 5,
        out_specs=[tile, tile, tile],
        out_shape=[jax.ShapeDtypeStruct((M, D), BF16), jax.ShapeDtypeStruct((M, D), F32),
                   jax.ShapeDtypeStruct((M, D), BF16)],
        scratch_shapes=[], dims=("parallel",), args=(a, b3, x_in, gate1p, ln_g, ln_b, sc_next, sh_next))
    return tuple(outs), landed


def mm_down_comb(a, b3, layer, dz, x_in, sc1p, name, *, parts=1, ln=None, side=None, tm=256):
    D, K = b3.shape[1], b3.shape[2]
    M = a.shape[-2]
    kp = K // parts
    tm = min(tm, M)
    n_ln = 0 if ln is None else 4

    def kern(*refs):
        a_refs = refs[:parts]
        b_ref, dz_ref, x_ref, sp_ref = refs[parts:parts + 4]
        ln_refs = refs[parts + 4:parts + 4 + n_ln]
        outs = refs[parts + 4 + n_ln:]

        @pl.when(pl.program_id(0) == 0)
        def _():
            for o in outs:
                if o.shape[0] == 8:
                    o[...] = jnp.zeros_like(o)
        if parts == 1:
            du = _dot_nt(a_refs[0][...], b_ref[...])
        else:
            du = _dot_nt(a_refs[0][...], b_ref[:, 0:kp])
            for p in range(1, parts):
                du = du + _dot_nt(a_refs[p][...], b_ref[:, p * kp:(p + 1) * kp])
        dx = ALPHA * dz_ref[...] + du * sp_ref[...]
        if ln is None:
            dx_ref, s_ref = outs
            dx_ref[...] = dx
        else:
            dzl_ref, dyl_ref, s_ref, sl_ref = outs
            _ln_bwd_tile(dx, *ln_refs, dzl_ref, dyl_ref, sl_ref)
        s_ref[0:1, :] += jnp.sum(du * x_ref[...], axis=0, keepdims=True)
        s_ref[1:2, :] += jnp.sum(du, axis=0, keepdims=True)
        if parts > 1:
            for p in range(parts):
                s_ref[2 + p:3 + p, :] += jnp.sum(a_refs[p][...].astype(F32), axis=0, keepdims=True)

    tile = pl.BlockSpec((tm, D), lambda i: (i, 0))
    sums = pl.BlockSpec((8, D), lambda i: (0, 0))
    if parts == 1:
        a_ins, a_specs = [a], [pl.BlockSpec((tm, K), lambda i: (i, 0))]
    else:
        assert kp == D and parts <= 6
        a_ins = [a] * parts
        a_specs = [pl.BlockSpec((None, tm, kp), functools.partial(lambda i, p: (p, i, 0), p=p)) for p in range(parts)]
    in_specs = a_specs + [pl.BlockSpec((None, D, K), lambda i: (layer, 0, 0)), tile, tile, _row_spec(D)]
    args = a_ins + [b3, dz, x_in, sc1p]
    if ln is None:
        out_specs = [tile, sums]
        out_shape = [jax.ShapeDtypeStruct((M, D), F32), jax.ShapeDtypeStruct((8, D), F32)]
    else:
        in_specs += [tile, tile, _row_spec(D), _row_spec(D)]
        args += list(ln)
        out_specs = [tile, tile, sums, sums]
        out_shape = [jax.ShapeDtypeStruct((M, D), F32), jax.ShapeDtypeStruct((M, D), BF16),
                     jax.ShapeDtypeStruct((8, D), F32), jax.ShapeDtypeStruct((8, D), F32)]
    return hosted_call(kern, side, name=name, grid=(M // tm,), in_specs=in_specs, out_specs=out_specs,
                       out_shape=out_shape, scratch_shapes=[], dims=("arbitrary",), args=tuple(args))


def mm_w(a, b, name, *, ts=2048, tk=512, chips_out=False, b_parts=1, tn=None):
    S, K = a.shape
    npart = b.shape[-1]
    N = npart * b_parts
    ts = min(ts, S)
    tk = min(tk, K)
    n_chip = N // N_CHIPS
    if tn is None:
        tn = _tn_for(n_chip if chips_out else npart)
    assert npart % tn == 0 and (not chips_out or n_chip % tn == 0)

    def kern(a_ref, b_ref, o_ref):
        @pl.when(pl.program_id(2) == 0)
        def _():
            o_ref[...] = jnp.zeros_like(o_ref)
        o_ref[...] += _dot_tn(a_ref[...], b_ref[...])

    if b_parts == 1:
        b_spec = pl.BlockSpec((ts, tn), lambda k, n, s: (s, n))
    else:
        per = npart // tn
        b_spec = pl.BlockSpec((None, ts, tn), lambda k, n, s: (n // per, s, n % per))
    if chips_out:
        per_chip = n_chip // tn
        o_spec = pl.BlockSpec((None, tk, tn), lambda k, n, s: (n // per_chip, k, n % per_chip))
        out_shape = jax.ShapeDtypeStruct((N_CHIPS, K, n_chip), F32)
    else:
        o_spec = pl.BlockSpec((tk, tn), lambda k, n, s: (k, n))
        out_shape = jax.ShapeDtypeStruct((K, N), F32)
    return pl.pallas_call(
        kern, name=name, grid=(K // tk, N // tn, S // ts),
        in_specs=[pl.BlockSpec((ts, tk), lambda k, n, s: (s, k)), b_spec],
        out_specs=o_spec, out_shape=out_shape,
        compiler_params=_cp(("parallel", "parallel", "arbitrary")),
    )(a, b)


def mm_w_chips3(a, b3, name, *, ts=512):
    S, K = a.shape
    P = b3.shape[2]
    n_chip = 3 * P // N_CHIPS
    ts = min(ts, S)
    pieces = []
    for chip in range(N_CHIPS):
        lo, hi = chip * n_chip, (chip + 1) * n_chip
        while lo < hi:
            part = lo // P
            w = min(hi, (part + 1) * P) - lo
            pieces.append((chip, lo - chip * n_chip, part, lo - part * P, w))
            lo += w

    def kern(a_ref, b_ref, o_ref):
        @pl.when(pl.program_id(0) == 0)
        def _():
            o_ref[...] = jnp.zeros_like(o_ref)
        at = a_ref[...].T
        for chip, oc, part, pc, w in pieces:
            o_ref[chip, :, oc:oc + w] += _dot(at, b_ref[part, :, pc:pc + w])

    return pl.pallas_call(
        kern, name=name, grid=(S // ts,),
        in_specs=[pl.BlockSpec((ts, K), lambda s: (s, 0)), pl.BlockSpec((3, ts, P), lambda s: (0, s, 0))],
        out_specs=pl.BlockSpec((N_CHIPS, K, n_chip), lambda s: (0, 0, 0)),
        out_shape=jax.ShapeDtypeStruct((N_CHIPS, K, n_chip), F32),
        compiler_params=_cp(("arbitrary",)),
    )(a, b3)


def mm_w_res(a, b, name, *, chips_out=False, ts=512):
    S, K = a.shape
    N = b.shape[1]
    ts = min(ts, S)
    n_chip = N // N_CHIPS

    def kern(a_ref, b_ref, o_ref):
        @pl.when(pl.program_id(0) == 0)
        def _():
            o_ref[...] = jnp.zeros_like(o_ref)
        at = a_ref[...].T
        if chips_out:
            for chip in range(N_CHIPS):
                o_ref[chip] += _dot(at, b_ref[:, chip * n_chip:(chip + 1) * n_chip])
        else:
            o_ref[...] += _dot(at, b_ref[...])

    o_shape = (N_CHIPS, K, n_chip) if chips_out else (K, N)
    return pl.pallas_call(
        kern, name=name, grid=(S // ts,),
        in_specs=[pl.BlockSpec((ts, K), lambda s: (s, 0)), pl.BlockSpec((ts, N), lambda s: (s, 0))],
        out_specs=pl.BlockSpec(o_shape, lambda s: (0,) * len(o_shape)),
        out_shape=jax.ShapeDtypeStruct(o_shape, F32),
        compiler_params=_cp(("arbitrary",)),
    )(a, b)


def mm_f32(a, b, name):
    def kern(a_ref, b_ref, o_ref):
        o_ref[...] = jnp.dot(a_ref[...], b_ref[...], precision=HIGHEST, preferred_element_type=F32)

    return pl.pallas_call(kern, name=name, out_shape=jax.ShapeDtypeStruct((a.shape[0], b.shape[1]), F32),
                          compiler_params=pltpu.CompilerParams(vmem_limit_bytes=VMEM_LIMIT))(a, b)


def _ln_bwd_tile(dxo_t, x_ref, y_ref, gp_ref, lg_ref, dz_ref, dy_ref, s_ref):
    yv = y_ref[...].astype(F32)
    z = ALPHA * x_ref[...] + gp_ref[...] * yv
    mu = jnp.mean(z, axis=-1, keepdims=True)
    zc = z - mu
    var = jnp.mean(zc * zc, axis=-1, keepdims=True)
    rstd = lax.rsqrt(var + LN_EPS)
    xhat = zc * rstd
    dxh = dxo_t * lg_ref[...]
    dz = rstd * (dxh - jnp.mean(dxh, axis=-1, keepdims=True)
                 - xhat * jnp.mean(dxh * xhat, axis=-1, keepdims=True))
    dz_ref[...] = dz
    dy_ref[...] = (gp_ref[...] * dz).astype(BF16)
    s_ref[0:1, :] += jnp.sum(dxo_t * xhat, axis=0, keepdims=True)
    s_ref[1:2, :] += jnp.sum(dxo_t, axis=0, keepdims=True)
    s_ref[2:3, :] += jnp.sum(dz * yv, axis=0, keepdims=True)


def loss_ln_bwd(x_out, target, x_in, y, gate1p, ln_g, name, *, tm=256):
    S, D = x_out.shape
    tm = min(tm, S)

    def kern(xo_ref, t_ref, x_ref, y_ref, gp_ref, lg_ref, dz_ref, dy_ref, s_ref, l_ref):
        @pl.when(pl.program_id(0) == 0)
        def _():
            s_ref[...] = jnp.zeros_like(s_ref)
            l_ref[...] = jnp.zeros_like(l_ref)
        e = xo_ref[...] - t_ref[...]
        l_ref[...] += jnp.sum(e * e)
        _ln_bwd_tile(e * (1.0 / D), x_ref, y_ref, gp_ref, lg_ref, dz_ref, dy_ref, s_ref)

    tile = pl.BlockSpec((tm, D), lambda i: (i, 0))
    return pl.pallas_call(
        kern, name=name, grid=(S // tm,),
        in_specs=[tile, tile, tile, tile, _row_spec(D), _row_spec(D)],
        out_specs=[tile, tile, pl.BlockSpec((8, D), lambda i: (0, 0)), pl.BlockSpec((8, 128), lambda i: (0, 0))],
        out_shape=[jax.ShapeDtypeStruct((S, D), F32), jax.ShapeDtypeStruct((S, D), BF16),
                   jax.ShapeDtypeStruct((8, D), F32), jax.ShapeDtypeStruct((8, 128), F32)],
        compiler_params=_cp(("arbitrary",)),
    )(x_out, target, x_in, y, gate1p, ln_g)


def _tri64():
    r = lax.broadcasted_iota(jnp.int32, (CHUNK, CHUNK), 0)
    c = lax.broadcasted_iota(jnp.int32, (CHUNK, CHUNK), 1)
    return r >= c


def _gla_chunk_common(proj_ref, rows, b, h):
    kc = slice(h * GLA_DK_HEAD, (h + 1) * GLA_DK_HEAD)
    bh = b[:, kc]
    ep = jnp.exp(bh)
    en = jnp.exp(-bh)
    bl = bh[CHUNK - 1:CHUNK, :]
    ee = jnp.exp(bl - bh)
    dec = jnp.exp(bl)
    q = proj_ref[rows, h * GLA_DK_HEAD:(h + 1) * GLA_DK_HEAD].astype(F32) * GLA_SCALE
    k = proj_ref[rows, GLA_DK + h * GLA_DK_HEAD:GLA_DK + (h + 1) * GLA_DK_HEAD].astype(F32)
    v = proj_ref[rows, 2 * GLA_DK + h * GLA_DV_HEAD:2 * GLA_DK + (h + 1) * GLA_DV_HEAD].astype(F32)
    g = proj_ref[rows, 2 * GLA_DK + GLA_DV + h * GLA_DV_HEAD:
                 2 * GLA_DK + GLA_DV + (h + 1) * GLA_DV_HEAD].astype(F32)
    return ep, en, ee, dec, q, k, v, g


def gla_fwd(proj, wgk_p, bgk, gnorm, name, side=None):
    S = proj.shape[0]
    TB = min(GLA_TB, S)
    ncb = TB // CHUNK

    def kern(proj_ref, wgk_ref, bgk_ref, gn_ref, zg_ref, st_ref, state_scr, la_scr):
        @pl.when(pl.program_id(0) == 0)
        def _():
            state_scr[...] = jnp.zeros_like(state_scr)
        lr = proj_ref[:, GLA_LR_OFF:GLA_IN_PAD].astype(BF16)
        gk = _dot(lr, wgk_ref[...]) + bgk_ref[...]
        la_scr[...] = _log_sigmoid(gk) * (1.0 / 16.0)
        lower = _tri64()
        tri = lower.astype(F32)

        def group(gi, carry):
            rows = [pl.ds(pl.multiple_of((gi * GLA_GROUP + g) * CHUNK, CHUNK), CHUNK) for g in range(GLA_GROUP)]
            b = [jnp.dot(tri, la_scr[r, :], precision=HIGHEST, preferred_element_type=F32) for r in rows]
            P = [(g, h) for g in range(GLA_GROUP) for h in range(GLA_HEADS)]
            cm = {p: _gla_chunk_common(proj_ref, rows[p[0]], b[p[0]], p[1]) for p in P}
            qf = {p: (cm[p][4] * cm[p][0]).astype(BF16) for p in P}
            kn = {p: (cm[p][5] * cm[p][1]).astype(BF16) for p in P}
            qn = {p: (cm[p][4] * cm[p][1]).astype(BF16) for p in P}
            kp = {p: (cm[p][5] * cm[p][0]).astype(BF16) for p in P}
            ke = {p: (cm[p][5] * cm[p][2]).astype(BF16) for p in P}
            vb = {p: cm[p][6].astype(BF16) for p in P}
            a_f = {p: _dot_nt(qf[p], kn[p]) for p in P}
            a_b = {p: _dot_nt(qn[p], kp[p]) for p in P}
            upd = {p: _dot_tn(vb[p], ke[p]) for p in P}
            st = {(0, h): state_scr[h] for h in range(GLA_HEADS)}
            for g in range(GLA_GROUP):
                for h in range(GLA_HEADS):
                    st[(g + 1, h)] = st[(g, h)] * cm[(g, h)][3] + upd[(g, h)]
            o_st = {p: _dot_nt(qf[p], st[p].astype(BF16)) for p in P}
            amat = {p: jnp.where(lower, a_f[p], a_b[p]).astype(BF16) for p in P}
            o = {p: _dot(amat[p], vb[p]) + o_st[p] for p in P}
            for g, h in P:
                st_ref[gi * GLA_GROUP + g, h] = st[(g, h)]
            for h in range(GLA_HEADS):
                state_scr[h] = st[(GLA_GROUP, h)]
            for g, h in P:
                gate = cm[(g, h)][7]
                vc = slice(h * GLA_DV_HEAD, (h + 1) * GLA_DV_HEAD)
                r = lax.rsqrt(jnp.mean(o[(g, h)] * o[(g, h)], axis=-1, keepdims=True) + RMS_EPS)
                on = (o[(g, h)] * r) * gn_ref[:, vc]
                zg_ref[rows[g], vc] = (on * (gate * _sigmoid(gate))).astype(BF16)
            return carry

        lax.fori_loop(0, ncb // GLA_GROUP, group, 0)

    return hosted_call(
        kern, side, name=name, grid=(S // TB,),
        in_specs=[pl.BlockSpec((TB, GLA_IN_PAD), lambda i: (i, 0)),
                  pl.BlockSpec((128, GLA_DK), lambda i: (0, 0)), _row_spec(GLA_DK), _row_spec(GLA_DV)],
        out_specs=[pl.BlockSpec((TB, GLA_DV), lambda i: (i, 0)),
                   pl.BlockSpec((ncb, GLA_HEADS, GLA_DV_HEAD, GLA_DK_HEAD), lambda i: (i, 0, 0, 0))],
        out_shape=[jax.ShapeDtypeStruct((S, GLA_DV), BF16),
                   jax.ShapeDtypeStruct((S // CHUNK, GLA_HEADS, GLA_DV_HEAD, GLA_DK_HEAD), F32)],
        scratch_shapes=[pltpu.VMEM((GLA_HEADS, GLA_DV_HEAD, GLA_DK_HEAD), F32), pltpu.VMEM((TB, GLA_DK), F32)],
        dims=("arbitrary",), args=(proj, wgk_p, bgk, gnorm))


def gla_bwd(proj, states, dzg, wgk_p, bgk, gnorm, name, side=None):
    S = proj.shape[0]
    TB = min(GLA_TB, S)
    ncb = TB // CHUNK
    nb = S // TB

    def kern(proj_ref, st_ref, dzg_ref, wgk_ref, bgk_ref, gn_ref,
             dproj_ref, dwgk_ref, dbgk_ref, dgn_ref, dstate_scr, la_scr, gk_scr, dgk_scr):
        @pl.when(pl.program_id(0) == 0)
        def _():
            dstate_scr[...] = jnp.zeros_like(dstate_scr)
            dwgk_ref[...] = jnp.zeros_like(dwgk_ref)
            dbgk_ref[...] = jnp.zeros_like(dbgk_ref)
            dgn_ref[...] = jnp.zeros_like(dgn_ref)
        lr = proj_ref[:, GLA_LR_OFF:GLA_IN_PAD].astype(BF16)
        gk = _dot(lr, wgk_ref[...]) + bgk_ref[...]
        gk_scr[...] = gk
        la_scr[...] = _log_sigmoid(gk) * (1.0 / 16.0)
        lower = _tri64()
        tri = lower.astype(F32)
        r_i = lax.broadcasted_iota(jnp.int32, (CHUNK, CHUNK), 0)
        c_i = lax.broadcasted_iota(jnp.int32, (CHUNK, CHUNK), 1)
        triu = (c_i >= r_i).astype(F32)
        last_row = lax.broadcasted_iota(jnp.int32, (CHUNK, GLA_DK_HEAD), 0) == CHUNK - 1

        def group(gi, carry):
            cs = [ncb - 1 - (gi * GLA_GROUP + g) for g in range(GLA_GROUP)]
            rows = [pl.ds(pl.multiple_of(c * CHUNK, CHUNK), CHUNK) for c in cs]
            b = [jnp.dot(tri, la_scr[r, :], precision=HIGHEST, preferred_element_type=F32) for r in rows]
            P = [(g, h) for g in range(GLA_GROUP) for h in range(GLA_HEADS)]
            kcs = [slice(h * GLA_DK_HEAD, (h + 1) * GLA_DK_HEAD) for h in range(GLA_HEADS)]
            vcs = [slice(h * GLA_DV_HEAD, (h + 1) * GLA_DV_HEAD) for h in range(GLA_HEADS)]
            cm = {p: _gla_chunk_common(proj_ref, rows[p[0]], b[p[0]], p[1]) for p in P}
            ep, en, ee, dec = ({p: cm[p][i] for p in P} for i in range(4))
            qf = {p: cm[p][4] * cm[p][0] for p in P}
            kn = {p: cm[p][5] * cm[p][1] for p in P}
            qn = {p: cm[p][4] * cm[p][1] for p in P}
            kp = {p: cm[p][5] * cm[p][0] for p in P}
            ke = {p: cm[p][5] * cm[p][2] for p in P}
            qf_b, kn_b, qn_b, kp_b, ke_b = ({p: t[p].astype(BF16) for p in P} for t in (qf, kn, qn, kp, ke))
            vb = {p: cm[p][6].astype(BF16) for p in P}
            st = {p: st_ref[cs[p[0]], p[1]] for p in P}
            st_b = {p: st[p].astype(BF16) for p in P}
            a_f = {p: _dot_nt(qf_b[p], kn_b[p]) for p in P}
            a_b = {p: _dot_nt(qn_b[p], kp_b[p]) for p in P}
            o_st = {p: _dot_nt(qf_b[p], st_b[p]) for p in P}
            amat = {p: jnp.where(lower, a_f[p], a_b[p]).astype(BF16) for p in P}
            o = {p: _dot(amat[p], vb[p]) + o_st[p] for p in P}
            do_b, dgs = {}, {}
            for p in P:
                g, h = p
                r = lax.rsqrt(jnp.mean(o[p] * o[p], axis=-1, keepdims=True) + RMS_EPS)
                oh = o[p] * r
                gn = gn_ref[:, vcs[h]]
                gate = cm[p][7]
                sg = _sigmoid(gate)
                dz = dzg_ref[rows[g], vcs[h]]
                don = dz * (gate * sg)
                dgs[p] = dz * (oh * gn) * (sg * (1.0 + gate * (1.0 - sg)))
                dgn_ref[:, vcs[h]] += jnp.sum(don * oh, axis=0, keepdims=True)
                doh = don * gn
                do_b[p] = (r * (doh - oh * jnp.mean(doh * oh, axis=-1, keepdims=True))).astype(BF16)
            da = {p: _dot_nt(do_b[p], vb[p]) for p in P}
            dv_a = {p: _dot_tn(amat[p], do_b[p]) for p in P}
            dqf_st = {p: _dot(do_b[p], st_b[p]) for p in P}
            dst_upd = {p: _dot_tn(do_b[p], qf_b[p]) for p in P}
            dst = {(0, h): dstate_scr[h] for h in range(GLA_HEADS)}
            for g in range(GLA_GROUP):
                for h in range(GLA_HEADS):
                    dst[(g + 1, h)] = dst[(g, h)] * dec[(g, h)] + dst_upd[(g, h)]
            for h in range(GLA_HEADS):
                dstate_scr[h] = dst[(GLA_GROUP, h)]
            dst_b = {p: dst[p].astype(BF16) for p in P}
            dv = {p: dv_a[p] + _dot_nt(ke_b[p], dst_b[p]) for p in P}
            dke = {p: _dot(vb[p], dst_b[p]) for p in P}
            da_f = {p: jnp.where(lower, da[p], 0.0).astype(BF16) for p in P}
            da_b = {p: jnp.where(lower, 0.0, da[p]).astype(BF16) for p in P}
            dqf = {p: _dot(da_f[p], kn_b[p]) + dqf_st[p] for p in P}
            dkn = {p: _dot_tn(da_f[p], qf_b[p]) for p in P}
            dqn = {p: _dot(da_b[p], kp_b[p]) for p in P}
            dkp = {p: _dot_tn(da_b[p], qn_b[p]) for p in P}
            dbs = {}
            for p in P:
                ddec = jnp.sum(dst[p] * st[p], axis=0, keepdims=True)
                db = dqf[p] * qf[p] - dkn[p] * kn[p] - dqn[p] * qn[p] + dkp[p] * kp[p] - dke[p] * ke[p]
                dbl = jnp.sum(dke[p] * ke[p], axis=0, keepdims=True) + ddec * dec[p]
                dbs[p] = db + jnp.where(last_row, dbl, 0.0)
            dla = {p: jnp.dot(triu, dbs[p], precision=HIGHEST, preferred_element_type=F32) for p in P}
            for p in P:
                g, h = p
                dq = (dqf[p] * ep[p] + dqn[p] * en[p]) * GLA_SCALE
                dk = dkn[p] * en[p] + dkp[p] * ep[p] + dke[p] * ee[p]
                dgk_scr[rows[g], kcs[h]] = dla[p] * (1.0 / 16.0) * _sigmoid(-gk_scr[rows[g], kcs[h]])
                dproj_ref[rows[g], kcs[h]] = dq.astype(BF16)
                dproj_ref[rows[g], GLA_DK + h * GLA_DK_HEAD:GLA_DK + (h + 1) * GLA_DK_HEAD] = dk.astype(BF16)
                dproj_ref[rows[g], 2 * GLA_DK + h * GLA_DV_HEAD:2 * GLA_DK + (h + 1) * GLA_DV_HEAD] = dv[p].astype(BF16)
                dproj_ref[rows[g], 2 * GLA_DK + GLA_DV + h * GLA_DV_HEAD:
                          2 * GLA_DK + GLA_DV + (h + 1) * GLA_DV_HEAD] = dgs[p].astype(BF16)
            return carry

        lax.fori_loop(0, ncb // GLA_GROUP, group, 0)
        dgk = dgk_scr[...]
        dgk_b = dgk.astype(BF16)
        dproj_ref[:, GLA_LR_OFF:GLA_IN_PAD] = _dot_nt(dgk_b, wgk_ref[...]).astype(BF16)
        dwgk_ref[...] += _dot_tn(lr, dgk_b)
        dbgk_ref[...] += jnp.sum(dgk, axis=0, keepdims=True)

    rev = lambda i: (nb - 1 - i, 0)
    return hosted_call(
        kern, side, name=name, grid=(nb,),
        in_specs=[pl.BlockSpec((TB, GLA_IN_PAD), rev),
                  pl.BlockSpec((ncb, GLA_HEADS, GLA_DV_HEAD, GLA_DK_HEAD), lambda i: (nb - 1 - i, 0, 0, 0)),
                  pl.BlockSpec((TB, GLA_DV), rev),
                  pl.BlockSpec((128, GLA_DK), lambda i: (0, 0)), _row_spec(GLA_DK), _row_spec(GLA_DV)],
        out_specs=[pl.BlockSpec((TB, GLA_IN_PAD), rev),
                   pl.BlockSpec((128, GLA_DK), lambda i: (0, 0)), _row_spec(GLA_DK), _row_spec(GLA_DV)],
        out_shape=[jax.ShapeDtypeStruct((S, GLA_IN_PAD), BF16), jax.ShapeDtypeStruct((128, GLA_DK), F32),
                   jax.ShapeDtypeStruct((1, GLA_DK), F32), jax.ShapeDtypeStruct((1, GLA_DV), F32)],
        scratch_shapes=[pltpu.VMEM((GLA_HEADS, GLA_DV_HEAD, GLA_DK_HEAD), F32), pltpu.VMEM((TB, GLA_DK), F32),
                        pltpu.VMEM((TB, GLA_DK), F32), pltpu.VMEM((TB, GLA_DK), F32)],
        dims=("arbitrary",), args=(proj, states, dzg, wgk_p, bgk, gnorm))


ATT_TW = 1024
ATT_CLASSES = 3


def _att_window(i):
    return pl.multiple_of(jnp.maximum(i * ATT_TQ - LEFT_CHUNKS * CHUNK, 0), ATT_TQ)


def _att_rel_index():
    e = jnp.arange(ATT_TW)[None, :]
    d = jnp.where(e < ATT_KW, e, e - ATT_TW)
    off = (jnp.arange(ATT_CLASSES) * ATT_TQ)[:, None]
    return jnp.clip(off - d, -MAX_REL, MAX_REL) + MAX_REL


def _row_bits():
    return lax.broadcasted_iota(jnp.int32, (ATT_TQ, ATT_TW), 0)


def att_bias_tiles(rel_bias, name):
    pick = (jnp.arange(384)[:, None] == _att_rel_index().reshape(-1)[None, :]).astype(F32)
    tab = mm_f32(jnp.pad(rel_bias, ((0, 0), (0, 384 - N_REL))), pick, name + "_tab")
    tab = tab.reshape(ATT_HEADS * ATT_CLASSES, 1, ATT_TW)

    def kern(t_ref, o_ref):
        cls = pl.program_id(0) % ATT_CLASSES
        x = jnp.broadcast_to(t_ref[...], (ATT_TQ, ATT_TW))
        x = pltpu.roll(x, 0, 1, stride=1, stride_axis=0)
        x = x[:, :ATT_KW]
        qc = cls * (ATT_TQ // CHUNK) + lax.shift_right_arithmetic(
            lax.broadcasted_iota(jnp.int32, (ATT_TQ, ATT_KW), 0), 6)
        kc = lax.shift_right_arithmetic(lax.broadcasted_iota(jnp.int32, (ATT_TQ, ATT_KW), 1), 6)
        o_ref[...] = jnp.where((kc <= qc) & (kc >= qc - LEFT_CHUNKS), x, NEG_INF)

    return pl.pallas_call(
        kern, name=name, grid=(ATT_HEADS * ATT_CLASSES,),
        in_specs=[pl.BlockSpec((None, 1, ATT_TW), lambda i: (i, 0, 0))],
        out_specs=pl.BlockSpec((None, ATT_TQ, ATT_KW), lambda i: (i, 0, 0)),
        out_shape=jax.ShapeDtypeStruct((ATT_HEADS * ATT_CLASSES, ATT_TQ, ATT_KW), F32),
        compiler_params=_cp(("parallel",)),
    )(tab)


def att_bias_grad(dbt, name):
    def kern(d_ref, o_ref):
        x = jnp.concatenate([d_ref[...], jnp.zeros((ATT_TQ, ATT_TW - ATT_KW), F32)], axis=1)
        row = _row_bits()
        for b in range(8):
            x = jnp.where((row & (1 << b)) != 0, pltpu.roll(x, ATT_TW - (1 << b), axis=1), x)
        o_ref[...] = jnp.sum(x, axis=0, keepdims=True)

    diag = pl.pallas_call(
        kern, name=name + "_diag", grid=(ATT_HEADS * ATT_CLASSES,),
        in_specs=[pl.BlockSpec((None, ATT_TQ, ATT_KW), lambda i: (i, 0, 0))],
        out_specs=pl.BlockSpec((None, 1, ATT_TW), lambda i: (i, 0, 0)),
        out_shape=jax.ShapeDtypeStruct((ATT_HEADS * ATT_CLASSES, 1, ATT_TW), F32),
        compiler_params=_cp(("parallel",)),
    )(dbt)
    diag = diag.reshape(ATT_HEADS, ATT_CLASSES * ATT_TW)
    onehot = (_att_rel_index().reshape(-1)[:, None] == jnp.arange(384)[None, :]).astype(F32)
    return mm_f32(diag, onehot, name + "_bins")[:, :N_REL]


ATT_GROUP = 2


def _att_scores(q_ref, k_ref, bias_refs, blk0):
    G = range(ATT_GROUP)
    hs = [slice(hh * ATT_HD, (hh + 1) * ATT_HD) for hh in range(2)]
    rows = [slice(g * ATT_TQ, (g + 1) * ATT_TQ) for g in G]
    wins = [pl.ds(_att_window(blk0 + g), ATT_KW) for g in G]
    kw = [k_ref[w, :] for w in wins]
    P = [(g, hh) for g in G for hh in range(2)]
    q = {p: q_ref[rows[p[0]], hs[p[1]]] * ATT_SCALE for p in P}
    k = {p: kw[p[0]][:, hs[p[1]]] for p in P}
    s = {p: _dot_nt(q[p], k[p]) + bias_refs[p[0]][p[1]] for p in P}
    e = {p: jnp.exp(s[p] - jnp.max(s[p], axis=-1, keepdims=True)) for p in P}
    inv = {p: 1.0 / jnp.sum(e[p], axis=-1, keepdims=True) for p in P}
    return P, rows, wins, hs, q, k, e, inv


def _att_specs(S):
    nq = D_MODEL // 128
    q_spec = pl.BlockSpec((ATT_GROUP * ATT_TQ, 128), lambda p, i: (i, p))
    k_spec = pl.BlockSpec((S, 128), lambda p, i: (0, nq + p))
    v_spec = pl.BlockSpec((S, 128), lambda p, i: (0, 2 * nq + p))
    b_specs = [pl.BlockSpec((2, None, ATT_TQ, ATT_KW),
                            functools.partial(lambda p, i, g: (p, jnp.minimum(ATT_GROUP * i + g, ATT_CLASSES - 1), 0, 0), g=g))
               for g in range(ATT_GROUP)]
    return q_spec, k_spec, v_spec, b_specs


def attn_fwd(qkv, bias, name, side=None):
    S = qkv.shape[0]
    q_spec, k_spec, v_spec, b_specs = _att_specs(S)

    def kern(q_ref, k_ref, v_ref, *rest):
        bias_refs, o_ref = rest[:ATT_GROUP], rest[ATT_GROUP]
        P, rows, wins, hs, _, _, e, inv = _att_scores(q_ref, k_ref, bias_refs, ATT_GROUP * pl.program_id(1))
        vw = [v_ref[w, :] for w in wins]
        o = {p: _dot(e[p].astype(BF16), vw[p[0]][:, hs[p[1]]]) * inv[p] for p in P}
        for g in range(ATT_GROUP):
            o_ref[rows[g], :] = jnp.concatenate([o[(g, 0)], o[(g, 1)]], axis=1).astype(BF16)

    return hosted_call(
        kern, side, name=name, grid=(ATT_HEADS // 2, S // (ATT_GROUP * ATT_TQ)),
        in_specs=[q_spec, k_spec, v_spec] + b_specs,
        out_specs=[pl.BlockSpec((ATT_GROUP * ATT_TQ, 128), lambda p, i: (i, p))],
        out_shape=[jax.ShapeDtypeStruct((S, D_MODEL), BF16)],
        scratch_shapes=[], dims=("parallel", "arbitrary"), args=(qkv, qkv, qkv) + (bias,) * ATT_GROUP)


def attn_bwd(qkv, bias, do, name, side=None):
    S = qkv.shape[0]
    nstep = S // (ATT_GROUP * ATT_TQ)
    q_spec, k_spec, v_spec, b_specs = _att_specs(S)

    def kern(q_ref, k_ref, v_ref, *rest):
        bias_refs = rest[:ATT_GROUP]
        do_ref, dqkv_ref, db_ref, dk_scr, dv_scr = rest[ATT_GROUP:]
        i = pl.program_id(1)

        @pl.when(i == 0)
        def _():
            dk_scr[...] = jnp.zeros_like(dk_scr)
            dv_scr[...] = jnp.zeros_like(dv_scr)
            db_ref[...] = jnp.zeros_like(db_ref)
        blk0 = ATT_GROUP * i
        P, rows, wins, hs, q, k, e, inv = _att_scores(q_ref, k_ref, bias_refs, blk0)
        vw = [v_ref[w, :] for w in wins]
        do_h = {p: do_ref[rows[p[0]], hs[p[1]]] for p in P}
        dp = {p: _dot_nt(do_h[p], vw[p[0]][:, hs[p[1]]]) for p in P}
        pr = {p: e[p] * inv[p] for p in P}
        dvs = {p: _dot_tn(pr[p].astype(BF16), do_h[p]) for p in P}
        ds = {p: pr[p] * (dp[p] - jnp.sum(pr[p] * dp[p], axis=-1, keepdims=True)) for p in P}
        ds_b = {p: ds[p].astype(BF16) for p in P}
        dqs = {p: _dot(ds_b[p], k[p]) * ATT_SCALE for p in P}
        dks = {p: _dot_tn(ds_b[p], q[p]) for p in P}
        for g, hh in P:
            db_ref[hh, jnp.minimum(blk0 + g, ATT_CLASSES - 1)] += ds[(g, hh)]
        for g in range(ATT_GROUP):
            first = pl.multiple_of((blk0 + g) * ATT_TQ, ATT_TQ)
            dqkv_ref[0, pl.ds(first, ATT_TQ), :] = jnp.concatenate([dqs[(g, 0)], dqs[(g, 1)]], axis=1).astype(BF16)
            dk_scr[wins[g], :] += jnp.concatenate([dks[(g, 0)], dks[(g, 1)]], axis=1)
            dv_scr[wins[g], :] += jnp.concatenate([dvs[(g, 0)], dvs[(g, 1)]], axis=1)

        @pl.when(i == nstep - 1)
        def _():
            dqkv_ref[1] = dk_scr[...].astype(BF16)
            dqkv_ref[2] = dv_scr[...].astype(BF16)

    return hosted_call(
        kern, side, name=name, grid=(ATT_HEADS // 2, nstep),
        in_specs=[q_spec, k_spec, v_spec] + b_specs + [pl.BlockSpec((ATT_GROUP * ATT_TQ, 128), lambda p, i: (i, p))],
        out_specs=[pl.BlockSpec((3, S, 128), lambda p, i: (0, 0, p)),
                   pl.BlockSpec((2, ATT_CLASSES, ATT_TQ, ATT_KW), lambda p, i: (p, 0, 0, 0))],
        out_shape=[jax.ShapeDtypeStruct((3, S, D_MODEL), BF16),
                   jax.ShapeDtypeStruct((ATT_HEADS, ATT_CLASSES, ATT_TQ, ATT_KW), F32)],
        scratch_shapes=[pltpu.VMEM((S, 128), F32), pltpu.VMEM((S, 128), F32)],
        dims=("parallel", "arbitrary"), args=(qkv, qkv, qkv) + (bias,) * ATT_GROUP + (do,))


def _me():
    return lax.axis_index("x"), lax.axis_index("y"), lax.axis_index("c")


def _other_chips(x, y):
    return [(1 - x, y), (x, 1 - y), (1 - x, 1 - y)]


def all_gather8(x_shard, name):
    m_per, n = x_shard.shape

    def body(x_ref, out_ref, send_sems, recv_sems, local_sem):
        x, y, c = _me()
        me, sibling = (x, y, c), (x, y, 1 - c)
        chips = _other_chips(x, y)

        def rows(px, py, pc):
            return out_ref.at[pl.ds((4 * px + 2 * py + pc) * m_per, m_per), :]

        def copy(k, block, to, src=None):
            return pltpu.make_async_remote_copy(
                src_ref=rows(*block) if src is None else src, dst_ref=rows(*block),
                send_sem=send_sems.at[k], recv_sem=recv_sems.at[k], device_id=to, device_id_type=MESH)

        mine = pltpu.make_async_copy(x_ref, rows(*me), local_sem)
        mine.start()
        first = [copy(0, me, sibling, src=x_ref)]
        first += [copy(1 + j, me, (*chip, c), src=x_ref) for j, chip in enumerate(chips)]
        for cp in first:
            cp.start()
        passed = [copy(4 + j, (*chip, c), sibling) for j, chip in enumerate(chips)]
        for j, chip in enumerate(chips):
            copy(1 + j, (*chip, c), me).wait_recv()
            passed[j].start()
        copy(0, sibling, me).wait_recv()
        for j, chip in enumerate(chips):
            copy(4 + j, (*chip, 1 - c), me).wait_recv()
        for cp in first + passed:
            cp.wait_send()
        mine.wait()

    return pl.pallas_call(
        body, name=name,
        out_shape=jax.ShapeDtypeStruct((N_DEV * m_per, n), x_shard.dtype),
        in_specs=[pl.BlockSpec(memory_space=pltpu.VMEM)],
        out_specs=pl.BlockSpec(memory_space=pltpu.VMEM),
        scratch_shapes=[pltpu.SemaphoreType.DMA((7,)), pltpu.SemaphoreType.DMA((7,)), pltpu.SemaphoreType.DMA],
        compiler_params=pltpu.CompilerParams(vmem_limit_bytes=VMEM_LIMIT),
    )(x_shard)


def _half_rows(n_rows, c):
    h = n_rows // 2
    return pl.ds(c * h, h)


def _gathered_shape(shape, flavour):
    L, a, b = shape
    return {"col": (L, a, N_CHIPS * b), "row": (L, N_CHIPS * a, b), "lead": (N_CHIPS, L, a, b)}[flavour]


def _gathered_part(out_ref, shape, flavour, s, rows):
    L, a, b = shape
    if flavour == "col":
        return out_ref.at[:, rows, pl.ds(s * b, b)]
    if flavour == "row":
        return out_ref.at[:, pl.ds(s * a + rows.start, rows.size), :]
    return out_ref.at[s, :, rows, :]


def gather_side(shards, flavours):
    n = len(shards)
    shapes = [w.shape for w in shards]

    def copies(w_refs, out_refs, send_sems, recv_sems, local_sems):
        x, y, c = _me()
        sibling = (x, y, 1 - c)
        chips = _other_chips(x, y)
        me_s = 2 * x + y

        def copy(k, src, dst, to):
            return pltpu.make_async_remote_copy(src_ref=src, dst_ref=dst, send_sem=send_sems.at[k],
                                                recv_sem=recv_sems.at[k], device_id=to, device_id_type=MESH)

        own, first, landed, passed, passed_in = [], [], [], [], []
        for w in range(n):
            shp, fl = shapes[w], flavours[w]
            my_half = _half_rows(shp[1], c)
            sib_half = _half_rows(shp[1], 1 - c)
            own.append(copy(7 * w + 6, w_refs[w], _gathered_part(out_refs[w], shp, fl, me_s, pl.ds(0, shp[1])), sibling))
            for j, chip in enumerate(chips):
                s = 2 * chip[0] + chip[1]
                first.append(copy(7 * w + j, w_refs[w].at[:, my_half, :],
                                  _gathered_part(out_refs[w], shp, fl, me_s, my_half), (*chip, c)))
                part = _gathered_part(out_refs[w], shp, fl, s, my_half)
                landed.append(copy(7 * w + j, part, part, (*chip, c)))
                passed.append(copy(7 * w + 3 + j, part, part, sibling))
                theirs = _gathered_part(out_refs[w], shp, fl, s, sib_half)
                passed_in.append(copy(7 * w + 3 + j, theirs, theirs, sibling))
        return own, first, landed, passed, passed_in

    def start(*refs):
        own, first, _, _, _ = copies(*refs)
        for cp in first + own:
            cp.start()

    def wait(*refs):
        own, first, landed, passed, passed_in = copies(*refs)
        for arrived, onward in zip(landed, passed):
            arrived.wait_recv()
            onward.start()
        for cp in passed_in:
            cp.wait_recv()
        for cp in own:
            cp.wait()
        for cp in first + passed:
            cp.wait_send()

    out_shapes = [jax.ShapeDtypeStruct(_gathered_shape(s, f), w.dtype) for w, s, f in zip(shards, shapes, flavours)]
    return Side(shards, out_shapes, 7 * n, 0, start, wait)


def swap_side(gs):
    n = len(gs)

    def copies(g_refs, out_refs, send_sems, recv_sems, local_sems):
        x, y, c = _me()
        return [pltpu.make_async_remote_copy(
            src_ref=g_refs[w].at[:, _half_rows(gs[w].shape[1], 1 - c), :], dst_ref=out_refs[w],
            send_sem=send_sems.at[w], recv_sem=recv_sems.at[w], device_id=(x, y, 1 - c), device_id_type=MESH)
            for w in range(n)]

    def start(*refs):
        for cp in copies(*refs):
            cp.start()

    def wait(*refs):
        for cp in copies(*refs):
            cp.wait()

    out_shapes = [jax.ShapeDtypeStruct((g.shape[0], g.shape[1] // 2, g.shape[2]), g.dtype) for g in gs]
    return Side(gs, out_shapes, n, 0, start, wait)


def add_half(g, r1, c_idx, name):
    n, R, C = g.shape
    half = R // 2
    tr = _rows_block(half, C)
    nbh = half // tr

    def kern(c_ref, g_ref, r_ref, o_ref, ob_ref):
        p = g_ref[...] + r_ref[...]
        o_ref[...] = p
        ob_ref[...] = p.astype(BF16)

    spec = pl.BlockSpec((1, tr, C), lambda d, r, c_ref: (d, r, 0))
    return pl.pallas_call(
        kern, name=name,
        grid_spec=pltpu.PrefetchScalarGridSpec(
            num_scalar_prefetch=1, grid=(n, nbh),
            in_specs=[pl.BlockSpec((1, tr, C), lambda d, r, c_ref: (d, c_ref[0] * nbh + r, 0)), spec],
            out_specs=[spec, spec]),
        out_shape=[jax.ShapeDtypeStruct((n, half, C), F32), jax.ShapeDtypeStruct((n, half, C), BF16)],
        compiler_params=_cp(("parallel", "parallel")),
    )(c_idx, g, r1)


def exchange_side(ps):
    n = len(ps)

    def copies(p_refs, out_refs, send_sems, recv_sems, local_sems):
        x, y, c = _me()
        return [pltpu.make_async_remote_copy(
            src_ref=p_refs[w].at[2 * chip[0] + chip[1]], dst_ref=out_refs[w].at[j],
            send_sem=send_sems.at[3 * w + j], recv_sem=recv_sems.at[3 * w + j],
            device_id=(*chip, c), device_id_type=MESH)
            for w in range(n) for j, chip in enumerate(_other_chips(x, y))]

    def start(*refs):
        for cp in copies(*refs):
            cp.start()

    def wait(*refs):
        for cp in copies(*refs):
            cp.wait()

    return Side(ps, [jax.ShapeDtypeStruct((3,) + p.shape[1:], p.dtype) for p in ps], 3 * n, 0, start, wait)


def add_chips(p, r2, chip_idx, name):
    n, H, C = p.shape
    tr = _rows_block(H, C)

    def kern(s_ref, p_ref, r_ref, o_ref):
        o_ref[...] = ((p_ref[0] + r_ref[0].astype(F32)) + r_ref[1].astype(F32)) + r_ref[2].astype(F32)

    return pl.pallas_call(
        kern, name=name,
        grid_spec=pltpu.PrefetchScalarGridSpec(
            num_scalar_prefetch=1, grid=(H // tr,),
            in_specs=[pl.BlockSpec((1, tr, C), lambda r, s_ref: (s_ref[0], r, 0)),
                      pl.BlockSpec((3, tr, C), lambda r, s_ref: (0, r, 0))],
            out_specs=pl.BlockSpec((tr, C), lambda r, s_ref: (r, 0))),
        out_shape=jax.ShapeDtypeStruct((H, C), F32),
        compiler_params=_cp(("parallel",)),
    )(chip_idx, p, r2)


def swap_reduced(ss, name):
    n = len(ss)

    def body(*refs):
        s_refs, out_refs = refs[:n], refs[n:2 * n]
        send_sems, recv_sems = refs[2 * n:]
        x, y, c = _me()
        cps = [pltpu.make_async_remote_copy(src_ref=s_refs[w], dst_ref=out_refs[w], send_sem=send_sems.at[w],
                                            recv_sem=recv_sems.at[w], device_id=(x, y, 1 - c), device_id_type=MESH)
               for w in range(n)]
        for cp in cps:
            cp.start()
        for cp in cps:
            cp.wait()

    any_spec = pl.BlockSpec(memory_space=pl.ANY)
    return pl.pallas_call(
        body, name=name, out_shape=[jax.ShapeDtypeStruct(s.shape, s.dtype) for s in ss],
        in_specs=[any_spec] * n, out_specs=[any_spec] * n,
        scratch_shapes=[pltpu.SemaphoreType.DMA((n,)), pltpu.SemaphoreType.DMA((n,))],
    )(*ss)


BIG = (("gla_w_in", 2, (1024, GLA_IN // N_CHIPS), "lead"), ("gla_w_out", 2, (256, 1024), "row"),
       ("att_w_in", 2, (1024, 768), "col"), ("att_w_out", 2, (256, 1024), "row"),
       ("ff_w1", 4, (1024, 1024), "col"), ("ff_w2", 4, (1024, 1024), "row"))
FLAVOUR = {n: f for n, _, _, f in BIG}


def layer_weights(i):
    mixer = "gla" if i % 2 == 0 else "att"
    return (("in", mixer + "_w_in", i // 2), ("out", mixer + "_w_out", i // 2), ("w1", "ff_w1", i), ("w2", "ff_w2", i))


class Comm:
    def __init__(self, weights, core, chip):
        self.weights, self.core, self.chip = weights, core, chip
        self.c_idx = jnp.reshape(core, (1,)).astype(jnp.int32)
        self.chip_idx = jnp.reshape(chip, (1,)).astype(jnp.int32)
        self.reduced = {}

    def gather(self, items):
        shards = [self.weights[n][l:l + 1].astype(BF16) for _, n, l in items]
        return gather_side(shards, [FLAVOUR[n] for _, n, _ in items])

    def full_weights(self, items, gathered):
        W = {}
        for (role, n, _), w in zip(items, gathered):
            if n == "gla_w_in":
                w = jnp.pad(w.transpose(1, 2, 0, 3).reshape(1, D_MODEL, GLA_IN), ((0, 0), (0, 0), (0, GLA_IN_PAD - GLA_IN)))
            W[role] = (w, 0)
        return W

    def gather_now(self, items, name):
        return self.full_weights(items, run_side(self.gather(items), name))

    def swap(self, items):
        return swap_side([g for _, _, g in items])

    def reduce_begin(self, tag, items, swapped):
        ps = [add_half(g, r, self.c_idx, f"rs_add2_{tag}_{w}") for w, ((_, _, g), r) in enumerate(zip(items, swapped))]
        return tag, [(n, l) for n, l, _ in items], ps

    def exchange(self, pending):
        return exchange_side([pb for _, pb in pending[2]])

    def reduce_mid(self, pending, landed):
        tag, keys, ps = pending
        for w, (key, (p, _), r) in enumerate(zip(keys, ps, landed)):
            self.reduced[key] = add_chips(p, r, self.chip_idx, f"rs_add4_{tag}_{w}")

    def reduce_tail(self, tag, items):
        pending = self.reduce_begin(tag, items, run_side(self.swap(items), f"rs_swap_{tag}"))
        self.reduce_mid(pending, run_side(self.exchange(pending), f"rs_xchg_{tag}"))

    def reduce_end(self):
        keys = [(n, l) for n, L, _, _ in BIG for l in range(L)]
        mine = [self.reduced[k] for k in keys]
        theirs = swap_reduced(mine, "rs_join")
        low = self.core == 0
        full = {k: jnp.concatenate([jnp.where(low, m, t), jnp.where(low, t, m)], axis=0)
                for k, m, t in zip(keys, mine, theirs)}
        return {n: jnp.stack([full[(n, l)] for l in range(L)]) for n, L, _, _ in BIG}


def local_step(x, target, mods, comm, small):
    S, D = x.shape
    row = lambda v: v.reshape(1, -1)
    saved = []
    tiles = [att_bias_tiles(small["att_rel_bias"][j], f"att_tiles_{j}").reshape(ATT_HEADS, ATT_CLASSES, ATT_TQ, ATT_KW)
             for j in range(2)]
    wgk_p = [jnp.pad(small["gla_w_gk2"][j], ((0, 128 - GLA_RANK), (0, 0))).astype(BF16) for j in range(2)]

    u1 = modulate(x, row(mods[0, 1]), row(mods[0, 0]), "mod_first")
    Ws = [dict() for _ in range(DEPTH)]
    items0 = layer_weights(0)
    Ws[0].update(comm.gather_now(items0[:1], "gather_w0"))
    for i in range(DEPTH):
        j = i // 2
        W = Ws[i]
        sh1, sc1, g1, sh2, sc2, g2 = (row(mods[i, k]) for k in range(6))
        nxt = min(i + 1, DEPTH - 1)
        more = i + 1 < DEPTH
        nxt_items = layer_weights(nxt)
        in_items = list(items0[1:3]) if i == 0 else []
        mix_items = (list(items0[3:]) if i == 0 else []) + (list(nxt_items[:2]) if more else [])
        up_items = list(nxt_items[2:3]) if more else []
        down_items = list(nxt_items[3:]) if more else []

        def hosted(items):
            return comm.gather(items) if items else None

        def landed_weights(items, landed):
            for k, it in enumerate(items):
                layer = 0 if it in items0 and i == 0 else nxt
                Ws[layer].update(comm.full_weights([it], landed[k:k + 1]))

        side = hosted(in_items)
        if i % 2 == 0:
            proj = mm_plain(u1, *W["in"], f"gla_in_{i}", mode="bf16", side=side)
        else:
            proj = mm_plain(u1, *W["in"], f"att_in_{i}", mode="bf16", bias=row(small["att_b_in"][j]), side=side)
        proj, landed = proj if side is not None else (proj, [])
        landed_weights(in_items, landed)
        if i % 2 == 0:
            (zmix, states), landed = gla_fwd(proj, wgk_p[j], row(small["gla_b_gk"][j]), row(small["gla_g_norm"][j]),
                                             f"gla_fwd_{i}", hosted(mix_items))
        else:
            (zmix,), landed = attn_fwd(proj, tiles[j], f"att_fwd_{i}", hosted(mix_items))
            states = None
        landed_weights(mix_items, landed)
        (y1, x_mid, u2), _ = mm_down_ln(zmix, *W["out"], x, 1.0 + g1, row(small["ln_g"][i, 0]),
                                        row(small["ln_b"][i, 0]), sc2, sh2, f"mix_out_{i}")
        side = hosted(up_items)
        act = mm_plain(u2, *W["w1"], f"ff_up_{i}", mode="mlp_up", side=side)
        act, landed = act if side is not None else (act, [])
        landed_weights(up_items, landed)
        (y2, x_out, u_next), landed = mm_down_ln(act, *W["w2"], x_mid, 1.0 + g2, row(small["ln_g"][i, 1]),
                                                 row(small["ln_b"][i, 1]), row(mods[nxt, 1]), row(mods[nxt, 0]),
                                                 f"ff_out_{i}", side=hosted(down_items))
        landed_weights(down_items, landed)
        saved.append(dict(x_in=x, u1=u1, proj=proj, zmix=zmix, states=states, y1=y1, x_mid=x_mid, u2=u2,
                          act=act, y2=y2))
        x, u1 = x_out, u_next

    g_small = dict(ln_g=[None] * DEPTH, ln_b=[None] * DEPTH, gla_w_gk2=[None] * 2, gla_b_gk=[None] * 2,
                   gla_g_norm=[None] * 2, att_b_in=[None] * 2, att_rel_bias=[None] * 2)
    dmods = [None] * DEPTH
    later = []
    top = saved[DEPTH - 1]
    dz2, dy2, s_ln2, sq = loss_ln_bwd(x, target, top["x_mid"], top["y2"], 1.0 + row(mods[DEPTH - 1, 5]),
                                      row(small["ln_g"][DEPTH - 1, 1]), "loss_ln_bwd")

    for i in reversed(range(DEPTH)):
        j = i // 2
        sv = saved[i]
        W = Ws[i]
        sh1, sc1, g1, sh2, sc2, g2 = (row(mods[i, k]) for k in range(6))
        dh = mm_plain(dy2, *W["w2"], f"ff_dn_{i}", mode="mlp_dn", nt=True, h=sv["act"])
        g_w2 = mm_w_res(sv["act"], dy2, f"ff_w2g_{i}").reshape(N_CHIPS, D_FF // N_CHIPS, D)
        g_w1 = mm_w_res(sv["u2"], dh, f"ff_w1g_{i}", chips_out=True)
        items = [("ff_w1", i, g_w1), ("ff_w2", i, g_w2)] + later
        (dz1, dy1, s_m2, s_ln1), swapped = mm_down_comb(
            dh, *W["w1"], dz2, sv["x_mid"], 1.0 + sc2, f"ff_dx_{i}",
            ln=(sv["x_in"], sv["y1"], 1.0 + g1, row(small["ln_g"][i, 0])), side=comm.swap(items))
        pending = comm.reduce_begin(i, items, swapped)
        side = comm.exchange(pending)
        mixer = "gla" if i % 2 == 0 else "att"
        below = None
        if i > 0:
            below = (saved[i - 1]["x_mid"], saved[i - 1]["y2"], 1.0 + row(mods[i - 1, 5]), row(small["ln_g"][i - 1, 1]))
        if i % 2 == 0:
            g_out = mm_w(sv["zmix"], dy1, f"gla_wog_{i}").reshape(N_CHIPS, D // N_CHIPS, D)
            dzg = mm_plain(dy1, *W["out"], f"gla_dz_{i}", nt=True)
            (dproj, dwgk, dbgk, dgn), landed = gla_bwd(sv["proj"], sv["states"], dzg, wgk_p[j],
                                                       row(small["gla_b_gk"][j]), row(small["gla_g_norm"][j]),
                                                       f"gla_bwd_{i}", side)
            g_small["gla_w_gk2"][j] = dwgk[:GLA_RANK]
            g_small["gla_b_gk"][j] = dbgk[0]
            g_small["gla_g_norm"][j] = dgn[0].reshape(GLA_HEADS, GLA_DV_HEAD)
            gwi = mm_w_res(sv["u1"], dproj, f"gla_wig_{i}")[:, :GLA_IN]
            g_in = gwi.reshape(D, N_CHIPS, GLA_IN // N_CHIPS).transpose(1, 0, 2)
            outs, _ = mm_down_comb(dproj, *W["in"], dz1, sv["x_in"], 1.0 + sc1, f"mix_dx_{i}", ln=below)
        else:
            g_out = mm_w(sv["zmix"], dy1, f"att_wog_{i}").reshape(N_CHIPS, D // N_CHIPS, D)
            do = mm_plain(dy1, *W["out"], f"att_do_{i}", mode="bf16", nt=True)
            (dqkv, dbt), landed = attn_bwd(sv["proj"], tiles[j], do, f"att_bwd_{i}", side)
            g_small["att_rel_bias"][j] = att_bias_grad(dbt.reshape(ATT_HEADS * ATT_CLASSES, ATT_TQ, ATT_KW),
                                                       f"att_bias_{i}")
            g_in = mm_w_chips3(sv["u1"], dqkv, f"att_wig_{i}")
            outs, _ = mm_down_comb(dqkv, *W["in"], dz1, sv["x_in"], 1.0 + sc1, f"mix_dx_{i}", parts=3, ln=below)
        s_m1 = outs[1] if below is None else outs[2]
        if i % 2 == 1:
            g_small["att_b_in"][j] = s_m1[2:5].reshape(3 * D)
        comm.reduce_mid(pending, landed)
        later = [(mixer + "_w_in", j, g_in), (mixer + "_w_out", j, g_out)]
        g_small["ln_g"][i] = jnp.stack([s_ln1[0], s_ln2[0]])
        g_small["ln_b"][i] = jnp.stack([s_ln1[1], s_ln2[1]])
        dmods[i] = jnp.stack([s_m1[1], s_m1[0], s_ln1[2], s_m2[1], s_m2[0], s_ln2[2]])
        if below is None:
            dx = outs[0]
        else:
            dz2, dy2, s_ln2 = outs[0], outs[1], outs[3]
    comm.reduce_tail("last", later)

    g_small = {n: jnp.stack(v) for n, v in g_small.items()}
    return sq, dx, jnp.stack(dmods), g_small


SMALL_SHARDED = (("ln_g", (4, 2, 256)), ("ln_b", (4, 2, 256)), ("gla_g_norm", (2, 4, 64)),
                 ("gla_w_gk2", (2, 16, 128)), ("att_b_in", (2, 768)))
SMALL_FULL = dict(ln_g=(4, 2, 1024), ln_b=(4, 2, 1024), gla_g_norm=(2, 4, 256), gla_w_gk2=(2, 16, 512),
                  att_b_in=(2, 3072), gla_b_gk=(2, 512), att_rel_bias=(2, 16, 257))
SMALL_GRAD_ORDER = ("ln_g", "ln_b", "gla_g_norm", "gla_w_gk2", "att_b_in", "gla_b_gk", "att_rel_bias")


def _pack_small(arrs, rows_total):
    parts = []
    for a in arrs:
        flat = a.reshape(-1)
        pad = (-flat.shape[0]) % PACK_W
        parts.append(jnp.pad(flat, (0, pad)).reshape(-1, PACK_W))
    buf = jnp.concatenate(parts, axis=0)
    return jnp.pad(buf, ((0, rows_total - buf.shape[0]), (0, 0)))


def _unpack_small(buf, shapes):
    out, r = [], 0
    for shp in shapes:
        n = 1
        for s in shp:
            n *= s
        nr = (n + PACK_W - 1) // PACK_W
        out.append(buf[..., r:r + nr, :].reshape(buf.shape[:-2] + (nr * PACK_W,))[..., :n].reshape(buf.shape[:-2] + shp))
        r += nr
    return out


def _unshard_last(g4):
    nd = g4.ndim
    perm = tuple(range(1, nd - 1)) + (0, nd - 1)
    t = g4.transpose(perm)
    return t.reshape(t.shape[:-2] + (-1,))


def _shard_last(full, s):
    n = full.shape[-1] // N_CHIPS
    return lax.dynamic_slice_in_dim(full, s * n, n, axis=full.ndim - 1)


WEIGHT_NAMES = ("w_ada", "b_ada", "ln_g", "ln_b", "gla_w_in", "gla_w_gk2", "gla_b_gk", "gla_g_norm", "gla_w_out",
                "att_w_in", "att_b_in", "att_rel_bias", "att_w_out", "ff_w1", "ff_w2")


def kernel(x, c, w_ada, b_ada, ln_g, ln_b, gla_w_in, gla_w_gk2, gla_b_gk, gla_g_norm, gla_w_out, att_w_in, att_b_in, att_rel_bias, att_w_out, ff_w1, ff_w2, loss_target, m_w_ada, m_b_ada, m_ln_g, m_ln_b, m_gla_w_in, m_gla_w_gk2, m_gla_b_gk, m_gla_g_norm, m_gla_w_out, m_att_w_in, m_att_b_in, m_att_rel_bias, m_att_w_out, m_ff_w1, m_ff_w2, v_w_ada, v_b_ada, v_ln_g, v_ln_b, v_gla_w_in, v_gla_w_gk2, v_gla_b_gk, v_gla_g_norm, v_gla_w_out, v_att_w_in, v_att_b_in, v_att_rel_bias, v_att_w_out, v_ff_w1, v_ff_w2):
    weights = dict(w_ada=w_ada, b_ada=b_ada, ln_g=ln_g, ln_b=ln_b, gla_w_in=gla_w_in, gla_w_gk2=gla_w_gk2,
                   gla_b_gk=gla_b_gk, gla_g_norm=gla_g_norm, gla_w_out=gla_w_out, att_w_in=att_w_in,
                   att_b_in=att_b_in, att_rel_bias=att_rel_bias, att_w_out=att_w_out, ff_w1=ff_w1, ff_w2=ff_w2)
    mom1 = dict(w_ada=m_w_ada, b_ada=m_b_ada, ln_g=m_ln_g, ln_b=m_ln_b, gla_w_in=m_gla_w_in, gla_w_gk2=m_gla_w_gk2,
                gla_b_gk=m_gla_b_gk, gla_g_norm=m_gla_g_norm, gla_w_out=m_gla_w_out, att_w_in=m_att_w_in,
                att_b_in=m_att_b_in, att_rel_bias=m_att_rel_bias, att_w_out=m_att_w_out, ff_w1=m_ff_w1, ff_w2=m_ff_w2)
    mom2 = dict(w_ada=v_w_ada, b_ada=v_b_ada, ln_g=v_ln_g, ln_b=v_ln_b, gla_w_in=v_gla_w_in, gla_w_gk2=v_gla_w_gk2,
                gla_b_gk=v_gla_b_gk, gla_g_norm=v_gla_g_norm, gla_w_out=v_gla_w_out, att_w_in=v_att_w_in,
                att_b_in=v_att_b_in, att_rel_bias=v_att_rel_bias, att_w_out=v_att_w_out, ff_w1=v_ff_w1, ff_w2=v_ff_w2)

    ax, ay, ac = lax.axis_index("x"), lax.axis_index("y"), lax.axis_index("c")
    chip = 2 * ax + ay
    dev = 2 * chip + ac
    S = x.shape[1]
    x2 = x.reshape(S, D_MODEL)
    t2 = loss_target.reshape(S, D_MODEL)

    comm = Comm(weights, ac, chip)

    small_rows = 16
    spack = _pack_small([c] + [weights[n] for n, _ in SMALL_SHARDED], small_rows)
    sg = all_gather8(spack, "gather_small").reshape(N_DEV, small_rows, PACK_W)
    parts = _unpack_small(sg, [(1, D_MODEL)] + [shp for _, shp in SMALL_SHARDED])
    c_all = parts[0].reshape(N_DEV, D_MODEL)
    small = {n: _unshard_last(p[0::2]) for (n, _), p in zip(SMALL_SHARDED, parts[1:])}
    small["gla_b_gk"] = gla_b_gk
    small["att_rel_bias"] = att_rel_bias

    c_act = silu_rows(jnp.pad(c_all, ((0, 128 - N_DEV), (0, 0))), "silu_c")
    mods_part = jnp.stack([mm_plain(c_act, w_ada, l, f"ada_fwd_{l}", tm=128)[:N_DEV] for l in range(DEPTH)], axis=1)
    mods_part = mods_part.reshape(N_DEV, DEPTH * 6 * D_MODEL // N_CHIPS)
    mg = all_gather8(mods_part, "gather_mods").reshape(N_CHIPS, 2, N_DEV, DEPTH, 6 * D_MODEL // N_CHIPS)
    mods_mine = lax.dynamic_index_in_dim(mg[:, 0], dev, axis=1, keepdims=False)
    mods = mods_mine.transpose(1, 0, 2).reshape(DEPTH, 6 * D_MODEL) + b_ada
    mods = mods.reshape(DEPTH, 6, D_MODEL)

    sq, grad_x, dmods, g_small = local_step(x2, t2, mods, comm, small)
    loss = lax.psum(0.5 * sq[0, 0] / D_MODEL, ("x", "y", "c"))

    g_shard = comm.reduce_end()

    dm_flat = dmods.reshape(DEPTH, 6 * D_MODEL)
    g_rows = 80
    gpack = _pack_small([dm_flat] + [g_small[n] for n in SMALL_GRAD_ORDER], g_rows)
    gg = all_gather8(gpack, "gather_small_grads").reshape(N_DEV, g_rows, PACK_W)
    gsum = sum_over_devices(gg, "sum_small_grads")
    sums = _unpack_small(gsum, [(DEPTH, 6 * D_MODEL)] + [SMALL_FULL[n] for n in SMALL_GRAD_ORDER])
    grads = dict(b_ada=sums[0])
    for n, full_g in zip(SMALL_GRAD_ORDER, sums[1:]):
        grads[n] = full_g if n in ("gla_b_gk", "att_rel_bias") else _shard_last(full_g, chip)
    dm_all = _unpack_small(gg, [(DEPTH, 6 * D_MODEL)])[0]
    dm_cols = _shard_last(dm_all, chip).reshape(N_DEV, DEPTH * 6 * D_MODEL // N_CHIPS)
    dm_cols = jnp.pad(dm_cols, ((0, 128 - N_DEV), (0, 0))).astype(BF16)
    gwa = mm_w(c_act, dm_cols, "ada_bwd", ts=128)
    grads["w_ada"] = gwa.reshape(D_MODEL, DEPTH, 6 * D_MODEL // N_CHIPS).transpose(1, 0, 2)
    grads.update(g_shard)

    deltas, new_m, new_v = {}, {}, {}
    for n in WEIGHT_NAMES:
        deltas[n], new_m[n], new_v[n] = adamw(weights[n], grads[n], mom1[n], mom2[n], "adamw_" + n)

    return (loss, grad_x.reshape(1, S, D_MODEL), *[grads[n] for n in WEIGHT_NAMES], *[deltas[n] for n in WEIGHT_NAMES],
            *[new_m[n] for n in WEIGHT_NAMES], *[new_v[n] for n in WEIGHT_NAMES])
```

```python
import functools

import jax
import jax.numpy as jnp
from jax import lax
from jax.experimental import pallas as pl
from jax.experimental.pallas import tpu as pltpu

F32 = jnp.float32
BF16 = jnp.bfloat16
HIGHEST = lax.Precision.HIGHEST
MESH = pl.DeviceIdType.MESH

D_MODEL = 1024
DEPTH = 4
CHUNK = 64
GLA_HEADS = 4
GLA_DK = 512
GLA_DV = 1024
GLA_DK_HEAD = 128
GLA_DV_HEAD = 256
GLA_RANK = 16
GLA_IN = 3088
GLA_IN_PAD = 3200
GLA_LR_OFF = 3072
ATT_HEADS = 16
ATT_HD = 64
LEFT_CHUNKS = 8
MAX_REL = 128
N_REL = 257
D_FF = 4096
ALPHA = (2.0 * DEPTH) ** 0.25
LN_EPS = 1e-5
RMS_EPS = 1e-6
NEG_INF = -1e30
GLA_SCALE = GLA_DK_HEAD ** -0.5
ATT_SCALE = ATT_HD ** -0.5
ADAM_LR = 0.001
ADAM_B1 = 0.9
ADAM_B2 = 0.999
ADAM_EPS = 1e-08
ADAM_WD = 0.01
ADAM_STEP = 10

ATT_TQ = 256
ATT_KW = 768
GLA_TB = 256
GLA_GROUP = 2
VMEM_LIMIT = 56 * 1024 * 1024
WHOLE_WEIGHT_BYTES = 8 * 1024 * 1024
N_CHIPS = 4
N_DEV = 8
PACK_W = 1024


def _dot(a, b):
    return jnp.dot(a, b, preferred_element_type=F32)


def _dot_nt(a, b):
    return lax.dot_general(a, b, (((1,), (1,)), ((), ())), preferred_element_type=F32)


def _dot_tn(a, b):
    return lax.dot_general(a, b, (((0,), (0,)), ((), ())), preferred_element_type=F32)


def _cp(sem, vmem=VMEM_LIMIT):
    return pltpu.CompilerParams(dimension_semantics=sem, vmem_limit_bytes=vmem)


def _row_spec(n):
    return pl.BlockSpec((1, n), lambda *_: (0, 0))


def _sigmoid(x):
    return 1.0 / (1.0 + jnp.exp(-x))


def _log_sigmoid(x):
    return jnp.minimum(x, 0.0) - jnp.log1p(jnp.exp(-jnp.abs(x)))


class Side:
    def __init__(self, ins, out_shapes, n_sems, n_local, start, wait):
        self.ins, self.out_shapes, self.n_sems, self.n_local = list(ins), list(out_shapes), n_sems, n_local
        self.start, self.wait = start, wait

    def sem_shapes(self):
        return [pltpu.SemaphoreType.DMA((self.n_sems,)), pltpu.SemaphoreType.DMA((self.n_sems,)),
                pltpu.SemaphoreType.DMA((max(self.n_local, 1),))]


def run_side(side, name):
    n_in = len(side.ins)
    n_out = len(side.out_shapes)

    def body(*refs):
        ins, outs, sems = refs[:n_in], refs[n_in:n_in + n_out], refs[n_in + n_out:]
        side.start(ins, outs, *sems)
        side.wait(ins, outs, *sems)

    any_spec = pl.BlockSpec(memory_space=pl.ANY)
    return pl.pallas_call(body, name=name, out_shape=side.out_shapes, in_specs=[any_spec] * n_in,
                          out_specs=[any_spec] * n_out, scratch_shapes=side.sem_shapes())(*side.ins)


def hosted_call(main, side, *, name, grid, in_specs, out_specs, out_shape, scratch_shapes, dims, args):
    if side is None:
        outs = pl.pallas_call(main, name=name, grid=grid, in_specs=in_specs, out_specs=out_specs,
                              out_shape=out_shape, scratch_shapes=scratch_shapes, compiler_params=_cp(dims))(*args)
        return list(outs), []
    n_mi, n_mo, n_ms = len(in_specs), len(out_specs), len(scratch_shapes)
    n_si, n_so = len(side.ins), len(side.out_shapes)

    def kern(*refs):
        mi, si = refs[:n_mi], refs[n_mi:n_mi + n_si]
        o0 = n_mi + n_si
        mo, so = refs[o0:o0 + n_mo], refs[o0 + n_mo:o0 + n_mo + n_so]
        s0 = o0 + n_mo + n_so
        ms, sems = refs[s0:s0 + n_ms], refs[s0 + n_ms:]
        ids = [pl.program_id(d) for d in range(len(grid))]
        first = functools.reduce(jnp.logical_and, [i == 0 for i in ids])
        last = functools.reduce(jnp.logical_and, [i == g - 1 for i, g in zip(ids, grid)])

        @pl.when(first)
        def _():
            side.start(si, so, *sems)
        main(*mi, *mo, *ms)

        @pl.when(last)
        def _():
            side.wait(si, so, *sems)

    any_spec = pl.BlockSpec(memory_space=pl.ANY)
    outs = pl.pallas_call(
        kern, name=name, grid=grid, in_specs=list(in_specs) + [any_spec] * n_si,
        out_specs=list(out_specs) + [any_spec] * n_so, out_shape=list(out_shape) + side.out_shapes,
        scratch_shapes=list(scratch_shapes) + side.sem_shapes(),
        compiler_params=_cp(("arbitrary",) * len(grid)))(*args, *side.ins)
    return list(outs[:n_mo]), list(outs[n_mo:])


def modulate(x, sc, sh, name):
    S, D = x.shape
    tm = min(512, S)

    def kern(x_ref, sc_ref, sh_ref, u_ref):
        u_ref[...] = (x_ref[...] * (1.0 + sc_ref[...]) + sh_ref[...]).astype(BF16)

    return pl.pallas_call(
        kern, name=name, grid=(S // tm,),
        in_specs=[pl.BlockSpec((tm, D), lambda i: (i, 0)), _row_spec(D), _row_spec(D)],
        out_specs=pl.BlockSpec((tm, D), lambda i: (i, 0)),
        out_shape=jax.ShapeDtypeStruct((S, D), BF16),
        compiler_params=_cp(("parallel",)),
    )(x, sc, sh)


def silu_rows(c_all, name):
    def kern(c_ref, o_ref):
        c = c_ref[...]
        o_ref[...] = (c * _sigmoid(c)).astype(BF16)

    return pl.pallas_call(kern, name=name, out_shape=jax.ShapeDtypeStruct(c_all.shape, BF16))(c_all)


def sum_over_devices(g, name):
    n, R, C = g.shape

    def kern(g_ref, o_ref):
        acc = g_ref[0]
        for d in range(1, n):
            acc = acc + g_ref[d]
        o_ref[...] = acc

    return pl.pallas_call(kern, name=name, out_shape=jax.ShapeDtypeStruct((R, C), F32))(g)


def _rows_block(R, C, budget=1 << 20):
    if R * C * 4 <= budget or R % 8:
        return R
    tr = max(8, (budget // (C * 4)) // 8 * 8)
    while R % tr:
        tr -= 8
    return tr


def adamw(w, g, m, v, name):
    shape = w.shape
    C = shape[-1]
    R = w.size // C
    w2, g2, m2, v2 = (t.reshape(R, C) for t in (w, g, m, v))
    tr = _rows_block(R, C)
    c1 = 1.0 - ADAM_B1 ** ADAM_STEP
    c2 = 1.0 - ADAM_B2 ** ADAM_STEP

    def kern(w_ref, g_ref, m_ref, v_ref, d_ref, nm_ref, nv_ref):
        gg = g_ref[...]
        nm = ADAM_B1 * m_ref[...] + (1.0 - ADAM_B1) * gg
        nv = ADAM_B2 * v_ref[...] + (1.0 - ADAM_B2) * (gg * gg)
        m_hat = nm / c1
        v_hat = nv / c2
        d_ref[...] = -ADAM_LR * (m_hat / (jnp.sqrt(v_hat) + ADAM_EPS) + ADAM_WD * w_ref[...])
        nm_ref[...] = nm
        nv_ref[...] = nv

    spec = pl.BlockSpec((tr, C), lambda i: (i, 0))
    outs = pl.pallas_call(
        kern, name=name, grid=(R // tr,),
        in_specs=[spec] * 4, out_specs=[spec] * 3,
        out_shape=[jax.ShapeDtypeStruct((R, C), F32)] * 3,
        compiler_params=_cp(("parallel",)),
    )(w2, g2, m2, v2)
    return tuple(o.reshape(shape) for o in outs)


def _tn_for(N):
    for tn in (1024, 768, 640, 512, 384, 256, 128):
        if N % tn == 0:
            return tn
    return N


def mm_plain(a, b3, layer, name, *, mode="f32", nt=False, bias=None, h=None, tm=1024, side=None):
    M, K = a.shape
    N = b3.shape[1] if nt else b3.shape[2]
    if K * N * 2 <= WHOLE_WEIGHT_BYTES:
        tn, tm = N, min(tm, 512)
    else:
        tn = _tn_for(N)
    tm = min(tm, M)
    a_spec = pl.BlockSpec((tm, K), lambda j, i: (i, 0))
    if nt:
        b_spec = pl.BlockSpec((None, tn, K), lambda j, i: (layer, j, 0))
    else:
        b_spec = pl.BlockSpec((None, K, tn), lambda j, i: (layer, 0, j))
    o_spec = pl.BlockSpec((tm, tn), lambda j, i: (i, j))
    ins, in_specs = [a, b3], [a_spec, b_spec]
    if bias is not None:
        ins.append(bias)
        in_specs.append(pl.BlockSpec((1, tn), lambda j, i: (0, j)))
    if mode == "mlp_dn":
        ins.append(h)
        in_specs.append(o_spec)
    elif mode not in ("f32", "bf16", "mlp_up"):
        raise ValueError(mode)
    odt = F32 if mode == "f32" else BF16

    def kern(a_ref, b_ref, *rest):
        rest = list(rest)
        bias_ref = rest.pop(0) if bias is not None else None
        h_ref = rest.pop(0) if mode == "mlp_dn" else None
        o_ref = rest.pop(0)
        if nt:
            bt_ref = rest.pop(0)

            @pl.when(pl.program_id(1) == 0)
            def _():
                bt_ref[...] = b_ref[...].T
            acc = _dot(a_ref[...], bt_ref[...])
        else:
            acc = _dot(a_ref[...], b_ref[...].astype(BF16))
        if bias_ref is not None:
            acc = acc + bias_ref[...]
        if mode == "mlp_up":
            r = jnp.maximum(acc, 0.0)
            acc = r * r
        elif mode == "mlp_dn":
            acc = acc * (2.0 * jnp.sqrt(h_ref[...].astype(F32)))
        o_ref[...] = acc.astype(odt)

    outs, landed = hosted_call(
        kern, side, name=name, grid=(N // tn, M // tm), in_specs=in_specs, out_specs=[o_spec],
        out_shape=[jax.ShapeDtypeStruct((M, N), odt)],
        scratch_shapes=[pltpu.VMEM((K, tn), BF16)] if nt else [], dims=("parallel", "arbitrary"), args=tuple(ins))
    return outs[0] if side is None else (outs[0], landed)


def mm_down_ln(a, b3, layer, x_in, gate1p, ln_g, ln_b, sc_next, sh_next, name, *, side=None, tm=512):
    M, K = a.shape
    D = b3.shape[2]
    tm = min(tm, M)

    def kern(a_ref, b_ref, x_ref, gp_ref, lg_ref, lb_ref, sc_ref, sh_ref, y_ref, xo_ref, u_ref):
        y = _dot(a_ref[...], b_ref[...])
        y_ref[...] = y.astype(BF16)
        z = ALPHA * x_ref[...] + gp_ref[...] * y
        mu = jnp.mean(z, axis=-1, keepdims=True)
        zc = z - mu
        var = jnp.mean(zc * zc, axis=-1, keepdims=True)
        xo = (zc * lax.rsqrt(var + LN_EPS)) * lg_ref[...] + lb_ref[...]
        xo_ref[...] = xo
        u_ref[...] = (xo * (1.0 + sc_ref[...]) + sh_ref[...]).astype(BF16)

    tile = pl.BlockSpec((tm, D), lambda i: (i, 0))
    outs, landed = hosted_call(
        kern, side, name=name, grid=(M // tm,),
        in_specs=[pl.BlockSpec((tm, K), lambda i: (i, 0)), pl.BlockSpec((None, K, D), lambda i: (layer, 0, 0)), tile]
        + [_row_spec(D)] * 5,
        out_specs=[tile, tile, tile],
        out_shape=[jax.ShapeDtypeStruct((M, D), BF16), jax.ShapeDtypeStruct((M, D), F32),
                   jax.ShapeDtypeStruct((M, D), BF16)],
        scratch_shapes=[], dims=("parallel",), args=(a, b3, x_in, gate1p, ln_g, ln_b, sc_next, sh_next))
    return tuple(outs), landed


def mm_down_comb(a, b3, layer, dz, x_in, sc1p, name, *, parts=1, ln=None, side=None, tm=512):
    D, K = b3.shape[1], b3.shape[2]
    M = a.shape[-2]
    kp = K // parts
    tm = min(tm, M)
    n_ln = 0 if ln is None else 4

    def kern(*refs):
        a_refs = refs[:parts]
        b_ref, dz_ref, x_ref, sp_ref = refs[parts:parts + 4]
        ln_refs = refs[parts + 4:parts + 4 + n_ln]
        outs = refs[parts + 4 + n_ln:]

        @pl.when(pl.program_id(0) == 0)
        def _():
            for o in outs:
                if o.shape[0] == 8:
                    o[...] = jnp.zeros_like(o)
        if parts == 1:
            du = _dot_nt(a_refs[0][...], b_ref[...])
        else:
            du = _dot_nt(a_refs[0][...], b_ref[:, 0:kp])
            for p in range(1, parts):
                du = du + _dot_nt(a_refs[p][...], b_ref[:, p * kp:(p + 1) * kp])
        dx = ALPHA * dz_ref[...] + du * sp_ref[...]
        if ln is None:
            dx_ref, s_ref = outs
            dx_ref[...] = dx
        else:
            dzl_ref, dyl_ref, s_ref, sl_ref = outs
            _ln_bwd_tile(dx, *ln_refs, dzl_ref, dyl_ref, sl_ref)
        s_ref[0:1, :] += jnp.sum(du * x_ref[...], axis=0, keepdims=True)
        s_ref[1:2, :] += jnp.sum(du, axis=0, keepdims=True)
        if parts > 1:
            for p in range(parts):
                s_ref[2 + p:3 + p, :] += jnp.sum(a_refs[p][...].astype(F32), axis=0, keepdims=True)

    tile = pl.BlockSpec((tm, D), lambda i: (i, 0))
    sums = pl.BlockSpec((8, D), lambda i: (0, 0))
    if parts == 1:
        a_ins, a_specs = [a], [pl.BlockSpec((tm, K), lambda i: (i, 0))]
    else:
        assert kp == D and parts <= 6
        a_ins = [a] * parts
        a_specs = [pl.BlockSpec((None, tm, kp), functools.partial(lambda i, p: (p, i, 0), p=p)) for p in range(parts)]
    in_specs = a_specs + [pl.BlockSpec((None, D, K), lambda i: (layer, 0, 0)), tile, tile, _row_spec(D)]
    args = a_ins + [b3, dz, x_in, sc1p]
    if ln is None:
        out_specs = [tile, sums]
        out_shape = [jax.ShapeDtypeStruct((M, D), F32), jax.ShapeDtypeStruct((8, D), F32)]
    else:
        in_specs += [tile, tile, _row_spec(D), _row_spec(D)]
        args += list(ln)
        out_specs = [tile, tile, sums, sums]
        out_shape = [jax.ShapeDtypeStruct((M, D), F32), jax.ShapeDtypeStruct((M, D), BF16),
                     jax.ShapeDtypeStruct((8, D), F32), jax.ShapeDtypeStruct((8, D), F32)]
    return hosted_call(kern, side, name=name, grid=(M // tm,), in_specs=in_specs, out_specs=out_specs,
                       out_shape=out_shape, scratch_shapes=[], dims=("arbitrary",), args=tuple(args))


def mm_w(a, b, name, *, ts=2048, tk=512, chips_out=False, b_parts=1, tn=None):
    S, K = a.shape
    npart = b.shape[-1]
    N = npart * b_parts
    ts = min(ts, S)
    tk = min(tk, K)
    n_chip = N // N_CHIPS
    if tn is None:
        tn = _tn_for(n_chip if chips_out else npart)
    assert npart % tn == 0 and (not chips_out or n_chip % tn == 0)

    def kern(a_ref, b_ref, o_ref):
        @pl.when(pl.program_id(2) == 0)
        def _():
            o_ref[...] = jnp.zeros_like(o_ref)
        o_ref[...] += _dot_tn(a_ref[...], b_ref[...])

    if b_parts == 1:
        b_spec = pl.BlockSpec((ts, tn), lambda k, n, s: (s, n))
    else:
        per = npart // tn
        b_spec = pl.BlockSpec((None, ts, tn), lambda k, n, s: (n // per, s, n % per))
    if chips_out:
        per_chip = n_chip // tn
        o_spec = pl.BlockSpec((None, tk, tn), lambda k, n, s: (n // per_chip, k, n % per_chip))
        out_shape = jax.ShapeDtypeStruct((N_CHIPS, K, n_chip), F32)
    else:
        o_spec = pl.BlockSpec((tk, tn), lambda k, n, s: (k, n))
        out_shape = jax.ShapeDtypeStruct((K, N), F32)
    return pl.pallas_call(
        kern, name=name, grid=(K // tk, N // tn, S // ts),
        in_specs=[pl.BlockSpec((ts, tk), lambda k, n, s: (s, k)), b_spec],
        out_specs=o_spec, out_shape=out_shape,
        compiler_params=_cp(("parallel", "parallel", "arbitrary")),
    )(a, b)


def mm_w_chips3(a, b3, name, *, ts=512):
    S, K = a.shape
    P = b3.shape[2]
    n_chip = 3 * P // N_CHIPS
    ts = min(ts, S)
    pieces = []
    for chip in range(N_CHIPS):
        lo, hi = chip * n_chip, (chip + 1) * n_chip
        while lo < hi:
            part = lo // P
            w = min(hi, (part + 1) * P) - lo
            pieces.append((chip, lo - chip * n_chip, part, lo - part * P, w))
            lo += w

    def kern(a_ref, b_ref, o_ref):
        @pl.when(pl.program_id(0) == 0)
        def _():
            o_ref[...] = jnp.zeros_like(o_ref)
        at = a_ref[...].T
        for chip, oc, part, pc, w in pieces:
            o_ref[chip, :, oc:oc + w] += _dot(at, b_ref[part, :, pc:pc + w])

    return pl.pallas_call(
        kern, name=name, grid=(S // ts,),
        in_specs=[pl.BlockSpec((ts, K), lambda s: (s, 0)), pl.BlockSpec((3, ts, P), lambda s: (0, s, 0))],
        out_specs=pl.BlockSpec((N_CHIPS, K, n_chip), lambda s: (0, 0, 0)),
        out_shape=jax.ShapeDtypeStruct((N_CHIPS, K, n_chip), F32),
        compiler_params=_cp(("arbitrary",)),
    )(a, b3)


def mm_w_res(a, b, name, *, chips_out=False, ts=512):
    S, K = a.shape
    N = b.shape[1]
    ts = min(ts, S)
    n_chip = N // N_CHIPS

    def kern(a_ref, b_ref, o_ref):
        @pl.when(pl.program_id(0) == 0)
        def _():
            o_ref[...] = jnp.zeros_like(o_ref)
        at = a_ref[...].T
        if chips_out:
            for chip in range(N_CHIPS):
                o_ref[chip] += _dot(at, b_ref[:, chip * n_chip:(chip + 1) * n_chip])
        else:
            o_ref[...] += _dot(at, b_ref[...])

    o_shape = (N_CHIPS, K, n_chip) if chips_out else (K, N)
    return pl.pallas_call(
        kern, name=name, grid=(S // ts,),
        in_specs=[pl.BlockSpec((ts, K), lambda s: (s, 0)), pl.BlockSpec((ts, N), lambda s: (s, 0))],
        out_specs=pl.BlockSpec(o_shape, lambda s: (0,) * len(o_shape)),
        out_shape=jax.ShapeDtypeStruct(o_shape, F32),
        compiler_params=_cp(("arbitrary",)),
    )(a, b)


def mm_f32(a, b, name):
    def kern(a_ref, b_ref, o_ref):
        o_ref[...] = jnp.dot(a_ref[...], b_ref[...], precision=HIGHEST, preferred_element_type=F32)

    return pl.pallas_call(kern, name=name, out_shape=jax.ShapeDtypeStruct((a.shape[0], b.shape[1]), F32),
                          compiler_params=pltpu.CompilerParams(vmem_limit_bytes=VMEM_LIMIT))(a, b)


def _ln_bwd_tile(dxo_t, x_ref, y_ref, gp_ref, lg_ref, dz_ref, dy_ref, s_ref):
    yv = y_ref[...].astype(F32)
    z = ALPHA * x_ref[...] + gp_ref[...] * yv
    mu = jnp.mean(z, axis=-1, keepdims=True)
    zc = z - mu
    var = jnp.mean(zc * zc, axis=-1, keepdims=True)
    rstd = lax.rsqrt(var + LN_EPS)
    xhat = zc * rstd
    dxh = dxo_t * lg_ref[...]
    dz = rstd * (dxh - jnp.mean(dxh, axis=-1, keepdims=True)
                 - xhat * jnp.mean(dxh * xhat, axis=-1, keepdims=True))
    dz_ref[...] = dz
    dy_ref[...] = (gp_ref[...] * dz).astype(BF16)
    s_ref[0:1, :] += jnp.sum(dxo_t * xhat, axis=0, keepdims=True)
    s_ref[1:2, :] += jnp.sum(dxo_t, axis=0, keepdims=True)
    s_ref[2:3, :] += jnp.sum(dz * yv, axis=0, keepdims=True)


def loss_ln_bwd(x_out, target, x_in, y, gate1p, ln_g, name, *, tm=256):
    S, D = x_out.shape
    tm = min(tm, S)

    def kern(xo_ref, t_ref, x_ref, y_ref, gp_ref, lg_ref, dz_ref, dy_ref, s_ref, l_ref):
        @pl.when(pl.program_id(0) == 0)
        def _():
            s_ref[...] = jnp.zeros_like(s_ref)
            l_ref[...] = jnp.zeros_like(l_ref)
        e = xo_ref[...] - t_ref[...]
        l_ref[...] += jnp.sum(e * e)
        _ln_bwd_tile(e * (1.0 / D), x_ref, y_ref, gp_ref, lg_ref, dz_ref, dy_ref, s_ref)

    tile = pl.BlockSpec((tm, D), lambda i: (i, 0))
    return pl.pallas_call(
        kern, name=name, grid=(S // tm,),
        in_specs=[tile, tile, tile, tile, _row_spec(D), _row_spec(D)],
        out_specs=[tile, tile, pl.BlockSpec((8, D), lambda i: (0, 0)), pl.BlockSpec((8, 128), lambda i: (0, 0))],
        out_shape=[jax.ShapeDtypeStruct((S, D), F32), jax.ShapeDtypeStruct((S, D), BF16),
                   jax.ShapeDtypeStruct((8, D), F32), jax.ShapeDtypeStruct((8, 128), F32)],
        compiler_params=_cp(("arbitrary",)),
    )(x_out, target, x_in, y, gate1p, ln_g)


def _tri64():
    r = lax.broadcasted_iota(jnp.int32, (CHUNK, CHUNK), 0)
    c = lax.broadcasted_iota(jnp.int32, (CHUNK, CHUNK), 1)
    return r >= c


def _gla_chunk_common(proj_ref, rows, b, h):
    kc = slice(h * GLA_DK_HEAD, (h + 1) * GLA_DK_HEAD)
    bh = b[:, kc]
    ep = jnp.exp(bh)
    en = jnp.exp(-bh)
    bl = bh[CHUNK - 1:CHUNK, :]
    ee = jnp.exp(bl - bh)
    dec = jnp.exp(bl)
    q = proj_ref[rows, h * GLA_DK_HEAD:(h + 1) * GLA_DK_HEAD] * GLA_SCALE
    k = proj_ref[rows, GLA_DK + h * GLA_DK_HEAD:GLA_DK + (h + 1) * GLA_DK_HEAD]
    v = proj_ref[rows, 2 * GLA_DK + h * GLA_DV_HEAD:2 * GLA_DK + (h + 1) * GLA_DV_HEAD]
    g = proj_ref[rows, 2 * GLA_DK + GLA_DV + h * GLA_DV_HEAD:2 * GLA_DK + GLA_DV + (h + 1) * GLA_DV_HEAD]
    return ep, en, ee, dec, q, k, v, g


def gla_fwd(proj, wgk_p, bgk, gnorm, name, side=None):
    S = proj.shape[0]
    TB = min(GLA_TB, S)
    ncb = TB // CHUNK

    def kern(proj_ref, wgk_ref, bgk_ref, gn_ref, zg_ref, st_ref, state_scr, la_scr):
        @pl.when(pl.program_id(0) == 0)
        def _():
            state_scr[...] = jnp.zeros_like(state_scr)
        lr = proj_ref[:, GLA_LR_OFF:GLA_IN_PAD].astype(BF16)
        gk = _dot(lr, wgk_ref[...]) + bgk_ref[...]
        la_scr[...] = _log_sigmoid(gk) * (1.0 / 16.0)
        lower = _tri64()
        tri = lower.astype(F32)

        def group(gi, carry):
            rows = [pl.ds(pl.multiple_of((gi * GLA_GROUP + g) * CHUNK, CHUNK), CHUNK) for g in range(GLA_GROUP)]
            b = [jnp.dot(tri, la_scr[r, :], precision=HIGHEST, preferred_element_type=F32) for r in rows]
            P = [(g, h) for g in range(GLA_GROUP) for h in range(GLA_HEADS)]
            cm = {p: _gla_chunk_common(proj_ref, rows[p[0]], b[p[0]], p[1]) for p in P}
            qf = {p: (cm[p][4] * cm[p][0]).astype(BF16) for p in P}
            kn = {p: (cm[p][5] * cm[p][1]).astype(BF16) for p in P}
            qn = {p: (cm[p][4] * cm[p][1]).astype(BF16) for p in P}
            kp = {p: (cm[p][5] * cm[p][0]).astype(BF16) for p in P}
            ke = {p: (cm[p][5] * cm[p][2]).astype(BF16) for p in P}
            vb = {p: cm[p][6].astype(BF16) for p in P}
            a_f = {p: _dot_nt(qf[p], kn[p]) for p in P}
            a_b = {p: _dot_nt(qn[p], kp[p]) for p in P}
            upd = {p: _dot_tn(vb[p], ke[p]) for p in P}
            st = {(0, h): state_scr[h] for h in range(GLA_HEADS)}
            for g in range(GLA_GROUP):
                for h in range(GLA_HEADS):
                    st[(g + 1, h)] = st[(g, h)] * cm[(g, h)][3] + upd[(g, h)]
            o_st = {p: _dot_nt(qf[p], st[p].astype(BF16)) for p in P}
            amat = {p: jnp.where(lower, a_f[p], a_b[p]).astype(BF16) for p in P}
            o = {p: _dot(amat[p], vb[p]) + o_st[p] for p in P}
            for g, h in P:
                st_ref[gi * GLA_GROUP + g, h] = st[(g, h)]
            for h in range(GLA_HEADS):
                state_scr[h] = st[(GLA_GROUP, h)]
            for g, h in P:
                gate = cm[(g, h)][7]
                vc = slice(h * GLA_DV_HEAD, (h + 1) * GLA_DV_HEAD)
                r = lax.rsqrt(jnp.mean(o[(g, h)] * o[(g, h)], axis=-1, keepdims=True) + RMS_EPS)
                on = (o[(g, h)] * r) * gn_ref[:, vc]
                zg_ref[rows[g], vc] = (on * (gate * _sigmoid(gate))).astype(BF16)
            return carry

        lax.fori_loop(0, ncb // GLA_GROUP, group, 0)

    return hosted_call(
        kern, side, name=name, grid=(S // TB,),
        in_specs=[pl.BlockSpec((TB, GLA_IN_PAD), lambda i: (i, 0)),
                  pl.BlockSpec((128, GLA_DK), lambda i: (0, 0)), _row_spec(GLA_DK), _row_spec(GLA_DV)],
        out_specs=[pl.BlockSpec((TB, GLA_DV), lambda i: (i, 0)),
                   pl.BlockSpec((ncb, GLA_HEADS, GLA_DV_HEAD, GLA_DK_HEAD), lambda i: (i, 0, 0, 0))],
        out_shape=[jax.ShapeDtypeStruct((S, GLA_DV), BF16),
                   jax.ShapeDtypeStruct((S // CHUNK, GLA_HEADS, GLA_DV_HEAD, GLA_DK_HEAD), F32)],
        scratch_shapes=[pltpu.VMEM((GLA_HEADS, GLA_DV_HEAD, GLA_DK_HEAD), F32), pltpu.VMEM((TB, GLA_DK), F32)],
        dims=("arbitrary",), args=(proj, wgk_p, bgk, gnorm))


def gla_bwd(proj, states, dzg, wgk_p, bgk, gnorm, name, side=None):
    S = proj.shape[0]
    TB = min(GLA_TB, S)
    ncb = TB // CHUNK
    nb = S // TB

    def kern(proj_ref, st_ref, dzg_ref, wgk_ref, bgk_ref, gn_ref,
             dproj_ref, dwgk_ref, dbgk_ref, dgn_ref, dstate_scr, la_scr, gk_scr, dgk_scr):
        @pl.when(pl.program_id(0) == 0)
        def _():
            dstate_scr[...] = jnp.zeros_like(dstate_scr)
            dwgk_ref[...] = jnp.zeros_like(dwgk_ref)
            dbgk_ref[...] = jnp.zeros_like(dbgk_ref)
            dgn_ref[...] = jnp.zeros_like(dgn_ref)
        lr = proj_ref[:, GLA_LR_OFF:GLA_IN_PAD].astype(BF16)
        gk = _dot(lr, wgk_ref[...]) + bgk_ref[...]
        gk_scr[...] = gk
        la_scr[...] = _log_sigmoid(gk) * (1.0 / 16.0)
        lower = _tri64()
        tri = lower.astype(F32)
        r_i = lax.broadcasted_iota(jnp.int32, (CHUNK, CHUNK), 0)
        c_i = lax.broadcasted_iota(jnp.int32, (CHUNK, CHUNK), 1)
        triu = (c_i >= r_i).astype(F32)
        last_row = lax.broadcasted_iota(jnp.int32, (CHUNK, GLA_DK_HEAD), 0) == CHUNK - 1

        def group(gi, carry):
            cs = [ncb - 1 - (gi * GLA_GROUP + g) for g in range(GLA_GROUP)]
            rows = [pl.ds(pl.multiple_of(c * CHUNK, CHUNK), CHUNK) for c in cs]
            b = [jnp.dot(tri, la_scr[r, :], precision=HIGHEST, preferred_element_type=F32) for r in rows]
            P = [(g, h) for g in range(GLA_GROUP) for h in range(GLA_HEADS)]
            kcs = [slice(h * GLA_DK_HEAD, (h + 1) * GLA_DK_HEAD) for h in range(GLA_HEADS)]
            vcs = [slice(h * GLA_DV_HEAD, (h + 1) * GLA_DV_HEAD) for h in range(GLA_HEADS)]
            cm = {p: _gla_chunk_common(proj_ref, rows[p[0]], b[p[0]], p[1]) for p in P}
            ep, en, ee, dec = ({p: cm[p][i] for p in P} for i in range(4))
            qf = {p: cm[p][4] * cm[p][0] for p in P}
            kn = {p: cm[p][5] * cm[p][1] for p in P}
            qn = {p: cm[p][4] * cm[p][1] for p in P}
            kp = {p: cm[p][5] * cm[p][0] for p in P}
            ke = {p: cm[p][5] * cm[p][2] for p in P}
            qf_b, kn_b, qn_b, kp_b, ke_b = ({p: t[p].astype(BF16) for p in P} for t in (qf, kn, qn, kp, ke))
            vb = {p: cm[p][6].astype(BF16) for p in P}
            st = {p: st_ref[cs[p[0]], p[1]] for p in P}
            st_b = {p: st[p].astype(BF16) for p in P}
            a_f = {p: _dot_nt(qf_b[p], kn_b[p]) for p in P}
            a_b = {p: _dot_nt(qn_b[p], kp_b[p]) for p in P}
            o_st = {p: _dot_nt(qf_b[p], st_b[p]) for p in P}
            amat = {p: jnp.where(lower, a_f[p], a_b[p]).astype(BF16) for p in P}
            o = {p: _dot(amat[p], vb[p]) + o_st[p] for p in P}
            do_b, dgs = {}, {}
            for p in P:
                g, h = p
                r = lax.rsqrt(jnp.mean(o[p] * o[p], axis=-1, keepdims=True) + RMS_EPS)
                oh = o[p] * r
                gn = gn_ref[:, vcs[h]]
                gate = cm[p][7]
                sg = _sigmoid(gate)
                dz = dzg_ref[rows[g], vcs[h]]
                don = dz * (gate * sg)
                dgs[p] = dz * (oh * gn) * (sg * (1.0 + gate * (1.0 - sg)))
                dgn_ref[:, vcs[h]] += jnp.sum(don * oh, axis=0, keepdims=True)
                doh = don * gn
                do_b[p] = (r * (doh - oh * jnp.mean(doh * oh, axis=-1, keepdims=True))).astype(BF16)
            da = {p: _dot_nt(do_b[p], vb[p]) for p in P}
            dv_a = {p: _dot_tn(amat[p], do_b[p]) for p in P}
            dqf_st = {p: _dot(do_b[p], st_b[p]) for p in P}
            dst_upd = {p: _dot_tn(do_b[p], qf_b[p]) for p in P}
            dst = {(0, h): dstate_scr[h] for h in range(GLA_HEADS)}
            for g in range(GLA_GROUP):
                for h in range(GLA_HEADS):
                    dst[(g + 1, h)] = dst[(g, h)] * dec[(g, h)] + dst_upd[(g, h)]
            for h in range(GLA_HEADS):
                dstate_scr[h] = dst[(GLA_GROUP, h)]
            dst_b = {p: dst[p].astype(BF16) for p in P}
            dv = {p: dv_a[p] + _dot_nt(ke_b[p], dst_b[p]) for p in P}
            dke = {p: _dot(vb[p], dst_b[p]) for p in P}
            da_f = {p: jnp.where(lower, da[p], 0.0).astype(BF16) for p in P}
            da_b = {p: jnp.where(lower, 0.0, da[p]).astype(BF16) for p in P}
            dqf = {p: _dot(da_f[p], kn_b[p]) + dqf_st[p] for p in P}
            dkn = {p: _dot_tn(da_f[p], qf_b[p]) for p in P}
            dqn = {p: _dot(da_b[p], kp_b[p]) for p in P}
            dkp = {p: _dot_tn(da_b[p], qn_b[p]) for p in P}
            dbs = {}
            for p in P:
                ddec = jnp.sum(dst[p] * st[p], axis=0, keepdims=True)
                db = dqf[p] * qf[p] - dkn[p] * kn[p] - dqn[p] * qn[p] + dkp[p] * kp[p] - dke[p] * ke[p]
                dbl = jnp.sum(dke[p] * ke[p], axis=0, keepdims=True) + ddec * dec[p]
                dbs[p] = db + jnp.where(last_row, dbl, 0.0)
            dla = {p: jnp.dot(triu, dbs[p], precision=HIGHEST, preferred_element_type=F32) for p in P}
            for p in P:
                g, h = p
                dq = (dqf[p] * ep[p] + dqn[p] * en[p]) * GLA_SCALE
                dk = dkn[p] * en[p] + dkp[p] * ep[p] + dke[p] * ee[p]
                dgk_scr[rows[g], kcs[h]] = dla[p] * (1.0 / 16.0) * _sigmoid(-gk_scr[rows[g], kcs[h]])
                dproj_ref[rows[g], kcs[h]] = dq.astype(BF16)
                dproj_ref[rows[g], GLA_DK + h * GLA_DK_HEAD:GLA_DK + (h + 1) * GLA_DK_HEAD] = dk.astype(BF16)
                dproj_ref[rows[g], 2 * GLA_DK + h * GLA_DV_HEAD:2 * GLA_DK + (h + 1) * GLA_DV_HEAD] = dv[p].astype(BF16)
                dproj_ref[rows[g], 2 * GLA_DK + GLA_DV + h * GLA_DV_HEAD:
                          2 * GLA_DK + GLA_DV + (h + 1) * GLA_DV_HEAD] = dgs[p].astype(BF16)
            return carry

        lax.fori_loop(0, ncb // GLA_GROUP, group, 0)
        dgk = dgk_scr[...]
        dgk_b = dgk.astype(BF16)
        dproj_ref[:, GLA_LR_OFF:GLA_IN_PAD] = _dot_nt(dgk_b, wgk_ref[...]).astype(BF16)
        dwgk_ref[...] += _dot_tn(lr, dgk_b)
        dbgk_ref[...] += jnp.sum(dgk, axis=0, keepdims=True)

    rev = lambda i: (nb - 1 - i, 0)
    return hosted_call(
        kern, side, name=name, grid=(nb,),
        in_specs=[pl.BlockSpec((TB, GLA_IN_PAD), rev),
                  pl.BlockSpec((ncb, GLA_HEADS, GLA_DV_HEAD, GLA_DK_HEAD), lambda i: (nb - 1 - i, 0, 0, 0)),
                  pl.BlockSpec((TB, GLA_DV), rev),
                  pl.BlockSpec((128, GLA_DK), lambda i: (0, 0)), _row_spec(GLA_DK), _row_spec(GLA_DV)],
        out_specs=[pl.BlockSpec((TB, GLA_IN_PAD), rev),
                   pl.BlockSpec((128, GLA_DK), lambda i: (0, 0)), _row_spec(GLA_DK), _row_spec(GLA_DV)],
        out_shape=[jax.ShapeDtypeStruct((S, GLA_IN_PAD), BF16), jax.ShapeDtypeStruct((128, GLA_DK), F32),
                   jax.ShapeDtypeStruct((1, GLA_DK), F32), jax.ShapeDtypeStruct((1, GLA_DV), F32)],
        scratch_shapes=[pltpu.VMEM((GLA_HEADS, GLA_DV_HEAD, GLA_DK_HEAD), F32), pltpu.VMEM((TB, GLA_DK), F32),
                        pltpu.VMEM((TB, GLA_DK), F32), pltpu.VMEM((TB, GLA_DK), F32)],
        dims=("arbitrary",), args=(proj, states, dzg, wgk_p, bgk, gnorm))


ATT_TW = 1024
ATT_CLASSES = 3


def _att_window(i):
    return pl.multiple_of(jnp.maximum(i * ATT_TQ - LEFT_CHUNKS * CHUNK, 0), ATT_TQ)


def _att_rel_index():
    e = jnp.arange(ATT_TW)[None, :]
    d = jnp.where(e < ATT_KW, e, e - ATT_TW)
    off = (jnp.arange(ATT_CLASSES) * ATT_TQ)[:, None]
    return jnp.clip(off - d, -MAX_REL, MAX_REL) + MAX_REL


def _row_bits():
    return lax.broadcasted_iota(jnp.int32, (ATT_TQ, ATT_TW), 0)


def att_bias_tiles(rel_bias, name):
    pick = (jnp.arange(384)[:, None] == _att_rel_index().reshape(-1)[None, :]).astype(F32)
    tab = mm_f32(jnp.pad(rel_bias, ((0, 0), (0, 384 - N_REL))), pick, name + "_tab")
    tab = tab.reshape(ATT_HEADS * ATT_CLASSES, 1, ATT_TW)

    def kern(t_ref, o_ref):
        cls = pl.program_id(0) % ATT_CLASSES
        x = jnp.broadcast_to(t_ref[...], (ATT_TQ, ATT_TW))
        x = pltpu.roll(x, 0, 1, stride=1, stride_axis=0)
        x = x[:, :ATT_KW]
        qc = cls * (ATT_TQ // CHUNK) + lax.shift_right_arithmetic(
            lax.broadcasted_iota(jnp.int32, (ATT_TQ, ATT_KW), 0), 6)
        kc = lax.shift_right_arithmetic(lax.broadcasted_iota(jnp.int32, (ATT_TQ, ATT_KW), 1), 6)
        o_ref[...] = jnp.where((kc <= qc) & (kc >= qc - LEFT_CHUNKS), x, NEG_INF)

    return pl.pallas_call(
        kern, name=name, grid=(ATT_HEADS * ATT_CLASSES,),
        in_specs=[pl.BlockSpec((None, 1, ATT_TW), lambda i: (i, 0, 0))],
        out_specs=pl.BlockSpec((None, ATT_TQ, ATT_KW), lambda i: (i, 0, 0)),
        out_shape=jax.ShapeDtypeStruct((ATT_HEADS * ATT_CLASSES, ATT_TQ, ATT_KW), F32),
        compiler_params=_cp(("parallel",)),
    )(tab)


def att_bias_grad(dbt, name):
    def kern(d_ref, o_ref):
        x = jnp.concatenate([d_ref[...], jnp.zeros((ATT_TQ, ATT_TW - ATT_KW), F32)], axis=1)
        x = jnp.concatenate([pltpu.roll(x[r0:r0 + 8, :], (ATT_TW - r0) % ATT_TW, axis=1)
                             for r0 in range(0, ATT_TQ, 8)], axis=0)
        row = _row_bits()
        for b in range(3):
            x = jnp.where((row & (1 << b)) != 0, pltpu.roll(x, ATT_TW - (1 << b), axis=1), x)
        o_ref[...] = jnp.sum(x, axis=0, keepdims=True)

    diag = pl.pallas_call(
        kern, name=name + "_diag", grid=(ATT_HEADS * ATT_CLASSES,),
        in_specs=[pl.BlockSpec((None, ATT_TQ, ATT_KW), lambda i: (i, 0, 0))],
        out_specs=pl.BlockSpec((None, 1, ATT_TW), lambda i: (i, 0, 0)),
        out_shape=jax.ShapeDtypeStruct((ATT_HEADS * ATT_CLASSES, 1, ATT_TW), F32),
        compiler_params=_cp(("parallel",)),
    )(dbt)
    diag = diag.reshape(ATT_HEADS, ATT_CLASSES * ATT_TW)
    onehot = (_att_rel_index().reshape(-1)[:, None] == jnp.arange(384)[None, :]).astype(F32)
    return mm_f32(diag, onehot, name + "_bins")[:, :N_REL]


ATT_GROUP = 2


def _att_scores(q_ref, k_ref, bias_refs, blk0):
    G = range(ATT_GROUP)
    hs = [slice(hh * ATT_HD, (hh + 1) * ATT_HD) for hh in range(2)]
    rows = [slice(g * ATT_TQ, (g + 1) * ATT_TQ) for g in G]
    wins = [pl.ds(_att_window(blk0 + g), ATT_KW) for g in G]
    kw = [k_ref[w, :] for w in wins]
    P = [(g, hh) for g in G for hh in range(2)]
    q = {p: q_ref[rows[p[0]], hs[p[1]]] * ATT_SCALE for p in P}
    k = {p: kw[p[0]][:, hs[p[1]]] for p in P}
    s = {p: _dot_nt(q[p], k[p]) + bias_refs[p[0]][p[1]] for p in P}
    e = {p: jnp.exp(s[p] - jnp.max(s[p], axis=-1, keepdims=True)) for p in P}
    inv = {p: 1.0 / jnp.sum(e[p], axis=-1, keepdims=True) for p in P}
    return P, rows, wins, hs, q, k, e, inv


def _att_specs(S):
    nq = D_MODEL // 128
    q_spec = pl.BlockSpec((ATT_GROUP * ATT_TQ, 128), lambda p, i: (i, p))
    k_spec = pl.BlockSpec((S, 128), lambda p, i: (0, nq + p))
    v_spec = pl.BlockSpec((S, 128), lambda p, i: (0, 2 * nq + p))
    b_specs = [pl.BlockSpec((2, None, ATT_TQ, ATT_KW),
                            functools.partial(lambda p, i, g: (p, jnp.minimum(ATT_GROUP * i + g, ATT_CLASSES - 1), 0, 0), g=g))
               for g in range(ATT_GROUP)]
    return q_spec, k_spec, v_spec, b_specs


def attn_fwd(qkv, bias, name, side=None):
    S = qkv.shape[0]
    q_spec, k_spec, v_spec, b_specs = _att_specs(S)

    def kern(q_ref, k_ref, v_ref, *rest):
        bias_refs, o_ref = rest[:ATT_GROUP], rest[ATT_GROUP]
        P, rows, wins, hs, _, _, e, inv = _att_scores(q_ref, k_ref, bias_refs, ATT_GROUP * pl.program_id(1))
        vw = [v_ref[w, :] for w in wins]
        o = {p: _dot(e[p].astype(BF16), vw[p[0]][:, hs[p[1]]]) * inv[p] for p in P}
        for g in range(ATT_GROUP):
            o_ref[rows[g], :] = jnp.concatenate([o[(g, 0)], o[(g, 1)]], axis=1).astype(BF16)

    return hosted_call(
        kern, side, name=name, grid=(ATT_HEADS // 2, S // (ATT_GROUP * ATT_TQ)),
        in_specs=[q_spec, k_spec, v_spec] + b_specs,
        out_specs=[pl.BlockSpec((ATT_GROUP * ATT_TQ, 128), lambda p, i: (i, p))],
        out_shape=[jax.ShapeDtypeStruct((S, D_MODEL), BF16)],
        scratch_shapes=[], dims=("parallel", "arbitrary"), args=(qkv, qkv, qkv) + (bias,) * ATT_GROUP)


def attn_bwd(qkv, bias, do, name, side=None):
    S = qkv.shape[0]
    nstep = S // (ATT_GROUP * ATT_TQ)
    q_spec, k_spec, v_spec, b_specs = _att_specs(S)

    def kern(q_ref, k_ref, v_ref, *rest):
        bias_refs = rest[:ATT_GROUP]
        do_ref, dqkv_ref, db_ref, dk_scr, dv_scr = rest[ATT_GROUP:]
        i = pl.program_id(1)

        @pl.when(i == 0)
        def _():
            dk_scr[...] = jnp.zeros_like(dk_scr)
            dv_scr[...] = jnp.zeros_like(dv_scr)
            db_ref[...] = jnp.zeros_like(db_ref)
        blk0 = ATT_GROUP * i
        P, rows, wins, hs, q, k, e, inv = _att_scores(q_ref, k_ref, bias_refs, blk0)
        vw = [v_ref[w, :] for w in wins]
        do_h = {p: do_ref[rows[p[0]], hs[p[1]]] for p in P}
        dp = {p: _dot_nt(do_h[p], vw[p[0]][:, hs[p[1]]]) for p in P}
        pr = {p: e[p] * inv[p] for p in P}
        dvs = {p: _dot_tn(pr[p].astype(BF16), do_h[p]) for p in P}
        ds = {p: pr[p] * (dp[p] - jnp.sum(pr[p] * dp[p], axis=-1, keepdims=True)) for p in P}
        ds_b = {p: ds[p].astype(BF16) for p in P}
        dqs = {p: _dot(ds_b[p], k[p]) * ATT_SCALE for p in P}
        dks = {p: _dot_tn(ds_b[p], q[p]) for p in P}
        for g, hh in P:
            db_ref[hh, jnp.minimum(blk0 + g, ATT_CLASSES - 1)] += ds[(g, hh)]
        for g in range(ATT_GROUP):
            first = pl.multiple_of((blk0 + g) * ATT_TQ, ATT_TQ)
            dqkv_ref[0, pl.ds(first, ATT_TQ), :] = jnp.concatenate([dqs[(g, 0)], dqs[(g, 1)]], axis=1).astype(BF16)
            dk_scr[wins[g], :] += jnp.concatenate([dks[(g, 0)], dks[(g, 1)]], axis=1)
            dv_scr[wins[g], :] += jnp.concatenate([dvs[(g, 0)], dvs[(g, 1)]], axis=1)

        @pl.when(i == nstep - 1)
        def _():
            dqkv_ref[1] = dk_scr[...].astype(BF16)
            dqkv_ref[2] = dv_scr[...].astype(BF16)

    return hosted_call(
        kern, side, name=name, grid=(ATT_HEADS // 2, nstep),
        in_specs=[q_spec, k_spec, v_spec] + b_specs + [pl.BlockSpec((ATT_GROUP * ATT_TQ, 128), lambda p, i: (i, p))],
        out_specs=[pl.BlockSpec((3, S, 128), lambda p, i: (0, 0, p)),
                   pl.BlockSpec((2, ATT_CLASSES, ATT_TQ, ATT_KW), lambda p, i: (p, 0, 0, 0))],
        out_shape=[jax.ShapeDtypeStruct((3, S, D_MODEL), BF16),
                   jax.ShapeDtypeStruct((ATT_HEADS, ATT_CLASSES, ATT_TQ, ATT_KW), F32)],
        scratch_shapes=[pltpu.VMEM((S, 128), F32), pltpu.VMEM((S, 128), F32)],
        dims=("parallel", "arbitrary"), args=(qkv, qkv, qkv) + (bias,) * ATT_GROUP + (do,))


def _me():
    return lax.axis_index("x"), lax.axis_index("y"), lax.axis_index("c")


def _other_chips(x, y):
    return [(1 - x, y), (x, 1 - y), (1 - x, 1 - y)]


def all_gather8(x_shard, name):
    m_per, n = x_shard.shape

    def body(x_ref, out_ref, send_sems, recv_sems, local_sem):
        x, y, c = _me()
        me, sibling = (x, y, c), (x, y, 1 - c)
        chips = _other_chips(x, y)

        def rows(px, py, pc):
            return out_ref.at[pl.ds((4 * px + 2 * py + pc) * m_per, m_per), :]

        def copy(k, block, to, src=None):
            return pltpu.make_async_remote_copy(
                src_ref=rows(*block) if src is None else src, dst_ref=rows(*block),
                send_sem=send_sems.at[k], recv_sem=recv_sems.at[k], device_id=to, device_id_type=MESH)

        mine = pltpu.make_async_copy(x_ref, rows(*me), local_sem)
        mine.start()
        first = [copy(0, me, sibling, src=x_ref)]
        first += [copy(1 + j, me, (*chip, c), src=x_ref) for j, chip in enumerate(chips)]
        for cp in first:
            cp.start()
        passed = [copy(4 + j, (*chip, c), sibling) for j, chip in enumerate(chips)]
        for j, chip in enumerate(chips):
            copy(1 + j, (*chip, c), me).wait_recv()
            passed[j].start()
        copy(0, sibling, me).wait_recv()
        for j, chip in enumerate(chips):
            copy(4 + j, (*chip, 1 - c), me).wait_recv()
        for cp in first + passed:
            cp.wait_send()
        mine.wait()

    return pl.pallas_call(
        body, name=name,
        out_shape=jax.ShapeDtypeStruct((N_DEV * m_per, n), x_shard.dtype),
        in_specs=[pl.BlockSpec(memory_space=pltpu.VMEM)],
        out_specs=pl.BlockSpec(memory_space=pltpu.VMEM),
        scratch_shapes=[pltpu.SemaphoreType.DMA((7,)), pltpu.SemaphoreType.DMA((7,)), pltpu.SemaphoreType.DMA],
        compiler_params=pltpu.CompilerParams(vmem_limit_bytes=VMEM_LIMIT),
    )(x_shard)


def _half_rows(n_rows, c):
    h = n_rows // 2
    return pl.ds(c * h, h)


def _gathered_shape(shape, flavour):
    L, a, b = shape
    return {"col": (L, a, N_CHIPS * b), "row": (L, N_CHIPS * a, b), "lead": (N_CHIPS, L, a, b)}[flavour]


def _gathered_part(out_ref, shape, flavour, s, rows):
    L, a, b = shape
    if flavour == "col":
        return out_ref.at[:, rows, pl.ds(s * b, b)]
    if flavour == "row":
        return out_ref.at[:, pl.ds(s * a + rows.start, rows.size), :]
    return out_ref.at[s, :, rows, :]


def gather_side(shards, flavours):
    n = len(shards)
    shapes = [w.shape for w in shards]

    def copies(w_refs, out_refs, send_sems, recv_sems, local_sems):
        x, y, c = _me()
        sibling = (x, y, 1 - c)
        chips = _other_chips(x, y)
        me_s = 2 * x + y

        def copy(k, src, dst, to):
            return pltpu.make_async_remote_copy(src_ref=src, dst_ref=dst, send_sem=send_sems.at[k],
                                                recv_sem=recv_sems.at[k], device_id=to, device_id_type=MESH)

        own, first, landed, passed, passed_in = [], [], [], [], []
        for w in range(n):
            shp, fl = shapes[w], flavours[w]
            my_half = _half_rows(shp[1], c)
            sib_half = _half_rows(shp[1], 1 - c)
            own.append(copy(7 * w + 6, w_refs[w], _gathered_part(out_refs[w], shp, fl, me_s, pl.ds(0, shp[1])), sibling))
            for j, chip in enumerate(chips):
                s = 2 * chip[0] + chip[1]
                first.append(copy(7 * w + j, w_refs[w].at[:, my_half, :],
                                  _gathered_part(out_refs[w], shp, fl, me_s, my_half), (*chip, c)))
                part = _gathered_part(out_refs[w], shp, fl, s, my_half)
                landed.append(copy(7 * w + j, part, part, (*chip, c)))
                passed.append(copy(7 * w + 3 + j, part, part, sibling))
                theirs = _gathered_part(out_refs[w], shp, fl, s, sib_half)
                passed_in.append(copy(7 * w + 3 + j, theirs, theirs, sibling))
        return own, first, landed, passed, passed_in

    def start(*refs):
        own, first, _, _, _ = copies(*refs)
        for cp in first + own:
            cp.start()

    def wait(*refs):
        own, first, landed, passed, passed_in = copies(*refs)
        for arrived, onward in zip(landed, passed):
            arrived.wait_recv()
            onward.start()
        for cp in passed_in:
            cp.wait_recv()
        for cp in own:
            cp.wait()
        for cp in first + passed:
            cp.wait_send()

    out_shapes = [jax.ShapeDtypeStruct(_gathered_shape(s, f), w.dtype) for w, s, f in zip(shards, shapes, flavours)]
    return Side(shards, out_shapes, 7 * n, 0, start, wait)


def swap_side(gs):
    n = len(gs)

    def copies(g_refs, out_refs, send_sems, recv_sems, local_sems):
        x, y, c = _me()
        return [pltpu.make_async_remote_copy(
            src_ref=g_refs[w].at[:, _half_rows(gs[w].shape[1], 1 - c), :], dst_ref=out_refs[w],
            send_sem=send_sems.at[w], recv_sem=recv_sems.at[w], device_id=(x, y, 1 - c), device_id_type=MESH)
            for w in range(n)]

    def start(*refs):
        for cp in copies(*refs):
            cp.start()

    def wait(*refs):
        for cp in copies(*refs):
            cp.wait()

    out_shapes = [jax.ShapeDtypeStruct((g.shape[0], g.shape[1] // 2, g.shape[2]), g.dtype) for g in gs]
    return Side(gs, out_shapes, n, 0, start, wait)


def add_half(g, r1, c_idx, name):
    n, R, C = g.shape
    half = R // 2
    tr = _rows_block(half, C)
    nbh = half // tr

    def kern(c_ref, g_ref, r_ref, o_ref, ob_ref):
        p = g_ref[...] + r_ref[...]
        o_ref[...] = p
        ob_ref[...] = p.astype(BF16)

    spec = pl.BlockSpec((1, tr, C), lambda d, r, c_ref: (d, r, 0))
    return pl.pallas_call(
        kern, name=name,
        grid_spec=pltpu.PrefetchScalarGridSpec(
            num_scalar_prefetch=1, grid=(n, nbh),
            in_specs=[pl.BlockSpec((1, tr, C), lambda d, r, c_ref: (d, c_ref[0] * nbh + r, 0)), spec],
            out_specs=[spec, spec]),
        out_shape=[jax.ShapeDtypeStruct((n, half, C), F32), jax.ShapeDtypeStruct((n, half, C), BF16)],
        compiler_params=_cp(("parallel", "parallel")),
    )(c_idx, g, r1)


def exchange_side(ps):
    n = len(ps)

    def copies(p_refs, out_refs, send_sems, recv_sems, local_sems):
        x, y, c = _me()
        return [pltpu.make_async_remote_copy(
            src_ref=p_refs[w].at[2 * chip[0] + chip[1]], dst_ref=out_refs[w].at[j],
            send_sem=send_sems.at[3 * w + j], recv_sem=recv_sems.at[3 * w + j],
            device_id=(*chip, c), device_id_type=MESH)
            for w in range(n) for j, chip in enumerate(_other_chips(x, y))]

    def start(*refs):
        for cp in copies(*refs):
            cp.start()

    def wait(*refs):
        for cp in copies(*refs):
            cp.wait()

    return Side(ps, [jax.ShapeDtypeStruct((3,) + p.shape[1:], p.dtype) for p in ps], 3 * n, 0, start, wait)


def add_chips(p, r2, chip_idx, name):
    n, H, C = p.shape
    tr = _rows_block(H, C)

    def kern(s_ref, p_ref, r_ref, o_ref):
        o_ref[...] = ((p_ref[0] + r_ref[0].astype(F32)) + r_ref[1].astype(F32)) + r_ref[2].astype(F32)

    return pl.pallas_call(
        kern, name=name,
        grid_spec=pltpu.PrefetchScalarGridSpec(
            num_scalar_prefetch=1, grid=(H // tr,),
            in_specs=[pl.BlockSpec((1, tr, C), lambda r, s_ref: (s_ref[0], r, 0)),
                      pl.BlockSpec((3, tr, C), lambda r, s_ref: (0, r, 0))],
            out_specs=pl.BlockSpec((tr, C), lambda r, s_ref: (r, 0))),
        out_shape=jax.ShapeDtypeStruct((H, C), F32),
        compiler_params=_cp(("parallel",)),
    )(chip_idx, p, r2)


def swap_reduced(ss, name):
    n = len(ss)

    def body(*refs):
        s_refs, out_refs = refs[:n], refs[n:2 * n]
        send_sems, recv_sems = refs[2 * n:]
        x, y, c = _me()
        cps = [pltpu.make_async_remote_copy(src_ref=s_refs[w], dst_ref=out_refs[w], send_sem=send_sems.at[w],
                                            recv_sem=recv_sems.at[w], device_id=(x, y, 1 - c), device_id_type=MESH)
               for w in range(n)]
        for cp in cps:
            cp.start()
        for cp in cps:
            cp.wait()

    any_spec = pl.BlockSpec(memory_space=pl.ANY)
    return pl.pallas_call(
        body, name=name, out_shape=[jax.ShapeDtypeStruct(s.shape, s.dtype) for s in ss],
        in_specs=[any_spec] * n, out_specs=[any_spec] * n,
        scratch_shapes=[pltpu.SemaphoreType.DMA((n,)), pltpu.SemaphoreType.DMA((n,))],
    )(*ss)


BIG = (("gla_w_in", 2, (1024, GLA_IN // N_CHIPS), "lead"), ("gla_w_out", 2, (256, 1024), "row"),
       ("att_w_in", 2, (1024, 768), "col"), ("att_w_out", 2, (256, 1024), "row"),
       ("ff_w1", 4, (1024, 1024), "col"), ("ff_w2", 4, (1024, 1024), "row"))
FLAVOUR = {n: f for n, _, _, f in BIG}


def layer_weights(i):
    mixer = "gla" if i % 2 == 0 else "att"
    return (("in", mixer + "_w_in", i // 2), ("out", mixer + "_w_out", i // 2), ("w1", "ff_w1", i), ("w2", "ff_w2", i))


class Comm:
    def __init__(self, weights, core, chip):
        self.weights, self.core, self.chip = weights, core, chip
        self.c_idx = jnp.reshape(core, (1,)).astype(jnp.int32)
        self.chip_idx = jnp.reshape(chip, (1,)).astype(jnp.int32)
        self.reduced = {}

    def gather(self, items):
        shards = [self.weights[n][l:l + 1].astype(BF16) for _, n, l in items]
        return gather_side(shards, [FLAVOUR[n] for _, n, _ in items])

    def full_weights(self, items, gathered):
        W = {}
        for (role, n, _), w in zip(items, gathered):
            if n == "gla_w_in":
                w = jnp.pad(w.transpose(1, 2, 0, 3).reshape(1, D_MODEL, GLA_IN), ((0, 0), (0, 0), (0, GLA_IN_PAD - GLA_IN)))
            W[role] = (w, 0)
        return W

    def gather_now(self, items, name):
        return self.full_weights(items, run_side(self.gather(items), name))

    def swap(self, items):
        return swap_side([g for _, _, g in items])

    def reduce_begin(self, tag, items, swapped):
        ps = [add_half(g, r, self.c_idx, f"rs_add2_{tag}_{w}") for w, ((_, _, g), r) in enumerate(zip(items, swapped))]
        return tag, [(n, l) for n, l, _ in items], ps

    def exchange(self, pending):
        return exchange_side([pb for _, pb in pending[2]])

    def reduce_mid(self, pending, landed):
        tag, keys, ps = pending
        for w, (key, (p, _), r) in enumerate(zip(keys, ps, landed)):
            self.reduced[key] = add_chips(p, r, self.chip_idx, f"rs_add4_{tag}_{w}")

    def reduce_tail(self, tag, items):
        pending = self.reduce_begin(tag, items, run_side(self.swap(items), f"rs_swap_{tag}"))
        self.reduce_mid(pending, run_side(self.exchange(pending), f"rs_xchg_{tag}"))

    def reduce_end(self):
        keys = [(n, l) for n, L, _, _ in BIG for l in range(L)]
        mine = [self.reduced[k] for k in keys]
        theirs = swap_reduced(mine, "rs_join")
        low = self.core == 0
        full = {k: jnp.concatenate([jnp.where(low, m, t), jnp.where(low, t, m)], axis=0)
                for k, m, t in zip(keys, mine, theirs)}
        return {n: jnp.stack([full[(n, l)] for l in range(L)]) for n, L, _, _ in BIG}


def local_step(x, target, mods, comm, small):
    S, D = x.shape
    row = lambda v: v.reshape(1, -1)
    saved = []
    tiles = [att_bias_tiles(small["att_rel_bias"][j], f"att_tiles_{j}").reshape(ATT_HEADS, ATT_CLASSES, ATT_TQ, ATT_KW)
             for j in range(2)]
    wgk_p = [jnp.pad(small["gla_w_gk2"][j], ((0, 128 - GLA_RANK), (0, 0))).astype(BF16) for j in range(2)]

    u1 = modulate(x, row(mods[0, 1]), row(mods[0, 0]), "mod_first")
    Ws = [dict() for _ in range(DEPTH)]
    items0 = layer_weights(0)
    Ws[0].update(comm.gather_now(items0[:1], "gather_w0"))
    for i in range(DEPTH):
        j = i // 2
        W = Ws[i]
        sh1, sc1, g1, sh2, sc2, g2 = (row(mods[i, k]) for k in range(6))
        nxt = min(i + 1, DEPTH - 1)
        more = i + 1 < DEPTH
        nxt_items = layer_weights(nxt)
        in_items = list(items0[1:3]) if i == 0 else []
        mix_items = (list(items0[3:]) if i == 0 else []) + (list(nxt_items[:2]) if more else [])
        up_items = list(nxt_items[2:3]) if more else []
        down_items = list(nxt_items[3:]) if more else []

        def hosted(items):
            return comm.gather(items) if items else None

        def landed_weights(items, landed):
            for k, it in enumerate(items):
                layer = 0 if it in items0 and i == 0 else nxt
                Ws[layer].update(comm.full_weights([it], landed[k:k + 1]))

        side = hosted(in_items)
        if i % 2 == 0:
            proj = mm_plain(u1, *W["in"], f"gla_in_{i}", side=side)
        else:
            proj = mm_plain(u1, *W["in"], f"att_in_{i}", mode="bf16", bias=row(small["att_b_in"][j]), side=side)
        proj, landed = proj if side is not None else (proj, [])
        landed_weights(in_items, landed)
        if i % 2 == 0:
            (zmix, states), landed = gla_fwd(proj, wgk_p[j], row(small["gla_b_gk"][j]), row(small["gla_g_norm"][j]),
                                             f"gla_fwd_{i}", hosted(mix_items))
        else:
            (zmix,), landed = attn_fwd(proj, tiles[j], f"att_fwd_{i}", hosted(mix_items))
            states = None
        landed_weights(mix_items, landed)
        (y1, x_mid, u2), _ = mm_down_ln(zmix, *W["out"], x, 1.0 + g1, row(small["ln_g"][i, 0]),
                                        row(small["ln_b"][i, 0]), sc2, sh2, f"mix_out_{i}")
        side = hosted(up_items)
        act = mm_plain(u2, *W["w1"], f"ff_up_{i}", mode="mlp_up", side=side)
        act, landed = act if side is not None else (act, [])
        landed_weights(up_items, landed)
        (y2, x_out, u_next), landed = mm_down_ln(act, *W["w2"], x_mid, 1.0 + g2, row(small["ln_g"][i, 1]),
                                                 row(small["ln_b"][i, 1]), row(mods[nxt, 1]), row(mods[nxt, 0]),
                                                 f"ff_out_{i}", side=hosted(down_items))
        landed_weights(down_items, landed)
        saved.append(dict(x_in=x, u1=u1, proj=proj, zmix=zmix, states=states, y1=y1, x_mid=x_mid, u2=u2,
                          act=act, y2=y2))
        x, u1 = x_out, u_next

    g_small = dict(ln_g=[None] * DEPTH, ln_b=[None] * DEPTH, gla_w_gk2=[None] * 2, gla_b_gk=[None] * 2,
                   gla_g_norm=[None] * 2, att_b_in=[None] * 2, att_rel_bias=[None] * 2)
    dmods = [None] * DEPTH
    later = []
    top = saved[DEPTH - 1]
    dz2, dy2, s_ln2, sq = loss_ln_bwd(x, target, top["x_mid"], top["y2"], 1.0 + row(mods[DEPTH - 1, 5]),
                                      row(small["ln_g"][DEPTH - 1, 1]), "loss_ln_bwd")

    for i in reversed(range(DEPTH)):
        j = i // 2
        sv = saved[i]
        W = Ws[i]
        sh1, sc1, g1, sh2, sc2, g2 = (row(mods[i, k]) for k in range(6))
        dh = mm_plain(dy2, *W["w2"], f"ff_dn_{i}", mode="mlp_dn", nt=True, h=sv["act"])
        g_w2 = mm_w_res(sv["act"], dy2, f"ff_w2g_{i}").reshape(N_CHIPS, D_FF // N_CHIPS, D)
        g_w1 = mm_w_res(sv["u2"], dh, f"ff_w1g_{i}", chips_out=True)
        items = [("ff_w1", i, g_w1), ("ff_w2", i, g_w2)] + later
        (dz1, dy1, s_m2, s_ln1), swapped = mm_down_comb(
            dh, *W["w1"], dz2, sv["x_mid"], 1.0 + sc2, f"ff_dx_{i}",
            ln=(sv["x_in"], sv["y1"], 1.0 + g1, row(small["ln_g"][i, 0])), side=comm.swap(items))
        pending = comm.reduce_begin(i, items, swapped)
        side = comm.exchange(pending)
        mixer = "gla" if i % 2 == 0 else "att"
        below = None
        if i > 0:
            below = (saved[i - 1]["x_mid"], saved[i - 1]["y2"], 1.0 + row(mods[i - 1, 5]), row(small["ln_g"][i - 1, 1]))
        if i % 2 == 0:
            g_out = mm_w(sv["zmix"], dy1, f"gla_wog_{i}").reshape(N_CHIPS, D // N_CHIPS, D)
            dzg = mm_plain(dy1, *W["out"], f"gla_dz_{i}", nt=True)
            (dproj, dwgk, dbgk, dgn), landed = gla_bwd(sv["proj"], sv["states"], dzg, wgk_p[j],
                                                       row(small["gla_b_gk"][j]), row(small["gla_g_norm"][j]),
                                                       f"gla_bwd_{i}", side)
            g_small["gla_w_gk2"][j] = dwgk[:GLA_RANK]
            g_small["gla_b_gk"][j] = dbgk[0]
            g_small["gla_g_norm"][j] = dgn[0].reshape(GLA_HEADS, GLA_DV_HEAD)
            gwi = mm_w_res(sv["u1"], dproj, f"gla_wig_{i}")[:, :GLA_IN]
            g_in = gwi.reshape(D, N_CHIPS, GLA_IN // N_CHIPS).transpose(1, 0, 2)
            outs, _ = mm_down_comb(dproj, *W["in"], dz1, sv["x_in"], 1.0 + sc1, f"mix_dx_{i}", ln=below)
        else:
            g_out = mm_w(sv["zmix"], dy1, f"att_wog_{i}").reshape(N_CHIPS, D // N_CHIPS, D)
            do = mm_plain(dy1, *W["out"], f"att_do_{i}", mode="bf16", nt=True)
            (dqkv, dbt), landed = attn_bwd(sv["proj"], tiles[j], do, f"att_bwd_{i}", side)
            g_small["att_rel_bias"][j] = att_bias_grad(dbt.reshape(ATT_HEADS * ATT_CLASSES, ATT_TQ, ATT_KW),
                                                       f"att_bias_{i}")
            g_in = mm_w_chips3(sv["u1"], dqkv, f"att_wig_{i}")
            outs, _ = mm_down_comb(dqkv, *W["in"], dz1, sv["x_in"], 1.0 + sc1, f"mix_dx_{i}", parts=3, ln=below)
        s_m1 = outs[1] if below is None else outs[2]
        if i % 2 == 1:
            g_small["att_b_in"][j] = s_m1[2:5].reshape(3 * D)
        comm.reduce_mid(pending, landed)
        later = [(mixer + "_w_in", j, g_in), (mixer + "_w_out", j, g_out)]
        g_small["ln_g"][i] = jnp.stack([s_ln1[0], s_ln2[0]])
        g_small["ln_b"][i] = jnp.stack([s_ln1[1], s_ln2[1]])
        dmods[i] = jnp.stack([s_m1[1], s_m1[0], s_ln1[2], s_m2[1], s_m2[0], s_ln2[2]])
        if below is None:
            dx = outs[0]
        else:
            dz2, dy2, s_ln2 = outs[0], outs[1], outs[3]
    comm.reduce_tail("last", later)

    g_small = {n: jnp.stack(v) for n, v in g_small.items()}
    return sq, dx, jnp.stack(dmods), g_small


SMALL_SHARDED = (("ln_g", (4, 2, 256)), ("ln_b", (4, 2, 256)), ("gla_g_norm", (2, 4, 64)),
                 ("gla_w_gk2", (2, 16, 128)), ("att_b_in", (2, 768)))
SMALL_FULL = dict(ln_g=(4, 2, 1024), ln_b=(4, 2, 1024), gla_g_norm=(2, 4, 256), gla_w_gk2=(2, 16, 512),
                  att_b_in=(2, 3072), gla_b_gk=(2, 512), att_rel_bias=(2, 16, 257))
SMALL_GRAD_ORDER = ("ln_g", "ln_b", "gla_g_norm", "gla_w_gk2", "att_b_in", "gla_b_gk", "att_rel_bias")


def _pack_small(arrs, rows_total):
    parts = []
    for a in arrs:
        flat = a.reshape(-1)
        pad = (-flat.shape[0]) % PACK_W
        parts.append(jnp.pad(flat, (0, pad)).reshape(-1, PACK_W))
    buf = jnp.concatenate(parts, axis=0)
    return jnp.pad(buf, ((0, rows_total - buf.shape[0]), (0, 0)))


def _unpack_small(buf, shapes):
    out, r = [], 0
    for shp in shapes:
        n = 1
        for s in shp:
            n *= s
        nr = (n + PACK_W - 1) // PACK_W
        out.append(buf[..., r:r + nr, :].reshape(buf.shape[:-2] + (nr * PACK_W,))[..., :n].reshape(buf.shape[:-2] + shp))
        r += nr
    return out


def _unshard_last(g4):
    nd = g4.ndim
    perm = tuple(range(1, nd - 1)) + (0, nd - 1)
    t = g4.transpose(perm)
    return t.reshape(t.shape[:-2] + (-1,))


def _shard_last(full, s):
    n = full.shape[-1] // N_CHIPS
    return lax.dynamic_slice_in_dim(full, s * n, n, axis=full.ndim - 1)


WEIGHT_NAMES = ("w_ada", "b_ada", "ln_g", "ln_b", "gla_w_in", "gla_w_gk2", "gla_b_gk", "gla_g_norm", "gla_w_out",
                "att_w_in", "att_b_in", "att_rel_bias", "att_w_out", "ff_w1", "ff_w2")


def kernel(x, c, w_ada, b_ada, ln_g, ln_b, gla_w_in, gla_w_gk2, gla_b_gk, gla_g_norm, gla_w_out, att_w_in, att_b_in, att_rel_bias, att_w_out, ff_w1, ff_w2, loss_target, m_w_ada, m_b_ada, m_ln_g, m_ln_b, m_gla_w_in, m_gla_w_gk2, m_gla_b_gk, m_gla_g_norm, m_gla_w_out, m_att_w_in, m_att_b_in, m_att_rel_bias, m_att_w_out, m_ff_w1, m_ff_w2, v_w_ada, v_b_ada, v_ln_g, v_ln_b, v_gla_w_in, v_gla_w_gk2, v_gla_b_gk, v_gla_g_norm, v_gla_w_out, v_att_w_in, v_att_b_in, v_att_rel_bias, v_att_w_out, v_ff_w1, v_ff_w2):
    weights = dict(w_ada=w_ada, b_ada=b_ada, ln_g=ln_g, ln_b=ln_b, gla_w_in=gla_w_in, gla_w_gk2=gla_w_gk2,
                   gla_b_gk=gla_b_gk, gla_g_norm=gla_g_norm, gla_w_out=gla_w_out, att_w_in=att_w_in,
                   att_b_in=att_b_in, att_rel_bias=att_rel_bias, att_w_out=att_w_out, ff_w1=ff_w1, ff_w2=ff_w2)
    mom1 = dict(w_ada=m_w_ada, b_ada=m_b_ada, ln_g=m_ln_g, ln_b=m_ln_b, gla_w_in=m_gla_w_in, gla_w_gk2=m_gla_w_gk2,
                gla_b_gk=m_gla_b_gk, gla_g_norm=m_gla_g_norm, gla_w_out=m_gla_w_out, att_w_in=m_att_w_in,
                att_b_in=m_att_b_in, att_rel_bias=m_att_rel_bias, att_w_out=m_att_w_out, ff_w1=m_ff_w1, ff_w2=m_ff_w2)
    mom2 = dict(w_ada=v_w_ada, b_ada=v_b_ada, ln_g=v_ln_g, ln_b=v_ln_b, gla_w_in=v_gla_w_in, gla_w_gk2=v_gla_w_gk2,
                gla_b_gk=v_gla_b_gk, gla_g_norm=v_gla_g_norm, gla_w_out=v_gla_w_out, att_w_in=v_att_w_in,
                att_b_in=v_att_b_in, att_rel_bias=v_att_rel_bias, att_w_out=v_att_w_out, ff_w1=v_ff_w1, ff_w2=v_ff_w2)

    ax, ay, ac = lax.axis_index("x"), lax.axis_index("y"), lax.axis_index("c")
    chip = 2 * ax + ay
    dev = 2 * chip + ac
    S = x.shape[1]
    x2 = x.reshape(S, D_MODEL)
    t2 = loss_target.reshape(S, D_MODEL)

    comm = Comm(weights, ac, chip)

    small_rows = 16
    spack = _pack_small([c] + [weights[n] for n, _ in SMALL_SHARDED], small_rows)
    sg = all_gather8(spack, "gather_small").reshape(N_DEV, small_rows, PACK_W)
    parts = _unpack_small(sg, [(1, D_MODEL)] + [shp for _, shp in SMALL_SHARDED])
    c_all = parts[0].reshape(N_DEV, D_MODEL)
    small = {n: _unshard_last(p[0::2]) for (n, _), p in zip(SMALL_SHARDED, parts[1:])}
    small["gla_b_gk"] = gla_b_gk
    small["att_rel_bias"] = att_rel_bias

    c_act = silu_rows(jnp.pad(c_all, ((0, 128 - N_DEV), (0, 0))), "silu_c")
    mods_part = jnp.stack([mm_plain(c_act, w_ada, l, f"ada_fwd_{l}", tm=128)[:N_DEV] for l in range(DEPTH)], axis=1)
    mods_part = mods_part.reshape(N_DEV, DEPTH * 6 * D_MODEL // N_CHIPS)
    mg = all_gather8(mods_part, "gather_mods").reshape(N_CHIPS, 2, N_DEV, DEPTH, 6 * D_MODEL // N_CHIPS)
    mods_mine = lax.dynamic_index_in_dim(mg[:, 0], dev, axis=1, keepdims=False)
    mods = mods_mine.transpose(1, 0, 2).reshape(DEPTH, 6 * D_MODEL) + b_ada
    mods = mods.reshape(DEPTH, 6, D_MODEL)

    sq, grad_x, dmods, g_small = local_step(x2, t2, mods, comm, small)
    loss = lax.psum(0.5 * sq[0, 0] / D_MODEL, ("x", "y", "c"))

    g_shard = comm.reduce_end()

    dm_flat = dmods.reshape(DEPTH, 6 * D_MODEL)
    g_rows = 80
    gpack = _pack_small([dm_flat] + [g_small[n] for n in SMALL_GRAD_ORDER], g_rows)
    gg = all_gather8(gpack, "gather_small_grads").reshape(N_DEV, g_rows, PACK_W)
    gsum = sum_over_devices(gg, "sum_small_grads")
    sums = _unpack_small(gsum, [(DEPTH, 6 * D_MODEL)] + [SMALL_FULL[n] for n in SMALL_GRAD_ORDER])
    grads = dict(b_ada=sums[0])
    for n, full_g in zip(SMALL_GRAD_ORDER, sums[1:]):
        grads[n] = full_g if n in ("gla_b_gk", "att_rel_bias") else _shard_last(full_g, chip)
    dm_all = _unpack_small(gg, [(DEPTH, 6 * D_MODEL)])[0]
    dm_cols = _shard_last(dm_all, chip).reshape(N_DEV, DEPTH * 6 * D_MODEL // N_CHIPS)
    dm_cols = jnp.pad(dm_cols, ((0, 128 - N_DEV), (0, 0))).astype(BF16)
    gwa = mm_w(c_act, dm_cols, "ada_bwd", ts=128)
    grads["w_ada"] = gwa.reshape(D_MODEL, DEPTH, 6 * D_MODEL // N_CHIPS).transpose(1, 0, 2)
    grads.update(g_shard)

    deltas, new_m, new_v = {}, {}, {}
    for n in WEIGHT_NAMES:
        deltas[n], new_m[n], new_v[n] = adamw(weights[n], grads[n], mom1[n], mom2[n], "adamw_" + n)

    return (loss, grad_x.reshape(1, S, D_MODEL), *[grads[n] for n in WEIGHT_NAMES], *[deltas[n] for n in WEIGHT_NAMES],
            *[new_m[n] for n in WEIGHT_NAMES], *[new_v[n] for n in WEIGHT_NAMES])
```

```python
import functools

import jax
import jax.numpy as jnp
from jax import lax
from jax.experimental import pallas as pl
from jax.experimental.pallas import tpu as pltpu

F32 = jnp.float32
BF16 = jnp.bfloat16
HIGHEST = lax.Precision.HIGHEST
MESH = pl.DeviceIdType.MESH

D_MODEL = 1024
DEPTH = 4
CHUNK = 64
GLA_HEADS = 4
GLA_DK = 512
GLA_DV = 1024
GLA_DK_HEAD = 128
GLA_DV_HEAD = 256
GLA_RANK = 16
GLA_IN = 3088
GLA_IN_PAD = 3200
GLA_LR_OFF = 3072
ATT_HEADS = 16
ATT_HD = 64
LEFT_CHUNKS = 8
MAX_REL = 128
N_REL = 257
D_FF = 4096
ALPHA = (2.0 * DEPTH) ** 0.25
LN_EPS = 1e-5
RMS_EPS = 1e-6
NEG_INF = -1e30
GLA_SCALE = GLA_DK_HEAD ** -0.5
ATT_SCALE = ATT_HD ** -0.5
ADAM_LR = 0.001
ADAM_B1 = 0.9
ADAM_B2 = 0.999
ADAM_EPS = 1e-08
ADAM_WD = 0.01
ADAM_STEP = 10

ATT_TQ = 256
ATT_KW = 768
GLA_TB = 256
GLA_GROUP = 2
VMEM_LIMIT = 56 * 1024 * 1024
WHOLE_WEIGHT_BYTES = 8 * 1024 * 1024
N_CHIPS = 4
N_DEV = 8
PACK_W = 1024


def _dot(a, b):
    return jnp.dot(a, b, preferred_element_type=F32)


def _dot_nt(a, b):
    return lax.dot_general(a, b, (((1,), (1,)), ((), ())), preferred_element_type=F32)


def _dot_tn(a, b):
    return lax.dot_general(a, b, (((0,), (0,)), ((), ())), preferred_element_type=F32)


def _cp(sem, vmem=VMEM_LIMIT):
    return pltpu.CompilerParams(dimension_semantics=sem, vmem_limit_bytes=vmem)


def _row_spec(n):
    return pl.BlockSpec((1, n), lambda *_: (0, 0))


def _sigmoid(x):
    return 1.0 / (1.0 + jnp.exp(-x))


def _log_sigmoid(x):
    return jnp.minimum(x, 0.0) - jnp.log1p(jnp.exp(-jnp.abs(x)))


class Side:
    def __init__(self, ins, out_shapes, n_sems, n_local, start, wait):
        self.ins, self.out_shapes, self.n_sems, self.n_local = list(ins), list(out_shapes), n_sems, n_local
        self.start, self.wait = start, wait

    def sem_shapes(self):
        return [pltpu.SemaphoreType.DMA((self.n_sems,)), pltpu.SemaphoreType.DMA((self.n_sems,)),
                pltpu.SemaphoreType.DMA((max(self.n_local, 1),))]


def run_side(side, name):
    n_in = len(side.ins)
    n_out = len(side.out_shapes)

    def body(*refs):
        ins, outs, sems = refs[:n_in], refs[n_in:n_in + n_out], refs[n_in + n_out:]
        side.start(ins, outs, *sems)
        side.wait(ins, outs, *sems)

    any_spec = pl.BlockSpec(memory_space=pl.ANY)
    return pl.pallas_call(body, name=name, out_shape=side.out_shapes, in_specs=[any_spec] * n_in,
                          out_specs=[any_spec] * n_out, scratch_shapes=side.sem_shapes())(*side.ins)


def hosted_call(main, side, *, name, grid, in_specs, out_specs, out_shape, scratch_shapes, dims, args):
    if side is None:
        outs = pl.pallas_call(main, name=name, grid=grid, in_specs=in_specs, out_specs=out_specs,
                              out_shape=out_shape, scratch_shapes=scratch_shapes, compiler_params=_cp(dims))(*args)
        return list(outs), []
    n_mi, n_mo, n_ms = len(in_specs), len(out_specs), len(scratch_shapes)
    n_si, n_so = len(side.ins), len(side.out_shapes)

    def kern(*refs):
        mi, si = refs[:n_mi], refs[n_mi:n_mi + n_si]
        o0 = n_mi + n_si
        mo, so = refs[o0:o0 + n_mo], refs[o0 + n_mo:o0 + n_mo + n_so]
        s0 = o0 + n_mo + n_so
        ms, sems = refs[s0:s0 + n_ms], refs[s0 + n_ms:]
        ids = [pl.program_id(d) for d in range(len(grid))]
        first = functools.reduce(jnp.logical_and, [i == 0 for i in ids])
        last = functools.reduce(jnp.logical_and, [i == g - 1 for i, g in zip(ids, grid)])

        @pl.when(first)
        def _():
            side.start(si, so, *sems)
        main(*mi, *mo, *ms)

        @pl.when(last)
        def _():
            side.wait(si, so, *sems)

    any_spec = pl.BlockSpec(memory_space=pl.ANY)
    outs = pl.pallas_call(
        kern, name=name, grid=grid, in_specs=list(in_specs) + [any_spec] * n_si,
        out_specs=list(out_specs) + [any_spec] * n_so, out_shape=list(out_shape) + side.out_shapes,
        scratch_shapes=list(scratch_shapes) + side.sem_shapes(),
        compiler_params=_cp(("arbitrary",) * len(grid)))(*args, *side.ins)
    return list(outs[:n_mo]), list(outs[n_mo:])


def modulate(x, sc, sh, name):
    S, D = x.shape
    tm = min(512, S)

    def kern(x_ref, sc_ref, sh_ref, u_ref):
        u_ref[...] = (x_ref[...] * (1.0 + sc_ref[...]) + sh_ref[...]).astype(BF16)

    return pl.pallas_call(
        kern, name=name, grid=(S // tm,),
        in_specs=[pl.BlockSpec((tm, D), lambda i: (i, 0)), _row_spec(D), _row_spec(D)],
        out_specs=pl.BlockSpec((tm, D), lambda i: (i, 0)),
        out_shape=jax.ShapeDtypeStruct((S, D), BF16),
        compiler_params=_cp(("parallel",)),
    )(x, sc, sh)


def silu_rows(c_all, name):
    def kern(c_ref, o_ref):
        c = c_ref[...]
        o_ref[...] = (c * _sigmoid(c)).astype(BF16)

    return pl.pallas_call(kern, name=name, out_shape=jax.ShapeDtypeStruct(c_all.shape, BF16))(c_all)


def sum_over_devices(g, name):
    n, R, C = g.shape

    def kern(g_ref, o_ref):
        acc = g_ref[0]
        for d in range(1, n):
            acc = acc + g_ref[d]
        o_ref[...] = acc

    return pl.pallas_call(kern, name=name, out_shape=jax.ShapeDtypeStruct((R, C), F32))(g)


def _rows_block(R, C, budget=1 << 20):
    if R * C * 4 <= budget or R % 8:
        return R
    tr = max(8, (budget // (C * 4)) // 8 * 8)
    while R % tr:
        tr -= 8
    return tr


def adamw(w, g, m, v, name):
    shape = w.shape
    C = shape[-1]
    R = w.size // C
    w2, g2, m2, v2 = (t.reshape(R, C) for t in (w, g, m, v))
    tr = _rows_block(R, C)
    c1 = 1.0 - ADAM_B1 ** ADAM_STEP
    c2 = 1.0 - ADAM_B2 ** ADAM_STEP

    def kern(w_ref, g_ref, m_ref, v_ref, d_ref, nm_ref, nv_ref):
        gg = g_ref[...]
        nm = ADAM_B1 * m_ref[...] + (1.0 - ADAM_B1) * gg
        nv = ADAM_B2 * v_ref[...] + (1.0 - ADAM_B2) * (gg * gg)
        m_hat = nm / c1
        v_hat = nv / c2
        d_ref[...] = -ADAM_LR * (m_hat / (jnp.sqrt(v_hat) + ADAM_EPS) + ADAM_WD * w_ref[...])
        nm_ref[...] = nm
        nv_ref[...] = nv

    spec = pl.BlockSpec((tr, C), lambda i: (i, 0))
    outs = pl.pallas_call(
        kern, name=name, grid=(R // tr,),
        in_specs=[spec] * 4, out_specs=[spec] * 3,
        out_shape=[jax.ShapeDtypeStruct((R, C), F32)] * 3,
        compiler_params=_cp(("parallel",)),
    )(w2, g2, m2, v2)
    return tuple(o.reshape(shape) for o in outs)


def _tn_for(N):
    for tn in (1024, 768, 640, 512, 384, 256, 128):
        if N % tn == 0:
            return tn
    return N


def mm_plain(a, b3, layer, name, *, mode="f32", nt=False, bias=None, h=None, tm=1024, side=None):
    M, K = a.shape
    N = b3.shape[1] if nt else b3.shape[2]
    if K * N * 2 <= WHOLE_WEIGHT_BYTES:
        tn, tm = N, min(tm, 512 if N > D_MODEL else 1024)
    else:
        tn = _tn_for(N)
    tm = min(tm, M)
    a_spec = pl.BlockSpec((tm, K), lambda j, i: (i, 0))
    if nt:
        b_spec = pl.BlockSpec((None, tn, K), lambda j, i: (layer, j, 0))
    else:
        b_spec = pl.BlockSpec((None, K, tn), lambda j, i: (layer, 0, j))
    o_spec = pl.BlockSpec((tm, tn), lambda j, i: (i, j))
    ins, in_specs = [a, b3], [a_spec, b_spec]
    if bias is not None:
        ins.append(bias)
        in_specs.append(pl.BlockSpec((1, tn), lambda j, i: (0, j)))
    if mode == "mlp_dn":
        ins.append(h)
        in_specs.append(o_spec)
    elif mode not in ("f32", "bf16", "mlp_up"):
        raise ValueError(mode)
    odt = F32 if mode == "f32" else BF16

    def kern(a_ref, b_ref, *rest):
        rest = list(rest)
        bias_ref = rest.pop(0) if bias is not None else None
        h_ref = rest.pop(0) if mode == "mlp_dn" else None
        o_ref = rest.pop(0)
        if nt:
            bt_ref = rest.pop(0)

            @pl.when(pl.program_id(1) == 0)
            def _():
                bt_ref[...] = b_ref[...].T
            acc = _dot(a_ref[...], bt_ref[...])
        else:
            acc = _dot(a_ref[...], b_ref[...].astype(BF16))
        if bias_ref is not None:
            acc = acc + bias_ref[...]
        if mode == "mlp_up":
            r = jnp.maximum(acc, 0.0)
            acc = r * r
        elif mode == "mlp_dn":
            acc = acc * (2.0 * jnp.sqrt(h_ref[...].astype(F32)))
        o_ref[...] = acc.astype(odt)

    outs, landed = hosted_call(
        kern, side, name=name, grid=(N // tn, M // tm), in_specs=in_specs, out_specs=[o_spec],
        out_shape=[jax.ShapeDtypeStruct((M, N), odt)],
        scratch_shapes=[pltpu.VMEM((K, tn), BF16)] if nt else [], dims=("parallel", "arbitrary"), args=tuple(ins))
    return outs[0] if side is None else (outs[0], landed)


def mm_down_ln(a, b3, layer, x_in, gate1p, ln_g, ln_b, sc_next, sh_next, name, *, side=None, tm=512):
    M, K = a.shape
    D = b3.shape[2]
    tm = min(tm, M)

    def kern(a_ref, b_ref, x_ref, gp_ref, lg_ref, lb_ref, sc_ref, sh_ref, y_ref, xo_ref, u_ref):
        y = _dot(a_ref[...], b_ref[...])
        y_ref[...] = y.astype(BF16)
        z = ALPHA * x_ref[...] + gp_ref[...] * y
        mu = jnp.mean(z, axis=-1, keepdims=True)
        zc = z - mu
        var = jnp.mean(zc * zc, axis=-1, keepdims=True)
        xo = (zc * lax.rsqrt(var + LN_EPS)) * lg_ref[...] + lb_ref[...]
        xo_ref[...] = xo
        u_ref[...] = (xo * (1.0 + sc_ref[...]) + sh_ref[...]).astype(BF16)

    tile = pl.BlockSpec((tm, D), lambda i: (i, 0))
    outs, landed = hosted_call(
        kern, side, name=name, grid=(M // tm,),
        in_specs=[pl.BlockSpec((tm, K), lambda i: (i, 0)), pl.BlockSpec((None, K, D), lambda i: (layer, 0, 0)), tile]
        + [_row_spec(D)] * 5,
        out_specs=[tile, tile, tile],
        out_shape=[jax.ShapeDtypeStruct((M, D), BF16), jax.ShapeDtypeStruct((M, D), F32),
                   jax.ShapeDtypeStruct((M, D), BF16)],
        scratch_shapes=[], dims=("parallel",), args=(a, b3, x_in, gate1p, ln_g, ln_b, sc_next, sh_next))
    return tuple(outs), landed


def mm_down_comb(a, b3, layer, dz, x_in, sc1p, name, *, parts=1, ln=None, side=None, tm=512):
    D, K = b3.shape[1], b3.shape[2]
    M = a.shape[-2]
    kp = K // parts
    tm = min(tm, M)
    n_ln = 0 if ln is None else 4

    def kern(*refs):
        a_refs = refs[:parts]
        b_ref, dz_ref, x_ref, sp_ref = refs[parts:parts + 4]
        ln_refs = refs[parts + 4:parts + 4 + n_ln]
        outs = refs[parts + 4 + n_ln:]

        @pl.when(pl.program_id(0) == 0)
        def _():
            for o in outs:
                if o.shape[0] == 8:
                    o[...] = jnp.zeros_like(o)
        if parts == 1:
            du = _dot_nt(a_refs[0][...], b_ref[...])
        else:
            du = _dot_nt(a_refs[0][...], b_ref[:, 0:kp])
            for p in range(1, parts):
                du = du + _dot_nt(a_refs[p][...], b_ref[:, p * kp:(p + 1) * kp])
        dx = ALPHA * dz_ref[...] + du * sp_ref[...]
        if ln is None:
            dx_ref, s_ref = outs
            dx_ref[...] = dx
        else:
            dzl_ref, dyl_ref, s_ref, sl_ref = outs
            _ln_bwd_tile(dx, *ln_refs, dzl_ref, dyl_ref, sl_ref)
        s_ref[0:1, :] += jnp.sum(du * x_ref[...], axis=0, keepdims=True)
        s_ref[1:2, :] += jnp.sum(du, axis=0, keepdims=True)
        if parts > 1:
            for p in range(parts):
                s_ref[2 + p:3 + p, :] += jnp.sum(a_refs[p][...].astype(F32), axis=0, keepdims=True)

    tile = pl.BlockSpec((tm, D), lambda i: (i, 0))
    sums = pl.BlockSpec((8, D), lambda i: (0, 0))
    if parts == 1:
        a_ins, a_specs = [a], [pl.BlockSpec((tm, K), lambda i: (i, 0))]
    else:
        assert kp == D and parts <= 6
        a_ins = [a] * parts
        a_specs = [pl.BlockSpec((None, tm, kp), functools.partial(lambda i, p: (p, i, 0), p=p)) for p in range(parts)]
    in_specs = a_specs + [pl.BlockSpec((None, D, K), lambda i: (layer, 0, 0)), tile, tile, _row_spec(D)]
    args = a_ins + [b3, dz, x_in, sc1p]
    if ln is None:
        out_specs = [tile, sums]
        out_shape = [jax.ShapeDtypeStruct((M, D), F32), jax.ShapeDtypeStruct((8, D), F32)]
    else:
        in_specs += [tile, tile, _row_spec(D), _row_spec(D)]
        args += list(ln)
        out_specs = [tile, tile, sums, sums]
        out_shape = [jax.ShapeDtypeStruct((M, D), F32), jax.ShapeDtypeStruct((M, D), BF16),
                     jax.ShapeDtypeStruct((8, D), F32), jax.ShapeDtypeStruct((8, D), F32)]
    return hosted_call(kern, side, name=name, grid=(M // tm,), in_specs=in_specs, out_specs=out_specs,
                       out_shape=out_shape, scratch_shapes=[], dims=("arbitrary",), args=tuple(args))


def mm_w(a, b, name, *, ts=2048, tk=512, chips_out=False, b_parts=1, tn=None):
    S, K = a.shape
    npart = b.shape[-1]
    N = npart * b_parts
    ts = min(ts, S)
    tk = min(tk, K)
    n_chip = N // N_CHIPS
    if tn is None:
        tn = _tn_for(n_chip if chips_out else npart)
    assert npart % tn == 0 and (not chips_out or n_chip % tn == 0)

    def kern(a_ref, b_ref, o_ref):
        @pl.when(pl.program_id(2) == 0)
        def _():
            o_ref[...] = jnp.zeros_like(o_ref)
        o_ref[...] += _dot_tn(a_ref[...], b_ref[...])

    if b_parts == 1:
        b_spec = pl.BlockSpec((ts, tn), lambda k, n, s: (s, n))
    else:
        per = npart // tn
        b_spec = pl.BlockSpec((None, ts, tn), lambda k, n, s: (n // per, s, n % per))
    if chips_out:
        per_chip = n_chip // tn
        o_spec = pl.BlockSpec((None, tk, tn), lambda k, n, s: (n // per_chip, k, n % per_chip))
        out_shape = jax.ShapeDtypeStruct((N_CHIPS, K, n_chip), F32)
    else:
        o_spec = pl.BlockSpec((tk, tn), lambda k, n, s: (k, n))
        out_shape = jax.ShapeDtypeStruct((K, N), F32)
    return pl.pallas_call(
        kern, name=name, grid=(K // tk, N // tn, S // ts),
        in_specs=[pl.BlockSpec((ts, tk), lambda k, n, s: (s, k)), b_spec],
        out_specs=o_spec, out_shape=out_shape,
        compiler_params=_cp(("parallel", "parallel", "arbitrary")),
    )(a, b)


def mm_w_chips3(a, b3, name, *, ts=512):
    S, K = a.shape
    P = b3.shape[2]
    n_chip = 3 * P // N_CHIPS
    ts = min(ts, S)
    pieces = []
    for chip in range(N_CHIPS):
        lo, hi = chip * n_chip, (chip + 1) * n_chip
        while lo < hi:
            part = lo // P
            w = min(hi, (part + 1) * P) - lo
            pieces.append((chip, lo - chip * n_chip, part, lo - part * P, w))
            lo += w

    def kern(a_ref, b_ref, o_ref):
        @pl.when(pl.program_id(0) == 0)
        def _():
            o_ref[...] = jnp.zeros_like(o_ref)
        at = a_ref[...].T
        for chip, oc, part, pc, w in pieces:
            o_ref[chip, :, oc:oc + w] += _dot(at, b_ref[part, :, pc:pc + w])

    return pl.pallas_call(
        kern, name=name, grid=(S // ts,),
        in_specs=[pl.BlockSpec((ts, K), lambda s: (s, 0)), pl.BlockSpec((3, ts, P), lambda s: (0, s, 0))],
        out_specs=pl.BlockSpec((N_CHIPS, K, n_chip), lambda s: (0, 0, 0)),
        out_shape=jax.ShapeDtypeStruct((N_CHIPS, K, n_chip), F32),
        compiler_params=_cp(("arbitrary",)),
    )(a, b3)


def mm_w_res(a, b, name, *, chips_out=False, ts=512):
    S, K = a.shape
    N = b.shape[1]
    ts = min(ts, S)
    n_chip = N // N_CHIPS

    def kern(a_ref, b_ref, o_ref):
        @pl.when(pl.program_id(0) == 0)
        def _():
            o_ref[...] = jnp.zeros_like(o_ref)
        at = a_ref[...].T
        if chips_out:
            for chip in range(N_CHIPS):
                o_ref[chip] += _dot(at, b_ref[:, chip * n_chip:(chip + 1) * n_chip])
        else:
            o_ref[...] += _dot(at, b_ref[...])

    o_shape = (N_CHIPS, K, n_chip) if chips_out else (K, N)
    return pl.pallas_call(
        kern, name=name, grid=(S // ts,),
        in_specs=[pl.BlockSpec((ts, K), lambda s: (s, 0)), pl.BlockSpec((ts, N), lambda s: (s, 0))],
        out_specs=pl.BlockSpec(o_shape, lambda s: (0,) * len(o_shape)),
        out_shape=jax.ShapeDtypeStruct(o_shape, F32),
        compiler_params=_cp(("arbitrary",)),
    )(a, b)


def mm_f32(a, b, name):
    def kern(a_ref, b_ref, o_ref):
        o_ref[...] = jnp.dot(a_ref[...], b_ref[...], precision=HIGHEST, preferred_element_type=F32)

    return pl.pallas_call(kern, name=name, out_shape=jax.ShapeDtypeStruct((a.shape[0], b.shape[1]), F32),
                          compiler_params=pltpu.CompilerParams(vmem_limit_bytes=VMEM_LIMIT))(a, b)


def _ln_bwd_tile(dxo_t, x_ref, y_ref, gp_ref, lg_ref, dz_ref, dy_ref, s_ref):
    yv = y_ref[...].astype(F32)
    z = ALPHA * x_ref[...] + gp_ref[...] * yv
    mu = jnp.mean(z, axis=-1, keepdims=True)
    zc = z - mu
    var = jnp.mean(zc * zc, axis=-1, keepdims=True)
    rstd = lax.rsqrt(var + LN_EPS)
    xhat = zc * rstd
    dxh = dxo_t * lg_ref[...]
    dz = rstd * (dxh - jnp.mean(dxh, axis=-1, keepdims=True)
                 - xhat * jnp.mean(dxh * xhat, axis=-1, keepdims=True))
    dz_ref[...] = dz
    dy_ref[...] = (gp_ref[...] * dz).astype(BF16)
    s_ref[0:1, :] += jnp.sum(dxo_t * xhat, axis=0, keepdims=True)
    s_ref[1:2, :] += jnp.sum(dxo_t, axis=0, keepdims=True)
    s_ref[2:3, :] += jnp.sum(dz * yv, axis=0, keepdims=True)


def loss_ln_bwd(x_out, target, x_in, y, gate1p, ln_g, name, *, tm=256):
    S, D = x_out.shape
    tm = min(tm, S)

    def kern(xo_ref, t_ref, x_ref, y_ref, gp_ref, lg_ref, dz_ref, dy_ref, s_ref, l_ref):
        @pl.when(pl.program_id(0) == 0)
        def _():
            s_ref[...] = jnp.zeros_like(s_ref)
            l_ref[...] = jnp.zeros_like(l_ref)
        e = xo_ref[...] - t_ref[...]
        l_ref[...] += jnp.sum(e * e)
        _ln_bwd_tile(e * (1.0 / D), x_ref, y_ref, gp_ref, lg_ref, dz_ref, dy_ref, s_ref)

    tile = pl.BlockSpec((tm, D), lambda i: (i, 0))
    return pl.pallas_call(
        kern, name=name, grid=(S // tm,),
        in_specs=[tile, tile, tile, tile, _row_spec(D), _row_spec(D)],
        out_specs=[tile, tile, pl.BlockSpec((8, D), lambda i: (0, 0)), pl.BlockSpec((8, 128), lambda i: (0, 0))],
        out_shape=[jax.ShapeDtypeStruct((S, D), F32), jax.ShapeDtypeStruct((S, D), BF16),
                   jax.ShapeDtypeStruct((8, D), F32), jax.ShapeDtypeStruct((8, 128), F32)],
        compiler_params=_cp(("arbitrary",)),
    )(x_out, target, x_in, y, gate1p, ln_g)


def _tri64():
    r = lax.broadcasted_iota(jnp.int32, (CHUNK, CHUNK), 0)
    c = lax.broadcasted_iota(jnp.int32, (CHUNK, CHUNK), 1)
    return r >= c


def _gla_chunk_common(proj_ref, rows, b, h):
    kc = slice(h * GLA_DK_HEAD, (h + 1) * GLA_DK_HEAD)
    bh = b[:, kc]
    ep = jnp.exp(bh)
    en = jnp.exp(-bh)
    bl = bh[CHUNK - 1:CHUNK, :]
    ee = jnp.exp(bl - bh)
    dec = jnp.exp(bl)
    q = proj_ref[rows, h * GLA_DK_HEAD:(h + 1) * GLA_DK_HEAD] * GLA_SCALE
    k = proj_ref[rows, GLA_DK + h * GLA_DK_HEAD:GLA_DK + (h + 1) * GLA_DK_HEAD]
    v = proj_ref[rows, 2 * GLA_DK + h * GLA_DV_HEAD:2 * GLA_DK + (h + 1) * GLA_DV_HEAD]
    g = proj_ref[rows, 2 * GLA_DK + GLA_DV + h * GLA_DV_HEAD:2 * GLA_DK + GLA_DV + (h + 1) * GLA_DV_HEAD]
    return ep, en, ee, dec, q, k, v, g


def gla_fwd(proj, wgk_p, bgk, gnorm, name, side=None):
    S = proj.shape[0]
    TB = min(GLA_TB, S)
    ncb = TB // CHUNK

    def kern(proj_ref, wgk_ref, bgk_ref, gn_ref, zg_ref, st_ref, state_scr, la_scr):
        @pl.when(pl.program_id(0) == 0)
        def _():
            state_scr[...] = jnp.zeros_like(state_scr)
        lr = proj_ref[:, GLA_LR_OFF:GLA_IN_PAD].astype(BF16)
        gk = _dot(lr, wgk_ref[...]) + bgk_ref[...]
        la_scr[...] = _log_sigmoid(gk) * (1.0 / 16.0)
        lower = _tri64()
        tri = lower.astype(F32)

        def group(gi, carry):
            rows = [pl.ds(pl.multiple_of((gi * GLA_GROUP + g) * CHUNK, CHUNK), CHUNK) for g in range(GLA_GROUP)]
            b = [jnp.dot(tri, la_scr[r, :], precision=HIGHEST, preferred_element_type=F32) for r in rows]
            P = [(g, h) for g in range(GLA_GROUP) for h in range(GLA_HEADS)]
            cm = {p: _gla_chunk_common(proj_ref, rows[p[0]], b[p[0]], p[1]) for p in P}
            qf = {p: (cm[p][4] * cm[p][0]).astype(BF16) for p in P}
            kn = {p: (cm[p][5] * cm[p][1]).astype(BF16) for p in P}
            qn = {p: (cm[p][4] * cm[p][1]).astype(BF16) for p in P}
            kp = {p: (cm[p][5] * cm[p][0]).astype(BF16) for p in P}
            ke = {p: (cm[p][5] * cm[p][2]).astype(BF16) for p in P}
            vb = {p: cm[p][6].astype(BF16) for p in P}
            a_f = {p: _dot_nt(qf[p], kn[p]) for p in P}
            a_b = {p: _dot_nt(qn[p], kp[p]) for p in P}
            upd = {p: _dot_tn(vb[p], ke[p]) for p in P}
            st = {(0, h): state_scr[h] for h in range(GLA_HEADS)}
            for g in range(GLA_GROUP):
                for h in range(GLA_HEADS):
                    st[(g + 1, h)] = st[(g, h)] * cm[(g, h)][3] + upd[(g, h)]
            o_st = {p: _dot_nt(qf[p], st[p].astype(BF16)) for p in P}
            amat = {p: jnp.where(lower, a_f[p], a_b[p]).astype(BF16) for p in P}
            o = {p: _dot(amat[p], vb[p]) + o_st[p] for p in P}
            for g, h in P:
                st_ref[gi * GLA_GROUP + g, h] = st[(g, h)]
            for h in range(GLA_HEADS):
                state_scr[h] = st[(GLA_GROUP, h)]
            for g, h in P:
                gate = cm[(g, h)][7]
                vc = slice(h * GLA_DV_HEAD, (h + 1) * GLA_DV_HEAD)
                r = lax.rsqrt(jnp.mean(o[(g, h)] * o[(g, h)], axis=-1, keepdims=True) + RMS_EPS)
                on = (o[(g, h)] * r) * gn_ref[:, vc]
                zg_ref[rows[g], vc] = (on * (gate * _sigmoid(gate))).astype(BF16)
            return carry

        lax.fori_loop(0, ncb // GLA_GROUP, group, 0)

    return hosted_call(
        kern, side, name=name, grid=(S // TB,),
        in_specs=[pl.BlockSpec((TB, GLA_IN_PAD), lambda i: (i, 0)),
                  pl.BlockSpec((128, GLA_DK), lambda i: (0, 0)), _row_spec(GLA_DK), _row_spec(GLA_DV)],
        out_specs=[pl.BlockSpec((TB, GLA_DV), lambda i: (i, 0)),
                   pl.BlockSpec((ncb, GLA_HEADS, GLA_DV_HEAD, GLA_DK_HEAD), lambda i: (i, 0, 0, 0))],
        out_shape=[jax.ShapeDtypeStruct((S, GLA_DV), BF16),
                   jax.ShapeDtypeStruct((S // CHUNK, GLA_HEADS, GLA_DV_HEAD, GLA_DK_HEAD), F32)],
        scratch_shapes=[pltpu.VMEM((GLA_HEADS, GLA_DV_HEAD, GLA_DK_HEAD), F32), pltpu.VMEM((TB, GLA_DK), F32)],
        dims=("arbitrary",), args=(proj, wgk_p, bgk, gnorm))


def gla_bwd(proj, states, dzg, wgk_p, bgk, gnorm, name, side=None):
    S = proj.shape[0]
    TB = min(GLA_TB, S)
    ncb = TB // CHUNK
    nb = S // TB

    def kern(proj_ref, st_ref, dzg_ref, wgk_ref, bgk_ref, gn_ref,
             dproj_ref, dwgk_ref, dbgk_ref, dgn_ref, dstate_scr, la_scr, gk_scr, dgk_scr):
        @pl.when(pl.program_id(0) == 0)
        def _():
            dstate_scr[...] = jnp.zeros_like(dstate_scr)
            dwgk_ref[...] = jnp.zeros_like(dwgk_ref)
            dbgk_ref[...] = jnp.zeros_like(dbgk_ref)
            dgn_ref[...] = jnp.zeros_like(dgn_ref)
        lr = proj_ref[:, GLA_LR_OFF:GLA_IN_PAD].astype(BF16)
        gk = _dot(lr, wgk_ref[...]) + bgk_ref[...]
        gk_scr[...] = gk
        la_scr[...] = _log_sigmoid(gk) * (1.0 / 16.0)
        lower = _tri64()
        tri = lower.astype(F32)
        r_i = lax.broadcasted_iota(jnp.int32, (CHUNK, CHUNK), 0)
        c_i = lax.broadcasted_iota(jnp.int32, (CHUNK, CHUNK), 1)
        triu = (c_i >= r_i).astype(F32)
        last_row = lax.broadcasted_iota(jnp.int32, (CHUNK, GLA_DK_HEAD), 0) == CHUNK - 1

        def group(gi, carry):
            cs = [ncb - 1 - (gi * GLA_GROUP + g) for g in range(GLA_GROUP)]
            rows = [pl.ds(pl.multiple_of(c * CHUNK, CHUNK), CHUNK) for c in cs]
            b = [jnp.dot(tri, la_scr[r, :], precision=HIGHEST, preferred_element_type=F32) for r in rows]
            P = [(g, h) for g in range(GLA_GROUP) for h in range(GLA_HEADS)]
            kcs = [slice(h * GLA_DK_HEAD, (h + 1) * GLA_DK_HEAD) for h in range(GLA_HEADS)]
            vcs = [slice(h * GLA_DV_HEAD, (h + 1) * GLA_DV_HEAD) for h in range(GLA_HEADS)]
            cm = {p: _gla_chunk_common(proj_ref, rows[p[0]], b[p[0]], p[1]) for p in P}
            ep, en, ee, dec = ({p: cm[p][i] for p in P} for i in range(4))
            qf = {p: cm[p][4] * cm[p][0] for p in P}
            kn = {p: cm[p][5] * cm[p][1] for p in P}
            qn = {p: cm[p][4] * cm[p][1] for p in P}
            kp = {p: cm[p][5] * cm[p][0] for p in P}
            ke = {p: cm[p][5] * cm[p][2] for p in P}
            qf_b, kn_b, qn_b, kp_b, ke_b = ({p: t[p].astype(BF16) for p in P} for t in (qf, kn, qn, kp, ke))
            vb = {p: cm[p][6].astype(BF16) for p in P}
            st = {p: st_ref[cs[p[0]], p[1]] for p in P}
            st_b = {p: st[p].astype(BF16) for p in P}
            a_f = {p: _dot_nt(qf_b[p], kn_b[p]) for p in P}
            a_b = {p: _dot_nt(qn_b[p], kp_b[p]) for p in P}
            o_st = {p: _dot_nt(qf_b[p], st_b[p]) for p in P}
            amat = {p: jnp.where(lower, a_f[p], a_b[p]).astype(BF16) for p in P}
            o = {p: _dot(amat[p], vb[p]) + o_st[p] for p in P}
            do_b, dgs = {}, {}
            for p in P:
                g, h = p
                r = lax.rsqrt(jnp.mean(o[p] * o[p], axis=-1, keepdims=True) + RMS_EPS)
                oh = o[p] * r
                gn = gn_ref[:, vcs[h]]
                gate = cm[p][7]
                sg = _sigmoid(gate)
                dz = dzg_ref[rows[g], vcs[h]]
                don = dz * (gate * sg)
                dgs[p] = dz * (oh * gn) * (sg * (1.0 + gate * (1.0 - sg)))
                dgn_ref[:, vcs[h]] += jnp.sum(don * oh, axis=0, keepdims=True)
                doh = don * gn
                do_b[p] = (r * (doh - oh * jnp.mean(doh * oh, axis=-1, keepdims=True))).astype(BF16)
            da = {p: _dot_nt(do_b[p], vb[p]) for p in P}
            dv_a = {p: _dot_tn(amat[p], do_b[p]) for p in P}
            dqf_st = {p: _dot(do_b[p], st_b[p]) for p in P}
            dst_upd = {p: _dot_tn(do_b[p], qf_b[p]) for p in P}
            dst = {(0, h): dstate_scr[h] for h in range(GLA_HEADS)}
            for g in range(GLA_GROUP):
                for h in range(GLA_HEADS):
                    dst[(g + 1, h)] = dst[(g, h)] * dec[(g, h)] + dst_upd[(g, h)]
            for h in range(GLA_HEADS):
                dstate_scr[h] = dst[(GLA_GROUP, h)]
            dst_b = {p: dst[p].astype(BF16) for p in P}
            dv = {p: dv_a[p] + _dot_nt(ke_b[p], dst_b[p]) for p in P}
            dke = {p: _dot(vb[p], dst_b[p]) for p in P}
            da_f = {p: jnp.where(lower, da[p], 0.0).astype(BF16) for p in P}
            da_b = {p: jnp.where(lower, 0.0, da[p]).astype(BF16) for p in P}
            dqf = {p: _dot(da_f[p], kn_b[p]) + dqf_st[p] for p in P}
            dkn = {p: _dot_tn(da_f[p], qf_b[p]) for p in P}
            dqn = {p: _dot(da_b[p], kp_b[p]) for p in P}
            dkp = {p: _dot_tn(da_b[p], qn_b[p]) for p in P}
            dbs = {}
            for p in P:
                ddec = jnp.sum(dst[p] * st[p], axis=0, keepdims=True)
                db = dqf[p] * qf[p] - dkn[p] * kn[p] - dqn[p] * qn[p] + dkp[p] * kp[p] - dke[p] * ke[p]
                dbl = jnp.sum(dke[p] * ke[p], axis=0, keepdims=True) + ddec * dec[p]
                dbs[p] = db + jnp.where(last_row, dbl, 0.0)
            dla = {p: jnp.dot(triu, dbs[p], precision=HIGHEST, preferred_element_type=F32) for p in P}
            for p in P:
                g, h = p
                dq = (dqf[p] * ep[p] + dqn[p] * en[p]) * GLA_SCALE
                dk = dkn[p] * en[p] + dkp[p] * ep[p] + dke[p] * ee[p]
                dgk_scr[rows[g], kcs[h]] = dla[p] * (1.0 / 16.0) * _sigmoid(-gk_scr[rows[g], kcs[h]])
                dproj_ref[rows[g], kcs[h]] = dq.astype(BF16)
                dproj_ref[rows[g], GLA_DK + h * GLA_DK_HEAD:GLA_DK + (h + 1) * GLA_DK_HEAD] = dk.astype(BF16)
                dproj_ref[rows[g], 2 * GLA_DK + h * GLA_DV_HEAD:2 * GLA_DK + (h + 1) * GLA_DV_HEAD] = dv[p].astype(BF16)
                dproj_ref[rows[g], 2 * GLA_DK + GLA_DV + h * GLA_DV_HEAD:
                          2 * GLA_DK + GLA_DV + (h + 1) * GLA_DV_HEAD] = dgs[p].astype(BF16)
            return carry

        lax.fori_loop(0, ncb // GLA_GROUP, group, 0)
        dgk = dgk_scr[...]
        dgk_b = dgk.astype(BF16)
        dproj_ref[:, GLA_LR_OFF:GLA_IN_PAD] = _dot_nt(dgk_b, wgk_ref[...]).astype(BF16)
        dwgk_ref[...] += _dot_tn(lr, dgk_b)
        dbgk_ref[...] += jnp.sum(dgk, axis=0, keepdims=True)

    rev = lambda i: (nb - 1 - i, 0)
    return hosted_call(
        kern, side, name=name, grid=(nb,),
        in_specs=[pl.BlockSpec((TB, GLA_IN_PAD), rev),
                  pl.BlockSpec((ncb, GLA_HEADS, GLA_DV_HEAD, GLA_DK_HEAD), lambda i: (nb - 1 - i, 0, 0, 0)),
                  pl.BlockSpec((TB, GLA_DV), rev),
                  pl.BlockSpec((128, GLA_DK), lambda i: (0, 0)), _row_spec(GLA_DK), _row_spec(GLA_DV)],
        out_specs=[pl.BlockSpec((TB, GLA_IN_PAD), rev),
                   pl.BlockSpec((128, GLA_DK), lambda i: (0, 0)), _row_spec(GLA_DK), _row_spec(GLA_DV)],
        out_shape=[jax.ShapeDtypeStruct((S, GLA_IN_PAD), BF16), jax.ShapeDtypeStruct((128, GLA_DK), F32),
                   jax.ShapeDtypeStruct((1, GLA_DK), F32), jax.ShapeDtypeStruct((1, GLA_DV), F32)],
        scratch_shapes=[pltpu.VMEM((GLA_HEADS, GLA_DV_HEAD, GLA_DK_HEAD), F32), pltpu.VMEM((TB, GLA_DK), F32),
                        pltpu.VMEM((TB, GLA_DK), F32), pltpu.VMEM((TB, GLA_DK), F32)],
        dims=("arbitrary",), args=(proj, states, dzg, wgk_p, bgk, gnorm))


ATT_TW = 1024
ATT_CLASSES = 3


def _att_window(i):
    return pl.multiple_of(jnp.maximum(i * ATT_TQ - LEFT_CHUNKS * CHUNK, 0), ATT_TQ)


def _att_rel_index():
    e = jnp.arange(ATT_TW)[None, :]
    d = jnp.where(e < ATT_KW, e, e - ATT_TW)
    off = (jnp.arange(ATT_CLASSES) * ATT_TQ)[:, None]
    return jnp.clip(off - d, -MAX_REL, MAX_REL) + MAX_REL


def _row_bits():
    return lax.broadcasted_iota(jnp.int32, (ATT_TQ, ATT_TW), 0)


def att_bias_tiles(rel_bias, name):
    pick = (jnp.arange(384)[:, None] == _att_rel_index().reshape(-1)[None, :]).astype(F32)
    tab = mm_f32(jnp.pad(rel_bias, ((0, 0), (0, 384 - N_REL))), pick, name + "_tab")
    tab = tab.reshape(ATT_HEADS * ATT_CLASSES, 1, ATT_TW)

    def kern(t_ref, o_ref):
        cls = pl.program_id(0) % ATT_CLASSES
        x = jnp.broadcast_to(t_ref[...], (ATT_TQ, ATT_TW))
        x = pltpu.roll(x, 0, 1, stride=1, stride_axis=0)
        x = x[:, :ATT_KW]
        qc = cls * (ATT_TQ // CHUNK) + lax.shift_right_arithmetic(
            lax.broadcasted_iota(jnp.int32, (ATT_TQ, ATT_KW), 0), 6)
        kc = lax.shift_right_arithmetic(lax.broadcasted_iota(jnp.int32, (ATT_TQ, ATT_KW), 1), 6)
        o_ref[...] = jnp.where((kc <= qc) & (kc >= qc - LEFT_CHUNKS), x, NEG_INF)

    return pl.pallas_call(
        kern, name=name, grid=(ATT_HEADS * ATT_CLASSES,),
        in_specs=[pl.BlockSpec((None, 1, ATT_TW), lambda i: (i, 0, 0))],
        out_specs=pl.BlockSpec((None, ATT_TQ, ATT_KW), lambda i: (i, 0, 0)),
        out_shape=jax.ShapeDtypeStruct((ATT_HEADS * ATT_CLASSES, ATT_TQ, ATT_KW), F32),
        compiler_params=_cp(("parallel",)),
    )(tab)


def att_bias_grad(dbt, name):
    def kern(d_ref, o_ref):
        x = jnp.concatenate([d_ref[...], jnp.zeros((ATT_TQ, ATT_TW - ATT_KW), F32)], axis=1)
        x = jnp.concatenate([pltpu.roll(x[r0:r0 + 8, :], (ATT_TW - r0) % ATT_TW, axis=1)
                             for r0 in range(0, ATT_TQ, 8)], axis=0)
        row = _row_bits()
        for b in range(3):
            x = jnp.where((row & (1 << b)) != 0, pltpu.roll(x, ATT_TW - (1 << b), axis=1), x)
        o_ref[...] = jnp.sum(x, axis=0, keepdims=True)

    diag = pl.pallas_call(
        kern, name=name + "_diag", grid=(ATT_HEADS * ATT_CLASSES,),
        in_specs=[pl.BlockSpec((None, ATT_TQ, ATT_KW), lambda i: (i, 0, 0))],
        out_specs=pl.BlockSpec((None, 1, ATT_TW), lambda i: (i, 0, 0)),
        out_shape=jax.ShapeDtypeStruct((ATT_HEADS * ATT_CLASSES, 1, ATT_TW), F32),
        compiler_params=_cp(("parallel",)),
    )(dbt)
    diag = diag.reshape(ATT_HEADS, ATT_CLASSES * ATT_TW)
    onehot = (_att_rel_index().reshape(-1)[:, None] == jnp.arange(384)[None, :]).astype(F32)
    return mm_f32(diag, onehot, name + "_bins")[:, :N_REL]


ATT_GROUP = 2


def _att_scores(q_ref, k_ref, bias_refs, blk0):
    G = range(ATT_GROUP)
    hs = [slice(hh * ATT_HD, (hh + 1) * ATT_HD) for hh in range(2)]
    rows = [slice(g * ATT_TQ, (g + 1) * ATT_TQ) for g in G]
    wins = [pl.ds(_att_window(blk0 + g), ATT_KW) for g in G]
    kw = [k_ref[w, :] for w in wins]
    P = [(g, hh) for g in G for hh in range(2)]
    q = {p: q_ref[rows[p[0]], hs[p[1]]] * ATT_SCALE for p in P}
    k = {p: kw[p[0]][:, hs[p[1]]] for p in P}
    s = {p: _dot_nt(q[p], k[p]) + bias_refs[p[0]][p[1]] for p in P}
    e = {p: jnp.exp(s[p] - jnp.max(s[p], axis=-1, keepdims=True)) for p in P}
    inv = {p: 1.0 / jnp.sum(e[p], axis=-1, keepdims=True) for p in P}
    return P, rows, wins, hs, q, k, e, inv


def _att_specs(S):
    nq = D_MODEL // 128
    q_spec = pl.BlockSpec((ATT_GROUP * ATT_TQ, 128), lambda p, i: (i, p))
    k_spec = pl.BlockSpec((S, 128), lambda p, i: (0, nq + p))
    v_spec = pl.BlockSpec((S, 128), lambda p, i: (0, 2 * nq + p))
    b_specs = [pl.BlockSpec((2, None, ATT_TQ, ATT_KW),
                            functools.partial(lambda p, i, g: (p, jnp.minimum(ATT_GROUP * i + g, ATT_CLASSES - 1), 0, 0), g=g))
               for g in range(ATT_GROUP)]
    return q_spec, k_spec, v_spec, b_specs


def attn_fwd(qkv, bias, name, side=None):
    S = qkv.shape[0]
    q_spec, k_spec, v_spec, b_specs = _att_specs(S)

    def kern(q_ref, k_ref, v_ref, *rest):
        bias_refs, o_ref = rest[:ATT_GROUP], rest[ATT_GROUP]
        P, rows, wins, hs, _, _, e, inv = _att_scores(q_ref, k_ref, bias_refs, ATT_GROUP * pl.program_id(1))
        vw = [v_ref[w, :] for w in wins]
        o = {p: _dot(e[p].astype(BF16), vw[p[0]][:, hs[p[1]]]) * inv[p] for p in P}
        for g in range(ATT_GROUP):
            o_ref[rows[g], :] = jnp.concatenate([o[(g, 0)], o[(g, 1)]], axis=1).astype(BF16)

    return hosted_call(
        kern, side, name=name, grid=(ATT_HEADS // 2, S // (ATT_GROUP * ATT_TQ)),
        in_specs=[q_spec, k_spec, v_spec] + b_specs,
        out_specs=[pl.BlockSpec((ATT_GROUP * ATT_TQ, 128), lambda p, i: (i, p))],
        out_shape=[jax.ShapeDtypeStruct((S, D_MODEL), BF16)],
        scratch_shapes=[], dims=("parallel", "arbitrary"), args=(qkv, qkv, qkv) + (bias,) * ATT_GROUP)


def attn_bwd(qkv, bias, do, name, side=None):
    S = qkv.shape[0]
    nstep = S // (ATT_GROUP * ATT_TQ)
    q_spec, k_spec, v_spec, b_specs = _att_specs(S)

    def kern(q_ref, k_ref, v_ref, *rest):
        bias_refs = rest[:ATT_GROUP]
        do_ref, dqkv_ref, db_ref, dk_scr, dv_scr = rest[ATT_GROUP:]
        i = pl.program_id(1)

        @pl.when(i == 0)
        def _():
            dk_scr[...] = jnp.zeros_like(dk_scr)
            dv_scr[...] = jnp.zeros_like(dv_scr)
            db_ref[...] = jnp.zeros_like(db_ref)
        blk0 = ATT_GROUP * i
        P, rows, wins, hs, q, k, e, inv = _att_scores(q_ref, k_ref, bias_refs, blk0)
        vw = [v_ref[w, :] for w in wins]
        do_h = {p: do_ref[rows[p[0]], hs[p[1]]] for p in P}
        dp = {p: _dot_nt(do_h[p], vw[p[0]][:, hs[p[1]]]) for p in P}
        pr = {p: e[p] * inv[p] for p in P}
        dvs = {p: _dot_tn(pr[p].astype(BF16), do_h[p]) for p in P}
        ds = {p: pr[p] * (dp[p] - jnp.sum(pr[p] * dp[p], axis=-1, keepdims=True)) for p in P}
        ds_b = {p: ds[p].astype(BF16) for p in P}
        dqs = {p: _dot(ds_b[p], k[p]) * ATT_SCALE for p in P}
        dks = {p: _dot_tn(ds_b[p], q[p]) for p in P}
        for g, hh in P:
            db_ref[hh, jnp.minimum(blk0 + g, ATT_CLASSES - 1)] += ds[(g, hh)]
        for g in range(ATT_GROUP):
            first = pl.multiple_of((blk0 + g) * ATT_TQ, ATT_TQ)
            dqkv_ref[0, pl.ds(first, ATT_TQ), :] = jnp.concatenate([dqs[(g, 0)], dqs[(g, 1)]], axis=1).astype(BF16)
            dk_scr[wins[g], :] += jnp.concatenate([dks[(g, 0)], dks[(g, 1)]], axis=1)
            dv_scr[wins[g], :] += jnp.concatenate([dvs[(g, 0)], dvs[(g, 1)]], axis=1)

        @pl.when(i == nstep - 1)
        def _():
            dqkv_ref[1] = dk_scr[...].astype(BF16)
            dqkv_ref[2] = dv_scr[...].astype(BF16)

    return hosted_call(
        kern, side, name=name, grid=(ATT_HEADS // 2, nstep),
        in_specs=[q_spec, k_spec, v_spec] + b_specs + [pl.BlockSpec((ATT_GROUP * ATT_TQ, 128), lambda p, i: (i, p))],
        out_specs=[pl.BlockSpec((3, S, 128), lambda p, i: (0, 0, p)),
                   pl.BlockSpec((2, ATT_CLASSES, ATT_TQ, ATT_KW), lambda p, i: (p, 0, 0, 0))],
        out_shape=[jax.ShapeDtypeStruct((3, S, D_MODEL), BF16),
                   jax.ShapeDtypeStruct((ATT_HEADS, ATT_CLASSES, ATT_TQ, ATT_KW), F32)],
        scratch_shapes=[pltpu.VMEM((S, 128), F32), pltpu.VMEM((S, 128), F32)],
        dims=("parallel", "arbitrary"), args=(qkv, qkv, qkv) + (bias,) * ATT_GROUP + (do,))


def _me():
    return lax.axis_index("x"), lax.axis_index("y"), lax.axis_index("c")


def _other_chips(x, y):
    return [(1 - x, y), (x, 1 - y), (1 - x, 1 - y)]


def all_gather8(x_shard, name):
    m_per, n = x_shard.shape

    def body(x_ref, out_ref, send_sems, recv_sems, local_sem):
        x, y, c = _me()
        me, sibling = (x, y, c), (x, y, 1 - c)
        chips = _other_chips(x, y)

        def rows(px, py, pc):
            return out_ref.at[pl.ds((4 * px + 2 * py + pc) * m_per, m_per), :]

        def copy(k, block, to, src=None):
            return pltpu.make_async_remote_copy(
                src_ref=rows(*block) if src is None else src, dst_ref=rows(*block),
                send_sem=send_sems.at[k], recv_sem=recv_sems.at[k], device_id=to, device_id_type=MESH)

        mine = pltpu.make_async_copy(x_ref, rows(*me), local_sem)
        mine.start()
        first = [copy(0, me, sibling, src=x_ref)]
        first += [copy(1 + j, me, (*chip, c), src=x_ref) for j, chip in enumerate(chips)]
        for cp in first:
            cp.start()
        passed = [copy(4 + j, (*chip, c), sibling) for j, chip in enumerate(chips)]
        for j, chip in enumerate(chips):
            copy(1 + j, (*chip, c), me).wait_recv()
            passed[j].start()
        copy(0, sibling, me).wait_recv()
        for j, chip in enumerate(chips):
            copy(4 + j, (*chip, 1 - c), me).wait_recv()
        for cp in first + passed:
            cp.wait_send()
        mine.wait()

    return pl.pallas_call(
        body, name=name,
        out_shape=jax.ShapeDtypeStruct((N_DEV * m_per, n), x_shard.dtype),
        in_specs=[pl.BlockSpec(memory_space=pltpu.VMEM)],
        out_specs=pl.BlockSpec(memory_space=pltpu.VMEM),
        scratch_shapes=[pltpu.SemaphoreType.DMA((7,)), pltpu.SemaphoreType.DMA((7,)), pltpu.SemaphoreType.DMA],
        compiler_params=pltpu.CompilerParams(vmem_limit_bytes=VMEM_LIMIT),
    )(x_shard)


def _half_rows(n_rows, c):
    h = n_rows // 2
    return pl.ds(c * h, h)


def _gathered_shape(shape, flavour):
    L, a, b = shape
    return {"col": (L, a, N_CHIPS * b), "row": (L, N_CHIPS * a, b), "lead": (N_CHIPS, L, a, b)}[flavour]


def _gathered_part(out_ref, shape, flavour, s, rows):
    L, a, b = shape
    if flavour == "col":
        return out_ref.at[:, rows, pl.ds(s * b, b)]
    if flavour == "row":
        return out_ref.at[:, pl.ds(s * a + rows.start, rows.size), :]
    return out_ref.at[s, :, rows, :]


def gather_side(shards, flavours):
    n = len(shards)
    shapes = [w.shape for w in shards]

    def copies(w_refs, out_refs, send_sems, recv_sems, local_sems):
        x, y, c = _me()
        sibling = (x, y, 1 - c)
        chips = _other_chips(x, y)
        me_s = 2 * x + y

        def copy(k, src, dst, to):
            return pltpu.make_async_remote_copy(src_ref=src, dst_ref=dst, send_sem=send_sems.at[k],
                                                recv_sem=recv_sems.at[k], device_id=to, device_id_type=MESH)

        own, first, landed, passed, passed_in = [], [], [], [], []
        for w in range(n):
            shp, fl = shapes[w], flavours[w]
            my_half = _half_rows(shp[1], c)
            sib_half = _half_rows(shp[1], 1 - c)
            own.append(copy(7 * w + 6, w_refs[w], _gathered_part(out_refs[w], shp, fl, me_s, pl.ds(0, shp[1])), sibling))
            for j, chip in enumerate(chips):
                s = 2 * chip[0] + chip[1]
                first.append(copy(7 * w + j, w_refs[w].at[:, my_half, :],
                                  _gathered_part(out_refs[w], shp, fl, me_s, my_half), (*chip, c)))
                part = _gathered_part(out_refs[w], shp, fl, s, my_half)
                landed.append(copy(7 * w + j, part, part, (*chip, c)))
                passed.append(copy(7 * w + 3 + j, part, part, sibling))
                theirs = _gathered_part(out_refs[w], shp, fl, s, sib_half)
                passed_in.append(copy(7 * w + 3 + j, theirs, theirs, sibling))
        return own, first, landed, passed, passed_in

    def start(*refs):
        own, first, _, _, _ = copies(*refs)
        for cp in first + own:
            cp.start()

    def wait(*refs):
        own, first, landed, passed, passed_in = copies(*refs)
        for arrived, onward in zip(landed, passed):
            arrived.wait_recv()
            onward.start()
        for cp in passed_in:
            cp.wait_recv()
        for cp in own:
            cp.wait()
        for cp in first + passed:
            cp.wait_send()

    out_shapes = [jax.ShapeDtypeStruct(_gathered_shape(s, f), w.dtype) for w, s, f in zip(shards, shapes, flavours)]
    return Side(shards, out_shapes, 7 * n, 0, start, wait)


def swap_side(gs):
    n = len(gs)

    def copies(g_refs, out_refs, send_sems, recv_sems, local_sems):
        x, y, c = _me()
        return [pltpu.make_async_remote_copy(
            src_ref=g_refs[w].at[:, _half_rows(gs[w].shape[1], 1 - c), :], dst_ref=out_refs[w],
            send_sem=send_sems.at[w], recv_sem=recv_sems.at[w], device_id=(x, y, 1 - c), device_id_type=MESH)
            for w in range(n)]

    def start(*refs):
        for cp in copies(*refs):
            cp.start()

    def wait(*refs):
        for cp in copies(*refs):
            cp.wait()

    out_shapes = [jax.ShapeDtypeStruct((g.shape[0], g.shape[1] // 2, g.shape[2]), g.dtype) for g in gs]
    return Side(gs, out_shapes, n, 0, start, wait)


def add_half(g, r1, c_idx, name):
    n, R, C = g.shape
    half = R // 2
    tr = _rows_block(half, C)
    nbh = half // tr

    def kern(c_ref, g_ref, r_ref, o_ref, ob_ref):
        p = g_ref[...] + r_ref[...]
        o_ref[...] = p
        ob_ref[...] = p.astype(BF16)

    spec = pl.BlockSpec((1, tr, C), lambda d, r, c_ref: (d, r, 0))
    return pl.pallas_call(
        kern, name=name,
        grid_spec=pltpu.PrefetchScalarGridSpec(
            num_scalar_prefetch=1, grid=(n, nbh),
            in_specs=[pl.BlockSpec((1, tr, C), lambda d, r, c_ref: (d, c_ref[0] * nbh + r, 0)), spec],
            out_specs=[spec, spec]),
        out_shape=[jax.ShapeDtypeStruct((n, half, C), F32), jax.ShapeDtypeStruct((n, half, C), BF16)],
        compiler_params=_cp(("parallel", "parallel")),
    )(c_idx, g, r1)


def exchange_side(ps):
    n = len(ps)

    def copies(p_refs, out_refs, send_sems, recv_sems, local_sems):
        x, y, c = _me()
        return [pltpu.make_async_remote_copy(
            src_ref=p_refs[w].at[2 * chip[0] + chip[1]], dst_ref=out_refs[w].at[j],
            send_sem=send_sems.at[3 * w + j], recv_sem=recv_sems.at[3 * w + j],
            device_id=(*chip, c), device_id_type=MESH)
            for w in range(n) for j, chip in enumerate(_other_chips(x, y))]

    def start(*refs):
        for cp in copies(*refs):
            cp.start()

    def wait(*refs):
        for cp in copies(*refs):
            cp.wait()

    return Side(ps, [jax.ShapeDtypeStruct((3,) + p.shape[1:], p.dtype) for p in ps], 3 * n, 0, start, wait)


def add_chips(p, r2, chip_idx, name):
    n, H, C = p.shape
    tr = _rows_block(H, C)

    def kern(s_ref, p_ref, r_ref, o_ref):
        o_ref[...] = ((p_ref[0] + r_ref[0].astype(F32)) + r_ref[1].astype(F32)) + r_ref[2].astype(F32)

    return pl.pallas_call(
        kern, name=name,
        grid_spec=pltpu.PrefetchScalarGridSpec(
            num_scalar_prefetch=1, grid=(H // tr,),
            in_specs=[pl.BlockSpec((1, tr, C), lambda r, s_ref: (s_ref[0], r, 0)),
                      pl.BlockSpec((3, tr, C), lambda r, s_ref: (0, r, 0))],
            out_specs=pl.BlockSpec((tr, C), lambda r, s_ref: (r, 0))),
        out_shape=jax.ShapeDtypeStruct((H, C), F32),
        compiler_params=_cp(("parallel",)),
    )(chip_idx, p, r2)


def swap_reduced(ss, name):
    n = len(ss)

    def body(*refs):
        s_refs, out_refs = refs[:n], refs[n:2 * n]
        send_sems, recv_sems = refs[2 * n:]
        x, y, c = _me()
        cps = [pltpu.make_async_remote_copy(src_ref=s_refs[w], dst_ref=out_refs[w], send_sem=send_sems.at[w],
                                            recv_sem=recv_sems.at[w], device_id=(x, y, 1 - c), device_id_type=MESH)
               for w in range(n)]
        for cp in cps:
            cp.start()
        for cp in cps:
            cp.wait()

    any_spec = pl.BlockSpec(memory_space=pl.ANY)
    return pl.pallas_call(
        body, name=name, out_shape=[jax.ShapeDtypeStruct(s.shape, s.dtype) for s in ss],
        in_specs=[any_spec] * n, out_specs=[any_spec] * n,
        scratch_shapes=[pltpu.SemaphoreType.DMA((n,)), pltpu.SemaphoreType.DMA((n,))],
    )(*ss)


BIG = (("gla_w_in", 2, (1024, GLA_IN // N_CHIPS), "lead"), ("gla_w_out", 2, (256, 1024), "row"),
       ("att_w_in", 2, (1024, 768), "col"), ("att_w_out", 2, (256, 1024), "row"),
       ("ff_w1", 4, (1024, 1024), "col"), ("ff_w2", 4, (1024, 1024), "row"))
FLAVOUR = {n: f for n, _, _, f in BIG}


def layer_weights(i):
    mixer = "gla" if i % 2 == 0 else "att"
    return (("in", mixer + "_w_in", i // 2), ("out", mixer + "_w_out", i // 2), ("w1", "ff_w1", i), ("w2", "ff_w2", i))


class Comm:
    def __init__(self, weights, core, chip):
        self.weights, self.core, self.chip = weights, core, chip
        self.c_idx = jnp.reshape(core, (1,)).astype(jnp.int32)
        self.chip_idx = jnp.reshape(chip, (1,)).astype(jnp.int32)
        self.reduced = {}

    def gather(self, items):
        shards = [self.weights[n][l:l + 1].astype(BF16) for _, n, l in items]
        return gather_side(shards, [FLAVOUR[n] for _, n, _ in items])

    def full_weights(self, items, gathered):
        W = {}
        for (role, n, _), w in zip(items, gathered):
            if n == "gla_w_in":
                w = jnp.pad(w.transpose(1, 2, 0, 3).reshape(1, D_MODEL, GLA_IN), ((0, 0), (0, 0), (0, GLA_IN_PAD - GLA_IN)))
            W[role] = (w, 0)
        return W

    def gather_now(self, items, name):
        return self.full_weights(items, run_side(self.gather(items), name))

    def swap(self, items):
        return swap_side([g for _, _, g in items])

    def reduce_begin(self, tag, items, swapped):
        ps = [add_half(g, r, self.c_idx, f"rs_add2_{tag}_{w}") for w, ((_, _, g), r) in enumerate(zip(items, swapped))]
        return tag, [(n, l) for n, l, _ in items], ps

    def exchange(self, pending):
        return exchange_side([pb for _, pb in pending[2]])

    def reduce_mid(self, pending, landed):
        tag, keys, ps = pending
        for w, (key, (p, _), r) in enumerate(zip(keys, ps, landed)):
            self.reduced[key] = add_chips(p, r, self.chip_idx, f"rs_add4_{tag}_{w}")

    def reduce_tail(self, tag, items):
        pending = self.reduce_begin(tag, items, run_side(self.swap(items), f"rs_swap_{tag}"))
        self.reduce_mid(pending, run_side(self.exchange(pending), f"rs_xchg_{tag}"))

    def reduce_end(self):
        keys = [(n, l) for n, L, _, _ in BIG for l in range(L)]
        mine = [self.reduced[k] for k in keys]
        theirs = swap_reduced(mine, "rs_join")
        low = self.core == 0
        full = {k: jnp.concatenate([jnp.where(low, m, t), jnp.where(low, t, m)], axis=0)
                for k, m, t in zip(keys, mine, theirs)}
        return {n: jnp.stack([full[(n, l)] for l in range(L)]) for n, L, _, _ in BIG}


def local_step(x, target, mods, comm, small):
    S, D = x.shape
    row = lambda v: v.reshape(1, -1)
    saved = []
    tiles = [att_bias_tiles(small["att_rel_bias"][j], f"att_tiles_{j}").reshape(ATT_HEADS, ATT_CLASSES, ATT_TQ, ATT_KW)
             for j in range(2)]
    wgk_p = [jnp.pad(small["gla_w_gk2"][j], ((0, 128 - GLA_RANK), (0, 0))).astype(BF16) for j in range(2)]

    u1 = modulate(x, row(mods[0, 1]), row(mods[0, 0]), "mod_first")
    Ws = [dict() for _ in range(DEPTH)]
    items0 = layer_weights(0)
    Ws[0].update(comm.gather_now(items0[:1], "gather_w0"))
    for i in range(DEPTH):
        j = i // 2
        W = Ws[i]
        sh1, sc1, g1, sh2, sc2, g2 = (row(mods[i, k]) for k in range(6))
        nxt = min(i + 1, DEPTH - 1)
        more = i + 1 < DEPTH
        nxt_items = layer_weights(nxt)
        in_items = list(items0[1:3]) if i == 0 else []
        mix_items = (list(items0[3:]) if i == 0 else []) + (list(nxt_items[:2]) if more else [])
        up_items = list(nxt_items[2:3]) if more else []
        down_items = list(nxt_items[3:]) if more else []

        def hosted(items):
            return comm.gather(items) if items else None

        def landed_weights(items, landed):
            for k, it in enumerate(items):
                layer = 0 if it in items0 and i == 0 else nxt
                Ws[layer].update(comm.full_weights([it], landed[k:k + 1]))

        side = hosted(in_items)
        if i % 2 == 0:
            proj = mm_plain(u1, *W["in"], f"gla_in_{i}", side=side)
        else:
            proj = mm_plain(u1, *W["in"], f"att_in_{i}", mode="bf16", bias=row(small["att_b_in"][j]), side=side)
        proj, landed = proj if side is not None else (proj, [])
        landed_weights(in_items, landed)
        if i % 2 == 0:
            (zmix, states), landed = gla_fwd(proj, wgk_p[j], row(small["gla_b_gk"][j]), row(small["gla_g_norm"][j]),
                                             f"gla_fwd_{i}", hosted(mix_items))
        else:
            (zmix,), landed = attn_fwd(proj, tiles[j], f"att_fwd_{i}", hosted(mix_items))
            states = None
        landed_weights(mix_items, landed)
        (y1, x_mid, u2), _ = mm_down_ln(zmix, *W["out"], x, 1.0 + g1, row(small["ln_g"][i, 0]),
                                        row(small["ln_b"][i, 0]), sc2, sh2, f"mix_out_{i}")
        side = hosted(up_items)
        act = mm_plain(u2, *W["w1"], f"ff_up_{i}", mode="mlp_up", side=side)
        act, landed = act if side is not None else (act, [])
        landed_weights(up_items, landed)
        (y2, x_out, u_next), landed = mm_down_ln(act, *W["w2"], x_mid, 1.0 + g2, row(small["ln_g"][i, 1]),
                                                 row(small["ln_b"][i, 1]), row(mods[nxt, 1]), row(mods[nxt, 0]),
                                                 f"ff_out_{i}", side=hosted(down_items))
        landed_weights(down_items, landed)
        saved.append(dict(x_in=x, u1=u1, proj=proj, zmix=zmix, states=states, y1=y1, x_mid=x_mid, u2=u2,
                          act=act, y2=y2))
        x, u1 = x_out, u_next

    g_small = dict(ln_g=[None] * DEPTH, ln_b=[None] * DEPTH, gla_w_gk2=[None] * 2, gla_b_gk=[None] * 2,
                   gla_g_norm=[None] * 2, att_b_in=[None] * 2, att_rel_bias=[None] * 2)
    dmods = [None] * DEPTH
    later = []
    top = saved[DEPTH - 1]
    dz2, dy2, s_ln2, sq = loss_ln_bwd(x, target, top["x_mid"], top["y2"], 1.0 + row(mods[DEPTH - 1, 5]),
                                      row(small["ln_g"][DEPTH - 1, 1]), "loss_ln_bwd")

    for i in reversed(range(DEPTH)):
        j = i // 2
        sv = saved[i]
        W = Ws[i]
        sh1, sc1, g1, sh2, sc2, g2 = (row(mods[i, k]) for k in range(6))
        dh = mm_plain(dy2, *W["w2"], f"ff_dn_{i}", mode="mlp_dn", nt=True, h=sv["act"])
        g_w2 = mm_w_res(sv["act"], dy2, f"ff_w2g_{i}").reshape(N_CHIPS, D_FF // N_CHIPS, D)
        g_w1 = mm_w_res(sv["u2"], dh, f"ff_w1g_{i}", chips_out=True)
        items = [("ff_w1", i, g_w1), ("ff_w2", i, g_w2)] + later
        (dz1, dy1, s_m2, s_ln1), swapped = mm_down_comb(
            dh, *W["w1"], dz2, sv["x_mid"], 1.0 + sc2, f"ff_dx_{i}",
            ln=(sv["x_in"], sv["y1"], 1.0 + g1, row(small["ln_g"][i, 0])), side=comm.swap(items))
        pending = comm.reduce_begin(i, items, swapped)
        side = comm.exchange(pending)
        mixer = "gla" if i % 2 == 0 else "att"
        below = None
        if i > 0:
            below = (saved[i - 1]["x_mid"], saved[i - 1]["y2"], 1.0 + row(mods[i - 1, 5]), row(small["ln_g"][i - 1, 1]))
        if i % 2 == 0:
            g_out = mm_w_res(sv["zmix"], dy1, f"gla_wog_{i}", ts=1024).reshape(N_CHIPS, D // N_CHIPS, D)
            dzg = mm_plain(dy1, *W["out"], f"gla_dz_{i}", nt=True)
            (dproj, dwgk, dbgk, dgn), landed = gla_bwd(sv["proj"], sv["states"], dzg, wgk_p[j],
                                                       row(small["gla_b_gk"][j]), row(small["gla_g_norm"][j]),
                                                       f"gla_bwd_{i}", side)
            g_small["gla_w_gk2"][j] = dwgk[:GLA_RANK]
            g_small["gla_b_gk"][j] = dbgk[0]
            g_small["gla_g_norm"][j] = dgn[0].reshape(GLA_HEADS, GLA_DV_HEAD)
            gwi = mm_w_res(sv["u1"], dproj, f"gla_wig_{i}")[:, :GLA_IN]
            g_in = gwi.reshape(D, N_CHIPS, GLA_IN // N_CHIPS).transpose(1, 0, 2)
            outs, _ = mm_down_comb(dproj, *W["in"], dz1, sv["x_in"], 1.0 + sc1, f"mix_dx_{i}", ln=below)
        else:
            g_out = mm_w_res(sv["zmix"], dy1, f"att_wog_{i}", ts=1024).reshape(N_CHIPS, D // N_CHIPS, D)
            do = mm_plain(dy1, *W["out"], f"att_do_{i}", mode="bf16", nt=True)
            (dqkv, dbt), landed = attn_bwd(sv["proj"], tiles[j], do, f"att_bwd_{i}", side)
            g_small["att_rel_bias"][j] = att_bias_grad(dbt.reshape(ATT_HEADS * ATT_CLASSES, ATT_TQ, ATT_KW),
                                                       f"att_bias_{i}")
            g_in = mm_w_chips3(sv["u1"], dqkv, f"att_wig_{i}")
            outs, _ = mm_down_comb(dqkv, *W["in"], dz1, sv["x_in"], 1.0 + sc1, f"mix_dx_{i}", parts=3, ln=below)
        s_m1 = outs[1] if below is None else outs[2]
        if i % 2 == 1:
            g_small["att_b_in"][j] = s_m1[2:5].reshape(3 * D)
        comm.reduce_mid(pending, landed)
        later = [(mixer + "_w_in", j, g_in), (mixer + "_w_out", j, g_out)]
        g_small["ln_g"][i] = jnp.stack([s_ln1[0], s_ln2[0]])
        g_small["ln_b"][i] = jnp.stack([s_ln1[1], s_ln2[1]])
        dmods[i] = jnp.stack([s_m1[1], s_m1[0], s_ln1[2], s_m2[1], s_m2[0], s_ln2[2]])
        if below is None:
            dx = outs[0]
        else:
            dz2, dy2, s_ln2 = outs[0], outs[1], outs[3]
    comm.reduce_tail("last", later)

    g_small = {n: jnp.stack(v) for n, v in g_small.items()}
    return sq, dx, jnp.stack(dmods), g_small


SMALL_SHARDED = (("ln_g", (4, 2, 256)), ("ln_b", (4, 2, 256)), ("gla_g_norm", (2, 4, 64)),
                 ("gla_w_gk2", (2, 16, 128)), ("att_b_in", (2, 768)))
SMALL_FULL = dict(ln_g=(4, 2, 1024), ln_b=(4, 2, 1024), gla_g_norm=(2, 4, 256), gla_w_gk2=(2, 16, 512),
                  att_b_in=(2, 3072), gla_b_gk=(2, 512), att_rel_bias=(2, 16, 257))
SMALL_GRAD_ORDER = ("ln_g", "ln_b", "gla_g_norm", "gla_w_gk2", "att_b_in", "gla_b_gk", "att_rel_bias")


def _pack_small(arrs, rows_total):
    parts = []
    for a in arrs:
        flat = a.reshape(-1)
        pad = (-flat.shape[0]) % PACK_W
        parts.append(jnp.pad(flat, (0, pad)).reshape(-1, PACK_W))
    buf = jnp.concatenate(parts, axis=0)
    return jnp.pad(buf, ((0, rows_total - buf.shape[0]), (0, 0)))


def _unpack_small(buf, shapes):
    out, r = [], 0
    for shp in shapes:
        n = 1
        for s in shp:
            n *= s
        nr = (n + PACK_W - 1) // PACK_W
        out.append(buf[..., r:r + nr, :].reshape(buf.shape[:-2] + (nr * PACK_W,))[..., :n].reshape(buf.shape[:-2] + shp))
        r += nr
    return out


def _unshard_last(g4):
    nd = g4.ndim
    perm = tuple(range(1, nd - 1)) + (0, nd - 1)
    t = g4.transpose(perm)
    return t.reshape(t.shape[:-2] + (-1,))


def _shard_last(full, s):
    n = full.shape[-1] // N_CHIPS
    return lax.dynamic_slice_in_dim(full, s * n, n, axis=full.ndim - 1)


WEIGHT_NAMES = ("w_ada", "b_ada", "ln_g", "ln_b", "gla_w_in", "gla_w_gk2", "gla_b_gk", "gla_g_norm", "gla_w_out",
                "att_w_in", "att_b_in", "att_rel_bias", "att_w_out", "ff_w1", "ff_w2")


def kernel(x, c, w_ada, b_ada, ln_g, ln_b, gla_w_in, gla_w_gk2, gla_b_gk, gla_g_norm, gla_w_out, att_w_in, att_b_in, att_rel_bias, att_w_out, ff_w1, ff_w2, loss_target, m_w_ada, m_b_ada, m_ln_g, m_ln_b, m_gla_w_in, m_gla_w_gk2, m_gla_b_gk, m_gla_g_norm, m_gla_w_out, m_att_w_in, m_att_b_in, m_att_rel_bias, m_att_w_out, m_ff_w1, m_ff_w2, v_w_ada, v_b_ada, v_ln_g, v_ln_b, v_gla_w_in, v_gla_w_gk2, v_gla_b_gk, v_gla_g_norm, v_gla_w_out, v_att_w_in, v_att_b_in, v_att_rel_bias, v_att_w_out, v_ff_w1, v_ff_w2):
    weights = dict(w_ada=w_ada, b_ada=b_ada, ln_g=ln_g, ln_b=ln_b, gla_w_in=gla_w_in, gla_w_gk2=gla_w_gk2,
                   gla_b_gk=gla_b_gk, gla_g_norm=gla_g_norm, gla_w_out=gla_w_out, att_w_in=att_w_in,
                   att_b_in=att_b_in, att_rel_bias=att_rel_bias, att_w_out=att_w_out, ff_w1=ff_w1, ff_w2=ff_w2)
    mom1 = dict(w_ada=m_w_ada, b_ada=m_b_ada, ln_g=m_ln_g, ln_b=m_ln_b, gla_w_in=m_gla_w_in, gla_w_gk2=m_gla_w_gk2,
                gla_b_gk=m_gla_b_gk, gla_g_norm=m_gla_g_norm, gla_w_out=m_gla_w_out, att_w_in=m_att_w_in,
                att_b_in=m_att_b_in, att_rel_bias=m_att_rel_bias, att_w_out=m_att_w_out, ff_w1=m_ff_w1, ff_w2=m_ff_w2)
    mom2 = dict(w_ada=v_w_ada, b_ada=v_b_ada, ln_g=v_ln_g, ln_b=v_ln_b, gla_w_in=v_gla_w_in, gla_w_gk2=v_gla_w_gk2,
                gla_b_gk=v_gla_b_gk, gla_g_norm=v_gla_g_norm, gla_w_out=v_gla_w_out, att_w_in=v_att_w_in,
                att_b_in=v_att_b_in, att_rel_bias=v_att_rel_bias, att_w_out=v_att_w_out, ff_w1=v_ff_w1, ff_w2=v_ff_w2)

    ax, ay, ac = lax.axis_index("x"), lax.axis_index("y"), lax.axis_index("c")
    chip = 2 * ax + ay
    dev = 2 * chip + ac
    S = x.shape[1]
    x2 = x.reshape(S, D_MODEL)
    t2 = loss_target.reshape(S, D_MODEL)

    comm = Comm(weights, ac, chip)

    small_rows = 16
    spack = _pack_small([c] + [weights[n] for n, _ in SMALL_SHARDED], small_rows)
    sg = all_gather8(spack, "gather_small").reshape(N_DEV, small_rows, PACK_W)
    parts = _unpack_small(sg, [(1, D_MODEL)] + [shp for _, shp in SMALL_SHARDED])
    c_all = parts[0].reshape(N_DEV, D_MODEL)
    small = {n: _unshard_last(p[0::2]) for (n, _), p in zip(SMALL_SHARDED, parts[1:])}
    small["gla_b_gk"] = gla_b_gk
    small["att_rel_bias"] = att_rel_bias

    c_act = silu_rows(jnp.pad(c_all, ((0, 128 - N_DEV), (0, 0))), "silu_c")
    mods_part = jnp.stack([mm_plain(c_act, w_ada, l, f"ada_fwd_{l}", tm=128)[:N_DEV] for l in range(DEPTH)], axis=1)
    mods_part = mods_part.reshape(N_DEV, DEPTH * 6 * D_MODEL // N_CHIPS)
    mg = all_gather8(mods_part, "gather_mods").reshape(N_CHIPS, 2, N_DEV, DEPTH, 6 * D_MODEL // N_CHIPS)
    mods_mine = lax.dynamic_index_in_dim(mg[:, 0], dev, axis=1, keepdims=False)
    mods = mods_mine.transpose(1, 0, 2).reshape(DEPTH, 6 * D_MODEL) + b_ada
    mods = mods.reshape(DEPTH, 6, D_MODEL)

    sq, grad_x, dmods, g_small = local_step(x2, t2, mods, comm, small)
    loss = lax.psum(0.5 * sq[0, 0] / D_MODEL, ("x", "y", "c"))

    g_shard = comm.reduce_end()

    dm_flat = dmods.reshape(DEPTH, 6 * D_MODEL)
    g_rows = 80
    gpack = _pack_small([dm_flat] + [g_small[n] for n in SMALL_GRAD_ORDER], g_rows)
    gg = all_gather8(gpack, "gather_small_grads").reshape(N_DEV, g_rows, PACK_W)
    gsum = sum_over_devices(gg, "sum_small_grads")
    sums = _unpack_small(gsum, [(DEPTH, 6 * D_MODEL)] + [SMALL_FULL[n] for n in SMALL_GRAD_ORDER])
    grads = dict(b_ada=sums[0])
    for n, full_g in zip(SMALL_GRAD_ORDER, sums[1:]):
        grads[n] = full_g if n in ("gla_b_gk", "att_rel_bias") else _shard_last(full_g, chip)
    dm_all = _unpack_small(gg, [(DEPTH, 6 * D_MODEL)])[0]
    dm_cols = _shard_last(dm_all, chip).reshape(N_DEV, DEPTH * 6 * D_MODEL // N_CHIPS)
    dm_cols = jnp.pad(dm_cols, ((0, 128 - N_DEV), (0, 0))).astype(BF16)
    gwa = mm_w(c_act, dm_cols, "ada_bwd", ts=128)
    grads["w_ada"] = gwa.reshape(D_MODEL, DEPTH, 6 * D_MODEL // N_CHIPS).transpose(1, 0, 2)
    grads.update(g_shard)

    deltas, new_m, new_v = {}, {}, {}
    for n in WEIGHT_NAMES:
        deltas[n], new_m[n], new_v[n] = adamw(weights[n], grads[n], mom1[n], mom2[n], "adamw_" + n)

    return (loss, grad_x.reshape(1, S, D_MODEL), *[grads[n] for n in WEIGHT_NAMES], *[deltas[n] for n in WEIGHT_NAMES],
            *[new_m[n] for n in WEIGHT_NAMES], *[new_v[n] for n in WEIGHT_NAMES])
```

```python
import functools

import jax
import jax.numpy as jnp
from jax import lax
from jax.experimental import pallas as pl
from jax.experimental.pallas import tpu as pltpu

F32 = jnp.float32
BF16 = jnp.bfloat16
HIGHEST = lax.Precision.HIGHEST
MESH = pl.DeviceIdType.MESH

D_MODEL = 1024
DEPTH = 4
CHUNK = 64
GLA_HEADS = 4
GLA_DK = 512
GLA_DV = 1024
GLA_DK_HEAD = 128
GLA_DV_HEAD = 256
GLA_RANK = 16
GLA_IN = 3088
GLA_IN_PAD = 3200
GLA_LR_OFF = 3072
ATT_HEADS = 16
ATT_HD = 64
LEFT_CHUNKS = 8
MAX_REL = 128
N_REL = 257
D_FF = 4096
ALPHA = (2.0 * DEPTH) ** 0.25
LN_EPS = 1e-5
RMS_EPS = 1e-6
NEG_INF = -1e30
GLA_SCALE = GLA_DK_HEAD ** -0.5
ATT_SCALE = ATT_HD ** -0.5
ADAM_LR = 0.001
ADAM_B1 = 0.9
ADAM_B2 = 0.999
ADAM_EPS = 1e-08
ADAM_WD = 0.01
ADAM_STEP = 10

ATT_TQ = 256
ATT_KW = 768
GLA_TB = 256
GLA_GROUP = 2
GLA_GROUP_FWD = 4
VMEM_LIMIT = 56 * 1024 * 1024
WHOLE_WEIGHT_BYTES = 8 * 1024 * 1024
N_CHIPS = 4
N_DEV = 8
PACK_W = 1024


def _dot(a, b):
    return jnp.dot(a, b, preferred_element_type=F32)


def _dot_nt(a, b):
    return lax.dot_general(a, b, (((1,), (1,)), ((), ())), preferred_element_type=F32)


def _dot_tn(a, b):
    return lax.dot_general(a, b, (((0,), (0,)), ((), ())), preferred_element_type=F32)


def _cp(sem, vmem=VMEM_LIMIT):
    return pltpu.CompilerParams(dimension_semantics=sem, vmem_limit_bytes=vmem)


def _row_spec(n):
    return pl.BlockSpec((1, n), lambda *_: (0, 0))


def _sigmoid(x):
    return 1.0 / (1.0 + jnp.exp(-x))


def _log_sigmoid(x):
    return jnp.minimum(x, 0.0) - jnp.log1p(jnp.exp(-jnp.abs(x)))


class Side:
    def __init__(self, ins, out_shapes, n_sems, n_local, start, wait):
        self.ins, self.out_shapes, self.n_sems, self.n_local = list(ins), list(out_shapes), n_sems, n_local
        self.start, self.wait = start, wait

    def sem_shapes(self):
        return [pltpu.SemaphoreType.DMA((self.n_sems,)), pltpu.SemaphoreType.DMA((self.n_sems,)),
                pltpu.SemaphoreType.DMA((max(self.n_local, 1),))]


def run_side(side, name):
    n_in = len(side.ins)
    n_out = len(side.out_shapes)

    def body(*refs):
        ins, outs, sems = refs[:n_in], refs[n_in:n_in + n_out], refs[n_in + n_out:]
        side.start(ins, outs, *sems)
        side.wait(ins, outs, *sems)

    any_spec = pl.BlockSpec(memory_space=pl.ANY)
    return pl.pallas_call(body, name=name, out_shape=side.out_shapes, in_specs=[any_spec] * n_in,
                          out_specs=[any_spec] * n_out, scratch_shapes=side.sem_shapes())(*side.ins)


def hosted_call(main, side, *, name, grid, in_specs, out_specs, out_shape, scratch_shapes, dims, args):
    if side is None:
        outs = pl.pallas_call(main, name=name, grid=grid, in_specs=in_specs, out_specs=out_specs,
                              out_shape=out_shape, scratch_shapes=scratch_shapes, compiler_params=_cp(dims))(*args)
        return list(outs), []
    n_mi, n_mo, n_ms = len(in_specs), len(out_specs), len(scratch_shapes)
    n_si, n_so = len(side.ins), len(side.out_shapes)

    def kern(*refs):
        mi, si = refs[:n_mi], refs[n_mi:n_mi + n_si]
        o0 = n_mi + n_si
        mo, so = refs[o0:o0 + n_mo], refs[o0 + n_mo:o0 + n_mo + n_so]
        s0 = o0 + n_mo + n_so
        ms, sems = refs[s0:s0 + n_ms], refs[s0 + n_ms:]
        ids = [pl.program_id(d) for d in range(len(grid))]
        first = functools.reduce(jnp.logical_and, [i == 0 for i in ids])
        last = functools.reduce(jnp.logical_and, [i == g - 1 for i, g in zip(ids, grid)])

        @pl.when(first)
        def _():
            side.start(si, so, *sems)
        main(*mi, *mo, *ms)

        @pl.when(last)
        def _():
            side.wait(si, so, *sems)

    any_spec = pl.BlockSpec(memory_space=pl.ANY)
    outs = pl.pallas_call(
        kern, name=name, grid=grid, in_specs=list(in_specs) + [any_spec] * n_si,
        out_specs=list(out_specs) + [any_spec] * n_so, out_shape=list(out_shape) + side.out_shapes,
        scratch_shapes=list(scratch_shapes) + side.sem_shapes(),
        compiler_params=_cp(("arbitrary",) * len(grid)))(*args, *side.ins)
    return list(outs[:n_mo]), list(outs[n_mo:])


def modulate(x, sc, sh, name):
    S, D = x.shape
    tm = min(512, S)

    def kern(x_ref, sc_ref, sh_ref, u_ref):
        u_ref[...] = (x_ref[...] * (1.0 + sc_ref[...]) + sh_ref[...]).astype(BF16)

    return pl.pallas_call(
        kern, name=name, grid=(S // tm,),
        in_specs=[pl.BlockSpec((tm, D), lambda i: (i, 0)), _row_spec(D), _row_spec(D)],
        out_specs=pl.BlockSpec((tm, D), lambda i: (i, 0)),
        out_shape=jax.ShapeDtypeStruct((S, D), BF16),
        compiler_params=_cp(("parallel",)),
    )(x, sc, sh)


def silu_rows(c_all, name):
    def kern(c_ref, o_ref):
        c = c_ref[...]
        o_ref[...] = (c * _sigmoid(c)).astype(BF16)

    return pl.pallas_call(kern, name=name, out_shape=jax.ShapeDtypeStruct(c_all.shape, BF16))(c_all)


def sum_over_devices(g, name):
    n, R, C = g.shape

    def kern(g_ref, o_ref):
        acc = g_ref[0]
        for d in range(1, n):
            acc = acc + g_ref[d]
        o_ref[...] = acc

    return pl.pallas_call(kern, name=name, out_shape=jax.ShapeDtypeStruct((R, C), F32))(g)


def _rows_block(R, C, budget=1 << 20):
    if R * C * 4 <= budget or R % 8:
        return R
    tr = max(8, (budget // (C * 4)) // 8 * 8)
    while R % tr:
        tr -= 8
    return tr


def adamw(w, g, m, v, name):
    shape = w.shape
    C = shape[-1]
    R = w.size // C
    w2, g2, m2, v2 = (t.reshape(R, C) for t in (w, g, m, v))
    tr = _rows_block(R, C)
    c1 = 1.0 - ADAM_B1 ** ADAM_STEP
    c2 = 1.0 - ADAM_B2 ** ADAM_STEP

    def kern(w_ref, g_ref, m_ref, v_ref, d_ref, nm_ref, nv_ref):
        gg = g_ref[...]
        nm = ADAM_B1 * m_ref[...] + (1.0 - ADAM_B1) * gg
        nv = ADAM_B2 * v_ref[...] + (1.0 - ADAM_B2) * (gg * gg)
        m_hat = nm / c1
        v_hat = nv / c2
        d_ref[...] = -ADAM_LR * (m_hat / (jnp.sqrt(v_hat) + ADAM_EPS) + ADAM_WD * w_ref[...])
        nm_ref[...] = nm
        nv_ref[...] = nv

    spec = pl.BlockSpec((tr, C), lambda i: (i, 0))
    outs = pl.pallas_call(
        kern, name=name, grid=(R // tr,),
        in_specs=[spec] * 4, out_specs=[spec] * 3,
        out_shape=[jax.ShapeDtypeStruct((R, C), F32)] * 3,
        compiler_params=_cp(("parallel",)),
    )(w2, g2, m2, v2)
    return tuple(o.reshape(shape) for o in outs)


def _tn_for(N):
    for tn in (1024, 768, 640, 512, 384, 256, 128):
        if N % tn == 0:
            return tn
    return N


def mm_plain(a, b3, layer, name, *, mode="f32", nt=False, bias=None, h=None, tm=1024, side=None):
    M, K = a.shape
    N = b3.shape[1] if nt else b3.shape[2]
    if K * N * 2 <= WHOLE_WEIGHT_BYTES:
        tn, tm = N, min(tm, 512 if N > D_MODEL else 1024)
    else:
        tn = _tn_for(N)
    tm = min(tm, M)
    a_spec = pl.BlockSpec((tm, K), lambda j, i: (i, 0))
    if nt:
        b_spec = pl.BlockSpec((None, tn, K), lambda j, i: (layer, j, 0))
    else:
        b_spec = pl.BlockSpec((None, K, tn), lambda j, i: (layer, 0, j))
    o_spec = pl.BlockSpec((tm, tn), lambda j, i: (i, j))
    ins, in_specs = [a, b3], [a_spec, b_spec]
    if bias is not None:
        ins.append(bias)
        in_specs.append(pl.BlockSpec((1, tn), lambda j, i: (0, j)))
    if mode == "mlp_dn":
        ins.append(h)
        in_specs.append(o_spec)
    elif mode not in ("f32", "bf16", "mlp_up"):
        raise ValueError(mode)
    odt = F32 if mode == "f32" else BF16

    def kern(a_ref, b_ref, *rest):
        rest = list(rest)
        bias_ref = rest.pop(0) if bias is not None else None
        h_ref = rest.pop(0) if mode == "mlp_dn" else None
        o_ref = rest.pop(0)
        if nt:
            bt_ref = rest.pop(0)

            @pl.when(pl.program_id(1) == 0)
            def _():
                bt_ref[...] = b_ref[...].T
            acc = _dot(a_ref[...], bt_ref[...])
        else:
            acc = _dot(a_ref[...], b_ref[...].astype(BF16))
        if bias_ref is not None:
            acc = acc + bias_ref[...]
        if mode == "mlp_up":
            r = jnp.maximum(acc, 0.0)
            acc = r * r
        elif mode == "mlp_dn":
            acc = acc * (2.0 * jnp.sqrt(h_ref[...].astype(F32)))
        o_ref[...] = acc.astype(odt)

    outs, landed = hosted_call(
        kern, side, name=name, grid=(N // tn, M // tm), in_specs=in_specs, out_specs=[o_spec],
        out_shape=[jax.ShapeDtypeStruct((M, N), odt)],
        scratch_shapes=[pltpu.VMEM((K, tn), BF16)] if nt else [], dims=("parallel", "arbitrary"), args=tuple(ins))
    return outs[0] if side is None else (outs[0], landed)


def mm_down_ln(a, b3, layer, x_in, gate1p, ln_g, ln_b, sc_next, sh_next, name, *, side=None, tm=512):
    M, K = a.shape
    D = b3.shape[2]
    tm = min(tm, M)

    def kern(a_ref, b_ref, x_ref, gp_ref, lg_ref, lb_ref, sc_ref, sh_ref, y_ref, xo_ref, u_ref):
        y = _dot(a_ref[...], b_ref[...])
        y_ref[...] = y.astype(BF16)
        z = ALPHA * x_ref[...] + gp_ref[...] * y
        mu = jnp.mean(z, axis=-1, keepdims=True)
        zc = z - mu
        var = jnp.mean(zc * zc, axis=-1, keepdims=True)
        xo = (zc * lax.rsqrt(var + LN_EPS)) * lg_ref[...] + lb_ref[...]
        xo_ref[...] = xo
        u_ref[...] = (xo * (1.0 + sc_ref[...]) + sh_ref[...]).astype(BF16)

    tile = pl.BlockSpec((tm, D), lambda i: (i, 0))
    outs, landed = hosted_call(
        kern, side, name=name, grid=(M // tm,),
        in_specs=[pl.BlockSpec((tm, K), lambda i: (i, 0)), pl.BlockSpec((None, K, D), lambda i: (layer, 0, 0)), tile]
        + [_row_spec(D)] * 5,
        out_specs=[tile, tile, tile],
        out_shape=[jax.ShapeDtypeStruct((M, D), BF16), jax.ShapeDtypeStruct((M, D), F32),
                   jax.ShapeDtypeStruct((M, D), BF16)],
        scratch_shapes=[], dims=("parallel",), args=(a, b3, x_in, gate1p, ln_g, ln_b, sc_next, sh_next))
    return tuple(outs), landed


def mm_down_comb(a, b3, layer, dz, x_in, sc1p, name, *, parts=1, ln=None, side=None, tm=512):
    D, K = b3.shape[1], b3.shape[2]
    M = a.shape[-2]
    kp = K // parts
    tm = min(tm, M)
    n_ln = 0 if ln is None else 4

    def kern(*refs):
        a_refs = refs[:parts]
        b_ref, dz_ref, x_ref, sp_ref = refs[parts:parts + 4]
        ln_refs = refs[parts + 4:parts + 4 + n_ln]
        outs = refs[parts + 4 + n_ln:]

        @pl.when(pl.program_id(0) == 0)
        def _():
            for o in outs:
                if o.shape[0] == 8:
                    o[...] = jnp.zeros_like(o)
        if parts == 1:
            du = _dot_nt(a_refs[0][...], b_ref[...])
        else:
            du = _dot_nt(a_refs[0][...], b_ref[:, 0:kp])
            for p in range(1, parts):
                du = du + _dot_nt(a_refs[p][...], b_ref[:, p * kp:(p + 1) * kp])
        dx = ALPHA * dz_ref[...] + du * sp_ref[...]
        if ln is None:
            dx_ref, s_ref = outs
            dx_ref[...] = dx
        else:
            dzl_ref, dyl_ref, s_ref, sl_ref = outs
            _ln_bwd_tile(dx, *ln_refs, dzl_ref, dyl_ref, sl_ref)
        s_ref[0:1, :] += jnp.sum(du * x_ref[...], axis=0, keepdims=True)
        s_ref[1:2, :] += jnp.sum(du, axis=0, keepdims=True)
        if parts > 1:
            for p in range(parts):
                s_ref[2 + p:3 + p, :] += jnp.sum(a_refs[p][...].astype(F32), axis=0, keepdims=True)

    tile = pl.BlockSpec((tm, D), lambda i: (i, 0))
    sums = pl.BlockSpec((8, D), lambda i: (0, 0))
    if parts == 1:
        a_ins, a_specs = [a], [pl.BlockSpec((tm, K), lambda i: (i, 0))]
    else:
        assert kp == D and parts <= 6
        a_ins = [a] * parts
        a_specs = [pl.BlockSpec((None, tm, kp), functools.partial(lambda i, p: (p, i, 0), p=p)) for p in range(parts)]
    in_specs = a_specs + [pl.BlockSpec((None, D, K), lambda i: (layer, 0, 0)), tile, tile, _row_spec(D)]
    args = a_ins + [b3, dz, x_in, sc1p]
    if ln is None:
        out_specs = [tile, sums]
        out_shape = [jax.ShapeDtypeStruct((M, D), F32), jax.ShapeDtypeStruct((8, D), F32)]
    else:
        in_specs += [tile, tile, _row_spec(D), _row_spec(D)]
        args += list(ln)
        out_specs = [tile, tile, sums, sums]
        out_shape = [jax.ShapeDtypeStruct((M, D), F32), jax.ShapeDtypeStruct((M, D), BF16),
                     jax.ShapeDtypeStruct((8, D), F32), jax.ShapeDtypeStruct((8, D), F32)]
    return hosted_call(kern, side, name=name, grid=(M // tm,), in_specs=in_specs, out_specs=out_specs,
                       out_shape=out_shape, scratch_shapes=[], dims=("arbitrary",), args=tuple(args))


def mm_w(a, b, name, *, ts=2048, tk=512, chips_out=False, b_parts=1, tn=None):
    S, K = a.shape
    npart = b.shape[-1]
    N = npart * b_parts
    ts = min(ts, S)
    tk = min(tk, K)
    n_chip = N // N_CHIPS
    if tn is None:
        tn = _tn_for(n_chip if chips_out else npart)
    assert npart % tn == 0 and (not chips_out or n_chip % tn == 0)

    def kern(a_ref, b_ref, o_ref):
        @pl.when(pl.program_id(2) == 0)
        def _():
            o_ref[...] = jnp.zeros_like(o_ref)
        o_ref[...] += _dot_tn(a_ref[...], b_ref[...])

    if b_parts == 1:
        b_spec = pl.BlockSpec((ts, tn), lambda k, n, s: (s, n))
    else:
        per = npart // tn
        b_spec = pl.BlockSpec((None, ts, tn), lambda k, n, s: (n // per, s, n % per))
    if chips_out:
        per_chip = n_chip // tn
        o_spec = pl.BlockSpec((None, tk, tn), lambda k, n, s: (n // per_chip, k, n % per_chip))
        out_shape = jax.ShapeDtypeStruct((N_CHIPS, K, n_chip), F32)
    else:
        o_spec = pl.BlockSpec((tk, tn), lambda k, n, s: (k, n))
        out_shape = jax.ShapeDtypeStruct((K, N), F32)
    return pl.pallas_call(
        kern, name=name, grid=(K // tk, N // tn, S // ts),
        in_specs=[pl.BlockSpec((ts, tk), lambda k, n, s: (s, k)), b_spec],
        out_specs=o_spec, out_shape=out_shape,
        compiler_params=_cp(("parallel", "parallel", "arbitrary")),
    )(a, b)


def mm_w_chips3(a, b3, name, *, ts=512):
    S, K = a.shape
    P = b3.shape[2]
    n_chip = 3 * P // N_CHIPS
    ts = min(ts, S)
    pieces = []
    for chip in range(N_CHIPS):
        lo, hi = chip * n_chip, (chip + 1) * n_chip
        while lo < hi:
            part = lo // P
            w = min(hi, (part + 1) * P) - lo
            pieces.append((chip, lo - chip * n_chip, part, lo - part * P, w))
            lo += w

    def kern(a_ref, b_ref, o_ref):
        @pl.when(pl.program_id(0) == 0)
        def _():
            o_ref[...] = jnp.zeros_like(o_ref)
        at = a_ref[...].T
        for chip, oc, part, pc, w in pieces:
            o_ref[chip, :, oc:oc + w] += _dot(at, b_ref[part, :, pc:pc + w])

    return pl.pallas_call(
        kern, name=name, grid=(S // ts,),
        in_specs=[pl.BlockSpec((ts, K), lambda s: (s, 0)), pl.BlockSpec((3, ts, P), lambda s: (0, s, 0))],
        out_specs=pl.BlockSpec((N_CHIPS, K, n_chip), lambda s: (0, 0, 0)),
        out_shape=jax.ShapeDtypeStruct((N_CHIPS, K, n_chip), F32),
        compiler_params=_cp(("arbitrary",)),
    )(a, b3)


def mm_w_res(a, b, name, *, chips_out=False, ts=512):
    S, K = a.shape
    N = b.shape[1]
    ts = min(ts, S)
    n_chip = N // N_CHIPS

    def kern(a_ref, b_ref, o_ref):
        @pl.when(pl.program_id(0) == 0)
        def _():
            o_ref[...] = jnp.zeros_like(o_ref)
        at = a_ref[...].T
        if chips_out:
            for chip in range(N_CHIPS):
                o_ref[chip] += _dot(at, b_ref[:, chip * n_chip:(chip + 1) * n_chip])
        else:
            o_ref[...] += _dot(at, b_ref[...])

    o_shape = (N_CHIPS, K, n_chip) if chips_out else (K, N)
    return pl.pallas_call(
        kern, name=name, grid=(S // ts,),
        in_specs=[pl.BlockSpec((ts, K), lambda s: (s, 0)), pl.BlockSpec((ts, N), lambda s: (s, 0))],
        out_specs=pl.BlockSpec(o_shape, lambda s: (0,) * len(o_shape)),
        out_shape=jax.ShapeDtypeStruct(o_shape, F32),
        compiler_params=_cp(("arbitrary",)),
    )(a, b)


def mm_f32(a, b, name):
    def kern(a_ref, b_ref, o_ref):
        o_ref[...] = jnp.dot(a_ref[...], b_ref[...], precision=HIGHEST, preferred_element_type=F32)

    return pl.pallas_call(kern, name=name, out_shape=jax.ShapeDtypeStruct((a.shape[0], b.shape[1]), F32),
                          compiler_params=pltpu.CompilerParams(vmem_limit_bytes=VMEM_LIMIT))(a, b)


def _ln_bwd_tile(dxo_t, x_ref, y_ref, gp_ref, lg_ref, dz_ref, dy_ref, s_ref):
    yv = y_ref[...].astype(F32)
    z = ALPHA * x_ref[...] + gp_ref[...] * yv
    mu = jnp.mean(z, axis=-1, keepdims=True)
    zc = z - mu
    var = jnp.mean(zc * zc, axis=-1, keepdims=True)
    rstd = lax.rsqrt(var + LN_EPS)
    xhat = zc * rstd
    dxh = dxo_t * lg_ref[...]
    dz = rstd * (dxh - jnp.mean(dxh, axis=-1, keepdims=True)
                 - xhat * jnp.mean(dxh * xhat, axis=-1, keepdims=True))
    dz_ref[...] = dz
    dy_ref[...] = (gp_ref[...] * dz).astype(BF16)
    s_ref[0:1, :] += jnp.sum(dxo_t * xhat, axis=0, keepdims=True)
    s_ref[1:2, :] += jnp.sum(dxo_t, axis=0, keepdims=True)
    s_ref[2:3, :] += jnp.sum(dz * yv, axis=0, keepdims=True)


def loss_ln_bwd(x_out, target, x_in, y, gate1p, ln_g, name, *, tm=256):
    S, D = x_out.shape
    tm = min(tm, S)

    def kern(xo_ref, t_ref, x_ref, y_ref, gp_ref, lg_ref, dz_ref, dy_ref, s_ref, l_ref):
        @pl.when(pl.program_id(0) == 0)
        def _():
            s_ref[...] = jnp.zeros_like(s_ref)
            l_ref[...] = jnp.zeros_like(l_ref)
        e = xo_ref[...] - t_ref[...]
        l_ref[...] += jnp.sum(e * e)
        _ln_bwd_tile(e * (1.0 / D), x_ref, y_ref, gp_ref, lg_ref, dz_ref, dy_ref, s_ref)

    tile = pl.BlockSpec((tm, D), lambda i: (i, 0))
    return pl.pallas_call(
        kern, name=name, grid=(S // tm,),
        in_specs=[tile, tile, tile, tile, _row_spec(D), _row_spec(D)],
        out_specs=[tile, tile, pl.BlockSpec((8, D), lambda i: (0, 0)), pl.BlockSpec((8, 128), lambda i: (0, 0))],
        out_shape=[jax.ShapeDtypeStruct((S, D), F32), jax.ShapeDtypeStruct((S, D), BF16),
                   jax.ShapeDtypeStruct((8, D), F32), jax.ShapeDtypeStruct((8, 128), F32)],
        compiler_params=_cp(("arbitrary",)),
    )(x_out, target, x_in, y, gate1p, ln_g)


def _tri64():
    r = lax.broadcasted_iota(jnp.int32, (CHUNK, CHUNK), 0)
    c = lax.broadcasted_iota(jnp.int32, (CHUNK, CHUNK), 1)
    return r >= c


def _gla_chunk_common(proj_ref, rows, b, h):
    kc = slice(h * GLA_DK_HEAD, (h + 1) * GLA_DK_HEAD)
    bh = b[:, kc]
    ep = jnp.exp(bh)
    en = jnp.exp(-bh)
    bl = bh[CHUNK - 1:CHUNK, :]
    ee = jnp.exp(bl - bh)
    dec = jnp.exp(bl)
    q = proj_ref[rows, h * GLA_DK_HEAD:(h + 1) * GLA_DK_HEAD] * GLA_SCALE
    k = proj_ref[rows, GLA_DK + h * GLA_DK_HEAD:GLA_DK + (h + 1) * GLA_DK_HEAD]
    v = proj_ref[rows, 2 * GLA_DK + h * GLA_DV_HEAD:2 * GLA_DK + (h + 1) * GLA_DV_HEAD]
    g = proj_ref[rows, 2 * GLA_DK + GLA_DV + h * GLA_DV_HEAD:2 * GLA_DK + GLA_DV + (h + 1) * GLA_DV_HEAD]
    return ep, en, ee, dec, q, k, v, g


def gla_fwd(proj, wgk_p, bgk, gnorm, name, side=None):
    S = proj.shape[0]
    TB = min(GLA_TB, S)
    ncb = TB // CHUNK

    def kern(proj_ref, wgk_ref, bgk_ref, gn_ref, zg_ref, st_ref, state_scr, la_scr):
        @pl.when(pl.program_id(0) == 0)
        def _():
            state_scr[...] = jnp.zeros_like(state_scr)
        lr = proj_ref[:, GLA_LR_OFF:GLA_IN_PAD].astype(BF16)
        gk = _dot(lr, wgk_ref[...]) + bgk_ref[...]
        la_scr[...] = _log_sigmoid(gk) * (1.0 / 16.0)
        lower = _tri64()
        tri = lower.astype(F32)

        def group(gi, carry):
            GG = min(GLA_GROUP_FWD, ncb)
            rows = [pl.ds(pl.multiple_of((gi * GG + g) * CHUNK, CHUNK), CHUNK) for g in range(GG)]
            b = [jnp.dot(tri, la_scr[r, :], precision=HIGHEST, preferred_element_type=F32) for r in rows]
            P = [(g, h) for g in range(GG) for h in range(GLA_HEADS)]
            cm = {p: _gla_chunk_common(proj_ref, rows[p[0]], b[p[0]], p[1]) for p in P}
            qf = {p: (cm[p][4] * cm[p][0]).astype(BF16) for p in P}
            kn = {p: (cm[p][5] * cm[p][1]).astype(BF16) for p in P}
            qn = {p: (cm[p][4] * cm[p][1]).astype(BF16) for p in P}
            kp = {p: (cm[p][5] * cm[p][0]).astype(BF16) for p in P}
            ke = {p: (cm[p][5] * cm[p][2]).astype(BF16) for p in P}
            vb = {p: cm[p][6].astype(BF16) for p in P}
            a_f = {p: _dot_nt(qf[p], kn[p]) for p in P}
            a_b = {p: _dot_nt(qn[p], kp[p]) for p in P}
            upd = {p: _dot_tn(vb[p], ke[p]) for p in P}
            st = {(0, h): state_scr[h] for h in range(GLA_HEADS)}
            for g in range(GG):
                for h in range(GLA_HEADS):
                    st[(g + 1, h)] = st[(g, h)] * cm[(g, h)][3] + upd[(g, h)]
            o_st = {p: _dot_nt(qf[p], st[p].astype(BF16)) for p in P}
            amat = {p: jnp.where(lower, a_f[p], a_b[p]).astype(BF16) for p in P}
            o = {p: _dot(amat[p], vb[p]) + o_st[p] for p in P}
            for g, h in P:
                st_ref[gi * GG + g, h] = st[(g, h)]
            for h in range(GLA_HEADS):
                state_scr[h] = st[(GG, h)]
            for g, h in P:
                gate = cm[(g, h)][7]
                vc = slice(h * GLA_DV_HEAD, (h + 1) * GLA_DV_HEAD)
                r = lax.rsqrt(jnp.mean(o[(g, h)] * o[(g, h)], axis=-1, keepdims=True) + RMS_EPS)
                on = (o[(g, h)] * r) * gn_ref[:, vc]
                zg_ref[rows[g], vc] = (on * (gate * _sigmoid(gate))).astype(BF16)
            return carry

        lax.fori_loop(0, ncb // min(GLA_GROUP_FWD, ncb), group, 0)

    return hosted_call(
        kern, side, name=name, grid=(S // TB,),
        in_specs=[pl.BlockSpec((TB, GLA_IN_PAD), lambda i: (i, 0)),
                  pl.BlockSpec((128, GLA_DK), lambda i: (0, 0)), _row_spec(GLA_DK), _row_spec(GLA_DV)],
        out_specs=[pl.BlockSpec((TB, GLA_DV), lambda i: (i, 0)),
                   pl.BlockSpec((ncb, GLA_HEADS, GLA_DV_HEAD, GLA_DK_HEAD), lambda i: (i, 0, 0, 0))],
        out_shape=[jax.ShapeDtypeStruct((S, GLA_DV), BF16),
                   jax.ShapeDtypeStruct((S // CHUNK, GLA_HEADS, GLA_DV_HEAD, GLA_DK_HEAD), F32)],
        scratch_shapes=[pltpu.VMEM((GLA_HEADS, GLA_DV_HEAD, GLA_DK_HEAD), F32), pltpu.VMEM((TB, GLA_DK), F32)],
        dims=("arbitrary",), args=(proj, wgk_p, bgk, gnorm))


def gla_bwd(proj, states, dzg, wgk_p, bgk, gnorm, name, side=None):
    S = proj.shape[0]
    TB = min(GLA_TB, S)
    ncb = TB // CHUNK
    nb = S // TB

    def kern(proj_ref, st_ref, dzg_ref, wgk_ref, bgk_ref, gn_ref,
             dproj_ref, dwgk_ref, dbgk_ref, dgn_ref, dstate_scr, la_scr, gk_scr, dgk_scr):
        @pl.when(pl.program_id(0) == 0)
        def _():
            dstate_scr[...] = jnp.zeros_like(dstate_scr)
            dwgk_ref[...] = jnp.zeros_like(dwgk_ref)
            dbgk_ref[...] = jnp.zeros_like(dbgk_ref)
            dgn_ref[...] = jnp.zeros_like(dgn_ref)
        lr = proj_ref[:, GLA_LR_OFF:GLA_IN_PAD].astype(BF16)
        gk = _dot(lr, wgk_ref[...]) + bgk_ref[...]
        gk_scr[...] = gk
        la_scr[...] = _log_sigmoid(gk) * (1.0 / 16.0)
        lower = _tri64()
        tri = lower.astype(F32)
        r_i = lax.broadcasted_iota(jnp.int32, (CHUNK, CHUNK), 0)
        c_i = lax.broadcasted_iota(jnp.int32, (CHUNK, CHUNK), 1)
        triu = (c_i >= r_i).astype(F32)
        last_row = lax.broadcasted_iota(jnp.int32, (CHUNK, GLA_DK_HEAD), 0) == CHUNK - 1

        def group(gi, carry):
            cs = [ncb - 1 - (gi * GLA_GROUP + g) for g in range(GLA_GROUP)]
            rows = [pl.ds(pl.multiple_of(c * CHUNK, CHUNK), CHUNK) for c in cs]
            b = [jnp.dot(tri, la_scr[r, :], precision=HIGHEST, preferred_element_type=F32) for r in rows]
            P = [(g, h) for g in range(GLA_GROUP) for h in range(GLA_HEADS)]
            kcs = [slice(h * GLA_DK_HEAD, (h + 1) * GLA_DK_HEAD) for h in range(GLA_HEADS)]
            vcs = [slice(h * GLA_DV_HEAD, (h + 1) * GLA_DV_HEAD) for h in range(GLA_HEADS)]
            cm = {p: _gla_chunk_common(proj_ref, rows[p[0]], b[p[0]], p[1]) for p in P}
            ep, en, ee, dec = ({p: cm[p][i] for p in P} for i in range(4))
            qf = {p: cm[p][4] * cm[p][0] for p in P}
            kn = {p: cm[p][5] * cm[p][1] for p in P}
            qn = {p: cm[p][4] * cm[p][1] for p in P}
            kp = {p: cm[p][5] * cm[p][0] for p in P}
            ke = {p: cm[p][5] * cm[p][2] for p in P}
            qf_b, kn_b, qn_b, kp_b, ke_b = ({p: t[p].astype(BF16) for p in P} for t in (qf, kn, qn, kp, ke))
            vb = {p: cm[p][6].astype(BF16) for p in P}
            st = {p: st_ref[cs[p[0]], p[1]] for p in P}
            st_b = {p: st[p].astype(BF16) for p in P}
            a_f = {p: _dot_nt(qf_b[p], kn_b[p]) for p in P}
            a_b = {p: _dot_nt(qn_b[p], kp_b[p]) for p in P}
            o_st = {p: _dot_nt(qf_b[p], st_b[p]) for p in P}
            amat = {p: jnp.where(lower, a_f[p], a_b[p]).astype(BF16) for p in P}
            o = {p: _dot(amat[p], vb[p]) + o_st[p] for p in P}
            do_b, dgs = {}, {}
            for p in P:
                g, h = p
                r = lax.rsqrt(jnp.mean(o[p] * o[p], axis=-1, keepdims=True) + RMS_EPS)
                oh = o[p] * r
                gn = gn_ref[:, vcs[h]]
                gate = cm[p][7]
                sg = _sigmoid(gate)
                dz = dzg_ref[rows[g], vcs[h]]
                don = dz * (gate * sg)
                dgs[p] = dz * (oh * gn) * (sg * (1.0 + gate * (1.0 - sg)))
                dgn_ref[:, vcs[h]] += jnp.sum(don * oh, axis=0, keepdims=True)
                doh = don * gn
                do_b[p] = (r * (doh - oh * jnp.mean(doh * oh, axis=-1, keepdims=True))).astype(BF16)
            da = {p: _dot_nt(do_b[p], vb[p]) for p in P}
            dv_a = {p: _dot_tn(amat[p], do_b[p]) for p in P}
            dqf_st = {p: _dot(do_b[p], st_b[p]) for p in P}
            dst_upd = {p: _dot_tn(do_b[p], qf_b[p]) for p in P}
            dst = {(0, h): dstate_scr[h] for h in range(GLA_HEADS)}
            for g in range(GLA_GROUP):
                for h in range(GLA_HEADS):
                    dst[(g + 1, h)] = dst[(g, h)] * dec[(g, h)] + dst_upd[(g, h)]
            for h in range(GLA_HEADS):
                dstate_scr[h] = dst[(GLA_GROUP, h)]
            dst_b = {p: dst[p].astype(BF16) for p in P}
            dv = {p: dv_a[p] + _dot_nt(ke_b[p], dst_b[p]) for p in P}
            dke = {p: _dot(vb[p], dst_b[p]) for p in P}
            da_f = {p: jnp.where(lower, da[p], 0.0).astype(BF16) for p in P}
            da_b = {p: jnp.where(lower, 0.0, da[p]).astype(BF16) for p in P}
            dqf = {p: _dot(da_f[p], kn_b[p]) + dqf_st[p] for p in P}
            dkn = {p: _dot_tn(da_f[p], qf_b[p]) for p in P}
            dqn = {p: _dot(da_b[p], kp_b[p]) for p in P}
            dkp = {p: _dot_tn(da_b[p], qn_b[p]) for p in P}
            dbs = {}
            for p in P:
                ddec = jnp.sum(dst[p] * st[p], axis=0, keepdims=True)
                db = dqf[p] * qf[p] - dkn[p] * kn[p] - dqn[p] * qn[p] + dkp[p] * kp[p] - dke[p] * ke[p]
                dbl = jnp.sum(dke[p] * ke[p], axis=0, keepdims=True) + ddec * dec[p]
                dbs[p] = db + jnp.where(last_row, dbl, 0.0)
            dla = {p: jnp.dot(triu, dbs[p], precision=HIGHEST, preferred_element_type=F32) for p in P}
            for p in P:
                g, h = p
                dq = (dqf[p] * ep[p] + dqn[p] * en[p]) * GLA_SCALE
                dk = dkn[p] * en[p] + dkp[p] * ep[p] + dke[p] * ee[p]
                dgk_scr[rows[g], kcs[h]] = dla[p] * (1.0 / 16.0) * _sigmoid(-gk_scr[rows[g], kcs[h]])
                dproj_ref[rows[g], kcs[h]] = dq.astype(BF16)
                dproj_ref[rows[g], GLA_DK + h * GLA_DK_HEAD:GLA_DK + (h + 1) * GLA_DK_HEAD] = dk.astype(BF16)
                dproj_ref[rows[g], 2 * GLA_DK + h * GLA_DV_HEAD:2 * GLA_DK + (h + 1) * GLA_DV_HEAD] = dv[p].astype(BF16)
                dproj_ref[rows[g], 2 * GLA_DK + GLA_DV + h * GLA_DV_HEAD:
                          2 * GLA_DK + GLA_DV + (h + 1) * GLA_DV_HEAD] = dgs[p].astype(BF16)
            return carry

        lax.fori_loop(0, ncb // GLA_GROUP, group, 0)
        dgk = dgk_scr[...]
        dgk_b = dgk.astype(BF16)
        dproj_ref[:, GLA_LR_OFF:GLA_IN_PAD] = _dot_nt(dgk_b, wgk_ref[...]).astype(BF16)
        dwgk_ref[...] += _dot_tn(lr, dgk_b)
        dbgk_ref[...] += jnp.sum(dgk, axis=0, keepdims=True)

    rev = lambda i: (nb - 1 - i, 0)
    return hosted_call(
        kern, side, name=name, grid=(nb,),
        in_specs=[pl.BlockSpec((TB, GLA_IN_PAD), rev),
                  pl.BlockSpec((ncb, GLA_HEADS, GLA_DV_HEAD, GLA_DK_HEAD), lambda i: (nb - 1 - i, 0, 0, 0)),
                  pl.BlockSpec((TB, GLA_DV), rev),
                  pl.BlockSpec((128, GLA_DK), lambda i: (0, 0)), _row_spec(GLA_DK), _row_spec(GLA_DV)],
        out_specs=[pl.BlockSpec((TB, GLA_IN_PAD), rev),
                   pl.BlockSpec((128, GLA_DK), lambda i: (0, 0)), _row_spec(GLA_DK), _row_spec(GLA_DV)],
        out_shape=[jax.ShapeDtypeStruct((S, GLA_IN_PAD), BF16), jax.ShapeDtypeStruct((128, GLA_DK), F32),
                   jax.ShapeDtypeStruct((1, GLA_DK), F32), jax.ShapeDtypeStruct((1, GLA_DV), F32)],
        scratch_shapes=[pltpu.VMEM((GLA_HEADS, GLA_DV_HEAD, GLA_DK_HEAD), F32), pltpu.VMEM((TB, GLA_DK), F32),
                        pltpu.VMEM((TB, GLA_DK), F32), pltpu.VMEM((TB, GLA_DK), F32)],
        dims=("arbitrary",), args=(proj, states, dzg, wgk_p, bgk, gnorm))


ATT_TW = 1024
ATT_CLASSES = 3


def _att_window(i):
    return pl.multiple_of(jnp.maximum(i * ATT_TQ - LEFT_CHUNKS * CHUNK, 0), ATT_TQ)


def _att_rel_index():
    e = jnp.arange(ATT_TW)[None, :]
    d = jnp.where(e < ATT_KW, e, e - ATT_TW)
    off = (jnp.arange(ATT_CLASSES) * ATT_TQ)[:, None]
    return jnp.clip(off - d, -MAX_REL, MAX_REL) + MAX_REL


def _row_bits():
    return lax.broadcasted_iota(jnp.int32, (ATT_TQ, ATT_TW), 0)


def att_bias_tiles(rel_bias, name):
    pick = (jnp.arange(384)[:, None] == _att_rel_index().reshape(-1)[None, :]).astype(F32)
    tab = mm_f32(jnp.pad(rel_bias, ((0, 0), (0, 384 - N_REL))), pick, name + "_tab")
    tab = tab.reshape(ATT_HEADS * ATT_CLASSES, 1, ATT_TW)

    def kern(t_ref, o_ref):
        cls = pl.program_id(0) % ATT_CLASSES
        x = jnp.broadcast_to(t_ref[...], (ATT_TQ, ATT_TW))
        x = pltpu.roll(x, 0, 1, stride=1, stride_axis=0)
        x = x[:, :ATT_KW]
        qc = cls * (ATT_TQ // CHUNK) + lax.shift_right_arithmetic(
            lax.broadcasted_iota(jnp.int32, (ATT_TQ, ATT_KW), 0), 6)
        kc = lax.shift_right_arithmetic(lax.broadcasted_iota(jnp.int32, (ATT_TQ, ATT_KW), 1), 6)
        o_ref[...] = jnp.where((kc <= qc) & (kc >= qc - LEFT_CHUNKS), x, NEG_INF)

    return pl.pallas_call(
        kern, name=name, grid=(ATT_HEADS * ATT_CLASSES,),
        in_specs=[pl.BlockSpec((None, 1, ATT_TW), lambda i: (i, 0, 0))],
        out_specs=pl.BlockSpec((None, ATT_TQ, ATT_KW), lambda i: (i, 0, 0)),
        out_shape=jax.ShapeDtypeStruct((ATT_HEADS * ATT_CLASSES, ATT_TQ, ATT_KW), F32),
        compiler_params=_cp(("parallel",)),
    )(tab)


def att_bias_grad(dbt, name):
    def kern(d_ref, o_ref):
        x = jnp.concatenate([d_ref[...], jnp.zeros((ATT_TQ, ATT_TW - ATT_KW), F32)], axis=1)
        x = jnp.concatenate([pltpu.roll(x[r0:r0 + 8, :], (ATT_TW - r0) % ATT_TW, axis=1)
                             for r0 in range(0, ATT_TQ, 8)], axis=0)
        row = _row_bits()
        for b in range(3):
            x = jnp.where((row & (1 << b)) != 0, pltpu.roll(x, ATT_TW - (1 << b), axis=1), x)
        o_ref[...] = jnp.sum(x, axis=0, keepdims=True)

    diag = pl.pallas_call(
        kern, name=name + "_diag", grid=(ATT_HEADS * ATT_CLASSES,),
        in_specs=[pl.BlockSpec((None, ATT_TQ, ATT_KW), lambda i: (i, 0, 0))],
        out_specs=pl.BlockSpec((None, 1, ATT_TW), lambda i: (i, 0, 0)),
        out_shape=jax.ShapeDtypeStruct((ATT_HEADS * ATT_CLASSES, 1, ATT_TW), F32),
        compiler_params=_cp(("parallel",)),
    )(dbt)
    diag = diag.reshape(ATT_HEADS, ATT_CLASSES * ATT_TW)
    onehot = (_att_rel_index().reshape(-1)[:, None] == jnp.arange(384)[None, :]).astype(F32)
    return mm_f32(diag, onehot, name + "_bins")[:, :N_REL]


ATT_GROUP = 2


def _att_scores(q_ref, k_ref, bias_refs, blk0):
    G = range(ATT_GROUP)
    hs = [slice(hh * ATT_HD, (hh + 1) * ATT_HD) for hh in range(2)]
    rows = [slice(g * ATT_TQ, (g + 1) * ATT_TQ) for g in G]
    wins = [pl.ds(_att_window(blk0 + g), ATT_KW) for g in G]
    kw = [k_ref[w, :] for w in wins]
    P = [(g, hh) for g in G for hh in range(2)]
    q = {p: q_ref[rows[p[0]], hs[p[1]]] * ATT_SCALE for p in P}
    k = {p: kw[p[0]][:, hs[p[1]]] for p in P}
    s = {p: _dot_nt(q[p], k[p]) + bias_refs[p[0]][p[1]] for p in P}
    e = {p: jnp.exp(s[p] - jnp.max(s[p], axis=-1, keepdims=True)) for p in P}
    inv = {p: 1.0 / jnp.sum(e[p], axis=-1, keepdims=True) for p in P}
    return P, rows, wins, hs, q, k, e, inv


def _att_specs(S):
    nq = D_MODEL // 128
    q_spec = pl.BlockSpec((ATT_GROUP * ATT_TQ, 128), lambda p, i: (i, p))
    k_spec = pl.BlockSpec((S, 128), lambda p, i: (0, nq + p))
    v_spec = pl.BlockSpec((S, 128), lambda p, i: (0, 2 * nq + p))
    b_specs = [pl.BlockSpec((2, None, ATT_TQ, ATT_KW),
                            functools.partial(lambda p, i, g: (p, jnp.minimum(ATT_GROUP * i + g, ATT_CLASSES - 1), 0, 0), g=g))
               for g in range(ATT_GROUP)]
    return q_spec, k_spec, v_spec, b_specs


def attn_fwd(qkv, bias, name, side=None):
    S = qkv.shape[0]
    q_spec, k_spec, v_spec, b_specs = _att_specs(S)

    def kern(q_ref, k_ref, v_ref, *rest):
        bias_refs, o_ref = rest[:ATT_GROUP], rest[ATT_GROUP]
        P, rows, wins, hs, _, _, e, inv = _att_scores(q_ref, k_ref, bias_refs, ATT_GROUP * pl.program_id(1))
        vw = [v_ref[w, :] for w in wins]
        o = {p: _dot(e[p].astype(BF16), vw[p[0]][:, hs[p[1]]]) * inv[p] for p in P}
        for g in range(ATT_GROUP):
            o_ref[rows[g], :] = jnp.concatenate([o[(g, 0)], o[(g, 1)]], axis=1).astype(BF16)

    return hosted_call(
        kern, side, name=name, grid=(ATT_HEADS // 2, S // (ATT_GROUP * ATT_TQ)),
        in_specs=[q_spec, k_spec, v_spec] + b_specs,
        out_specs=[pl.BlockSpec((ATT_GROUP * ATT_TQ, 128), lambda p, i: (i, p))],
        out_shape=[jax.ShapeDtypeStruct((S, D_MODEL), BF16)],
        scratch_shapes=[], dims=("parallel", "arbitrary"), args=(qkv, qkv, qkv) + (bias,) * ATT_GROUP)


def attn_bwd(qkv, bias, do, name, side=None):
    S = qkv.shape[0]
    nstep = S // (ATT_GROUP * ATT_TQ)
    q_spec, k_spec, v_spec, b_specs = _att_specs(S)

    def kern(q_ref, k_ref, v_ref, *rest):
        bias_refs = rest[:ATT_GROUP]
        do_ref, dqkv_ref, db_ref, dk_scr, dv_scr = rest[ATT_GROUP:]
        i = pl.program_id(1)

        @pl.when(i == 0)
        def _():
            dk_scr[...] = jnp.zeros_like(dk_scr)
            dv_scr[...] = jnp.zeros_like(dv_scr)
            db_ref[...] = jnp.zeros_like(db_ref)
        blk0 = ATT_GROUP * i
        P, rows, wins, hs, q, k, e, inv = _att_scores(q_ref, k_ref, bias_refs, blk0)
        vw = [v_ref[w, :] for w in wins]
        do_h = {p: do_ref[rows[p[0]], hs[p[1]]] for p in P}
        dp = {p: _dot_nt(do_h[p], vw[p[0]][:, hs[p[1]]]) for p in P}
        pr = {p: e[p] * inv[p] for p in P}
        dvs = {p: _dot_tn(pr[p].astype(BF16), do_h[p]) for p in P}
        ds = {p: pr[p] * (dp[p] - jnp.sum(pr[p] * dp[p], axis=-1, keepdims=True)) for p in P}
        ds_b = {p: ds[p].astype(BF16) for p in P}
        dqs = {p: _dot(ds_b[p], k[p]) * ATT_SCALE for p in P}
        dks = {p: _dot_tn(ds_b[p], q[p]) for p in P}
        for g, hh in P:
            db_ref[hh, jnp.minimum(blk0 + g, ATT_CLASSES - 1)] += ds[(g, hh)]
        for g in range(ATT_GROUP):
            first = pl.multiple_of((blk0 + g) * ATT_TQ, ATT_TQ)
            dqkv_ref[0, pl.ds(first, ATT_TQ), :] = jnp.concatenate([dqs[(g, 0)], dqs[(g, 1)]], axis=1).astype(BF16)
            dk_scr[wins[g], :] += jnp.concatenate([dks[(g, 0)], dks[(g, 1)]], axis=1)
            dv_scr[wins[g], :] += jnp.concatenate([dvs[(g, 0)], dvs[(g, 1)]], axis=1)

        @pl.when(i == nstep - 1)
        def _():
            dqkv_ref[1] = dk_scr[...].astype(BF16)
            dqkv_ref[2] = dv_scr[...].astype(BF16)

    return hosted_call(
        kern, side, name=name, grid=(ATT_HEADS // 2, nstep),
        in_specs=[q_spec, k_spec, v_spec] + b_specs + [pl.BlockSpec((ATT_GROUP * ATT_TQ, 128), lambda p, i: (i, p))],
        out_specs=[pl.BlockSpec((3, S, 128), lambda p, i: (0, 0, p)),
                   pl.BlockSpec((2, ATT_CLASSES, ATT_TQ, ATT_KW), lambda p, i: (p, 0, 0, 0))],
        out_shape=[jax.ShapeDtypeStruct((3, S, D_MODEL), BF16),
                   jax.ShapeDtypeStruct((ATT_HEADS, ATT_CLASSES, ATT_TQ, ATT_KW), F32)],
        scratch_shapes=[pltpu.VMEM((S, 128), F32), pltpu.VMEM((S, 128), F32)],
        dims=("parallel", "arbitrary"), args=(qkv, qkv, qkv) + (bias,) * ATT_GROUP + (do,))


def _me():
    return lax.axis_index("x"), lax.axis_index("y"), lax.axis_index("c")


def _other_chips(x, y):
    return [(1 - x, y), (x, 1 - y), (1 - x, 1 - y)]


def all_gather8(x_shard, name):
    m_per, n = x_shard.shape

    def body(x_ref, out_ref, send_sems, recv_sems, local_sem):
        x, y, c = _me()
        me, sibling = (x, y, c), (x, y, 1 - c)
        chips = _other_chips(x, y)

        def rows(px, py, pc):
            return out_ref.at[pl.ds((4 * px + 2 * py + pc) * m_per, m_per), :]

        def copy(k, block, to, src=None):
            return pltpu.make_async_remote_copy(
                src_ref=rows(*block) if src is None else src, dst_ref=rows(*block),
                send_sem=send_sems.at[k], recv_sem=recv_sems.at[k], device_id=to, device_id_type=MESH)

        mine = pltpu.make_async_copy(x_ref, rows(*me), local_sem)
        mine.start()
        first = [copy(0, me, sibling, src=x_ref)]
        first += [copy(1 + j, me, (*chip, c), src=x_ref) for j, chip in enumerate(chips)]
        for cp in first:
            cp.start()
        passed = [copy(4 + j, (*chip, c), sibling) for j, chip in enumerate(chips)]
        for j, chip in enumerate(chips):
            copy(1 + j, (*chip, c), me).wait_recv()
            passed[j].start()
        copy(0, sibling, me).wait_recv()
        for j, chip in enumerate(chips):
            copy(4 + j, (*chip, 1 - c), me).wait_recv()
        for cp in first + passed:
            cp.wait_send()
        mine.wait()

    return pl.pallas_call(
        body, name=name,
        out_shape=jax.ShapeDtypeStruct((N_DEV * m_per, n), x_shard.dtype),
        in_specs=[pl.BlockSpec(memory_space=pltpu.VMEM)],
        out_specs=pl.BlockSpec(memory_space=pltpu.VMEM),
        scratch_shapes=[pltpu.SemaphoreType.DMA((7,)), pltpu.SemaphoreType.DMA((7,)), pltpu.SemaphoreType.DMA],
        compiler_params=pltpu.CompilerParams(vmem_limit_bytes=VMEM_LIMIT),
    )(x_shard)


def _half_rows(n_rows, c):
    h = n_rows // 2
    return pl.ds(c * h, h)


def _gathered_shape(shape, flavour):
    L, a, b = shape
    return {"col": (L, a, N_CHIPS * b), "row": (L, N_CHIPS * a, b), "lead": (N_CHIPS, L, a, b)}[flavour]


def _gathered_part(out_ref, shape, flavour, s, rows):
    L, a, b = shape
    if flavour == "col":
        return out_ref.at[:, rows, pl.ds(s * b, b)]
    if flavour == "row":
        return out_ref.at[:, pl.ds(s * a + rows.start, rows.size), :]
    return out_ref.at[s, :, rows, :]


def gather_side(shards, flavours):
    n = len(shards)
    shapes = [w.shape for w in shards]

    def copies(w_refs, out_refs, send_sems, recv_sems, local_sems):
        x, y, c = _me()
        sibling = (x, y, 1 - c)
        chips = _other_chips(x, y)
        me_s = 2 * x + y

        def copy(k, src, dst, to):
            return pltpu.make_async_remote_copy(src_ref=src, dst_ref=dst, send_sem=send_sems.at[k],
                                                recv_sem=recv_sems.at[k], device_id=to, device_id_type=MESH)

        own, first, landed, passed, passed_in = [], [], [], [], []
        for w in range(n):
            shp, fl = shapes[w], flavours[w]
            my_half = _half_rows(shp[1], c)
            sib_half = _half_rows(shp[1], 1 - c)
            own.append(copy(7 * w + 6, w_refs[w], _gathered_part(out_refs[w], shp, fl, me_s, pl.ds(0, shp[1])), sibling))
            for j, chip in enumerate(chips):
                s = 2 * chip[0] + chip[1]
                first.append(copy(7 * w + j, w_refs[w].at[:, my_half, :],
                                  _gathered_part(out_refs[w], shp, fl, me_s, my_half), (*chip, c)))
                part = _gathered_part(out_refs[w], shp, fl, s, my_half)
                landed.append(copy(7 * w + j, part, part, (*chip, c)))
                passed.append(copy(7 * w + 3 + j, part, part, sibling))
                theirs = _gathered_part(out_refs[w], shp, fl, s, sib_half)
                passed_in.append(copy(7 * w + 3 + j, theirs, theirs, sibling))
        return own, first, landed, passed, passed_in

    def start(*refs):
        own, first, _, _, _ = copies(*refs)
        for cp in first + own:
            cp.start()

    def wait(*refs):
        own, first, landed, passed, passed_in = copies(*refs)
        for arrived, onward in zip(landed, passed):
            arrived.wait_recv()
            onward.start()
        for cp in passed_in:
            cp.wait_recv()
        for cp in own:
            cp.wait()
        for cp in first + passed:
            cp.wait_send()

    out_shapes = [jax.ShapeDtypeStruct(_gathered_shape(s, f), w.dtype) for w, s, f in zip(shards, shapes, flavours)]
    return Side(shards, out_shapes, 7 * n, 0, start, wait)


def swap_side(gs):
    n = len(gs)

    def copies(g_refs, out_refs, send_sems, recv_sems, local_sems):
        x, y, c = _me()
        return [pltpu.make_async_remote_copy(
            src_ref=g_refs[w].at[:, _half_rows(gs[w].shape[1], 1 - c), :], dst_ref=out_refs[w],
            send_sem=send_sems.at[w], recv_sem=recv_sems.at[w], device_id=(x, y, 1 - c), device_id_type=MESH)
            for w in range(n)]

    def start(*refs):
        for cp in copies(*refs):
            cp.start()

    def wait(*refs):
        for cp in copies(*refs):
            cp.wait()

    out_shapes = [jax.ShapeDtypeStruct((g.shape[0], g.shape[1] // 2, g.shape[2]), g.dtype) for g in gs]
    return Side(gs, out_shapes, n, 0, start, wait)


def add_half(g, r1, c_idx, name):
    n, R, C = g.shape
    half = R // 2
    tr = _rows_block(half, C)
    nbh = half // tr

    def kern(c_ref, g_ref, r_ref, o_ref, ob_ref):
        p = g_ref[...] + r_ref[...]
        o_ref[...] = p
        ob_ref[...] = p.astype(BF16)

    spec = pl.BlockSpec((1, tr, C), lambda d, r, c_ref: (d, r, 0))
    return pl.pallas_call(
        kern, name=name,
        grid_spec=pltpu.PrefetchScalarGridSpec(
            num_scalar_prefetch=1, grid=(n, nbh),
            in_specs=[pl.BlockSpec((1, tr, C), lambda d, r, c_ref: (d, c_ref[0] * nbh + r, 0)), spec],
            out_specs=[spec, spec]),
        out_shape=[jax.ShapeDtypeStruct((n, half, C), F32), jax.ShapeDtypeStruct((n, half, C), BF16)],
        compiler_params=_cp(("parallel", "parallel")),
    )(c_idx, g, r1)


def exchange_side(ps):
    n = len(ps)

    def copies(p_refs, out_refs, send_sems, recv_sems, local_sems):
        x, y, c = _me()
        return [pltpu.make_async_remote_copy(
            src_ref=p_refs[w].at[2 * chip[0] + chip[1]], dst_ref=out_refs[w].at[j],
            send_sem=send_sems.at[3 * w + j], recv_sem=recv_sems.at[3 * w + j],
            device_id=(*chip, c), device_id_type=MESH)
            for w in range(n) for j, chip in enumerate(_other_chips(x, y))]

    def start(*refs):
        for cp in copies(*refs):
            cp.start()

    def wait(*refs):
        for cp in copies(*refs):
            cp.wait()

    return Side(ps, [jax.ShapeDtypeStruct((3,) + p.shape[1:], p.dtype) for p in ps], 3 * n, 0, start, wait)


def add_chips(p, r2, chip_idx, name):
    n, H, C = p.shape
    tr = _rows_block(H, C)

    def kern(s_ref, p_ref, r_ref, o_ref):
        o_ref[...] = ((p_ref[0] + r_ref[0].astype(F32)) + r_ref[1].astype(F32)) + r_ref[2].astype(F32)

    return pl.pallas_call(
        kern, name=name,
        grid_spec=pltpu.PrefetchScalarGridSpec(
            num_scalar_prefetch=1, grid=(H // tr,),
            in_specs=[pl.BlockSpec((1, tr, C), lambda r, s_ref: (s_ref[0], r, 0)),
                      pl.BlockSpec((3, tr, C), lambda r, s_ref: (0, r, 0))],
            out_specs=pl.BlockSpec((tr, C), lambda r, s_ref: (r, 0))),
        out_shape=jax.ShapeDtypeStruct((H, C), F32),
        compiler_params=_cp(("parallel",)),
    )(chip_idx, p, r2)


def swap_reduced(ss, name):
    n = len(ss)

    def body(*refs):
        s_refs, out_refs = refs[:n], refs[n:2 * n]
        send_sems, recv_sems = refs[2 * n:]
        x, y, c = _me()
        cps = [pltpu.make_async_remote_copy(src_ref=s_refs[w], dst_ref=out_refs[w], send_sem=send_sems.at[w],
                                            recv_sem=recv_sems.at[w], device_id=(x, y, 1 - c), device_id_type=MESH)
               for w in range(n)]
        for cp in cps:
            cp.start()
        for cp in cps:
            cp.wait()

    any_spec = pl.BlockSpec(memory_space=pl.ANY)
    return pl.pallas_call(
        body, name=name, out_shape=[jax.ShapeDtypeStruct(s.shape, s.dtype) for s in ss],
        in_specs=[any_spec] * n, out_specs=[any_spec] * n,
        scratch_shapes=[pltpu.SemaphoreType.DMA((n,)), pltpu.SemaphoreType.DMA((n,))],
    )(*ss)


BIG = (("gla_w_in", 2, (1024, GLA_IN // N_CHIPS), "lead"), ("gla_w_out", 2, (256, 1024), "row"),
       ("att_w_in", 2, (1024, 768), "col"), ("att_w_out", 2, (256, 1024), "row"),
       ("ff_w1", 4, (1024, 1024), "col"), ("ff_w2", 4, (1024, 1024), "row"))
FLAVOUR = {n: f for n, _, _, f in BIG}


def layer_weights(i):
    mixer = "gla" if i % 2 == 0 else "att"
    return (("in", mixer + "_w_in", i // 2), ("out", mixer + "_w_out", i // 2), ("w1", "ff_w1", i), ("w2", "ff_w2", i))


class Comm:
    def __init__(self, weights, core, chip):
        self.weights, self.core, self.chip = weights, core, chip
        self.c_idx = jnp.reshape(core, (1,)).astype(jnp.int32)
        self.chip_idx = jnp.reshape(chip, (1,)).astype(jnp.int32)
        self.reduced = {}

    def gather(self, items):
        shards = [self.weights[n][l:l + 1].astype(BF16) for _, n, l in items]
        return gather_side(shards, [FLAVOUR[n] for _, n, _ in items])

    def full_weights(self, items, gathered):
        W = {}
        for (role, n, _), w in zip(items, gathered):
            if n == "gla_w_in":
                w = jnp.pad(w.transpose(1, 2, 0, 3).reshape(1, D_MODEL, GLA_IN), ((0, 0), (0, 0), (0, GLA_IN_PAD - GLA_IN)))
            W[role] = (w, 0)
        return W

    def gather_now(self, items, name):
        return self.full_weights(items, run_side(self.gather(items), name))

    def swap(self, items):
        return swap_side([g for _, _, g in items])

    def reduce_begin(self, tag, items, swapped):
        ps = [add_half(g, r, self.c_idx, f"rs_add2_{tag}_{w}") for w, ((_, _, g), r) in enumerate(zip(items, swapped))]
        return tag, [(n, l) for n, l, _ in items], ps

    def exchange(self, pending):
        return exchange_side([pb for _, pb in pending[2]])

    def reduce_mid(self, pending, landed):
        tag, keys, ps = pending
        for w, (key, (p, _), r) in enumerate(zip(keys, ps, landed)):
            self.reduced[key] = add_chips(p, r, self.chip_idx, f"rs_add4_{tag}_{w}")

    def reduce_tail(self, tag, items):
        pending = self.reduce_begin(tag, items, run_side(self.swap(items), f"rs_swap_{tag}"))
        self.reduce_mid(pending, run_side(self.exchange(pending), f"rs_xchg_{tag}"))

    def reduce_end(self):
        keys = [(n, l) for n, L, _, _ in BIG for l in range(L)]
        mine = [self.reduced[k] for k in keys]
        theirs = swap_reduced(mine, "rs_join")
        low = self.core == 0
        full = {k: jnp.concatenate([jnp.where(low, m, t), jnp.where(low, t, m)], axis=0)
                for k, m, t in zip(keys, mine, theirs)}
        return {n: jnp.stack([full[(n, l)] for l in range(L)]) for n, L, _, _ in BIG}


def local_step(x, target, mods, comm, small):
    S, D = x.shape
    row = lambda v: v.reshape(1, -1)
    saved = []
    tiles = [att_bias_tiles(small["att_rel_bias"][j], f"att_tiles_{j}").reshape(ATT_HEADS, ATT_CLASSES, ATT_TQ, ATT_KW)
             for j in range(2)]
    wgk_p = [jnp.pad(small["gla_w_gk2"][j], ((0, 128 - GLA_RANK), (0, 0))).astype(BF16) for j in range(2)]

    u1 = modulate(x, row(mods[0, 1]), row(mods[0, 0]), "mod_first")
    Ws = [dict() for _ in range(DEPTH)]
    items0 = layer_weights(0)
    Ws[0].update(comm.gather_now(items0[:1], "gather_w0"))
    for i in range(DEPTH):
        j = i // 2
        W = Ws[i]
        sh1, sc1, g1, sh2, sc2, g2 = (row(mods[i, k]) for k in range(6))
        nxt = min(i + 1, DEPTH - 1)
        more = i + 1 < DEPTH
        nxt_items = layer_weights(nxt)
        in_items = list(items0[1:3]) if i == 0 else []
        mix_items = (list(items0[3:]) if i == 0 else []) + (list(nxt_items[:2]) if more else [])
        up_items = list(nxt_items[2:3]) if more else []
        down_items = list(nxt_items[3:]) if more else []

        def hosted(items):
            return comm.gather(items) if items else None

        def landed_weights(items, landed):
            for k, it in enumerate(items):
                layer = 0 if it in items0 and i == 0 else nxt
                Ws[layer].update(comm.full_weights([it], landed[k:k + 1]))

        side = hosted(in_items)
        if i % 2 == 0:
            proj = mm_plain(u1, *W["in"], f"gla_in_{i}", side=side)
        else:
            proj = mm_plain(u1, *W["in"], f"att_in_{i}", mode="bf16", bias=row(small["att_b_in"][j]), side=side)
        proj, landed = proj if side is not None else (proj, [])
        landed_weights(in_items, landed)
        if i % 2 == 0:
            (zmix, states), landed = gla_fwd(proj, wgk_p[j], row(small["gla_b_gk"][j]), row(small["gla_g_norm"][j]),
                                             f"gla_fwd_{i}", hosted(mix_items))
        else:
            (zmix,), landed = attn_fwd(proj, tiles[j], f"att_fwd_{i}", hosted(mix_items))
            states = None
        landed_weights(mix_items, landed)
        (y1, x_mid, u2), _ = mm_down_ln(zmix, *W["out"], x, 1.0 + g1, row(small["ln_g"][i, 0]),
                                        row(small["ln_b"][i, 0]), sc2, sh2, f"mix_out_{i}")
        side = hosted(up_items)
        act = mm_plain(u2, *W["w1"], f"ff_up_{i}", mode="mlp_up", side=side)
        act, landed = act if side is not None else (act, [])
        landed_weights(up_items, landed)
        (y2, x_out, u_next), landed = mm_down_ln(act, *W["w2"], x_mid, 1.0 + g2, row(small["ln_g"][i, 1]),
                                                 row(small["ln_b"][i, 1]), row(mods[nxt, 1]), row(mods[nxt, 0]),
                                                 f"ff_out_{i}", side=hosted(down_items))
        landed_weights(down_items, landed)
        saved.append(dict(x_in=x, u1=u1, proj=proj, zmix=zmix, states=states, y1=y1, x_mid=x_mid, u2=u2,
                          act=act, y2=y2))
        x, u1 = x_out, u_next

    g_small = dict(ln_g=[None] * DEPTH, ln_b=[None] * DEPTH, gla_w_gk2=[None] * 2, gla_b_gk=[None] * 2,
                   gla_g_norm=[None] * 2, att_b_in=[None] * 2, att_rel_bias=[None] * 2)
    dmods = [None] * DEPTH
    later = []
    top = saved[DEPTH - 1]
    dz2, dy2, s_ln2, sq = loss_ln_bwd(x, target, top["x_mid"], top["y2"], 1.0 + row(mods[DEPTH - 1, 5]),
                                      row(small["ln_g"][DEPTH - 1, 1]), "loss_ln_bwd")

    for i in reversed(range(DEPTH)):
        j = i // 2
        sv = saved[i]
        W = Ws[i]
        sh1, sc1, g1, sh2, sc2, g2 = (row(mods[i, k]) for k in range(6))
        dh = mm_plain(dy2, *W["w2"], f"ff_dn_{i}", mode="mlp_dn", nt=True, h=sv["act"])
        g_w2 = mm_w_res(sv["act"], dy2, f"ff_w2g_{i}").reshape(N_CHIPS, D_FF // N_CHIPS, D)
        g_w1 = mm_w_res(sv["u2"], dh, f"ff_w1g_{i}", chips_out=True)
        items = [("ff_w1", i, g_w1), ("ff_w2", i, g_w2)] + later
        (dz1, dy1, s_m2, s_ln1), swapped = mm_down_comb(
            dh, *W["w1"], dz2, sv["x_mid"], 1.0 + sc2, f"ff_dx_{i}",
            ln=(sv["x_in"], sv["y1"], 1.0 + g1, row(small["ln_g"][i, 0])), side=comm.swap(items))
        pending = comm.reduce_begin(i, items, swapped)
        side = comm.exchange(pending)
        mixer = "gla" if i % 2 == 0 else "att"
        below = None
        if i > 0:
            below = (saved[i - 1]["x_mid"], saved[i - 1]["y2"], 1.0 + row(mods[i - 1, 5]), row(small["ln_g"][i - 1, 1]))
        if i % 2 == 0:
            g_out = mm_w_res(sv["zmix"], dy1, f"gla_wog_{i}", ts=1024).reshape(N_CHIPS, D // N_CHIPS, D)
            dzg = mm_plain(dy1, *W["out"], f"gla_dz_{i}", nt=True)
            (dproj, dwgk, dbgk, dgn), landed = gla_bwd(sv["proj"], sv["states"], dzg, wgk_p[j],
                                                       row(small["gla_b_gk"][j]), row(small["gla_g_norm"][j]),
                                                       f"gla_bwd_{i}", side)
            g_small["gla_w_gk2"][j] = dwgk[:GLA_RANK]
            g_small["gla_b_gk"][j] = dbgk[0]
            g_small["gla_g_norm"][j] = dgn[0].reshape(GLA_HEADS, GLA_DV_HEAD)
            gwi = mm_w_res(sv["u1"], dproj, f"gla_wig_{i}")[:, :GLA_IN]
            g_in = gwi.reshape(D, N_CHIPS, GLA_IN // N_CHIPS).transpose(1, 0, 2)
            outs, _ = mm_down_comb(dproj, *W["in"], dz1, sv["x_in"], 1.0 + sc1, f"mix_dx_{i}", ln=below)
        else:
            g_out = mm_w_res(sv["zmix"], dy1, f"att_wog_{i}", ts=1024).reshape(N_CHIPS, D // N_CHIPS, D)
            do = mm_plain(dy1, *W["out"], f"att_do_{i}", mode="bf16", nt=True)
            (dqkv, dbt), landed = attn_bwd(sv["proj"], tiles[j], do, f"att_bwd_{i}", side)
            g_small["att_rel_bias"][j] = att_bias_grad(dbt.reshape(ATT_HEADS * ATT_CLASSES, ATT_TQ, ATT_KW),
                                                       f"att_bias_{i}")
            g_in = mm_w_chips3(sv["u1"], dqkv, f"att_wig_{i}")
            outs, _ = mm_down_comb(dqkv, *W["in"], dz1, sv["x_in"], 1.0 + sc1, f"mix_dx_{i}", parts=3, ln=below)
        s_m1 = outs[1] if below is None else outs[2]
        if i % 2 == 1:
            g_small["att_b_in"][j] = s_m1[2:5].reshape(3 * D)
        comm.reduce_mid(pending, landed)
        later = [(mixer + "_w_in", j, g_in), (mixer + "_w_out", j, g_out)]
        g_small["ln_g"][i] = jnp.stack([s_ln1[0], s_ln2[0]])
        g_small["ln_b"][i] = jnp.stack([s_ln1[1], s_ln2[1]])
        dmods[i] = jnp.stack([s_m1[1], s_m1[0], s_ln1[2], s_m2[1], s_m2[0], s_ln2[2]])
        if below is None:
            dx = outs[0]
        else:
            dz2, dy2, s_ln2 = outs[0], outs[1], outs[3]
    comm.reduce_tail("last", later)

    g_small = {n: jnp.stack(v) for n, v in g_small.items()}
    return sq, dx, jnp.stack(dmods), g_small


SMALL_SHARDED = (("ln_g", (4, 2, 256)), ("ln_b", (4, 2, 256)), ("gla_g_norm", (2, 4, 64)),
                 ("gla_w_gk2", (2, 16, 128)), ("att_b_in", (2, 768)))
SMALL_FULL = dict(ln_g=(4, 2, 1024), ln_b=(4, 2, 1024), gla_g_norm=(2, 4, 256), gla_w_gk2=(2, 16, 512),
                  att_b_in=(2, 3072), gla_b_gk=(2, 512), att_rel_bias=(2, 16, 257))
SMALL_GRAD_ORDER = ("ln_g", "ln_b", "gla_g_norm", "gla_w_gk2", "att_b_in", "gla_b_gk", "att_rel_bias")


def _pack_small(arrs, rows_total):
    parts = []
    for a in arrs:
        flat = a.reshape(-1)
        pad = (-flat.shape[0]) % PACK_W
        parts.append(jnp.pad(flat, (0, pad)).reshape(-1, PACK_W))
    buf = jnp.concatenate(parts, axis=0)
    return jnp.pad(buf, ((0, rows_total - buf.shape[0]), (0, 0)))


def _unpack_small(buf, shapes):
    out, r = [], 0
    for shp in shapes:
        n = 1
        for s in shp:
            n *= s
        nr = (n + PACK_W - 1) // PACK_W
        out.append(buf[..., r:r + nr, :].reshape(buf.shape[:-2] + (nr * PACK_W,))[..., :n].reshape(buf.shape[:-2] + shp))
        r += nr
    return out


def _unshard_last(g4):
    nd = g4.ndim
    perm = tuple(range(1, nd - 1)) + (0, nd - 1)
    t = g4.transpose(perm)
    return t.reshape(t.shape[:-2] + (-1,))


def _shard_last(full, s):
    n = full.shape[-1] // N_CHIPS
    return lax.dynamic_slice_in_dim(full, s * n, n, axis=full.ndim - 1)


WEIGHT_NAMES = ("w_ada", "b_ada", "ln_g", "ln_b", "gla_w_in", "gla_w_gk2", "gla_b_gk", "gla_g_norm", "gla_w_out",
                "att_w_in", "att_b_in", "att_rel_bias", "att_w_out", "ff_w1", "ff_w2")


def kernel(x, c, w_ada, b_ada, ln_g, ln_b, gla_w_in, gla_w_gk2, gla_b_gk, gla_g_norm, gla_w_out, att_w_in, att_b_in, att_rel_bias, att_w_out, ff_w1, ff_w2, loss_target, m_w_ada, m_b_ada, m_ln_g, m_ln_b, m_gla_w_in, m_gla_w_gk2, m_gla_b_gk, m_gla_g_norm, m_gla_w_out, m_att_w_in, m_att_b_in, m_att_rel_bias, m_att_w_out, m_ff_w1, m_ff_w2, v_w_ada, v_b_ada, v_ln_g, v_ln_b, v_gla_w_in, v_gla_w_gk2, v_gla_b_gk, v_gla_g_norm, v_gla_w_out, v_att_w_in, v_att_b_in, v_att_rel_bias, v_att_w_out, v_ff_w1, v_ff_w2):
    weights = dict(w_ada=w_ada, b_ada=b_ada, ln_g=ln_g, ln_b=ln_b, gla_w_in=gla_w_in, gla_w_gk2=gla_w_gk2,
                   gla_b_gk=gla_b_gk, gla_g_norm=gla_g_norm, gla_w_out=gla_w_out, att_w_in=att_w_in,
                   att_b_in=att_b_in, att_rel_bias=att_rel_bias, att_w_out=att_w_out, ff_w1=ff_w1, ff_w2=ff_w2)
    mom1 = dict(w_ada=m_w_ada, b_ada=m_b_ada, ln_g=m_ln_g, ln_b=m_ln_b, gla_w_in=m_gla_w_in, gla_w_gk2=m_gla_w_gk2,
                gla_b_gk=m_gla_b_gk, gla_g_norm=m_gla_g_norm, gla_w_out=m_gla_w_out, att_w_in=m_att_w_in,
                att_b_in=m_att_b_in, att_rel_bias=m_att_rel_bias, att_w_out=m_att_w_out, ff_w1=m_ff_w1, ff_w2=m_ff_w2)
    mom2 = dict(w_ada=v_w_ada, b_ada=v_b_ada, ln_g=v_ln_g, ln_b=v_ln_b, gla_w_in=v_gla_w_in, gla_w_gk2=v_gla_w_gk2,
                gla_b_gk=v_gla_b_gk, gla_g_norm=v_gla_g_norm, gla_w_out=v_gla_w_out, att_w_in=v_att_w_in,
                att_b_in=v_att_b_in, att_rel_bias=v_att_rel_bias, att_w_out=v_att_w_out, ff_w1=v_ff_w1, ff_w2=v_ff_w2)

    ax, ay, ac = lax.axis_index("x"), lax.axis_index("y"), lax.axis_index("c")
    chip = 2 * ax + ay
    dev = 2 * chip + ac
    S = x.shape[1]
    x2 = x.reshape(S, D_MODEL)
    t2 = loss_target.reshape(S, D_MODEL)

    comm = Comm(weights, ac, chip)

    small_rows = 16
    spack = _pack_small([c] + [weights[n] for n, _ in SMALL_SHARDED], small_rows)
    sg = all_gather8(spack, "gather_small").reshape(N_DEV, small_rows, PACK_W)
    parts = _unpack_small(sg, [(1, D_MODEL)] + [shp for _, shp in SMALL_SHARDED])
    c_all = parts[0].reshape(N_DEV, D_MODEL)
    small = {n: _unshard_last(p[0::2]) for (n, _), p in zip(SMALL_SHARDED, parts[1:])}
    small["gla_b_gk"] = gla_b_gk
    small["att_rel_bias"] = att_rel_bias

    c_act = silu_rows(jnp.pad(c_all, ((0, 128 - N_DEV), (0, 0))), "silu_c")
    mods_part = jnp.stack([mm_plain(c_act, w_ada, l, f"ada_fwd_{l}", tm=128)[:N_DEV] for l in range(DEPTH)], axis=1)
    mods_part = mods_part.reshape(N_DEV, DEPTH * 6 * D_MODEL // N_CHIPS)
    mg = all_gather8(mods_part, "gather_mods").reshape(N_CHIPS, 2, N_DEV, DEPTH, 6 * D_MODEL // N_CHIPS)
    mods_mine = lax.dynamic_index_in_dim(mg[:, 0], dev, axis=1, keepdims=False)
    mods = mods_mine.transpose(1, 0, 2).reshape(DEPTH, 6 * D_MODEL) + b_ada
    mods = mods.reshape(DEPTH, 6, D_MODEL)

    sq, grad_x, dmods, g_small = local_step(x2, t2, mods, comm, small)
    loss = lax.psum(0.5 * sq[0, 0] / D_MODEL, ("x", "y", "c"))

    g_shard = comm.reduce_end()

    dm_flat = dmods.reshape(DEPTH, 6 * D_MODEL)
    g_rows = 80
    gpack = _pack_small([dm_flat] + [g_small[n] for n in SMALL_GRAD_ORDER], g_rows)
    gg = all_gather8(gpack, "gather_small_grads").reshape(N_DEV, g_rows, PACK_W)
    gsum = sum_over_devices(gg, "sum_small_grads")
    sums = _unpack_small(gsum, [(DEPTH, 6 * D_MODEL)] + [SMALL_FULL[n] for n in SMALL_GRAD_ORDER])
    grads = dict(b_ada=sums[0])
    for n, full_g in zip(SMALL_GRAD_ORDER, sums[1:]):
        grads[n] = full_g if n in ("gla_b_gk", "att_rel_bias") else _shard_last(full_g, chip)
    dm_all = _unpack_small(gg, [(DEPTH, 6 * D_MODEL)])[0]
    dm_cols = _shard_last(dm_all, chip).reshape(N_DEV, DEPTH * 6 * D_MODEL // N_CHIPS)
    dm_cols = jnp.pad(dm_cols, ((0, 128 - N_DEV), (0, 0))).astype(BF16)
    gwa = mm_w(c_act, dm_cols, "ada_bwd", ts=128)
    grads["w_ada"] = gwa.reshape(D_MODEL, DEPTH, 6 * D_MODEL // N_CHIPS).transpose(1, 0, 2)
    grads.update(g_shard)

    deltas, new_m, new_v = {}, {}, {}
    for n in WEIGHT_NAMES:
        deltas[n], new_m[n], new_v[n] = adamw(weights[n], grads[n], mom1[n], mom2[n], "adamw_" + n)

    return (loss, grad_x.reshape(1, S, D_MODEL), *[grads[n] for n in WEIGHT_NAMES], *[deltas[n] for n in WEIGHT_NAMES],
            *[new_m[n] for n in WEIGHT_NAMES], *[new_v[n] for n in WEIGHT_NAMES])
```

```python
import functools

import jax
import jax.numpy as jnp
from jax import lax
from jax.experimental import pallas as pl
from jax.experimental.pallas import tpu as pltpu

F32 = jnp.float32
BF16 = jnp.bfloat16
HIGHEST = lax.Precision.HIGHEST
MESH = pl.DeviceIdType.MESH

D_MODEL = 1024
DEPTH = 4
CHUNK = 64
GLA_HEADS = 4
GLA_DK = 512
GLA_DV = 1024
GLA_DK_HEAD = 128
GLA_DV_HEAD = 256
GLA_RANK = 16
GLA_IN = 3088
GLA_IN_PAD = 3200
GLA_LR_OFF = 3072
ATT_HEADS = 16
ATT_HD = 64
LEFT_CHUNKS = 8
MAX_REL = 128
N_REL = 257
D_FF = 4096
ALPHA = (2.0 * DEPTH) ** 0.25
LN_EPS = 1e-5
RMS_EPS = 1e-6
NEG_INF = -1e30
GLA_SCALE = GLA_DK_HEAD ** -0.5
ATT_SCALE = ATT_HD ** -0.5
ADAM_LR = 0.001
ADAM_B1 = 0.9
ADAM_B2 = 0.999
ADAM_EPS = 1e-08
ADAM_WD = 0.01
ADAM_STEP = 10

ATT_TQ = 256
ATT_KW = 768
GLA_TB = 256
GLA_GROUP = 2
GLA_GROUP_FWD = 4
VMEM_LIMIT = 56 * 1024 * 1024
WHOLE_WEIGHT_BYTES = 8 * 1024 * 1024
N_CHIPS = 4
N_DEV = 8
PACK_W = 1024


def _dot(a, b):
    return jnp.dot(a, b, preferred_element_type=F32)


def _dot_nt(a, b):
    return lax.dot_general(a, b, (((1,), (1,)), ((), ())), preferred_element_type=F32)


def _dot_tn(a, b):
    return lax.dot_general(a, b, (((0,), (0,)), ((), ())), preferred_element_type=F32)


def _cp(sem, vmem=VMEM_LIMIT):
    return pltpu.CompilerParams(dimension_semantics=sem, vmem_limit_bytes=vmem)


def _row_spec(n):
    return pl.BlockSpec((1, n), lambda *_: (0, 0))


def _sigmoid(x):
    return 1.0 / (1.0 + jnp.exp(-x))


def _log_sigmoid(x):
    return jnp.minimum(x, 0.0) - jnp.log1p(jnp.exp(-jnp.abs(x)))


class Side:
    def __init__(self, ins, out_shapes, n_sems, n_local, start, wait):
        self.ins, self.out_shapes, self.n_sems, self.n_local = list(ins), list(out_shapes), n_sems, n_local
        self.start, self.wait = start, wait

    def sem_shapes(self):
        return [pltpu.SemaphoreType.DMA((self.n_sems,)), pltpu.SemaphoreType.DMA((self.n_sems,)),
                pltpu.SemaphoreType.DMA((max(self.n_local, 1),))]


def run_side(side, name):
    n_in = len(side.ins)
    n_out = len(side.out_shapes)

    def body(*refs):
        ins, outs, sems = refs[:n_in], refs[n_in:n_in + n_out], refs[n_in + n_out:]
        side.start(ins, outs, *sems)
        side.wait(ins, outs, *sems)

    any_spec = pl.BlockSpec(memory_space=pl.ANY)
    return pl.pallas_call(body, name=name, out_shape=side.out_shapes, in_specs=[any_spec] * n_in,
                          out_specs=[any_spec] * n_out, scratch_shapes=side.sem_shapes())(*side.ins)


def hosted_call(main, side, *, name, grid, in_specs, out_specs, out_shape, scratch_shapes, dims, args):
    if side is None:
        outs = pl.pallas_call(main, name=name, grid=grid, in_specs=in_specs, out_specs=out_specs,
                              out_shape=out_shape, scratch_shapes=scratch_shapes, compiler_params=_cp(dims))(*args)
        return list(outs), []
    n_mi, n_mo, n_ms = len(in_specs), len(out_specs), len(scratch_shapes)
    n_si, n_so = len(side.ins), len(side.out_shapes)

    def kern(*refs):
        mi, si = refs[:n_mi], refs[n_mi:n_mi + n_si]
        o0 = n_mi + n_si
        mo, so = refs[o0:o0 + n_mo], refs[o0 + n_mo:o0 + n_mo + n_so]
        s0 = o0 + n_mo + n_so
        ms, sems = refs[s0:s0 + n_ms], refs[s0 + n_ms:]
        ids = [pl.program_id(d) for d in range(len(grid))]
        first = functools.reduce(jnp.logical_and, [i == 0 for i in ids])
        last = functools.reduce(jnp.logical_and, [i == g - 1 for i, g in zip(ids, grid)])

        @pl.when(first)
        def _():
            side.start(si, so, *sems)
        main(*mi, *mo, *ms)

        @pl.when(last)
        def _():
            side.wait(si, so, *sems)

    any_spec = pl.BlockSpec(memory_space=pl.ANY)
    outs = pl.pallas_call(
        kern, name=name, grid=grid, in_specs=list(in_specs) + [any_spec] * n_si,
        out_specs=list(out_specs) + [any_spec] * n_so, out_shape=list(out_shape) + side.out_shapes,
        scratch_shapes=list(scratch_shapes) + side.sem_shapes(),
        compiler_params=_cp(("arbitrary",) * len(grid)))(*args, *side.ins)
    return list(outs[:n_mo]), list(outs[n_mo:])


def modulate(x, sc, sh, name):
    S, D = x.shape
    tm = min(512, S)

    def kern(x_ref, sc_ref, sh_ref, u_ref):
        u_ref[...] = (x_ref[...] * (1.0 + sc_ref[...]) + sh_ref[...]).astype(BF16)

    return pl.pallas_call(
        kern, name=name, grid=(S // tm,),
        in_specs=[pl.BlockSpec((tm, D), lambda i: (i, 0)), _row_spec(D), _row_spec(D)],
        out_specs=pl.BlockSpec((tm, D), lambda i: (i, 0)),
        out_shape=jax.ShapeDtypeStruct((S, D), BF16),
        compiler_params=_cp(("parallel",)),
    )(x, sc, sh)


def silu_rows(c_all, name):
    def kern(c_ref, o_ref):
        c = c_ref[...]
        o_ref[...] = (c * _sigmoid(c)).astype(BF16)

    return pl.pallas_call(kern, name=name, out_shape=jax.ShapeDtypeStruct(c_all.shape, BF16))(c_all)


def sum_over_devices(g, name):
    n, R, C = g.shape

    def kern(g_ref, o_ref):
        acc = g_ref[0]
        for d in range(1, n):
            acc = acc + g_ref[d]
        o_ref[...] = acc

    return pl.pallas_call(kern, name=name, out_shape=jax.ShapeDtypeStruct((R, C), F32))(g)


def _rows_block(R, C, budget=1 << 20):
    if R * C * 4 <= budget or R % 8:
        return R
    tr = max(8, (budget // (C * 4)) // 8 * 8)
    while R % tr:
        tr -= 8
    return tr


def adamw(w, g, m, v, name):
    shape = w.shape
    C = shape[-1]
    R = w.size // C
    w2, g2, m2, v2 = (t.reshape(R, C) for t in (w, g, m, v))
    tr = _rows_block(R, C)
    c1 = 1.0 - ADAM_B1 ** ADAM_STEP
    c2 = 1.0 - ADAM_B2 ** ADAM_STEP

    def kern(w_ref, g_ref, m_ref, v_ref, d_ref, nm_ref, nv_ref):
        gg = g_ref[...]
        nm = ADAM_B1 * m_ref[...] + (1.0 - ADAM_B1) * gg
        nv = ADAM_B2 * v_ref[...] + (1.0 - ADAM_B2) * (gg * gg)
        m_hat = nm / c1
        v_hat = nv / c2
        d_ref[...] = -ADAM_LR * (m_hat / (jnp.sqrt(v_hat) + ADAM_EPS) + ADAM_WD * w_ref[...])
        nm_ref[...] = nm
        nv_ref[...] = nv

    spec = pl.BlockSpec((tr, C), lambda i: (i, 0))
    outs = pl.pallas_call(
        kern, name=name, grid=(R // tr,),
        in_specs=[spec] * 4, out_specs=[spec] * 3,
        out_shape=[jax.ShapeDtypeStruct((R, C), F32)] * 3,
        compiler_params=_cp(("parallel",)),
    )(w2, g2, m2, v2)
    return tuple(o.reshape(shape) for o in outs)


def _tn_for(N):
    for tn in (1024, 768, 640, 512, 384, 256, 128):
        if N % tn == 0:
            return tn
    return N


def mm_plain(a, b3, layer, name, *, mode="f32", nt=False, bias=None, h=None, tm=1024, side=None):
    M, K = a.shape
    N = b3.shape[1] if nt else b3.shape[2]
    if K * N * 2 <= WHOLE_WEIGHT_BYTES:
        tn, tm = N, min(tm, 512 if N > D_MODEL else 1024)
    else:
        tn = _tn_for(N)
    tm = min(tm, M)
    a_spec = pl.BlockSpec((tm, K), lambda j, i: (i, 0))
    if nt:
        b_spec = pl.BlockSpec((None, tn, K), lambda j, i: (layer, j, 0))
    else:
        b_spec = pl.BlockSpec((None, K, tn), lambda j, i: (layer, 0, j))
    o_spec = pl.BlockSpec((tm, tn), lambda j, i: (i, j))
    ins, in_specs = [a, b3], [a_spec, b_spec]
    if bias is not None:
        ins.append(bias)
        in_specs.append(pl.BlockSpec((1, tn), lambda j, i: (0, j)))
    if mode == "mlp_dn":
        ins.append(h)
        in_specs.append(o_spec)
    elif mode not in ("f32", "bf16", "mlp_up"):
        raise ValueError(mode)
    odt = F32 if mode == "f32" else BF16

    def kern(a_ref, b_ref, *rest):
        rest = list(rest)
        bias_ref = rest.pop(0) if bias is not None else None
        h_ref = rest.pop(0) if mode == "mlp_dn" else None
        o_ref = rest.pop(0)
        if nt:
            bt_ref = rest.pop(0)

            @pl.when(pl.program_id(1) == 0)
            def _():
                bt_ref[...] = b_ref[...].T
            acc = _dot(a_ref[...], bt_ref[...])
        else:
            acc = _dot(a_ref[...], b_ref[...].astype(BF16))
        if bias_ref is not None:
            acc = acc + bias_ref[...]
        if mode == "mlp_up":
            r = jnp.maximum(acc, 0.0)
            acc = r * r
        elif mode == "mlp_dn":
            acc = acc * (2.0 * jnp.sqrt(h_ref[...].astype(F32)))
        o_ref[...] = acc.astype(odt)

    outs, landed = hosted_call(
        kern, side, name=name, grid=(N // tn, M // tm), in_specs=in_specs, out_specs=[o_spec],
        out_shape=[jax.ShapeDtypeStruct((M, N), odt)],
        scratch_shapes=[pltpu.VMEM((K, tn), BF16)] if nt else [], dims=("parallel", "arbitrary"), args=tuple(ins))
    return outs[0] if side is None else (outs[0], landed)


def mm_down_ln(a, b3, layer, x_in, gate1p, ln_g, ln_b, sc_next, sh_next, name, *, side=None, tm=512):
    M, K = a.shape
    D = b3.shape[2]
    tm = min(tm, M)

    def kern(a_ref, b_ref, x_ref, gp_ref, lg_ref, lb_ref, sc_ref, sh_ref, y_ref, xo_ref, u_ref):
        y = _dot(a_ref[...], b_ref[...])
        y_ref[...] = y.astype(BF16)
        z = ALPHA * x_ref[...] + gp_ref[...] * y
        mu = jnp.mean(z, axis=-1, keepdims=True)
        zc = z - mu
        var = jnp.mean(zc * zc, axis=-1, keepdims=True)
        xo = (zc * lax.rsqrt(var + LN_EPS)) * lg_ref[...] + lb_ref[...]
        xo_ref[...] = xo
        u_ref[...] = (xo * (1.0 + sc_ref[...]) + sh_ref[...]).astype(BF16)

    tile = pl.BlockSpec((tm, D), lambda i: (i, 0))
    outs, landed = hosted_call(
        kern, side, name=name, grid=(M // tm,),
        in_specs=[pl.BlockSpec((tm, K), lambda i: (i, 0)), pl.BlockSpec((None, K, D), lambda i: (layer, 0, 0)), tile]
        + [_row_spec(D)] * 5,
        out_specs=[tile, tile, tile],
        out_shape=[jax.ShapeDtypeStruct((M, D), BF16), jax.ShapeDtypeStruct((M, D), F32),
                   jax.ShapeDtypeStruct((M, D), BF16)],
        scratch_shapes=[], dims=("parallel",), args=(a, b3, x_in, gate1p, ln_g, ln_b, sc_next, sh_next))
    return tuple(outs), landed


def mm_down_comb(a, b3, layer, dz, x_in, sc1p, name, *, parts=1, ln=None, side=None, tm=512):
    D, K = b3.shape[1], b3.shape[2]
    M = a.shape[-2]
    kp = K // parts
    tm = min(tm, M)
    n_ln = 0 if ln is None else 4

    def kern(*refs):
        a_refs = refs[:parts]
        b_ref, dz_ref, x_ref, sp_ref = refs[parts:parts + 4]
        ln_refs = refs[parts + 4:parts + 4 + n_ln]
        outs = refs[parts + 4 + n_ln:]

        @pl.when(pl.program_id(0) == 0)
        def _():
            for o in outs:
                if o.shape[0] == 8:
                    o[...] = jnp.zeros_like(o)
        if parts == 1:
            du = _dot_nt(a_refs[0][...], b_ref[...])
        else:
            du = _dot_nt(a_refs[0][...], b_ref[:, 0:kp])
            for p in range(1, parts):
                du = du + _dot_nt(a_refs[p][...], b_ref[:, p * kp:(p + 1) * kp])
        dx = ALPHA * dz_ref[...] + du * sp_ref[...]
        if ln is None:
            dx_ref, s_ref = outs
            dx_ref[...] = dx
        else:
            dzl_ref, dyl_ref, s_ref, sl_ref = outs
            _ln_bwd_tile(dx, *ln_refs, dzl_ref, dyl_ref, sl_ref)
        s_ref[0:1, :] += jnp.sum(du * x_ref[...], axis=0, keepdims=True)
        s_ref[1:2, :] += jnp.sum(du, axis=0, keepdims=True)
        if parts > 1:
            for p in range(parts):
                s_ref[2 + p:3 + p, :] += jnp.sum(a_refs[p][...].astype(F32), axis=0, keepdims=True)

    tile = pl.BlockSpec((tm, D), lambda i: (i, 0))
    sums = pl.BlockSpec((8, D), lambda i: (0, 0))
    if parts == 1:
        a_ins, a_specs = [a], [pl.BlockSpec((tm, K), lambda i: (i, 0))]
    else:
        assert kp == D and parts <= 6
        a_ins = [a] * parts
        a_specs = [pl.BlockSpec((None, tm, kp), functools.partial(lambda i, p: (p, i, 0), p=p)) for p in range(parts)]
    in_specs = a_specs + [pl.BlockSpec((None, D, K), lambda i: (layer, 0, 0)), tile, tile, _row_spec(D)]
    args = a_ins + [b3, dz, x_in, sc1p]
    if ln is None:
        out_specs = [tile, sums]
        out_shape = [jax.ShapeDtypeStruct((M, D), F32), jax.ShapeDtypeStruct((8, D), F32)]
    else:
        in_specs += [tile, tile, _row_spec(D), _row_spec(D)]
        args += list(ln)
        out_specs = [tile, tile, sums, sums]
        out_shape = [jax.ShapeDtypeStruct((M, D), F32), jax.ShapeDtypeStruct((M, D), BF16),
                     jax.ShapeDtypeStruct((8, D), F32), jax.ShapeDtypeStruct((8, D), F32)]
    return hosted_call(kern, side, name=name, grid=(M // tm,), in_specs=in_specs, out_specs=out_specs,
                       out_shape=out_shape, scratch_shapes=[], dims=("arbitrary",), args=tuple(args))


def mm_w(a, b, name, *, ts=2048, tk=512, chips_out=False, b_parts=1, tn=None):
    S, K = a.shape
    npart = b.shape[-1]
    N = npart * b_parts
    ts = min(ts, S)
    tk = min(tk, K)
    n_chip = N // N_CHIPS
    if tn is None:
        tn = _tn_for(n_chip if chips_out else npart)
    assert npart % tn == 0 and (not chips_out or n_chip % tn == 0)

    def kern(a_ref, b_ref, o_ref):
        @pl.when(pl.program_id(2) == 0)
        def _():
            o_ref[...] = jnp.zeros_like(o_ref)
        o_ref[...] += _dot_tn(a_ref[...], b_ref[...])

    if b_parts == 1:
        b_spec = pl.BlockSpec((ts, tn), lambda k, n, s: (s, n))
    else:
        per = npart // tn
        b_spec = pl.BlockSpec((None, ts, tn), lambda k, n, s: (n // per, s, n % per))
    if chips_out:
        per_chip = n_chip // tn
        o_spec = pl.BlockSpec((None, tk, tn), lambda k, n, s: (n // per_chip, k, n % per_chip))
        out_shape = jax.ShapeDtypeStruct((N_CHIPS, K, n_chip), F32)
    else:
        o_spec = pl.BlockSpec((tk, tn), lambda k, n, s: (k, n))
        out_shape = jax.ShapeDtypeStruct((K, N), F32)
    return pl.pallas_call(
        kern, name=name, grid=(K // tk, N // tn, S // ts),
        in_specs=[pl.BlockSpec((ts, tk), lambda k, n, s: (s, k)), b_spec],
        out_specs=o_spec, out_shape=out_shape,
        compiler_params=_cp(("parallel", "parallel", "arbitrary")),
    )(a, b)


def mm_w_chips3(a, b3, name, *, ts=512):
    S, K = a.shape
    P = b3.shape[2]
    n_chip = 3 * P // N_CHIPS
    ts = min(ts, S)
    pieces = []
    for chip in range(N_CHIPS):
        lo, hi = chip * n_chip, (chip + 1) * n_chip
        while lo < hi:
            part = lo // P
            w = min(hi, (part + 1) * P) - lo
            pieces.append((chip, lo - chip * n_chip, part, lo - part * P, w))
            lo += w

    def kern(a_ref, b_ref, o_ref):
        @pl.when(pl.program_id(0) == 0)
        def _():
            o_ref[...] = jnp.zeros_like(o_ref)
        at = a_ref[...].T
        for chip, oc, part, pc, w in pieces:
            o_ref[chip, :, oc:oc + w] += _dot(at, b_ref[part, :, pc:pc + w])

    return pl.pallas_call(
        kern, name=name, grid=(S // ts,),
        in_specs=[pl.BlockSpec((ts, K), lambda s: (s, 0)), pl.BlockSpec((3, ts, P), lambda s: (0, s, 0))],
        out_specs=pl.BlockSpec((N_CHIPS, K, n_chip), lambda s: (0, 0, 0)),
        out_shape=jax.ShapeDtypeStruct((N_CHIPS, K, n_chip), F32),
        compiler_params=_cp(("arbitrary",)),
    )(a, b3)


def mm_w_res(a, b, name, *, chips_out=False, ts=512):
    S, K = a.shape
    N = b.shape[1]
    ts = min(ts, S)
    n_chip = N // N_CHIPS

    def kern(a_ref, b_ref, o_ref):
        @pl.when(pl.program_id(0) == 0)
        def _():
            o_ref[...] = jnp.zeros_like(o_ref)
        at = a_ref[...].T
        if chips_out:
            for chip in range(N_CHIPS):
                o_ref[chip] += _dot(at, b_ref[:, chip * n_chip:(chip + 1) * n_chip])
        else:
            o_ref[...] += _dot(at, b_ref[...])

    o_shape = (N_CHIPS, K, n_chip) if chips_out else (K, N)
    return pl.pallas_call(
        kern, name=name, grid=(S // ts,),
        in_specs=[pl.BlockSpec((ts, K), lambda s: (s, 0)), pl.BlockSpec((ts, N), lambda s: (s, 0))],
        out_specs=pl.BlockSpec(o_shape, lambda s: (0,) * len(o_shape)),
        out_shape=jax.ShapeDtypeStruct(o_shape, F32),
        compiler_params=_cp(("arbitrary",)),
    )(a, b)


def mm_f32(a, b, name):
    def kern(a_ref, b_ref, o_ref):
        o_ref[...] = jnp.dot(a_ref[...], b_ref[...], precision=HIGHEST, preferred_element_type=F32)

    return pl.pallas_call(kern, name=name, out_shape=jax.ShapeDtypeStruct((a.shape[0], b.shape[1]), F32),
                          compiler_params=pltpu.CompilerParams(vmem_limit_bytes=VMEM_LIMIT))(a, b)


def _ln_bwd_tile(dxo_t, x_ref, y_ref, gp_ref, lg_ref, dz_ref, dy_ref, s_ref):
    yv = y_ref[...].astype(F32)
    z = ALPHA * x_ref[...] + gp_ref[...] * yv
    mu = jnp.mean(z, axis=-1, keepdims=True)
    zc = z - mu
    var = jnp.mean(zc * zc, axis=-1, keepdims=True)
    rstd = lax.rsqrt(var + LN_EPS)
    xhat = zc * rstd
    dxh = dxo_t * lg_ref[...]
    dz = rstd * (dxh - jnp.mean(dxh, axis=-1, keepdims=True)
                 - xhat * jnp.mean(dxh * xhat, axis=-1, keepdims=True))
    dz_ref[...] = dz
    dy_ref[...] = (gp_ref[...] * dz).astype(BF16)
    s_ref[0:1, :] += jnp.sum(dxo_t * xhat, axis=0, keepdims=True)
    s_ref[1:2, :] += jnp.sum(dxo_t, axis=0, keepdims=True)
    s_ref[2:3, :] += jnp.sum(dz * yv, axis=0, keepdims=True)


def loss_ln_bwd(x_out, target, x_in, y, gate1p, ln_g, name, *, tm=256):
    S, D = x_out.shape
    tm = min(tm, S)

    def kern(xo_ref, t_ref, x_ref, y_ref, gp_ref, lg_ref, dz_ref, dy_ref, s_ref, l_ref):
        @pl.when(pl.program_id(0) == 0)
        def _():
            s_ref[...] = jnp.zeros_like(s_ref)
            l_ref[...] = jnp.zeros_like(l_ref)
        e = xo_ref[...] - t_ref[...]
        l_ref[...] += jnp.sum(e * e)
        _ln_bwd_tile(e * (1.0 / D), x_ref, y_ref, gp_ref, lg_ref, dz_ref, dy_ref, s_ref)

    tile = pl.BlockSpec((tm, D), lambda i: (i, 0))
    return pl.pallas_call(
        kern, name=name, grid=(S // tm,),
        in_specs=[tile, tile, tile, tile, _row_spec(D), _row_spec(D)],
        out_specs=[tile, tile, pl.BlockSpec((8, D), lambda i: (0, 0)), pl.BlockSpec((8, 128), lambda i: (0, 0))],
        out_shape=[jax.ShapeDtypeStruct((S, D), F32), jax.ShapeDtypeStruct((S, D), BF16),
                   jax.ShapeDtypeStruct((8, D), F32), jax.ShapeDtypeStruct((8, 128), F32)],
        compiler_params=_cp(("arbitrary",)),
    )(x_out, target, x_in, y, gate1p, ln_g)


def _tri64():
    r = lax.broadcasted_iota(jnp.int32, (CHUNK, CHUNK), 0)
    c = lax.broadcasted_iota(jnp.int32, (CHUNK, CHUNK), 1)
    return r >= c


def _gla_chunk_common(proj_ref, rows, b, h):
    kc = slice(h * GLA_DK_HEAD, (h + 1) * GLA_DK_HEAD)
    bh = b[:, kc]
    ep = jnp.exp(bh)
    en = jnp.exp(-bh)
    bl = bh[CHUNK - 1:CHUNK, :]
    ee = jnp.exp(bl - bh)
    dec = jnp.exp(bl)
    q = proj_ref[rows, h * GLA_DK_HEAD:(h + 1) * GLA_DK_HEAD] * GLA_SCALE
    k = proj_ref[rows, GLA_DK + h * GLA_DK_HEAD:GLA_DK + (h + 1) * GLA_DK_HEAD]
    v = proj_ref[rows, 2 * GLA_DK + h * GLA_DV_HEAD:2 * GLA_DK + (h + 1) * GLA_DV_HEAD]
    g = proj_ref[rows, 2 * GLA_DK + GLA_DV + h * GLA_DV_HEAD:2 * GLA_DK + GLA_DV + (h + 1) * GLA_DV_HEAD]
    return ep, en, ee, dec, q, k, v, g


def gla_fwd(proj, wgk_p, bgk, gnorm, name, side=None):
    S = proj.shape[0]
    TB = min(GLA_TB, S)
    ncb = TB // CHUNK

    def kern(proj_ref, wgk_ref, bgk_ref, gn_ref, zg_ref, st_ref, state_scr, la_scr):
        @pl.when(pl.program_id(0) == 0)
        def _():
            state_scr[...] = jnp.zeros_like(state_scr)
        lr = proj_ref[:, GLA_LR_OFF:GLA_IN_PAD].astype(BF16)
        gk = _dot(lr, wgk_ref[...]) + bgk_ref[...]
        la_scr[...] = _log_sigmoid(gk) * (1.0 / 16.0)
        lower = _tri64()
        tri = lower.astype(F32)

        def group(gi, carry):
            GG = min(GLA_GROUP_FWD, ncb)
            rows = [pl.ds(pl.multiple_of((gi * GG + g) * CHUNK, CHUNK), CHUNK) for g in range(GG)]
            b = [jnp.dot(tri, la_scr[r, :], precision=HIGHEST, preferred_element_type=F32) for r in rows]
            P = [(g, h) for g in range(GG) for h in range(GLA_HEADS)]
            cm = {p: _gla_chunk_common(proj_ref, rows[p[0]], b[p[0]], p[1]) for p in P}
            qf = {p: (cm[p][4] * cm[p][0]).astype(BF16) for p in P}
            kn = {p: (cm[p][5] * cm[p][1]).astype(BF16) for p in P}
            qn = {p: (cm[p][4] * cm[p][1]).astype(BF16) for p in P}
            kp = {p: (cm[p][5] * cm[p][0]).astype(BF16) for p in P}
            ke = {p: (cm[p][5] * cm[p][2]).astype(BF16) for p in P}
            vb = {p: cm[p][6].astype(BF16) for p in P}
            a_f = {p: _dot_nt(qf[p], kn[p]) for p in P}
            a_b = {p: _dot_nt(qn[p], kp[p]) for p in P}
            upd = {p: _dot_tn(vb[p], ke[p]) for p in P}
            st = {(0, h): state_scr[h] for h in range(GLA_HEADS)}
            for g in range(GG):
                for h in range(GLA_HEADS):
                    st[(g + 1, h)] = st[(g, h)] * cm[(g, h)][3] + upd[(g, h)]
            o_st = {p: _dot_nt(qf[p], st[p].astype(BF16)) for p in P}
            amat = {p: jnp.where(lower, a_f[p], a_b[p]).astype(BF16) for p in P}
            o = {p: _dot(amat[p], vb[p]) + o_st[p] for p in P}
            for g, h in P:
                st_ref[gi * GG + g, h] = st[(g, h)]
            for h in range(GLA_HEADS):
                state_scr[h] = st[(GG, h)]
            for g, h in P:
                gate = cm[(g, h)][7]
                vc = slice(h * GLA_DV_HEAD, (h + 1) * GLA_DV_HEAD)
                r = lax.rsqrt(jnp.mean(o[(g, h)] * o[(g, h)], axis=-1, keepdims=True) + RMS_EPS)
                on = (o[(g, h)] * r) * gn_ref[:, vc]
                zg_ref[rows[g], vc] = (on * (gate * _sigmoid(gate))).astype(BF16)
            return carry

        lax.fori_loop(0, ncb // min(GLA_GROUP_FWD, ncb), group, 0)

    return hosted_call(
        kern, side, name=name, grid=(S // TB,),
        in_specs=[pl.BlockSpec((TB, GLA_IN_PAD), lambda i: (i, 0)),
                  pl.BlockSpec((128, GLA_DK), lambda i: (0, 0)), _row_spec(GLA_DK), _row_spec(GLA_DV)],
        out_specs=[pl.BlockSpec((TB, GLA_DV), lambda i: (i, 0)),
                   pl.BlockSpec((ncb, GLA_HEADS, GLA_DV_HEAD, GLA_DK_HEAD), lambda i: (i, 0, 0, 0))],
        out_shape=[jax.ShapeDtypeStruct((S, GLA_DV), BF16),
                   jax.ShapeDtypeStruct((S // CHUNK, GLA_HEADS, GLA_DV_HEAD, GLA_DK_HEAD), F32)],
        scratch_shapes=[pltpu.VMEM((GLA_HEADS, GLA_DV_HEAD, GLA_DK_HEAD), F32), pltpu.VMEM((TB, GLA_DK), F32)],
        dims=("arbitrary",), args=(proj, wgk_p, bgk, gnorm))


def gla_bwd(proj, states, dzg, wgk_p, bgk, gnorm, name, side=None):
    S = proj.shape[0]
    TB = min(GLA_TB, S)
    ncb = TB // CHUNK
    nb = S // TB

    def kern(proj_ref, st_ref, dzg_ref, wgk_ref, bgk_ref, gn_ref,
             dproj_ref, dwgk_ref, dbgk_ref, dgn_ref, dstate_scr, la_scr, gk_scr, dgk_scr):
        @pl.when(pl.program_id(0) == 0)
        def _():
            dstate_scr[...] = jnp.zeros_like(dstate_scr)
            dwgk_ref[...] = jnp.zeros_like(dwgk_ref)
            dbgk_ref[...] = jnp.zeros_like(dbgk_ref)
            dgn_ref[...] = jnp.zeros_like(dgn_ref)
        lr = proj_ref[:, GLA_LR_OFF:GLA_IN_PAD].astype(BF16)
        gk = _dot(lr, wgk_ref[...]) + bgk_ref[...]
        gk_scr[...] = gk
        la_scr[...] = _log_sigmoid(gk) * (1.0 / 16.0)
        lower = _tri64()
        tri = lower.astype(F32)
        r_i = lax.broadcasted_iota(jnp.int32, (CHUNK, CHUNK), 0)
        c_i = lax.broadcasted_iota(jnp.int32, (CHUNK, CHUNK), 1)
        triu = (c_i >= r_i).astype(F32)
        last_row = lax.broadcasted_iota(jnp.int32, (CHUNK, GLA_DK_HEAD), 0) == CHUNK - 1

        def group(gi, carry):
            cs = [ncb - 1 - (gi * GLA_GROUP + g) for g in range(GLA_GROUP)]
            rows = [pl.ds(pl.multiple_of(c * CHUNK, CHUNK), CHUNK) for c in cs]
            b = [jnp.dot(tri, la_scr[r, :], precision=HIGHEST, preferred_element_type=F32) for r in rows]
            P = [(g, h) for g in range(GLA_GROUP) for h in range(GLA_HEADS)]
            kcs = [slice(h * GLA_DK_HEAD, (h + 1) * GLA_DK_HEAD) for h in range(GLA_HEADS)]
            vcs = [slice(h * GLA_DV_HEAD, (h + 1) * GLA_DV_HEAD) for h in range(GLA_HEADS)]
            cm = {p: _gla_chunk_common(proj_ref, rows[p[0]], b[p[0]], p[1]) for p in P}
            ep, en, ee, dec = ({p: cm[p][i] for p in P} for i in range(4))
            qf = {p: cm[p][4] * cm[p][0] for p in P}
            kn = {p: cm[p][5] * cm[p][1] for p in P}
            qn = {p: cm[p][4] * cm[p][1] for p in P}
            kp = {p: cm[p][5] * cm[p][0] for p in P}
            ke = {p: cm[p][5] * cm[p][2] for p in P}
            qf_b, kn_b, qn_b, kp_b, ke_b = ({p: t[p].astype(BF16) for p in P} for t in (qf, kn, qn, kp, ke))
            vb = {p: cm[p][6].astype(BF16) for p in P}
            st = {p: st_ref[cs[p[0]], p[1]] for p in P}
            st_b = {p: st[p].astype(BF16) for p in P}
            a_f = {p: _dot_nt(qf_b[p], kn_b[p]) for p in P}
            a_b = {p: _dot_nt(qn_b[p], kp_b[p]) for p in P}
            o_st = {p: _dot_nt(qf_b[p], st_b[p]) for p in P}
            amat = {p: jnp.where(lower, a_f[p], a_b[p]).astype(BF16) for p in P}
            o = {p: _dot(amat[p], vb[p]) + o_st[p] for p in P}
            do_b, dgs = {}, {}
            for p in P:
                g, h = p
                r = lax.rsqrt(jnp.mean(o[p] * o[p], axis=-1, keepdims=True) + RMS_EPS)
                oh = o[p] * r
                gn = gn_ref[:, vcs[h]]
                gate = cm[p][7]
                sg = _sigmoid(gate)
                dz = dzg_ref[rows[g], vcs[h]]
                don = dz * (gate * sg)
                dgs[p] = dz * (oh * gn) * (sg * (1.0 + gate * (1.0 - sg)))
                dgn_ref[:, vcs[h]] += jnp.sum(don * oh, axis=0, keepdims=True)
                doh = don * gn
                do_b[p] = (r * (doh - oh * jnp.mean(doh * oh, axis=-1, keepdims=True))).astype(BF16)
            da = {p: _dot_nt(do_b[p], vb[p]) for p in P}
            dv_a = {p: _dot_tn(amat[p], do_b[p]) for p in P}
            dqf_st = {p: _dot(do_b[p], st_b[p]) for p in P}
            dst_upd = {p: _dot_tn(do_b[p], qf_b[p]) for p in P}
            dst = {(0, h): dstate_scr[h] for h in range(GLA_HEADS)}
            for g in range(GLA_GROUP):
                for h in range(GLA_HEADS):
                    dst[(g + 1, h)] = dst[(g, h)] * dec[(g, h)] + dst_upd[(g, h)]
            for h in range(GLA_HEADS):
                dstate_scr[h] = dst[(GLA_GROUP, h)]
            dst_b = {p: dst[p].astype(BF16) for p in P}
            dv = {p: dv_a[p] + _dot_nt(ke_b[p], dst_b[p]) for p in P}
            dke = {p: _dot(vb[p], dst_b[p]) for p in P}
            da_f = {p: jnp.where(lower, da[p], 0.0).astype(BF16) for p in P}
            da_b = {p: jnp.where(lower, 0.0, da[p]).astype(BF16) for p in P}
            dqf = {p: _dot(da_f[p], kn_b[p]) + dqf_st[p] for p in P}
            dkn = {p: _dot_tn(da_f[p], qf_b[p]) for p in P}
            dqn = {p: _dot(da_b[p], kp_b[p]) for p in P}
            dkp = {p: _dot_tn(da_b[p], qn_b[p]) for p in P}
            dbs = {}
            for p in P:
                ddec = jnp.sum(dst[p] * st[p], axis=0, keepdims=True)
                db = dqf[p] * qf[p] - dkn[p] * kn[p] - dqn[p] * qn[p] + dkp[p] * kp[p] - dke[p] * ke[p]
                dbl = jnp.sum(dke[p] * ke[p], axis=0, keepdims=True) + ddec * dec[p]
                dbs[p] = db + jnp.where(last_row, dbl, 0.0)
            dla = {p: jnp.dot(triu, dbs[p], precision=HIGHEST, preferred_element_type=F32) for p in P}
            for p in P:
                g, h = p
                dq = (dqf[p] * ep[p] + dqn[p] * en[p]) * GLA_SCALE
                dk = dkn[p] * en[p] + dkp[p] * ep[p] + dke[p] * ee[p]
                dgk_scr[rows[g], kcs[h]] = dla[p] * (1.0 / 16.0) * _sigmoid(-gk_scr[rows[g], kcs[h]])
                dproj_ref[rows[g], kcs[h]] = dq.astype(BF16)
                dproj_ref[rows[g], GLA_DK + h * GLA_DK_HEAD:GLA_DK + (h + 1) * GLA_DK_HEAD] = dk.astype(BF16)
                dproj_ref[rows[g], 2 * GLA_DK + h * GLA_DV_HEAD:2 * GLA_DK + (h + 1) * GLA_DV_HEAD] = dv[p].astype(BF16)
                dproj_ref[rows[g], 2 * GLA_DK + GLA_DV + h * GLA_DV_HEAD:
                          2 * GLA_DK + GLA_DV + (h + 1) * GLA_DV_HEAD] = dgs[p].astype(BF16)
            return carry

        lax.fori_loop(0, ncb // GLA_GROUP, group, 0)
        dgk = dgk_scr[...]
        dgk_b = dgk.astype(BF16)
        dproj_ref[:, GLA_LR_OFF:GLA_IN_PAD] = _dot_nt(dgk_b, wgk_ref[...]).astype(BF16)
        dwgk_ref[...] += _dot_tn(lr, dgk_b)
        dbgk_ref[...] += jnp.sum(dgk, axis=0, keepdims=True)

    rev = lambda i: (nb - 1 - i, 0)
    return hosted_call(
        kern, side, name=name, grid=(nb,),
        in_specs=[pl.BlockSpec((TB, GLA_IN_PAD), rev),
                  pl.BlockSpec((ncb, GLA_HEADS, GLA_DV_HEAD, GLA_DK_HEAD), lambda i: (nb - 1 - i, 0, 0, 0)),
                  pl.BlockSpec((TB, GLA_DV), rev),
                  pl.BlockSpec((128, GLA_DK), lambda i: (0, 0)), _row_spec(GLA_DK), _row_spec(GLA_DV)],
        out_specs=[pl.BlockSpec((TB, GLA_IN_PAD), rev),
                   pl.BlockSpec((128, GLA_DK), lambda i: (0, 0)), _row_spec(GLA_DK), _row_spec(GLA_DV)],
        out_shape=[jax.ShapeDtypeStruct((S, GLA_IN_PAD), BF16), jax.ShapeDtypeStruct((128, GLA_DK), F32),
                   jax.ShapeDtypeStruct((1, GLA_DK), F32), jax.ShapeDtypeStruct((1, GLA_DV), F32)],
        scratch_shapes=[pltpu.VMEM((GLA_HEADS, GLA_DV_HEAD, GLA_DK_HEAD), F32), pltpu.VMEM((TB, GLA_DK), F32),
                        pltpu.VMEM((TB, GLA_DK), F32), pltpu.VMEM((TB, GLA_DK), F32)],
        dims=("arbitrary",), args=(proj, states, dzg, wgk_p, bgk, gnorm))


ATT_TW = 1024
ATT_CLASSES = 3


def _att_window(i):
    return pl.multiple_of(jnp.maximum(i * ATT_TQ - LEFT_CHUNKS * CHUNK, 0), ATT_TQ)


def _att_rel_index():
    e = jnp.arange(ATT_TW)[None, :]
    d = jnp.where(e < ATT_KW, e, e - ATT_TW)
    off = (jnp.arange(ATT_CLASSES) * ATT_TQ)[:, None]
    return jnp.clip(off - d, -MAX_REL, MAX_REL) + MAX_REL


def _row_bits():
    return lax.broadcasted_iota(jnp.int32, (ATT_TQ, ATT_TW), 0)


def att_bias_tiles(rel_bias, name):
    pick = (jnp.arange(384)[:, None] == _att_rel_index().reshape(-1)[None, :]).astype(F32)
    tab = mm_f32(jnp.pad(rel_bias, ((0, 0), (0, 384 - N_REL))), pick, name + "_tab")
    tab = tab.reshape(ATT_HEADS * ATT_CLASSES, 1, ATT_TW)

    def kern(t_ref, o_ref):
        cls = pl.program_id(0) % ATT_CLASSES
        x = jnp.broadcast_to(t_ref[...], (ATT_TQ, ATT_TW))
        x = pltpu.roll(x, 0, 1, stride=1, stride_axis=0)
        x = x[:, :ATT_KW]
        qc = cls * (ATT_TQ // CHUNK) + lax.shift_right_arithmetic(
            lax.broadcasted_iota(jnp.int32, (ATT_TQ, ATT_KW), 0), 6)
        kc = lax.shift_right_arithmetic(lax.broadcasted_iota(jnp.int32, (ATT_TQ, ATT_KW), 1), 6)
        o_ref[...] = jnp.where((kc <= qc) & (kc >= qc - LEFT_CHUNKS), x, NEG_INF)

    return pl.pallas_call(
        kern, name=name, grid=(ATT_HEADS * ATT_CLASSES,),
        in_specs=[pl.BlockSpec((None, 1, ATT_TW), lambda i: (i, 0, 0))],
        out_specs=pl.BlockSpec((None, ATT_TQ, ATT_KW), lambda i: (i, 0, 0)),
        out_shape=jax.ShapeDtypeStruct((ATT_HEADS * ATT_CLASSES, ATT_TQ, ATT_KW), F32),
        compiler_params=_cp(("parallel",)),
    )(tab)


def att_bias_grad(dbt, name):
    def kern(d_ref, o_ref):
        x = jnp.concatenate([d_ref[...], jnp.zeros((ATT_TQ, ATT_TW - ATT_KW), F32)], axis=1)
        x = jnp.concatenate([pltpu.roll(x[r0:r0 + 8, :], (ATT_TW - r0) % ATT_TW, axis=1)
                             for r0 in range(0, ATT_TQ, 8)], axis=0)
        row = _row_bits()
        for b in range(3):
            x = jnp.where((row & (1 << b)) != 0, pltpu.roll(x, ATT_TW - (1 << b), axis=1), x)
        o_ref[...] = jnp.sum(x, axis=0, keepdims=True)

    diag = pl.pallas_call(
        kern, name=name + "_diag", grid=(ATT_HEADS * ATT_CLASSES,),
        in_specs=[pl.BlockSpec((None, ATT_TQ, ATT_KW), lambda i: (i, 0, 0))],
        out_specs=pl.BlockSpec((None, 1, ATT_TW), lambda i: (i, 0, 0)),
        out_shape=jax.ShapeDtypeStruct((ATT_HEADS * ATT_CLASSES, 1, ATT_TW), F32),
        compiler_params=_cp(("parallel",)),
    )(dbt)
    diag = diag.reshape(ATT_HEADS, ATT_CLASSES * ATT_TW)
    onehot = (_att_rel_index().reshape(-1)[:, None] == jnp.arange(384)[None, :]).astype(F32)
    return mm_f32(diag, onehot, name + "_bins")[:, :N_REL]


ATT_GROUP = 2


def _att_scores(q_ref, k_ref, bias_refs, blk0):
    G = range(ATT_GROUP)
    hs = [slice(hh * ATT_HD, (hh + 1) * ATT_HD) for hh in range(2)]
    rows = [slice(g * ATT_TQ, (g + 1) * ATT_TQ) for g in G]
    wins = [pl.ds(_att_window(blk0 + g), ATT_KW) for g in G]
    kw = [k_ref[w, :] for w in wins]
    P = [(g, hh) for g in G for hh in range(2)]
    q = {p: q_ref[rows[p[0]], hs[p[1]]] * ATT_SCALE for p in P}
    k = {p: kw[p[0]][:, hs[p[1]]] for p in P}
    s = {p: _dot_nt(q[p], k[p]) + bias_refs[p[0]][p[1]] for p in P}
    e = {p: jnp.exp(s[p] - jnp.max(s[p], axis=-1, keepdims=True)) for p in P}
    inv = {p: 1.0 / jnp.sum(e[p], axis=-1, keepdims=True) for p in P}
    return P, rows, wins, hs, q, k, e, inv


def _att_specs(S):
    nq = D_MODEL // 128
    q_spec = pl.BlockSpec((ATT_GROUP * ATT_TQ, 128), lambda p, i: (i, p))
    k_spec = pl.BlockSpec((S, 128), lambda p, i: (0, nq + p))
    v_spec = pl.BlockSpec((S, 128), lambda p, i: (0, 2 * nq + p))
    b_specs = [pl.BlockSpec((2, None, ATT_TQ, ATT_KW),
                            functools.partial(lambda p, i, g: (p, jnp.minimum(ATT_GROUP * i + g, ATT_CLASSES - 1), 0, 0), g=g))
               for g in range(ATT_GROUP)]
    return q_spec, k_spec, v_spec, b_specs


def attn_fwd(qkv, bias, name, side=None):
    S = qkv.shape[0]
    q_spec, k_spec, v_spec, b_specs = _att_specs(S)

    def kern(q_ref, k_ref, v_ref, *rest):
        bias_refs, o_ref = rest[:ATT_GROUP], rest[ATT_GROUP]
        P, rows, wins, hs, _, _, e, inv = _att_scores(q_ref, k_ref, bias_refs, ATT_GROUP * pl.program_id(1))
        vw = [v_ref[w, :] for w in wins]
        o = {p: _dot(e[p].astype(BF16), vw[p[0]][:, hs[p[1]]]) * inv[p] for p in P}
        for g in range(ATT_GROUP):
            o_ref[rows[g], :] = jnp.concatenate([o[(g, 0)], o[(g, 1)]], axis=1).astype(BF16)

    return hosted_call(
        kern, side, name=name, grid=(ATT_HEADS // 2, S // (ATT_GROUP * ATT_TQ)),
        in_specs=[q_spec, k_spec, v_spec] + b_specs,
        out_specs=[pl.BlockSpec((ATT_GROUP * ATT_TQ, 128), lambda p, i: (i, p))],
        out_shape=[jax.ShapeDtypeStruct((S, D_MODEL), BF16)],
        scratch_shapes=[], dims=("parallel", "arbitrary"), args=(qkv, qkv, qkv) + (bias,) * ATT_GROUP)


def attn_bwd(qkv, bias, do, o, name, side=None):
    S = qkv.shape[0]
    nstep = S // (ATT_GROUP * ATT_TQ)
    q_spec, k_spec, v_spec, b_specs = _att_specs(S)

    def kern(q_ref, k_ref, v_ref, *rest):
        bias_refs = rest[:ATT_GROUP]
        do_ref, o_ref, dqkv_ref, db_ref, dk_scr, dv_scr = rest[ATT_GROUP:]
        i = pl.program_id(1)

        @pl.when(i == 0)
        def _():
            dk_scr[...] = jnp.zeros_like(dk_scr)
            dv_scr[...] = jnp.zeros_like(dv_scr)
            db_ref[...] = jnp.zeros_like(db_ref)
        blk0 = ATT_GROUP * i
        P, rows, wins, hs, q, k, e, inv = _att_scores(q_ref, k_ref, bias_refs, blk0)
        vw = [v_ref[w, :] for w in wins]
        do_h = {p: do_ref[rows[p[0]], hs[p[1]]] for p in P}
        dp = {p: _dot_nt(do_h[p], vw[p[0]][:, hs[p[1]]]) for p in P}
        pr = {p: e[p] * inv[p] for p in P}
        dvs = {p: _dot_tn(pr[p].astype(BF16), do_h[p]) for p in P}
        delta = {p: jnp.sum(do_h[p].astype(F32) * o_ref[rows[p[0]], hs[p[1]]].astype(F32), axis=-1, keepdims=True)
                 for p in P}
        ds = {p: pr[p] * (dp[p] - delta[p]) for p in P}
        ds_b = {p: ds[p].astype(BF16) for p in P}
        dqs = {p: _dot(ds_b[p], k[p]) * ATT_SCALE for p in P}
        dks = {p: _dot_tn(ds_b[p], q[p]) for p in P}
        for g, hh in P:
            db_ref[hh, jnp.minimum(blk0 + g, ATT_CLASSES - 1)] += ds[(g, hh)]
        for g in range(ATT_GROUP):
            first = pl.multiple_of((blk0 + g) * ATT_TQ, ATT_TQ)
            dqkv_ref[0, pl.ds(first, ATT_TQ), :] = jnp.concatenate([dqs[(g, 0)], dqs[(g, 1)]], axis=1).astype(BF16)
            dk_scr[wins[g], :] += jnp.concatenate([dks[(g, 0)], dks[(g, 1)]], axis=1)
            dv_scr[wins[g], :] += jnp.concatenate([dvs[(g, 0)], dvs[(g, 1)]], axis=1)

        @pl.when(i == nstep - 1)
        def _():
            dqkv_ref[1] = dk_scr[...].astype(BF16)
            dqkv_ref[2] = dv_scr[...].astype(BF16)

    return hosted_call(
        kern, side, name=name, grid=(ATT_HEADS // 2, nstep),
        in_specs=[q_spec, k_spec, v_spec] + b_specs + [pl.BlockSpec((ATT_GROUP * ATT_TQ, 128), lambda p, i: (i, p))] * 2,
        out_specs=[pl.BlockSpec((3, S, 128), lambda p, i: (0, 0, p)),
                   pl.BlockSpec((2, ATT_CLASSES, ATT_TQ, ATT_KW), lambda p, i: (p, 0, 0, 0))],
        out_shape=[jax.ShapeDtypeStruct((3, S, D_MODEL), BF16),
                   jax.ShapeDtypeStruct((ATT_HEADS, ATT_CLASSES, ATT_TQ, ATT_KW), F32)],
        scratch_shapes=[pltpu.VMEM((S, 128), F32), pltpu.VMEM((S, 128), F32)],
        dims=("parallel", "arbitrary"), args=(qkv, qkv, qkv) + (bias,) * ATT_GROUP + (do, o))


def _me():
    return lax.axis_index("x"), lax.axis_index("y"), lax.axis_index("c")


def _other_chips(x, y):
    return [(1 - x, y), (x, 1 - y), (1 - x, 1 - y)]


def all_gather8(x_shard, name):
    m_per, n = x_shard.shape

    def body(x_ref, out_ref, send_sems, recv_sems, local_sem):
        x, y, c = _me()
        me, sibling = (x, y, c), (x, y, 1 - c)
        chips = _other_chips(x, y)

        def rows(px, py, pc):
            return out_ref.at[pl.ds((4 * px + 2 * py + pc) * m_per, m_per), :]

        def copy(k, block, to, src=None):
            return pltpu.make_async_remote_copy(
                src_ref=rows(*block) if src is None else src, dst_ref=rows(*block),
                send_sem=send_sems.at[k], recv_sem=recv_sems.at[k], device_id=to, device_id_type=MESH)

        mine = pltpu.make_async_copy(x_ref, rows(*me), local_sem)
        mine.start()
        first = [copy(0, me, sibling, src=x_ref)]
        first += [copy(1 + j, me, (*chip, c), src=x_ref) for j, chip in enumerate(chips)]
        for cp in first:
            cp.start()
        passed = [copy(4 + j, (*chip, c), sibling) for j, chip in enumerate(chips)]
        for j, chip in enumerate(chips):
            copy(1 + j, (*chip, c), me).wait_recv()
            passed[j].start()
        copy(0, sibling, me).wait_recv()
        for j, chip in enumerate(chips):
            copy(4 + j, (*chip, 1 - c), me).wait_recv()
        for cp in first + passed:
            cp.wait_send()
        mine.wait()

    return pl.pallas_call(
        body, name=name,
        out_shape=jax.ShapeDtypeStruct((N_DEV * m_per, n), x_shard.dtype),
        in_specs=[pl.BlockSpec(memory_space=pltpu.VMEM)],
        out_specs=pl.BlockSpec(memory_space=pltpu.VMEM),
        scratch_shapes=[pltpu.SemaphoreType.DMA((7,)), pltpu.SemaphoreType.DMA((7,)), pltpu.SemaphoreType.DMA],
        compiler_params=pltpu.CompilerParams(vmem_limit_bytes=VMEM_LIMIT),
    )(x_shard)


def _half_rows(n_rows, c):
    h = n_rows // 2
    return pl.ds(c * h, h)


def _gathered_shape(shape, flavour):
    L, a, b = shape
    return {"col": (L, a, N_CHIPS * b), "row": (L, N_CHIPS * a, b), "lead": (N_CHIPS, L, a, b)}[flavour]


def _gathered_part(out_ref, shape, flavour, s, rows):
    L, a, b = shape
    if flavour == "col":
        return out_ref.at[:, rows, pl.ds(s * b, b)]
    if flavour == "row":
        return out_ref.at[:, pl.ds(s * a + rows.start, rows.size), :]
    return out_ref.at[s, :, rows, :]


def gather_side(shards, flavours):
    n = len(shards)
    shapes = [w.shape for w in shards]

    def copies(w_refs, out_refs, send_sems, recv_sems, local_sems):
        x, y, c = _me()
        sibling = (x, y, 1 - c)
        chips = _other_chips(x, y)
        me_s = 2 * x + y

        def copy(k, src, dst, to):
            return pltpu.make_async_remote_copy(src_ref=src, dst_ref=dst, send_sem=send_sems.at[k],
                                                recv_sem=recv_sems.at[k], device_id=to, device_id_type=MESH)

        own, first, landed, passed, passed_in = [], [], [], [], []
        for w in range(n):
            shp, fl = shapes[w], flavours[w]
            my_half = _half_rows(shp[1], c)
            sib_half = _half_rows(shp[1], 1 - c)
            own.append(copy(7 * w + 6, w_refs[w], _gathered_part(out_refs[w], shp, fl, me_s, pl.ds(0, shp[1])), sibling))
            for j, chip in enumerate(chips):
                s = 2 * chip[0] + chip[1]
                first.append(copy(7 * w + j, w_refs[w].at[:, my_half, :],
                                  _gathered_part(out_refs[w], shp, fl, me_s, my_half), (*chip, c)))
                part = _gathered_part(out_refs[w], shp, fl, s, my_half)
                landed.append(copy(7 * w + j, part, part, (*chip, c)))
                passed.append(copy(7 * w + 3 + j, part, part, sibling))
                theirs = _gathered_part(out_refs[w], shp, fl, s, sib_half)
                passed_in.append(copy(7 * w + 3 + j, theirs, theirs, sibling))
        return own, first, landed, passed, passed_in

    def start(*refs):
        own, first, _, _, _ = copies(*refs)
        for cp in first + own:
            cp.start()

    def wait(*refs):
        own, first, landed, passed, passed_in = copies(*refs)
        for arrived, onward in zip(landed, passed):
            arrived.wait_recv()
            onward.start()
        for cp in passed_in:
            cp.wait_recv()
        for cp in own:
            cp.wait()
        for cp in first + passed:
            cp.wait_send()

    out_shapes = [jax.ShapeDtypeStruct(_gathered_shape(s, f), w.dtype) for w, s, f in zip(shards, shapes, flavours)]
    return Side(shards, out_shapes, 7 * n, 0, start, wait)


def swap_side(gs):
    n = len(gs)

    def copies(g_refs, out_refs, send_sems, recv_sems, local_sems):
        x, y, c = _me()
        return [pltpu.make_async_remote_copy(
            src_ref=g_refs[w].at[:, _half_rows(gs[w].shape[1], 1 - c), :], dst_ref=out_refs[w],
            send_sem=send_sems.at[w], recv_sem=recv_sems.at[w], device_id=(x, y, 1 - c), device_id_type=MESH)
            for w in range(n)]

    def start(*refs):
        for cp in copies(*refs):
            cp.start()

    def wait(*refs):
        for cp in copies(*refs):
            cp.wait()

    out_shapes = [jax.ShapeDtypeStruct((g.shape[0], g.shape[1] // 2, g.shape[2]), g.dtype) for g in gs]
    return Side(gs, out_shapes, n, 0, start, wait)


def add_half(g, r1, c_idx, name):
    n, R, C = g.shape
    half = R // 2
    tr = _rows_block(half, C)
    nbh = half // tr

    def kern(c_ref, g_ref, r_ref, o_ref, ob_ref):
        p = g_ref[...] + r_ref[...]
        o_ref[...] = p
        ob_ref[...] = p.astype(BF16)

    spec = pl.BlockSpec((1, tr, C), lambda d, r, c_ref: (d, r, 0))
    return pl.pallas_call(
        kern, name=name,
        grid_spec=pltpu.PrefetchScalarGridSpec(
            num_scalar_prefetch=1, grid=(n, nbh),
            in_specs=[pl.BlockSpec((1, tr, C), lambda d, r, c_ref: (d, c_ref[0] * nbh + r, 0)), spec],
            out_specs=[spec, spec]),
        out_shape=[jax.ShapeDtypeStruct((n, half, C), F32), jax.ShapeDtypeStruct((n, half, C), BF16)],
        compiler_params=_cp(("parallel", "parallel")),
    )(c_idx, g, r1)


def exchange_side(ps):
    n = len(ps)

    def copies(p_refs, out_refs, send_sems, recv_sems, local_sems):
        x, y, c = _me()
        return [pltpu.make_async_remote_copy(
            src_ref=p_refs[w].at[2 * chip[0] + chip[1]], dst_ref=out_refs[w].at[j],
            send_sem=send_sems.at[3 * w + j], recv_sem=recv_sems.at[3 * w + j],
            device_id=(*chip, c), device_id_type=MESH)
            for w in range(n) for j, chip in enumerate(_other_chips(x, y))]

    def start(*refs):
        for cp in copies(*refs):
            cp.start()

    def wait(*refs):
        for cp in copies(*refs):
            cp.wait()

    return Side(ps, [jax.ShapeDtypeStruct((3,) + p.shape[1:], p.dtype) for p in ps], 3 * n, 0, start, wait)


def add_chips(p, r2, chip_idx, name):
    n, H, C = p.shape
    tr = _rows_block(H, C)

    def kern(s_ref, p_ref, r_ref, o_ref):
        o_ref[...] = ((p_ref[0] + r_ref[0].astype(F32)) + r_ref[1].astype(F32)) + r_ref[2].astype(F32)

    return pl.pallas_call(
        kern, name=name,
        grid_spec=pltpu.PrefetchScalarGridSpec(
            num_scalar_prefetch=1, grid=(H // tr,),
            in_specs=[pl.BlockSpec((1, tr, C), lambda r, s_ref: (s_ref[0], r, 0)),
                      pl.BlockSpec((3, tr, C), lambda r, s_ref: (0, r, 0))],
            out_specs=pl.BlockSpec((tr, C), lambda r, s_ref: (r, 0))),
        out_shape=jax.ShapeDtypeStruct((H, C), F32),
        compiler_params=_cp(("parallel",)),
    )(chip_idx, p, r2)


def swap_reduced(ss, name):
    n = len(ss)

    def body(*refs):
        s_refs, out_refs = refs[:n], refs[n:2 * n]
        send_sems, recv_sems = refs[2 * n:]
        x, y, c = _me()
        cps = [pltpu.make_async_remote_copy(src_ref=s_refs[w], dst_ref=out_refs[w], send_sem=send_sems.at[w],
                                            recv_sem=recv_sems.at[w], device_id=(x, y, 1 - c), device_id_type=MESH)
               for w in range(n)]
        for cp in cps:
            cp.start()
        for cp in cps:
            cp.wait()

    any_spec = pl.BlockSpec(memory_space=pl.ANY)
    return pl.pallas_call(
        body, name=name, out_shape=[jax.ShapeDtypeStruct(s.shape, s.dtype) for s in ss],
        in_specs=[any_spec] * n, out_specs=[any_spec] * n,
        scratch_shapes=[pltpu.SemaphoreType.DMA((n,)), pltpu.SemaphoreType.DMA((n,))],
    )(*ss)


BIG = (("gla_w_in", 2, (1024, GLA_IN // N_CHIPS), "lead"), ("gla_w_out", 2, (256, 1024), "row"),
       ("att_w_in", 2, (1024, 768), "col"), ("att_w_out", 2, (256, 1024), "row"),
       ("ff_w1", 4, (1024, 1024), "col"), ("ff_w2", 4, (1024, 1024), "row"))
FLAVOUR = {n: f for n, _, _, f in BIG}


def layer_weights(i):
    mixer = "gla" if i % 2 == 0 else "att"
    return (("in", mixer + "_w_in", i // 2), ("out", mixer + "_w_out", i // 2), ("w1", "ff_w1", i), ("w2", "ff_w2", i))


class Comm:
    def __init__(self, weights, core, chip):
        self.weights, self.core, self.chip = weights, core, chip
        self.c_idx = jnp.reshape(core, (1,)).astype(jnp.int32)
        self.chip_idx = jnp.reshape(chip, (1,)).astype(jnp.int32)
        self.reduced = {}

    def gather(self, items):
        shards = [self.weights[n][l:l + 1].astype(BF16) for _, n, l in items]
        return gather_side(shards, [FLAVOUR[n] for _, n, _ in items])

    def full_weights(self, items, gathered):
        W = {}
        for (role, n, _), w in zip(items, gathered):
            if n == "gla_w_in":
                w = jnp.pad(w.transpose(1, 2, 0, 3).reshape(1, D_MODEL, GLA_IN), ((0, 0), (0, 0), (0, GLA_IN_PAD - GLA_IN)))
            W[role] = (w, 0)
        return W

    def gather_now(self, items, name):
        return self.full_weights(items, run_side(self.gather(items), name))

    def swap(self, items):
        return swap_side([g for _, _, g in items])

    def reduce_begin(self, tag, items, swapped):
        ps = [add_half(g, r, self.c_idx, f"rs_add2_{tag}_{w}") for w, ((_, _, g), r) in enumerate(zip(items, swapped))]
        return tag, [(n, l) for n, l, _ in items], ps

    def exchange(self, pending):
        return exchange_side([pb for _, pb in pending[2]])

    def reduce_mid(self, pending, landed):
        tag, keys, ps = pending
        for w, (key, (p, _), r) in enumerate(zip(keys, ps, landed)):
            self.reduced[key] = add_chips(p, r, self.chip_idx, f"rs_add4_{tag}_{w}")

    def reduce_tail(self, tag, items):
        pending = self.reduce_begin(tag, items, run_side(self.swap(items), f"rs_swap_{tag}"))
        self.reduce_mid(pending, run_side(self.exchange(pending), f"rs_xchg_{tag}"))

    def reduce_end(self):
        keys = [(n, l) for n, L, _, _ in BIG for l in range(L)]
        mine = [self.reduced[k] for k in keys]
        theirs = swap_reduced(mine, "rs_join")
        low = self.core == 0
        full = {k: jnp.concatenate([jnp.where(low, m, t), jnp.where(low, t, m)], axis=0)
                for k, m, t in zip(keys, mine, theirs)}
        return {n: jnp.stack([full[(n, l)] for l in range(L)]) for n, L, _, _ in BIG}


def local_step(x, target, mods, comm, small):
    S, D = x.shape
    row = lambda v: v.reshape(1, -1)
    saved = []
    tiles = [att_bias_tiles(small["att_rel_bias"][j], f"att_tiles_{j}").reshape(ATT_HEADS, ATT_CLASSES, ATT_TQ, ATT_KW)
             for j in range(2)]
    wgk_p = [jnp.pad(small["gla_w_gk2"][j], ((0, 128 - GLA_RANK), (0, 0))).astype(BF16) for j in range(2)]

    u1 = modulate(x, row(mods[0, 1]), row(mods[0, 0]), "mod_first")
    Ws = [dict() for _ in range(DEPTH)]
    items0 = layer_weights(0)
    Ws[0].update(comm.gather_now(items0[:1], "gather_w0"))
    for i in range(DEPTH):
        j = i // 2
        W = Ws[i]
        sh1, sc1, g1, sh2, sc2, g2 = (row(mods[i, k]) for k in range(6))
        nxt = min(i + 1, DEPTH - 1)
        more = i + 1 < DEPTH
        nxt_items = layer_weights(nxt)
        in_items = list(items0[1:3]) if i == 0 else []
        mix_items = (list(items0[3:]) if i == 0 else []) + (list(nxt_items[:2]) if more else [])
        up_items = list(nxt_items[2:3]) if more else []
        down_items = list(nxt_items[3:]) if more else []

        def hosted(items):
            return comm.gather(items) if items else None

        def landed_weights(items, landed):
            for k, it in enumerate(items):
                layer = 0 if it in items0 and i == 0 else nxt
                Ws[layer].update(comm.full_weights([it], landed[k:k + 1]))

        side = hosted(in_items)
        if i % 2 == 0:
            proj = mm_plain(u1, *W["in"], f"gla_in_{i}", side=side)
        else:
            proj = mm_plain(u1, *W["in"], f"att_in_{i}", mode="bf16", bias=row(small["att_b_in"][j]), side=side)
        proj, landed = proj if side is not None else (proj, [])
        landed_weights(in_items, landed)
        if i % 2 == 0:
            (zmix, states), landed = gla_fwd(proj, wgk_p[j], row(small["gla_b_gk"][j]), row(small["gla_g_norm"][j]),
                                             f"gla_fwd_{i}", hosted(mix_items))
        else:
            (zmix,), landed = attn_fwd(proj, tiles[j], f"att_fwd_{i}", hosted(mix_items))
            states = None
        landed_weights(mix_items, landed)
        (y1, x_mid, u2), _ = mm_down_ln(zmix, *W["out"], x, 1.0 + g1, row(small["ln_g"][i, 0]),
                                        row(small["ln_b"][i, 0]), sc2, sh2, f"mix_out_{i}")
        side = hosted(up_items)
        act = mm_plain(u2, *W["w1"], f"ff_up_{i}", mode="mlp_up", side=side)
        act, landed = act if side is not None else (act, [])
        landed_weights(up_items, landed)
        (y2, x_out, u_next), landed = mm_down_ln(act, *W["w2"], x_mid, 1.0 + g2, row(small["ln_g"][i, 1]),
                                                 row(small["ln_b"][i, 1]), row(mods[nxt, 1]), row(mods[nxt, 0]),
                                                 f"ff_out_{i}", side=hosted(down_items))
        landed_weights(down_items, landed)
        saved.append(dict(x_in=x, u1=u1, proj=proj, zmix=zmix, states=states, y1=y1, x_mid=x_mid, u2=u2,
                          act=act, y2=y2))
        x, u1 = x_out, u_next

    g_small = dict(ln_g=[None] * DEPTH, ln_b=[None] * DEPTH, gla_w_gk2=[None] * 2, gla_b_gk=[None] * 2,
                   gla_g_norm=[None] * 2, att_b_in=[None] * 2, att_rel_bias=[None] * 2)
    dmods = [None] * DEPTH
    later = []
    top = saved[DEPTH - 1]
    dz2, dy2, s_ln2, sq = loss_ln_bwd(x, target, top["x_mid"], top["y2"], 1.0 + row(mods[DEPTH - 1, 5]),
                                      row(small["ln_g"][DEPTH - 1, 1]), "loss_ln_bwd")

    for i in reversed(range(DEPTH)):
        j = i // 2
        sv = saved[i]
        W = Ws[i]
        sh1, sc1, g1, sh2, sc2, g2 = (row(mods[i, k]) for k in range(6))
        dh = mm_plain(dy2, *W["w2"], f"ff_dn_{i}", mode="mlp_dn", nt=True, h=sv["act"])
        g_w2 = mm_w_res(sv["act"], dy2, f"ff_w2g_{i}").reshape(N_CHIPS, D_FF // N_CHIPS, D)
        g_w1 = mm_w_res(sv["u2"], dh, f"ff_w1g_{i}", chips_out=True)
        items = [("ff_w1", i, g_w1), ("ff_w2", i, g_w2)] + later
        (dz1, dy1, s_m2, s_ln1), swapped = mm_down_comb(
            dh, *W["w1"], dz2, sv["x_mid"], 1.0 + sc2, f"ff_dx_{i}",
            ln=(sv["x_in"], sv["y1"], 1.0 + g1, row(small["ln_g"][i, 0])), side=comm.swap(items))
        pending = comm.reduce_begin(i, items, swapped)
        side = comm.exchange(pending)
        mixer = "gla" if i % 2 == 0 else "att"
        below = None
        if i > 0:
            below = (saved[i - 1]["x_mid"], saved[i - 1]["y2"], 1.0 + row(mods[i - 1, 5]), row(small["ln_g"][i - 1, 1]))
        if i % 2 == 0:
            g_out = mm_w_res(sv["zmix"], dy1, f"gla_wog_{i}", ts=1024).reshape(N_CHIPS, D // N_CHIPS, D)
            dzg = mm_plain(dy1, *W["out"], f"gla_dz_{i}", nt=True)
            (dproj, dwgk, dbgk, dgn), landed = gla_bwd(sv["proj"], sv["states"], dzg, wgk_p[j],
                                                       row(small["gla_b_gk"][j]), row(small["gla_g_norm"][j]),
                                                       f"gla_bwd_{i}", side)
            g_small["gla_w_gk2"][j] = dwgk[:GLA_RANK]
            g_small["gla_b_gk"][j] = dbgk[0]
            g_small["gla_g_norm"][j] = dgn[0].reshape(GLA_HEADS, GLA_DV_HEAD)
            gwi = mm_w_res(sv["u1"], dproj, f"gla_wig_{i}")[:, :GLA_IN]
            g_in = gwi.reshape(D, N_CHIPS, GLA_IN // N_CHIPS).transpose(1, 0, 2)
            outs, _ = mm_down_comb(dproj, *W["in"], dz1, sv["x_in"], 1.0 + sc1, f"mix_dx_{i}", ln=below)
        else:
            g_out = mm_w_res(sv["zmix"], dy1, f"att_wog_{i}", ts=1024).reshape(N_CHIPS, D // N_CHIPS, D)
            do = mm_plain(dy1, *W["out"], f"att_do_{i}", mode="bf16", nt=True)
            (dqkv, dbt), landed = attn_bwd(sv["proj"], tiles[j], do, sv["zmix"], f"att_bwd_{i}", side)
            g_small["att_rel_bias"][j] = att_bias_grad(dbt.reshape(ATT_HEADS * ATT_CLASSES, ATT_TQ, ATT_KW),
                                                       f"att_bias_{i}")
            g_in = mm_w_chips3(sv["u1"], dqkv, f"att_wig_{i}")
            outs, _ = mm_down_comb(dqkv, *W["in"], dz1, sv["x_in"], 1.0 + sc1, f"mix_dx_{i}", parts=3, ln=below)
        s_m1 = outs[1] if below is None else outs[2]
        if i % 2 == 1:
            g_small["att_b_in"][j] = s_m1[2:5].reshape(3 * D)
        comm.reduce_mid(pending, landed)
        later = [(mixer + "_w_in", j, g_in), (mixer + "_w_out", j, g_out)]
        g_small["ln_g"][i] = jnp.stack([s_ln1[0], s_ln2[0]])
        g_small["ln_b"][i] = jnp.stack([s_ln1[1], s_ln2[1]])
        dmods[i] = jnp.stack([s_m1[1], s_m1[0], s_ln1[2], s_m2[1], s_m2[0], s_ln2[2]])
        if below is None:
            dx = outs[0]
        else:
            dz2, dy2, s_ln2 = outs[0], outs[1], outs[3]
    comm.reduce_tail("last", later)

    g_small = {n: jnp.stack(v) for n, v in g_small.items()}
    return sq, dx, jnp.stack(dmods), g_small


SMALL_SHARDED = (("ln_g", (4, 2, 256)), ("ln_b", (4, 2, 256)), ("gla_g_norm", (2, 4, 64)),
                 ("gla_w_gk2", (2, 16, 128)), ("att_b_in", (2, 768)))
SMALL_FULL = dict(ln_g=(4, 2, 1024), ln_b=(4, 2, 1024), gla_g_norm=(2, 4, 256), gla_w_gk2=(2, 16, 512),
                  att_b_in=(2, 3072), gla_b_gk=(2, 512), att_rel_bias=(2, 16, 257))
SMALL_GRAD_ORDER = ("ln_g", "ln_b", "gla_g_norm", "gla_w_gk2", "att_b_in", "gla_b_gk", "att_rel_bias")


def _pack_small(arrs, rows_total):
    parts = []
    for a in arrs:
        flat = a.reshape(-1)
        pad = (-flat.shape[0]) % PACK_W
        parts.append(jnp.pad(flat, (0, pad)).reshape(-1, PACK_W))
    buf = jnp.concatenate(parts, axis=0)
    return jnp.pad(buf, ((0, rows_total - buf.shape[0]), (0, 0)))


def _unpack_small(buf, shapes):
    out, r = [], 0
    for shp in shapes:
        n = 1
        for s in shp:
            n *= s
        nr = (n + PACK_W - 1) // PACK_W
        out.append(buf[..., r:r + nr, :].reshape(buf.shape[:-2] + (nr * PACK_W,))[..., :n].reshape(buf.shape[:-2] + shp))
        r += nr
    return out


def _unshard_last(g4):
    nd = g4.ndim
    perm = tuple(range(1, nd - 1)) + (0, nd - 1)
    t = g4.transpose(perm)
    return t.reshape(t.shape[:-2] + (-1,))


def _shard_last(full, s):
    n = full.shape[-1] // N_CHIPS
    return lax.dynamic_slice_in_dim(full, s * n, n, axis=full.ndim - 1)


WEIGHT_NAMES = ("w_ada", "b_ada", "ln_g", "ln_b", "gla_w_in", "gla_w_gk2", "gla_b_gk", "gla_g_norm", "gla_w_out",
                "att_w_in", "att_b_in", "att_rel_bias", "att_w_out", "ff_w1", "ff_w2")


def kernel(x, c, w_ada, b_ada, ln_g, ln_b, gla_w_in, gla_w_gk2, gla_b_gk, gla_g_norm, gla_w_out, att_w_in, att_b_in, att_rel_bias, att_w_out, ff_w1, ff_w2, loss_target, m_w_ada, m_b_ada, m_ln_g, m_ln_b, m_gla_w_in, m_gla_w_gk2, m_gla_b_gk, m_gla_g_norm, m_gla_w_out, m_att_w_in, m_att_b_in, m_att_rel_bias, m_att_w_out, m_ff_w1, m_ff_w2, v_w_ada, v_b_ada, v_ln_g, v_ln_b, v_gla_w_in, v_gla_w_gk2, v_gla_b_gk, v_gla_g_norm, v_gla_w_out, v_att_w_in, v_att_b_in, v_att_rel_bias, v_att_w_out, v_ff_w1, v_ff_w2):
    weights = dict(w_ada=w_ada, b_ada=b_ada, ln_g=ln_g, ln_b=ln_b, gla_w_in=gla_w_in, gla_w_gk2=gla_w_gk2,
                   gla_b_gk=gla_b_gk, gla_g_norm=gla_g_norm, gla_w_out=gla_w_out, att_w_in=att_w_in,
                   att_b_in=att_b_in, att_rel_bias=att_rel_bias, att_w_out=att_w_out, ff_w1=ff_w1, ff_w2=ff_w2)
    mom1 = dict(w_ada=m_w_ada, b_ada=m_b_ada, ln_g=m_ln_g, ln_b=m_ln_b, gla_w_in=m_gla_w_in, gla_w_gk2=m_gla_w_gk2,
                gla_b_gk=m_gla_b_gk, gla_g_norm=m_gla_g_norm, gla_w_out=m_gla_w_out, att_w_in=m_att_w_in,
                att_b_in=m_att_b_in, att_rel_bias=m_att_rel_bias, att_w_out=m_att_w_out, ff_w1=m_ff_w1, ff_w2=m_ff_w2)
    mom2 = dict(w_ada=v_w_ada, b_ada=v_b_ada, ln_g=v_ln_g, ln_b=v_ln_b, gla_w_in=v_gla_w_in, gla_w_gk2=v_gla_w_gk2,
                gla_b_gk=v_gla_b_gk, gla_g_norm=v_gla_g_norm, gla_w_out=v_gla_w_out, att_w_in=v_att_w_in,
                att_b_in=v_att_b_in, att_rel_bias=v_att_rel_bias, att_w_out=v_att_w_out, ff_w1=v_ff_w1, ff_w2=v_ff_w2)

    ax, ay, ac = lax.axis_index("x"), lax.axis_index("y"), lax.axis_index("c")
    chip = 2 * ax + ay
    dev = 2 * chip + ac
    S = x.shape[1]
    x2 = x.reshape(S, D_MODEL)
    t2 = loss_target.reshape(S, D_MODEL)

    comm = Comm(weights, ac, chip)

    small_rows = 16
    spack = _pack_small([c] + [weights[n] for n, _ in SMALL_SHARDED], small_rows)
    sg = all_gather8(spack, "gather_small").reshape(N_DEV, small_rows, PACK_W)
    parts = _unpack_small(sg, [(1, D_MODEL)] + [shp for _, shp in SMALL_SHARDED])
    c_all = parts[0].reshape(N_DEV, D_MODEL)
    small = {n: _unshard_last(p[0::2]) for (n, _), p in zip(SMALL_SHARDED, parts[1:])}
    small["gla_b_gk"] = gla_b_gk
    small["att_rel_bias"] = att_rel_bias

    c_act = silu_rows(jnp.pad(c_all, ((0, 128 - N_DEV), (0, 0))), "silu_c")
    mods_part = jnp.stack([mm_plain(c_act, w_ada, l, f"ada_fwd_{l}", tm=128)[:N_DEV] for l in range(DEPTH)], axis=1)
    mods_part = mods_part.reshape(N_DEV, DEPTH * 6 * D_MODEL // N_CHIPS)
    mg = all_gather8(mods_part, "gather_mods").reshape(N_CHIPS, 2, N_DEV, DEPTH, 6 * D_MODEL // N_CHIPS)
    mods_mine = lax.dynamic_index_in_dim(mg[:, 0], dev, axis=1, keepdims=False)
    mods = mods_mine.transpose(1, 0, 2).reshape(DEPTH, 6 * D_MODEL) + b_ada
    mods = mods.reshape(DEPTH, 6, D_MODEL)

    sq, grad_x, dmods, g_small = local_step(x2, t2, mods, comm, small)
    loss = lax.psum(0.5 * sq[0, 0] / D_MODEL, ("x", "y", "c"))

    g_shard = comm.reduce_end()

    dm_flat = dmods.reshape(DEPTH, 6 * D_MODEL)
    g_rows = 80
    gpack = _pack_small([dm_flat] + [g_small[n] for n in SMALL_GRAD_ORDER], g_rows)
    gg = all_gather8(gpack, "gather_small_grads").reshape(N_DEV, g_rows, PACK_W)
    gsum = sum_over_devices(gg, "sum_small_grads")
    sums = _unpack_small(gsum, [(DEPTH, 6 * D_MODEL)] + [SMALL_FULL[n] for n in SMALL_GRAD_ORDER])
    grads = dict(b_ada=sums[0])
    for n, full_g in zip(SMALL_GRAD_ORDER, sums[1:]):
        grads[n] = full_g if n in ("gla_b_gk", "att_rel_bias") else _shard_last(full_g, chip)
    dm_all = _unpack_small(gg, [(DEPTH, 6 * D_MODEL)])[0]
    dm_cols = _shard_last(dm_all, chip).reshape(N_DEV, DEPTH * 6 * D_MODEL // N_CHIPS)
    dm_cols = jnp.pad(dm_cols, ((0, 128 - N_DEV), (0, 0))).astype(BF16)
    gwa = mm_w(c_act, dm_cols, "ada_bwd", ts=128)
    grads["w_ada"] = gwa.reshape(D_MODEL, DEPTH, 6 * D_MODEL // N_CHIPS).transpose(1, 0, 2)
    grads.update(g_shard)

    deltas, new_m, new_v = {}, {}, {}
    for n in WEIGHT_NAMES:
        deltas[n], new_m[n], new_v[n] = adamw(weights[n], grads[n], mom1[n], mom2[n], "adamw_" + n)

    return (loss, grad_x.reshape(1, S, D_MODEL), *[grads[n] for n in WEIGHT_NAMES], *[deltas[n] for n in WEIGHT_NAMES],
            *[new_m[n] for n in WEIGHT_NAMES], *[new_v[n] for n in WEIGHT_NAMES])
```
